```python
import math
import jax, jax.numpy as jnp
from jax import lax
import numpy as np

D_MODEL = 2048
BATCH = 8
SEQ = 4096
DEPTH = 2

PLE_DIM = 256
D_FF = 4 * D_MODEL
NORM_EPS = 1e-6
CHUNK = 64

A_WIDTH = D_MODEL // 2
A_HEAD_DIM = 128
A_HEADS = A_WIDTH // A_HEAD_DIM

B_WIDTH = D_MODEL // 2
B_HEAD_DIM = 128
B_HEADS = B_WIDTH // B_HEAD_DIM
CONV_WIDTH = 4

S5_GROUP = 16
S5_GROUPS = D_MODEL // S5_GROUP
S5_STATE = 64
DT_MIN = 0.001
DT_MAX = 0.1

N_EVEN = (DEPTH + 1) // 2
N_ODD = DEPTH // 2

IN_SIZES = [A_WIDTH, A_WIDTH, A_WIDTH, A_WIDTH, 3 * B_WIDTH, B_WIDTH, B_HEADS, B_HEADS]
IN_COLS = sum(IN_SIZES)
IN_CUTS = [int(c) for c in np.cumsum(IN_SIZES)[:-1]]

kernel_name = "hgrn2_deltanet_s5_hybrid_trunk"


def rms_norm(x, g):
    xf = x.astype(jnp.float32)
    y = xf * lax.rsqrt(jnp.mean(xf * xf, axis=-1, keepdims=True) + NORM_EPS)
    return (y * g.astype(jnp.float32)).astype(x.dtype)


def l2_norm(t):
    return t * lax.rsqrt(jnp.sum(t * t, axis=-1, keepdims=True) + NORM_EPS)


def to_head_chunks(t, heads):
    b, s, _ = t.shape
    return t.reshape(b, s // CHUNK, CHUNK, heads, -1).transpose(0, 3, 1, 2, 4)


def from_head_chunks(t):
    b, h, n, c, d = t.shape
    return t.transpose(0, 2, 3, 1, 4).reshape(b, n * c, h * d)


def to_scalar_chunks(t):
    b, s, h = t.shape
    return t.reshape(b, s // CHUNK, CHUNK, h).transpose(0, 3, 1, 2)


def causal_depthwise_conv(x, w):
    ch = x.shape[-1]
    return lax.conv_general_dilated(
        x, w.astype(x.dtype)[:, None, :], window_strides=(1,),
        padding=((CONV_WIDTH - 1, 0),), dimension_numbers=('NWC', 'WIO', 'NWC'),
        feature_group_count=ch)


def hgrn2_recurrence(q, k, v, log_f):
    bsz, heads, _, _, dk = q.shape
    dv = v.shape[-1]
    cum = jnp.cumsum(log_f, axis=3)
    cum_end = cum[:, :, :, -1:, :]
    q_dec = q * jnp.exp(cum)
    k_dec = k * jnp.exp(cum_end - cum)
    chunk_decay = jnp.exp(cum_end[:, :, :, 0, :])
    causal = jnp.tril(jnp.ones((CHUNK, CHUNK), dtype=bool))[:, :, None]

    def chunk_step(state, inputs):
        q_c, k_c, v_c, cum_c, qd_c, kd_c, dec_c = inputs
        rel = cum_c[:, :, :, None, :] - cum_c[:, :, None, :, :]
        pair_decay = jnp.exp(jnp.where(causal, rel, -jnp.inf))
        scores = jnp.einsum('bhtd,bhsd,bhtsd->bhts', q_c, k_c, pair_decay)
        out = (jnp.einsum('bhts,bhsv->bhtv', scores, v_c)
               + jnp.einsum('bhtd,bhdv->bhtv', qd_c, state))
        state = state * dec_c[..., None] + jnp.einsum('bhsd,bhsv->bhdv', kd_c, v_c)
        return state, out

    state0 = jnp.zeros((bsz, heads, dk, dv), jnp.float32)
    xs = tuple(jnp.moveaxis(t, 2, 0) for t in (q, k, v, cum, q_dec, k_dec, chunk_decay))
    _, out = lax.scan(chunk_step, state0, xs)
    return jnp.moveaxis(out, 0, 2)


def gated_delta_rule(q, k, v, log_a, beta):
    bsz, heads, _, _, dk = q.shape
    dv = v.shape[-1]
    cum = jnp.cumsum(log_a, axis=-1)
    causal = jnp.tril(jnp.ones((CHUNK, CHUNK), dtype=bool))
    strict = jnp.tril(jnp.ones((CHUNK, CHUNK), dtype=bool), k=-1)
    decay = jnp.exp(jnp.where(causal, cum[..., :, None] - cum[..., None, :], -jnp.inf))
    k_beta = k * beta[..., None]
    lower = jnp.where(strict, jnp.einsum('bhntd,bhnsd->bhnts', k_beta, k) * decay, 0.0)
    rhs = jnp.concatenate([v * beta[..., None], k_beta * jnp.exp(cum)[..., None]], axis=-1)
    sol = lax.linalg.triangular_solve(lower + jnp.eye(CHUNK, dtype=lower.dtype), rhs,
                                      left_side=True, lower=True, unit_diagonal=True)
    u, w = sol[..., :dv], sol[..., dv:]
    intra = jnp.einsum('bhntd,bhnsd->bhnts', q, k) * decay
    q_dec = q * jnp.exp(cum)[..., None]
    k_dec = k * jnp.exp(cum[..., -1:] - cum)[..., None]
    chunk_decay = jnp.exp(cum[..., -1])

    def chunk_step(state, inputs):
        qd_c, kd_c, u_c, w_c, a_c, dec_c = inputs
        v_new = u_c - jnp.einsum('bhcd,bhdv->bhcv', w_c, state)
        out = (jnp.einsum('bhcd,bhdv->bhcv', qd_c, state)
               + jnp.einsum('bhts,bhsv->bhtv', a_c, v_new))
        state = state * dec_c[..., None, None] + jnp.einsum('bhsd,bhsv->bhdv', kd_c, v_new)
        return state, out

    state0 = jnp.zeros((bsz, heads, dk, dv), jnp.float32)
    xs = tuple(jnp.moveaxis(t, 2, 0) for t in (q_dec, k_dec, u, w, intra, chunk_decay))
    _, out = lax.scan(chunk_step, state0, xs)
    return jnp.moveaxis(out, 0, 2)


def hgrn2_deltanet_mixer(hn, w_in, w_out, lb, g_norm_a, conv_w, a_log, dt_bias, g_norm_b):
    f32 = jnp.float32
    proj = hn @ w_in
    q_a, f_a, i_a, g_a, qkv_b, z_b, a_b, b_b = jnp.split(proj, IN_CUTS, axis=-1)

    forget = lb + (1.0 - lb) * jax.nn.sigmoid(f_a.astype(f32))
    o_a = hgrn2_recurrence(to_head_chunks(q_a.astype(f32), A_HEADS),
                           to_head_chunks(1.0 - forget, A_HEADS),
                           to_head_chunks(i_a.astype(f32), A_HEADS),
                           to_head_chunks(jnp.log(forget), A_HEADS))
    o_a = from_head_chunks(rms_norm(o_a, g_norm_a)) * jax.nn.silu(g_a.astype(f32))

    qkv = jax.nn.silu(causal_depthwise_conv(qkv_b, conv_w)).astype(f32)
    q_b, k_b, v_b = jnp.split(qkv, 3, axis=-1)
    q_b = l2_norm(to_head_chunks(q_b, B_HEADS)) * (B_HEAD_DIM ** -0.5)
    k_b = l2_norm(to_head_chunks(k_b, B_HEADS))
    v_b = to_head_chunks(v_b, B_HEADS)
    beta = jax.nn.sigmoid(to_scalar_chunks(b_b.astype(f32)))
    log_a = -jnp.exp(a_log.astype(f32)) * jax.nn.softplus(a_b.astype(f32) + dt_bias.astype(f32))
    o_b = gated_delta_rule(q_b, k_b, v_b, to_scalar_chunks(log_a), beta)
    o_b = from_head_chunks(rms_norm(o_b, g_norm_b)) * jax.nn.silu(z_b.astype(f32))

    merged = jnp.concatenate([o_a, o_b], axis=-1).astype(hn.dtype)
    return merged @ w_out


def ssm_combine(e1, e2):
    a1, b1 = e1
    a2, b2 = e2
    return a1 * a2, a2 * b1 + b2


def s5_mixer(hn, a_re, a_im, b_re, b_im, c_re, c_im, d_skip, log_dt, w_glu, b_glu, w_out):
    f32 = jnp.float32
    bsz, s, _ = hn.shape
    u = hn.astype(f32)
    u_g = u.reshape(bsz, s, S5_GROUPS, S5_GROUP).astype(jnp.complex64)
    lam = lax.complex(a_re.astype(f32), a_im.astype(f32))
    step = jnp.exp(log_dt.astype(f32))[:, None]
    lam_bar = jnp.exp(lam * step)
    b_mat = lax.complex(b_re.astype(f32), b_im.astype(f32))
    b_bar = ((lam_bar - 1.0) / lam)[..., None] * b_mat
    c_mat = lax.complex(c_re.astype(f32), c_im.astype(f32))
    bu = jnp.einsum('gpc,bsgc->bsgp', b_bar, u_g)
    a_elems = jnp.broadcast_to(lam_bar, (1, s) + lam_bar.shape)
    _, states = lax.associative_scan(ssm_combine, (a_elems, bu), axis=1)
    y = jnp.einsum('gcp,bsgp->bsgc', c_mat, states).real.reshape(bsz, s, D_MODEL)
    y = y + d_skip.astype(f32) * u
    act = jax.nn.gelu(y)
    glu = act * jax.nn.sigmoid(act @ w_glu.astype(f32) + b_glu.astype(f32))
    return glu.astype(hn.dtype) @ w_out


def _fwd_setup_inputs(seed: int = 0) -> dict:
    key = jax.random.key(seed)
    ks = iter(jax.random.split(key, 40))
    f32 = jnp.float32

    def nrm(shape, scale):
        return scale * jax.random.normal(next(ks), shape, f32)

    def unif(shape, lo, hi):
        return jax.random.uniform(next(ks), shape, f32, minval=lo, maxval=hi)

    x = nrm((BATCH, SEQ, D_MODEL), 1.0)
    p = nrm((DEPTH, BATCH, SEQ, PLE_DIM), 1.0)
    norm_mix = 1.0 + nrm((DEPTH, D_MODEL), 0.02)
    norm_mlp = 1.0 + nrm((DEPTH, D_MODEL), 0.02)
    norm_ple = 1.0 + nrm((DEPTH, D_MODEL), 0.02)
    w_in_e = nrm((N_EVEN, D_MODEL, IN_COLS), D_MODEL ** -0.5)
    w_out_e = nrm((N_EVEN, D_MODEL, D_MODEL), D_MODEL ** -0.5)
    hgrn_lb = nrm((DEPTH + 1, A_WIDTH), 0.1)
    g_norm_a = 1.0 + nrm((N_EVEN, A_HEAD_DIM), 0.02)
    conv_w = nrm((N_EVEN, CONV_WIDTH, 3 * B_WIDTH), CONV_WIDTH ** -0.5)
    a_log = jnp.log(unif((N_EVEN, B_HEADS), 1.0, 16.0))
    dt = jnp.exp(unif((N_EVEN, B_HEADS), math.log(DT_MIN), math.log(DT_MAX)))
    dt_bias = dt + jnp.log(-jnp.expm1(-dt))
    g_norm_b = 1.0 + nrm((N_EVEN, B_HEAD_DIM), 0.02)
    s5_a_re = -0.5 + nrm((N_ODD, S5_GROUPS, S5_STATE), 0.01)
    s5_a_im = (math.pi * jnp.arange(S5_STATE, dtype=f32))[None, None, :] + nrm((N_ODD, S5_GROUPS, S5_STATE), 0.01)
    s5_b_re = nrm((N_ODD, S5_GROUPS, S5_STATE, S5_GROUP), (2 * S5_GROUP) ** -0.5)
    s5_b_im = nrm((N_ODD, S5_GROUPS, S5_STATE, S5_GROUP), (2 * S5_GROUP) ** -0.5)
    s5_c_re = nrm((N_ODD, S5_GROUPS, S5_GROUP, S5_STATE), S5_STATE ** -0.5)
    s5_c_im = nrm((N_ODD, S5_GROUPS, S5_GROUP, S5_STATE), S5_STATE ** -0.5)
    s5_d = nrm((N_ODD, D_MODEL), 1.0)
    s5_log_dt = unif((N_ODD, S5_GROUPS), math.log(DT_MIN), math.log(DT_MAX))
    w_glu = nrm((N_ODD, D_MODEL, D_MODEL), D_MODEL ** -0.5)
    b_glu = nrm((N_ODD, D_MODEL), 0.01)
    w_out_o = nrm((N_ODD, D_MODEL, D_MODEL), D_MODEL ** -0.5)
    w_up = nrm((DEPTH, D_MODEL, D_FF), D_MODEL ** -0.5)
    w_down = nrm((DEPTH, D_FF, D_MODEL), D_FF ** -0.5)
    w_ple_gate = nrm((DEPTH, D_MODEL, D_MODEL), D_MODEL ** -0.5)
    w_ple_proj = nrm((DEPTH, PLE_DIM, D_MODEL), PLE_DIM ** -0.5)
    final_norm = 1.0 + nrm((D_MODEL,), 0.02)
    return {"x": x, "p": p, "norm_mix": norm_mix, "norm_mlp": norm_mlp, "norm_ple": norm_ple,
            "w_in_e": w_in_e, "w_out_e": w_out_e, "hgrn_lb": hgrn_lb, "g_norm_a": g_norm_a,
            "conv_w": conv_w, "a_log": a_log, "dt_bias": dt_bias, "g_norm_b": g_norm_b,
            "s5_a_re": s5_a_re, "s5_a_im": s5_a_im, "s5_b_re": s5_b_re, "s5_b_im": s5_b_im,
            "s5_c_re": s5_c_re, "s5_c_im": s5_c_im, "s5_d": s5_d, "s5_log_dt": s5_log_dt,
            "w_glu": w_glu, "b_glu": b_glu, "w_out_o": w_out_o, "w_up": w_up, "w_down": w_down,
            "w_ple_gate": w_ple_gate, "w_ple_proj": w_ple_proj, "final_norm": final_norm}


def _fwd_reference(x, p, norm_mix, norm_mlp, norm_ple, w_in_e, w_out_e, hgrn_lb, g_norm_a,
              conv_w, a_log, dt_bias, g_norm_b, s5_a_re, s5_a_im, s5_b_re, s5_b_im,
              s5_c_re, s5_c_im, s5_d, s5_log_dt, w_glu, b_glu, w_out_o, w_up, w_down,
              w_ple_gate, w_ple_proj, final_norm):
    lower_bounds = jnp.cumsum(jax.nn.softmax(hgrn_lb.astype(jnp.float32), axis=0), axis=0)
    h = x
    for i in range(DEPTH):
        j = i // 2
        hn = rms_norm(h, norm_mix[i])
        if i % 2 == 0:
            mix = hgrn2_deltanet_mixer(hn, w_in_e[j], w_out_e[j], lower_bounds[i], g_norm_a[j],
                                       conv_w[j], a_log[j], dt_bias[j], g_norm_b[j])
        else:
            mix = s5_mixer(hn, s5_a_re[j], s5_a_im[j], s5_b_re[j], s5_b_im[j], s5_c_re[j],
                           s5_c_im[j], s5_d[j], s5_log_dt[j], w_glu[j], b_glu[j], w_out_o[j])
        h = h + mix.astype(h.dtype)
        hn = rms_norm(h, norm_mlp[i])
        h = h + (jnp.square(jax.nn.relu(hn @ w_up[i])) @ w_down[i]).astype(h.dtype)
        gate = jax.nn.sigmoid(rms_norm(h, norm_ple[i]) @ w_ple_gate[i])
        h = h + (gate * (p[i] @ w_ple_proj[i])).astype(h.dtype)
    return rms_norm(h, final_norm)


import jax as _jax
import jax.numpy as _jnp

TWIN_FORMAT = 'train_step'
FWD_PARAMS = ['x', 'p', 'norm_mix', 'norm_mlp', 'norm_ple', 'w_in_e', 'w_out_e', 'hgrn_lb', 'g_norm_a', 'conv_w', 'a_log', 'dt_bias', 'g_norm_b', 's5_a_re', 's5_a_im', 's5_b_re', 's5_b_im', 's5_c_re', 's5_c_im', 's5_d', 's5_log_dt', 'w_glu', 'b_glu', 'w_out_o', 'w_up', 'w_down', 'w_ple_gate', 'w_ple_proj', 'final_norm']
TWIN_WEIGHTS = ['norm_mix', 'norm_mlp', 'norm_ple', 'w_in_e', 'w_out_e', 'hgrn_lb', 'g_norm_a', 'conv_w', 'a_log', 'dt_bias', 'g_norm_b', 's5_a_re', 's5_a_im', 's5_b_re', 's5_b_im', 's5_c_re', 's5_c_im', 's5_d', 's5_log_dt', 'w_glu', 'b_glu', 'w_out_o', 'w_up', 'w_down', 'w_ple_gate', 'w_ple_proj', 'final_norm']
TWIN_DIFF_INPUT = 'x'
TWIN_INPUTS = ['x', 'p', 'norm_mix', 'norm_mlp', 'norm_ple', 'w_in_e', 'w_out_e', 'hgrn_lb', 'g_norm_a', 'conv_w', 'a_log', 'dt_bias', 'g_norm_b', 's5_a_re', 's5_a_im', 's5_b_re', 's5_b_im', 's5_c_re', 's5_c_im', 's5_d', 's5_log_dt', 'w_glu', 'b_glu', 'w_out_o', 'w_up', 'w_down', 'w_ple_gate', 'w_ple_proj', 'final_norm', 'loss_target', 'm_norm_mix', 'm_norm_mlp', 'm_norm_ple', 'm_w_in_e', 'm_w_out_e', 'm_hgrn_lb', 'm_g_norm_a', 'm_conv_w', 'm_a_log', 'm_dt_bias', 'm_g_norm_b', 'm_s5_a_re', 'm_s5_a_im', 'm_s5_b_re', 'm_s5_b_im', 'm_s5_c_re', 'm_s5_c_im', 'm_s5_d', 'm_s5_log_dt', 'm_w_glu', 'm_b_glu', 'm_w_out_o', 'm_w_up', 'm_w_down', 'm_w_ple_gate', 'm_w_ple_proj', 'm_final_norm', 'v_norm_mix', 'v_norm_mlp', 'v_norm_ple', 'v_w_in_e', 'v_w_out_e', 'v_hgrn_lb', 'v_g_norm_a', 'v_conv_w', 'v_a_log', 'v_dt_bias', 'v_g_norm_b', 'v_s5_a_re', 'v_s5_a_im', 'v_s5_b_re', 'v_s5_b_im', 'v_s5_c_re', 'v_s5_c_im', 'v_s5_d', 'v_s5_log_dt', 'v_w_glu', 'v_b_glu', 'v_w_out_o', 'v_w_up', 'v_w_down', 'v_w_ple_gate', 'v_w_ple_proj', 'v_final_norm']
TWIN_OUTPUTS = ['loss', 'grad_x', 'grad_norm_mix', 'grad_norm_mlp', 'grad_norm_ple', 'grad_w_in_e', 'grad_w_out_e', 'grad_hgrn_lb', 'grad_g_norm_a', 'grad_conv_w', 'grad_a_log', 'grad_dt_bias', 'grad_g_norm_b', 'grad_s5_a_re', 'grad_s5_a_im', 'grad_s5_b_re', 'grad_s5_b_im', 'grad_s5_c_re', 'grad_s5_c_im', 'grad_s5_d', 'grad_s5_log_dt', 'grad_w_glu', 'grad_b_glu', 'grad_w_out_o', 'grad_w_up', 'grad_w_down', 'grad_w_ple_gate', 'grad_w_ple_proj', 'grad_final_norm', 'delta_norm_mix', 'delta_norm_mlp', 'delta_norm_ple', 'delta_w_in_e', 'delta_w_out_e', 'delta_hgrn_lb', 'delta_g_norm_a', 'delta_conv_w', 'delta_a_log', 'delta_dt_bias', 'delta_g_norm_b', 'delta_s5_a_re', 'delta_s5_a_im', 'delta_s5_b_re', 'delta_s5_b_im', 'delta_s5_c_re', 'delta_s5_c_im', 'delta_s5_d', 'delta_s5_log_dt', 'delta_w_glu', 'delta_b_glu', 'delta_w_out_o', 'delta_w_up', 'delta_w_down', 'delta_w_ple_gate', 'delta_w_ple_proj', 'delta_final_norm', 'new_m_norm_mix', 'new_m_norm_mlp', 'new_m_norm_ple', 'new_m_w_in_e', 'new_m_w_out_e', 'new_m_hgrn_lb', 'new_m_g_norm_a', 'new_m_conv_w', 'new_m_a_log', 'new_m_dt_bias', 'new_m_g_norm_b', 'new_m_s5_a_re', 'new_m_s5_a_im', 'new_m_s5_b_re', 'new_m_s5_b_im', 'new_m_s5_c_re', 'new_m_s5_c_im', 'new_m_s5_d', 'new_m_s5_log_dt', 'new_m_w_glu', 'new_m_b_glu', 'new_m_w_out_o', 'new_m_w_up', 'new_m_w_down', 'new_m_w_ple_gate', 'new_m_w_ple_proj', 'new_m_final_norm', 'new_v_norm_mix', 'new_v_norm_mlp', 'new_v_norm_ple', 'new_v_w_in_e', 'new_v_w_out_e', 'new_v_hgrn_lb', 'new_v_g_norm_a', 'new_v_conv_w', 'new_v_a_log', 'new_v_dt_bias', 'new_v_g_norm_b', 'new_v_s5_a_re', 'new_v_s5_a_im', 'new_v_s5_b_re', 'new_v_s5_b_im', 'new_v_s5_c_re', 'new_v_s5_c_im', 'new_v_s5_d', 'new_v_s5_log_dt', 'new_v_w_glu', 'new_v_b_glu', 'new_v_w_out_o', 'new_v_w_up', 'new_v_w_down', 'new_v_w_ple_gate', 'new_v_w_ple_proj', 'new_v_final_norm']
TWIN_LEAF_KINDS = {'loss': 'loss', 'grad_x': 'grad_x', 'grad_norm_mix': 'grad_w', 'grad_norm_mlp': 'grad_w', 'grad_norm_ple': 'grad_w', 'grad_w_in_e': 'grad_w', 'grad_w_out_e': 'grad_w', 'grad_hgrn_lb': 'grad_w', 'grad_g_norm_a': 'grad_w', 'grad_conv_w': 'grad_w', 'grad_a_log': 'grad_w', 'grad_dt_bias': 'grad_w', 'grad_g_norm_b': 'grad_w', 'grad_s5_a_re': 'grad_w', 'grad_s5_a_im': 'grad_w', 'grad_s5_b_re': 'grad_w', 'grad_s5_b_im': 'grad_w', 'grad_s5_c_re': 'grad_w', 'grad_s5_c_im': 'grad_w', 'grad_s5_d': 'grad_w', 'grad_s5_log_dt': 'grad_w', 'grad_w_glu': 'grad_w', 'grad_b_glu': 'grad_w', 'grad_w_out_o': 'grad_w', 'grad_w_up': 'grad_w', 'grad_w_down': 'grad_w', 'grad_w_ple_gate': 'grad_w', 'grad_w_ple_proj': 'grad_w', 'grad_final_norm': 'grad_w', 'delta_norm_mix': 'delta_w', 'delta_norm_mlp': 'delta_w', 'delta_norm_ple': 'delta_w', 'delta_w_in_e': 'delta_w', 'delta_w_out_e': 'delta_w', 'delta_hgrn_lb': 'delta_w', 'delta_g_norm_a': 'delta_w', 'delta_conv_w': 'delta_w', 'delta_a_log': 'delta_w', 'delta_dt_bias': 'delta_w', 'delta_g_norm_b': 'delta_w', 'delta_s5_a_re': 'delta_w', 'delta_s5_a_im': 'delta_w', 'delta_s5_b_re': 'delta_w', 'delta_s5_b_im': 'delta_w', 'delta_s5_c_re': 'delta_w', 'delta_s5_c_im': 'delta_w', 'delta_s5_d': 'delta_w', 'delta_s5_log_dt': 'delta_w', 'delta_w_glu': 'delta_w', 'delta_b_glu': 'delta_w', 'delta_w_out_o': 'delta_w', 'delta_w_up': 'delta_w', 'delta_w_down': 'delta_w', 'delta_w_ple_gate': 'delta_w', 'delta_w_ple_proj': 'delta_w', 'delta_final_norm': 'delta_w', 'new_m_norm_mix': 'new_m', 'new_m_norm_mlp': 'new_m', 'new_m_norm_ple': 'new_m', 'new_m_w_in_e': 'new_m', 'new_m_w_out_e': 'new_m', 'new_m_hgrn_lb': 'new_m', 'new_m_g_norm_a': 'new_m', 'new_m_conv_w': 'new_m', 'new_m_a_log': 'new_m', 'new_m_dt_bias': 'new_m', 'new_m_g_norm_b': 'new_m', 'new_m_s5_a_re': 'new_m', 'new_m_s5_a_im': 'new_m', 'new_m_s5_b_re': 'new_m', 'new_m_s5_b_im': 'new_m', 'new_m_s5_c_re': 'new_m', 'new_m_s5_c_im': 'new_m', 'new_m_s5_d': 'new_m', 'new_m_s5_log_dt': 'new_m', 'new_m_w_glu': 'new_m', 'new_m_b_glu': 'new_m', 'new_m_w_out_o': 'new_m', 'new_m_w_up': 'new_m', 'new_m_w_down': 'new_m', 'new_m_w_ple_gate': 'new_m', 'new_m_w_ple_proj': 'new_m', 'new_m_final_norm': 'new_m', 'new_v_norm_mix': 'new_v', 'new_v_norm_mlp': 'new_v', 'new_v_norm_ple': 'new_v', 'new_v_w_in_e': 'new_v', 'new_v_w_out_e': 'new_v', 'new_v_hgrn_lb': 'new_v', 'new_v_g_norm_a': 'new_v', 'new_v_conv_w': 'new_v', 'new_v_a_log': 'new_v', 'new_v_dt_bias': 'new_v', 'new_v_g_norm_b': 'new_v', 'new_v_s5_a_re': 'new_v', 'new_v_s5_a_im': 'new_v', 'new_v_s5_b_re': 'new_v', 'new_v_s5_b_im': 'new_v', 'new_v_s5_c_re': 'new_v', 'new_v_s5_c_im': 'new_v', 'new_v_s5_d': 'new_v', 'new_v_s5_log_dt': 'new_v', 'new_v_w_glu': 'new_v', 'new_v_b_glu': 'new_v', 'new_v_w_out_o': 'new_v', 'new_v_w_up': 'new_v', 'new_v_w_down': 'new_v', 'new_v_w_ple_gate': 'new_v', 'new_v_w_ple_proj': 'new_v', 'new_v_final_norm': 'new_v'}


def _forward(args):
    return _fwd_reference(*[args[k] for k in FWD_PARAMS])


def _output_shape():
    def fwd():
        inp = _fwd_setup_inputs(0)
        return _fwd_reference(*[inp[k] for k in FWD_PARAMS])
    out = _jax.eval_shape(fwd)
    return out.shape, out.dtype

N_MICROBATCH = 1
ADAM_LR = 0.001
ADAM_B1 = 0.9
ADAM_B2 = 0.999
ADAM_EPS = 1e-08
ADAM_WD = 0.01
ADAM_STEP = 10
PER_EXAMPLE_BATCH_AXIS = {'x': 0, 'p': 1, 'loss_target': 0}
SHARED_INPUTS = []
_WEIGHT_DTYPES = {'norm_mix': _jnp.float32, 'norm_mlp': _jnp.float32, 'norm_ple': _jnp.float32, 'w_in_e': _jnp.float32, 'w_out_e': _jnp.float32, 'hgrn_lb': _jnp.float32, 'g_norm_a': _jnp.float32, 'conv_w': _jnp.float32, 'a_log': _jnp.float32, 'dt_bias': _jnp.float32, 'g_norm_b': _jnp.float32, 's5_a_re': _jnp.float32, 's5_a_im': _jnp.float32, 's5_b_re': _jnp.float32, 's5_b_im': _jnp.float32, 's5_c_re': _jnp.float32, 's5_c_im': _jnp.float32, 's5_d': _jnp.float32, 's5_log_dt': _jnp.float32, 'w_glu': _jnp.float32, 'b_glu': _jnp.float32, 'w_out_o': _jnp.float32, 'w_up': _jnp.float32, 'w_down': _jnp.float32, 'w_ple_gate': _jnp.float32, 'w_ple_proj': _jnp.float32, 'final_norm': _jnp.float32}
MOMENT_SCALE = {'norm_mix': 7.225974e-02, 'norm_mlp': 6.598692e-02, 'norm_ple': 1.024725e-02, 'w_in_e': 4.754690e-02, 'w_out_e': 4.875536e-02, 'hgrn_lb': 1.600731e-02, 'g_norm_a': 1.388512e-01, 'conv_w': 3.680209e-02, 'a_log': 1.993511e-01, 'dt_bias': 1.947035e-01, 'g_norm_b': 1.373627e-01, 's5_a_re': 2.545720e-03, 's5_a_im': 2.130561e-03, 's5_b_re': 1.337416e-03, 's5_b_im': 1.176142e-03, 's5_c_re': 1.859512e-03, 's5_c_im': 1.670644e-03, 's5_d': 2.425473e-02, 's5_log_dt': 6.195389e-01, 'w_glu': 5.399395e-03, 'b_glu': 9.482404e-03, 'w_out_o': 2.206464e-02, 'w_up': 3.317505e-02, 'w_down': 6.852579e-02, 'w_ple_gate': 1.011912e-02, 'w_ple_proj': 2.488445e-02, 'final_norm': 1.621255e+01}


def _to_microbatches(a, axis):
    t = _jnp.moveaxis(a, axis, 0)
    t = t.reshape((N_MICROBATCH, t.shape[0] // N_MICROBATCH) + t.shape[1:])
    return _jnp.moveaxis(t, 1, axis + 1)


def setup_inputs(seed: int = 0) -> dict:
    inp = _fwd_setup_inputs(seed)
    key = _jax.random.fold_in(_jax.random.key(seed), 7919)
    shape, _ = _output_shape()
    out = dict(inp)
    out["loss_target"] = _jax.random.normal(_jax.random.fold_in(key, 0), shape, _jnp.float32)
    for i, name in enumerate(TWIN_WEIGHTS):
        w = inp[name].astype(_jnp.float32)
        if MOMENT_SCALE is None:
            s = _jnp.sqrt(_jnp.mean(_jnp.square(w)) + 1e-30)
        else:
            s = MOMENT_SCALE[name]
        km, kv = _jax.random.split(_jax.random.fold_in(key, i + 1))
        out[name] = w
        out["m_" + name] = s * _jax.random.normal(km, w.shape, _jnp.float32)
        out["v_" + name] = (s * s) * _jax.random.uniform(kv, w.shape, _jnp.float32, 0.5, 1.5)
    if N_MICROBATCH > 1:
        for name, axis in PER_EXAMPLE_BATCH_AXIS.items():
            out[name] = _to_microbatches(out[name], axis)
    return {'x': out['x'], 'p': out['p'], 'norm_mix': out['norm_mix'], 'norm_mlp': out['norm_mlp'], 'norm_ple': out['norm_ple'], 'w_in_e': out['w_in_e'], 'w_out_e': out['w_out_e'], 'hgrn_lb': out['hgrn_lb'], 'g_norm_a': out['g_norm_a'], 'conv_w': out['conv_w'], 'a_log': out['a_log'], 'dt_bias': out['dt_bias'], 'g_norm_b': out['g_norm_b'], 's5_a_re': out['s5_a_re'], 's5_a_im': out['s5_a_im'], 's5_b_re': out['s5_b_re'], 's5_b_im': out['s5_b_im'], 's5_c_re': out['s5_c_re'], 's5_c_im': out['s5_c_im'], 's5_d': out['s5_d'], 's5_log_dt': out['s5_log_dt'], 'w_glu': out['w_glu'], 'b_glu': out['b_glu'], 'w_out_o': out['w_out_o'], 'w_up': out['w_up'], 'w_down': out['w_down'], 'w_ple_gate': out['w_ple_gate'], 'w_ple_proj': out['w_ple_proj'], 'final_norm': out['final_norm'], 'loss_target': out['loss_target'], 'm_norm_mix': out['m_norm_mix'], 'm_norm_mlp': out['m_norm_mlp'], 'm_norm_ple': out['m_norm_ple'], 'm_w_in_e': out['m_w_in_e'], 'm_w_out_e': out['m_w_out_e'], 'm_hgrn_lb': out['m_hgrn_lb'], 'm_g_norm_a': out['m_g_norm_a'], 'm_conv_w': out['m_conv_w'], 'm_a_log': out['m_a_log'], 'm_dt_bias': out['m_dt_bias'], 'm_g_norm_b': out['m_g_norm_b'], 'm_s5_a_re': out['m_s5_a_re'], 'm_s5_a_im': out['m_s5_a_im'], 'm_s5_b_re': out['m_s5_b_re'], 'm_s5_b_im': out['m_s5_b_im'], 'm_s5_c_re': out['m_s5_c_re'], 'm_s5_c_im': out['m_s5_c_im'], 'm_s5_d': out['m_s5_d'], 'm_s5_log_dt': out['m_s5_log_dt'], 'm_w_glu': out['m_w_glu'], 'm_b_glu': out['m_b_glu'], 'm_w_out_o': out['m_w_out_o'], 'm_w_up': out['m_w_up'], 'm_w_down': out['m_w_down'], 'm_w_ple_gate': out['m_w_ple_gate'], 'm_w_ple_proj': out['m_w_ple_proj'], 'm_final_norm': out['m_final_norm'], 'v_norm_mix': out['v_norm_mix'], 'v_norm_mlp': out['v_norm_mlp'], 'v_norm_ple': out['v_norm_ple'], 'v_w_in_e': out['v_w_in_e'], 'v_w_out_e': out['v_w_out_e'], 'v_hgrn_lb': out['v_hgrn_lb'], 'v_g_norm_a': out['v_g_norm_a'], 'v_conv_w': out['v_conv_w'], 'v_a_log': out['v_a_log'], 'v_dt_bias': out['v_dt_bias'], 'v_g_norm_b': out['v_g_norm_b'], 'v_s5_a_re': out['v_s5_a_re'], 'v_s5_a_im': out['v_s5_a_im'], 'v_s5_b_re': out['v_s5_b_re'], 'v_s5_b_im': out['v_s5_b_im'], 'v_s5_c_re': out['v_s5_c_re'], 'v_s5_c_im': out['v_s5_c_im'], 'v_s5_d': out['v_s5_d'], 'v_s5_log_dt': out['v_s5_log_dt'], 'v_w_glu': out['v_w_glu'], 'v_b_glu': out['v_b_glu'], 'v_w_out_o': out['v_w_out_o'], 'v_w_up': out['v_w_up'], 'v_w_down': out['v_w_down'], 'v_w_ple_gate': out['v_w_ple_gate'], 'v_w_ple_proj': out['v_w_ple_proj'], 'v_final_norm': out['v_final_norm']}


def _loss(weights, diff, rest, loss_target):
    with _jax.named_scope("forward"):
        args = {**rest, TWIN_DIFF_INPUT: diff, **{k: w.astype(_WEIGHT_DTYPES[k]) for k, w in weights.items()}}
        y = _forward(args)
    with _jax.named_scope("loss_head"):
        err = _jnp.square(y.astype(_jnp.float32) - loss_target)
        return 0.5 * _jnp.sum(_jnp.mean(err, axis=-1)) if err.ndim else 0.5 * err


def _adamw(w, g, m, v):
    m = ADAM_B1 * m + (1.0 - ADAM_B1) * g
    v = ADAM_B2 * v + (1.0 - ADAM_B2) * _jnp.square(g)
    m_hat = m / (1.0 - ADAM_B1 ** ADAM_STEP)
    v_hat = v / (1.0 - ADAM_B2 ** ADAM_STEP)
    delta = -ADAM_LR * (m_hat / (_jnp.sqrt(v_hat) + ADAM_EPS) + ADAM_WD * w)
    return delta, m, v


def reference(x, p, norm_mix, norm_mlp, norm_ple, w_in_e, w_out_e, hgrn_lb, g_norm_a, conv_w, a_log, dt_bias, g_norm_b, s5_a_re, s5_a_im, s5_b_re, s5_b_im, s5_c_re, s5_c_im, s5_d, s5_log_dt, w_glu, b_glu, w_out_o, w_up, w_down, w_ple_gate, w_ple_proj, final_norm, loss_target, m_norm_mix, m_norm_mlp, m_norm_ple, m_w_in_e, m_w_out_e, m_hgrn_lb, m_g_norm_a, m_conv_w, m_a_log, m_dt_bias, m_g_norm_b, m_s5_a_re, m_s5_a_im, m_s5_b_re, m_s5_b_im, m_s5_c_re, m_s5_c_im, m_s5_d, m_s5_log_dt, m_w_glu, m_b_glu, m_w_out_o, m_w_up, m_w_down, m_w_ple_gate, m_w_ple_proj, m_final_norm, v_norm_mix, v_norm_mlp, v_norm_ple, v_w_in_e, v_w_out_e, v_hgrn_lb, v_g_norm_a, v_conv_w, v_a_log, v_dt_bias, v_g_norm_b, v_s5_a_re, v_s5_a_im, v_s5_b_re, v_s5_b_im, v_s5_c_re, v_s5_c_im, v_s5_d, v_s5_log_dt, v_w_glu, v_b_glu, v_w_out_o, v_w_up, v_w_down, v_w_ple_gate, v_w_ple_proj, v_final_norm):
    given = dict(x=x, p=p, norm_mix=norm_mix, norm_mlp=norm_mlp, norm_ple=norm_ple, w_in_e=w_in_e, w_out_e=w_out_e, hgrn_lb=hgrn_lb, g_norm_a=g_norm_a, conv_w=conv_w, a_log=a_log, dt_bias=dt_bias, g_norm_b=g_norm_b, s5_a_re=s5_a_re, s5_a_im=s5_a_im, s5_b_re=s5_b_re, s5_b_im=s5_b_im, s5_c_re=s5_c_re, s5_c_im=s5_c_im, s5_d=s5_d, s5_log_dt=s5_log_dt, w_glu=w_glu, b_glu=b_glu, w_out_o=w_out_o, w_up=w_up, w_down=w_down, w_ple_gate=w_ple_gate, w_ple_proj=w_ple_proj, final_norm=final_norm, loss_target=loss_target, m_norm_mix=m_norm_mix, m_norm_mlp=m_norm_mlp, m_norm_ple=m_norm_ple, m_w_in_e=m_w_in_e, m_w_out_e=m_w_out_e, m_hgrn_lb=m_hgrn_lb, m_g_norm_a=m_g_norm_a, m_conv_w=m_conv_w, m_a_log=m_a_log, m_dt_bias=m_dt_bias, m_g_norm_b=m_g_norm_b, m_s5_a_re=m_s5_a_re, m_s5_a_im=m_s5_a_im, m_s5_b_re=m_s5_b_re, m_s5_b_im=m_s5_b_im, m_s5_c_re=m_s5_c_re, m_s5_c_im=m_s5_c_im, m_s5_d=m_s5_d, m_s5_log_dt=m_s5_log_dt, m_w_glu=m_w_glu, m_b_glu=m_b_glu, m_w_out_o=m_w_out_o, m_w_up=m_w_up, m_w_down=m_w_down, m_w_ple_gate=m_w_ple_gate, m_w_ple_proj=m_w_ple_proj, m_final_norm=m_final_norm, v_norm_mix=v_norm_mix, v_norm_mlp=v_norm_mlp, v_norm_ple=v_norm_ple, v_w_in_e=v_w_in_e, v_w_out_e=v_w_out_e, v_hgrn_lb=v_hgrn_lb, v_g_norm_a=v_g_norm_a, v_conv_w=v_conv_w, v_a_log=v_a_log, v_dt_bias=v_dt_bias, v_g_norm_b=v_g_norm_b, v_s5_a_re=v_s5_a_re, v_s5_a_im=v_s5_a_im, v_s5_b_re=v_s5_b_re, v_s5_b_im=v_s5_b_im, v_s5_c_re=v_s5_c_re, v_s5_c_im=v_s5_c_im, v_s5_d=v_s5_d, v_s5_log_dt=v_s5_log_dt, v_w_glu=v_w_glu, v_b_glu=v_b_glu, v_w_out_o=v_w_out_o, v_w_up=v_w_up, v_w_down=v_w_down, v_w_ple_gate=v_w_ple_gate, v_w_ple_proj=v_w_ple_proj, v_final_norm=v_final_norm)
    weights = {n: given[n] for n in TWIN_WEIGHTS}
    shared = {n: given[n] for n in SHARED_INPUTS}
    per_example = {n: given[n] for n in ['x', 'p']}
    grad_fn = _jax.value_and_grad(_loss, argnums=(0, 1))

    def one_microbatch(ex, loss_target):
        ex = dict(ex)
        diff = ex.pop(TWIN_DIFF_INPUT)
        return grad_fn(weights, diff, {**shared, **ex}, loss_target)

    if N_MICROBATCH == 1:
        loss, (grad_w, grad_x) = one_microbatch(per_example, given["loss_target"])
    else:
        def body(carry, xs):
            loss_sum, grad_sum = carry
            l_k, (gw_k, gx_k) = one_microbatch(xs[0], xs[1])
            with _jax.named_scope("update"):
                return (loss_sum + l_k, _jax.tree.map(_jnp.add, grad_sum, gw_k)), gx_k

        init = (_jnp.zeros((), _jnp.float32), _jax.tree.map(_jnp.zeros_like, weights))
        (loss, grad_w), grad_x = _jax.lax.scan(body, init, (per_example, given["loss_target"]))
    with _jax.named_scope("update"):
        delta_w, new_m, new_v = {}, {}, {}
        for n in TWIN_WEIGHTS:
            delta_w[n], new_m[n], new_v[n] = _adamw(weights[n], grad_w[n], given["m_" + n], given["v_" + n])
    return (loss, grad_x, *[grad_w[n] for n in TWIN_WEIGHTS], *[delta_w[n] for n in TWIN_WEIGHTS],
            *[new_m[n] for n in TWIN_WEIGHTS], *[new_v[n] for n in TWIN_WEIGHTS])
```

```python
import functools
import math

import numpy as np
import jax
import jax.numpy as jnp
from jax import lax
from jax.experimental import pallas as pl
from jax.experimental.pallas import tpu as pltpu

F32 = jnp.float32
BF16 = jnp.bfloat16
MM_DTYPE = BF16
HI = lax.Precision.HIGHEST
MESH = pl.DeviceIdType.MESH

NORM_EPS = 1e-6
CHUNK = 64
HEAD = 128
CONV_WIDTH = 4
S5_GROUP = 16
S5_STATE = 64
S5_GB = 8
S5_HALF = S5_GB * S5_STATE
N_DEV = 8
ADAM_LR, ADAM_B1, ADAM_B2, ADAM_EPS, ADAM_WD, ADAM_STEP = 0.001, 0.9, 0.999, 1e-08, 0.01, 10
VMEM_LIMIT = 56 * 1024 * 1024

NN = (((1,), (0,)), ((), ()))
NT = (((1,), (1,)), ((), ()))
TN = (((0,), (0,)), ((), ()))


def _dot(a, b, dn=NN):
    return lax.dot_general(a, b, dn, precision=HI, preferred_element_type=F32)


def _params(n_axes):
    return pltpu.CompilerParams(dimension_semantics=("arbitrary",) * n_axes, vmem_limit_bytes=VMEM_LIMIT)


def _full_spec(a):
    nd = a.ndim
    return pl.BlockSpec(a.shape, lambda *_: (0,) * nd)


def _mm(a, b, *, name, ta=False, tb=False, extras=(), epilogue=None, out_dtypes=(F32,), tm=512, tn=1024, tk=512):
    m = a.shape[1] if ta else a.shape[0]
    k = a.shape[0] if ta else a.shape[1]
    n = b.shape[0] if tb else b.shape[1]
    assert k == (b.shape[1] if tb else b.shape[0]), (name, a.shape, b.shape)
    tm, tn, tk = min(tm, m), min(tn, n), min(tk, k)
    assert m % tm == 0 and n % tn == 0 and k % tk == 0, (name, m, n, k)
    nk = k // tk
    n_ex, n_out = len(extras), len(out_dtypes)
    dn = (((0 if ta else 1,), (1 if tb else 0,)), ((), ()))

    def body(a_ref, b_ref, *rest):
        ex_refs, out_refs, acc_ref = rest[:n_ex], rest[n_ex:n_ex + n_out], rest[-1]
        kk = pl.program_id(2)

        @pl.when(kk == 0)
        def _():
            acc_ref[...] = jnp.zeros_like(acc_ref)

        acc_ref[...] += lax.dot_general(a_ref[...].astype(MM_DTYPE), b_ref[...].astype(MM_DTYPE), dn,
                                        preferred_element_type=F32)

        @pl.when(kk == nk - 1)
        def _():
            acc = acc_ref[...]
            outs = epilogue(acc, *[r[...] for r in ex_refs]) if epilogue is not None else (acc,)
            for o_ref, o in zip(out_refs, outs):
                o_ref[...] = o.astype(o_ref.dtype)

    a_spec = pl.BlockSpec((tk, tm), lambda i, j, q: (q, i)) if ta else pl.BlockSpec((tm, tk), lambda i, j, q: (i, q))
    b_spec = pl.BlockSpec((tn, tk), lambda i, j, q: (j, q)) if tb else pl.BlockSpec((tk, tn), lambda i, j, q: (q, j))
    ex_specs = []
    for e in extras:
        if e.shape[0] == 1 and m != 1:
            ex_specs.append(pl.BlockSpec((1, tn), lambda i, j, q: (0, j)))
        else:
            ex_specs.append(pl.BlockSpec((tm, tn), lambda i, j, q: (i, j)))
    outs = pl.pallas_call(
        body, name=name, grid=(m // tm, n // tn, nk),
        in_specs=[a_spec, b_spec] + ex_specs,
        out_specs=[pl.BlockSpec((tm, tn), lambda i, j, q: (i, j)) for _ in out_dtypes],
        out_shape=[jax.ShapeDtypeStruct((m, n), dt) for dt in out_dtypes],
        scratch_shapes=[pltpu.VMEM((tm, tn), F32)],
        compiler_params=_params(3),
    )(a, b, *extras)
    return outs[0] if n_out == 1 else tuple(outs)


def _rows_call(fn, rows, consts, out_dtypes, *, name, tr=256):
    s = rows[0].shape[0]
    tr = min(tr, s)
    nr, nc = len(rows), len(consts)
    widths = [o.shape[1] for o in jax.eval_shape(
        fn, *[jax.ShapeDtypeStruct((tr, r.shape[1]), F32) for r in rows],
        *[jax.ShapeDtypeStruct(c.shape, F32) for c in consts])]

    def body(*refs):
        rv = [r[...].astype(F32) for r in refs[:nr]]
        cv = [c[...] for c in refs[nr:nr + nc]]
        for o_ref, o in zip(refs[nr + nc:], fn(*rv, *cv)):
            o_ref[...] = o.astype(o_ref.dtype)

    outs = pl.pallas_call(
        body, name=name, grid=(s // tr,),
        in_specs=[pl.BlockSpec((tr, r.shape[1]), lambda i: (i, 0)) for r in rows] + [_full_spec(c) for c in consts],
        out_specs=[pl.BlockSpec((tr, w), lambda i: (i, 0)) for w in widths],
        out_shape=[jax.ShapeDtypeStruct((s, w), dt) for w, dt in zip(widths, out_dtypes)],
        compiler_params=_params(1),
    )(*rows, *consts)
    return outs[0] if len(outs) == 1 else tuple(outs)


def _rows_vjp(fn, rows, consts, cots, *, name, row_grads, adds=None, tr=256):
    adds = adds or {}
    s = rows[0].shape[0]
    tr = min(tr, s)
    nr, nc, nt = len(rows), len(consts), len(cots)
    rg = sorted(row_grads)
    ad = sorted(adds)

    def body(*refs):
        rv = [r[...].astype(F32) for r in refs[:nr]]
        cv = [c[...] for c in refs[nr:nr + nc]]
        ct = [c[...].astype(F32) for c in refs[nr + nc:nr + nc + nt]]
        av = {i: r[...].astype(F32) for i, r in zip(ad, refs[nr + nc + nt:nr + nc + nt + len(ad)])}
        out_refs = refs[nr + nc + nt + len(ad):]
        _, vjp = jax.vjp(fn, *rv, *cv)
        grads = vjp(tuple(ct))
        for o_ref, i in zip(out_refs[:len(rg)], rg):
            g = grads[i]
            if i in av:
                g = g + av[i]
            o_ref[...] = g.astype(o_ref.dtype)

        @pl.when(pl.program_id(0) == 0)
        def _():
            for o_ref in out_refs[len(rg):]:
                o_ref[...] = jnp.zeros_like(o_ref)

        for o_ref, g in zip(out_refs[len(rg):], grads[nr:]):
            o_ref[...] += g

    row_spec = lambda a: pl.BlockSpec((tr, a.shape[1]), lambda i: (i, 0))
    outs = pl.pallas_call(
        body, name=name, grid=(s // tr,),
        in_specs=[row_spec(r) for r in rows] + [_full_spec(c) for c in consts] + [row_spec(c) for c in cots]
        + [row_spec(adds[i]) for i in ad],
        out_specs=[row_spec(rows[i]) for i in rg] + [_full_spec(c) for c in consts],
        out_shape=[jax.ShapeDtypeStruct(rows[i].shape, row_grads[i]) for i in rg]
        + [jax.ShapeDtypeStruct(c.shape, F32) for c in consts],
        compiler_params=_params(1),
    )(*rows, *consts, *cots, *[adds[i] for i in ad])
    return list(outs[:len(rg)]), list(outs[len(rg):])


def _small_call(fn, ins, *, name):
    shapes = jax.eval_shape(fn, *[jax.ShapeDtypeStruct(a.shape, F32) for a in ins])

    def body(*refs):
        for o_ref, o in zip(refs[len(ins):], fn(*[r[...] for r in refs[:len(ins)]])):
            o_ref[...] = o

    return pl.pallas_call(
        body, name=name, in_specs=[_full_spec(a) for a in ins],
        out_specs=[pl.BlockSpec(o.shape, functools.partial(lambda nd, *_: (0,) * nd, len(o.shape))) for o in shapes],
        out_shape=[jax.ShapeDtypeStruct(o.shape, F32) for o in shapes], grid=(1,),
        compiler_params=_params(1),
    )(*ins)


def _small_vjp(fn, ins, cots, *, name):
    def body(*refs):
        vals = [r[...] for r in refs[:len(ins)]]
        ct = [r[...] for r in refs[len(ins):len(ins) + len(cots)]]
        _, vjp = jax.vjp(fn, *vals)
        for o_ref, g in zip(refs[len(ins) + len(cots):], vjp(tuple(ct))):
            o_ref[...] = g

    return pl.pallas_call(
        body, name=name, in_specs=[_full_spec(a) for a in ins] + [_full_spec(c) for c in cots],
        out_specs=[_full_spec(a) for a in ins],
        out_shape=[jax.ShapeDtypeStruct(a.shape, F32) for a in ins], grid=(1,),
        compiler_params=_params(1),
    )(*ins, *cots)


def _rms(x, g):
    return x * lax.rsqrt(jnp.mean(x * x, axis=-1, keepdims=True) + NORM_EPS) * g


def _rms_stage(x, g):
    return (_rms(x, g),)


def _silu(x):
    return x * jax.nn.sigmoid(x)


def _softplus(x):
    return jnp.maximum(x, 0.0) + jnp.log1p(jnp.exp(-jnp.abs(x)))


def _gelu(x):
    return jax.nn.gelu(x, approximate=True)


def _gelu_stage(y):
    return (_gelu(y),)


def _glu_stage(y, gl_raw, b):
    return (_gelu(y) * jax.nn.sigmoid(gl_raw + b),)


def _ple_stage(h, gpre, pp):
    return (h + jax.nn.sigmoid(gpre) * pp,)


def _relu2_grad_epilogue(acc, up):
    return (acc * (2.0 * jnp.maximum(up, 0.0)),)


def _lb0_stage(x0, x1, x2):
    mx = jnp.maximum(jnp.maximum(x0, x1), x2)
    e0, e1, e2 = jnp.exp(x0 - mx), jnp.exp(x1 - mx), jnp.exp(x2 - mx)
    return (e0 / (e0 + e1 + e2),)


def _s5_prep_stage(a_re, a_im, log_dt, b_re, b_im, expand):
    step = jnp.exp(log_dt)
    mag = jnp.exp(a_re * step)
    lr = mag * jnp.cos(a_im * step)
    li = mag * jnp.sin(a_im * step)
    den = a_re * a_re + a_im * a_im
    cr = ((lr - 1.0) * a_re + li * a_im) / den
    ci = (li * a_re - (lr - 1.0) * a_im) / den
    cr_e, ci_e = _dot(cr, expand), _dot(ci, expand)
    return lr, li, cr_e * b_re - ci_e * b_im, cr_e * b_im + ci_e * b_re


def _loss_call(h, g, target, *, name, tr=256):
    s, d = h.shape
    tr = min(tr, s)

    def loss_fn(hv, gv, tv):
        err = _rms(hv, gv) - tv
        return 0.5 * jnp.sum(jnp.mean(err * err, axis=-1))

    def body(h_ref, g_ref, t_ref, dh_ref, dg_ref, loss_ref):
        val, (dh, dg) = jax.value_and_grad(loss_fn, argnums=(0, 1))(h_ref[...], g_ref[...], t_ref[...])
        dh_ref[...] = dh

        @pl.when(pl.program_id(0) == 0)
        def _():
            dg_ref[...] = jnp.zeros_like(dg_ref)
            loss_ref[...] = jnp.zeros_like(loss_ref)

        dg_ref[...] += dg
        loss_ref[...] += jnp.full(loss_ref.shape, val, F32)

    row = pl.BlockSpec((tr, d), lambda i: (i, 0))
    return pl.pallas_call(
        body, name=name, grid=(s // tr,),
        in_specs=[row, _full_spec(g), row],
        out_specs=[row, _full_spec(g), pl.BlockSpec((8, 128), lambda i: (0, 0))],
        out_shape=[jax.ShapeDtypeStruct((s, d), F32), jax.ShapeDtypeStruct(g.shape, F32),
                   jax.ShapeDtypeStruct((8, 128), F32)],
        compiler_params=_params(1),
    )(h, g, target)


def _hgrn_chunk(q, fp, iv, gp, lb, gn, st_t):
    c = q.shape[0]
    row = lax.broadcasted_iota(jnp.int32, (c, c), 0)
    col = lax.broadcasted_iota(jnp.int32, (c, c), 1)
    causal = row >= col
    fg = lb + (1.0 - lb) * jax.nn.sigmoid(fp)
    k = 1.0 - fg
    lf = jnp.log(fg)
    cum = _dot(causal.astype(F32), lf)
    first_half = (lax.broadcasted_iota(jnp.int32, (c, 1), 0) < c // 2).astype(F32)
    ref = jnp.sum(lf * first_half, axis=0, keepdims=True)
    cend = jnp.sum(lf, axis=0, keepdims=True)
    scores = jnp.where(causal, _dot(q * jnp.exp(cum - ref), k * jnp.exp(ref - cum), NT), 0.0)
    out = _dot(scores, iv) + _dot(q * jnp.exp(cum), st_t, NT)
    st_new = st_t * jnp.exp(cend) + _dot(iv, k * jnp.exp(cend - cum), TN)
    res = _rms(out, gn) * _silu(gp)
    return res, st_new


def _hgrn_fwd(proj, lb, gn, *, heads, name):
    s = proj.shape[0]
    n = s // CHUNK

    def body(q_ref, f_ref, i_ref, g_ref, lb_ref, gn_ref, o_ref, st_ref, state):
        @pl.when(pl.program_id(1) == 0)
        def _():
            state[...] = jnp.zeros_like(state)

        st_ref[...] = state[...]
        res, st_new = _hgrn_chunk(q_ref[...], f_ref[...], i_ref[...], g_ref[...], lb_ref[...], gn_ref[...], state[...])
        o_ref[...] = res.astype(o_ref.dtype)
        state[...] = st_new

    blk = lambda off: pl.BlockSpec((CHUNK, HEAD), lambda h, c: (c, off + h))
    return pl.pallas_call(
        body, name=name, grid=(heads, n),
        in_specs=[blk(0), blk(heads), blk(2 * heads), blk(3 * heads),
                  pl.BlockSpec((1, HEAD), lambda h, c: (0, h)), pl.BlockSpec((1, HEAD), lambda h, c: (0, 0))],
        out_specs=[pl.BlockSpec((CHUNK, HEAD), lambda h, c: (c, h)),
                   pl.BlockSpec((None, None, HEAD, HEAD), lambda h, c: (h, c, 0, 0))],
        out_shape=[jax.ShapeDtypeStruct((s, heads * HEAD), BF16), jax.ShapeDtypeStruct((heads, n, HEAD, HEAD), F32)],
        scratch_shapes=[pltpu.VMEM((HEAD, HEAD), F32)],
        compiler_params=_params(2),
    )(proj, proj, proj, proj, lb, gn)


def _hgrn_bwd(proj, lb, gn, states, d_out, *, heads, name):
    s = proj.shape[0]
    n = s // CHUNK

    def body(q_ref, f_ref, i_ref, g_ref, lb_ref, gn_ref, st_ref, do_ref,
             dq_ref, df_ref, di_ref, dg_ref, dlb_ref, dgn_ref, dstate):
        h, c = pl.program_id(0), pl.program_id(1)

        @pl.when(c == 0)
        def _():
            dstate[...] = jnp.zeros_like(dstate)
            dlb_ref[...] = jnp.zeros_like(dlb_ref)

        @pl.when((c == 0) & (h == 0))
        def _():
            dgn_ref[...] = jnp.zeros_like(dgn_ref)

        _, vjp = jax.vjp(_hgrn_chunk, q_ref[...], f_ref[...], i_ref[...], g_ref[...], lb_ref[...], gn_ref[...],
                         st_ref[...])
        dq, df, di, dg, dlb, dgn, dst = vjp((do_ref[...].astype(F32), dstate[...]))
        dq_ref[...] = dq.astype(dq_ref.dtype)
        df_ref[...] = df.astype(df_ref.dtype)
        di_ref[...] = di.astype(di_ref.dtype)
        dg_ref[...] = dg.astype(dg_ref.dtype)
        dlb_ref[...] += dlb
        dgn_ref[...] += dgn
        dstate[...] = dst

    rev = lambda off: pl.BlockSpec((CHUNK, HEAD), lambda h, c: (n - 1 - c, off + h))
    out_blk = pl.BlockSpec((CHUNK, HEAD), lambda h, c: (n - 1 - c, h))
    width = heads * HEAD
    return pl.pallas_call(
        body, name=name, grid=(heads, n),
        in_specs=[rev(0), rev(heads), rev(2 * heads), rev(3 * heads),
                  pl.BlockSpec((1, HEAD), lambda h, c: (0, h)), pl.BlockSpec((1, HEAD), lambda h, c: (0, 0)),
                  pl.BlockSpec((None, None, HEAD, HEAD), lambda h, c: (h, n - 1 - c, 0, 0)), out_blk],
        out_specs=[out_blk, out_blk, out_blk, out_blk,
                   pl.BlockSpec((1, HEAD), lambda h, c: (0, h)), pl.BlockSpec((1, HEAD), lambda h, c: (0, 0))],
        out_shape=[jax.ShapeDtypeStruct((s, width), BF16)] * 4
        + [jax.ShapeDtypeStruct((1, width), F32), jax.ShapeDtypeStruct((1, HEAD), F32)],
        scratch_shapes=[pltpu.VMEM((HEAD, HEAD), F32)],
        compiler_params=_params(2),
    )(proj, proj, proj, proj, lb, gn, states, d_out)


def _shift_rows(x, d, rowi):
    if d == 0:
        return x
    n = x.shape[0]
    rolled = pltpu.roll(x, d % n, 0)
    keep = rowi >= d if d > 0 else rowi < n + d
    return jnp.where(keep, rolled, 0.0)


def _conv_pre(x, w_ref, rowi):
    acc = None
    for j in range(CONV_WIDTH):
        term = w_ref[j:j + 1, :] * _shift_rows(x, CONV_WIDTH - 1 - j, rowi)
        acc = term if acc is None else acc + term
    return acc


def _conv_fwd(proj, w, *, col_off, name, cb=256):
    s = proj.shape[0]
    width = w.shape[1]
    cb = min(cb, width)

    def body(x_ref, w_ref, o_ref):
        rowi = lax.broadcasted_iota(jnp.int32, (s, cb), 0)
        o_ref[...] = _silu(_conv_pre(x_ref[...], w_ref, rowi))

    return pl.pallas_call(
        body, name=name, grid=(width // cb,),
        in_specs=[pl.BlockSpec((s, cb), lambda j: (0, col_off // cb + j)), pl.BlockSpec((CONV_WIDTH, cb), lambda j: (0, j))],
        out_specs=pl.BlockSpec((s, cb), lambda j: (0, j)),
        out_shape=jax.ShapeDtypeStruct((s, width), F32),
        compiler_params=_params(1),
    )(proj, w)


def _conv_bwd(proj, w, d_out, *, col_off, name, cb=256):
    s = proj.shape[0]
    width = w.shape[1]
    cb = min(cb, width)

    def body(x_ref, w_ref, do_ref, dx_ref, dw_ref):
        rowi = lax.broadcasted_iota(jnp.int32, (s, cb), 0)
        x = x_ref[...]
        pre = _conv_pre(x, w_ref, rowi)
        sg = jax.nn.sigmoid(pre)
        dpre = do_ref[...] * (sg + pre * sg * (1.0 - sg))
        dx = None
        for j in range(CONV_WIDTH):
            d = CONV_WIDTH - 1 - j
            term = w_ref[j:j + 1, :] * _shift_rows(dpre, -d, rowi)
            dx = term if dx is None else dx + term
            dw_ref[j:j + 1, :] = jnp.sum(dpre * _shift_rows(x, d, rowi), axis=0, keepdims=True)
        dx_ref[...] = dx.astype(dx_ref.dtype)

    return pl.pallas_call(
        body, name=name, grid=(width // cb,),
        in_specs=[pl.BlockSpec((s, cb), lambda j: (0, col_off // cb + j)), pl.BlockSpec((CONV_WIDTH, cb), lambda j: (0, j)),
                  pl.BlockSpec((s, cb), lambda j: (0, j))],
        out_specs=[pl.BlockSpec((s, cb), lambda j: (0, j)), pl.BlockSpec((CONV_WIDTH, cb), lambda j: (0, j))],
        out_shape=[jax.ShapeDtypeStruct((s, width), BF16), jax.ShapeDtypeStruct((CONV_WIDTH, width), F32)],
        compiler_params=_params(1),
    )(proj, w, d_out)


def _delta_chunk(h, heads, qr, kr, vr, ab, zp, alog, dtb, gn, st):
    c = qr.shape[0]
    row = lax.broadcasted_iota(jnp.int32, (c, c), 0)
    col = lax.broadcasted_iota(jnp.int32, (c, c), 1)
    causal = row >= col
    strict = row > col
    lane = lax.broadcasted_iota(jnp.int32, (c, HEAD), 1)
    mine = lane == h
    la_full = -jnp.exp(alog) * _softplus(ab + dtb)
    cum_full = _dot(causal.astype(F32), la_full)
    cum = jnp.sum(jnp.where(mine, cum_full, 0.0), axis=1, keepdims=True)
    cend = jnp.sum(jnp.sum(jnp.where(mine, la_full, 0.0), axis=1, keepdims=True), axis=0, keepdims=True)
    beta = jnp.sum(jnp.where(lane == heads + h, jax.nn.sigmoid(ab), 0.0), axis=1, keepdims=True)
    cum_row = _dot(mine.astype(F32), cum_full, NT)
    decay = jnp.where(causal, jnp.exp(jnp.where(causal, cum - cum_row, 0.0)), 0.0)
    qn = qr * lax.rsqrt(jnp.sum(qr * qr, axis=-1, keepdims=True) + NORM_EPS) * (HEAD ** -0.5)
    kn = kr * lax.rsqrt(jnp.sum(kr * kr, axis=-1, keepdims=True) + NORM_EPS)
    kb = kn * beta
    lower = jnp.where(strict, _dot(kb, kn, NT) * decay, 0.0)
    inv = (row == col).astype(F32)
    lvl = 0
    while (1 << lvl) < c:
        same_pair = (row >> (lvl + 1)) == (col >> (lvl + 1))
        off_block = same_pair & (((row >> lvl) & 1) == 1) & (((col >> lvl) & 1) == 0)
        inv = inv - _dot(_dot(inv, jnp.where(off_block, lower, 0.0)), inv)
        lvl += 1
    ecum = jnp.exp(cum)
    u = _dot(inv, vr * beta)
    w = _dot(inv, kb * ecum)
    intra = _dot(qn, kn, NT) * decay
    v_new = u - _dot(w, st)
    out = _dot(qn * ecum, st) + _dot(intra, v_new)
    st_new = st * jnp.exp(cend) + _dot(kn * jnp.exp(cend - cum), v_new, TN)
    res = _rms(out, gn) * _silu(zp)
    return res, st_new


def _delta_fwd(qkv, ab, proj, hp, gn, *, heads, z_off, name):
    s = qkv.shape[0]
    n = s // CHUNK

    def body(q_ref, k_ref, v_ref, ab_ref, z_ref, hp_ref, gn_ref, o_ref, st_ref, state):
        h = pl.program_id(1)

        @pl.when(pl.program_id(0) == 0)
        def _():
            state[h] = jnp.zeros((HEAD, HEAD), F32)

        st = state[h]
        st_ref[...] = st
        res, st_new = _delta_chunk(h, heads, q_ref[...], k_ref[...], v_ref[...], ab_ref[...], z_ref[...],
                                   hp_ref[0:1, :], hp_ref[1:2, :], gn_ref[...], st)
        o_ref[...] = res.astype(o_ref.dtype)
        state[h] = st_new

    blk = lambda off: pl.BlockSpec((CHUNK, HEAD), lambda c, h: (c, off + h))
    return pl.pallas_call(
        body, name=name, grid=(n, heads),
        in_specs=[blk(0), blk(heads), blk(2 * heads), pl.BlockSpec((CHUNK, HEAD), lambda c, h: (c, 0)), blk(z_off),
                  pl.BlockSpec((8, HEAD), lambda c, h: (0, 0)), pl.BlockSpec((1, HEAD), lambda c, h: (0, 0))],
        out_specs=[pl.BlockSpec((CHUNK, HEAD), lambda c, h: (c, h)),
                   pl.BlockSpec((None, None, HEAD, HEAD), lambda c, h: (h, c, 0, 0))],
        out_shape=[jax.ShapeDtypeStruct((s, heads * HEAD), BF16), jax.ShapeDtypeStruct((heads, n, HEAD, HEAD), F32)],
        scratch_shapes=[pltpu.VMEM((heads, HEAD, HEAD), F32)],
        compiler_params=_params(2),
    )(qkv, qkv, qkv, ab, proj, hp, gn)


def _delta_bwd(qkv, ab, proj, hp, gn, states, d_out, *, heads, z_off, name):
    s = qkv.shape[0]
    n = s // CHUNK

    def body(q_ref, k_ref, v_ref, ab_ref, z_ref, hp_ref, gn_ref, st_ref, do_ref,
             dq_ref, dk_ref, dv_ref, dab_ref, dz_ref, dhp_ref, dgn_ref, dstate):
        c, h = pl.program_id(0), pl.program_id(1)

        @pl.when(c == 0)
        def _():
            dstate[h] = jnp.zeros((HEAD, HEAD), F32)

        @pl.when((c == 0) & (h == 0))
        def _():
            dgn_ref[...] = jnp.zeros_like(dgn_ref)
            dhp_ref[...] = jnp.zeros_like(dhp_ref)

        @pl.when(h == 0)
        def _():
            dab_ref[...] = jnp.zeros_like(dab_ref)

        fn = functools.partial(_delta_chunk, h, heads)
        _, vjp = jax.vjp(fn, q_ref[...], k_ref[...], v_ref[...], ab_ref[...], z_ref[...],
                         hp_ref[0:1, :], hp_ref[1:2, :], gn_ref[...], st_ref[...])
        dq, dk, dv, dab, dz, dal, ddt, dgn, dst = vjp((do_ref[...].astype(F32), dstate[h]))
        dq_ref[...] = dq
        dk_ref[...] = dk
        dv_ref[...] = dv
        dz_ref[...] = dz.astype(dz_ref.dtype)
        dab_ref[...] += dab
        dhp_ref[0:1, :] += dal
        dhp_ref[1:2, :] += ddt
        dgn_ref[...] += dgn
        dstate[h] = dst

    rev = lambda off: pl.BlockSpec((CHUNK, HEAD), lambda c, h: (n - 1 - c, off + h))
    width = heads * HEAD
    head_blk = pl.BlockSpec((CHUNK, HEAD), lambda c, h: (n - 1 - c, h))
    ab_blk = pl.BlockSpec((CHUNK, HEAD), lambda c, h: (n - 1 - c, 0))
    return pl.pallas_call(
        body, name=name, grid=(n, heads),
        in_specs=[rev(0), rev(heads), rev(2 * heads), ab_blk, rev(z_off),
                  pl.BlockSpec((8, HEAD), lambda c, h: (0, 0)), pl.BlockSpec((1, HEAD), lambda c, h: (0, 0)),
                  pl.BlockSpec((None, None, HEAD, HEAD), lambda c, h: (h, n - 1 - c, 0, 0)), head_blk],
        out_specs=[head_blk, head_blk, head_blk, ab_blk, head_blk,
                   pl.BlockSpec((8, HEAD), lambda c, h: (0, 0)), pl.BlockSpec((1, HEAD), lambda c, h: (0, 0))],
        out_shape=[jax.ShapeDtypeStruct((s, width), F32)] * 3
        + [jax.ShapeDtypeStruct((s, HEAD), F32), jax.ShapeDtypeStruct((s, width), BF16),
           jax.ShapeDtypeStruct((8, HEAD), F32), jax.ShapeDtypeStruct((1, HEAD), F32)],
        scratch_shapes=[pltpu.VMEM((heads, HEAD, HEAD), F32)],
        compiler_params=_params(2),
    )(qkv, qkv, qkv, ab, proj, hp, gn, states, d_out)


def _s5_scan(buf, lt_ref, cin_r, cin_i, tt, reverse):
    nblk = tt // 8
    hl = S5_HALF
    rowi = lax.broadcasted_iota(jnp.int32, (8, hl), 0)
    sign = -1.0 if reverse else 1.0

    def body(j, carry):
        cr, ci = carry
        off = pl.multiple_of((nblk - 1 - j if reverse else j) * 8, 8)
        xr = buf[pl.ds(off, 8), 0:hl]
        xi = buf[pl.ds(off, 8), hl:2 * hl]
        for lv, d in enumerate((1, 2, 4)):
            ar, ai = lt_ref[2 * lv], sign * lt_ref[2 * lv + 1]
            sr = _shift_rows(xr, -d if reverse else d, rowi)
            si = _shift_rows(xi, -d if reverse else d, rowi)
            xr, xi = xr + ar * sr - ai * si, xi + ar * si + ai * sr
        pr, pi = (lt_ref[8], -lt_ref[9]) if reverse else (lt_ref[6], lt_ref[7])
        xr, xi = xr + pr * cr - pi * ci, xi + pr * ci + pi * cr
        buf[pl.ds(off, 8), 0:hl] = xr
        buf[pl.ds(off, 8), hl:2 * hl] = xi
        edge = rowi == (0 if reverse else 7)
        return (jnp.sum(jnp.where(edge, xr, 0.0), axis=0, keepdims=True),
                jnp.sum(jnp.where(edge, xi, 0.0), axis=0, keepdims=True))

    return lax.fori_loop(0, nblk, body, (cin_r, cin_i))


def _s5_fwd(u, wb, wc, lt, dskip, *, name, tt=512):
    s, d = u.shape
    nb = d // HEAD
    tt = min(tt, s)
    nt = s // tt
    hl = S5_HALF

    def body(u_ref, wb_ref, wc_ref, lt_ref, d_ref, y_ref, cin_ref, buf, carry):
        @pl.when(pl.program_id(1) == 0)
        def _():
            carry[...] = jnp.zeros_like(carry)

        cin_ref[...] = carry[0:1, :]
        uv = u_ref[...]
        buf[...] = _dot(uv, wb_ref[...])
        cr, ci = _s5_scan(buf, lt_ref, carry[0:1, 0:hl], carry[0:1, hl:2 * hl], tt, False)
        carry[0:1, 0:hl] = cr
        carry[0:1, hl:2 * hl] = ci
        y_ref[...] = _dot(buf[...], wc_ref[...]) + d_ref[...] * uv

    return pl.pallas_call(
        body, name=name, grid=(nb, nt),
        in_specs=[pl.BlockSpec((tt, HEAD), lambda b, t: (t, b)),
                  pl.BlockSpec((None, HEAD, 2 * hl), lambda b, t: (b, 0, 0)),
                  pl.BlockSpec((None, 2 * hl, HEAD), lambda b, t: (b, 0, 0)),
                  pl.BlockSpec((None, 10, 8, hl), lambda b, t: (b, 0, 0, 0)),
                  pl.BlockSpec((1, HEAD), lambda b, t: (0, b))],
        out_specs=[pl.BlockSpec((tt, HEAD), lambda b, t: (t, b)),
                   pl.BlockSpec((None, None, 1, 2 * hl), lambda b, t: (b, t, 0, 0))],
        out_shape=[jax.ShapeDtypeStruct((s, d), F32), jax.ShapeDtypeStruct((nb, nt, 1, 2 * hl), F32)],
        scratch_shapes=[pltpu.VMEM((tt, 2 * hl), F32), pltpu.VMEM((8, 2 * hl), F32)],
        compiler_params=_params(2),
    )(u, wb, wc, lt, dskip)


def _s5_bwd(u, dy, wb, wc, lt, dskip, cins, *, name, tt=512):
    s, d = u.shape
    nb = d // HEAD
    tt = min(tt, s)
    nt = s // tt
    hl = S5_HALF

    def body(u_ref, dy_ref, wb_ref, wc_ref, lt_ref, d_ref, cin_ref,
             du_ref, dwb_ref, dwc_ref, dd_ref, dlam_ref, sbuf, abuf, acarry):
        @pl.when(pl.program_id(1) == 0)
        def _():
            acarry[...] = jnp.zeros_like(acarry)
            dwb_ref[...] = jnp.zeros_like(dwb_ref)
            dwc_ref[...] = jnp.zeros_like(dwc_ref)
            dd_ref[...] = jnp.zeros_like(dd_ref)
            dlam_ref[...] = jnp.zeros_like(dlam_ref)

        uv, dyv = u_ref[...], dy_ref[...]
        sbuf[...] = _dot(uv, wb_ref[...])
        _s5_scan(sbuf, lt_ref, cin_ref[:, 0:hl], cin_ref[:, hl:2 * hl], tt, False)
        abuf[...] = _dot(dyv, wc_ref[...], NT)
        ar, ai = _s5_scan(abuf, lt_ref, acarry[0:1, 0:hl], acarry[0:1, hl:2 * hl], tt, True)
        acarry[0:1, 0:hl] = ar
        acarry[0:1, hl:2 * hl] = ai
        du_ref[...] = _dot(abuf[...], wb_ref[...], NT) + d_ref[...] * dyv
        dwb_ref[...] += _dot(uv, abuf[...], TN)
        dwc_ref[...] += _dot(sbuf[...], dyv, TN)
        dd_ref[...] += jnp.sum(dyv * uv, axis=0, keepdims=True)
        first = lax.broadcasted_iota(jnp.int32, (tt, hl), 0) == 0
        spr = jnp.where(first, cin_ref[:, 0:hl], pltpu.roll(sbuf[:, 0:hl], 1, 0))
        spi = jnp.where(first, cin_ref[:, hl:2 * hl], pltpu.roll(sbuf[:, hl:2 * hl], 1, 0))
        avr, avi = abuf[:, 0:hl], abuf[:, hl:2 * hl]
        dlam_ref[:, 0:hl] += jnp.sum(avr * spr + avi * spi, axis=0, keepdims=True)
        dlam_ref[:, hl:2 * hl] += jnp.sum(avi * spr - avr * spi, axis=0, keepdims=True)

    rev = pl.BlockSpec((tt, HEAD), lambda b, t: (nt - 1 - t, b))
    return pl.pallas_call(
        body, name=name, grid=(nb, nt),
        in_specs=[rev, rev,
                  pl.BlockSpec((None, HEAD, 2 * hl), lambda b, t: (b, 0, 0)),
                  pl.BlockSpec((None, 2 * hl, HEAD), lambda b, t: (b, 0, 0)),
                  pl.BlockSpec((None, 10, 8, hl), lambda b, t: (b, 0, 0, 0)),
                  pl.BlockSpec((1, HEAD), lambda b, t: (0, b)),
                  pl.BlockSpec((None, None, 1, 2 * hl), lambda b, t: (b, nt - 1 - t, 0, 0))],
        out_specs=[rev,
                   pl.BlockSpec((None, HEAD, 2 * hl), lambda b, t: (b, 0, 0)),
                   pl.BlockSpec((None, 2 * hl, HEAD), lambda b, t: (b, 0, 0)),
                   pl.BlockSpec((1, HEAD), lambda b, t: (0, b)),
                   pl.BlockSpec((None, 1, 2 * hl), lambda b, t: (b, 0, 0))],
        out_shape=[jax.ShapeDtypeStruct((s, d), F32), jax.ShapeDtypeStruct(wb.shape, F32),
                   jax.ShapeDtypeStruct(wc.shape, F32), jax.ShapeDtypeStruct((1, d), F32),
                   jax.ShapeDtypeStruct((nb, 1, 2 * hl), F32)],
        scratch_shapes=[pltpu.VMEM((tt, 2 * hl), F32), pltpu.VMEM((tt, 2 * hl), F32), pltpu.VMEM((8, 2 * hl), F32)],
        compiler_params=_params(2),
    )(u, dy, wb, wc, lt, dskip, cins)


def _s5_pack(lr, li, br, bi, c_re, c_im):
    g = lr.shape[0]
    nb = g // S5_GB
    eye = jnp.eye(S5_GB, dtype=F32)
    bm = jnp.stack([br, bi]).reshape(2, nb, S5_GB, S5_STATE, S5_GROUP)
    wb = jnp.einsum("rbgpc,gh->bgcrhp", bm, eye).reshape(nb, HEAD, 2 * S5_HALF)
    cm = jnp.stack([c_re, -c_im]).reshape(2, nb, S5_GB, S5_GROUP, S5_STATE)
    wc = jnp.einsum("rbgcp,gh->brgphc", cm, eye).reshape(nb, 2 * S5_HALF, HEAD)
    pw = [(lr, li)]
    for _ in range(7):
        pr, pi = pw[-1]
        pw.append((pr * lr - pi * li, pr * li + pi * lr))
    blk = lambda a: a.reshape(nb, 1, S5_HALF)
    rows8 = lambda a: jnp.broadcast_to(blk(a), (nb, 8, S5_HALF))
    tables = []
    for n in (1, 2, 4):
        tables += [rows8(pw[n - 1][0]), rows8(pw[n - 1][1])]
    for order in (range(8), range(7, -1, -1)):
        tables += [jnp.concatenate([blk(pw[n][0]) for n in order], axis=1),
                   jnp.concatenate([blk(pw[n][1]) for n in order], axis=1)]
    return wb, wc, jnp.stack(tables, axis=1)


def _s5_unpack(dwb, dwc, dlam):
    nb = dwb.shape[0]
    g = nb * S5_GB
    eye = jnp.eye(S5_GB, dtype=F32)
    db = jnp.einsum("bgcrhp,gh->rbgpc", dwb.reshape(nb, S5_GB, S5_GROUP, 2, S5_GB, S5_STATE), eye)
    db = db.reshape(2, g, S5_STATE * S5_GROUP)
    dc = jnp.einsum("brgphc,gh->rbgcp", dwc.reshape(nb, 2, S5_GB, S5_STATE, S5_GB, S5_GROUP), eye)
    dc = dc.reshape(2, g, S5_GROUP, S5_STATE)
    dl = dlam.reshape(nb, 2, S5_GB, S5_STATE).transpose(1, 0, 2, 3).reshape(2, g, S5_STATE)
    return dl[0], dl[1], db[0], db[1], dc[0], -dc[1]


def _peer(r):
    mx, my, mc = lax.axis_index("x"), lax.axis_index("y"), lax.axis_index("c")
    px = 1 - mx if r & 4 else mx
    py = 1 - my if r & 2 else my
    pc = 1 - mc if r & 1 else mc
    return (px, py, pc), 4 * px + 2 * py + pc


def _exchange(x, *, gather, name):
    blk_shape = x.shape if gather else x.shape[1:]

    def body(x_ref, out_ref, send_sems, recv_sems, local_sem):
        _, me = _peer(0)
        mine = pltpu.make_async_copy(x_ref if gather else x_ref.at[me], out_ref.at[me], local_sem)
        mine.start()
        sends = []
        for r in range(1, N_DEV):
            pos, idx = _peer(r)
            sends.append(pltpu.make_async_remote_copy(
                src_ref=x_ref if gather else x_ref.at[idx], dst_ref=out_ref.at[me],
                send_sem=send_sems.at[r - 1], recv_sem=recv_sems.at[r - 1], device_id=pos, device_id_type=MESH))
            sends[-1].start()
        for r in range(1, N_DEV):
            pos, idx = _peer(r)
            pltpu.make_async_remote_copy(
                src_ref=x_ref if gather else x_ref.at[idx], dst_ref=out_ref.at[idx],
                send_sem=send_sems.at[r - 1], recv_sem=recv_sems.at[r - 1], device_id=pos, device_id_type=MESH).wait_recv()
        for cp in sends:
            cp.wait_send()
        mine.wait()

    return pl.pallas_call(
        body, name=name,
        in_specs=[pl.BlockSpec(memory_space=pl.ANY)], out_specs=pl.BlockSpec(memory_space=pl.ANY),
        out_shape=jax.ShapeDtypeStruct((N_DEV,) + tuple(blk_shape), x.dtype),
        scratch_shapes=[pltpu.SemaphoreType.DMA((N_DEV - 1,)), pltpu.SemaphoreType.DMA((N_DEV - 1,)),
                        pltpu.SemaphoreType.DMA],
    )(x)


def _adamw(w, parts, m, v, *, name, tr=128):
    r, c = w.shape
    npart = parts.shape[0]
    tr = min(tr, r)
    assert r % tr == 0, (name, r)

    def body(w_ref, p_ref, m_ref, v_ref, g_ref, d_ref, mo_ref, vo_ref):
        g = p_ref[0].astype(F32)
        for k in range(1, npart):
            g = g + p_ref[k].astype(F32)
        m2 = ADAM_B1 * m_ref[...] + (1.0 - ADAM_B1) * g
        v2 = ADAM_B2 * v_ref[...] + (1.0 - ADAM_B2) * (g * g)
        m_hat = m2 / (1.0 - ADAM_B1 ** ADAM_STEP)
        v_hat = v2 / (1.0 - ADAM_B2 ** ADAM_STEP)
        g_ref[...] = g
        d_ref[...] = -ADAM_LR * (m_hat / (jnp.sqrt(v_hat) + ADAM_EPS) + ADAM_WD * w_ref[...])
        mo_ref[...] = m2
        vo_ref[...] = v2

    blk = pl.BlockSpec((tr, c), lambda i: (i, 0))
    return pl.pallas_call(
        body, name=name, grid=(r // tr,),
        in_specs=[blk, pl.BlockSpec((npart, tr, c), lambda i: (0, i, 0)), blk, blk],
        out_specs=[blk] * 4, out_shape=[jax.ShapeDtypeStruct((r, c), F32)] * 4,
        compiler_params=_params(1),
    )(w, parts, m, v)


def _sum_parts(parts, *, name):
    npart = parts.shape[0]

    def body(p_ref, o_ref):
        g = p_ref[0]
        for k in range(1, npart):
            g = g + p_ref[k]
        o_ref[...] = g

    return pl.pallas_call(
        body, name=name, grid=(1,), in_specs=[_full_spec(parts)],
        out_specs=pl.BlockSpec(parts.shape[1:], lambda i: (0, 0)),
        out_shape=jax.ShapeDtypeStruct(parts.shape[1:], F32), compiler_params=_params(1),
    )(parts)


def _pack(arrs):
    flat = jnp.concatenate([a.reshape(-1).astype(F32) for a in arrs])
    pad = (-flat.shape[0]) % (HEAD * HEAD)
    return jnp.pad(flat, (0, pad)).reshape(-1, HEAD)


def _unpack(packed, shapes):
    flat = packed.reshape(-1)
    out, off = [], 0
    for shp in shapes:
        size = math.prod(shp)
        out.append(flat[off:off + size].reshape(shp))
        off += size
    return out


def _add_epilogue(acc, res):
    return (acc + res,)


def _relu2_epilogue(acc):
    r = jnp.maximum(acc, 0.0)
    return acc, r * r


def _ple_epilogue(acc, gpre, h):
    return h + jax.nn.sigmoid(gpre) * acc, acc


def kernel(x, p, norm_mix, norm_mlp, norm_ple, w_in_e, w_out_e, hgrn_lb, g_norm_a, conv_w, a_log, dt_bias, g_norm_b, s5_a_re, s5_a_im, s5_b_re, s5_b_im, s5_c_re, s5_c_im, s5_d, s5_log_dt, w_glu, b_glu, w_out_o, w_up, w_down, w_ple_gate, w_ple_proj, final_norm, loss_target, m_norm_mix, m_norm_mlp, m_norm_ple, m_w_in_e, m_w_out_e, m_hgrn_lb, m_g_norm_a, m_conv_w, m_a_log, m_dt_bias, m_g_norm_b, m_s5_a_re, m_s5_a_im, m_s5_b_re, m_s5_b_im, m_s5_c_re, m_s5_c_im, m_s5_d, m_s5_log_dt, m_w_glu, m_b_glu, m_w_out_o, m_w_up, m_w_down, m_w_ple_gate, m_w_ple_proj, m_final_norm, v_norm_mix, v_norm_mlp, v_norm_ple, v_w_in_e, v_w_out_e, v_hgrn_lb, v_g_norm_a, v_conv_w, v_a_log, v_dt_bias, v_g_norm_b, v_s5_a_re, v_s5_a_im, v_s5_b_re, v_s5_b_im, v_s5_c_re, v_s5_c_im, v_s5_d, v_s5_log_dt, v_w_glu, v_b_glu, v_w_out_o, v_w_up, v_w_down, v_w_ple_gate, v_w_ple_proj, v_final_norm):
    args = dict(locals())
    s, d = x.shape[1], x.shape[2]
    aw = d // 2
    ha = hb = aw // HEAD
    main = 4 * d
    z_col = 2 * d + 3 * aw
    ff = w_up.shape[2] * N_DEV
    ple = p.shape[-1]
    groups = d // S5_GROUP
    me = 4 * lax.axis_index("x") + 2 * lax.axis_index("y") + lax.axis_index("c")
    x2, target = x[0], loss_target[0]
    row = lambda a, i: a[i:i + 1]

    def gathered(w, nm):
        return _exchange(w.astype(BF16), gather=True, name="ag_" + nm)

    w_in = jnp.transpose(gathered(w_in_e[0], "w_in"), (1, 0, 2)).reshape(d, -1)
    w_main = w_in[:, :main]
    w_tail = jnp.pad(w_in[:, main:], ((0, 0), (0, HEAD - 2 * hb)))
    w_oe = gathered(w_out_e[0], "w_out_e").reshape(d, d)
    w_top, w_bot = w_oe[:aw], w_oe[aw:]
    w_gl = gathered(w_glu[0], "w_glu").reshape(d, d)
    w_oo = gathered(w_out_o[0], "w_out_o").reshape(d, d)
    w_upg = jnp.transpose(gathered(w_up, "w_up"), (1, 2, 0, 3)).reshape(2, d, ff)
    w_dng = jnp.transpose(gathered(w_down, "w_down"), (1, 0, 2, 3)).reshape(2, ff, d)
    w_pgg = jnp.transpose(gathered(w_ple_gate, "w_ple_gate"), (1, 0, 2, 3)).reshape(2, d, d)
    w_ppg = jnp.transpose(gathered(w_ple_proj, "w_ple_proj"), (1, 2, 0, 3)).reshape(2, ple, d)
    shard_shapes = [conv_w[0].shape, s5_d.shape, b_glu.shape]
    small = _exchange(_pack([conv_w[0], s5_d, b_glu]), gather=True, name="ag_small")
    conv_g, s5d_g, bglu_g = zip(*[_unpack(small[j], shard_shapes) for j in range(N_DEV)])
    conv_full = jnp.concatenate(conv_g, axis=1)
    s5d_full = jnp.concatenate(s5d_g, axis=1)
    bglu_full = jnp.concatenate(bglu_g, axis=1)

    lb_rows = [row(hgrn_lb, 0), row(hgrn_lb, 1), row(hgrn_lb, 2)]
    (lb0,) = _small_call(_lb0_stage, lb_rows, name="f_lb0")
    hp = jnp.zeros((8, HEAD), F32).at[0, :hb].set(a_log[0]).at[1, :hb].set(dt_bias[0])
    expand = jnp.asarray(np.kron(np.eye(S5_STATE, dtype=np.float32), np.ones((1, S5_GROUP), np.float32)))
    prep_in = [s5_a_re[0], s5_a_im[0], s5_log_dt[0].reshape(groups, 1),
               s5_b_re[0].reshape(groups, -1), s5_b_im[0].reshape(groups, -1), expand]
    lr, li, br, bi = _small_call(_s5_prep_stage, prep_in, name="f_s5_prep")
    wb, wc, lt = _s5_pack(lr, li, br, bi, s5_c_re[0], s5_c_im[0])
    fnorm = final_norm.reshape(1, d)

    def block_fwd(h, l):
        hn = _rows_call(_rms_stage, [h], [row(norm_mlp, l)], [BF16], name=f"f_norm_mlp{l}")
        up, act = _mm(hn, w_upg[l], epilogue=_relu2_epilogue, out_dtypes=(F32, BF16), name=f"f_up{l}")
        h2 = _mm(act, w_dng[l], extras=(h,), epilogue=_add_epilogue, name=f"f_down{l}")
        hq = _rows_call(_rms_stage, [h2], [row(norm_ple, l)], [BF16], name=f"f_norm_ple{l}")
        gpre = _mm(hq, w_pgg[l], name=f"f_ple_gate{l}")
        h3, pp = _mm(p[l, 0], w_ppg[l], extras=(gpre, h2), epilogue=_ple_epilogue, out_dtypes=(F32, F32),
                     name=f"f_ple_proj{l}")
        return h3, dict(h=h, hn=hn, up=up, act=act, h2=h2, hq=hq, gpre=gpre, pp=pp)

    hn0 = _rows_call(_rms_stage, [x2], [row(norm_mix, 0)], [BF16], name="f_norm_mix0")
    proj = _mm(hn0, w_main, name="f_proj")
    ab = _mm(hn0, w_tail, name="f_ab")
    oa, st_a = _hgrn_fwd(proj, lb0, g_norm_a, heads=ha, name="f_hgrn")
    qkv = _conv_fwd(proj, conv_full, col_off=2 * d, name="f_conv")
    ob, st_b = _delta_fwd(qkv, ab, proj, hp, g_norm_b, heads=hb, z_off=z_col // HEAD, name="f_delta")
    h1 = _mm(oa, w_top, extras=(x2,), epilogue=_add_epilogue, name="f_out_a")
    h1 = _mm(ob, w_bot, extras=(h1,), epilogue=_add_epilogue, name="f_out_b")
    h3, sv0 = block_fwd(h1, 0)

    u = _rows_call(_rms_stage, [h3], [row(norm_mix, 1)], [F32], name="f_norm_mix1")
    y, cins = _s5_fwd(u, wb, wc, lt, s5d_full, name="f_s5")
    act_g = _rows_call(_gelu_stage, [y], [], [BF16], name="f_gelu")
    gl_raw = _mm(act_g, w_gl, name="f_glu")
    glu = _rows_call(_glu_stage, [y, gl_raw], [bglu_full], [BF16], name="f_glu_gate")
    h4 = _mm(glu, w_oo, extras=(h3,), epilogue=_add_epilogue, name="f_out_o")
    h6, sv1 = block_fwd(h4, 1)
    dh, d_fnorm, loss8 = _loss_call(h6, fnorm, target, name="loss")
    loss = lax.psum(loss8[0, 0], ("x", "y", "c"))

    def block_bwd(dh3, l, sv):
        (dgpre, dpp), _ = _rows_vjp(_ple_stage, [sv["h2"], sv["gpre"], sv["pp"]], [], [dh3],
                                    row_grads={1: BF16, 2: BF16}, name=f"b_ple{l}")
        g_pp = _mm(p[l, 0], dpp, ta=True, out_dtypes=(BF16,), name=f"b_w_ple_proj{l}")
        g_pg = _mm(sv["hq"], dgpre, ta=True, out_dtypes=(BF16,), name=f"b_w_ple_gate{l}")
        dhq = _mm(dgpre, w_pgg[l], tb=True, name=f"b_ple_gate{l}")
        (dh2,), (g_nple,) = _rows_vjp(_rms_stage, [sv["h2"]], [row(norm_ple, l)], [dhq], row_grads={0: F32},
                                      adds={0: dh3}, name=f"b_norm_ple{l}")
        dup = _mm(dh2, w_dng[l], tb=True, extras=(sv["up"],), epilogue=_relu2_grad_epilogue, out_dtypes=(BF16,),
                  name=f"b_down{l}")
        g_dn = _mm(sv["act"], dh2, ta=True, out_dtypes=(BF16,), name=f"b_w_down{l}")
        g_up = _mm(sv["hn"], dup, ta=True, out_dtypes=(BF16,), name=f"b_w_up{l}")
        dhn = _mm(dup, w_upg[l], tb=True, name=f"b_up{l}")
        (dh0,), (g_nmlp,) = _rows_vjp(_rms_stage, [sv["h"]], [row(norm_mlp, l)], [dhn], row_grads={0: F32},
                                      adds={0: dh2}, name=f"b_norm_mlp{l}")
        return dh0, dict(w_ple_proj=g_pp, w_ple_gate=g_pg, norm_ple=g_nple, w_down=g_dn, w_up=g_up, norm_mlp=g_nmlp)

    dh4, gb1 = block_bwd(dh, 1, sv1)
    dglu = _mm(dh4, w_oo, tb=True, name="b_out_o")
    g_oo = _mm(glu, dh4, ta=True, out_dtypes=(BF16,), name="b_w_out_o")
    (dy1, dgl), (g_bglu,) = _rows_vjp(_glu_stage, [y, gl_raw], [bglu_full], [dglu], row_grads={0: F32, 1: BF16},
                                      name="b_glu_gate")
    g_gl = _mm(act_g, dgl, ta=True, out_dtypes=(BF16,), name="b_w_glu")
    dact = _mm(dgl, w_gl, tb=True, name="b_glu")
    (dy,), _ = _rows_vjp(_gelu_stage, [y], [], [dact], row_grads={0: F32}, adds={0: dy1}, name="b_gelu")
    du, dwb, dwc, g_s5d, dlam = _s5_bwd(u, dy, wb, wc, lt, s5d_full, cins, name="b_s5")
    (dh3,), (g_nmix1,) = _rows_vjp(_rms_stage, [h3], [row(norm_mix, 1)], [du], row_grads={0: F32}, adds={0: dh4},
                                   name="b_norm_mix1")
    dlr, dli, dbr, dbi, g_cre, g_cim = _s5_unpack(dwb, dwc, dlam)
    g_are, g_aim, g_ldt, g_bre, g_bim, _ = _small_vjp(_s5_prep_stage, prep_in, [dlr, dli, dbr, dbi], name="b_s5_prep")

    dh1, gb0 = block_bwd(dh3, 0, sv0)
    doa = _mm(dh1, w_top, tb=True, name="b_out_a")
    dob = _mm(dh1, w_bot, tb=True, name="b_out_b")
    g_oe = jnp.concatenate([_mm(oa, dh1, ta=True, out_dtypes=(BF16,), name="b_w_out_a"),
                            _mm(ob, dh1, ta=True, out_dtypes=(BF16,), name="b_w_out_b")], axis=0)
    dq, df, di, dg, dlb, g_gna = _hgrn_bwd(proj, lb0, g_norm_a, st_a, doa, heads=ha, name="b_hgrn")
    dqb, dkb, dvb, dab, dz, dhp, g_gnb = _delta_bwd(qkv, ab, proj, hp, g_norm_b, st_b, dob, heads=hb,
                                                    z_off=z_col // HEAD, name="b_delta")
    dqkv, g_conv = _conv_bwd(proj, conv_full, jnp.concatenate([dqb, dkb, dvb], axis=1), col_off=2 * d, name="b_conv")
    dproj = jnp.concatenate([dq, df, di, dg, dqkv, dz], axis=1)
    dhn0 = _mm(dproj, w_main, tb=True, name="b_proj")
    dhn0 = _mm(dab, w_tail, tb=True, extras=(dhn0,), epilogue=_add_epilogue, name="b_ab")
    g_main = _mm(hn0, dproj, ta=True, out_dtypes=(BF16,), name="b_w_proj")
    g_tail = _mm(hn0, dab, ta=True, out_dtypes=(BF16,), name="b_w_ab")
    (dx,), (g_nmix0,) = _rows_vjp(_rms_stage, [x2], [row(norm_mix, 0)], [dhn0], row_grads={0: F32}, adds={0: dh1},
                                  name="b_norm_mix0")
    g_lb = jnp.concatenate(_small_vjp(_lb0_stage, lb_rows, [dlb], name="b_lb0"), axis=0)

    small_grads = dict(
        norm_mix=jnp.concatenate([g_nmix0, g_nmix1], axis=0),
        norm_mlp=jnp.concatenate([gb0["norm_mlp"], gb1["norm_mlp"]], axis=0),
        norm_ple=jnp.concatenate([gb0["norm_ple"], gb1["norm_ple"]], axis=0),
        hgrn_lb=g_lb, g_norm_a=g_gna, a_log=dhp[0:1, :hb], dt_bias=dhp[1:2, :hb], g_norm_b=g_gnb,
        s5_a_re=g_are[None], s5_a_im=g_aim[None], s5_b_re=g_bre.reshape(s5_b_re.shape),
        s5_b_im=g_bim.reshape(s5_b_im.shape), s5_c_re=g_cre[None], s5_c_im=g_cim[None],
        s5_log_dt=g_ldt.reshape(1, groups), final_norm=d_fnorm.reshape(d),
        conv_w=g_conv, s5_d=g_s5d, b_glu=g_bglu)
    rep_names = ["norm_mix", "norm_mlp", "norm_ple", "hgrn_lb", "g_norm_a", "a_log", "dt_bias", "g_norm_b", "s5_a_re",
                 "s5_a_im", "s5_b_re", "s5_b_im", "s5_c_re", "s5_c_im", "s5_log_dt", "final_norm"]
    full_names = rep_names + ["conv_w", "s5_d", "b_glu"]
    parts = _exchange(_pack([small_grads[k] for k in full_names]), gather=True, name="ag_small_grads")
    summed = _unpack(_sum_parts(parts, name="sum_small_grads"), [small_grads[k].shape for k in full_names])
    summed = dict(zip(full_names, summed))
    cw = conv_w.shape[2]
    dshard = d // N_DEV
    shard_g = dict(conv_w=lax.dynamic_slice(summed["conv_w"], (0, me * cw), (CONV_WIDTH, cw))[None],
                   s5_d=lax.dynamic_slice(summed["s5_d"], (0, me * dshard), (1, dshard)),
                   b_glu=lax.dynamic_slice(summed["b_glu"], (0, me * dshard), (1, dshard)))
    small_names = rep_names + ["conv_w", "s5_d", "b_glu"]
    g_small = [summed[k] if k in rep_names else shard_g[k] for k in small_names]
    shapes = [args[k].shape for k in small_names]
    sm_out = _adamw(_pack([args[k] for k in small_names]), _pack(g_small)[None], _pack([args["m_" + k] for k in small_names]),
                    _pack([args["v_" + k] for k in small_names]), name="adamw_small")
    sm_out = [dict(zip(small_names, _unpack(o, shapes))) for o in sm_out]

    cols = w_in_e.shape[2]
    ffs = ff // N_DEV
    big = dict(
        w_in_e=jnp.transpose(jnp.concatenate([g_main, g_tail[:, :2 * hb]], axis=1).reshape(d, N_DEV, cols), (1, 0, 2)),
        w_out_e=g_oe.reshape(N_DEV, dshard, d),
        w_glu=g_gl.reshape(N_DEV, dshard, d),
        w_out_o=g_oo.reshape(N_DEV, dshard, d),
        w_up=jnp.transpose(jnp.stack([gb0["w_up"], gb1["w_up"]]).reshape(2, d, N_DEV, ffs), (2, 0, 1, 3)).reshape(N_DEV, 2 * d, ffs),
        w_down=jnp.transpose(jnp.stack([gb0["w_down"], gb1["w_down"]]).reshape(2, N_DEV, ffs, d), (1, 0, 2, 3)).reshape(N_DEV, 2 * ffs, d),
        w_ple_gate=jnp.transpose(jnp.stack([gb0["w_ple_gate"], gb1["w_ple_gate"]]).reshape(2, N_DEV, dshard, d), (1, 0, 2, 3)).reshape(N_DEV, 2 * dshard, d),
        w_ple_proj=jnp.transpose(jnp.stack([gb0["w_ple_proj"], gb1["w_ple_proj"]]).reshape(2, ple, N_DEV, dshard), (2, 0, 1, 3)).reshape(N_DEV, 2 * ple, dshard),
    )
    big_out = {}
    for k, g8 in big.items():
        recv = _exchange(g8, gather=False, name="rs_" + k)
        shp = args[k].shape
        two_d = lambda a: a.reshape(-1, shp[-1])
        outs = _adamw(two_d(args[k]), recv, two_d(args["m_" + k]), two_d(args["v_" + k]), name="adamw_" + k)
        big_out[k] = [o.reshape(shp) for o in outs]

    names = ["norm_mix", "norm_mlp", "norm_ple", "w_in_e", "w_out_e", "hgrn_lb", "g_norm_a", "conv_w", "a_log", "dt_bias",
             "g_norm_b", "s5_a_re", "s5_a_im", "s5_b_re", "s5_b_im", "s5_c_re", "s5_c_im", "s5_d", "s5_log_dt", "w_glu",
             "b_glu", "w_out_o", "w_up", "w_down", "w_ple_gate", "w_ple_proj", "final_norm"]
    result = [loss, dx[None]]
    for j in range(4):
        result += [big_out[k][j] if k in big_out else sm_out[j][k] for k in names]
    return tuple(result)
```

```python
import functools
import math

import numpy as np
import jax
import jax.numpy as jnp
from jax import lax
from jax.experimental import pallas as pl
from jax.experimental.pallas import tpu as pltpu

F32 = jnp.float32
BF16 = jnp.bfloat16
MM_DTYPE = BF16
HI = lax.Precision.HIGHEST
MESH = pl.DeviceIdType.MESH

NORM_EPS = 1e-6
CHUNK = 64
HEAD = 128
CONV_WIDTH = 4
S5_GROUP = 16
S5_STATE = 64
S5_GB = 8
S5_HALF = S5_GB * S5_STATE
N_DEV = 8
HEADS_PER_STEP = 4
ADAM_LR, ADAM_B1, ADAM_B2, ADAM_EPS, ADAM_WD, ADAM_STEP = 0.001, 0.9, 0.999, 1e-08, 0.01, 10
VMEM_LIMIT = 56 * 1024 * 1024

NN = (((1,), (0,)), ((), ()))
NT = (((1,), (1,)), ((), ()))
TN = (((0,), (0,)), ((), ()))


def _dot(a, b, dn=NN):
    return lax.dot_general(a, b, dn, precision=HI, preferred_element_type=F32)


def _hdot(a, b, dn=NN):
    return lax.dot_general(a, b, dn, precision=lax.Precision.HIGH, preferred_element_type=F32)


def _bdot_raw(a, b, dn=NN):
    return lax.dot_general(a.astype(BF16), b.astype(BF16), dn, preferred_element_type=F32)


@functools.partial(jax.custom_vjp, nondiff_argnums=(2,))
def _bdot(a, b, dn):
    return _bdot_raw(a, b, dn)


def _bdot_fwd(a, b, dn):
    return _bdot_raw(a, b, dn), (a, b)


def _bdot_bwd(dn, res, g):
    a, b = res
    if dn == NN:
        return _bdot_raw(g, b, NT), _bdot_raw(a, g, TN)
    if dn == NT:
        return _bdot_raw(g, b, NN), _bdot_raw(g, a, TN)
    assert dn == TN
    return _bdot_raw(b, g, NT), _bdot_raw(a, g, NN)


_bdot.defvjp(_bdot_fwd, _bdot_bwd)


def _params(n_axes):
    return pltpu.CompilerParams(dimension_semantics=("arbitrary",) * n_axes, vmem_limit_bytes=VMEM_LIMIT)


def _full_spec(a):
    nd = a.ndim
    return pl.BlockSpec(a.shape, lambda *_: (0,) * nd)


def _mm(a, b, *, name, ta=False, tb=False, extras=(), epilogue=None, out_dtypes=(F32,), tm=512, tn=1024, tk=2048):
    m = a.shape[1] if ta else a.shape[0]
    k = a.shape[0] if ta else a.shape[1]
    n = b.shape[0] if tb else b.shape[1]
    assert k == (b.shape[1] if tb else b.shape[0]), (name, a.shape, b.shape)
    tm, tn, tk = min(tm, m), min(tn, n), min(tk, k)
    assert m % tm == 0 and n % tn == 0 and k % tk == 0, (name, m, n, k)
    nk = k // tk
    n_ex, n_out = len(extras), len(out_dtypes)
    dn = (((0 if ta else 1,), (1 if tb else 0,)), ((), ()))

    def body(a_ref, b_ref, *rest):
        ex_refs, out_refs = rest[:n_ex], rest[n_ex:n_ex + n_out]
        part = lax.dot_general(a_ref[...].astype(MM_DTYPE), b_ref[...].astype(MM_DTYPE), dn, preferred_element_type=F32)

        def finish(acc):
            outs = epilogue(acc, *[r[...] for r in ex_refs]) if epilogue is not None else (acc,)
            for o_ref, o in zip(out_refs, outs):
                o_ref[...] = o.astype(o_ref.dtype)

        if nk == 1:
            finish(part)
            return
        acc_ref = rest[-1]
        kk = pl.program_id(2)

        @pl.when(kk == 0)
        def _():
            acc_ref[...] = part

        @pl.when((kk > 0) & (kk < nk - 1))
        def _():
            acc_ref[...] += part

        @pl.when(kk == nk - 1)
        def _():
            finish(acc_ref[...] + part)

    a_spec = pl.BlockSpec((tk, tm), lambda i, j, q: (q, i)) if ta else pl.BlockSpec((tm, tk), lambda i, j, q: (i, q))
    b_spec = pl.BlockSpec((tn, tk), lambda i, j, q: (j, q)) if tb else pl.BlockSpec((tk, tn), lambda i, j, q: (q, j))
    ex_specs = []
    for e in extras:
        if e.shape[0] == 1 and m != 1:
            ex_specs.append(pl.BlockSpec((1, tn), lambda i, j, q: (0, j)))
        else:
            ex_specs.append(pl.BlockSpec((tm, tn), lambda i, j, q: (i, j)))
    outs = pl.pallas_call(
        body, name=name, grid=(m // tm, n // tn, nk),
        in_specs=[a_spec, b_spec] + ex_specs,
        out_specs=[pl.BlockSpec((tm, tn), lambda i, j, q: (i, j)) for _ in out_dtypes],
        out_shape=[jax.ShapeDtypeStruct((m, n), dt) for dt in out_dtypes],
        scratch_shapes=[pltpu.VMEM((tm, tn), F32)] if nk > 1 else [],
        compiler_params=_params(3),
    )(a, b, *extras)
    return outs[0] if n_out == 1 else tuple(outs)


def _rows_call(fn, rows, consts, out_dtypes, *, name, tr=256):
    s = rows[0].shape[0]
    tr = min(tr, s)
    nr, nc = len(rows), len(consts)
    widths = [o.shape[1] for o in jax.eval_shape(
        fn, *[jax.ShapeDtypeStruct((tr, r.shape[1]), F32) for r in rows],
        *[jax.ShapeDtypeStruct(c.shape, F32) for c in consts])]

    def body(*refs):
        rv = [r[...].astype(F32) for r in refs[:nr]]
        cv = [c[...] for c in refs[nr:nr + nc]]
        for o_ref, o in zip(refs[nr + nc:], fn(*rv, *cv)):
            o_ref[...] = o.astype(o_ref.dtype)

    outs = pl.pallas_call(
        body, name=name, grid=(s // tr,),
        in_specs=[pl.BlockSpec((tr, r.shape[1]), lambda i: (i, 0)) for r in rows] + [_full_spec(c) for c in consts],
        out_specs=[pl.BlockSpec((tr, w), lambda i: (i, 0)) for w in widths],
        out_shape=[jax.ShapeDtypeStruct((s, w), dt) for w, dt in zip(widths, out_dtypes)],
        compiler_params=_params(1),
    )(*rows, *consts)
    return outs[0] if len(outs) == 1 else tuple(outs)


def _rows_vjp(fn, rows, consts, cots, *, name, row_grads, adds=None, tr=256):
    adds = adds or {}
    s = rows[0].shape[0]
    tr = min(tr, s)
    nr, nc, nt = len(rows), len(consts), len(cots)
    rg = sorted(row_grads)
    ad = sorted(adds)

    def body(*refs):
        rv = [r[...].astype(F32) for r in refs[:nr]]
        cv = [c[...] for c in refs[nr:nr + nc]]
        ct = [c[...].astype(F32) for c in refs[nr + nc:nr + nc + nt]]
        av = {i: r[...].astype(F32) for i, r in zip(ad, refs[nr + nc + nt:nr + nc + nt + len(ad)])}
        out_refs = refs[nr + nc + nt + len(ad):]
        _, vjp = jax.vjp(fn, *rv, *cv)
        grads = vjp(tuple(ct))
        for o_ref, i in zip(out_refs[:len(rg)], rg):
            g = grads[i]
            if i in av:
                g = g + av[i]
            o_ref[...] = g.astype(o_ref.dtype)

        @pl.when(pl.program_id(0) == 0)
        def _():
            for o_ref in out_refs[len(rg):]:
                o_ref[...] = jnp.zeros_like(o_ref)

        for o_ref, g in zip(out_refs[len(rg):], grads[nr:]):
            o_ref[...] += g

    row_spec = lambda a: pl.BlockSpec((tr, a.shape[1]), lambda i: (i, 0))
    outs = pl.pallas_call(
        body, name=name, grid=(s // tr,),
        in_specs=[row_spec(r) for r in rows] + [_full_spec(c) for c in consts] + [row_spec(c) for c in cots]
        + [row_spec(adds[i]) for i in ad],
        out_specs=[row_spec(rows[i]) for i in rg] + [_full_spec(c) for c in consts],
        out_shape=[jax.ShapeDtypeStruct(rows[i].shape, row_grads[i]) for i in rg]
        + [jax.ShapeDtypeStruct(c.shape, F32) for c in consts],
        compiler_params=_params(1),
    )(*rows, *consts, *cots, *[adds[i] for i in ad])
    return list(outs[:len(rg)]), list(outs[len(rg):])


def _small_call(fn, ins, *, name):
    shapes = jax.eval_shape(fn, *[jax.ShapeDtypeStruct(a.shape, F32) for a in ins])

    def body(*refs):
        for o_ref, o in zip(refs[len(ins):], fn(*[r[...] for r in refs[:len(ins)]])):
            o_ref[...] = o

    return pl.pallas_call(
        body, name=name, in_specs=[_full_spec(a) for a in ins],
        out_specs=[pl.BlockSpec(o.shape, functools.partial(lambda nd, *_: (0,) * nd, len(o.shape))) for o in shapes],
        out_shape=[jax.ShapeDtypeStruct(o.shape, F32) for o in shapes], grid=(1,),
        compiler_params=_params(1),
    )(*ins)


def _small_vjp(fn, ins, cots, *, name):
    def body(*refs):
        vals = [r[...] for r in refs[:len(ins)]]
        ct = [r[...] for r in refs[len(ins):len(ins) + len(cots)]]
        _, vjp = jax.vjp(fn, *vals)
        for o_ref, g in zip(refs[len(ins) + len(cots):], vjp(tuple(ct))):
            o_ref[...] = g

    return pl.pallas_call(
        body, name=name, in_specs=[_full_spec(a) for a in ins] + [_full_spec(c) for c in cots],
        out_specs=[_full_spec(a) for a in ins],
        out_shape=[jax.ShapeDtypeStruct(a.shape, F32) for a in ins], grid=(1,),
        compiler_params=_params(1),
    )(*ins, *cots)


def _rms(x, g):
    return x * lax.rsqrt(jnp.mean(x * x, axis=-1, keepdims=True) + NORM_EPS) * g


def _rms_stage(x, g):
    return (_rms(x, g),)


def _silu(x):
    return x * jax.nn.sigmoid(x)


def _softplus(x):
    return jnp.maximum(x, 0.0) + jnp.log1p(jnp.exp(-jnp.abs(x)))


def _gelu(x):
    return jax.nn.gelu(x, approximate=True)


def _gelu_stage(y):
    return (_gelu(y),)


def _glu_stage(y, gl_raw, b):
    return (_gelu(y) * jax.nn.sigmoid(gl_raw + b),)


def _ple_stage(h, gpre, pp):
    return (h + jax.nn.sigmoid(gpre) * pp,)


def _relu2_grad_epilogue(acc, up):
    return (acc * (2.0 * jnp.maximum(up, 0.0)),)


def _lb0_stage(x0, x1, x2):
    mx = jnp.maximum(jnp.maximum(x0, x1), x2)
    e0, e1, e2 = jnp.exp(x0 - mx), jnp.exp(x1 - mx), jnp.exp(x2 - mx)
    return (e0 / (e0 + e1 + e2),)


def _s5_prep_stage(a_re, a_im, log_dt, b_re, b_im, expand):
    step = jnp.exp(log_dt)
    mag = jnp.exp(a_re * step)
    lr = mag * jnp.cos(a_im * step)
    li = mag * jnp.sin(a_im * step)
    den = a_re * a_re + a_im * a_im
    cr = ((lr - 1.0) * a_re + li * a_im) / den
    ci = (li * a_re - (lr - 1.0) * a_im) / den
    cr_e, ci_e = _dot(cr, expand), _dot(ci, expand)
    return lr, li, cr_e * b_re - ci_e * b_im, cr_e * b_im + ci_e * b_re


def _loss_call(h, g, target, *, name, tr=256):
    s, d = h.shape
    tr = min(tr, s)

    def loss_fn(hv, gv, tv):
        err = _rms(hv, gv) - tv
        return 0.5 * jnp.sum(jnp.mean(err * err, axis=-1))

    def body(h_ref, g_ref, t_ref, dh_ref, dg_ref, loss_ref):
        val, (dh, dg) = jax.value_and_grad(loss_fn, argnums=(0, 1))(h_ref[...], g_ref[...], t_ref[...])
        dh_ref[...] = dh

        @pl.when(pl.program_id(0) == 0)
        def _():
            dg_ref[...] = jnp.zeros_like(dg_ref)
            loss_ref[...] = jnp.zeros_like(loss_ref)

        dg_ref[...] += dg
        loss_ref[...] += jnp.full(loss_ref.shape, val, F32)

    row = pl.BlockSpec((tr, d), lambda i: (i, 0))
    return pl.pallas_call(
        body, name=name, grid=(s // tr,),
        in_specs=[row, _full_spec(g), row],
        out_specs=[row, _full_spec(g), pl.BlockSpec((8, 128), lambda i: (0, 0))],
        out_shape=[jax.ShapeDtypeStruct((s, d), F32), jax.ShapeDtypeStruct(g.shape, F32),
                   jax.ShapeDtypeStruct((8, 128), F32)],
        compiler_params=_params(1),
    )(h, g, target)


def _hgrn_chunk(q, fp, iv, gp, lb, gn, st_t):
    c = q.shape[0]
    row = lax.broadcasted_iota(jnp.int32, (c, c), 0)
    col = lax.broadcasted_iota(jnp.int32, (c, c), 1)
    causal = row >= col
    fg = lb + (1.0 - lb) * jax.nn.sigmoid(fp)
    k = 1.0 - fg
    lf = jnp.log(fg)
    cum = _hdot(causal.astype(F32), lf)
    first_half = (lax.broadcasted_iota(jnp.int32, (c, 1), 0) < c // 2).astype(F32)
    ref = jnp.sum(lf * first_half, axis=0, keepdims=True)
    cend = jnp.sum(lf, axis=0, keepdims=True)
    scores = jnp.where(causal, _hdot(q * jnp.exp(cum - ref), k * jnp.exp(ref - cum), NT), 0.0)
    out = _bdot(scores, iv, NN) + _bdot(q * jnp.exp(cum), st_t, NT)
    st_new = st_t * jnp.exp(cend) + _bdot(iv, k * jnp.exp(cend - cum), TN)
    res = _rms(out, gn) * _silu(gp)
    return res, st_new


def _lanes(j):
    return slice(j * HEAD, (j + 1) * HEAD)


def _hgrn_fwd(proj, lb, gn, *, heads, name):
    s = proj.shape[0]
    n = s // CHUNK
    hpb = min(HEADS_PER_STEP, heads)
    assert heads % hpb == 0

    def body(q_ref, f_ref, i_ref, g_ref, lb_ref, gn_ref, o_ref, st_ref, state):
        @pl.when(pl.program_id(1) == 0)
        def _():
            state[...] = jnp.zeros_like(state)

        gnv = gn_ref[...]
        loaded = [(q_ref[:, _lanes(j)], f_ref[:, _lanes(j)], i_ref[:, _lanes(j)], g_ref[:, _lanes(j)],
                   lb_ref[:, _lanes(j)], state[j]) for j in range(hpb)]
        outs = [_hgrn_chunk(qv, fv, iv, gv, lbv, gnv, stv) for qv, fv, iv, gv, lbv, stv in loaded]
        for j, (res, st_new) in enumerate(outs):
            st_ref[j] = loaded[j][5]
            o_ref[:, _lanes(j)] = res.astype(o_ref.dtype)
            state[j] = st_new

    wide = hpb * HEAD
    blk = lambda off: pl.BlockSpec((CHUNK, wide), lambda h, c: (c, off // hpb + h))
    return pl.pallas_call(
        body, name=name, grid=(heads // hpb, n),
        in_specs=[blk(0), blk(heads), blk(2 * heads), blk(3 * heads),
                  pl.BlockSpec((1, wide), lambda h, c: (0, h)), pl.BlockSpec((1, HEAD), lambda h, c: (0, 0))],
        out_specs=[pl.BlockSpec((CHUNK, wide), lambda h, c: (c, h)),
                   pl.BlockSpec((hpb, None, HEAD, HEAD), lambda h, c: (h, c, 0, 0))],
        out_shape=[jax.ShapeDtypeStruct((s, heads * HEAD), BF16), jax.ShapeDtypeStruct((heads, n, HEAD, HEAD), F32)],
        scratch_shapes=[pltpu.VMEM((hpb, HEAD, HEAD), F32)],
        compiler_params=_params(2),
    )(proj, proj, proj, proj, lb, gn)


def _hgrn_bwd(proj, lb, gn, states, d_out, *, heads, name):
    s = proj.shape[0]
    n = s // CHUNK
    hpb = min(HEADS_PER_STEP, heads)

    def body(q_ref, f_ref, i_ref, g_ref, lb_ref, gn_ref, st_ref, do_ref,
             dq_ref, df_ref, di_ref, dg_ref, dlb_ref, dgn_ref, dstate):
        h, c = pl.program_id(0), pl.program_id(1)

        @pl.when(c == 0)
        def _():
            dstate[...] = jnp.zeros_like(dstate)
            dlb_ref[...] = jnp.zeros_like(dlb_ref)

        @pl.when((c == 0) & (h == 0))
        def _():
            dgn_ref[...] = jnp.zeros_like(dgn_ref)

        gnv = gn_ref[...]
        loaded = [(q_ref[:, _lanes(j)], f_ref[:, _lanes(j)], i_ref[:, _lanes(j)], g_ref[:, _lanes(j)],
                   lb_ref[:, _lanes(j)], st_ref[j], do_ref[:, _lanes(j)].astype(F32), dstate[j]) for j in range(hpb)]
        grads = []
        for qv, fv, iv, gv, lbv, stv, dov, dsv in loaded:
            _, vjp = jax.vjp(_hgrn_chunk, qv, fv, iv, gv, lbv, gnv, stv)
            grads.append(vjp((dov, dsv)))
        dgn_sum = None
        for j, (dq, df, di, dg, dlb, dgn, dst) in enumerate(grads):
            ln = _lanes(j)
            dq_ref[:, ln] = dq.astype(dq_ref.dtype)
            df_ref[:, ln] = df.astype(df_ref.dtype)
            di_ref[:, ln] = di.astype(di_ref.dtype)
            dg_ref[:, ln] = dg.astype(dg_ref.dtype)
            dlb_ref[:, ln] += dlb
            dgn_sum = dgn if dgn_sum is None else dgn_sum + dgn
            dstate[j] = dst
        dgn_ref[...] += dgn_sum

    wide = hpb * HEAD
    rev = lambda off: pl.BlockSpec((CHUNK, wide), lambda h, c: (n - 1 - c, off // hpb + h))
    out_blk = pl.BlockSpec((CHUNK, wide), lambda h, c: (n - 1 - c, h))
    width = heads * HEAD
    return pl.pallas_call(
        body, name=name, grid=(heads // hpb, n),
        in_specs=[rev(0), rev(heads), rev(2 * heads), rev(3 * heads),
                  pl.BlockSpec((1, wide), lambda h, c: (0, h)), pl.BlockSpec((1, HEAD), lambda h, c: (0, 0)),
                  pl.BlockSpec((hpb, None, HEAD, HEAD), lambda h, c: (h, n - 1 - c, 0, 0)), out_blk],
        out_specs=[out_blk, out_blk, out_blk, out_blk,
                   pl.BlockSpec((1, wide), lambda h, c: (0, h)), pl.BlockSpec((1, HEAD), lambda h, c: (0, 0))],
        out_shape=[jax.ShapeDtypeStruct((s, width), BF16)] * 4
        + [jax.ShapeDtypeStruct((1, width), F32), jax.ShapeDtypeStruct((1, HEAD), F32)],
        scratch_shapes=[pltpu.VMEM((hpb, HEAD, HEAD), F32)],
        compiler_params=_params(2),
    )(proj, proj, proj, proj, lb, gn, states, d_out)


def _shift_rows(x, d, rowi):
    if d == 0:
        return x
    n = x.shape[0]
    rolled = pltpu.roll(x, d % n, 0)
    keep = rowi >= d if d > 0 else rowi < n + d
    return jnp.where(keep, rolled, 0.0)


def _conv_pre(x, w_ref, rowi):
    acc = None
    for j in range(CONV_WIDTH):
        term = w_ref[j:j + 1, :] * _shift_rows(x, CONV_WIDTH - 1 - j, rowi)
        acc = term if acc is None else acc + term
    return acc


def _conv_fwd(proj, w, *, col_off, name, cb=256):
    s = proj.shape[0]
    width = w.shape[1]
    cb = min(cb, width)

    def body(x_ref, w_ref, o_ref):
        rowi = lax.broadcasted_iota(jnp.int32, (s, cb), 0)
        o_ref[...] = _silu(_conv_pre(x_ref[...], w_ref, rowi))

    return pl.pallas_call(
        body, name=name, grid=(width // cb,),
        in_specs=[pl.BlockSpec((s, cb), lambda j: (0, col_off // cb + j)), pl.BlockSpec((CONV_WIDTH, cb), lambda j: (0, j))],
        out_specs=pl.BlockSpec((s, cb), lambda j: (0, j)),
        out_shape=jax.ShapeDtypeStruct((s, width), F32),
        compiler_params=_params(1),
    )(proj, w)


def _conv_bwd(proj, w, d_out, *, col_off, name, cb=256):
    s = proj.shape[0]
    width = w.shape[1]
    cb = min(cb, width)

    def body(x_ref, w_ref, do_ref, dx_ref, dw_ref):
        rowi = lax.broadcasted_iota(jnp.int32, (s, cb), 0)
        x = x_ref[...]
        pre = _conv_pre(x, w_ref, rowi)
        sg = jax.nn.sigmoid(pre)
        dpre = do_ref[...] * (sg + pre * sg * (1.0 - sg))
        dx = None
        for j in range(CONV_WIDTH):
            d = CONV_WIDTH - 1 - j
            term = w_ref[j:j + 1, :] * _shift_rows(dpre, -d, rowi)
            dx = term if dx is None else dx + term
            dw_ref[j:j + 1, :] = jnp.sum(dpre * _shift_rows(x, d, rowi), axis=0, keepdims=True)
        dx_ref[...] = dx.astype(dx_ref.dtype)

    return pl.pallas_call(
        body, name=name, grid=(width // cb,),
        in_specs=[pl.BlockSpec((s, cb), lambda j: (0, col_off // cb + j)), pl.BlockSpec((CONV_WIDTH, cb), lambda j: (0, j)),
                  pl.BlockSpec((s, cb), lambda j: (0, j))],
        out_specs=[pl.BlockSpec((s, cb), lambda j: (0, j)), pl.BlockSpec((CONV_WIDTH, cb), lambda j: (0, j))],
        out_shape=[jax.ShapeDtypeStruct((s, width), BF16), jax.ShapeDtypeStruct((CONV_WIDTH, width), F32)],
        compiler_params=_params(1),
    )(proj, w, d_out)


def _delta_chunk(h, heads, qr, kr, vr, ab, zp, alog, dtb, gn, st):
    c = qr.shape[0]
    row = lax.broadcasted_iota(jnp.int32, (c, c), 0)
    col = lax.broadcasted_iota(jnp.int32, (c, c), 1)
    causal = row >= col
    strict = row > col
    lane = lax.broadcasted_iota(jnp.int32, (c, HEAD), 1)
    mine = lane == h
    la_full = -jnp.exp(alog) * _softplus(ab + dtb)
    cum_full = _hdot(causal.astype(F32), la_full)
    cum = jnp.sum(jnp.where(mine, cum_full, 0.0), axis=1, keepdims=True)
    cend = jnp.sum(jnp.sum(jnp.where(mine, la_full, 0.0), axis=1, keepdims=True), axis=0, keepdims=True)
    beta = jnp.sum(jnp.where(lane == heads + h, jax.nn.sigmoid(ab), 0.0), axis=1, keepdims=True)
    cum_row = _hdot(mine.astype(F32), cum_full, NT)
    decay = jnp.where(causal, jnp.exp(jnp.where(causal, cum - cum_row, 0.0)), 0.0)
    qn = qr * lax.rsqrt(jnp.sum(qr * qr, axis=-1, keepdims=True) + NORM_EPS) * (HEAD ** -0.5)
    kn = kr * lax.rsqrt(jnp.sum(kr * kr, axis=-1, keepdims=True) + NORM_EPS)
    kb = kn * beta
    lower = jnp.where(strict, _bdot(kb, kn, NT) * decay, 0.0)
    inv = (row == col).astype(F32)
    lvl = 0
    while (1 << lvl) < c:
        same_pair = (row >> (lvl + 1)) == (col >> (lvl + 1))
        off_block = same_pair & (((row >> lvl) & 1) == 1) & (((col >> lvl) & 1) == 0)
        inv = inv - _hdot(_hdot(inv, jnp.where(off_block, lower, 0.0)), inv)
        lvl += 1
    ecum = jnp.exp(cum)
    u = _hdot(inv, vr * beta)
    w = _hdot(inv, kb * ecum)
    intra = _bdot(qn, kn, NT) * decay
    v_new = u - _bdot(w, st, NN)
    out = _bdot(qn * ecum, st, NN) + _bdot(intra, v_new, NN)
    st_new = st * jnp.exp(cend) + _bdot(kn * jnp.exp(cend - cum), v_new, TN)
    res = _rms(out, gn) * _silu(zp)
    return res, st_new


def _delta_fwd(qkv, ab, proj, hp, gn, *, heads, z_off, name):
    s = qkv.shape[0]
    n = s // CHUNK

    hpb = min(HEADS_PER_STEP, heads)
    assert heads % hpb == 0 and z_off % hpb == 0

    def body(q_ref, k_ref, v_ref, ab_ref, z_ref, hp_ref, gn_ref, o_ref, st_ref, state):
        hb = pl.program_id(1)

        @pl.when(pl.program_id(0) == 0)
        def _():
            for j in range(hpb):
                state[hb * hpb + j] = jnp.zeros((HEAD, HEAD), F32)

        shared = (ab_ref[...], hp_ref[0:1, :], hp_ref[1:2, :], gn_ref[...])
        loaded = [(q_ref[:, _lanes(j)], k_ref[:, _lanes(j)], v_ref[:, _lanes(j)], z_ref[:, _lanes(j)],
                   state[hb * hpb + j]) for j in range(hpb)]
        outs = [_delta_chunk(hb * hpb + j, heads, qv, kv, vv, shared[0], zv, shared[1], shared[2], shared[3], stv)
                for j, (qv, kv, vv, zv, stv) in enumerate(loaded)]
        for j, (res, st_new) in enumerate(outs):
            st_ref[j] = loaded[j][4]
            o_ref[:, _lanes(j)] = res.astype(o_ref.dtype)
            state[hb * hpb + j] = st_new

    wide = hpb * HEAD
    blk = lambda off: pl.BlockSpec((CHUNK, wide), lambda c, h: (c, off // hpb + h))
    return pl.pallas_call(
        body, name=name, grid=(n, heads // hpb),
        in_specs=[blk(0), blk(heads), blk(2 * heads), pl.BlockSpec((CHUNK, HEAD), lambda c, h: (c, 0)), blk(z_off),
                  pl.BlockSpec((8, HEAD), lambda c, h: (0, 0)), pl.BlockSpec((1, HEAD), lambda c, h: (0, 0))],
        out_specs=[pl.BlockSpec((CHUNK, wide), lambda c, h: (c, h)),
                   pl.BlockSpec((hpb, None, HEAD, HEAD), lambda c, h: (h, c, 0, 0))],
        out_shape=[jax.ShapeDtypeStruct((s, heads * HEAD), BF16), jax.ShapeDtypeStruct((heads, n, HEAD, HEAD), F32)],
        scratch_shapes=[pltpu.VMEM((heads, HEAD, HEAD), F32)],
        compiler_params=_params(2),
    )(qkv, qkv, qkv, ab, proj, hp, gn)


def _delta_bwd(qkv, ab, proj, hp, gn, states, d_out, *, heads, z_off, name):
    s = qkv.shape[0]
    n = s // CHUNK
    hpb = min(HEADS_PER_STEP, heads)

    def body(q_ref, k_ref, v_ref, ab_ref, z_ref, hp_ref, gn_ref, st_ref, do_ref,
             dq_ref, dk_ref, dv_ref, dab_ref, dz_ref, dhp_ref, dgn_ref, dstate):
        c, hb = pl.program_id(0), pl.program_id(1)

        @pl.when(c == 0)
        def _():
            for j in range(hpb):
                dstate[hb * hpb + j] = jnp.zeros((HEAD, HEAD), F32)

        @pl.when((c == 0) & (hb == 0))
        def _():
            dgn_ref[...] = jnp.zeros_like(dgn_ref)
            dhp_ref[...] = jnp.zeros_like(dhp_ref)

        @pl.when(hb == 0)
        def _():
            dab_ref[...] = jnp.zeros_like(dab_ref)

        shared = (ab_ref[...], hp_ref[0:1, :], hp_ref[1:2, :], gn_ref[...])
        loaded = [(q_ref[:, _lanes(j)], k_ref[:, _lanes(j)], v_ref[:, _lanes(j)], z_ref[:, _lanes(j)], st_ref[j],
                   do_ref[:, _lanes(j)].astype(F32), dstate[hb * hpb + j]) for j in range(hpb)]
        grads = []
        for j, (qv, kv, vv, zv, stv, dov, dsv) in enumerate(loaded):
            fn = functools.partial(_delta_chunk, hb * hpb + j, heads)
            _, vjp = jax.vjp(fn, qv, kv, vv, shared[0], zv, shared[1], shared[2], shared[3], stv)
            grads.append(vjp((dov, dsv)))
        sums = None
        for j, (dq, dk, dv, dab, dz, dal, ddt, dgn, dst) in enumerate(grads):
            ln = _lanes(j)
            dq_ref[:, ln] = dq
            dk_ref[:, ln] = dk
            dv_ref[:, ln] = dv
            dz_ref[:, ln] = dz.astype(dz_ref.dtype)
            dstate[hb * hpb + j] = dst
            part = (dab, dal, ddt, dgn)
            sums = part if sums is None else tuple(a + b for a, b in zip(sums, part))
        dab_ref[...] += sums[0]
        dhp_ref[0:1, :] += sums[1]
        dhp_ref[1:2, :] += sums[2]
        dgn_ref[...] += sums[3]

    wide = hpb * HEAD
    rev = lambda off: pl.BlockSpec((CHUNK, wide), lambda c, h: (n - 1 - c, off // hpb + h))
    width = heads * HEAD
    head_blk = pl.BlockSpec((CHUNK, wide), lambda c, h: (n - 1 - c, h))
    ab_blk = pl.BlockSpec((CHUNK, HEAD), lambda c, h: (n - 1 - c, 0))
    return pl.pallas_call(
        body, name=name, grid=(n, heads // hpb),
        in_specs=[rev(0), rev(heads), rev(2 * heads), ab_blk, rev(z_off),
                  pl.BlockSpec((8, HEAD), lambda c, h: (0, 0)), pl.BlockSpec((1, HEAD), lambda c, h: (0, 0)),
                  pl.BlockSpec((hpb, None, HEAD, HEAD), lambda c, h: (h, n - 1 - c, 0, 0)), head_blk],
        out_specs=[head_blk, head_blk, head_blk, ab_blk, head_blk,
                   pl.BlockSpec((8, HEAD), lambda c, h: (0, 0)), pl.BlockSpec((1, HEAD), lambda c, h: (0, 0))],
        out_shape=[jax.ShapeDtypeStruct((s, width), F32)] * 3
        + [jax.ShapeDtypeStruct((s, HEAD), F32), jax.ShapeDtypeStruct((s, width), BF16),
           jax.ShapeDtypeStruct((8, HEAD), F32), jax.ShapeDtypeStruct((1, HEAD), F32)],
        scratch_shapes=[pltpu.VMEM((heads, HEAD, HEAD), F32)],
        compiler_params=_params(2),
    )(qkv, qkv, qkv, ab, proj, hp, gn, states, d_out)


def _s5_scan(buf, lt_ref, cin_r, cin_i, tt, reverse):
    nblk = tt // 8
    hl = S5_HALF
    rowi = lax.broadcasted_iota(jnp.int32, (8, hl), 0)
    sign = -1.0 if reverse else 1.0

    def body(j, carry):
        cr, ci = carry
        off = pl.multiple_of((nblk - 1 - j if reverse else j) * 8, 8)
        xr = buf[pl.ds(off, 8), 0:hl]
        xi = buf[pl.ds(off, 8), hl:2 * hl]
        for lv, d in enumerate((1, 2, 4)):
            ar, ai = lt_ref[2 * lv], sign * lt_ref[2 * lv + 1]
            sr = _shift_rows(xr, -d if reverse else d, rowi)
            si = _shift_rows(xi, -d if reverse else d, rowi)
            xr, xi = xr + ar * sr - ai * si, xi + ar * si + ai * sr
        pr, pi = (lt_ref[8], -lt_ref[9]) if reverse else (lt_ref[6], lt_ref[7])
        xr, xi = xr + pr * cr - pi * ci, xi + pr * ci + pi * cr
        buf[pl.ds(off, 8), 0:hl] = xr
        buf[pl.ds(off, 8), hl:2 * hl] = xi
        edge = rowi == (0 if reverse else 7)
        return (jnp.sum(jnp.where(edge, xr, 0.0), axis=0, keepdims=True),
                jnp.sum(jnp.where(edge, xi, 0.0), axis=0, keepdims=True))

    return lax.fori_loop(0, nblk, body, (cin_r, cin_i))


def _s5_fwd(u, wb, wc, lt, dskip, *, name, tt=512):
    s, d = u.shape
    nb = d // HEAD
    tt = min(tt, s)
    nt = s // tt
    hl = S5_HALF

    def body(u_ref, wb_ref, wc_ref, lt_ref, d_ref, y_ref, cin_ref, buf, carry):
        @pl.when(pl.program_id(1) == 0)
        def _():
            carry[...] = jnp.zeros_like(carry)

        cin_ref[...] = carry[0:1, :]
        uv = u_ref[...]
        buf[...] = _bdot_raw(uv, wb_ref[...])
        cr, ci = _s5_scan(buf, lt_ref, carry[0:1, 0:hl], carry[0:1, hl:2 * hl], tt, False)
        carry[0:1, 0:hl] = cr
        carry[0:1, hl:2 * hl] = ci
        y_ref[...] = _bdot_raw(buf[...], wc_ref[...]) + d_ref[...] * uv

    return pl.pallas_call(
        body, name=name, grid=(nb, nt),
        in_specs=[pl.BlockSpec((tt, HEAD), lambda b, t: (t, b)),
                  pl.BlockSpec((None, HEAD, 2 * hl), lambda b, t: (b, 0, 0)),
                  pl.BlockSpec((None, 2 * hl, HEAD), lambda b, t: (b, 0, 0)),
                  pl.BlockSpec((None, 10, 8, hl), lambda b, t: (b, 0, 0, 0)),
                  pl.BlockSpec((1, HEAD), lambda b, t: (0, b))],
        out_specs=[pl.BlockSpec((tt, HEAD), lambda b, t: (t, b)),
                   pl.BlockSpec((None, None, 1, 2 * hl), lambda b, t: (b, t, 0, 0))],
        out_shape=[jax.ShapeDtypeStruct((s, d), F32), jax.ShapeDtypeStruct((nb, nt, 1, 2 * hl), F32)],
        scratch_shapes=[pltpu.VMEM((tt, 2 * hl), F32), pltpu.VMEM((8, 2 * hl), F32)],
        compiler_params=_params(2),
    )(u, wb, wc, lt, dskip)


def _s5_bwd(u, dy, wb, wc, lt, dskip, cins, *, name, tt=512):
    s, d = u.shape
    nb = d // HEAD
    tt = min(tt, s)
    nt = s // tt
    hl = S5_HALF

    def body(u_ref, dy_ref, wb_ref, wc_ref, lt_ref, d_ref, cin_ref,
             du_ref, dwb_ref, dwc_ref, dd_ref, dlam_ref, sbuf, abuf, acarry):
        @pl.when(pl.program_id(1) == 0)
        def _():
            acarry[...] = jnp.zeros_like(acarry)
            dwb_ref[...] = jnp.zeros_like(dwb_ref)
            dwc_ref[...] = jnp.zeros_like(dwc_ref)
            dd_ref[...] = jnp.zeros_like(dd_ref)
            dlam_ref[...] = jnp.zeros_like(dlam_ref)

        uv, dyv = u_ref[...], dy_ref[...]
        sbuf[...] = _bdot_raw(uv, wb_ref[...])
        _s5_scan(sbuf, lt_ref, cin_ref[:, 0:hl], cin_ref[:, hl:2 * hl], tt, False)
        abuf[...] = _bdot_raw(dyv, wc_ref[...], NT)
        ar, ai = _s5_scan(abuf, lt_ref, acarry[0:1, 0:hl], acarry[0:1, hl:2 * hl], tt, True)
        acarry[0:1, 0:hl] = ar
        acarry[0:1, hl:2 * hl] = ai
        du_ref[...] = _bdot_raw(abuf[...], wb_ref[...], NT) + d_ref[...] * dyv
        dwb_ref[...] += _bdot_raw(uv, abuf[...], TN)
        dwc_ref[...] += _bdot_raw(sbuf[...], dyv, TN)
        dd_ref[...] += jnp.sum(dyv * uv, axis=0, keepdims=True)
        first = lax.broadcasted_iota(jnp.int32, (tt, hl), 0) == 0
        spr = jnp.where(first, cin_ref[:, 0:hl], pltpu.roll(sbuf[:, 0:hl], 1, 0))
        spi = jnp.where(first, cin_ref[:, hl:2 * hl], pltpu.roll(sbuf[:, hl:2 * hl], 1, 0))
        avr, avi = abuf[:, 0:hl], abuf[:, hl:2 * hl]
        dlam_ref[:, 0:hl] += jnp.sum(avr * spr + avi * spi, axis=0, keepdims=True)
        dlam_ref[:, hl:2 * hl] += jnp.sum(avi * spr - avr * spi, axis=0, keepdims=True)

    rev = pl.BlockSpec((tt, HEAD), lambda b, t: (nt - 1 - t, b))
    return pl.pallas_call(
        body, name=name, grid=(nb, nt),
        in_specs=[rev, rev,
                  pl.BlockSpec((None, HEAD, 2 * hl), lambda b, t: (b, 0, 0)),
                  pl.BlockSpec((None, 2 * hl, HEAD), lambda b, t: (b, 0, 0)),
                  pl.BlockSpec((None, 10, 8, hl), lambda b, t: (b, 0, 0, 0)),
                  pl.BlockSpec((1, HEAD), lambda b, t: (0, b)),
                  pl.BlockSpec((None, None, 1, 2 * hl), lambda b, t: (b, nt - 1 - t, 0, 0))],
        out_specs=[rev,
                   pl.BlockSpec((None, HEAD, 2 * hl), lambda b, t: (b, 0, 0)),
                   pl.BlockSpec((None, 2 * hl, HEAD), lambda b, t: (b, 0, 0)),
                   pl.BlockSpec((1, HEAD), lambda b, t: (0, b)),
                   pl.BlockSpec((None, 1, 2 * hl), lambda b, t: (b, 0, 0))],
        out_shape=[jax.ShapeDtypeStruct((s, d), F32), jax.ShapeDtypeStruct(wb.shape, F32),
                   jax.ShapeDtypeStruct(wc.shape, F32), jax.ShapeDtypeStruct((1, d), F32),
                   jax.ShapeDtypeStruct((nb, 1, 2 * hl), F32)],
        scratch_shapes=[pltpu.VMEM((tt, 2 * hl), F32), pltpu.VMEM((tt, 2 * hl), F32), pltpu.VMEM((8, 2 * hl), F32)],
        compiler_params=_params(2),
    )(u, dy, wb, wc, lt, dskip, cins)


def _s5_pack(lr, li, br, bi, c_re, c_im):
    g = lr.shape[0]
    nb = g // S5_GB
    eye = jnp.eye(S5_GB, dtype=F32)
    bm = jnp.stack([br, bi]).reshape(2, nb, S5_GB, S5_STATE, S5_GROUP)
    wb = jnp.einsum("rbgpc,gh->bgcrhp", bm, eye).reshape(nb, HEAD, 2 * S5_HALF)
    cm = jnp.stack([c_re, -c_im]).reshape(2, nb, S5_GB, S5_GROUP, S5_STATE)
    wc = jnp.einsum("rbgcp,gh->brgphc", cm, eye).reshape(nb, 2 * S5_HALF, HEAD)
    pw = [(lr, li)]
    for _ in range(7):
        pr, pi = pw[-1]
        pw.append((pr * lr - pi * li, pr * li + pi * lr))
    blk = lambda a: a.reshape(nb, 1, S5_HALF)
    rows8 = lambda a: jnp.broadcast_to(blk(a), (nb, 8, S5_HALF))
    tables = []
    for n in (1, 2, 4):
        tables += [rows8(pw[n - 1][0]), rows8(pw[n - 1][1])]
    for order in (range(8), range(7, -1, -1)):
        tables += [jnp.concatenate([blk(pw[n][0]) for n in order], axis=1),
                   jnp.concatenate([blk(pw[n][1]) for n in order], axis=1)]
    return wb, wc, jnp.stack(tables, axis=1)


def _s5_unpack(dwb, dwc, dlam):
    nb = dwb.shape[0]
    g = nb * S5_GB
    eye = jnp.eye(S5_GB, dtype=F32)
    db = jnp.einsum("bgcrhp,gh->rbgpc", dwb.reshape(nb, S5_GB, S5_GROUP, 2, S5_GB, S5_STATE), eye)
    db = db.reshape(2, g, S5_STATE * S5_GROUP)
    dc = jnp.einsum("brgphc,gh->rbgcp", dwc.reshape(nb, 2, S5_GB, S5_STATE, S5_GB, S5_GROUP), eye)
    dc = dc.reshape(2, g, S5_GROUP, S5_STATE)
    dl = dlam.reshape(nb, 2, S5_GB, S5_STATE).transpose(1, 0, 2, 3).reshape(2, g, S5_STATE)
    return dl[0], dl[1], db[0], db[1], dc[0], -dc[1]


def _peer(r):
    mx, my, mc = lax.axis_index("x"), lax.axis_index("y"), lax.axis_index("c")
    px = 1 - mx if r & 4 else mx
    py = 1 - my if r & 2 else my
    pc = 1 - mc if r & 1 else mc
    return (px, py, pc), 4 * px + 2 * py + pc


_COMM_SCRATCH = [pltpu.SemaphoreType.DMA((N_DEV - 1,)), pltpu.SemaphoreType.DMA((N_DEV - 1,)), pltpu.SemaphoreType.DMA]


def _all_to_all(x, *, name):
    def body(x_ref, out_ref, send_sems, recv_sems, local_sem):
        _, me = _peer(0)
        mine = pltpu.make_async_copy(x_ref.at[me], out_ref.at[me], local_sem)
        mine.start()
        sends = []
        for r in range(1, N_DEV):
            pos, idx = _peer(r)
            sends.append(pltpu.make_async_remote_copy(
                src_ref=x_ref.at[idx], dst_ref=out_ref.at[me],
                send_sem=send_sems.at[r - 1], recv_sem=recv_sems.at[r - 1], device_id=pos, device_id_type=MESH))
            sends[-1].start()
        for r in range(1, N_DEV):
            pos, idx = _peer(r)
            pltpu.make_async_remote_copy(
                src_ref=x_ref.at[idx], dst_ref=out_ref.at[idx],
                send_sem=send_sems.at[r - 1], recv_sem=recv_sems.at[r - 1], device_id=pos, device_id_type=MESH).wait_recv()
        for cp in sends:
            cp.wait_send()
        mine.wait()

    return pl.pallas_call(
        body, name=name,
        in_specs=[pl.BlockSpec(memory_space=pl.ANY)], out_specs=pl.BlockSpec(memory_space=pl.ANY),
        out_shape=jax.ShapeDtypeStruct(x.shape, x.dtype), scratch_shapes=list(_COMM_SCRATCH),
    )(x)


def _all_gather(x, *, name):
    def body(x_ref, out_ref, send_sems, recv_sems, local_sem):
        mx, my, mc = lax.axis_index("x"), lax.axis_index("y"), lax.axis_index("c")
        me, sibling = (mx, my, mc), (mx, my, 1 - mc)
        chips = [(1 - mx, my), (mx, 1 - my), (1 - mx, 1 - my)]

        def slot(px, py, pc):
            return out_ref.at[4 * px + 2 * py + pc]

        def copy(k, block, to, src=None):
            return pltpu.make_async_remote_copy(
                src_ref=slot(*block) if src is None else src, dst_ref=slot(*block),
                send_sem=send_sems.at[k], recv_sem=recv_sems.at[k], device_id=to, device_id_type=MESH)

        mine = pltpu.make_async_copy(x_ref, slot(*me), local_sem)
        mine.start()
        first = [copy(0, me, sibling, src=x_ref)] + [copy(1 + j, me, (*chip, mc), src=x_ref) for j, chip in enumerate(chips)]
        for cp in first:
            cp.start()
        passed = [copy(4 + j, (*chip, mc), sibling) for j, chip in enumerate(chips)]
        for j, chip in enumerate(chips):
            copy(1 + j, (*chip, mc), me).wait_recv()
            passed[j].start()
        copy(0, sibling, me).wait_recv()
        for j, chip in enumerate(chips):
            copy(4 + j, (*chip, 1 - mc), me).wait_recv()
        for cp in first + passed:
            cp.wait_send()
        mine.wait()

    return pl.pallas_call(
        body, name=name,
        in_specs=[pl.BlockSpec(memory_space=pl.ANY)], out_specs=pl.BlockSpec(memory_space=pl.ANY),
        out_shape=jax.ShapeDtypeStruct((N_DEV,) + tuple(x.shape), x.dtype), scratch_shapes=list(_COMM_SCRATCH),
    )(x)


def _adamw(w, parts, m, v, *, name, tr=128):
    r, c = w.shape
    npart = parts.shape[0]
    tr = min(tr, r)
    assert r % tr == 0, (name, r)

    def body(w_ref, p_ref, m_ref, v_ref, g_ref, d_ref, mo_ref, vo_ref):
        g = p_ref[0].astype(F32)
        for k in range(1, npart):
            g = g + p_ref[k].astype(F32)
        m2 = ADAM_B1 * m_ref[...] + (1.0 - ADAM_B1) * g
        v2 = ADAM_B2 * v_ref[...] + (1.0 - ADAM_B2) * (g * g)
        m_hat = m2 / (1.0 - ADAM_B1 ** ADAM_STEP)
        v_hat = v2 / (1.0 - ADAM_B2 ** ADAM_STEP)
        g_ref[...] = g
        d_ref[...] = -ADAM_LR * (m_hat / (jnp.sqrt(v_hat) + ADAM_EPS) + ADAM_WD * w_ref[...])
        mo_ref[...] = m2
        vo_ref[...] = v2

    blk = pl.BlockSpec((tr, c), lambda i: (i, 0))
    return pl.pallas_call(
        body, name=name, grid=(r // tr,),
        in_specs=[blk, pl.BlockSpec((npart, tr, c), lambda i: (0, i, 0)), blk, blk],
        out_specs=[blk] * 4, out_shape=[jax.ShapeDtypeStruct((r, c), F32)] * 4,
        compiler_params=_params(1),
    )(w, parts, m, v)


def _sum_parts(parts, *, name):
    npart = parts.shape[0]

    def body(p_ref, o_ref):
        g = p_ref[0]
        for k in range(1, npart):
            g = g + p_ref[k]
        o_ref[...] = g

    return pl.pallas_call(
        body, name=name, grid=(1,), in_specs=[_full_spec(parts)],
        out_specs=pl.BlockSpec(parts.shape[1:], lambda i: (0, 0)),
        out_shape=jax.ShapeDtypeStruct(parts.shape[1:], F32), compiler_params=_params(1),
    )(parts)


def _pack(arrs):
    flat = jnp.concatenate([a.reshape(-1).astype(F32) for a in arrs])
    pad = (-flat.shape[0]) % (HEAD * HEAD)
    return jnp.pad(flat, (0, pad)).reshape(-1, HEAD)


def _unpack(packed, shapes):
    flat = packed.reshape(-1)
    out, off = [], 0
    for shp in shapes:
        size = math.prod(shp)
        out.append(flat[off:off + size].reshape(shp))
        off += size
    return out


def _add_epilogue(acc, res):
    return (acc + res,)


def _relu2_epilogue(acc):
    r = jnp.maximum(acc, 0.0)
    return acc, r * r


def _ple_epilogue(acc, gpre, h):
    return h + jax.nn.sigmoid(gpre) * acc, acc


def kernel(x, p, norm_mix, norm_mlp, norm_ple, w_in_e, w_out_e, hgrn_lb, g_norm_a, conv_w, a_log, dt_bias, g_norm_b, s5_a_re, s5_a_im, s5_b_re, s5_b_im, s5_c_re, s5_c_im, s5_d, s5_log_dt, w_glu, b_glu, w_out_o, w_up, w_down, w_ple_gate, w_ple_proj, final_norm, loss_target, m_norm_mix, m_norm_mlp, m_norm_ple, m_w_in_e, m_w_out_e, m_hgrn_lb, m_g_norm_a, m_conv_w, m_a_log, m_dt_bias, m_g_norm_b, m_s5_a_re, m_s5_a_im, m_s5_b_re, m_s5_b_im, m_s5_c_re, m_s5_c_im, m_s5_d, m_s5_log_dt, m_w_glu, m_b_glu, m_w_out_o, m_w_up, m_w_down, m_w_ple_gate, m_w_ple_proj, m_final_norm, v_norm_mix, v_norm_mlp, v_norm_ple, v_w_in_e, v_w_out_e, v_hgrn_lb, v_g_norm_a, v_conv_w, v_a_log, v_dt_bias, v_g_norm_b, v_s5_a_re, v_s5_a_im, v_s5_b_re, v_s5_b_im, v_s5_c_re, v_s5_c_im, v_s5_d, v_s5_log_dt, v_w_glu, v_b_glu, v_w_out_o, v_w_up, v_w_down, v_w_ple_gate, v_w_ple_proj, v_final_norm):
    args = dict(locals())
    s, d = x.shape[1], x.shape[2]
    aw = d // 2
    ha = hb = aw // HEAD
    main = 4 * d
    z_col = 2 * d + 3 * aw
    ff = w_up.shape[2] * N_DEV
    ple = p.shape[-1]
    groups = d // S5_GROUP
    me = 4 * lax.axis_index("x") + 2 * lax.axis_index("y") + lax.axis_index("c")
    x2, target = x[0], loss_target[0]
    row = lambda a, i: a[i:i + 1]

    def gathered(w, nm):
        return _all_gather(w.astype(BF16), name="ag_" + nm)

    w_in = jnp.transpose(gathered(w_in_e[0], "w_in"), (1, 0, 2)).reshape(d, -1)
    w_main = w_in[:, :main]
    w_tail = jnp.pad(w_in[:, main:], ((0, 0), (0, HEAD - 2 * hb)))
    w_oe = gathered(w_out_e[0], "w_out_e").reshape(d, d)
    w_top, w_bot = w_oe[:aw], w_oe[aw:]
    w_gl = gathered(w_glu[0], "w_glu").reshape(d, d)
    w_oo = gathered(w_out_o[0], "w_out_o").reshape(d, d)
    w_upg = jnp.transpose(gathered(w_up, "w_up"), (1, 2, 0, 3)).reshape(2, d, ff)
    w_dng = jnp.transpose(gathered(w_down, "w_down"), (1, 0, 2, 3)).reshape(2, ff, d)
    w_pgg = jnp.transpose(gathered(w_ple_gate, "w_ple_gate"), (1, 0, 2, 3)).reshape(2, d, d)
    w_ppg = jnp.transpose(gathered(w_ple_proj, "w_ple_proj"), (1, 2, 0, 3)).reshape(2, ple, d)
    shard_shapes = [conv_w[0].shape, s5_d.shape, b_glu.shape]
    small = _all_gather(_pack([conv_w[0], s5_d, b_glu]), name="ag_small")
    conv_g, s5d_g, bglu_g = zip(*[_unpack(small[j], shard_shapes) for j in range(N_DEV)])
    conv_full = jnp.concatenate(conv_g, axis=1)
    s5d_full = jnp.concatenate(s5d_g, axis=1)
    bglu_full = jnp.concatenate(bglu_g, axis=1)

    lb_rows = [row(hgrn_lb, 0), row(hgrn_lb, 1), row(hgrn_lb, 2)]
    (lb0,) = _small_call(_lb0_stage, lb_rows, name="f_lb0")
    hp = jnp.zeros((8, HEAD), F32).at[0, :hb].set(a_log[0]).at[1, :hb].set(dt_bias[0])
    expand = jnp.asarray(np.kron(np.eye(S5_STATE, dtype=np.float32), np.ones((1, S5_GROUP), np.float32)))
    prep_in = [s5_a_re[0], s5_a_im[0], s5_log_dt[0].reshape(groups, 1),
               s5_b_re[0].reshape(groups, -1), s5_b_im[0].reshape(groups, -1), expand]
    lr, li, br, bi = _small_call(_s5_prep_stage, prep_in, name="f_s5_prep")
    wb, wc, lt = _s5_pack(lr, li, br, bi, s5_c_re[0], s5_c_im[0])
    fnorm = final_norm.reshape(1, d)

    def block_fwd(h, l):
        hn = _rows_call(_rms_stage, [h], [row(norm_mlp, l)], [BF16], name=f"f_norm_mlp{l}")
        up, act = _mm(hn, w_upg[l], epilogue=_relu2_epilogue, out_dtypes=(F32, BF16), name=f"f_up{l}")
        h2 = _mm(act, w_dng[l], extras=(h,), epilogue=_add_epilogue, name=f"f_down{l}")
        hq = _rows_call(_rms_stage, [h2], [row(norm_ple, l)], [BF16], name=f"f_norm_ple{l}")
        gpre = _mm(hq, w_pgg[l], name=f"f_ple_gate{l}")
        h3, pp = _mm(p[l, 0], w_ppg[l], extras=(gpre, h2), epilogue=_ple_epilogue, out_dtypes=(F32, F32),
                     name=f"f_ple_proj{l}")
        return h3, dict(h=h, hn=hn, up=up, act=act, h2=h2, hq=hq, gpre=gpre, pp=pp)

    hn0 = _rows_call(_rms_stage, [x2], [row(norm_mix, 0)], [BF16], name="f_norm_mix0")
    proj = _mm(hn0, w_main, name="f_proj")
    ab = _mm(hn0, w_tail, name="f_ab")
    oa, st_a = _hgrn_fwd(proj, lb0, g_norm_a, heads=ha, name="f_hgrn")
    qkv = _conv_fwd(proj, conv_full, col_off=2 * d, name="f_conv")
    ob, st_b = _delta_fwd(qkv, ab, proj, hp, g_norm_b, heads=hb, z_off=z_col // HEAD, name="f_delta")
    h1 = _mm(oa, w_top, extras=(x2,), epilogue=_add_epilogue, name="f_out_a")
    h1 = _mm(ob, w_bot, extras=(h1,), epilogue=_add_epilogue, name="f_out_b")
    h3, sv0 = block_fwd(h1, 0)

    u = _rows_call(_rms_stage, [h3], [row(norm_mix, 1)], [F32], name="f_norm_mix1")
    y, cins = _s5_fwd(u, wb, wc, lt, s5d_full, name="f_s5")
    act_g = _rows_call(_gelu_stage, [y], [], [BF16], name="f_gelu")
    gl_raw = _mm(act_g, w_gl, name="f_glu")
    glu = _rows_call(_glu_stage, [y, gl_raw], [bglu_full], [BF16], name="f_glu_gate")
    h4 = _mm(glu, w_oo, extras=(h3,), epilogue=_add_epilogue, name="f_out_o")
    h6, sv1 = block_fwd(h4, 1)
    dh, d_fnorm, loss8 = _loss_call(h6, fnorm, target, name="loss")
    loss = lax.psum(loss8[0, 0], ("x", "y", "c"))

    def block_bwd(dh3, l, sv):
        (dgpre, dpp), _ = _rows_vjp(_ple_stage, [sv["h2"], sv["gpre"], sv["pp"]], [], [dh3],
                                    row_grads={1: BF16, 2: BF16}, name=f"b_ple{l}")
        g_pp = _mm(p[l, 0], dpp, ta=True, out_dtypes=(BF16,), name=f"b_w_ple_proj{l}")
        g_pg = _mm(sv["hq"], dgpre, ta=True, out_dtypes=(BF16,), name=f"b_w_ple_gate{l}")
        dhq = _mm(dgpre, w_pgg[l], tb=True, name=f"b_ple_gate{l}")
        (dh2,), (g_nple,) = _rows_vjp(_rms_stage, [sv["h2"]], [row(norm_ple, l)], [dhq], row_grads={0: F32},
                                      adds={0: dh3}, name=f"b_norm_ple{l}")
        dup = _mm(dh2, w_dng[l], tb=True, extras=(sv["up"],), epilogue=_relu2_grad_epilogue, out_dtypes=(BF16,),
                  name=f"b_down{l}")
        g_dn = _mm(sv["act"], dh2, ta=True, out_dtypes=(BF16,), name=f"b_w_down{l}")
        g_up = _mm(sv["hn"], dup, ta=True, out_dtypes=(BF16,), name=f"b_w_up{l}")
        dhn = _mm(dup, w_upg[l], tb=True, name=f"b_up{l}")
        (dh0,), (g_nmlp,) = _rows_vjp(_rms_stage, [sv["h"]], [row(norm_mlp, l)], [dhn], row_grads={0: F32},
                                      adds={0: dh2}, name=f"b_norm_mlp{l}")
        return dh0, dict(w_ple_proj=g_pp, w_ple_gate=g_pg, norm_ple=g_nple, w_down=g_dn, w_up=g_up, norm_mlp=g_nmlp)

    dh4, gb1 = block_bwd(dh, 1, sv1)
    dglu = _mm(dh4, w_oo, tb=True, name="b_out_o")
    g_oo = _mm(glu, dh4, ta=True, out_dtypes=(BF16,), name="b_w_out_o")
    (dy1, dgl), (g_bglu,) = _rows_vjp(_glu_stage, [y, gl_raw], [bglu_full], [dglu], row_grads={0: F32, 1: BF16},
                                      name="b_glu_gate")
    g_gl = _mm(act_g, dgl, ta=True, out_dtypes=(BF16,), name="b_w_glu")
    dact = _mm(dgl, w_gl, tb=True, name="b_glu")
    (dy,), _ = _rows_vjp(_gelu_stage, [y], [], [dact], row_grads={0: F32}, adds={0: dy1}, name="b_gelu")
    du, dwb, dwc, g_s5d, dlam = _s5_bwd(u, dy, wb, wc, lt, s5d_full, cins, name="b_s5")
    (dh3,), (g_nmix1,) = _rows_vjp(_rms_stage, [h3], [row(norm_mix, 1)], [du], row_grads={0: F32}, adds={0: dh4},
                                   name="b_norm_mix1")
    dlr, dli, dbr, dbi, g_cre, g_cim = _s5_unpack(dwb, dwc, dlam)
    g_are, g_aim, g_ldt, g_bre, g_bim, _ = _small_vjp(_s5_prep_stage, prep_in, [dlr, dli, dbr, dbi], name="b_s5_prep")

    dh1, gb0 = block_bwd(dh3, 0, sv0)
    doa = _mm(dh1, w_top, tb=True, name="b_out_a")
    dob = _mm(dh1, w_bot, tb=True, name="b_out_b")
    g_oe = jnp.concatenate([_mm(oa, dh1, ta=True, out_dtypes=(BF16,), name="b_w_out_a"),
                            _mm(ob, dh1, ta=True, out_dtypes=(BF16,), name="b_w_out_b")], axis=0)
    dq, df, di, dg, dlb, g_gna = _hgrn_bwd(proj, lb0, g_norm_a, st_a, doa, heads=ha, name="b_hgrn")
    dqb, dkb, dvb, dab, dz, dhp, g_gnb = _delta_bwd(qkv, ab, proj, hp, g_norm_b, st_b, dob, heads=hb,
                                                    z_off=z_col // HEAD, name="b_delta")
    dqkv, g_conv = _conv_bwd(proj, conv_full, jnp.concatenate([dqb, dkb, dvb], axis=1), col_off=2 * d, name="b_conv")
    dproj = jnp.concatenate([dq, df, di, dg, dqkv, dz], axis=1)
    dhn0 = _mm(dproj, w_main, tb=True, name="b_proj")
    dhn0 = _mm(dab, w_tail, tb=True, extras=(dhn0,), epilogue=_add_epilogue, name="b_ab")
    g_main = _mm(hn0, dproj, ta=True, out_dtypes=(BF16,), name="b_w_proj")
    g_tail = _mm(hn0, dab, ta=True, out_dtypes=(BF16,), name="b_w_ab")
    (dx,), (g_nmix0,) = _rows_vjp(_rms_stage, [x2], [row(norm_mix, 0)], [dhn0], row_grads={0: F32}, adds={0: dh1},
                                  name="b_norm_mix0")
    g_lb = jnp.concatenate(_small_vjp(_lb0_stage, lb_rows, [dlb], name="b_lb0"), axis=0)

    small_grads = dict(
        norm_mix=jnp.concatenate([g_nmix0, g_nmix1], axis=0),
        norm_mlp=jnp.concatenate([gb0["norm_mlp"], gb1["norm_mlp"]], axis=0),
        norm_ple=jnp.concatenate([gb0["norm_ple"], gb1["norm_ple"]], axis=0),
        hgrn_lb=g_lb, g_norm_a=g_gna, a_log=dhp[0:1, :hb], dt_bias=dhp[1:2, :hb], g_norm_b=g_gnb,
        s5_a_re=g_are[None], s5_a_im=g_aim[None], s5_b_re=g_bre.reshape(s5_b_re.shape),
        s5_b_im=g_bim.reshape(s5_b_im.shape), s5_c_re=g_cre[None], s5_c_im=g_cim[None],
        s5_log_dt=g_ldt.reshape(1, groups), final_norm=d_fnorm.reshape(d),
        conv_w=g_conv, s5_d=g_s5d, b_glu=g_bglu)
    rep_names = ["norm_mix", "norm_mlp", "norm_ple", "hgrn_lb", "g_norm_a", "a_log", "dt_bias", "g_norm_b", "s5_a_re",
                 "s5_a_im", "s5_b_re", "s5_b_im", "s5_c_re", "s5_c_im", "s5_log_dt", "final_norm"]
    full_names = rep_names + ["conv_w", "s5_d", "b_glu"]
    parts = _all_gather(_pack([small_grads[k] for k in full_names]), name="ag_small_grads")
    summed = _unpack(_sum_parts(parts, name="sum_small_grads"), [small_grads[k].shape for k in full_names])
    summed = dict(zip(full_names, summed))
    cw = conv_w.shape[2]
    dshard = d // N_DEV
    shard_g = dict(conv_w=lax.dynamic_slice(summed["conv_w"], (0, me * cw), (CONV_WIDTH, cw))[None],
                   s5_d=lax.dynamic_slice(summed["s5_d"], (0, me * dshard), (1, dshard)),
                   b_glu=lax.dynamic_slice(summed["b_glu"], (0, me * dshard), (1, dshard)))
    small_names = rep_names + ["conv_w", "s5_d", "b_glu"]
    g_small = [summed[k] if k in rep_names else shard_g[k] for k in small_names]
    shapes = [args[k].shape for k in small_names]
    sm_out = _adamw(_pack([args[k] for k in small_names]), _pack(g_small)[None], _pack([args["m_" + k] for k in small_names]),
                    _pack([args["v_" + k] for k in small_names]), name="adamw_small")
    sm_out = [dict(zip(small_names, _unpack(o, shapes))) for o in sm_out]

    cols = w_in_e.shape[2]
    ffs = ff // N_DEV
    big = dict(
        w_in_e=jnp.transpose(jnp.concatenate([g_main, g_tail[:, :2 * hb]], axis=1).reshape(d, N_DEV, cols), (1, 0, 2)),
        w_out_e=g_oe.reshape(N_DEV, dshard, d),
        w_glu=g_gl.reshape(N_DEV, dshard, d),
        w_out_o=g_oo.reshape(N_DEV, dshard, d),
        w_up=jnp.transpose(jnp.stack([gb0["w_up"], gb1["w_up"]]).reshape(2, d, N_DEV, ffs), (2, 0, 1, 3)).reshape(N_DEV, 2 * d, ffs),
        w_down=jnp.transpose(jnp.stack([gb0["w_down"], gb1["w_down"]]).reshape(2, N_DEV, ffs, d), (1, 0, 2, 3)).reshape(N_DEV, 2 * ffs, d),
        w_ple_gate=jnp.transpose(jnp.stack([gb0["w_ple_gate"], gb1["w_ple_gate"]]).reshape(2, N_DEV, dshard, d), (1, 0, 2, 3)).reshape(N_DEV, 2 * dshard, d),
        w_ple_proj=jnp.transpose(jnp.stack([gb0["w_ple_proj"], gb1["w_ple_proj"]]).reshape(2, ple, N_DEV, dshard), (2, 0, 1, 3)).reshape(N_DEV, 2 * ple, dshard),
    )
    big_out = {}
    for k, g8 in big.items():
        recv = _all_to_all(g8, name="rs_" + k)
        shp = args[k].shape
        two_d = lambda a: a.reshape(-1, shp[-1])
        outs = _adamw(two_d(args[k]), recv, two_d(args["m_" + k]), two_d(args["v_" + k]), name="adamw_" + k)
        big_out[k] = [o.reshape(shp) for o in outs]

    names = ["norm_mix", "norm_mlp", "norm_ple", "w_in_e", "w_out_e", "hgrn_lb", "g_norm_a", "conv_w", "a_log", "dt_bias",
             "g_norm_b", "s5_a_re", "s5_a_im", "s5_b_re", "s5_b_im", "s5_c_re", "s5_c_im", "s5_d", "s5_log_dt", "w_glu",
             "b_glu", "w_out_o", "w_up", "w_down", "w_ple_gate", "w_ple_proj", "final_norm"]
    result = [loss, dx[None]]
    for j in range(4):
        result += [big_out[k][j] if k in big_out else sm_out[j][k] for k in names]
    return tuple(result)
```

```python
import functools
import math
import operator

import numpy as np
import jax
import jax.numpy as jnp
from jax import lax
from jax.experimental import pallas as pl
from jax.experimental.pallas import tpu as pltpu

F32 = jnp.float32
BF16 = jnp.bfloat16
MM_DTYPE = BF16
HI = lax.Precision.HIGHEST
MESH = pl.DeviceIdType.MESH

NORM_EPS = 1e-6
CHUNK = 64
HEAD = 128
CONV_WIDTH = 4
S5_GROUP = 16
S5_STATE = 64
S5_GB = 8
S5_HALF = S5_GB * S5_STATE
N_DEV = 8
HEADS_PER_STEP = 4
ADAM_LR, ADAM_B1, ADAM_B2, ADAM_EPS, ADAM_WD, ADAM_STEP = 0.001, 0.9, 0.999, 1e-08, 0.01, 10
VMEM_LIMIT = 56 * 1024 * 1024

NN = (((1,), (0,)), ((), ()))
NT = (((1,), (1,)), ((), ()))
TN = (((0,), (0,)), ((), ()))


def _dot(a, b, dn=NN):
    return lax.dot_general(a, b, dn, precision=HI, preferred_element_type=F32)


def _hdot(a, b, dn=NN):
    return lax.dot_general(a, b, dn, precision=lax.Precision.HIGH, preferred_element_type=F32)


def _bdot_raw(a, b, dn=NN):
    return lax.dot_general(a.astype(BF16), b.astype(BF16), dn, preferred_element_type=F32)


@functools.partial(jax.custom_vjp, nondiff_argnums=(2,))
def _bdot(a, b, dn):
    return _bdot_raw(a, b, dn)


def _bdot_fwd(a, b, dn):
    return _bdot_raw(a, b, dn), (a, b)


def _bdot_bwd(dn, res, g):
    a, b = res
    if dn == NN:
        return _bdot_raw(g, b, NT), _bdot_raw(a, g, TN)
    if dn == NT:
        return _bdot_raw(g, b, NN), _bdot_raw(g, a, TN)
    assert dn == TN
    return _bdot_raw(b, g, NT), _bdot_raw(a, g, NN)


_bdot.defvjp(_bdot_fwd, _bdot_bwd)


def _per_head(f):
    def g(*args, **kw):
        n = [len(a.vals) for a in args if isinstance(a, _Heads)]
        if not n:
            return f(*args, **kw)
        return _Heads([f(*[a.vals[j] if isinstance(a, _Heads) else a for a in args], **kw) for j in range(n[0])])
    return g


class _Heads:
    def __init__(self, vals):
        self.vals = list(vals)

    def __add__(self, o):
        return _per_head(operator.add)(self, o)

    def __radd__(self, o):
        return _per_head(operator.add)(o, self)

    def __sub__(self, o):
        return _per_head(operator.sub)(self, o)

    def __rsub__(self, o):
        return _per_head(operator.sub)(o, self)

    def __mul__(self, o):
        return _per_head(operator.mul)(self, o)

    def __rmul__(self, o):
        return _per_head(operator.mul)(o, self)

    def __neg__(self):
        return _per_head(operator.neg)(self)


_exp, _log, _where, _sum, _mean = (_per_head(f) for f in (jnp.exp, jnp.log, jnp.where, jnp.sum, jnp.mean))
_sigmoid, _rsqrt, _equal = _per_head(jax.nn.sigmoid), _per_head(lax.rsqrt), _per_head(operator.eq)
_hdot_h, _bdot_h = _per_head(_hdot), _per_head(_bdot)


def _params(n_axes):
    return pltpu.CompilerParams(dimension_semantics=("arbitrary",) * n_axes, vmem_limit_bytes=VMEM_LIMIT)


def _full_spec(a):
    nd = a.ndim
    return pl.BlockSpec(a.shape, lambda *_: (0,) * nd)


def _mm(a, b, *, name, ta=False, tb=False, extras=(), epilogue=None, out_dtypes=(F32,), tm=512, tn=1024, tk=2048):
    m = a.shape[1] if ta else a.shape[0]
    k = a.shape[0] if ta else a.shape[1]
    n = b.shape[0] if tb else b.shape[1]
    assert k == (b.shape[1] if tb else b.shape[0]), (name, a.shape, b.shape)
    tm, tn, tk = min(tm, m), min(tn, n), min(tk, k)
    assert m % tm == 0 and n % tn == 0 and k % tk == 0, (name, m, n, k)
    nk = k // tk
    n_ex, n_out = len(extras), len(out_dtypes)
    dn = (((0 if ta else 1,), (1 if tb else 0,)), ((), ()))

    def body(a_ref, b_ref, *rest):
        ex_refs, out_refs = rest[:n_ex], rest[n_ex:n_ex + n_out]
        part = lax.dot_general(a_ref[...].astype(MM_DTYPE), b_ref[...].astype(MM_DTYPE), dn, preferred_element_type=F32)

        def finish(acc):
            outs = epilogue(acc, *[r[...] for r in ex_refs]) if epilogue is not None else (acc,)
            for o_ref, o in zip(out_refs, outs):
                o_ref[...] = o.astype(o_ref.dtype)

        if nk == 1:
            finish(part)
            return
        acc_ref = rest[-1]
        kk = pl.program_id(2)

        @pl.when(kk == 0)
        def _():
            acc_ref[...] = part

        @pl.when((kk > 0) & (kk < nk - 1))
        def _():
            acc_ref[...] += part

        @pl.when(kk == nk - 1)
        def _():
            finish(acc_ref[...] + part)

    a_spec = pl.BlockSpec((tk, tm), lambda i, j, q: (q, i)) if ta else pl.BlockSpec((tm, tk), lambda i, j, q: (i, q))
    b_spec = pl.BlockSpec((tn, tk), lambda i, j, q: (j, q)) if tb else pl.BlockSpec((tk, tn), lambda i, j, q: (q, j))
    ex_specs = []
    for e in extras:
        if e.shape[0] == 1 and m != 1:
            ex_specs.append(pl.BlockSpec((1, tn), lambda i, j, q: (0, j)))
        else:
            ex_specs.append(pl.BlockSpec((tm, tn), lambda i, j, q: (i, j)))
    outs = pl.pallas_call(
        body, name=name, grid=(m // tm, n // tn, nk),
        in_specs=[a_spec, b_spec] + ex_specs,
        out_specs=[pl.BlockSpec((tm, tn), lambda i, j, q: (i, j)) for _ in out_dtypes],
        out_shape=[jax.ShapeDtypeStruct((m, n), dt) for dt in out_dtypes],
        scratch_shapes=[pltpu.VMEM((tm, tn), F32)] if nk > 1 else [],
        compiler_params=_params(3),
    )(a, b, *extras)
    return outs[0] if n_out == 1 else tuple(outs)


def _rows_call(fn, rows, consts, out_dtypes, *, name, tr=256):
    s = rows[0].shape[0]
    tr = min(tr, s)
    nr, nc = len(rows), len(consts)
    widths = [o.shape[1] for o in jax.eval_shape(
        fn, *[jax.ShapeDtypeStruct((tr, r.shape[1]), F32) for r in rows],
        *[jax.ShapeDtypeStruct(c.shape, F32) for c in consts])]

    def body(*refs):
        rv = [r[...].astype(F32) for r in refs[:nr]]
        cv = [c[...] for c in refs[nr:nr + nc]]
        for o_ref, o in zip(refs[nr + nc:], fn(*rv, *cv)):
            o_ref[...] = o.astype(o_ref.dtype)

    outs = pl.pallas_call(
        body, name=name, grid=(s // tr,),
        in_specs=[pl.BlockSpec((tr, r.shape[1]), lambda i: (i, 0)) for r in rows] + [_full_spec(c) for c in consts],
        out_specs=[pl.BlockSpec((tr, w), lambda i: (i, 0)) for w in widths],
        out_shape=[jax.ShapeDtypeStruct((s, w), dt) for w, dt in zip(widths, out_dtypes)],
        compiler_params=_params(1),
    )(*rows, *consts)
    return outs[0] if len(outs) == 1 else tuple(outs)


def _rows_vjp(fn, rows, consts, cots, *, name, row_grads, adds=None, tr=256):
    adds = adds or {}
    s = rows[0].shape[0]
    tr = min(tr, s)
    nr, nc, nt = len(rows), len(consts), len(cots)
    rg = sorted(row_grads)
    ad = sorted(adds)

    def body(*refs):
        rv = [r[...].astype(F32) for r in refs[:nr]]
        cv = [c[...] for c in refs[nr:nr + nc]]
        ct = [c[...].astype(F32) for c in refs[nr + nc:nr + nc + nt]]
        av = {i: r[...].astype(F32) for i, r in zip(ad, refs[nr + nc + nt:nr + nc + nt + len(ad)])}
        out_refs = refs[nr + nc + nt + len(ad):]
        _, vjp = jax.vjp(fn, *rv, *cv)
        grads = vjp(tuple(ct))
        for o_ref, i in zip(out_refs[:len(rg)], rg):
            g = grads[i]
            if i in av:
                g = g + av[i]
            o_ref[...] = g.astype(o_ref.dtype)

        @pl.when(pl.program_id(0) == 0)
        def _():
            for o_ref in out_refs[len(rg):]:
                o_ref[...] = jnp.zeros_like(o_ref)

        for o_ref, g in zip(out_refs[len(rg):], grads[nr:]):
            o_ref[...] += g

    row_spec = lambda a: pl.BlockSpec((tr, a.shape[1]), lambda i: (i, 0))
    outs = pl.pallas_call(
        body, name=name, grid=(s // tr,),
        in_specs=[row_spec(r) for r in rows] + [_full_spec(c) for c in consts] + [row_spec(c) for c in cots]
        + [row_spec(adds[i]) for i in ad],
        out_specs=[row_spec(rows[i]) for i in rg] + [_full_spec(c) for c in consts],
        out_shape=[jax.ShapeDtypeStruct(rows[i].shape, row_grads[i]) for i in rg]
        + [jax.ShapeDtypeStruct(c.shape, F32) for c in consts],
        compiler_params=_params(1),
    )(*rows, *consts, *cots, *[adds[i] for i in ad])
    return list(outs[:len(rg)]), list(outs[len(rg):])


def _small_call(fn, ins, *, name):
    shapes = jax.eval_shape(fn, *[jax.ShapeDtypeStruct(a.shape, F32) for a in ins])

    def body(*refs):
        for o_ref, o in zip(refs[len(ins):], fn(*[r[...] for r in refs[:len(ins)]])):
            o_ref[...] = o

    return pl.pallas_call(
        body, name=name, in_specs=[_full_spec(a) for a in ins],
        out_specs=[pl.BlockSpec(o.shape, functools.partial(lambda nd, *_: (0,) * nd, len(o.shape))) for o in shapes],
        out_shape=[jax.ShapeDtypeStruct(o.shape, F32) for o in shapes], grid=(1,),
        compiler_params=_params(1),
    )(*ins)


def _small_vjp(fn, ins, cots, *, name):
    def body(*refs):
        vals = [r[...] for r in refs[:len(ins)]]
        ct = [r[...] for r in refs[len(ins):len(ins) + len(cots)]]
        _, vjp = jax.vjp(fn, *vals)
        for o_ref, g in zip(refs[len(ins) + len(cots):], vjp(tuple(ct))):
            o_ref[...] = g

    return pl.pallas_call(
        body, name=name, in_specs=[_full_spec(a) for a in ins] + [_full_spec(c) for c in cots],
        out_specs=[_full_spec(a) for a in ins],
        out_shape=[jax.ShapeDtypeStruct(a.shape, F32) for a in ins], grid=(1,),
        compiler_params=_params(1),
    )(*ins, *cots)


def _rms(x, g):
    return x * _rsqrt(_mean(x * x, axis=-1, keepdims=True) + NORM_EPS) * g


def _rms_stage(x, g):
    return (_rms(x, g),)


def _silu(x):
    return x * _sigmoid(x)


def _softplus(x):
    return jnp.maximum(x, 0.0) + jnp.log1p(jnp.exp(-jnp.abs(x)))


def _gelu(x):
    return jax.nn.gelu(x, approximate=True)


def _gelu_stage(y):
    return (_gelu(y),)


def _glu_stage(y, gl_raw, b):
    return (_gelu(y) * jax.nn.sigmoid(gl_raw + b),)


def _ple_stage(h, gpre, pp):
    return (h + jax.nn.sigmoid(gpre) * pp,)


def _relu2_grad_epilogue(acc, up):
    return (acc * (2.0 * jnp.maximum(up, 0.0)),)


def _lb0_stage(x0, x1, x2):
    mx = jnp.maximum(jnp.maximum(x0, x1), x2)
    e0, e1, e2 = jnp.exp(x0 - mx), jnp.exp(x1 - mx), jnp.exp(x2 - mx)
    return (e0 / (e0 + e1 + e2),)


def _s5_prep_stage(a_re, a_im, log_dt, b_re, b_im, expand):
    step = jnp.exp(log_dt)
    mag = jnp.exp(a_re * step)
    lr = mag * jnp.cos(a_im * step)
    li = mag * jnp.sin(a_im * step)
    den = a_re * a_re + a_im * a_im
    cr = ((lr - 1.0) * a_re + li * a_im) / den
    ci = (li * a_re - (lr - 1.0) * a_im) / den
    cr_e, ci_e = _dot(cr, expand), _dot(ci, expand)
    return lr, li, cr_e * b_re - ci_e * b_im, cr_e * b_im + ci_e * b_re


def _loss_call(h, g, target, *, name, tr=256):
    s, d = h.shape
    tr = min(tr, s)

    def loss_fn(hv, gv, tv):
        err = _rms(hv, gv) - tv
        return 0.5 * jnp.sum(jnp.mean(err * err, axis=-1))

    def body(h_ref, g_ref, t_ref, dh_ref, dg_ref, loss_ref):
        val, (dh, dg) = jax.value_and_grad(loss_fn, argnums=(0, 1))(h_ref[...], g_ref[...], t_ref[...])
        dh_ref[...] = dh

        @pl.when(pl.program_id(0) == 0)
        def _():
            dg_ref[...] = jnp.zeros_like(dg_ref)
            loss_ref[...] = jnp.zeros_like(loss_ref)

        dg_ref[...] += dg
        loss_ref[...] += jnp.full(loss_ref.shape, val, F32)

    row = pl.BlockSpec((tr, d), lambda i: (i, 0))
    return pl.pallas_call(
        body, name=name, grid=(s // tr,),
        in_specs=[row, _full_spec(g), row],
        out_specs=[row, _full_spec(g), pl.BlockSpec((8, 128), lambda i: (0, 0))],
        out_shape=[jax.ShapeDtypeStruct((s, d), F32), jax.ShapeDtypeStruct(g.shape, F32),
                   jax.ShapeDtypeStruct((8, 128), F32)],
        compiler_params=_params(1),
    )(h, g, target)


def _hgrn_chunk(q, fp, iv, gp, lb, gn, st_t):
    c = CHUNK
    row = lax.broadcasted_iota(jnp.int32, (c, c), 0)
    col = lax.broadcasted_iota(jnp.int32, (c, c), 1)
    causal = row >= col
    fg = lb + (1.0 - lb) * _sigmoid(fp)
    k = 1.0 - fg
    lf = _log(fg)
    cum = _hdot_h(causal.astype(F32), lf, NN)
    first_half = (lax.broadcasted_iota(jnp.int32, (c, 1), 0) < c // 2).astype(F32)
    ref = _sum(lf * first_half, axis=0, keepdims=True)
    cend = _sum(lf, axis=0, keepdims=True)
    scores = _where(causal, _hdot_h(q * _exp(cum - ref), k * _exp(ref - cum), NT), 0.0)
    out = _bdot_h(scores, iv, NN) + _bdot_h(q * _exp(cum), st_t, NT)
    st_new = st_t * _exp(cend) + _bdot_h(iv, k * _exp(cend - cum), TN)
    res = _rms(out, gn) * _silu(gp)
    return res, st_new


def _hgrn_heads(qs, fs, ivs, gs, lbs, gn, sts):
    res, st_new = _hgrn_chunk(_Heads(qs), _Heads(fs), _Heads(ivs), _Heads(gs), _Heads(lbs), gn, _Heads(sts))
    return res.vals, st_new.vals


def _lanes(j):
    return slice(j * HEAD, (j + 1) * HEAD)


def _hgrn_fwd(proj, lb, gn, *, heads, name):
    s = proj.shape[0]
    n = s // CHUNK
    hpb = min(HEADS_PER_STEP, heads)
    assert heads % hpb == 0

    def body(q_ref, f_ref, i_ref, g_ref, lb_ref, gn_ref, o_ref, st_ref, state):
        @pl.when(pl.program_id(1) == 0)
        def _():
            state[...] = jnp.zeros_like(state)

        gnv = gn_ref[...]
        loaded = [(q_ref[:, _lanes(j)], f_ref[:, _lanes(j)], i_ref[:, _lanes(j)], g_ref[:, _lanes(j)],
                   lb_ref[:, _lanes(j)], state[j]) for j in range(hpb)]
        qs, fs, ivs, gs, lbs, sts = (list(t) for t in zip(*loaded))
        res, st_new = _hgrn_heads(qs, fs, ivs, gs, lbs, gnv, sts)
        for j in range(hpb):
            st_ref[j] = sts[j]
            o_ref[:, _lanes(j)] = res[j].astype(o_ref.dtype)
            state[j] = st_new[j]

    wide = hpb * HEAD
    blk = lambda off: pl.BlockSpec((CHUNK, wide), lambda h, c: (c, off // hpb + h))
    return pl.pallas_call(
        body, name=name, grid=(heads // hpb, n),
        in_specs=[blk(0), blk(heads), blk(2 * heads), blk(3 * heads),
                  pl.BlockSpec((1, wide), lambda h, c: (0, h)), pl.BlockSpec((1, HEAD), lambda h, c: (0, 0))],
        out_specs=[pl.BlockSpec((CHUNK, wide), lambda h, c: (c, h)),
                   pl.BlockSpec((hpb, None, HEAD, HEAD), lambda h, c: (h, c, 0, 0))],
        out_shape=[jax.ShapeDtypeStruct((s, heads * HEAD), BF16), jax.ShapeDtypeStruct((heads, n, HEAD, HEAD), F32)],
        scratch_shapes=[pltpu.VMEM((hpb, HEAD, HEAD), F32)],
        compiler_params=_params(2),
    )(proj, proj, proj, proj, lb, gn)


def _hgrn_bwd(proj, lb, gn, states, d_out, *, heads, name):
    s = proj.shape[0]
    n = s // CHUNK
    hpb = min(HEADS_PER_STEP, heads)

    def body(q_ref, f_ref, i_ref, g_ref, lb_ref, gn_ref, st_ref, do_ref,
             dq_ref, df_ref, di_ref, dg_ref, dlb_ref, dgn_ref, dstate):
        h, c = pl.program_id(0), pl.program_id(1)

        @pl.when(c == 0)
        def _():
            dstate[...] = jnp.zeros_like(dstate)
            dlb_ref[...] = jnp.zeros_like(dlb_ref)

        @pl.when((c == 0) & (h == 0))
        def _():
            dgn_ref[...] = jnp.zeros_like(dgn_ref)

        gnv = gn_ref[...]
        loaded = [(q_ref[:, _lanes(j)], f_ref[:, _lanes(j)], i_ref[:, _lanes(j)], g_ref[:, _lanes(j)],
                   lb_ref[:, _lanes(j)], st_ref[j], do_ref[:, _lanes(j)].astype(F32), dstate[j]) for j in range(hpb)]
        qs, fs, ivs, gs, lbs, sts, dos, dss = (list(t) for t in zip(*loaded))
        _, vjp = jax.vjp(_hgrn_heads, qs, fs, ivs, gs, lbs, gnv, sts)
        dqs, dfs, dis, dgs, dlbs, dgn_sum, dsts = vjp((dos, dss))
        for j in range(hpb):
            ln = _lanes(j)
            dq_ref[:, ln] = dqs[j].astype(dq_ref.dtype)
            df_ref[:, ln] = dfs[j].astype(df_ref.dtype)
            di_ref[:, ln] = dis[j].astype(di_ref.dtype)
            dg_ref[:, ln] = dgs[j].astype(dg_ref.dtype)
            dlb_ref[:, ln] += dlbs[j]
            dstate[j] = dsts[j]
        dgn_ref[...] += dgn_sum

    wide = hpb * HEAD
    rev = lambda off: pl.BlockSpec((CHUNK, wide), lambda h, c: (n - 1 - c, off // hpb + h))
    out_blk = pl.BlockSpec((CHUNK, wide), lambda h, c: (n - 1 - c, h))
    width = heads * HEAD
    return pl.pallas_call(
        body, name=name, grid=(heads // hpb, n),
        in_specs=[rev(0), rev(heads), rev(2 * heads), rev(3 * heads),
                  pl.BlockSpec((1, wide), lambda h, c: (0, h)), pl.BlockSpec((1, HEAD), lambda h, c: (0, 0)),
                  pl.BlockSpec((hpb, None, HEAD, HEAD), lambda h, c: (h, n - 1 - c, 0, 0)), out_blk],
        out_specs=[out_blk, out_blk, out_blk, out_blk,
                   pl.BlockSpec((1, wide), lambda h, c: (0, h)), pl.BlockSpec((1, HEAD), lambda h, c: (0, 0))],
        out_shape=[jax.ShapeDtypeStruct((s, width), BF16)] * 4
        + [jax.ShapeDtypeStruct((1, width), F32), jax.ShapeDtypeStruct((1, HEAD), F32)],
        scratch_shapes=[pltpu.VMEM((hpb, HEAD, HEAD), F32)],
        compiler_params=_params(2),
    )(proj, proj, proj, proj, lb, gn, states, d_out)


def _shift_rows(x, d, rowi):
    if d == 0:
        return x
    n = x.shape[0]
    rolled = pltpu.roll(x, d % n, 0)
    keep = rowi >= d if d > 0 else rowi < n + d
    return jnp.where(keep, rolled, 0.0)


def _conv_pre(x, w_ref, rowi):
    acc = None
    for j in range(CONV_WIDTH):
        term = w_ref[j:j + 1, :] * _shift_rows(x, CONV_WIDTH - 1 - j, rowi)
        acc = term if acc is None else acc + term
    return acc


def _conv_fwd(proj, w, *, col_off, name, cb=256):
    s = proj.shape[0]
    width = w.shape[1]
    cb = min(cb, width)

    def body(x_ref, w_ref, o_ref):
        rowi = lax.broadcasted_iota(jnp.int32, (s, cb), 0)
        o_ref[...] = _silu(_conv_pre(x_ref[...], w_ref, rowi))

    return pl.pallas_call(
        body, name=name, grid=(width // cb,),
        in_specs=[pl.BlockSpec((s, cb), lambda j: (0, col_off // cb + j)), pl.BlockSpec((CONV_WIDTH, cb), lambda j: (0, j))],
        out_specs=pl.BlockSpec((s, cb), lambda j: (0, j)),
        out_shape=jax.ShapeDtypeStruct((s, width), F32),
        compiler_params=_params(1),
    )(proj, w)


def _conv_bwd(proj, w, d_out, *, col_off, name, cb=256):
    s = proj.shape[0]
    width = w.shape[1]
    cb = min(cb, width)

    def body(x_ref, w_ref, do_ref, dx_ref, dw_ref):
        rowi = lax.broadcasted_iota(jnp.int32, (s, cb), 0)
        x = x_ref[...]
        pre = _conv_pre(x, w_ref, rowi)
        sg = jax.nn.sigmoid(pre)
        dpre = do_ref[...] * (sg + pre * sg * (1.0 - sg))
        dx = None
        for j in range(CONV_WIDTH):
            d = CONV_WIDTH - 1 - j
            term = w_ref[j:j + 1, :] * _shift_rows(dpre, -d, rowi)
            dx = term if dx is None else dx + term
            dw_ref[j:j + 1, :] = jnp.sum(dpre * _shift_rows(x, d, rowi), axis=0, keepdims=True)
        dx_ref[...] = dx.astype(dx_ref.dtype)

    return pl.pallas_call(
        body, name=name, grid=(width // cb,),
        in_specs=[pl.BlockSpec((s, cb), lambda j: (0, col_off // cb + j)), pl.BlockSpec((CONV_WIDTH, cb), lambda j: (0, j)),
                  pl.BlockSpec((s, cb), lambda j: (0, j))],
        out_specs=[pl.BlockSpec((s, cb), lambda j: (0, j)), pl.BlockSpec((CONV_WIDTH, cb), lambda j: (0, j))],
        out_shape=[jax.ShapeDtypeStruct((s, width), BF16), jax.ShapeDtypeStruct((CONV_WIDTH, width), F32)],
        compiler_params=_params(1),
    )(proj, w, d_out)


def _delta_chunk(h, heads, qr, kr, vr, ab, zp, alog, dtb, gn, st):
    c = CHUNK
    row = lax.broadcasted_iota(jnp.int32, (c, c), 0)
    col = lax.broadcasted_iota(jnp.int32, (c, c), 1)
    causal = row >= col
    strict = row > col
    lane = lax.broadcasted_iota(jnp.int32, (c, HEAD), 1)
    mine = _equal(h, lane)
    la_full = -jnp.exp(alog) * _softplus(ab + dtb)
    cum_full = _hdot(causal.astype(F32), la_full)
    cum = _sum(_where(mine, cum_full, 0.0), axis=1, keepdims=True)
    cend = _sum(_sum(_where(mine, la_full, 0.0), axis=1, keepdims=True), axis=0, keepdims=True)
    beta = _sum(_where(_equal(heads + h, lane), jax.nn.sigmoid(ab), 0.0), axis=1, keepdims=True)
    cum_row = _hdot_h(_where(mine, 1.0, 0.0), cum_full, NT)
    decay = _where(causal, _exp(_where(causal, cum - cum_row, 0.0)), 0.0)
    qn = qr * _rsqrt(_sum(qr * qr, axis=-1, keepdims=True) + NORM_EPS) * (HEAD ** -0.5)
    kn = kr * _rsqrt(_sum(kr * kr, axis=-1, keepdims=True) + NORM_EPS)
    kb = kn * beta
    lower = _where(strict, _bdot_h(kb, kn, NT) * decay, 0.0)
    inv = (row == col).astype(F32)
    lvl = 0
    while (1 << lvl) < c:
        same_pair = (row >> (lvl + 1)) == (col >> (lvl + 1))
        off_block = same_pair & (((row >> lvl) & 1) == 1) & (((col >> lvl) & 1) == 0)
        inv = inv - _hdot_h(_hdot_h(inv, _where(off_block, lower, 0.0), NN), inv, NN)
        lvl += 1
    ecum = _exp(cum)
    u = _hdot_h(inv, vr * beta, NN)
    w = _hdot_h(inv, kb * ecum, NN)
    intra = _bdot_h(qn, kn, NT) * decay
    v_new = u - _bdot_h(w, st, NN)
    out = _bdot_h(qn * ecum, st, NN) + _bdot_h(intra, v_new, NN)
    st_new = st * _exp(cend) + _bdot_h(kn * _exp(cend - cum), v_new, TN)
    res = _rms(out, gn) * _silu(zp)
    return res, st_new


def _delta_heads(hs, heads, qs, ks, vs, ab, zs, alog, dtb, gn, sts):
    res, st_new = _delta_chunk(_Heads(hs), heads, _Heads(qs), _Heads(ks), _Heads(vs), ab, _Heads(zs), alog, dtb, gn,
                               _Heads(sts))
    return res.vals, st_new.vals


def _delta_fwd(qkv, ab, proj, hp, gn, *, heads, z_off, name):
    s = qkv.shape[0]
    n = s // CHUNK

    hpb = min(HEADS_PER_STEP, heads)
    assert heads % hpb == 0 and z_off % hpb == 0

    def body(q_ref, k_ref, v_ref, ab_ref, z_ref, hp_ref, gn_ref, o_ref, st_ref, state):
        hb = pl.program_id(1)

        @pl.when(pl.program_id(0) == 0)
        def _():
            for j in range(hpb):
                state[hb * hpb + j] = jnp.zeros((HEAD, HEAD), F32)

        shared = (ab_ref[...], hp_ref[0:1, :], hp_ref[1:2, :], gn_ref[...])
        loaded = [(q_ref[:, _lanes(j)], k_ref[:, _lanes(j)], v_ref[:, _lanes(j)], z_ref[:, _lanes(j)],
                   state[hb * hpb + j]) for j in range(hpb)]
        qs, ks, vs, zs, sts = (list(t) for t in zip(*loaded))
        res, st_new = _delta_heads([hb * hpb + j for j in range(hpb)], heads, qs, ks, vs, shared[0], zs, shared[1],
                                   shared[2], shared[3], sts)
        for j in range(hpb):
            st_ref[j] = sts[j]
            o_ref[:, _lanes(j)] = res[j].astype(o_ref.dtype)
            state[hb * hpb + j] = st_new[j]

    wide = hpb * HEAD
    blk = lambda off: pl.BlockSpec((CHUNK, wide), lambda c, h: (c, off // hpb + h))
    return pl.pallas_call(
        body, name=name, grid=(n, heads // hpb),
        in_specs=[blk(0), blk(heads), blk(2 * heads), pl.BlockSpec((CHUNK, HEAD), lambda c, h: (c, 0)), blk(z_off),
                  pl.BlockSpec((8, HEAD), lambda c, h: (0, 0)), pl.BlockSpec((1, HEAD), lambda c, h: (0, 0))],
        out_specs=[pl.BlockSpec((CHUNK, wide), lambda c, h: (c, h)),
                   pl.BlockSpec((hpb, None, HEAD, HEAD), lambda c, h: (h, c, 0, 0))],
        out_shape=[jax.ShapeDtypeStruct((s, heads * HEAD), BF16), jax.ShapeDtypeStruct((heads, n, HEAD, HEAD), F32)],
        scratch_shapes=[pltpu.VMEM((heads, HEAD, HEAD), F32)],
        compiler_params=_params(2),
    )(qkv, qkv, qkv, ab, proj, hp, gn)


def _delta_bwd(qkv, ab, proj, hp, gn, states, d_out, *, heads, z_off, name):
    s = qkv.shape[0]
    n = s // CHUNK
    hpb = min(HEADS_PER_STEP, heads)

    def body(q_ref, k_ref, v_ref, ab_ref, z_ref, hp_ref, gn_ref, st_ref, do_ref,
             dq_ref, dk_ref, dv_ref, dab_ref, dz_ref, dhp_ref, dgn_ref, dstate):
        c, hb = pl.program_id(0), pl.program_id(1)

        @pl.when(c == 0)
        def _():
            for j in range(hpb):
                dstate[hb * hpb + j] = jnp.zeros((HEAD, HEAD), F32)

        @pl.when((c == 0) & (hb == 0))
        def _():
            dgn_ref[...] = jnp.zeros_like(dgn_ref)
            dhp_ref[...] = jnp.zeros_like(dhp_ref)

        @pl.when(hb == 0)
        def _():
            dab_ref[...] = jnp.zeros_like(dab_ref)

        shared = (ab_ref[...], hp_ref[0:1, :], hp_ref[1:2, :], gn_ref[...])
        loaded = [(q_ref[:, _lanes(j)], k_ref[:, _lanes(j)], v_ref[:, _lanes(j)], z_ref[:, _lanes(j)], st_ref[j],
                   do_ref[:, _lanes(j)].astype(F32), dstate[hb * hpb + j]) for j in range(hpb)]
        qs, ks, vs, zs, sts, dos, dss = (list(t) for t in zip(*loaded))
        fn = functools.partial(_delta_heads, [hb * hpb + j for j in range(hpb)], heads)
        _, vjp = jax.vjp(fn, qs, ks, vs, shared[0], zs, shared[1], shared[2], shared[3], sts)
        dqs, dks, dvs, dab, dzs, dal, ddt, dgn, dsts = vjp((dos, dss))
        for j in range(hpb):
            ln = _lanes(j)
            dq_ref[:, ln] = dqs[j]
            dk_ref[:, ln] = dks[j]
            dv_ref[:, ln] = dvs[j]
            dz_ref[:, ln] = dzs[j].astype(dz_ref.dtype)
            dstate[hb * hpb + j] = dsts[j]
        dab_ref[...] += dab
        dhp_ref[0:1, :] += dal
        dhp_ref[1:2, :] += ddt
        dgn_ref[...] += dgn

    wide = hpb * HEAD
    rev = lambda off: pl.BlockSpec((CHUNK, wide), lambda c, h: (n - 1 - c, off // hpb + h))
    width = heads * HEAD
    head_blk = pl.BlockSpec((CHUNK, wide), lambda c, h: (n - 1 - c, h))
    ab_blk = pl.BlockSpec((CHUNK, HEAD), lambda c, h: (n - 1 - c, 0))
    return pl.pallas_call(
        body, name=name, grid=(n, heads // hpb),
        in_specs=[rev(0), rev(heads), rev(2 * heads), ab_blk, rev(z_off),
                  pl.BlockSpec((8, HEAD), lambda c, h: (0, 0)), pl.BlockSpec((1, HEAD), lambda c, h: (0, 0)),
                  pl.BlockSpec((hpb, None, HEAD, HEAD), lambda c, h: (h, n - 1 - c, 0, 0)), head_blk],
        out_specs=[head_blk, head_blk, head_blk, ab_blk, head_blk,
                   pl.BlockSpec((8, HEAD), lambda c, h: (0, 0)), pl.BlockSpec((1, HEAD), lambda c, h: (0, 0))],
        out_shape=[jax.ShapeDtypeStruct((s, width), F32)] * 3
        + [jax.ShapeDtypeStruct((s, HEAD), F32), jax.ShapeDtypeStruct((s, width), BF16),
           jax.ShapeDtypeStruct((8, HEAD), F32), jax.ShapeDtypeStruct((1, HEAD), F32)],
        scratch_shapes=[pltpu.VMEM((heads, HEAD, HEAD), F32)],
        compiler_params=_params(2),
    )(qkv, qkv, qkv, ab, proj, hp, gn, states, d_out)


def _s5_scan(buf, lt_ref, cin_r, cin_i, tt, reverse):
    nblk = tt // 8
    hl = S5_HALF
    rowi = lax.broadcasted_iota(jnp.int32, (8, hl), 0)
    sign = -1.0 if reverse else 1.0

    def body(j, carry):
        cr, ci = carry
        off = pl.multiple_of((nblk - 1 - j if reverse else j) * 8, 8)
        xr = buf[pl.ds(off, 8), 0:hl]
        xi = buf[pl.ds(off, 8), hl:2 * hl]
        for lv, d in enumerate((1, 2, 4)):
            ar, ai = lt_ref[2 * lv], sign * lt_ref[2 * lv + 1]
            sr = _shift_rows(xr, -d if reverse else d, rowi)
            si = _shift_rows(xi, -d if reverse else d, rowi)
            xr, xi = xr + ar * sr - ai * si, xi + ar * si + ai * sr
        pr, pi = (lt_ref[8], -lt_ref[9]) if reverse else (lt_ref[6], lt_ref[7])
        xr, xi = xr + pr * cr - pi * ci, xi + pr * ci + pi * cr
        buf[pl.ds(off, 8), 0:hl] = xr
        buf[pl.ds(off, 8), hl:2 * hl] = xi
        edge = rowi == (0 if reverse else 7)
        return (jnp.sum(jnp.where(edge, xr, 0.0), axis=0, keepdims=True),
                jnp.sum(jnp.where(edge, xi, 0.0), axis=0, keepdims=True))

    return lax.fori_loop(0, nblk, body, (cin_r, cin_i))


def _s5_fwd(u, wb, wc, lt, dskip, *, name, tt=512):
    s, d = u.shape
    nb = d // HEAD
    tt = min(tt, s)
    nt = s // tt
    hl = S5_HALF

    def body(u_ref, wb_ref, wc_ref, lt_ref, d_ref, y_ref, cin_ref, buf, carry):
        @pl.when(pl.program_id(1) == 0)
        def _():
            carry[...] = jnp.zeros_like(carry)

        cin_ref[...] = carry[0:1, :]
        uv = u_ref[...]
        buf[...] = _bdot_raw(uv, wb_ref[...])
        cr, ci = _s5_scan(buf, lt_ref, carry[0:1, 0:hl], carry[0:1, hl:2 * hl], tt, False)
        carry[0:1, 0:hl] = cr
        carry[0:1, hl:2 * hl] = ci
        y_ref[...] = _bdot_raw(buf[...], wc_ref[...]) + d_ref[...] * uv

    return pl.pallas_call(
        body, name=name, grid=(nb, nt),
        in_specs=[pl.BlockSpec((tt, HEAD), lambda b, t: (t, b)),
                  pl.BlockSpec((None, HEAD, 2 * hl), lambda b, t: (b, 0, 0)),
                  pl.BlockSpec((None, 2 * hl, HEAD), lambda b, t: (b, 0, 0)),
                  pl.BlockSpec((None, 10, 8, hl), lambda b, t: (b, 0, 0, 0)),
                  pl.BlockSpec((1, HEAD), lambda b, t: (0, b))],
        out_specs=[pl.BlockSpec((tt, HEAD), lambda b, t: (t, b)),
                   pl.BlockSpec((None, None, 1, 2 * hl), lambda b, t: (b, t, 0, 0))],
        out_shape=[jax.ShapeDtypeStruct((s, d), F32), jax.ShapeDtypeStruct((nb, nt, 1, 2 * hl), F32)],
        scratch_shapes=[pltpu.VMEM((tt, 2 * hl), F32), pltpu.VMEM((8, 2 * hl), F32)],
        compiler_params=_params(2),
    )(u, wb, wc, lt, dskip)


def _s5_bwd(u, dy, wb, wc, lt, dskip, cins, *, name, tt=512):
    s, d = u.shape
    nb = d // HEAD
    tt = min(tt, s)
    nt = s // tt
    hl = S5_HALF

    def body(u_ref, dy_ref, wb_ref, wc_ref, lt_ref, d_ref, cin_ref,
             du_ref, dwb_ref, dwc_ref, dd_ref, dlam_ref, sbuf, abuf, acarry):
        @pl.when(pl.program_id(1) == 0)
        def _():
            acarry[...] = jnp.zeros_like(acarry)
            dwb_ref[...] = jnp.zeros_like(dwb_ref)
            dwc_ref[...] = jnp.zeros_like(dwc_ref)
            dd_ref[...] = jnp.zeros_like(dd_ref)
            dlam_ref[...] = jnp.zeros_like(dlam_ref)

        uv, dyv = u_ref[...], dy_ref[...]
        sbuf[...] = _bdot_raw(uv, wb_ref[...])
        _s5_scan(sbuf, lt_ref, cin_ref[:, 0:hl], cin_ref[:, hl:2 * hl], tt, False)
        abuf[...] = _bdot_raw(dyv, wc_ref[...], NT)
        ar, ai = _s5_scan(abuf, lt_ref, acarry[0:1, 0:hl], acarry[0:1, hl:2 * hl], tt, True)
        acarry[0:1, 0:hl] = ar
        acarry[0:1, hl:2 * hl] = ai
        du_ref[...] = _bdot_raw(abuf[...], wb_ref[...], NT) + d_ref[...] * dyv
        dwb_ref[...] += _bdot_raw(uv, abuf[...], TN)
        dwc_ref[...] += _bdot_raw(sbuf[...], dyv, TN)
        dd_ref[...] += jnp.sum(dyv * uv, axis=0, keepdims=True)
        first = lax.broadcasted_iota(jnp.int32, (tt, hl), 0) == 0
        spr = jnp.where(first, cin_ref[:, 0:hl], pltpu.roll(sbuf[:, 0:hl], 1, 0))
        spi = jnp.where(first, cin_ref[:, hl:2 * hl], pltpu.roll(sbuf[:, hl:2 * hl], 1, 0))
        avr, avi = abuf[:, 0:hl], abuf[:, hl:2 * hl]
        dlam_ref[:, 0:hl] += jnp.sum(avr * spr + avi * spi, axis=0, keepdims=True)
        dlam_ref[:, hl:2 * hl] += jnp.sum(avi * spr - avr * spi, axis=0, keepdims=True)

    rev = pl.BlockSpec((tt, HEAD), lambda b, t: (nt - 1 - t, b))
    return pl.pallas_call(
        body, name=name, grid=(nb, nt),
        in_specs=[rev, rev,
                  pl.BlockSpec((None, HEAD, 2 * hl), lambda b, t: (b, 0, 0)),
                  pl.BlockSpec((None, 2 * hl, HEAD), lambda b, t: (b, 0, 0)),
                  pl.BlockSpec((None, 10, 8, hl), lambda b, t: (b, 0, 0, 0)),
                  pl.BlockSpec((1, HEAD), lambda b, t: (0, b)),
                  pl.BlockSpec((None, None, 1, 2 * hl), lambda b, t: (b, nt - 1 - t, 0, 0))],
        out_specs=[rev,
                   pl.BlockSpec((None, HEAD, 2 * hl), lambda b, t: (b, 0, 0)),
                   pl.BlockSpec((None, 2 * hl, HEAD), lambda b, t: (b, 0, 0)),
                   pl.BlockSpec((1, HEAD), lambda b, t: (0, b)),
                   pl.BlockSpec((None, 1, 2 * hl), lambda b, t: (b, 0, 0))],
        out_shape=[jax.ShapeDtypeStruct((s, d), F32), jax.ShapeDtypeStruct(wb.shape, F32),
                   jax.ShapeDtypeStruct(wc.shape, F32), jax.ShapeDtypeStruct((1, d), F32),
                   jax.ShapeDtypeStruct((nb, 1, 2 * hl), F32)],
        scratch_shapes=[pltpu.VMEM((tt, 2 * hl), F32), pltpu.VMEM((tt, 2 * hl), F32), pltpu.VMEM((8, 2 * hl), F32)],
        compiler_params=_params(2),
    )(u, dy, wb, wc, lt, dskip, cins)


def _s5_pack(lr, li, br, bi, c_re, c_im):
    g = lr.shape[0]
    nb = g // S5_GB
    eye = jnp.eye(S5_GB, dtype=F32)
    bm = jnp.stack([br, bi]).reshape(2, nb, S5_GB, S5_STATE, S5_GROUP)
    wb = jnp.einsum("rbgpc,gh->bgcrhp", bm, eye).reshape(nb, HEAD, 2 * S5_HALF)
    cm = jnp.stack([c_re, -c_im]).reshape(2, nb, S5_GB, S5_GROUP, S5_STATE)
    wc = jnp.einsum("rbgcp,gh->brgphc", cm, eye).reshape(nb, 2 * S5_HALF, HEAD)
    pw = [(lr, li)]
    for _ in range(7):
        pr, pi = pw[-1]
        pw.append((pr * lr - pi * li, pr * li + pi * lr))
    blk = lambda a: a.reshape(nb, 1, S5_HALF)
    rows8 = lambda a: jnp.broadcast_to(blk(a), (nb, 8, S5_HALF))
    tables = []
    for n in (1, 2, 4):
        tables += [rows8(pw[n - 1][0]), rows8(pw[n - 1][1])]
    for order in (range(8), range(7, -1, -1)):
        tables += [jnp.concatenate([blk(pw[n][0]) for n in order], axis=1),
                   jnp.concatenate([blk(pw[n][1]) for n in order], axis=1)]
    return wb, wc, jnp.stack(tables, axis=1)


def _s5_unpack(dwb, dwc, dlam):
    nb = dwb.shape[0]
    g = nb * S5_GB
    eye = jnp.eye(S5_GB, dtype=F32)
    db = jnp.einsum("bgcrhp,gh->rbgpc", dwb.reshape(nb, S5_GB, S5_GROUP, 2, S5_GB, S5_STATE), eye)
    db = db.reshape(2, g, S5_STATE * S5_GROUP)
    dc = jnp.einsum("brgphc,gh->rbgcp", dwc.reshape(nb, 2, S5_GB, S5_STATE, S5_GB, S5_GROUP), eye)
    dc = dc.reshape(2, g, S5_GROUP, S5_STATE)
    dl = dlam.reshape(nb, 2, S5_GB, S5_STATE).transpose(1, 0, 2, 3).reshape(2, g, S5_STATE)
    return dl[0], dl[1], db[0], db[1], dc[0], -dc[1]


def _peer(r):
    mx, my, mc = lax.axis_index("x"), lax.axis_index("y"), lax.axis_index("c")
    px = 1 - mx if r & 4 else mx
    py = 1 - my if r & 2 else my
    pc = 1 - mc if r & 1 else mc
    return (px, py, pc), 4 * px + 2 * py + pc


_COMM_SCRATCH = [pltpu.SemaphoreType.DMA((N_DEV - 1,)), pltpu.SemaphoreType.DMA((N_DEV - 1,)), pltpu.SemaphoreType.DMA]


def _all_to_all(x, *, name):
    def body(x_ref, out_ref, send_sems, recv_sems, local_sem):
        _, me = _peer(0)
        mine = pltpu.make_async_copy(x_ref.at[me], out_ref.at[me], local_sem)
        mine.start()
        sends = []
        for r in range(1, N_DEV):
            pos, idx = _peer(r)
            sends.append(pltpu.make_async_remote_copy(
                src_ref=x_ref.at[idx], dst_ref=out_ref.at[me],
                send_sem=send_sems.at[r - 1], recv_sem=recv_sems.at[r - 1], device_id=pos, device_id_type=MESH))
            sends[-1].start()
        for r in range(1, N_DEV):
            pos, idx = _peer(r)
            pltpu.make_async_remote_copy(
                src_ref=x_ref.at[idx], dst_ref=out_ref.at[idx],
                send_sem=send_sems.at[r - 1], recv_sem=recv_sems.at[r - 1], device_id=pos, device_id_type=MESH).wait_recv()
        for cp in sends:
            cp.wait_send()
        mine.wait()

    return pl.pallas_call(
        body, name=name,
        in_specs=[pl.BlockSpec(memory_space=pl.ANY)], out_specs=pl.BlockSpec(memory_space=pl.ANY),
        out_shape=jax.ShapeDtypeStruct(x.shape, x.dtype), scratch_shapes=list(_COMM_SCRATCH),
    )(x)


def _all_gather(x, *, name):
    def body(x_ref, out_ref, send_sems, recv_sems, local_sem):
        mx, my, mc = lax.axis_index("x"), lax.axis_index("y"), lax.axis_index("c")
        me, sibling = (mx, my, mc), (mx, my, 1 - mc)
        chips = [(1 - mx, my), (mx, 1 - my), (1 - mx, 1 - my)]

        def slot(px, py, pc):
            return out_ref.at[4 * px + 2 * py + pc]

        def copy(k, block, to, src=None):
            return pltpu.make_async_remote_copy(
                src_ref=slot(*block) if src is None else src, dst_ref=slot(*block),
                send_sem=send_sems.at[k], recv_sem=recv_sems.at[k], device_id=to, device_id_type=MESH)

        mine = pltpu.make_async_copy(x_ref, slot(*me), local_sem)
        mine.start()
        first = [copy(0, me, sibling, src=x_ref)] + [copy(1 + j, me, (*chip, mc), src=x_ref) for j, chip in enumerate(chips)]
        for cp in first:
            cp.start()
        passed = [copy(4 + j, (*chip, mc), sibling) for j, chip in enumerate(chips)]
        for j, chip in enumerate(chips):
            copy(1 + j, (*chip, mc), me).wait_recv()
            passed[j].start()
        copy(0, sibling, me).wait_recv()
        for j, chip in enumerate(chips):
            copy(4 + j, (*chip, 1 - mc), me).wait_recv()
        for cp in first + passed:
            cp.wait_send()
        mine.wait()

    return pl.pallas_call(
        body, name=name,
        in_specs=[pl.BlockSpec(memory_space=pl.ANY)], out_specs=pl.BlockSpec(memory_space=pl.ANY),
        out_shape=jax.ShapeDtypeStruct((N_DEV,) + tuple(x.shape), x.dtype), scratch_shapes=list(_COMM_SCRATCH),
    )(x)


def _adamw(w, parts, m, v, *, name, tr=128):
    r, c = w.shape
    npart = parts.shape[0]
    tr = min(tr, r)
    assert r % tr == 0, (name, r)

    def body(w_ref, p_ref, m_ref, v_ref, g_ref, d_ref, mo_ref, vo_ref):
        g = p_ref[0].astype(F32)
        for k in range(1, npart):
            g = g + p_ref[k].astype(F32)
        m2 = ADAM_B1 * m_ref[...] + (1.0 - ADAM_B1) * g
        v2 = ADAM_B2 * v_ref[...] + (1.0 - ADAM_B2) * (g * g)
        m_hat = m2 / (1.0 - ADAM_B1 ** ADAM_STEP)
        v_hat = v2 / (1.0 - ADAM_B2 ** ADAM_STEP)
        g_ref[...] = g
        d_ref[...] = -ADAM_LR * (m_hat / (jnp.sqrt(v_hat) + ADAM_EPS) + ADAM_WD * w_ref[...])
        mo_ref[...] = m2
        vo_ref[...] = v2

    blk = pl.BlockSpec((tr, c), lambda i: (i, 0))
    return pl.pallas_call(
        body, name=name, grid=(r // tr,),
        in_specs=[blk, pl.BlockSpec((npart, tr, c), lambda i: (0, i, 0)), blk, blk],
        out_specs=[blk] * 4, out_shape=[jax.ShapeDtypeStruct((r, c), F32)] * 4,
        compiler_params=_params(1),
    )(w, parts, m, v)


def _sum_parts(parts, *, name):
    npart = parts.shape[0]

    def body(p_ref, o_ref):
        g = p_ref[0]
        for k in range(1, npart):
            g = g + p_ref[k]
        o_ref[...] = g

    return pl.pallas_call(
        body, name=name, grid=(1,), in_specs=[_full_spec(parts)],
        out_specs=pl.BlockSpec(parts.shape[1:], lambda i: (0, 0)),
        out_shape=jax.ShapeDtypeStruct(parts.shape[1:], F32), compiler_params=_params(1),
    )(parts)


def _pack(arrs):
    flat = jnp.concatenate([a.reshape(-1).astype(F32) for a in arrs])
    pad = (-flat.shape[0]) % (HEAD * HEAD)
    return jnp.pad(flat, (0, pad)).reshape(-1, HEAD)


def _unpack(packed, shapes):
    flat = packed.reshape(-1)
    out, off = [], 0
    for shp in shapes:
        size = math.prod(shp)
        out.append(flat[off:off + size].reshape(shp))
        off += size
    return out


def _add_epilogue(acc, res):
    return (acc + res,)


def _relu2_epilogue(acc):
    r = jnp.maximum(acc, 0.0)
    return acc, r * r


def _ple_epilogue(acc, gpre, h):
    return h + jax.nn.sigmoid(gpre) * acc, acc


def kernel(x, p, norm_mix, norm_mlp, norm_ple, w_in_e, w_out_e, hgrn_lb, g_norm_a, conv_w, a_log, dt_bias, g_norm_b, s5_a_re, s5_a_im, s5_b_re, s5_b_im, s5_c_re, s5_c_im, s5_d, s5_log_dt, w_glu, b_glu, w_out_o, w_up, w_down, w_ple_gate, w_ple_proj, final_norm, loss_target, m_norm_mix, m_norm_mlp, m_norm_ple, m_w_in_e, m_w_out_e, m_hgrn_lb, m_g_norm_a, m_conv_w, m_a_log, m_dt_bias, m_g_norm_b, m_s5_a_re, m_s5_a_im, m_s5_b_re, m_s5_b_im, m_s5_c_re, m_s5_c_im, m_s5_d, m_s5_log_dt, m_w_glu, m_b_glu, m_w_out_o, m_w_up, m_w_down, m_w_ple_gate, m_w_ple_proj, m_final_norm, v_norm_mix, v_norm_mlp, v_norm_ple, v_w_in_e, v_w_out_e, v_hgrn_lb, v_g_norm_a, v_conv_w, v_a_log, v_dt_bias, v_g_norm_b, v_s5_a_re, v_s5_a_im, v_s5_b_re, v_s5_b_im, v_s5_c_re, v_s5_c_im, v_s5_d, v_s5_log_dt, v_w_glu, v_b_glu, v_w_out_o, v_w_up, v_w_down, v_w_ple_gate, v_w_ple_proj, v_final_norm):
    args = dict(locals())
    s, d = x.shape[1], x.shape[2]
    aw = d // 2
    ha = hb = aw // HEAD
    main = 4 * d
    z_col = 2 * d + 3 * aw
    ff = w_up.shape[2] * N_DEV
    ple = p.shape[-1]
    groups = d // S5_GROUP
    me = 4 * lax.axis_index("x") + 2 * lax.axis_index("y") + lax.axis_index("c")
    x2, target = x[0], loss_target[0]
    row = lambda a, i: a[i:i + 1]

    def gathered(w, nm):
        return _all_gather(w.astype(BF16), name="ag_" + nm)

    w_in = jnp.transpose(gathered(w_in_e[0], "w_in"), (1, 0, 2)).reshape(d, -1)
    w_main = w_in[:, :main]
    w_tail = jnp.pad(w_in[:, main:], ((0, 0), (0, HEAD - 2 * hb)))
    w_oe = gathered(w_out_e[0], "w_out_e").reshape(d, d)
    w_top, w_bot = w_oe[:aw], w_oe[aw:]
    w_gl = gathered(w_glu[0], "w_glu").reshape(d, d)
    w_oo = gathered(w_out_o[0], "w_out_o").reshape(d, d)
    w_upg = jnp.transpose(gathered(w_up, "w_up"), (1, 2, 0, 3)).reshape(2, d, ff)
    w_dng = jnp.transpose(gathered(w_down, "w_down"), (1, 0, 2, 3)).reshape(2, ff, d)
    w_pgg = jnp.transpose(gathered(w_ple_gate, "w_ple_gate"), (1, 0, 2, 3)).reshape(2, d, d)
    w_ppg = jnp.transpose(gathered(w_ple_proj, "w_ple_proj"), (1, 2, 0, 3)).reshape(2, ple, d)
    shard_shapes = [conv_w[0].shape, s5_d.shape, b_glu.shape]
    small = _all_gather(_pack([conv_w[0], s5_d, b_glu]), name="ag_small")
    conv_g, s5d_g, bglu_g = zip(*[_unpack(small[j], shard_shapes) for j in range(N_DEV)])
    conv_full = jnp.concatenate(conv_g, axis=1)
    s5d_full = jnp.concatenate(s5d_g, axis=1)
    bglu_full = jnp.concatenate(bglu_g, axis=1)

    lb_rows = [row(hgrn_lb, 0), row(hgrn_lb, 1), row(hgrn_lb, 2)]
    (lb0,) = _small_call(_lb0_stage, lb_rows, name="f_lb0")
    hp = jnp.zeros((8, HEAD), F32).at[0, :hb].set(a_log[0]).at[1, :hb].set(dt_bias[0])
    expand = jnp.asarray(np.kron(np.eye(S5_STATE, dtype=np.float32), np.ones((1, S5_GROUP), np.float32)))
    prep_in = [s5_a_re[0], s5_a_im[0], s5_log_dt[0].reshape(groups, 1),
               s5_b_re[0].reshape(groups, -1), s5_b_im[0].reshape(groups, -1), expand]
    lr, li, br, bi = _small_call(_s5_prep_stage, prep_in, name="f_s5_prep")
    wb, wc, lt = _s5_pack(lr, li, br, bi, s5_c_re[0], s5_c_im[0])
    fnorm = final_norm.reshape(1, d)

    def block_fwd(h, l):
        hn = _rows_call(_rms_stage, [h], [row(norm_mlp, l)], [BF16], name=f"f_norm_mlp{l}")
        up, act = _mm(hn, w_upg[l], epilogue=_relu2_epilogue, out_dtypes=(F32, BF16), name=f"f_up{l}")
        h2 = _mm(act, w_dng[l], extras=(h,), epilogue=_add_epilogue, name=f"f_down{l}")
        hq = _rows_call(_rms_stage, [h2], [row(norm_ple, l)], [BF16], name=f"f_norm_ple{l}")
        gpre = _mm(hq, w_pgg[l], name=f"f_ple_gate{l}")
        h3, pp = _mm(p[l, 0], w_ppg[l], extras=(gpre, h2), epilogue=_ple_epilogue, out_dtypes=(F32, F32),
                     name=f"f_ple_proj{l}")
        return h3, dict(h=h, hn=hn, up=up, act=act, h2=h2, hq=hq, gpre=gpre, pp=pp)

    hn0 = _rows_call(_rms_stage, [x2], [row(norm_mix, 0)], [BF16], name="f_norm_mix0")
    proj = _mm(hn0, w_main, name="f_proj")
    ab = _mm(hn0, w_tail, name="f_ab")
    oa, st_a = _hgrn_fwd(proj, lb0, g_norm_a, heads=ha, name="f_hgrn")
    qkv = _conv_fwd(proj, conv_full, col_off=2 * d, name="f_conv")
    ob, st_b = _delta_fwd(qkv, ab, proj, hp, g_norm_b, heads=hb, z_off=z_col // HEAD, name="f_delta")
    h1 = _mm(oa, w_top, extras=(x2,), epilogue=_add_epilogue, name="f_out_a")
    h1 = _mm(ob, w_bot, extras=(h1,), epilogue=_add_epilogue, name="f_out_b")
    h3, sv0 = block_fwd(h1, 0)

    u = _rows_call(_rms_stage, [h3], [row(norm_mix, 1)], [F32], name="f_norm_mix1")
    y, cins = _s5_fwd(u, wb, wc, lt, s5d_full, name="f_s5")
    act_g = _rows_call(_gelu_stage, [y], [], [BF16], name="f_gelu")
    gl_raw = _mm(act_g, w_gl, name="f_glu")
    glu = _rows_call(_glu_stage, [y, gl_raw], [bglu_full], [BF16], name="f_glu_gate")
    h4 = _mm(glu, w_oo, extras=(h3,), epilogue=_add_epilogue, name="f_out_o")
    h6, sv1 = block_fwd(h4, 1)
    dh, d_fnorm, loss8 = _loss_call(h6, fnorm, target, name="loss")
    loss = lax.psum(loss8[0, 0], ("x", "y", "c"))

    def block_bwd(dh3, l, sv):
        (dgpre, dpp), _ = _rows_vjp(_ple_stage, [sv["h2"], sv["gpre"], sv["pp"]], [], [dh3],
                                    row_grads={1: BF16, 2: BF16}, name=f"b_ple{l}")
        g_pp = _mm(p[l, 0], dpp, ta=True, out_dtypes=(BF16,), name=f"b_w_ple_proj{l}")
        g_pg = _mm(sv["hq"], dgpre, ta=True, out_dtypes=(BF16,), name=f"b_w_ple_gate{l}")
        dhq = _mm(dgpre, w_pgg[l], tb=True, name=f"b_ple_gate{l}")
        (dh2,), (g_nple,) = _rows_vjp(_rms_stage, [sv["h2"]], [row(norm_ple, l)], [dhq], row_grads={0: F32},
                                      adds={0: dh3}, name=f"b_norm_ple{l}")
        dup = _mm(dh2, w_dng[l], tb=True, extras=(sv["up"],), epilogue=_relu2_grad_epilogue, out_dtypes=(BF16,),
                  name=f"b_down{l}")
        g_dn = _mm(sv["act"], dh2, ta=True, out_dtypes=(BF16,), name=f"b_w_down{l}")
        g_up = _mm(sv["hn"], dup, ta=True, out_dtypes=(BF16,), name=f"b_w_up{l}")
        dhn = _mm(dup, w_upg[l], tb=True, name=f"b_up{l}")
        (dh0,), (g_nmlp,) = _rows_vjp(_rms_stage, [sv["h"]], [row(norm_mlp, l)], [dhn], row_grads={0: F32},
                                      adds={0: dh2}, name=f"b_norm_mlp{l}")
        return dh0, dict(w_ple_proj=g_pp, w_ple_gate=g_pg, norm_ple=g_nple, w_down=g_dn, w_up=g_up, norm_mlp=g_nmlp)

    dh4, gb1 = block_bwd(dh, 1, sv1)
    dglu = _mm(dh4, w_oo, tb=True, name="b_out_o")
    g_oo = _mm(glu, dh4, ta=True, out_dtypes=(BF16,), name="b_w_out_o")
    (dy1, dgl), (g_bglu,) = _rows_vjp(_glu_stage, [y, gl_raw], [bglu_full], [dglu], row_grads={0: F32, 1: BF16},
                                      name="b_glu_gate")
    g_gl = _mm(act_g, dgl, ta=True, out_dtypes=(BF16,), name="b_w_glu")
    dact = _mm(dgl, w_gl, tb=True, name="b_glu")
    (dy,), _ = _rows_vjp(_gelu_stage, [y], [], [dact], row_grads={0: F32}, adds={0: dy1}, name="b_gelu")
    du, dwb, dwc, g_s5d, dlam = _s5_bwd(u, dy, wb, wc, lt, s5d_full, cins, name="b_s5")
    (dh3,), (g_nmix1,) = _rows_vjp(_rms_stage, [h3], [row(norm_mix, 1)], [du], row_grads={0: F32}, adds={0: dh4},
                                   name="b_norm_mix1")
    dlr, dli, dbr, dbi, g_cre, g_cim = _s5_unpack(dwb, dwc, dlam)
    g_are, g_aim, g_ldt, g_bre, g_bim, _ = _small_vjp(_s5_prep_stage, prep_in, [dlr, dli, dbr, dbi], name="b_s5_prep")

    dh1, gb0 = block_bwd(dh3, 0, sv0)
    doa = _mm(dh1, w_top, tb=True, name="b_out_a")
    dob = _mm(dh1, w_bot, tb=True, name="b_out_b")
    g_oe = jnp.concatenate([_mm(oa, dh1, ta=True, out_dtypes=(BF16,), name="b_w_out_a"),
                            _mm(ob, dh1, ta=True, out_dtypes=(BF16,), name="b_w_out_b")], axis=0)
    dq, df, di, dg, dlb, g_gna = _hgrn_bwd(proj, lb0, g_norm_a, st_a, doa, heads=ha, name="b_hgrn")
    dqb, dkb, dvb, dab, dz, dhp, g_gnb = _delta_bwd(qkv, ab, proj, hp, g_norm_b, st_b, dob, heads=hb,
                                                    z_off=z_col // HEAD, name="b_delta")
    dqkv, g_conv = _conv_bwd(proj, conv_full, jnp.concatenate([dqb, dkb, dvb], axis=1), col_off=2 * d, name="b_conv")
    dproj = jnp.concatenate([dq, df, di, dg, dqkv, dz], axis=1)
    dhn0 = _mm(dproj, w_main, tb=True, name="b_proj")
    dhn0 = _mm(dab, w_tail, tb=True, extras=(dhn0,), epilogue=_add_epilogue, name="b_ab")
    g_main = _mm(hn0, dproj, ta=True, out_dtypes=(BF16,), name="b_w_proj")
    g_tail = _mm(hn0, dab, ta=True, out_dtypes=(BF16,), name="b_w_ab")
    (dx,), (g_nmix0,) = _rows_vjp(_rms_stage, [x2], [row(norm_mix, 0)], [dhn0], row_grads={0: F32}, adds={0: dh1},
                                  name="b_norm_mix0")
    g_lb = jnp.concatenate(_small_vjp(_lb0_stage, lb_rows, [dlb], name="b_lb0"), axis=0)

    small_grads = dict(
        norm_mix=jnp.concatenate([g_nmix0, g_nmix1], axis=0),
        norm_mlp=jnp.concatenate([gb0["norm_mlp"], gb1["norm_mlp"]], axis=0),
        norm_ple=jnp.concatenate([gb0["norm_ple"], gb1["norm_ple"]], axis=0),
        hgrn_lb=g_lb, g_norm_a=g_gna, a_log=dhp[0:1, :hb], dt_bias=dhp[1:2, :hb], g_norm_b=g_gnb,
        s5_a_re=g_are[None], s5_a_im=g_aim[None], s5_b_re=g_bre.reshape(s5_b_re.shape),
        s5_b_im=g_bim.reshape(s5_b_im.shape), s5_c_re=g_cre[None], s5_c_im=g_cim[None],
        s5_log_dt=g_ldt.reshape(1, groups), final_norm=d_fnorm.reshape(d),
        conv_w=g_conv, s5_d=g_s5d, b_glu=g_bglu)
    rep_names = ["norm_mix", "norm_mlp", "norm_ple", "hgrn_lb", "g_norm_a", "a_log", "dt_bias", "g_norm_b", "s5_a_re",
                 "s5_a_im", "s5_b_re", "s5_b_im", "s5_c_re", "s5_c_im", "s5_log_dt", "final_norm"]
    full_names = rep_names + ["conv_w", "s5_d", "b_glu"]
    parts = _all_gather(_pack([small_grads[k] for k in full_names]), name="ag_small_grads")
    summed = _unpack(_sum_parts(parts, name="sum_small_grads"), [small_grads[k].shape for k in full_names])
    summed = dict(zip(full_names, summed))
    cw = conv_w.shape[2]
    dshard = d // N_DEV
    shard_g = dict(conv_w=lax.dynamic_slice(summed["conv_w"], (0, me * cw), (CONV_WIDTH, cw))[None],
                   s5_d=lax.dynamic_slice(summed["s5_d"], (0, me * dshard), (1, dshard)),
                   b_glu=lax.dynamic_slice(summed["b_glu"], (0, me * dshard), (1, dshard)))
    small_names = rep_names + ["conv_w", "s5_d", "b_glu"]
    g_small = [summed[k] if k in rep_names else shard_g[k] for k in small_names]
    shapes = [args[k].shape for k in small_names]
    sm_out = _adamw(_pack([args[k] for k in small_names]), _pack(g_small)[None], _pack([args["m_" + k] for k in small_names]),
                    _pack([args["v_" + k] for k in small_names]), name="adamw_small")
    sm_out = [dict(zip(small_names, _unpack(o, shapes))) for o in sm_out]

    cols = w_in_e.shape[2]
    ffs = ff // N_DEV
    big = dict(
        w_in_e=jnp.transpose(jnp.concatenate([g_main, g_tail[:, :2 * hb]], axis=1).reshape(d, N_DEV, cols), (1, 0, 2)),
        w_out_e=g_oe.reshape(N_DEV, dshard, d),
        w_glu=g_gl.reshape(N_DEV, dshard, d),
        w_out_o=g_oo.reshape(N_DEV, dshard, d),
        w_up=jnp.transpose(jnp.stack([gb0["w_up"], gb1["w_up"]]).reshape(2, d, N_DEV, ffs), (2, 0, 1, 3)).reshape(N_DEV, 2 * d, ffs),
        w_down=jnp.transpose(jnp.stack([gb0["w_down"], gb1["w_down"]]).reshape(2, N_DEV, ffs, d), (1, 0, 2, 3)).reshape(N_DEV, 2 * ffs, d),
        w_ple_gate=jnp.transpose(jnp.stack([gb0["w_ple_gate"], gb1["w_ple_gate"]]).reshape(2, N_DEV, dshard, d), (1, 0, 2, 3)).reshape(N_DEV, 2 * dshard, d),
        w_ple_proj=jnp.transpose(jnp.stack([gb0["w_ple_proj"], gb1["w_ple_proj"]]).reshape(2, ple, N_DEV, dshard), (2, 0, 1, 3)).reshape(N_DEV, 2 * ple, dshard),
    )
    big_out = {}
    for k, g8 in big.items():
        recv = _all_to_all(g8, name="rs_" + k)
        shp = args[k].shape
        two_d = lambda a: a.reshape(-1, shp[-1])
        outs = _adamw(two_d(args[k]), recv, two_d(args["m_" + k]), two_d(args["v_" + k]), name="adamw_" + k)
        big_out[k] = [o.reshape(shp) for o in outs]

    names = ["norm_mix", "norm_mlp", "norm_ple", "w_in_e", "w_out_e", "hgrn_lb", "g_norm_a", "conv_w", "a_log", "dt_bias",
             "g_norm_b", "s5_a_re", "s5_a_im", "s5_b_re", "s5_b_im", "s5_c_re", "s5_c_im", "s5_d", "s5_log_dt", "w_glu",
             "b_glu", "w_out_o", "w_up", "w_down", "w_ple_gate", "w_ple_proj", "final_norm"]
    result = [loss, dx[None]]
    for j in range(4):
        result += [big_out[k][j] if k in big_out else sm_out[j][k] for k in names]
    return tuple(result)
```

```python
import functools
import math
import operator

import numpy as np
import jax
import jax.numpy as jnp
from jax import lax
from jax.experimental import pallas as pl
from jax.experimental.pallas import tpu as pltpu

F32 = jnp.float32
BF16 = jnp.bfloat16
MM_DTYPE = BF16
HI = lax.Precision.HIGHEST
MESH = pl.DeviceIdType.MESH

NORM_EPS = 1e-6
CHUNK = 64
HEAD = 128
CONV_WIDTH = 4
S5_GROUP = 16
S5_STATE = 64
S5_GB = 8
S5_HALF = S5_GB * S5_STATE
N_DEV = 8
HEADS_PER_STEP = 4
ADAM_LR, ADAM_B1, ADAM_B2, ADAM_EPS, ADAM_WD, ADAM_STEP = 0.001, 0.9, 0.999, 1e-08, 0.01, 10
VMEM_LIMIT = 56 * 1024 * 1024

NN = (((1,), (0,)), ((), ()))
NT = (((1,), (1,)), ((), ()))
TN = (((0,), (0,)), ((), ()))


def _dot(a, b, dn=NN):
    return lax.dot_general(a, b, dn, precision=HI, preferred_element_type=F32)


def _hdot(a, b, dn=NN):
    return lax.dot_general(a, b, dn, precision=lax.Precision.HIGH, preferred_element_type=F32)


def _bdot_raw(a, b, dn=NN):
    return lax.dot_general(a.astype(BF16), b.astype(BF16), dn, preferred_element_type=F32)


@functools.partial(jax.custom_vjp, nondiff_argnums=(2,))
def _bdot(a, b, dn):
    return _bdot_raw(a, b, dn)


def _bdot_fwd(a, b, dn):
    return _bdot_raw(a, b, dn), (a, b)


def _bdot_bwd(dn, res, g):
    a, b = res
    if dn == NN:
        return _bdot_raw(g, b, NT), _bdot_raw(a, g, TN)
    if dn == NT:
        return _bdot_raw(g, b, NN), _bdot_raw(g, a, TN)
    assert dn == TN
    return _bdot_raw(b, g, NT), _bdot_raw(a, g, NN)


_bdot.defvjp(_bdot_fwd, _bdot_bwd)


def _per_head(f):
    def g(*args, **kw):
        n = [len(a.vals) for a in args if isinstance(a, _Heads)]
        if not n:
            return f(*args, **kw)
        return _Heads([f(*[a.vals[j] if isinstance(a, _Heads) else a for a in args], **kw) for j in range(n[0])])
    return g


class _Heads:
    def __init__(self, vals):
        self.vals = list(vals)

    def __add__(self, o):
        return _per_head(operator.add)(self, o)

    def __radd__(self, o):
        return _per_head(operator.add)(o, self)

    def __sub__(self, o):
        return _per_head(operator.sub)(self, o)

    def __rsub__(self, o):
        return _per_head(operator.sub)(o, self)

    def __mul__(self, o):
        return _per_head(operator.mul)(self, o)

    def __rmul__(self, o):
        return _per_head(operator.mul)(o, self)

    def __neg__(self):
        return _per_head(operator.neg)(self)


_exp, _log, _where, _sum, _mean = (_per_head(f) for f in (jnp.exp, jnp.log, jnp.where, jnp.sum, jnp.mean))
_sigmoid, _rsqrt, _equal = _per_head(jax.nn.sigmoid), _per_head(lax.rsqrt), _per_head(operator.eq)
_hdot_h, _bdot_h = _per_head(_hdot), _per_head(_bdot)


def _params(n_axes):
    return pltpu.CompilerParams(dimension_semantics=("arbitrary",) * n_axes, vmem_limit_bytes=VMEM_LIMIT)


def _full_spec(a):
    nd = a.ndim
    return pl.BlockSpec(a.shape, lambda *_: (0,) * nd)


def _mm(a, b, *, name, ta=False, tb=False, extras=(), epilogue=None, out_dtypes=(F32,), tm=512, tn=1024, tk=2048,
        comm=()):
    m = a.shape[1] if ta else a.shape[0]
    k = a.shape[0] if ta else a.shape[1]
    n = b.shape[0] if tb else b.shape[1]
    assert k == (b.shape[1] if tb else b.shape[0]), (name, a.shape, b.shape)
    tm, tn, tk = min(tm, m), min(tn, n), min(tk, k)
    assert m % tm == 0 and n % tn == 0 and k % tk == 0, (name, m, n, k)
    nk = k // tk
    n_ex, n_out = len(extras), len(out_dtypes)
    dn = (((0 if ta else 1,), (1 if tb else 0,)), ((), ()))

    def body(a_ref, b_ref, *rest):
        ex_refs, out_refs = rest[:n_ex], rest[n_ex:n_ex + n_out]
        part = lax.dot_general(a_ref[...].astype(MM_DTYPE), b_ref[...].astype(MM_DTYPE), dn, preferred_element_type=F32)

        def finish(acc):
            outs = epilogue(acc, *[r[...] for r in ex_refs]) if epilogue is not None else (acc,)
            for o_ref, o in zip(out_refs, outs):
                o_ref[...] = o.astype(o_ref.dtype)

        if nk == 1:
            finish(part)
            return
        acc_ref = rest[-1]
        kk = pl.program_id(2)

        @pl.when(kk == 0)
        def _():
            acc_ref[...] = part

        @pl.when((kk > 0) & (kk < nk - 1))
        def _():
            acc_ref[...] += part

        @pl.when(kk == nk - 1)
        def _():
            finish(acc_ref[...] + part)

    a_spec = pl.BlockSpec((tk, tm), lambda i, j, q: (q, i)) if ta else pl.BlockSpec((tm, tk), lambda i, j, q: (i, q))
    b_spec = pl.BlockSpec((tn, tk), lambda i, j, q: (j, q)) if tb else pl.BlockSpec((tk, tn), lambda i, j, q: (q, j))
    ex_specs = []
    for e in extras:
        if e.shape[0] == 1 and m != 1:
            ex_specs.append(pl.BlockSpec((1, tn), lambda i, j, q: (0, j)))
        else:
            ex_specs.append(pl.BlockSpec((tm, tn), lambda i, j, q: (i, j)))
    outs, exchanged = _call(
        body, (a, b, *extras), name=name, grid=(m // tm, n // tn, nk),
        in_specs=[a_spec, b_spec] + ex_specs,
        out_specs=[pl.BlockSpec((tm, tn), lambda i, j, q: (i, j)) for _ in out_dtypes],
        out_shape=[jax.ShapeDtypeStruct((m, n), dt) for dt in out_dtypes],
        scratch_shapes=[pltpu.VMEM((tm, tn), F32)] if nk > 1 else [], comm=comm)
    outs = outs[0] if n_out == 1 else tuple(outs)
    return (outs, exchanged) if comm else outs


def _rows_call(fn, rows, consts, out_dtypes, *, name, tr=256):
    s = rows[0].shape[0]
    tr = min(tr, s)
    nr, nc = len(rows), len(consts)
    widths = [o.shape[1] for o in jax.eval_shape(
        fn, *[jax.ShapeDtypeStruct((tr, r.shape[1]), F32) for r in rows],
        *[jax.ShapeDtypeStruct(c.shape, F32) for c in consts])]

    def body(*refs):
        rv = [r[...].astype(F32) for r in refs[:nr]]
        cv = [c[...] for c in refs[nr:nr + nc]]
        for o_ref, o in zip(refs[nr + nc:], fn(*rv, *cv)):
            o_ref[...] = o.astype(o_ref.dtype)

    outs = pl.pallas_call(
        body, name=name, grid=(s // tr,),
        in_specs=[pl.BlockSpec((tr, r.shape[1]), lambda i: (i, 0)) for r in rows] + [_full_spec(c) for c in consts],
        out_specs=[pl.BlockSpec((tr, w), lambda i: (i, 0)) for w in widths],
        out_shape=[jax.ShapeDtypeStruct((s, w), dt) for w, dt in zip(widths, out_dtypes)],
        compiler_params=_params(1),
    )(*rows, *consts)
    return outs[0] if len(outs) == 1 else tuple(outs)


def _rows_vjp(fn, rows, consts, cots, *, name, row_grads, adds=None, tr=256):
    adds = adds or {}
    s = rows[0].shape[0]
    tr = min(tr, s)
    nr, nc, nt = len(rows), len(consts), len(cots)
    rg = sorted(row_grads)
    ad = sorted(adds)

    def body(*refs):
        rv = [r[...].astype(F32) for r in refs[:nr]]
        cv = [c[...] for c in refs[nr:nr + nc]]
        ct = [c[...].astype(F32) for c in refs[nr + nc:nr + nc + nt]]
        av = {i: r[...].astype(F32) for i, r in zip(ad, refs[nr + nc + nt:nr + nc + nt + len(ad)])}
        out_refs = refs[nr + nc + nt + len(ad):]
        _, vjp = jax.vjp(fn, *rv, *cv)
        grads = vjp(tuple(ct))
        for o_ref, i in zip(out_refs[:len(rg)], rg):
            g = grads[i]
            if i in av:
                g = g + av[i]
            o_ref[...] = g.astype(o_ref.dtype)

        @pl.when(pl.program_id(0) == 0)
        def _():
            for o_ref in out_refs[len(rg):]:
                o_ref[...] = jnp.zeros_like(o_ref)

        for o_ref, g in zip(out_refs[len(rg):], grads[nr:]):
            o_ref[...] += g

    row_spec = lambda a: pl.BlockSpec((tr, a.shape[1]), lambda i: (i, 0))
    outs = pl.pallas_call(
        body, name=name, grid=(s // tr,),
        in_specs=[row_spec(r) for r in rows] + [_full_spec(c) for c in consts] + [row_spec(c) for c in cots]
        + [row_spec(adds[i]) for i in ad],
        out_specs=[row_spec(rows[i]) for i in rg] + [_full_spec(c) for c in consts],
        out_shape=[jax.ShapeDtypeStruct(rows[i].shape, row_grads[i]) for i in rg]
        + [jax.ShapeDtypeStruct(c.shape, F32) for c in consts],
        compiler_params=_params(1),
    )(*rows, *consts, *cots, *[adds[i] for i in ad])
    return list(outs[:len(rg)]), list(outs[len(rg):])


def _small_call(fn, ins, *, name):
    shapes = jax.eval_shape(fn, *[jax.ShapeDtypeStruct(a.shape, F32) for a in ins])

    def body(*refs):
        for o_ref, o in zip(refs[len(ins):], fn(*[r[...] for r in refs[:len(ins)]])):
            o_ref[...] = o

    return pl.pallas_call(
        body, name=name, in_specs=[_full_spec(a) for a in ins],
        out_specs=[pl.BlockSpec(o.shape, functools.partial(lambda nd, *_: (0,) * nd, len(o.shape))) for o in shapes],
        out_shape=[jax.ShapeDtypeStruct(o.shape, F32) for o in shapes], grid=(1,),
        compiler_params=_params(1),
    )(*ins)


def _small_vjp(fn, ins, cots, *, name):
    def body(*refs):
        vals = [r[...] for r in refs[:len(ins)]]
        ct = [r[...] for r in refs[len(ins):len(ins) + len(cots)]]
        _, vjp = jax.vjp(fn, *vals)
        for o_ref, g in zip(refs[len(ins) + len(cots):], vjp(tuple(ct))):
            o_ref[...] = g

    return pl.pallas_call(
        body, name=name, in_specs=[_full_spec(a) for a in ins] + [_full_spec(c) for c in cots],
        out_specs=[_full_spec(a) for a in ins],
        out_shape=[jax.ShapeDtypeStruct(a.shape, F32) for a in ins], grid=(1,),
        compiler_params=_params(1),
    )(*ins, *cots)


def _rms(x, g):
    return x * _rsqrt(_mean(x * x, axis=-1, keepdims=True) + NORM_EPS) * g


def _rms_stage(x, g):
    return (_rms(x, g),)


def _silu(x):
    return x * _sigmoid(x)


def _softplus(x):
    return jnp.maximum(x, 0.0) + jnp.log1p(jnp.exp(-jnp.abs(x)))


def _gelu(x):
    return jax.nn.gelu(x, approximate=True)


def _gelu_stage(y):
    return (_gelu(y),)


def _glu_stage(y, gl_raw, b):
    return (_gelu(y) * jax.nn.sigmoid(gl_raw + b),)


def _ple_stage(h, gpre, pp):
    return (h + jax.nn.sigmoid(gpre) * pp,)


def _relu2_grad_epilogue(acc, up):
    return (acc * (2.0 * jnp.maximum(up, 0.0)),)


def _lb0_stage(x0, x1, x2):
    mx = jnp.maximum(jnp.maximum(x0, x1), x2)
    e0, e1, e2 = jnp.exp(x0 - mx), jnp.exp(x1 - mx), jnp.exp(x2 - mx)
    return (e0 / (e0 + e1 + e2),)


def _s5_prep_stage(a_re, a_im, log_dt, b_re, b_im, expand):
    step = jnp.exp(log_dt)
    mag = jnp.exp(a_re * step)
    lr = mag * jnp.cos(a_im * step)
    li = mag * jnp.sin(a_im * step)
    den = a_re * a_re + a_im * a_im
    cr = ((lr - 1.0) * a_re + li * a_im) / den
    ci = (li * a_re - (lr - 1.0) * a_im) / den
    cr_e, ci_e = _dot(cr, expand), _dot(ci, expand)
    return lr, li, cr_e * b_re - ci_e * b_im, cr_e * b_im + ci_e * b_re


def _loss_call(h, g, target, *, name, tr=256):
    s, d = h.shape
    tr = min(tr, s)

    def loss_fn(hv, gv, tv):
        err = _rms(hv, gv) - tv
        return 0.5 * jnp.sum(jnp.mean(err * err, axis=-1))

    def body(h_ref, g_ref, t_ref, dh_ref, dg_ref, loss_ref):
        val, (dh, dg) = jax.value_and_grad(loss_fn, argnums=(0, 1))(h_ref[...], g_ref[...], t_ref[...])
        dh_ref[...] = dh

        @pl.when(pl.program_id(0) == 0)
        def _():
            dg_ref[...] = jnp.zeros_like(dg_ref)
            loss_ref[...] = jnp.zeros_like(loss_ref)

        dg_ref[...] += dg
        loss_ref[...] += jnp.full(loss_ref.shape, val, F32)

    row = pl.BlockSpec((tr, d), lambda i: (i, 0))
    return pl.pallas_call(
        body, name=name, grid=(s // tr,),
        in_specs=[row, _full_spec(g), row],
        out_specs=[row, _full_spec(g), pl.BlockSpec((8, 128), lambda i: (0, 0))],
        out_shape=[jax.ShapeDtypeStruct((s, d), F32), jax.ShapeDtypeStruct(g.shape, F32),
                   jax.ShapeDtypeStruct((8, 128), F32)],
        compiler_params=_params(1),
    )(h, g, target)


def _hgrn_chunk(q, fp, iv, gp, lb, gn, st_t):
    c = CHUNK
    row = lax.broadcasted_iota(jnp.int32, (c, c), 0)
    col = lax.broadcasted_iota(jnp.int32, (c, c), 1)
    causal = row >= col
    fg = lb + (1.0 - lb) * _sigmoid(fp)
    k = 1.0 - fg
    lf = _log(fg)
    cum = _hdot_h(causal.astype(F32), lf, NN)
    first_half = (lax.broadcasted_iota(jnp.int32, (c, 1), 0) < c // 2).astype(F32)
    ref = _sum(lf * first_half, axis=0, keepdims=True)
    cend = _sum(lf, axis=0, keepdims=True)
    scores = _where(causal, _hdot_h(q * _exp(cum - ref), k * _exp(ref - cum), NT), 0.0)
    out = _bdot_h(scores, iv, NN) + _bdot_h(q * _exp(cum), st_t, NT)
    st_new = st_t * _exp(cend) + _bdot_h(iv, k * _exp(cend - cum), TN)
    res = _rms(out, gn) * _silu(gp)
    return res, st_new


def _hgrn_heads(qs, fs, ivs, gs, lbs, gn, sts):
    res, st_new = _hgrn_chunk(_Heads(qs), _Heads(fs), _Heads(ivs), _Heads(gs), _Heads(lbs), gn, _Heads(sts))
    return res.vals, st_new.vals


def _lanes(j):
    return slice(j * HEAD, (j + 1) * HEAD)


def _hgrn_fwd(proj, lb, gn, *, heads, name, comm=()):
    s = proj.shape[0]
    n = s // CHUNK
    hpb = min(HEADS_PER_STEP, heads)
    assert heads % hpb == 0

    def body(q_ref, f_ref, i_ref, g_ref, lb_ref, gn_ref, o_ref, st_ref, state):
        @pl.when(pl.program_id(1) == 0)
        def _():
            state[...] = jnp.zeros_like(state)

        gnv = gn_ref[...]
        loaded = [(q_ref[:, _lanes(j)], f_ref[:, _lanes(j)], i_ref[:, _lanes(j)], g_ref[:, _lanes(j)],
                   lb_ref[:, _lanes(j)], state[j]) for j in range(hpb)]
        qs, fs, ivs, gs, lbs, sts = (list(t) for t in zip(*loaded))
        res, st_new = _hgrn_heads(qs, fs, ivs, gs, lbs, gnv, sts)
        for j in range(hpb):
            st_ref[j] = sts[j]
            o_ref[:, _lanes(j)] = res[j].astype(o_ref.dtype)
            state[j] = st_new[j]

    wide = hpb * HEAD
    blk = lambda off: pl.BlockSpec((CHUNK, wide), lambda h, c: (c, off // hpb + h))
    return _call(
        body, (proj, proj, proj, proj, lb, gn), name=name, grid=(heads // hpb, n),
        in_specs=[blk(0), blk(heads), blk(2 * heads), blk(3 * heads),
                  pl.BlockSpec((1, wide), lambda h, c: (0, h)), pl.BlockSpec((1, HEAD), lambda h, c: (0, 0))],
        out_specs=[pl.BlockSpec((CHUNK, wide), lambda h, c: (c, h)),
                   pl.BlockSpec((hpb, None, HEAD, HEAD), lambda h, c: (h, c, 0, 0))],
        out_shape=[jax.ShapeDtypeStruct((s, heads * HEAD), BF16), jax.ShapeDtypeStruct((heads, n, HEAD, HEAD), F32)],
        scratch_shapes=[pltpu.VMEM((hpb, HEAD, HEAD), F32)], comm=comm)


def _hgrn_bwd(proj, lb, gn, states, d_out, *, heads, name):
    s = proj.shape[0]
    n = s // CHUNK
    hpb = min(HEADS_PER_STEP, heads)

    def body(q_ref, f_ref, i_ref, g_ref, lb_ref, gn_ref, st_ref, do_ref,
             dq_ref, df_ref, di_ref, dg_ref, dlb_ref, dgn_ref, dstate):
        h, c = pl.program_id(0), pl.program_id(1)

        @pl.when(c == 0)
        def _():
            dstate[...] = jnp.zeros_like(dstate)
            dlb_ref[...] = jnp.zeros_like(dlb_ref)

        @pl.when((c == 0) & (h == 0))
        def _():
            dgn_ref[...] = jnp.zeros_like(dgn_ref)

        gnv = gn_ref[...]
        loaded = [(q_ref[:, _lanes(j)], f_ref[:, _lanes(j)], i_ref[:, _lanes(j)], g_ref[:, _lanes(j)],
                   lb_ref[:, _lanes(j)], st_ref[j], do_ref[:, _lanes(j)].astype(F32), dstate[j]) for j in range(hpb)]
        qs, fs, ivs, gs, lbs, sts, dos, dss = (list(t) for t in zip(*loaded))
        _, vjp = jax.vjp(_hgrn_heads, qs, fs, ivs, gs, lbs, gnv, sts)
        dqs, dfs, dis, dgs, dlbs, dgn_sum, dsts = vjp((dos, dss))
        for j in range(hpb):
            ln = _lanes(j)
            dq_ref[:, ln] = dqs[j].astype(dq_ref.dtype)
            df_ref[:, ln] = dfs[j].astype(df_ref.dtype)
            di_ref[:, ln] = dis[j].astype(di_ref.dtype)
            dg_ref[:, ln] = dgs[j].astype(dg_ref.dtype)
            dlb_ref[:, ln] += dlbs[j]
            dstate[j] = dsts[j]
        dgn_ref[...] += dgn_sum

    wide = hpb * HEAD
    rev = lambda off: pl.BlockSpec((CHUNK, wide), lambda h, c: (n - 1 - c, off // hpb + h))
    out_blk = pl.BlockSpec((CHUNK, wide), lambda h, c: (n - 1 - c, h))
    width = heads * HEAD
    return pl.pallas_call(
        body, name=name, grid=(heads // hpb, n),
        in_specs=[rev(0), rev(heads), rev(2 * heads), rev(3 * heads),
                  pl.BlockSpec((1, wide), lambda h, c: (0, h)), pl.BlockSpec((1, HEAD), lambda h, c: (0, 0)),
                  pl.BlockSpec((hpb, None, HEAD, HEAD), lambda h, c: (h, n - 1 - c, 0, 0)), out_blk],
        out_specs=[out_blk, out_blk, out_blk, out_blk,
                   pl.BlockSpec((1, wide), lambda h, c: (0, h)), pl.BlockSpec((1, HEAD), lambda h, c: (0, 0))],
        out_shape=[jax.ShapeDtypeStruct((s, width), BF16)] * 4
        + [jax.ShapeDtypeStruct((1, width), F32), jax.ShapeDtypeStruct((1, HEAD), F32)],
        scratch_shapes=[pltpu.VMEM((hpb, HEAD, HEAD), F32)],
        compiler_params=_params(2),
    )(proj, proj, proj, proj, lb, gn, states, d_out)


def _shift_rows(x, d, rowi):
    if d == 0:
        return x
    n = x.shape[0]
    rolled = pltpu.roll(x, d % n, 0)
    keep = rowi >= d if d > 0 else rowi < n + d
    return jnp.where(keep, rolled, 0.0)


def _conv_pre(x, w_ref, rowi):
    acc = None
    for j in range(CONV_WIDTH):
        term = w_ref[j:j + 1, :] * _shift_rows(x, CONV_WIDTH - 1 - j, rowi)
        acc = term if acc is None else acc + term
    return acc


def _conv_fwd(proj, w, *, col_off, name, cb=256):
    s = proj.shape[0]
    width = w.shape[1]
    cb = min(cb, width)

    def body(x_ref, w_ref, o_ref):
        rowi = lax.broadcasted_iota(jnp.int32, (s, cb), 0)
        o_ref[...] = _silu(_conv_pre(x_ref[...], w_ref, rowi))

    return pl.pallas_call(
        body, name=name, grid=(width // cb,),
        in_specs=[pl.BlockSpec((s, cb), lambda j: (0, col_off // cb + j)), pl.BlockSpec((CONV_WIDTH, cb), lambda j: (0, j))],
        out_specs=pl.BlockSpec((s, cb), lambda j: (0, j)),
        out_shape=jax.ShapeDtypeStruct((s, width), F32),
        compiler_params=_params(1),
    )(proj, w)


def _conv_bwd(proj, w, d_out, *, col_off, name, cb=256):
    s = proj.shape[0]
    width = w.shape[1]
    cb = min(cb, width)

    def body(x_ref, w_ref, do_ref, dx_ref, dw_ref):
        rowi = lax.broadcasted_iota(jnp.int32, (s, cb), 0)
        x = x_ref[...]
        pre = _conv_pre(x, w_ref, rowi)
        sg = jax.nn.sigmoid(pre)
        dpre = do_ref[...] * (sg + pre * sg * (1.0 - sg))
        dx = None
        for j in range(CONV_WIDTH):
            d = CONV_WIDTH - 1 - j
            term = w_ref[j:j + 1, :] * _shift_rows(dpre, -d, rowi)
            dx = term if dx is None else dx + term
            dw_ref[j:j + 1, :] = jnp.sum(dpre * _shift_rows(x, d, rowi), axis=0, keepdims=True)
        dx_ref[...] = dx.astype(dx_ref.dtype)

    return pl.pallas_call(
        body, name=name, grid=(width // cb,),
        in_specs=[pl.BlockSpec((s, cb), lambda j: (0, col_off // cb + j)), pl.BlockSpec((CONV_WIDTH, cb), lambda j: (0, j)),
                  pl.BlockSpec((s, cb), lambda j: (0, j))],
        out_specs=[pl.BlockSpec((s, cb), lambda j: (0, j)), pl.BlockSpec((CONV_WIDTH, cb), lambda j: (0, j))],
        out_shape=[jax.ShapeDtypeStruct((s, width), BF16), jax.ShapeDtypeStruct((CONV_WIDTH, width), F32)],
        compiler_params=_params(1),
    )(proj, w, d_out)


def _delta_chunk(h, heads, qr, kr, vr, ab, zp, alog, dtb, gn, st):
    c = CHUNK
    row = lax.broadcasted_iota(jnp.int32, (c, c), 0)
    col = lax.broadcasted_iota(jnp.int32, (c, c), 1)
    causal = row >= col
    strict = row > col
    lane = lax.broadcasted_iota(jnp.int32, (c, HEAD), 1)
    mine = _equal(h, lane)
    la_full = -jnp.exp(alog) * _softplus(ab + dtb)
    cum_full = _hdot(causal.astype(F32), la_full)
    cum = _sum(_where(mine, cum_full, 0.0), axis=1, keepdims=True)
    cend = _sum(_sum(_where(mine, la_full, 0.0), axis=1, keepdims=True), axis=0, keepdims=True)
    beta = _sum(_where(_equal(heads + h, lane), jax.nn.sigmoid(ab), 0.0), axis=1, keepdims=True)
    cum_row = _hdot_h(_where(mine, 1.0, 0.0), cum_full, NT)
    decay = _where(causal, _exp(_where(causal, cum - cum_row, 0.0)), 0.0)
    qn = qr * _rsqrt(_sum(qr * qr, axis=-1, keepdims=True) + NORM_EPS) * (HEAD ** -0.5)
    kn = kr * _rsqrt(_sum(kr * kr, axis=-1, keepdims=True) + NORM_EPS)
    kb = kn * beta
    lower = _where(strict, _bdot_h(kb, kn, NT) * decay, 0.0)
    inv = (row == col).astype(F32)
    lvl = 0
    while (1 << lvl) < c:
        same_pair = (row >> (lvl + 1)) == (col >> (lvl + 1))
        off_block = same_pair & (((row >> lvl) & 1) == 1) & (((col >> lvl) & 1) == 0)
        inv = inv - _hdot_h(_hdot_h(inv, _where(off_block, lower, 0.0), NN), inv, NN)
        lvl += 1
    ecum = _exp(cum)
    u = _hdot_h(inv, vr * beta, NN)
    w = _hdot_h(inv, kb * ecum, NN)
    intra = _bdot_h(qn, kn, NT) * decay
    v_new = u - _bdot_h(w, st, NN)
    out = _bdot_h(qn * ecum, st, NN) + _bdot_h(intra, v_new, NN)
    st_new = st * _exp(cend) + _bdot_h(kn * _exp(cend - cum), v_new, TN)
    res = _rms(out, gn) * _silu(zp)
    return res, st_new


def _delta_heads(hs, heads, qs, ks, vs, ab, zs, alog, dtb, gn, sts):
    res, st_new = _delta_chunk(_Heads(hs), heads, _Heads(qs), _Heads(ks), _Heads(vs), ab, _Heads(zs), alog, dtb, gn,
                               _Heads(sts))
    return res.vals, st_new.vals


def _delta_fwd(qkv, ab, proj, hp, gn, *, heads, z_off, name, comm=()):
    s = qkv.shape[0]
    n = s // CHUNK

    hpb = min(HEADS_PER_STEP, heads)
    assert heads % hpb == 0 and z_off % hpb == 0

    def body(q_ref, k_ref, v_ref, ab_ref, z_ref, hp_ref, gn_ref, o_ref, st_ref, state):
        hb = pl.program_id(1)

        @pl.when(pl.program_id(0) == 0)
        def _():
            for j in range(hpb):
                state[hb * hpb + j] = jnp.zeros((HEAD, HEAD), F32)

        shared = (ab_ref[...], hp_ref[0:1, :], hp_ref[1:2, :], gn_ref[...])
        loaded = [(q_ref[:, _lanes(j)], k_ref[:, _lanes(j)], v_ref[:, _lanes(j)], z_ref[:, _lanes(j)],
                   state[hb * hpb + j]) for j in range(hpb)]
        qs, ks, vs, zs, sts = (list(t) for t in zip(*loaded))
        res, st_new = _delta_heads([hb * hpb + j for j in range(hpb)], heads, qs, ks, vs, shared[0], zs, shared[1],
                                   shared[2], shared[3], sts)
        for j in range(hpb):
            st_ref[j] = sts[j]
            o_ref[:, _lanes(j)] = res[j].astype(o_ref.dtype)
            state[hb * hpb + j] = st_new[j]

    wide = hpb * HEAD
    blk = lambda off: pl.BlockSpec((CHUNK, wide), lambda c, h: (c, off // hpb + h))
    return _call(
        body, (qkv, qkv, qkv, ab, proj, hp, gn), name=name, grid=(n, heads // hpb),
        in_specs=[blk(0), blk(heads), blk(2 * heads), pl.BlockSpec((CHUNK, HEAD), lambda c, h: (c, 0)), blk(z_off),
                  pl.BlockSpec((8, HEAD), lambda c, h: (0, 0)), pl.BlockSpec((1, HEAD), lambda c, h: (0, 0))],
        out_specs=[pl.BlockSpec((CHUNK, wide), lambda c, h: (c, h)),
                   pl.BlockSpec((hpb, None, HEAD, HEAD), lambda c, h: (h, c, 0, 0))],
        out_shape=[jax.ShapeDtypeStruct((s, heads * HEAD), BF16), jax.ShapeDtypeStruct((heads, n, HEAD, HEAD), F32)],
        scratch_shapes=[pltpu.VMEM((heads, HEAD, HEAD), F32)], comm=comm)


def _delta_bwd(qkv, ab, proj, hp, gn, states, d_out, *, heads, z_off, name, comm=()):
    s = qkv.shape[0]
    n = s // CHUNK
    hpb = min(HEADS_PER_STEP, heads)

    def body(q_ref, k_ref, v_ref, ab_ref, z_ref, hp_ref, gn_ref, st_ref, do_ref,
             dq_ref, dk_ref, dv_ref, dab_ref, dz_ref, dhp_ref, dgn_ref, dstate):
        c, hb = pl.program_id(0), pl.program_id(1)

        @pl.when(c == 0)
        def _():
            for j in range(hpb):
                dstate[hb * hpb + j] = jnp.zeros((HEAD, HEAD), F32)

        @pl.when((c == 0) & (hb == 0))
        def _():
            dgn_ref[...] = jnp.zeros_like(dgn_ref)
            dhp_ref[...] = jnp.zeros_like(dhp_ref)

        @pl.when(hb == 0)
        def _():
            dab_ref[...] = jnp.zeros_like(dab_ref)

        shared = (ab_ref[...], hp_ref[0:1, :], hp_ref[1:2, :], gn_ref[...])
        loaded = [(q_ref[:, _lanes(j)], k_ref[:, _lanes(j)], v_ref[:, _lanes(j)], z_ref[:, _lanes(j)], st_ref[j],
                   do_ref[:, _lanes(j)].astype(F32), dstate[hb * hpb + j]) for j in range(hpb)]
        qs, ks, vs, zs, sts, dos, dss = (list(t) for t in zip(*loaded))
        fn = functools.partial(_delta_heads, [hb * hpb + j for j in range(hpb)], heads)
        _, vjp = jax.vjp(fn, qs, ks, vs, shared[0], zs, shared[1], shared[2], shared[3], sts)
        dqs, dks, dvs, dab, dzs, dal, ddt, dgn, dsts = vjp((dos, dss))
        for j in range(hpb):
            ln = _lanes(j)
            dq_ref[:, ln] = dqs[j]
            dk_ref[:, ln] = dks[j]
            dv_ref[:, ln] = dvs[j]
            dz_ref[:, ln] = dzs[j].astype(dz_ref.dtype)
            dstate[hb * hpb + j] = dsts[j]
        dab_ref[...] += dab
        dhp_ref[0:1, :] += dal
        dhp_ref[1:2, :] += ddt
        dgn_ref[...] += dgn

    wide = hpb * HEAD
    rev = lambda off: pl.BlockSpec((CHUNK, wide), lambda c, h: (n - 1 - c, off // hpb + h))
    width = heads * HEAD
    head_blk = pl.BlockSpec((CHUNK, wide), lambda c, h: (n - 1 - c, h))
    ab_blk = pl.BlockSpec((CHUNK, HEAD), lambda c, h: (n - 1 - c, 0))
    return _call(
        body, (qkv, qkv, qkv, ab, proj, hp, gn, states, d_out), name=name, grid=(n, heads // hpb),
        in_specs=[rev(0), rev(heads), rev(2 * heads), ab_blk, rev(z_off),
                  pl.BlockSpec((8, HEAD), lambda c, h: (0, 0)), pl.BlockSpec((1, HEAD), lambda c, h: (0, 0)),
                  pl.BlockSpec((hpb, None, HEAD, HEAD), lambda c, h: (h, n - 1 - c, 0, 0)), head_blk],
        out_specs=[head_blk, head_blk, head_blk, ab_blk, head_blk,
                   pl.BlockSpec((8, HEAD), lambda c, h: (0, 0)), pl.BlockSpec((1, HEAD), lambda c, h: (0, 0))],
        out_shape=[jax.ShapeDtypeStruct((s, width), F32)] * 3
        + [jax.ShapeDtypeStruct((s, HEAD), F32), jax.ShapeDtypeStruct((s, width), BF16),
           jax.ShapeDtypeStruct((8, HEAD), F32), jax.ShapeDtypeStruct((1, HEAD), F32)],
        scratch_shapes=[pltpu.VMEM((heads, HEAD, HEAD), F32)], comm=comm)


def _s5_scan(buf, lt_ref, cin_r, cin_i, tt, reverse):
    nblk = tt // 8
    hl = S5_HALF
    rowi = lax.broadcasted_iota(jnp.int32, (8, hl), 0)
    sign = -1.0 if reverse else 1.0

    def body(j, carry):
        cr, ci = carry
        off = pl.multiple_of((nblk - 1 - j if reverse else j) * 8, 8)
        xr = buf[pl.ds(off, 8), 0:hl]
        xi = buf[pl.ds(off, 8), hl:2 * hl]
        for lv, d in enumerate((1, 2, 4)):
            ar, ai = lt_ref[2 * lv], sign * lt_ref[2 * lv + 1]
            sr = _shift_rows(xr, -d if reverse else d, rowi)
            si = _shift_rows(xi, -d if reverse else d, rowi)
            xr, xi = xr + ar * sr - ai * si, xi + ar * si + ai * sr
        pr, pi = (lt_ref[8], -lt_ref[9]) if reverse else (lt_ref[6], lt_ref[7])
        xr, xi = xr + pr * cr - pi * ci, xi + pr * ci + pi * cr
        buf[pl.ds(off, 8), 0:hl] = xr
        buf[pl.ds(off, 8), hl:2 * hl] = xi
        edge = rowi == (0 if reverse else 7)
        return (jnp.sum(jnp.where(edge, xr, 0.0), axis=0, keepdims=True),
                jnp.sum(jnp.where(edge, xi, 0.0), axis=0, keepdims=True))

    return lax.fori_loop(0, nblk, body, (cin_r, cin_i))


def _s5_fwd(u, wb, wc, lt, dskip, *, name, tt=512):
    s, d = u.shape
    nb = d // HEAD
    tt = min(tt, s)
    nt = s // tt
    hl = S5_HALF

    def body(u_ref, wb_ref, wc_ref, lt_ref, d_ref, y_ref, cin_ref, buf, carry):
        @pl.when(pl.program_id(1) == 0)
        def _():
            carry[...] = jnp.zeros_like(carry)

        cin_ref[...] = carry[0:1, :]
        uv = u_ref[...]
        buf[...] = _bdot_raw(uv, wb_ref[...])
        cr, ci = _s5_scan(buf, lt_ref, carry[0:1, 0:hl], carry[0:1, hl:2 * hl], tt, False)
        carry[0:1, 0:hl] = cr
        carry[0:1, hl:2 * hl] = ci
        y_ref[...] = _bdot_raw(buf[...], wc_ref[...]) + d_ref[...] * uv

    return pl.pallas_call(
        body, name=name, grid=(nb, nt),
        in_specs=[pl.BlockSpec((tt, HEAD), lambda b, t: (t, b)),
                  pl.BlockSpec((None, HEAD, 2 * hl), lambda b, t: (b, 0, 0)),
                  pl.BlockSpec((None, 2 * hl, HEAD), lambda b, t: (b, 0, 0)),
                  pl.BlockSpec((None, 10, 8, hl), lambda b, t: (b, 0, 0, 0)),
                  pl.BlockSpec((1, HEAD), lambda b, t: (0, b))],
        out_specs=[pl.BlockSpec((tt, HEAD), lambda b, t: (t, b)),
                   pl.BlockSpec((None, None, 1, 2 * hl), lambda b, t: (b, t, 0, 0))],
        out_shape=[jax.ShapeDtypeStruct((s, d), F32), jax.ShapeDtypeStruct((nb, nt, 1, 2 * hl), F32)],
        scratch_shapes=[pltpu.VMEM((tt, 2 * hl), F32), pltpu.VMEM((8, 2 * hl), F32)],
        compiler_params=_params(2),
    )(u, wb, wc, lt, dskip)


def _s5_bwd(u, dy, wb, wc, lt, dskip, cins, *, name, tt=512, comm=()):
    s, d = u.shape
    nb = d // HEAD
    tt = min(tt, s)
    nt = s // tt
    hl = S5_HALF

    def body(u_ref, dy_ref, wb_ref, wc_ref, lt_ref, d_ref, cin_ref,
             du_ref, dwb_ref, dwc_ref, dd_ref, dlam_ref, sbuf, abuf, acarry):
        @pl.when(pl.program_id(1) == 0)
        def _():
            acarry[...] = jnp.zeros_like(acarry)
            dwb_ref[...] = jnp.zeros_like(dwb_ref)
            dwc_ref[...] = jnp.zeros_like(dwc_ref)
            dd_ref[...] = jnp.zeros_like(dd_ref)
            dlam_ref[...] = jnp.zeros_like(dlam_ref)

        uv, dyv = u_ref[...], dy_ref[...]
        sbuf[...] = _bdot_raw(uv, wb_ref[...])
        _s5_scan(sbuf, lt_ref, cin_ref[:, 0:hl], cin_ref[:, hl:2 * hl], tt, False)
        abuf[...] = _bdot_raw(dyv, wc_ref[...], NT)
        ar, ai = _s5_scan(abuf, lt_ref, acarry[0:1, 0:hl], acarry[0:1, hl:2 * hl], tt, True)
        acarry[0:1, 0:hl] = ar
        acarry[0:1, hl:2 * hl] = ai
        du_ref[...] = _bdot_raw(abuf[...], wb_ref[...], NT) + d_ref[...] * dyv
        dwb_ref[...] += _bdot_raw(uv, abuf[...], TN)
        dwc_ref[...] += _bdot_raw(sbuf[...], dyv, TN)
        dd_ref[...] += jnp.sum(dyv * uv, axis=0, keepdims=True)
        first = lax.broadcasted_iota(jnp.int32, (tt, hl), 0) == 0
        spr = jnp.where(first, cin_ref[:, 0:hl], pltpu.roll(sbuf[:, 0:hl], 1, 0))
        spi = jnp.where(first, cin_ref[:, hl:2 * hl], pltpu.roll(sbuf[:, hl:2 * hl], 1, 0))
        avr, avi = abuf[:, 0:hl], abuf[:, hl:2 * hl]
        dlam_ref[:, 0:hl] += jnp.sum(avr * spr + avi * spi, axis=0, keepdims=True)
        dlam_ref[:, hl:2 * hl] += jnp.sum(avi * spr - avr * spi, axis=0, keepdims=True)

    rev = pl.BlockSpec((tt, HEAD), lambda b, t: (nt - 1 - t, b))
    return _call(
        body, (u, dy, wb, wc, lt, dskip, cins), name=name, grid=(nb, nt),
        in_specs=[rev, rev,
                  pl.BlockSpec((None, HEAD, 2 * hl), lambda b, t: (b, 0, 0)),
                  pl.BlockSpec((None, 2 * hl, HEAD), lambda b, t: (b, 0, 0)),
                  pl.BlockSpec((None, 10, 8, hl), lambda b, t: (b, 0, 0, 0)),
                  pl.BlockSpec((1, HEAD), lambda b, t: (0, b)),
                  pl.BlockSpec((None, None, 1, 2 * hl), lambda b, t: (b, nt - 1 - t, 0, 0))],
        out_specs=[rev,
                   pl.BlockSpec((None, HEAD, 2 * hl), lambda b, t: (b, 0, 0)),
                   pl.BlockSpec((None, 2 * hl, HEAD), lambda b, t: (b, 0, 0)),
                   pl.BlockSpec((1, HEAD), lambda b, t: (0, b)),
                   pl.BlockSpec((None, 1, 2 * hl), lambda b, t: (b, 0, 0))],
        out_shape=[jax.ShapeDtypeStruct((s, d), F32), jax.ShapeDtypeStruct(wb.shape, F32),
                   jax.ShapeDtypeStruct(wc.shape, F32), jax.ShapeDtypeStruct((1, d), F32),
                   jax.ShapeDtypeStruct((nb, 1, 2 * hl), F32)],
        scratch_shapes=[pltpu.VMEM((tt, 2 * hl), F32), pltpu.VMEM((tt, 2 * hl), F32), pltpu.VMEM((8, 2 * hl), F32)],
        comm=comm)


def _s5_pack(lr, li, br, bi, c_re, c_im):
    g = lr.shape[0]
    nb = g // S5_GB
    eye = jnp.eye(S5_GB, dtype=F32)
    bm = jnp.stack([br, bi]).reshape(2, nb, S5_GB, S5_STATE, S5_GROUP)
    wb = jnp.einsum("rbgpc,gh->bgcrhp", bm, eye).reshape(nb, HEAD, 2 * S5_HALF)
    cm = jnp.stack([c_re, -c_im]).reshape(2, nb, S5_GB, S5_GROUP, S5_STATE)
    wc = jnp.einsum("rbgcp,gh->brgphc", cm, eye).reshape(nb, 2 * S5_HALF, HEAD)
    pw = [(lr, li)]
    for _ in range(7):
        pr, pi = pw[-1]
        pw.append((pr * lr - pi * li, pr * li + pi * lr))
    blk = lambda a: a.reshape(nb, 1, S5_HALF)
    rows8 = lambda a: jnp.broadcast_to(blk(a), (nb, 8, S5_HALF))
    tables = []
    for n in (1, 2, 4):
        tables += [rows8(pw[n - 1][0]), rows8(pw[n - 1][1])]
    for order in (range(8), range(7, -1, -1)):
        tables += [jnp.concatenate([blk(pw[n][0]) for n in order], axis=1),
                   jnp.concatenate([blk(pw[n][1]) for n in order], axis=1)]
    return wb, wc, jnp.stack(tables, axis=1)


def _s5_unpack(dwb, dwc, dlam):
    nb = dwb.shape[0]
    g = nb * S5_GB
    eye = jnp.eye(S5_GB, dtype=F32)
    db = jnp.einsum("bgcrhp,gh->rbgpc", dwb.reshape(nb, S5_GB, S5_GROUP, 2, S5_GB, S5_STATE), eye)
    db = db.reshape(2, g, S5_STATE * S5_GROUP)
    dc = jnp.einsum("brgphc,gh->rbgcp", dwc.reshape(nb, 2, S5_GB, S5_STATE, S5_GB, S5_GROUP), eye)
    dc = dc.reshape(2, g, S5_GROUP, S5_STATE)
    dl = dlam.reshape(nb, 2, S5_GB, S5_STATE).transpose(1, 0, 2, 3).reshape(2, g, S5_STATE)
    return dl[0], dl[1], db[0], db[1], dc[0], -dc[1]


def _peer(r):
    mx, my, mc = lax.axis_index("x"), lax.axis_index("y"), lax.axis_index("c")
    px = 1 - mx if r & 4 else mx
    py = 1 - my if r & 2 else my
    pc = 1 - mc if r & 1 else mc
    return (px, py, pc), 4 * px + 2 * py + pc


_COMM_SCRATCH = [pltpu.SemaphoreType.DMA((N_DEV - 1,)), pltpu.SemaphoreType.DMA((N_DEV - 1,)), pltpu.SemaphoreType.DMA]


class _AllToAll:
    def __init__(self, x):
        self.x = x
        self.out_shape = jax.ShapeDtypeStruct(x.shape, x.dtype)

    def _copies(self, x_ref, out_ref, send_sems, recv_sems, local_sem):
        _, me = _peer(0)
        mine = pltpu.make_async_copy(x_ref.at[me], out_ref.at[me], local_sem)
        sends, recvs = [], []
        for r in range(1, N_DEV):
            pos, idx = _peer(r)
            sems = dict(send_sem=send_sems.at[r - 1], recv_sem=recv_sems.at[r - 1], device_id=pos, device_id_type=MESH)
            sends.append(pltpu.make_async_remote_copy(src_ref=x_ref.at[idx], dst_ref=out_ref.at[me], **sems))
            recvs.append(pltpu.make_async_remote_copy(src_ref=x_ref.at[idx], dst_ref=out_ref.at[idx], **sems))
        return mine, sends, recvs

    def start(self, *refs):
        mine, sends, _ = self._copies(*refs)
        mine.start()
        for cp in sends:
            cp.start()

    def finish(self, *refs):
        mine, sends, recvs = self._copies(*refs)
        for cp in recvs:
            cp.wait_recv()
        for cp in sends:
            cp.wait_send()
        mine.wait()


class _Gather:
    def __init__(self, x):
        self.x = x
        self.out_shape = jax.ShapeDtypeStruct((N_DEV,) + tuple(x.shape), x.dtype)

    def _copies(self, x_ref, out_ref, send_sems, recv_sems, local_sem):
        mx, my, mc = lax.axis_index("x"), lax.axis_index("y"), lax.axis_index("c")
        me, sibling = (mx, my, mc), (mx, my, 1 - mc)
        chips = [(1 - mx, my), (mx, 1 - my), (1 - mx, 1 - my)]

        def slot(px, py, pc):
            return out_ref.at[4 * px + 2 * py + pc]

        def copy(k, block, to, src=None):
            return pltpu.make_async_remote_copy(
                src_ref=slot(*block) if src is None else src, dst_ref=slot(*block),
                send_sem=send_sems.at[k], recv_sem=recv_sems.at[k], device_id=to, device_id_type=MESH)

        return dict(
            mine=pltpu.make_async_copy(x_ref, slot(*me), local_sem),
            first=[copy(0, me, sibling, src=x_ref)] + [copy(1 + j, me, (*chip, mc), src=x_ref) for j, chip in enumerate(chips)],
            passed=[copy(4 + j, (*chip, mc), sibling) for j, chip in enumerate(chips)],
            over_ici=[copy(1 + j, (*chip, mc), me) for j, chip in enumerate(chips)],
            from_sibling=[copy(0, sibling, me)] + [copy(4 + j, (*chip, 1 - mc), me) for j, chip in enumerate(chips)])

    def start(self, *refs):
        cps = self._copies(*refs)
        cps["mine"].start()
        for cp in cps["first"]:
            cp.start()

    def finish(self, *refs):
        cps = self._copies(*refs)
        for arrived, onward in zip(cps["over_ici"], cps["passed"]):
            arrived.wait_recv()
            onward.start()
        for cp in cps["from_sibling"]:
            cp.wait_recv()
        for cp in cps["first"] + cps["passed"]:
            cp.wait_send()
        cps["mine"].wait()


def _call(body, args, *, name, grid, in_specs, out_specs, out_shape, scratch_shapes=(), comm=()):
    n_in, n_out, n_scr, nc = len(in_specs), len(out_shape), len(scratch_shapes), len(comm)

    def wrapped(*refs):
        ins, c_in = refs[:n_in], refs[n_in:n_in + nc]
        outs, c_out = refs[n_in + nc:n_in + nc + n_out], refs[n_in + nc + n_out:n_in + 2 * nc + n_out]
        scr = refs[n_in + 2 * nc + n_out:n_in + 2 * nc + n_out + n_scr]
        sems = refs[n_in + 2 * nc + n_out + n_scr:]
        ids = [pl.program_id(a) for a in range(len(grid))]
        if nc:
            @pl.when(functools.reduce(operator.and_, [i == 0 for i in ids]))
            def _():
                for k, op in enumerate(comm):
                    op.start(c_in[k], c_out[k], *sems[3 * k:3 * k + 3])

        body(*ins, *outs, *scr)
        if nc:
            @pl.when(functools.reduce(operator.and_, [i == g - 1 for i, g in zip(ids, grid)]))
            def _():
                for k, op in enumerate(comm):
                    op.finish(c_in[k], c_out[k], *sems[3 * k:3 * k + 3])

    any_spec = pl.BlockSpec(memory_space=pl.ANY)
    res = pl.pallas_call(
        wrapped, name=name, grid=grid,
        in_specs=list(in_specs) + [any_spec] * nc, out_specs=list(out_specs) + [any_spec] * nc,
        out_shape=list(out_shape) + [op.out_shape for op in comm],
        scratch_shapes=list(scratch_shapes) + list(_COMM_SCRATCH) * nc,
        compiler_params=_params(len(grid)),
    )(*args, *[op.x for op in comm])
    return list(res[:n_out]), list(res[n_out:])


def _comm_call(op, *, name):
    return _call(lambda: None, (), name=name, grid=(1,), in_specs=[], out_specs=[], out_shape=[], comm=(op,))[1][0]


def _adamw(w, parts, m, v, *, name, tr=128):
    r, c = w.shape
    npart = parts.shape[0]
    tr = min(tr, r)
    assert r % tr == 0, (name, r)

    def body(w_ref, p_ref, m_ref, v_ref, g_ref, d_ref, mo_ref, vo_ref):
        g = p_ref[0].astype(F32)
        for k in range(1, npart):
            g = g + p_ref[k].astype(F32)
        m2 = ADAM_B1 * m_ref[...] + (1.0 - ADAM_B1) * g
        v2 = ADAM_B2 * v_ref[...] + (1.0 - ADAM_B2) * (g * g)
        m_hat = m2 / (1.0 - ADAM_B1 ** ADAM_STEP)
        v_hat = v2 / (1.0 - ADAM_B2 ** ADAM_STEP)
        g_ref[...] = g
        d_ref[...] = -ADAM_LR * (m_hat / (jnp.sqrt(v_hat) + ADAM_EPS) + ADAM_WD * w_ref[...])
        mo_ref[...] = m2
        vo_ref[...] = v2

    blk = pl.BlockSpec((tr, c), lambda i: (i, 0))
    return pl.pallas_call(
        body, name=name, grid=(r // tr,),
        in_specs=[blk, pl.BlockSpec((npart, tr, c), lambda i: (0, i, 0)), blk, blk],
        out_specs=[blk] * 4, out_shape=[jax.ShapeDtypeStruct((r, c), F32)] * 4,
        compiler_params=_params(1),
    )(w, parts, m, v)


def _sum_parts(parts, *, name):
    npart = parts.shape[0]

    def body(p_ref, o_ref):
        g = p_ref[0]
        for k in range(1, npart):
            g = g + p_ref[k]
        o_ref[...] = g

    return pl.pallas_call(
        body, name=name, grid=(1,), in_specs=[_full_spec(parts)],
        out_specs=pl.BlockSpec(parts.shape[1:], lambda i: (0, 0)),
        out_shape=jax.ShapeDtypeStruct(parts.shape[1:], F32), compiler_params=_params(1),
    )(parts)


def _pack(arrs):
    flat = jnp.concatenate([a.reshape(-1).astype(F32) for a in arrs])
    pad = (-flat.shape[0]) % (HEAD * HEAD)
    return jnp.pad(flat, (0, pad)).reshape(-1, HEAD)


def _unpack(packed, shapes):
    flat = packed.reshape(-1)
    out, off = [], 0
    for shp in shapes:
        size = math.prod(shp)
        out.append(flat[off:off + size].reshape(shp))
        off += size
    return out


def _add_epilogue(acc, res):
    return (acc + res,)


def _relu2_epilogue(acc):
    r = jnp.maximum(acc, 0.0)
    return acc, r * r


def _ple_epilogue(acc, gpre, h):
    return h + jax.nn.sigmoid(gpre) * acc, acc


def kernel(x, p, norm_mix, norm_mlp, norm_ple, w_in_e, w_out_e, hgrn_lb, g_norm_a, conv_w, a_log, dt_bias, g_norm_b, s5_a_re, s5_a_im, s5_b_re, s5_b_im, s5_c_re, s5_c_im, s5_d, s5_log_dt, w_glu, b_glu, w_out_o, w_up, w_down, w_ple_gate, w_ple_proj, final_norm, loss_target, m_norm_mix, m_norm_mlp, m_norm_ple, m_w_in_e, m_w_out_e, m_hgrn_lb, m_g_norm_a, m_conv_w, m_a_log, m_dt_bias, m_g_norm_b, m_s5_a_re, m_s5_a_im, m_s5_b_re, m_s5_b_im, m_s5_c_re, m_s5_c_im, m_s5_d, m_s5_log_dt, m_w_glu, m_b_glu, m_w_out_o, m_w_up, m_w_down, m_w_ple_gate, m_w_ple_proj, m_final_norm, v_norm_mix, v_norm_mlp, v_norm_ple, v_w_in_e, v_w_out_e, v_hgrn_lb, v_g_norm_a, v_conv_w, v_a_log, v_dt_bias, v_g_norm_b, v_s5_a_re, v_s5_a_im, v_s5_b_re, v_s5_b_im, v_s5_c_re, v_s5_c_im, v_s5_d, v_s5_log_dt, v_w_glu, v_b_glu, v_w_out_o, v_w_up, v_w_down, v_w_ple_gate, v_w_ple_proj, v_final_norm):
    args = dict(locals())
    s, d = x.shape[1], x.shape[2]
    aw = d // 2
    ha = hb = aw // HEAD
    main = 4 * d
    z_col = 2 * d + 3 * aw
    ff = w_up.shape[2] * N_DEV
    ple = p.shape[-1]
    groups = d // S5_GROUP
    me = 4 * lax.axis_index("x") + 2 * lax.axis_index("y") + lax.axis_index("c")
    x2, target = x[0], loss_target[0]
    row = lambda a, i: a[i:i + 1]

    def gather_of(w):
        return _Gather(w.astype(BF16))

    w_in = jnp.transpose(_comm_call(gather_of(w_in_e[0]), name="ag_w_in"), (1, 0, 2)).reshape(d, -1)
    w_main = w_in[:, :main]
    w_tail = jnp.pad(w_in[:, main:], ((0, 0), (0, HEAD - 2 * hb)))

    lb_rows = [row(hgrn_lb, 0), row(hgrn_lb, 1), row(hgrn_lb, 2)]
    (lb0,) = _small_call(_lb0_stage, lb_rows, name="f_lb0")
    hp = jnp.zeros((8, HEAD), F32).at[0, :hb].set(a_log[0]).at[1, :hb].set(dt_bias[0])
    expand = jnp.asarray(np.kron(np.eye(S5_STATE, dtype=np.float32), np.ones((1, S5_GROUP), np.float32)))
    prep_in = [s5_a_re[0], s5_a_im[0], s5_log_dt[0].reshape(groups, 1),
               s5_b_re[0].reshape(groups, -1), s5_b_im[0].reshape(groups, -1), expand]
    lr, li, br, bi = _small_call(_s5_prep_stage, prep_in, name="f_s5_prep")
    wb, wc, lt = _s5_pack(lr, li, br, bi, s5_c_re[0], s5_c_im[0])
    fnorm = final_norm.reshape(1, d)

    def block_fwd(h, l, w_dn):
        hn = _rows_call(_rms_stage, [h], [row(norm_mlp, l)], [BF16], name=f"f_norm_mlp{l}")
        up_args = dict(epilogue=_relu2_epilogue, out_dtypes=(F32, BF16), name=f"f_up{l}")
        next_dn = None
        if w_dn is None:
            (up, act), (dn0,) = _mm(hn, w_upg[l], comm=(gather_of(w_down[0]),), **up_args)
            w_dn = dn0.reshape(ff, d)
            h2, (dn1,) = _mm(act, w_dn, extras=(h,), epilogue=_add_epilogue, name=f"f_down{l}",
                             comm=(gather_of(w_down[1]),))
            next_dn = dn1.reshape(ff, d)
        else:
            up, act = _mm(hn, w_upg[l], **up_args)
            h2 = _mm(act, w_dn, extras=(h,), epilogue=_add_epilogue, name=f"f_down{l}")
        hq = _rows_call(_rms_stage, [h2], [row(norm_ple, l)], [BF16], name=f"f_norm_ple{l}")
        gpre = _mm(hq, w_pgg[l], name=f"f_ple_gate{l}")
        h3, pp = _mm(p[l, 0], w_ppg[l], extras=(gpre, h2), epilogue=_ple_epilogue, out_dtypes=(F32, F32),
                     name=f"f_ple_proj{l}")
        return h3, dict(h=h, hn=hn, up=up, act=act, h2=h2, hq=hq, gpre=gpre, pp=pp, w_dn=w_dn), next_dn

    hn0 = _rows_call(_rms_stage, [x2], [row(norm_mix, 0)], [BF16], name="f_norm_mix0")
    shard_shapes = [conv_w[0].shape, s5_d.shape, b_glu.shape]
    proj, (oe8, pg8, small) = _mm(hn0, w_main, name="f_proj", comm=(
        gather_of(w_out_e[0]), gather_of(w_ple_gate), _Gather(_pack([conv_w[0], s5_d, b_glu]))))
    w_oe = oe8.reshape(d, d)
    w_top, w_bot = w_oe[:aw], w_oe[aw:]
    w_pgg = jnp.transpose(pg8, (1, 0, 2, 3)).reshape(2, d, d)
    conv_g, s5d_g, bglu_g = zip(*[_unpack(small[j], shard_shapes) for j in range(N_DEV)])
    conv_full = jnp.concatenate(conv_g, axis=1)
    s5d_full = jnp.concatenate(s5d_g, axis=1)
    bglu_full = jnp.concatenate(bglu_g, axis=1)
    ab = _mm(hn0, w_tail, name="f_ab")
    (oa, st_a), (gl8, oo8) = _hgrn_fwd(proj, lb0, g_norm_a, heads=ha, name="f_hgrn",
                                       comm=(gather_of(w_glu[0]), gather_of(w_out_o[0])))
    w_gl, w_oo = gl8.reshape(d, d), oo8.reshape(d, d)
    qkv = _conv_fwd(proj, conv_full, col_off=2 * d, name="f_conv")
    (ob, st_b), (up8,) = _delta_fwd(qkv, ab, proj, hp, g_norm_b, heads=hb, z_off=z_col // HEAD, name="f_delta",
                                    comm=(gather_of(w_up),))
    w_upg = jnp.transpose(up8, (1, 2, 0, 3)).reshape(2, d, ff)
    h1, (pp8,) = _mm(oa, w_top, extras=(x2,), epilogue=_add_epilogue, name="f_out_a", comm=(gather_of(w_ple_proj),))
    w_ppg = jnp.transpose(pp8, (1, 2, 0, 3)).reshape(2, ple, d)
    h1 = _mm(ob, w_bot, extras=(h1,), epilogue=_add_epilogue, name="f_out_b")
    h3, sv0, w_dn1 = block_fwd(h1, 0, None)

    u = _rows_call(_rms_stage, [h3], [row(norm_mix, 1)], [F32], name="f_norm_mix1")
    y, cins = _s5_fwd(u, wb, wc, lt, s5d_full, name="f_s5")
    act_g = _rows_call(_gelu_stage, [y], [], [BF16], name="f_gelu")
    gl_raw = _mm(act_g, w_gl, name="f_glu")
    glu = _rows_call(_glu_stage, [y, gl_raw], [bglu_full], [BF16], name="f_glu_gate")
    h4 = _mm(glu, w_oo, extras=(h3,), epilogue=_add_epilogue, name="f_out_o")
    h6, sv1, _ = block_fwd(h4, 1, w_dn1)
    dh, d_fnorm, loss8 = _loss_call(h6, fnorm, target, name="loss")
    loss = lax.psum(loss8[0, 0], ("x", "y", "c"))

    dshard, ffs, cols = d // N_DEV, ff // N_DEV, w_in_e.shape[2]
    rows8 = lambda g: _AllToAll(g.reshape(N_DEV, -1, g.shape[-1]))
    cols8 = lambda g: _AllToAll(jnp.transpose(g.reshape(g.shape[0], N_DEV, -1), (1, 0, 2)))

    def block_bwd(dh3, l, sv, carried):
        (dgpre, dpp), _ = _rows_vjp(_ple_stage, [sv["h2"], sv["gpre"], sv["pp"]], [], [dh3],
                                    row_grads={1: BF16, 2: BF16}, name=f"b_ple{l}")
        g_pp = _mm(p[l, 0], dpp, ta=True, out_dtypes=(BF16,), name=f"b_w_ple_proj{l}")
        g_pg = _mm(sv["hq"], dgpre, ta=True, out_dtypes=(BF16,), name=f"b_w_ple_gate{l}")
        dhq = _mm(dgpre, w_pgg[l], tb=True, name=f"b_ple_gate{l}")
        (dh2,), (g_nple,) = _rows_vjp(_rms_stage, [sv["h2"]], [row(norm_ple, l)], [dhq], row_grads={0: F32},
                                      adds={0: dh3}, name=f"b_norm_ple{l}")
        dup, (r_pg, r_pp) = _mm(dh2, sv["w_dn"], tb=True, extras=(sv["up"],), epilogue=_relu2_grad_epilogue,
                                out_dtypes=(BF16,), name=f"b_down{l}", comm=(rows8(g_pg), cols8(g_pp)))
        g_dn = _mm(sv["act"], dh2, ta=True, out_dtypes=(BF16,), name=f"b_w_down{l}", comm=carried)
        g_dn, r_carried = g_dn if carried else (g_dn, [])
        g_up = _mm(sv["hn"], dup, ta=True, out_dtypes=(BF16,), name=f"b_w_up{l}")
        dhn = _mm(dup, w_upg[l], tb=True, name=f"b_up{l}")
        (dh0,), (g_nmlp,) = _rows_vjp(_rms_stage, [sv["h"]], [row(norm_mlp, l)], [dhn], row_grads={0: F32},
                                      adds={0: dh2}, name=f"b_norm_mlp{l}")
        return dh0, dict(w_ple_proj=r_pp, w_ple_gate=r_pg, norm_ple=g_nple, w_down=g_dn, w_up=g_up, norm_mlp=g_nmlp,
                         carried=r_carried)

    dh4, gb1 = block_bwd(dh, 1, sv1, ())
    dglu = _mm(dh4, w_oo, tb=True, name="b_out_o")
    g_oo = _mm(glu, dh4, ta=True, out_dtypes=(BF16,), name="b_w_out_o")
    (dy1, dgl), (g_bglu,) = _rows_vjp(_glu_stage, [y, gl_raw], [bglu_full], [dglu], row_grads={0: F32, 1: BF16},
                                      name="b_glu_gate")
    g_gl = _mm(act_g, dgl, ta=True, out_dtypes=(BF16,), name="b_w_glu")
    dact = _mm(dgl, w_gl, tb=True, name="b_glu")
    (dy,), _ = _rows_vjp(_gelu_stage, [y], [], [dact], row_grads={0: F32}, adds={0: dy1}, name="b_gelu")
    (du, dwb, dwc, g_s5d, dlam), (r_dn1, r_up1) = _s5_bwd(u, dy, wb, wc, lt, s5d_full, cins, name="b_s5",
                                                         comm=(rows8(gb1["w_down"]), cols8(gb1["w_up"])))
    (dh3,), (g_nmix1,) = _rows_vjp(_rms_stage, [h3], [row(norm_mix, 1)], [du], row_grads={0: F32}, adds={0: dh4},
                                   name="b_norm_mix1")
    dlr, dli, dbr, dbi, g_cre, g_cim = _s5_unpack(dwb, dwc, dlam)
    g_are, g_aim, g_ldt, g_bre, g_bim, _ = _small_vjp(_s5_prep_stage, prep_in, [dlr, dli, dbr, dbi], name="b_s5_prep")

    dh1, gb0 = block_bwd(dh3, 0, sv0, (rows8(g_oo), rows8(g_gl)))
    r_oo, r_gl = gb0["carried"]
    doa = _mm(dh1, w_top, tb=True, name="b_out_a")
    dob = _mm(dh1, w_bot, tb=True, name="b_out_b")
    g_oe = jnp.concatenate([_mm(oa, dh1, ta=True, out_dtypes=(BF16,), name="b_w_out_a"),
                            _mm(ob, dh1, ta=True, out_dtypes=(BF16,), name="b_w_out_b")], axis=0)
    dq, df, di, dg, dlb, g_gna = _hgrn_bwd(proj, lb0, g_norm_a, st_a, doa, heads=ha, name="b_hgrn")
    (dqb, dkb, dvb, dab, dz, dhp, g_gnb), (r_dn0, r_up0, r_oe) = _delta_bwd(
        qkv, ab, proj, hp, g_norm_b, st_b, dob, heads=hb, z_off=z_col // HEAD, name="b_delta",
        comm=(rows8(gb0["w_down"]), cols8(gb0["w_up"]), rows8(g_oe)))
    dqkv, g_conv = _conv_bwd(proj, conv_full, jnp.concatenate([dqb, dkb, dvb], axis=1), col_off=2 * d, name="b_conv")
    dproj = jnp.concatenate([dq, df, di, dg, dqkv, dz], axis=1)
    g_main = _mm(hn0, dproj, ta=True, out_dtypes=(BF16,), name="b_w_proj")
    g_tail = _mm(hn0, dab, ta=True, out_dtypes=(BF16,), name="b_w_ab")
    dhn0, (r_in,) = _mm(dproj, w_main, tb=True, name="b_proj",
                        comm=(cols8(jnp.concatenate([g_main, g_tail[:, :2 * hb]], axis=1)),))
    dhn0 = _mm(dab, w_tail, tb=True, extras=(dhn0,), epilogue=_add_epilogue, name="b_ab")
    (dx,), (g_nmix0,) = _rows_vjp(_rms_stage, [x2], [row(norm_mix, 0)], [dhn0], row_grads={0: F32}, adds={0: dh1},
                                  name="b_norm_mix0")
    g_lb = jnp.concatenate(_small_vjp(_lb0_stage, lb_rows, [dlb], name="b_lb0"), axis=0)

    small_grads = dict(
        norm_mix=jnp.concatenate([g_nmix0, g_nmix1], axis=0),
        norm_mlp=jnp.concatenate([gb0["norm_mlp"], gb1["norm_mlp"]], axis=0),
        norm_ple=jnp.concatenate([gb0["norm_ple"], gb1["norm_ple"]], axis=0),
        hgrn_lb=g_lb, g_norm_a=g_gna, a_log=dhp[0:1, :hb], dt_bias=dhp[1:2, :hb], g_norm_b=g_gnb,
        s5_a_re=g_are[None], s5_a_im=g_aim[None], s5_b_re=g_bre.reshape(s5_b_re.shape),
        s5_b_im=g_bim.reshape(s5_b_im.shape), s5_c_re=g_cre[None], s5_c_im=g_cim[None],
        s5_log_dt=g_ldt.reshape(1, groups), final_norm=d_fnorm.reshape(d),
        conv_w=g_conv, s5_d=g_s5d, b_glu=g_bglu)
    rep_names = ["norm_mix", "norm_mlp", "norm_ple", "hgrn_lb", "g_norm_a", "a_log", "dt_bias", "g_norm_b", "s5_a_re",
                 "s5_a_im", "s5_b_re", "s5_b_im", "s5_c_re", "s5_c_im", "s5_log_dt", "final_norm"]
    full_names = rep_names + ["conv_w", "s5_d", "b_glu"]
    parts = _comm_call(_Gather(_pack([small_grads[k] for k in full_names])), name="ag_small_grads")
    summed = _unpack(_sum_parts(parts, name="sum_small_grads"), [small_grads[k].shape for k in full_names])
    summed = dict(zip(full_names, summed))
    cw = conv_w.shape[2]
    dshard = d // N_DEV
    shard_g = dict(conv_w=lax.dynamic_slice(summed["conv_w"], (0, me * cw), (CONV_WIDTH, cw))[None],
                   s5_d=lax.dynamic_slice(summed["s5_d"], (0, me * dshard), (1, dshard)),
                   b_glu=lax.dynamic_slice(summed["b_glu"], (0, me * dshard), (1, dshard)))
    small_names = rep_names + ["conv_w", "s5_d", "b_glu"]
    g_small = [summed[k] if k in rep_names else shard_g[k] for k in small_names]
    shapes = [args[k].shape for k in small_names]
    sm_out = _adamw(_pack([args[k] for k in small_names]), _pack(g_small)[None], _pack([args["m_" + k] for k in small_names]),
                    _pack([args["v_" + k] for k in small_names]), name="adamw_small")
    sm_out = [dict(zip(small_names, _unpack(o, shapes))) for o in sm_out]

    received = dict(w_in_e=[r_in], w_out_e=[r_oe], w_glu=[r_gl], w_out_o=[r_oo], w_up=[r_up0, r_up1],
                    w_down=[r_dn0, r_dn1], w_ple_gate=[gb0["w_ple_gate"], gb1["w_ple_gate"]],
                    w_ple_proj=[gb0["w_ple_proj"], gb1["w_ple_proj"]])
    big_out = {}
    for k, layers in received.items():
        per_layer = [_adamw(args[k][l], recv, args["m_" + k][l], args["v_" + k][l], name=f"adamw_{k}{l}")
                     for l, recv in enumerate(layers)]
        big_out[k] = [jnp.stack(o) for o in zip(*per_layer)]

    names = ["norm_mix", "norm_mlp", "norm_ple", "w_in_e", "w_out_e", "hgrn_lb", "g_norm_a", "conv_w", "a_log", "dt_bias",
             "g_norm_b", "s5_a_re", "s5_a_im", "s5_b_re", "s5_b_im", "s5_c_re", "s5_c_im", "s5_d", "s5_log_dt", "w_glu",
             "b_glu", "w_out_o", "w_up", "w_down", "w_ple_gate", "w_ple_proj", "final_norm"]
    result = [loss, dx[None]]
    for j in range(4):
        result += [big_out[k][j] if k in big_out else sm_out[j][k] for k in names]
    return tuple(result)
```

```python
import functools
import math
import operator

import numpy as np
import jax
import jax.numpy as jnp
from jax import lax
from jax.experimental import pallas as pl
from jax.experimental.pallas import tpu as pltpu

F32 = jnp.float32
BF16 = jnp.bfloat16
MM_DTYPE = BF16
HI = lax.Precision.HIGHEST
MESH = pl.DeviceIdType.MESH

NORM_EPS = 1e-6
CHUNK = 64
HEAD = 128
CONV_WIDTH = 4
S5_GROUP = 16
S5_STATE = 64
S5_GB = 8
S5_HALF = S5_GB * S5_STATE
N_DEV = 8
HEADS_PER_STEP = 8
ADAM_LR, ADAM_B1, ADAM_B2, ADAM_EPS, ADAM_WD, ADAM_STEP = 0.001, 0.9, 0.999, 1e-08, 0.01, 10
VMEM_LIMIT = 56 * 1024 * 1024

NN = (((1,), (0,)), ((), ()))
NT = (((1,), (1,)), ((), ()))
TN = (((0,), (0,)), ((), ()))


def _dot(a, b, dn=NN):
    return lax.dot_general(a, b, dn, precision=HI, preferred_element_type=F32)


def _hdot(a, b, dn=NN):
    return lax.dot_general(a, b, dn, precision=lax.Precision.HIGH, preferred_element_type=F32)


def _bdot_raw(a, b, dn=NN):
    return lax.dot_general(a.astype(BF16), b.astype(BF16), dn, preferred_element_type=F32)


@functools.partial(jax.custom_vjp, nondiff_argnums=(2,))
def _bdot(a, b, dn):
    return _bdot_raw(a, b, dn)


def _bdot_fwd(a, b, dn):
    return _bdot_raw(a, b, dn), (a, b)


def _bdot_bwd(dn, res, g):
    a, b = res
    if dn == NN:
        return _bdot_raw(g, b, NT), _bdot_raw(a, g, TN)
    if dn == NT:
        return _bdot_raw(g, b, NN), _bdot_raw(g, a, TN)
    assert dn == TN
    return _bdot_raw(b, g, NT), _bdot_raw(a, g, NN)


_bdot.defvjp(_bdot_fwd, _bdot_bwd)


def _per_head(f):
    def g(*args, **kw):
        n = [len(a.vals) for a in args if isinstance(a, _Heads)]
        if not n:
            return f(*args, **kw)
        return _Heads([f(*[a.vals[j] if isinstance(a, _Heads) else a for a in args], **kw) for j in range(n[0])])
    return g


class _Heads:
    def __init__(self, vals):
        self.vals = list(vals)

    def __add__(self, o):
        return _per_head(operator.add)(self, o)

    def __radd__(self, o):
        return _per_head(operator.add)(o, self)

    def __sub__(self, o):
        return _per_head(operator.sub)(self, o)

    def __rsub__(self, o):
        return _per_head(operator.sub)(o, self)

    def __mul__(self, o):
        return _per_head(operator.mul)(self, o)

    def __rmul__(self, o):
        return _per_head(operator.mul)(o, self)

    def __neg__(self):
        return _per_head(operator.neg)(self)


_exp, _log, _where, _sum, _mean = (_per_head(f) for f in (jnp.exp, jnp.log, jnp.where, jnp.sum, jnp.mean))
_sigmoid, _rsqrt, _equal = _per_head(jax.nn.sigmoid), _per_head(lax.rsqrt), _per_head(operator.eq)
_hdot_h, _bdot_h = _per_head(_hdot), _per_head(_bdot)


def _params(n_axes):
    return pltpu.CompilerParams(dimension_semantics=("arbitrary",) * n_axes, vmem_limit_bytes=VMEM_LIMIT)


def _full_spec(a):
    nd = a.ndim
    return pl.BlockSpec(a.shape, lambda *_: (0,) * nd)


def _mm(a, b, *, name, ta=False, tb=False, extras=(), epilogue=None, out_dtypes=(F32,), tm=1024, tn=1024, tk=2048,
        out_slots=False, comm=()):
    m = a.shape[1] if ta else a.shape[0]
    k = a.shape[0] if ta else a.shape[1]
    n = b.shape[0] if tb else b.shape[1]
    assert k == (b.shape[1] if tb else b.shape[0]), (name, a.shape, b.shape)
    tm, tn, tk = min(tm, m), min(tn, n), min(tk, k)
    assert m % tm == 0 and n % tn == 0 and k % tk == 0, (name, m, n, k)
    nk = k // tk
    n_ex, n_out = len(extras), len(out_dtypes)
    dn = (((0 if ta else 1,), (1 if tb else 0,)), ((), ()))

    def body(a_ref, b_ref, *rest):
        ex_refs, out_refs = rest[:n_ex], rest[n_ex:n_ex + n_out]
        part = lax.dot_general(a_ref[...].astype(MM_DTYPE), b_ref[...].astype(MM_DTYPE), dn, preferred_element_type=F32)

        def finish(acc):
            outs = epilogue(acc, *[r[...] for r in ex_refs]) if epilogue is not None else (acc,)
            for o_ref, o in zip(out_refs, outs):
                o_ref[...] = o.astype(o_ref.dtype)

        if nk == 1:
            finish(part)
            return
        acc_ref = rest[-1]
        kk = pl.program_id(2)

        @pl.when(kk == 0)
        def _():
            acc_ref[...] = part

        @pl.when((kk > 0) & (kk < nk - 1))
        def _():
            acc_ref[...] += part

        @pl.when(kk == nk - 1)
        def _():
            finish(acc_ref[...] + part)

    a_spec = pl.BlockSpec((tk, tm), lambda i, j, q: (q, i)) if ta else pl.BlockSpec((tm, tk), lambda i, j, q: (i, q))
    b_spec = pl.BlockSpec((tn, tk), lambda i, j, q: (j, q)) if tb else pl.BlockSpec((tk, tn), lambda i, j, q: (q, j))
    ex_specs = []
    for e in extras:
        if e.shape[0] == 1 and m != 1:
            ex_specs.append(pl.BlockSpec((1, tn), lambda i, j, q: (0, j)))
        else:
            ex_specs.append(pl.BlockSpec((tm, tn), lambda i, j, q: (i, j)))
    if out_slots:
        out_spec, out_dims = pl.BlockSpec((None, tm, tn), lambda i, j, q: (j, i, 0)), (n // tn, m, tn)
    else:
        out_spec, out_dims = pl.BlockSpec((tm, tn), lambda i, j, q: (i, j)), (m, n)
    outs, exchanged = _call(
        body, (a, b, *extras), name=name, grid=(m // tm, n // tn, nk),
        in_specs=[a_spec, b_spec] + ex_specs,
        out_specs=[out_spec for _ in out_dtypes],
        out_shape=[jax.ShapeDtypeStruct(out_dims, dt) for dt in out_dtypes],
        scratch_shapes=[pltpu.VMEM((tm, tn), F32)] if nk > 1 else [], comm=comm)
    outs = outs[0] if n_out == 1 else tuple(outs)
    return (outs, exchanged) if comm else outs


def _rows_call(fn, rows, consts, out_dtypes, *, name, tr=256):
    s = rows[0].shape[0]
    tr = min(tr, s)
    nr, nc = len(rows), len(consts)
    widths = [o.shape[1] for o in jax.eval_shape(
        fn, *[jax.ShapeDtypeStruct((tr, r.shape[1]), F32) for r in rows],
        *[jax.ShapeDtypeStruct(c.shape, F32) for c in consts])]

    def body(*refs):
        rv = [r[...].astype(F32) for r in refs[:nr]]
        cv = [c[...] for c in refs[nr:nr + nc]]
        for o_ref, o in zip(refs[nr + nc:], fn(*rv, *cv)):
            o_ref[...] = o.astype(o_ref.dtype)

    outs = pl.pallas_call(
        body, name=name, grid=(s // tr,),
        in_specs=[pl.BlockSpec((tr, r.shape[1]), lambda i: (i, 0)) for r in rows] + [_full_spec(c) for c in consts],
        out_specs=[pl.BlockSpec((tr, w), lambda i: (i, 0)) for w in widths],
        out_shape=[jax.ShapeDtypeStruct((s, w), dt) for w, dt in zip(widths, out_dtypes)],
        compiler_params=_params(1),
    )(*rows, *consts)
    return outs[0] if len(outs) == 1 else tuple(outs)


def _rows_vjp(fn, rows, consts, cots, *, name, row_grads, adds=None, tr=256):
    adds = adds or {}
    s = rows[0].shape[0]
    tr = min(tr, s)
    nr, nc, nt = len(rows), len(consts), len(cots)
    rg = sorted(row_grads)
    ad = sorted(adds)

    def body(*refs):
        rv = [r[...].astype(F32) for r in refs[:nr]]
        cv = [c[...] for c in refs[nr:nr + nc]]
        ct = [c[...].astype(F32) for c in refs[nr + nc:nr + nc + nt]]
        av = {i: r[...].astype(F32) for i, r in zip(ad, refs[nr + nc + nt:nr + nc + nt + len(ad)])}
        out_refs = refs[nr + nc + nt + len(ad):]
        _, vjp = jax.vjp(fn, *rv, *cv)
        grads = vjp(tuple(ct))
        for o_ref, i in zip(out_refs[:len(rg)], rg):
            g = grads[i]
            if i in av:
                g = g + av[i]
            o_ref[...] = g.astype(o_ref.dtype)

        @pl.when(pl.program_id(0) == 0)
        def _():
            for o_ref in out_refs[len(rg):]:
                o_ref[...] = jnp.zeros_like(o_ref)

        for o_ref, g in zip(out_refs[len(rg):], grads[nr:]):
            o_ref[...] += g

    row_spec = lambda a: pl.BlockSpec((tr, a.shape[1]), lambda i: (i, 0))
    outs = pl.pallas_call(
        body, name=name, grid=(s // tr,),
        in_specs=[row_spec(r) for r in rows] + [_full_spec(c) for c in consts] + [row_spec(c) for c in cots]
        + [row_spec(adds[i]) for i in ad],
        out_specs=[row_spec(rows[i]) for i in rg] + [_full_spec(c) for c in consts],
        out_shape=[jax.ShapeDtypeStruct(rows[i].shape, row_grads[i]) for i in rg]
        + [jax.ShapeDtypeStruct(c.shape, F32) for c in consts],
        compiler_params=_params(1),
    )(*rows, *consts, *cots, *[adds[i] for i in ad])
    return list(outs[:len(rg)]), list(outs[len(rg):])


def _small_call(fn, ins, *, name):
    shapes = jax.eval_shape(fn, *[jax.ShapeDtypeStruct(a.shape, F32) for a in ins])

    def body(*refs):
        for o_ref, o in zip(refs[len(ins):], fn(*[r[...] for r in refs[:len(ins)]])):
            o_ref[...] = o

    return pl.pallas_call(
        body, name=name, in_specs=[_full_spec(a) for a in ins],
        out_specs=[pl.BlockSpec(o.shape, functools.partial(lambda nd, *_: (0,) * nd, len(o.shape))) for o in shapes],
        out_shape=[jax.ShapeDtypeStruct(o.shape, F32) for o in shapes], grid=(1,),
        compiler_params=_params(1),
    )(*ins)


def _small_vjp(fn, ins, cots, *, name):
    def body(*refs):
        vals = [r[...] for r in refs[:len(ins)]]
        ct = [r[...] for r in refs[len(ins):len(ins) + len(cots)]]
        _, vjp = jax.vjp(fn, *vals)
        for o_ref, g in zip(refs[len(ins) + len(cots):], vjp(tuple(ct))):
            o_ref[...] = g

    return pl.pallas_call(
        body, name=name, in_specs=[_full_spec(a) for a in ins] + [_full_spec(c) for c in cots],
        out_specs=[_full_spec(a) for a in ins],
        out_shape=[jax.ShapeDtypeStruct(a.shape, F32) for a in ins], grid=(1,),
        compiler_params=_params(1),
    )(*ins, *cots)


def _rms(x, g):
    return x * _rsqrt(_mean(x * x, axis=-1, keepdims=True) + NORM_EPS) * g


def _rms_stage(x, g):
    return (_rms(x, g),)


def _silu(x):
    return x * _sigmoid(x)


def _softplus(x):
    return jnp.maximum(x, 0.0) + jnp.log1p(jnp.exp(-jnp.abs(x)))


def _gelu(x):
    return jax.nn.gelu(x, approximate=True)


def _gelu_stage(y):
    return (_gelu(y),)


def _glu_stage(y, gl_raw, b):
    return (_gelu(y) * jax.nn.sigmoid(gl_raw + b),)


def _ple_stage(h, gpre, pp):
    return (h + jax.nn.sigmoid(gpre) * pp,)


def _relu2_grad_epilogue(acc, up):
    return (acc * (2.0 * jnp.maximum(up, 0.0)),)


def _lb0_stage(x0, x1, x2):
    mx = jnp.maximum(jnp.maximum(x0, x1), x2)
    e0, e1, e2 = jnp.exp(x0 - mx), jnp.exp(x1 - mx), jnp.exp(x2 - mx)
    return (e0 / (e0 + e1 + e2),)


def _s5_prep_stage(a_re, a_im, log_dt, b_re, b_im, expand):
    step = jnp.exp(log_dt)
    mag = jnp.exp(a_re * step)
    lr = mag * jnp.cos(a_im * step)
    li = mag * jnp.sin(a_im * step)
    den = a_re * a_re + a_im * a_im
    cr = ((lr - 1.0) * a_re + li * a_im) / den
    ci = (li * a_re - (lr - 1.0) * a_im) / den
    cr_e, ci_e = _dot(cr, expand), _dot(ci, expand)
    return lr, li, cr_e * b_re - ci_e * b_im, cr_e * b_im + ci_e * b_re


def _loss_call(h, g, target, *, name, tr=256):
    s, d = h.shape
    tr = min(tr, s)

    def loss_fn(hv, gv, tv):
        err = _rms(hv, gv) - tv
        return 0.5 * jnp.sum(jnp.mean(err * err, axis=-1))

    def body(h_ref, g_ref, t_ref, dh_ref, dg_ref, loss_ref):
        val, (dh, dg) = jax.value_and_grad(loss_fn, argnums=(0, 1))(h_ref[...], g_ref[...], t_ref[...])
        dh_ref[...] = dh

        @pl.when(pl.program_id(0) == 0)
        def _():
            dg_ref[...] = jnp.zeros_like(dg_ref)
            loss_ref[...] = jnp.zeros_like(loss_ref)

        dg_ref[...] += dg
        loss_ref[...] += jnp.full(loss_ref.shape, val, F32)

    row = pl.BlockSpec((tr, d), lambda i: (i, 0))
    return pl.pallas_call(
        body, name=name, grid=(s // tr,),
        in_specs=[row, _full_spec(g), row],
        out_specs=[row, _full_spec(g), pl.BlockSpec((8, 128), lambda i: (0, 0))],
        out_shape=[jax.ShapeDtypeStruct((s, d), F32), jax.ShapeDtypeStruct(g.shape, F32),
                   jax.ShapeDtypeStruct((8, 128), F32)],
        compiler_params=_params(1),
    )(h, g, target)


def _hgrn_chunk(q, fp, iv, gp, lb, gn, st_t):
    c = CHUNK
    row = lax.broadcasted_iota(jnp.int32, (c, c), 0)
    col = lax.broadcasted_iota(jnp.int32, (c, c), 1)
    causal = row >= col
    fg = lb + (1.0 - lb) * _sigmoid(fp)
    k = 1.0 - fg
    lf = _log(fg)
    cum = _hdot_h(causal.astype(F32), lf, NN)
    first_half = (lax.broadcasted_iota(jnp.int32, (c, 1), 0) < c // 2).astype(F32)
    ref = _sum(lf * first_half, axis=0, keepdims=True)
    cend = _sum(lf, axis=0, keepdims=True)
    scores = _where(causal, _hdot_h(q * _exp(cum - ref), k * _exp(ref - cum), NT), 0.0)
    out = _bdot_h(scores, iv, NN) + _bdot_h(q * _exp(cum), st_t, NT)
    st_new = st_t * _exp(cend) + _bdot_h(iv, k * _exp(cend - cum), TN)
    res = _rms(out, gn) * _silu(gp)
    return res, st_new


def _hgrn_heads(qs, fs, ivs, gs, lbs, gn, sts):
    res, st_new = _hgrn_chunk(_Heads(qs), _Heads(fs), _Heads(ivs), _Heads(gs), _Heads(lbs), gn, _Heads(sts))
    return res.vals, st_new.vals


def _lanes(j):
    return slice(j * HEAD, (j + 1) * HEAD)


def _hgrn_fwd(proj, lb, gn, *, heads, name, comm=()):
    s = proj.shape[0]
    n = s // CHUNK
    hpb = min(HEADS_PER_STEP, heads)
    assert heads % hpb == 0

    def body(q_ref, f_ref, i_ref, g_ref, lb_ref, gn_ref, o_ref, st_ref, state):
        @pl.when(pl.program_id(1) == 0)
        def _():
            state[...] = jnp.zeros_like(state)

        gnv = gn_ref[...]
        loaded = [(q_ref[:, _lanes(j)], f_ref[:, _lanes(j)], i_ref[:, _lanes(j)], g_ref[:, _lanes(j)],
                   lb_ref[:, _lanes(j)], state[j]) for j in range(hpb)]
        qs, fs, ivs, gs, lbs, sts = (list(t) for t in zip(*loaded))
        res, st_new = _hgrn_heads(qs, fs, ivs, gs, lbs, gnv, sts)
        for j in range(hpb):
            st_ref[j] = sts[j]
            o_ref[:, _lanes(j)] = res[j].astype(o_ref.dtype)
            state[j] = st_new[j]

    wide = hpb * HEAD
    blk = lambda off: pl.BlockSpec((CHUNK, wide), lambda h, c: (c, off // hpb + h))
    return _call(
        body, (proj, proj, proj, proj, lb, gn), name=name, grid=(heads // hpb, n),
        in_specs=[blk(0), blk(heads), blk(2 * heads), blk(3 * heads),
                  pl.BlockSpec((1, wide), lambda h, c: (0, h)), pl.BlockSpec((1, HEAD), lambda h, c: (0, 0))],
        out_specs=[pl.BlockSpec((CHUNK, wide), lambda h, c: (c, h)),
                   pl.BlockSpec((hpb, None, HEAD, HEAD), lambda h, c: (h, c, 0, 0))],
        out_shape=[jax.ShapeDtypeStruct((s, heads * HEAD), BF16), jax.ShapeDtypeStruct((heads, n, HEAD, HEAD), F32)],
        scratch_shapes=[pltpu.VMEM((hpb, HEAD, HEAD), F32)], comm=comm)


def _hgrn_bwd(proj, lb, gn, states, d_out, *, heads, name):
    s = proj.shape[0]
    n = s // CHUNK
    hpb = min(HEADS_PER_STEP, heads)

    def body(q_ref, f_ref, i_ref, g_ref, lb_ref, gn_ref, st_ref, do_ref,
             dq_ref, df_ref, di_ref, dg_ref, dlb_ref, dgn_ref, dstate):
        h, c = pl.program_id(0), pl.program_id(1)

        @pl.when(c == 0)
        def _():
            dstate[...] = jnp.zeros_like(dstate)
            dlb_ref[...] = jnp.zeros_like(dlb_ref)

        @pl.when((c == 0) & (h == 0))
        def _():
            dgn_ref[...] = jnp.zeros_like(dgn_ref)

        gnv = gn_ref[...]
        loaded = [(q_ref[:, _lanes(j)], f_ref[:, _lanes(j)], i_ref[:, _lanes(j)], g_ref[:, _lanes(j)],
                   lb_ref[:, _lanes(j)], st_ref[j], do_ref[:, _lanes(j)].astype(F32), dstate[j]) for j in range(hpb)]
        qs, fs, ivs, gs, lbs, sts, dos, dss = (list(t) for t in zip(*loaded))
        _, vjp = jax.vjp(_hgrn_heads, qs, fs, ivs, gs, lbs, gnv, sts)
        dqs, dfs, dis, dgs, dlbs, dgn_sum, dsts = vjp((dos, dss))
        for j in range(hpb):
            ln = _lanes(j)
            dq_ref[:, ln] = dqs[j].astype(dq_ref.dtype)
            df_ref[:, ln] = dfs[j].astype(df_ref.dtype)
            di_ref[:, ln] = dis[j].astype(di_ref.dtype)
            dg_ref[:, ln] = dgs[j].astype(dg_ref.dtype)
            dlb_ref[:, ln] += dlbs[j]
            dstate[j] = dsts[j]
        dgn_ref[...] += dgn_sum

    wide = hpb * HEAD
    rev = lambda off: pl.BlockSpec((CHUNK, wide), lambda h, c: (n - 1 - c, off // hpb + h))
    out_blk = pl.BlockSpec((CHUNK, wide), lambda h, c: (n - 1 - c, h))
    width = heads * HEAD
    return pl.pallas_call(
        body, name=name, grid=(heads // hpb, n),
        in_specs=[rev(0), rev(heads), rev(2 * heads), rev(3 * heads),
                  pl.BlockSpec((1, wide), lambda h, c: (0, h)), pl.BlockSpec((1, HEAD), lambda h, c: (0, 0)),
                  pl.BlockSpec((hpb, None, HEAD, HEAD), lambda h, c: (h, n - 1 - c, 0, 0)), out_blk],
        out_specs=[out_blk, out_blk, out_blk, out_blk,
                   pl.BlockSpec((1, wide), lambda h, c: (0, h)), pl.BlockSpec((1, HEAD), lambda h, c: (0, 0))],
        out_shape=[jax.ShapeDtypeStruct((s, width), BF16)] * 4
        + [jax.ShapeDtypeStruct((1, width), F32), jax.ShapeDtypeStruct((1, HEAD), F32)],
        scratch_shapes=[pltpu.VMEM((hpb, HEAD, HEAD), F32)],
        compiler_params=_params(2),
    )(proj, proj, proj, proj, lb, gn, states, d_out)


def _shift_rows(x, d, rowi):
    if d == 0:
        return x
    n = x.shape[0]
    rolled = pltpu.roll(x, d % n, 0)
    keep = rowi >= d if d > 0 else rowi < n + d
    return jnp.where(keep, rolled, 0.0)


def _conv_pre(x, w_ref, rowi):
    acc = None
    for j in range(CONV_WIDTH):
        term = w_ref[j:j + 1, :] * _shift_rows(x, CONV_WIDTH - 1 - j, rowi)
        acc = term if acc is None else acc + term
    return acc


def _conv_fwd(proj, w, *, col_off, name, cb=256):
    s = proj.shape[0]
    width = w.shape[1]
    cb = min(cb, width)

    def body(x_ref, w_ref, o_ref):
        rowi = lax.broadcasted_iota(jnp.int32, (s, cb), 0)
        o_ref[...] = _silu(_conv_pre(x_ref[...], w_ref, rowi))

    return pl.pallas_call(
        body, name=name, grid=(width // cb,),
        in_specs=[pl.BlockSpec((s, cb), lambda j: (0, col_off // cb + j)), pl.BlockSpec((CONV_WIDTH, cb), lambda j: (0, j))],
        out_specs=pl.BlockSpec((s, cb), lambda j: (0, j)),
        out_shape=jax.ShapeDtypeStruct((s, width), F32),
        compiler_params=_params(1),
    )(proj, w)


def _conv_bwd(proj, w, d_out, *, col_off, name, cb=256):
    s = proj.shape[0]
    width = w.shape[1]
    cb = min(cb, width)

    def body(x_ref, w_ref, do_ref, dx_ref, dw_ref):
        rowi = lax.broadcasted_iota(jnp.int32, (s, cb), 0)
        x = x_ref[...]
        pre = _conv_pre(x, w_ref, rowi)
        sg = jax.nn.sigmoid(pre)
        dpre = do_ref[...] * (sg + pre * sg * (1.0 - sg))
        dx = None
        for j in range(CONV_WIDTH):
            d = CONV_WIDTH - 1 - j
            term = w_ref[j:j + 1, :] * _shift_rows(dpre, -d, rowi)
            dx = term if dx is None else dx + term
            dw_ref[j:j + 1, :] = jnp.sum(dpre * _shift_rows(x, d, rowi), axis=0, keepdims=True)
        dx_ref[...] = dx.astype(dx_ref.dtype)

    return pl.pallas_call(
        body, name=name, grid=(width // cb,),
        in_specs=[pl.BlockSpec((s, cb), lambda j: (0, col_off // cb + j)), pl.BlockSpec((CONV_WIDTH, cb), lambda j: (0, j)),
                  pl.BlockSpec((s, cb), lambda j: (0, j))],
        out_specs=[pl.BlockSpec((s, cb), lambda j: (0, j)), pl.BlockSpec((CONV_WIDTH, cb), lambda j: (0, j))],
        out_shape=[jax.ShapeDtypeStruct((s, width), BF16), jax.ShapeDtypeStruct((CONV_WIDTH, width), F32)],
        compiler_params=_params(1),
    )(proj, w, d_out)


def _delta_chunk(h, heads, qr, kr, vr, ab, zp, alog, dtb, gn, st):
    c = CHUNK
    row = lax.broadcasted_iota(jnp.int32, (c, c), 0)
    col = lax.broadcasted_iota(jnp.int32, (c, c), 1)
    causal = row >= col
    strict = row > col
    lane = lax.broadcasted_iota(jnp.int32, (c, HEAD), 1)
    mine = _equal(h, lane)
    la_full = -jnp.exp(alog) * _softplus(ab + dtb)
    cum_full = _hdot(causal.astype(F32), la_full)
    cum = _sum(_where(mine, cum_full, 0.0), axis=1, keepdims=True)
    cend = _sum(_sum(_where(mine, la_full, 0.0), axis=1, keepdims=True), axis=0, keepdims=True)
    beta = _sum(_where(_equal(heads + h, lane), jax.nn.sigmoid(ab), 0.0), axis=1, keepdims=True)
    cum_row = _hdot_h(_where(mine, 1.0, 0.0), cum_full, NT)
    decay = _where(causal, _exp(_where(causal, cum - cum_row, 0.0)), 0.0)
    qn = qr * _rsqrt(_sum(qr * qr, axis=-1, keepdims=True) + NORM_EPS) * (HEAD ** -0.5)
    kn = kr * _rsqrt(_sum(kr * kr, axis=-1, keepdims=True) + NORM_EPS)
    kb = kn * beta
    lower = _where(strict, _bdot_h(kb, kn, NT) * decay, 0.0)
    inv = (row == col).astype(F32)
    lvl = 0
    while (1 << lvl) < c:
        same_pair = (row >> (lvl + 1)) == (col >> (lvl + 1))
        off_block = same_pair & (((row >> lvl) & 1) == 1) & (((col >> lvl) & 1) == 0)
        inv = inv - _hdot_h(_hdot_h(inv, _where(off_block, lower, 0.0), NN), inv, NN)
        lvl += 1
    ecum = _exp(cum)
    u = _hdot_h(inv, vr * beta, NN)
    w = _hdot_h(inv, kb * ecum, NN)
    intra = _bdot_h(qn, kn, NT) * decay
    v_new = u - _bdot_h(w, st, NN)
    out = _bdot_h(qn * ecum, st, NN) + _bdot_h(intra, v_new, NN)
    st_new = st * _exp(cend) + _bdot_h(kn * _exp(cend - cum), v_new, TN)
    res = _rms(out, gn) * _silu(zp)
    return res, st_new


def _delta_heads(hs, heads, qs, ks, vs, ab, zs, alog, dtb, gn, sts):
    res, st_new = _delta_chunk(_Heads(hs), heads, _Heads(qs), _Heads(ks), _Heads(vs), ab, _Heads(zs), alog, dtb, gn,
                               _Heads(sts))
    return res.vals, st_new.vals


def _delta_fwd(qkv, ab, proj, hp, gn, *, heads, z_off, name, comm=()):
    s = qkv.shape[0]
    n = s // CHUNK

    hpb = min(HEADS_PER_STEP, heads)
    assert heads % hpb == 0 and z_off % hpb == 0

    def body(q_ref, k_ref, v_ref, ab_ref, z_ref, hp_ref, gn_ref, o_ref, st_ref, state):
        hb = pl.program_id(1)

        @pl.when(pl.program_id(0) == 0)
        def _():
            for j in range(hpb):
                state[hb * hpb + j] = jnp.zeros((HEAD, HEAD), F32)

        shared = (ab_ref[...], hp_ref[0:1, :], hp_ref[1:2, :], gn_ref[...])
        loaded = [(q_ref[:, _lanes(j)], k_ref[:, _lanes(j)], v_ref[:, _lanes(j)], z_ref[:, _lanes(j)],
                   state[hb * hpb + j]) for j in range(hpb)]
        qs, ks, vs, zs, sts = (list(t) for t in zip(*loaded))
        res, st_new = _delta_heads([hb * hpb + j for j in range(hpb)], heads, qs, ks, vs, shared[0], zs, shared[1],
                                   shared[2], shared[3], sts)
        for j in range(hpb):
            st_ref[j] = sts[j]
            o_ref[:, _lanes(j)] = res[j].astype(o_ref.dtype)
            state[hb * hpb + j] = st_new[j]

    wide = hpb * HEAD
    blk = lambda off: pl.BlockSpec((CHUNK, wide), lambda c, h: (c, off // hpb + h))
    return _call(
        body, (qkv, qkv, qkv, ab, proj, hp, gn), name=name, grid=(n, heads // hpb),
        in_specs=[blk(0), blk(heads), blk(2 * heads), pl.BlockSpec((CHUNK, HEAD), lambda c, h: (c, 0)), blk(z_off),
                  pl.BlockSpec((8, HEAD), lambda c, h: (0, 0)), pl.BlockSpec((1, HEAD), lambda c, h: (0, 0))],
        out_specs=[pl.BlockSpec((CHUNK, wide), lambda c, h: (c, h)),
                   pl.BlockSpec((hpb, None, HEAD, HEAD), lambda c, h: (h, c, 0, 0))],
        out_shape=[jax.ShapeDtypeStruct((s, heads * HEAD), BF16), jax.ShapeDtypeStruct((heads, n, HEAD, HEAD), F32)],
        scratch_shapes=[pltpu.VMEM((heads, HEAD, HEAD), F32)], comm=comm)


def _delta_bwd(qkv, ab, proj, hp, gn, states, d_out, *, heads, z_off, name, comm=()):
    s = qkv.shape[0]
    n = s // CHUNK
    hpb = min(HEADS_PER_STEP, heads)

    def body(q_ref, k_ref, v_ref, ab_ref, z_ref, hp_ref, gn_ref, st_ref, do_ref,
             dq_ref, dk_ref, dv_ref, dab_ref, dz_ref, dhp_ref, dgn_ref, dstate):
        c, hb = pl.program_id(0), pl.program_id(1)

        @pl.when(c == 0)
        def _():
            for j in range(hpb):
                dstate[hb * hpb + j] = jnp.zeros((HEAD, HEAD), F32)

        @pl.when((c == 0) & (hb == 0))
        def _():
            dgn_ref[...] = jnp.zeros_like(dgn_ref)
            dhp_ref[...] = jnp.zeros_like(dhp_ref)

        @pl.when(hb == 0)
        def _():
            dab_ref[...] = jnp.zeros_like(dab_ref)

        shared = (ab_ref[...], hp_ref[0:1, :], hp_ref[1:2, :], gn_ref[...])
        loaded = [(q_ref[:, _lanes(j)], k_ref[:, _lanes(j)], v_ref[:, _lanes(j)], z_ref[:, _lanes(j)], st_ref[j],
                   do_ref[:, _lanes(j)].astype(F32), dstate[hb * hpb + j]) for j in range(hpb)]
        qs, ks, vs, zs, sts, dos, dss = (list(t) for t in zip(*loaded))
        fn = functools.partial(_delta_heads, [hb * hpb + j for j in range(hpb)], heads)
        _, vjp = jax.vjp(fn, qs, ks, vs, shared[0], zs, shared[1], shared[2], shared[3], sts)
        dqs, dks, dvs, dab, dzs, dal, ddt, dgn, dsts = vjp((dos, dss))
        for j in range(hpb):
            ln = _lanes(j)
            dq_ref[:, ln] = dqs[j]
            dk_ref[:, ln] = dks[j]
            dv_ref[:, ln] = dvs[j]
            dz_ref[:, ln] = dzs[j].astype(dz_ref.dtype)
            dstate[hb * hpb + j] = dsts[j]
        dab_ref[...] += dab
        dhp_ref[0:1, :] += dal
        dhp_ref[1:2, :] += ddt
        dgn_ref[...] += dgn

    wide = hpb * HEAD
    rev = lambda off: pl.BlockSpec((CHUNK, wide), lambda c, h: (n - 1 - c, off // hpb + h))
    width = heads * HEAD
    head_blk = pl.BlockSpec((CHUNK, wide), lambda c, h: (n - 1 - c, h))
    ab_blk = pl.BlockSpec((CHUNK, HEAD), lambda c, h: (n - 1 - c, 0))
    return _call(
        body, (qkv, qkv, qkv, ab, proj, hp, gn, states, d_out), name=name, grid=(n, heads // hpb),
        in_specs=[rev(0), rev(heads), rev(2 * heads), ab_blk, rev(z_off),
                  pl.BlockSpec((8, HEAD), lambda c, h: (0, 0)), pl.BlockSpec((1, HEAD), lambda c, h: (0, 0)),
                  pl.BlockSpec((hpb, None, HEAD, HEAD), lambda c, h: (h, n - 1 - c, 0, 0)), head_blk],
        out_specs=[head_blk, head_blk, head_blk, ab_blk, head_blk,
                   pl.BlockSpec((8, HEAD), lambda c, h: (0, 0)), pl.BlockSpec((1, HEAD), lambda c, h: (0, 0))],
        out_shape=[jax.ShapeDtypeStruct((s, width), F32)] * 3
        + [jax.ShapeDtypeStruct((s, HEAD), F32), jax.ShapeDtypeStruct((s, width), BF16),
           jax.ShapeDtypeStruct((8, HEAD), F32), jax.ShapeDtypeStruct((1, HEAD), F32)],
        scratch_shapes=[pltpu.VMEM((heads, HEAD, HEAD), F32)], comm=comm)


def _s5_scan(buf, lt_ref, cin_r, cin_i, tt, reverse):
    nblk = tt // 8
    hl = S5_HALF
    rowi = lax.broadcasted_iota(jnp.int32, (8, hl), 0)
    sign = -1.0 if reverse else 1.0

    def body(j, carry):
        cr, ci = carry
        off = pl.multiple_of((nblk - 1 - j if reverse else j) * 8, 8)
        xr = buf[pl.ds(off, 8), 0:hl]
        xi = buf[pl.ds(off, 8), hl:2 * hl]
        for lv, d in enumerate((1, 2, 4)):
            ar, ai = lt_ref[2 * lv], sign * lt_ref[2 * lv + 1]
            sr = _shift_rows(xr, -d if reverse else d, rowi)
            si = _shift_rows(xi, -d if reverse else d, rowi)
            xr, xi = xr + ar * sr - ai * si, xi + ar * si + ai * sr
        pr, pi = (lt_ref[8], -lt_ref[9]) if reverse else (lt_ref[6], lt_ref[7])
        xr, xi = xr + pr * cr - pi * ci, xi + pr * ci + pi * cr
        buf[pl.ds(off, 8), 0:hl] = xr
        buf[pl.ds(off, 8), hl:2 * hl] = xi
        edge = rowi == (0 if reverse else 7)
        return (jnp.sum(jnp.where(edge, xr, 0.0), axis=0, keepdims=True),
                jnp.sum(jnp.where(edge, xi, 0.0), axis=0, keepdims=True))

    return lax.fori_loop(0, nblk, body, (cin_r, cin_i))


def _s5_fwd(u, wb, wc, lt, dskip, *, name, tt=512):
    s, d = u.shape
    nb = d // HEAD
    tt = min(tt, s)
    nt = s // tt
    hl = S5_HALF

    def body(u_ref, wb_ref, wc_ref, lt_ref, d_ref, y_ref, cin_ref, st_ref, buf, carry):
        @pl.when(pl.program_id(1) == 0)
        def _():
            carry[...] = jnp.zeros_like(carry)

        cin_ref[...] = carry[0:1, :]
        uv = u_ref[...]
        buf[...] = _bdot_raw(uv, wb_ref[...])
        cr, ci = _s5_scan(buf, lt_ref, carry[0:1, 0:hl], carry[0:1, hl:2 * hl], tt, False)
        carry[0:1, 0:hl] = cr
        carry[0:1, hl:2 * hl] = ci
        states = buf[...].astype(BF16)
        st_ref[...] = states
        y_ref[...] = _bdot_raw(states, wc_ref[...]) + d_ref[...] * uv

    return pl.pallas_call(
        body, name=name, grid=(nb, nt),
        in_specs=[pl.BlockSpec((tt, HEAD), lambda b, t: (t, b)),
                  pl.BlockSpec((None, HEAD, 2 * hl), lambda b, t: (b, 0, 0)),
                  pl.BlockSpec((None, 2 * hl, HEAD), lambda b, t: (b, 0, 0)),
                  pl.BlockSpec((None, 10, 8, hl), lambda b, t: (b, 0, 0, 0)),
                  pl.BlockSpec((1, HEAD), lambda b, t: (0, b))],
        out_specs=[pl.BlockSpec((tt, HEAD), lambda b, t: (t, b)),
                   pl.BlockSpec((None, None, 1, 2 * hl), lambda b, t: (b, t, 0, 0)),
                   pl.BlockSpec((tt, 2 * hl), lambda b, t: (t, b))],
        out_shape=[jax.ShapeDtypeStruct((s, d), F32), jax.ShapeDtypeStruct((nb, nt, 1, 2 * hl), F32),
                   jax.ShapeDtypeStruct((s, nb * 2 * hl), BF16)],
        scratch_shapes=[pltpu.VMEM((tt, 2 * hl), F32), pltpu.VMEM((8, 2 * hl), F32)],
        compiler_params=_params(2),
    )(u, wb, wc, lt, dskip)


def _s5_bwd(u, dy, wb, wc, lt, dskip, cins, states, *, name, tt=512, comm=()):
    s, d = u.shape
    nb = d // HEAD
    tt = min(tt, s)
    nt = s // tt
    hl = S5_HALF

    def body(u_ref, dy_ref, wb_ref, wc_ref, lt_ref, d_ref, cin_ref, st_ref,
             du_ref, dwb_ref, dwc_ref, dd_ref, dlam_ref, abuf, acarry):
        @pl.when(pl.program_id(1) == 0)
        def _():
            acarry[...] = jnp.zeros_like(acarry)
            dwb_ref[...] = jnp.zeros_like(dwb_ref)
            dwc_ref[...] = jnp.zeros_like(dwc_ref)
            dd_ref[...] = jnp.zeros_like(dd_ref)
            dlam_ref[...] = jnp.zeros_like(dlam_ref)

        uv, dyv = u_ref[...], dy_ref[...]
        abuf[...] = _bdot_raw(dyv, wc_ref[...], NT)
        ar, ai = _s5_scan(abuf, lt_ref, acarry[0:1, 0:hl], acarry[0:1, hl:2 * hl], tt, True)
        acarry[0:1, 0:hl] = ar
        acarry[0:1, hl:2 * hl] = ai
        du_ref[...] = _bdot_raw(abuf[...], wb_ref[...], NT) + d_ref[...] * dyv
        dwb_ref[...] += _bdot_raw(uv, abuf[...], TN)
        dwc_ref[...] += _bdot_raw(st_ref[...], dyv, TN)
        dd_ref[...] += jnp.sum(dyv * uv, axis=0, keepdims=True)
        first = lax.broadcasted_iota(jnp.int32, (tt, hl), 0) == 0
        spr = jnp.where(first, cin_ref[:, 0:hl], pltpu.roll(st_ref[:, 0:hl].astype(F32), 1, 0))
        spi = jnp.where(first, cin_ref[:, hl:2 * hl], pltpu.roll(st_ref[:, hl:2 * hl].astype(F32), 1, 0))
        avr, avi = abuf[:, 0:hl], abuf[:, hl:2 * hl]
        dlam_ref[:, 0:hl] += jnp.sum(avr * spr + avi * spi, axis=0, keepdims=True)
        dlam_ref[:, hl:2 * hl] += jnp.sum(avi * spr - avr * spi, axis=0, keepdims=True)

    rev = pl.BlockSpec((tt, HEAD), lambda b, t: (nt - 1 - t, b))
    return _call(
        body, (u, dy, wb, wc, lt, dskip, cins, states), name=name, grid=(nb, nt),
        in_specs=[rev, rev,
                  pl.BlockSpec((None, HEAD, 2 * hl), lambda b, t: (b, 0, 0)),
                  pl.BlockSpec((None, 2 * hl, HEAD), lambda b, t: (b, 0, 0)),
                  pl.BlockSpec((None, 10, 8, hl), lambda b, t: (b, 0, 0, 0)),
                  pl.BlockSpec((1, HEAD), lambda b, t: (0, b)),
                  pl.BlockSpec((None, None, 1, 2 * hl), lambda b, t: (b, nt - 1 - t, 0, 0)),
                  pl.BlockSpec((tt, 2 * hl), lambda b, t: (nt - 1 - t, b))],
        out_specs=[rev,
                   pl.BlockSpec((None, HEAD, 2 * hl), lambda b, t: (b, 0, 0)),
                   pl.BlockSpec((None, 2 * hl, HEAD), lambda b, t: (b, 0, 0)),
                   pl.BlockSpec((1, HEAD), lambda b, t: (0, b)),
                   pl.BlockSpec((None, 1, 2 * hl), lambda b, t: (b, 0, 0))],
        out_shape=[jax.ShapeDtypeStruct((s, d), F32), jax.ShapeDtypeStruct(wb.shape, F32),
                   jax.ShapeDtypeStruct(wc.shape, F32), jax.ShapeDtypeStruct((1, d), F32),
                   jax.ShapeDtypeStruct((nb, 1, 2 * hl), F32)],
        scratch_shapes=[pltpu.VMEM((tt, 2 * hl), F32), pltpu.VMEM((8, 2 * hl), F32)],
        comm=comm)


def _s5_pack(lr, li, br, bi, c_re, c_im):
    g = lr.shape[0]
    nb = g // S5_GB
    eye = jnp.eye(S5_GB, dtype=F32)
    bm = jnp.stack([br, bi]).reshape(2, nb, S5_GB, S5_STATE, S5_GROUP)
    wb = jnp.einsum("rbgpc,gh->bgcrhp", bm, eye).reshape(nb, HEAD, 2 * S5_HALF)
    cm = jnp.stack([c_re, -c_im]).reshape(2, nb, S5_GB, S5_GROUP, S5_STATE)
    wc = jnp.einsum("rbgcp,gh->brgphc", cm, eye).reshape(nb, 2 * S5_HALF, HEAD)
    pw = [(lr, li)]
    for _ in range(7):
        pr, pi = pw[-1]
        pw.append((pr * lr - pi * li, pr * li + pi * lr))
    blk = lambda a: a.reshape(nb, 1, S5_HALF)
    rows8 = lambda a: jnp.broadcast_to(blk(a), (nb, 8, S5_HALF))
    tables = []
    for n in (1, 2, 4):
        tables += [rows8(pw[n - 1][0]), rows8(pw[n - 1][1])]
    for order in (range(8), range(7, -1, -1)):
        tables += [jnp.concatenate([blk(pw[n][0]) for n in order], axis=1),
                   jnp.concatenate([blk(pw[n][1]) for n in order], axis=1)]
    return wb, wc, jnp.stack(tables, axis=1)


def _s5_unpack(dwb, dwc, dlam):
    nb = dwb.shape[0]
    g = nb * S5_GB
    eye = jnp.eye(S5_GB, dtype=F32)
    db = jnp.einsum("bgcrhp,gh->rbgpc", dwb.reshape(nb, S5_GB, S5_GROUP, 2, S5_GB, S5_STATE), eye)
    db = db.reshape(2, g, S5_STATE * S5_GROUP)
    dc = jnp.einsum("brgphc,gh->rbgcp", dwc.reshape(nb, 2, S5_GB, S5_STATE, S5_GB, S5_GROUP), eye)
    dc = dc.reshape(2, g, S5_GROUP, S5_STATE)
    dl = dlam.reshape(nb, 2, S5_GB, S5_STATE).transpose(1, 0, 2, 3).reshape(2, g, S5_STATE)
    return dl[0], dl[1], db[0], db[1], dc[0], -dc[1]


def _peer(r):
    mx, my, mc = lax.axis_index("x"), lax.axis_index("y"), lax.axis_index("c")
    px = 1 - mx if r & 4 else mx
    py = 1 - my if r & 2 else my
    pc = 1 - mc if r & 1 else mc
    return (px, py, pc), 4 * px + 2 * py + pc


_COMM_SCRATCH = [pltpu.SemaphoreType.DMA((N_DEV - 1,)), pltpu.SemaphoreType.DMA((N_DEV - 1,)), pltpu.SemaphoreType.DMA]


class _AllToAll:
    def __init__(self, x):
        self.x = x
        self.out_shape = jax.ShapeDtypeStruct(x.shape, x.dtype)

    def _copies(self, x_ref, out_ref, send_sems, recv_sems, local_sem):
        _, me = _peer(0)
        mine = pltpu.make_async_copy(x_ref.at[me], out_ref.at[me], local_sem)
        sends, recvs = [], []
        for r in range(1, N_DEV):
            pos, idx = _peer(r)
            sems = dict(send_sem=send_sems.at[r - 1], recv_sem=recv_sems.at[r - 1], device_id=pos, device_id_type=MESH)
            sends.append(pltpu.make_async_remote_copy(src_ref=x_ref.at[idx], dst_ref=out_ref.at[me], **sems))
            recvs.append(pltpu.make_async_remote_copy(src_ref=x_ref.at[idx], dst_ref=out_ref.at[idx], **sems))
        return mine, sends, recvs

    def start(self, *refs):
        mine, sends, _ = self._copies(*refs)
        mine.start()
        for cp in sends:
            cp.start()

    def finish(self, *refs):
        mine, sends, recvs = self._copies(*refs)
        for cp in recvs:
            cp.wait_recv()
        for cp in sends:
            cp.wait_send()
        mine.wait()


class _Gather:
    def __init__(self, x):
        self.x = x
        self.out_shape = jax.ShapeDtypeStruct((N_DEV,) + tuple(x.shape), x.dtype)

    def _copies(self, x_ref, out_ref, send_sems, recv_sems, local_sem):
        mx, my, mc = lax.axis_index("x"), lax.axis_index("y"), lax.axis_index("c")
        me, sibling = (mx, my, mc), (mx, my, 1 - mc)
        chips = [(1 - mx, my), (mx, 1 - my), (1 - mx, 1 - my)]

        def slot(px, py, pc):
            return out_ref.at[4 * px + 2 * py + pc]

        def copy(k, block, to, src=None):
            return pltpu.make_async_remote_copy(
                src_ref=slot(*block) if src is None else src, dst_ref=slot(*block),
                send_sem=send_sems.at[k], recv_sem=recv_sems.at[k], device_id=to, device_id_type=MESH)

        return dict(
            mine=pltpu.make_async_copy(x_ref, slot(*me), local_sem),
            first=[copy(0, me, sibling, src=x_ref)] + [copy(1 + j, me, (*chip, mc), src=x_ref) for j, chip in enumerate(chips)],
            passed=[copy(4 + j, (*chip, mc), sibling) for j, chip in enumerate(chips)],
            over_ici=[copy(1 + j, (*chip, mc), me) for j, chip in enumerate(chips)],
            from_sibling=[copy(0, sibling, me)] + [copy(4 + j, (*chip, 1 - mc), me) for j, chip in enumerate(chips)])

    def start(self, *refs):
        cps = self._copies(*refs)
        cps["mine"].start()
        for cp in cps["first"]:
            cp.start()

    def finish(self, *refs):
        cps = self._copies(*refs)
        for arrived, onward in zip(cps["over_ici"], cps["passed"]):
            arrived.wait_recv()
            onward.start()
        for cp in cps["from_sibling"]:
            cp.wait_recv()
        for cp in cps["first"] + cps["passed"]:
            cp.wait_send()
        cps["mine"].wait()


def _call(body, args, *, name, grid, in_specs, out_specs, out_shape, scratch_shapes=(), comm=()):
    n_in, n_out, n_scr, nc = len(in_specs), len(out_shape), len(scratch_shapes), len(comm)

    def wrapped(*refs):
        ins, c_in = refs[:n_in], refs[n_in:n_in + nc]
        outs, c_out = refs[n_in + nc:n_in + nc + n_out], refs[n_in + nc + n_out:n_in + 2 * nc + n_out]
        scr = refs[n_in + 2 * nc + n_out:n_in + 2 * nc + n_out + n_scr]
        sems = refs[n_in + 2 * nc + n_out + n_scr:]
        ids = [pl.program_id(a) for a in range(len(grid))]
        if nc:
            @pl.when(functools.reduce(operator.and_, [i == 0 for i in ids]))
            def _():
                for k, op in enumerate(comm):
                    op.start(c_in[k], c_out[k], *sems[3 * k:3 * k + 3])

        body(*ins, *outs, *scr)
        if nc:
            @pl.when(functools.reduce(operator.and_, [i == g - 1 for i, g in zip(ids, grid)]))
            def _():
                for k, op in enumerate(comm):
                    op.finish(c_in[k], c_out[k], *sems[3 * k:3 * k + 3])

    any_spec = pl.BlockSpec(memory_space=pl.ANY)
    res = pl.pallas_call(
        wrapped, name=name, grid=grid,
        in_specs=list(in_specs) + [any_spec] * nc, out_specs=list(out_specs) + [any_spec] * nc,
        out_shape=list(out_shape) + [op.out_shape for op in comm],
        scratch_shapes=list(scratch_shapes) + list(_COMM_SCRATCH) * nc,
        compiler_params=_params(len(grid)),
    )(*args, *[op.x for op in comm])
    return list(res[:n_out]), list(res[n_out:])


def _comm_call(op, *, name):
    return _call(lambda: None, (), name=name, grid=(1,), in_specs=[], out_specs=[], out_shape=[], comm=(op,))[1][0]


def _adamw(w, parts, m, v, *, name, tr=128):
    nl, r, c = w.shape
    assert len(parts) == nl
    npart = parts[0].shape[0]
    tr = min(tr, r)
    assert r % tr == 0, (name, r)

    def body(w_ref, m_ref, v_ref, *rest):
        p_refs, (g_ref, d_ref, mo_ref, vo_ref) = rest[:nl], rest[nl:]
        for l in range(nl):
            @pl.when(pl.program_id(0) == l)
            def _():
                g = p_refs[l][0].astype(F32)
                for k in range(1, npart):
                    g = g + p_refs[l][k].astype(F32)
                m2 = ADAM_B1 * m_ref[...] + (1.0 - ADAM_B1) * g
                v2 = ADAM_B2 * v_ref[...] + (1.0 - ADAM_B2) * (g * g)
                m_hat = m2 / (1.0 - ADAM_B1 ** ADAM_STEP)
                v_hat = v2 / (1.0 - ADAM_B2 ** ADAM_STEP)
                g_ref[...] = g
                d_ref[...] = -ADAM_LR * (m_hat / (jnp.sqrt(v_hat) + ADAM_EPS) + ADAM_WD * w_ref[...])
                mo_ref[...] = m2
                vo_ref[...] = v2

    blk = pl.BlockSpec((None, tr, c), lambda l, i: (l, i, 0))
    part_spec = lambda k: pl.BlockSpec((npart, tr, c), lambda l, i: (0, jnp.where(l == k, i, 0), 0))
    return pl.pallas_call(
        body, name=name, grid=(nl, r // tr),
        in_specs=[blk, blk, blk] + [part_spec(k) for k in range(nl)],
        out_specs=[blk] * 4, out_shape=[jax.ShapeDtypeStruct((nl, r, c), F32)] * 4,
        compiler_params=_params(2),
    )(w, m, v, *parts)


def _sum_parts(parts, *, name):
    npart = parts.shape[0]

    def body(p_ref, o_ref):
        g = p_ref[0]
        for k in range(1, npart):
            g = g + p_ref[k]
        o_ref[...] = g

    return pl.pallas_call(
        body, name=name, grid=(1,), in_specs=[_full_spec(parts)],
        out_specs=pl.BlockSpec(parts.shape[1:], lambda i: (0, 0)),
        out_shape=jax.ShapeDtypeStruct(parts.shape[1:], F32), compiler_params=_params(1),
    )(parts)


def _pack(arrs):
    flat = jnp.concatenate([a.reshape(-1).astype(F32) for a in arrs])
    pad = (-flat.shape[0]) % (HEAD * HEAD)
    return jnp.pad(flat, (0, pad)).reshape(-1, HEAD)


def _unpack(packed, shapes):
    flat = packed.reshape(-1)
    out, off = [], 0
    for shp in shapes:
        size = math.prod(shp)
        out.append(flat[off:off + size].reshape(shp))
        off += size
    return out


def _add_epilogue(acc, res):
    return (acc + res,)


def _relu2_epilogue(acc):
    r = jnp.maximum(acc, 0.0)
    return acc, r * r


def _ple_epilogue(acc, gpre, h):
    return h + jax.nn.sigmoid(gpre) * acc, acc


def kernel(x, p, norm_mix, norm_mlp, norm_ple, w_in_e, w_out_e, hgrn_lb, g_norm_a, conv_w, a_log, dt_bias, g_norm_b, s5_a_re, s5_a_im, s5_b_re, s5_b_im, s5_c_re, s5_c_im, s5_d, s5_log_dt, w_glu, b_glu, w_out_o, w_up, w_down, w_ple_gate, w_ple_proj, final_norm, loss_target, m_norm_mix, m_norm_mlp, m_norm_ple, m_w_in_e, m_w_out_e, m_hgrn_lb, m_g_norm_a, m_conv_w, m_a_log, m_dt_bias, m_g_norm_b, m_s5_a_re, m_s5_a_im, m_s5_b_re, m_s5_b_im, m_s5_c_re, m_s5_c_im, m_s5_d, m_s5_log_dt, m_w_glu, m_b_glu, m_w_out_o, m_w_up, m_w_down, m_w_ple_gate, m_w_ple_proj, m_final_norm, v_norm_mix, v_norm_mlp, v_norm_ple, v_w_in_e, v_w_out_e, v_hgrn_lb, v_g_norm_a, v_conv_w, v_a_log, v_dt_bias, v_g_norm_b, v_s5_a_re, v_s5_a_im, v_s5_b_re, v_s5_b_im, v_s5_c_re, v_s5_c_im, v_s5_d, v_s5_log_dt, v_w_glu, v_b_glu, v_w_out_o, v_w_up, v_w_down, v_w_ple_gate, v_w_ple_proj, v_final_norm):
    args = dict(locals())
    s, d = x.shape[1], x.shape[2]
    aw = d // 2
    ha = hb = aw // HEAD
    main = 4 * d
    z_col = 2 * d + 3 * aw
    ff = w_up.shape[2] * N_DEV
    ple = p.shape[-1]
    groups = d // S5_GROUP
    me = 4 * lax.axis_index("x") + 2 * lax.axis_index("y") + lax.axis_index("c")
    x2, target = x[0], loss_target[0]
    row = lambda a, i: a[i:i + 1]

    def gather_of(w):
        return _Gather(w.astype(BF16))

    w_in = jnp.transpose(_comm_call(gather_of(w_in_e[0]), name="ag_w_in"), (1, 0, 2)).reshape(d, -1)
    w_main = w_in[:, :main]
    w_tail = jnp.pad(w_in[:, main:], ((0, 0), (0, HEAD - 2 * hb)))

    lb_rows = [row(hgrn_lb, 0), row(hgrn_lb, 1), row(hgrn_lb, 2)]
    (lb0,) = _small_call(_lb0_stage, lb_rows, name="f_lb0")
    hp = jnp.zeros((8, HEAD), F32).at[0, :hb].set(a_log[0]).at[1, :hb].set(dt_bias[0])
    expand = jnp.asarray(np.kron(np.eye(S5_STATE, dtype=np.float32), np.ones((1, S5_GROUP), np.float32)))
    prep_in = [s5_a_re[0], s5_a_im[0], s5_log_dt[0].reshape(groups, 1),
               s5_b_re[0].reshape(groups, -1), s5_b_im[0].reshape(groups, -1), expand]
    lr, li, br, bi = _small_call(_s5_prep_stage, prep_in, name="f_s5_prep")
    wb, wc, lt = _s5_pack(lr, li, br, bi, s5_c_re[0], s5_c_im[0])
    fnorm = final_norm.reshape(1, d)

    def block_fwd(h, l, w_dn):
        hn = _rows_call(_rms_stage, [h], [row(norm_mlp, l)], [BF16], name=f"f_norm_mlp{l}")
        up_args = dict(epilogue=_relu2_epilogue, out_dtypes=(F32, BF16), name=f"f_up{l}")
        next_dn = None
        if w_dn is None:
            (up, act), (dn0,) = _mm(hn, w_upg[l], comm=(gather_of(w_down[0]),), **up_args)
            w_dn = dn0.reshape(ff, d)
            h2, (dn1,) = _mm(act, w_dn, extras=(h,), epilogue=_add_epilogue, name=f"f_down{l}",
                             comm=(gather_of(w_down[1]),))
            next_dn = dn1.reshape(ff, d)
        else:
            up, act = _mm(hn, w_upg[l], **up_args)
            h2 = _mm(act, w_dn, extras=(h,), epilogue=_add_epilogue, name=f"f_down{l}")
        hq = _rows_call(_rms_stage, [h2], [row(norm_ple, l)], [BF16], name=f"f_norm_ple{l}")
        gpre = _mm(hq, w_pgg[l], name=f"f_ple_gate{l}")
        h3, pp = _mm(p[l, 0], w_ppg[l], extras=(gpre, h2), epilogue=_ple_epilogue, out_dtypes=(F32, F32),
                     name=f"f_ple_proj{l}")
        return h3, dict(h=h, hn=hn, up=up, act=act, h2=h2, hq=hq, gpre=gpre, pp=pp, w_dn=w_dn), next_dn

    hn0 = _rows_call(_rms_stage, [x2], [row(norm_mix, 0)], [BF16], name="f_norm_mix0")
    shard_shapes = [conv_w[0].shape, s5_d.shape, b_glu.shape]
    proj, (oe8, pg8, small) = _mm(hn0, w_main, name="f_proj", comm=(
        gather_of(w_out_e[0]), gather_of(w_ple_gate), _Gather(_pack([conv_w[0], s5_d, b_glu]))))
    w_oe = oe8.reshape(d, d)
    w_top, w_bot = w_oe[:aw], w_oe[aw:]
    w_pgg = jnp.transpose(pg8, (1, 0, 2, 3)).reshape(2, d, d)
    conv_g, s5d_g, bglu_g = zip(*[_unpack(small[j], shard_shapes) for j in range(N_DEV)])
    conv_full = jnp.concatenate(conv_g, axis=1)
    s5d_full = jnp.concatenate(s5d_g, axis=1)
    bglu_full = jnp.concatenate(bglu_g, axis=1)
    ab = _mm(hn0, w_tail, name="f_ab")
    (oa, st_a), (gl8, oo8) = _hgrn_fwd(proj, lb0, g_norm_a, heads=ha, name="f_hgrn",
                                       comm=(gather_of(w_glu[0]), gather_of(w_out_o[0])))
    w_gl, w_oo = gl8.reshape(d, d), oo8.reshape(d, d)
    qkv = _conv_fwd(proj, conv_full, col_off=2 * d, name="f_conv")
    (ob, st_b), (up8,) = _delta_fwd(qkv, ab, proj, hp, g_norm_b, heads=hb, z_off=z_col // HEAD, name="f_delta",
                                    comm=(gather_of(w_up),))
    w_upg = jnp.transpose(up8, (1, 2, 0, 3)).reshape(2, d, ff)
    h1, (pp8,) = _mm(oa, w_top, extras=(x2,), epilogue=_add_epilogue, name="f_out_a", comm=(gather_of(w_ple_proj),))
    w_ppg = jnp.transpose(pp8, (1, 2, 0, 3)).reshape(2, ple, d)
    h1 = _mm(ob, w_bot, extras=(h1,), epilogue=_add_epilogue, name="f_out_b")
    h3, sv0, w_dn1 = block_fwd(h1, 0, None)

    u = _rows_call(_rms_stage, [h3], [row(norm_mix, 1)], [F32], name="f_norm_mix1")
    y, cins, s5_states = _s5_fwd(u, wb, wc, lt, s5d_full, name="f_s5")
    act_g = _rows_call(_gelu_stage, [y], [], [BF16], name="f_gelu")
    gl_raw = _mm(act_g, w_gl, name="f_glu")
    glu = _rows_call(_glu_stage, [y, gl_raw], [bglu_full], [BF16], name="f_glu_gate")
    h4 = _mm(glu, w_oo, extras=(h3,), epilogue=_add_epilogue, name="f_out_o")
    h6, sv1, _ = block_fwd(h4, 1, w_dn1)
    dh, d_fnorm, loss8 = _loss_call(h6, fnorm, target, name="loss")
    loss = lax.psum(loss8[0, 0], ("x", "y", "c"))

    dshard, ffs, cols = d // N_DEV, ff // N_DEV, w_in_e.shape[2]
    rows8 = lambda g: _AllToAll(g.reshape(N_DEV, -1, g.shape[-1]))
    cols8 = lambda g: _AllToAll(jnp.transpose(g.reshape(g.shape[0], N_DEV, -1), (1, 0, 2)))

    def block_bwd(dh3, l, sv, carried):
        (dgpre, dpp), _ = _rows_vjp(_ple_stage, [sv["h2"], sv["gpre"], sv["pp"]], [], [dh3],
                                    row_grads={1: BF16, 2: BF16}, name=f"b_ple{l}")
        g_pp = _mm(p[l, 0], dpp, ta=True, out_dtypes=(BF16,), name=f"b_w_ple_proj{l}")
        g_pg = _mm(sv["hq"], dgpre, ta=True, out_dtypes=(BF16,), name=f"b_w_ple_gate{l}")
        dhq = _mm(dgpre, w_pgg[l], tb=True, name=f"b_ple_gate{l}")
        (dh2,), (g_nple,) = _rows_vjp(_rms_stage, [sv["h2"]], [row(norm_ple, l)], [dhq], row_grads={0: F32},
                                      adds={0: dh3}, name=f"b_norm_ple{l}")
        dup, (r_pg, r_pp) = _mm(dh2, sv["w_dn"], tb=True, extras=(sv["up"],), epilogue=_relu2_grad_epilogue,
                                out_dtypes=(BF16,), name=f"b_down{l}", comm=(rows8(g_pg), cols8(g_pp)))
        g_dn = _mm(sv["act"], dh2, ta=True, out_dtypes=(BF16,), name=f"b_w_down{l}", comm=carried)
        g_dn, r_carried = g_dn if carried else (g_dn, [])
        g_up = _mm(sv["hn"], dup, ta=True, out_dtypes=(BF16,), tn=ffs, out_slots=True, name=f"b_w_up{l}")
        dhn = _mm(dup, w_upg[l], tb=True, name=f"b_up{l}")
        (dh0,), (g_nmlp,) = _rows_vjp(_rms_stage, [sv["h"]], [row(norm_mlp, l)], [dhn], row_grads={0: F32},
                                      adds={0: dh2}, name=f"b_norm_mlp{l}")
        return dh0, dict(w_ple_proj=r_pp, w_ple_gate=r_pg, norm_ple=g_nple, w_down=g_dn, w_up=g_up, norm_mlp=g_nmlp,
                         carried=r_carried)

    dh4, gb1 = block_bwd(dh, 1, sv1, ())
    dglu = _mm(dh4, w_oo, tb=True, name="b_out_o")
    g_oo = _mm(glu, dh4, ta=True, out_dtypes=(BF16,), name="b_w_out_o")
    (dy1, dgl), (g_bglu,) = _rows_vjp(_glu_stage, [y, gl_raw], [bglu_full], [dglu], row_grads={0: F32, 1: BF16},
                                      name="b_glu_gate")
    g_gl = _mm(act_g, dgl, ta=True, out_dtypes=(BF16,), name="b_w_glu")
    dact = _mm(dgl, w_gl, tb=True, name="b_glu")
    (dy,), _ = _rows_vjp(_gelu_stage, [y], [], [dact], row_grads={0: F32}, adds={0: dy1}, name="b_gelu")
    (du, dwb, dwc, g_s5d, dlam), (r_dn1, r_up1) = _s5_bwd(u, dy, wb, wc, lt, s5d_full, cins, s5_states, name="b_s5",
                                                         comm=(rows8(gb1["w_down"]), _AllToAll(gb1["w_up"])))
    (dh3,), (g_nmix1,) = _rows_vjp(_rms_stage, [h3], [row(norm_mix, 1)], [du], row_grads={0: F32}, adds={0: dh4},
                                   name="b_norm_mix1")
    dlr, dli, dbr, dbi, g_cre, g_cim = _s5_unpack(dwb, dwc, dlam)
    g_are, g_aim, g_ldt, g_bre, g_bim, _ = _small_vjp(_s5_prep_stage, prep_in, [dlr, dli, dbr, dbi], name="b_s5_prep")

    dh1, gb0 = block_bwd(dh3, 0, sv0, (rows8(g_oo), rows8(g_gl)))
    r_oo, r_gl = gb0["carried"]
    doa = _mm(dh1, w_top, tb=True, name="b_out_a")
    dob = _mm(dh1, w_bot, tb=True, name="b_out_b")
    g_oe = jnp.concatenate([_mm(oa, dh1, ta=True, out_dtypes=(BF16,), name="b_w_out_a"),
                            _mm(ob, dh1, ta=True, out_dtypes=(BF16,), name="b_w_out_b")], axis=0)
    dq, df, di, dg, dlb, g_gna = _hgrn_bwd(proj, lb0, g_norm_a, st_a, doa, heads=ha, name="b_hgrn")
    (dqb, dkb, dvb, dab, dz, dhp, g_gnb), (r_dn0, r_up0, r_oe) = _delta_bwd(
        qkv, ab, proj, hp, g_norm_b, st_b, dob, heads=hb, z_off=z_col // HEAD, name="b_delta",
        comm=(rows8(gb0["w_down"]), _AllToAll(gb0["w_up"]), rows8(g_oe)))
    dqkv, g_conv = _conv_bwd(proj, conv_full, jnp.concatenate([dqb, dkb, dvb], axis=1), col_off=2 * d, name="b_conv")
    dproj = jnp.concatenate([dq, df, di, dg, dqkv, dz], axis=1)
    g_main = _mm(hn0, dproj, ta=True, out_dtypes=(BF16,), name="b_w_proj")
    g_tail = _mm(hn0, dab, ta=True, out_dtypes=(BF16,), name="b_w_ab")
    dhn0, (r_in,) = _mm(dproj, w_main, tb=True, name="b_proj",
                        comm=(cols8(jnp.concatenate([g_main, g_tail[:, :2 * hb]], axis=1)),))
    dhn0 = _mm(dab, w_tail, tb=True, extras=(dhn0,), epilogue=_add_epilogue, name="b_ab")
    (dx,), (g_nmix0,) = _rows_vjp(_rms_stage, [x2], [row(norm_mix, 0)], [dhn0], row_grads={0: F32}, adds={0: dh1},
                                  name="b_norm_mix0")
    g_lb = jnp.concatenate(_small_vjp(_lb0_stage, lb_rows, [dlb], name="b_lb0"), axis=0)

    small_grads = dict(
        norm_mix=jnp.concatenate([g_nmix0, g_nmix1], axis=0),
        norm_mlp=jnp.concatenate([gb0["norm_mlp"], gb1["norm_mlp"]], axis=0),
        norm_ple=jnp.concatenate([gb0["norm_ple"], gb1["norm_ple"]], axis=0),
        hgrn_lb=g_lb, g_norm_a=g_gna, a_log=dhp[0:1, :hb], dt_bias=dhp[1:2, :hb], g_norm_b=g_gnb,
        s5_a_re=g_are[None], s5_a_im=g_aim[None], s5_b_re=g_bre.reshape(s5_b_re.shape),
        s5_b_im=g_bim.reshape(s5_b_im.shape), s5_c_re=g_cre[None], s5_c_im=g_cim[None],
        s5_log_dt=g_ldt.reshape(1, groups), final_norm=d_fnorm.reshape(d),
        conv_w=g_conv, s5_d=g_s5d, b_glu=g_bglu)
    rep_names = ["norm_mix", "norm_mlp", "norm_ple", "hgrn_lb", "g_norm_a", "a_log", "dt_bias", "g_norm_b", "s5_a_re",
                 "s5_a_im", "s5_b_re", "s5_b_im", "s5_c_re", "s5_c_im", "s5_log_dt", "final_norm"]
    full_names = rep_names + ["conv_w", "s5_d", "b_glu"]
    parts = _comm_call(_Gather(_pack([small_grads[k] for k in full_names])), name="ag_small_grads")
    summed = _unpack(_sum_parts(parts, name="sum_small_grads"), [small_grads[k].shape for k in full_names])
    summed = dict(zip(full_names, summed))
    cw = conv_w.shape[2]
    dshard = d // N_DEV
    shard_g = dict(conv_w=lax.dynamic_slice(summed["conv_w"], (0, me * cw), (CONV_WIDTH, cw))[None],
                   s5_d=lax.dynamic_slice(summed["s5_d"], (0, me * dshard), (1, dshard)),
                   b_glu=lax.dynamic_slice(summed["b_glu"], (0, me * dshard), (1, dshard)))
    small_names = rep_names + ["conv_w", "s5_d", "b_glu"]
    g_small = [summed[k] if k in rep_names else shard_g[k] for k in small_names]
    shapes = [args[k].shape for k in small_names]
    sm_out = _adamw(_pack([args[k] for k in small_names])[None], [_pack(g_small)[None]],
                    _pack([args["m_" + k] for k in small_names])[None], _pack([args["v_" + k] for k in small_names])[None],
                    name="adamw_small")
    sm_out = [dict(zip(small_names, _unpack(o[0], shapes))) for o in sm_out]

    received = dict(w_in_e=[r_in], w_out_e=[r_oe], w_glu=[r_gl], w_out_o=[r_oo], w_up=[r_up0, r_up1],
                    w_down=[r_dn0, r_dn1], w_ple_gate=[gb0["w_ple_gate"], gb1["w_ple_gate"]],
                    w_ple_proj=[gb0["w_ple_proj"], gb1["w_ple_proj"]])
    big_out = {k: _adamw(args[k], layers, args["m_" + k], args["v_" + k], name="adamw_" + k)
               for k, layers in received.items()}

    names = ["norm_mix", "norm_mlp", "norm_ple", "w_in_e", "w_out_e", "hgrn_lb", "g_norm_a", "conv_w", "a_log", "dt_bias",
             "g_norm_b", "s5_a_re", "s5_a_im", "s5_b_re", "s5_b_im", "s5_c_re", "s5_c_im", "s5_d", "s5_log_dt", "w_glu",
             "b_glu", "w_out_o", "w_up", "w_down", "w_ple_gate", "w_ple_proj", "final_norm"]
    result = [loss, dx[None]]
    for j in range(4):
        result += [big_out[k][j] if k in big_out else sm_out[j][k] for k in names]
    return tuple(result)
```

```python
import functools
import math
import operator

import numpy as np
import jax
import jax.numpy as jnp
from jax import lax
from jax.experimental import pallas as pl
from jax.experimental.pallas import tpu as pltpu

F32 = jnp.float32
BF16 = jnp.bfloat16
MM_DTYPE = BF16
HI = lax.Precision.HIGHEST
MESH = pl.DeviceIdType.MESH

NORM_EPS = 1e-6
CHUNK = 64
HEAD = 128
CONV_WIDTH = 4
S5_GROUP = 16
S5_STATE = 64
S5_GB = 8
S5_HALF = S5_GB * S5_STATE
N_DEV = 8
HEADS_PER_STEP = 8
ADAM_LR, ADAM_B1, ADAM_B2, ADAM_EPS, ADAM_WD, ADAM_STEP = 0.001, 0.9, 0.999, 1e-08, 0.01, 10
VMEM_LIMIT = 56 * 1024 * 1024

NN = (((1,), (0,)), ((), ()))
NT = (((1,), (1,)), ((), ()))
TN = (((0,), (0,)), ((), ()))


def _dot(a, b, dn=NN):
    return lax.dot_general(a, b, dn, precision=HI, preferred_element_type=F32)


def _hdot(a, b, dn=NN):
    return lax.dot_general(a, b, dn, precision=lax.Precision.HIGH, preferred_element_type=F32)


def _bdot_raw(a, b, dn=NN):
    return lax.dot_general(a.astype(BF16), b.astype(BF16), dn, preferred_element_type=F32)


@functools.partial(jax.custom_vjp, nondiff_argnums=(2,))
def _bdot(a, b, dn):
    return _bdot_raw(a, b, dn)


def _bdot_fwd(a, b, dn):
    return _bdot_raw(a, b, dn), (a, b)


def _bdot_bwd(dn, res, g):
    a, b = res
    if dn == NN:
        return _bdot_raw(g, b, NT), _bdot_raw(a, g, TN)
    if dn == NT:
        return _bdot_raw(g, b, NN), _bdot_raw(g, a, TN)
    assert dn == TN
    return _bdot_raw(b, g, NT), _bdot_raw(a, g, NN)


_bdot.defvjp(_bdot_fwd, _bdot_bwd)


def _per_head(f):
    def g(*args, **kw):
        n = [len(a.vals) for a in args if isinstance(a, _Heads)]
        if not n:
            return f(*args, **kw)
        return _Heads([f(*[a.vals[j] if isinstance(a, _Heads) else a for a in args], **kw) for j in range(n[0])])
    return g


class _Heads:
    def __init__(self, vals):
        self.vals = list(vals)

    def __add__(self, o):
        return _per_head(operator.add)(self, o)

    def __radd__(self, o):
        return _per_head(operator.add)(o, self)

    def __sub__(self, o):
        return _per_head(operator.sub)(self, o)

    def __rsub__(self, o):
        return _per_head(operator.sub)(o, self)

    def __mul__(self, o):
        return _per_head(operator.mul)(self, o)

    def __rmul__(self, o):
        return _per_head(operator.mul)(o, self)

    def __neg__(self):
        return _per_head(operator.neg)(self)


_exp, _log, _where, _sum, _mean = (_per_head(f) for f in (jnp.exp, jnp.log, jnp.where, jnp.sum, jnp.mean))
_sigmoid, _rsqrt, _equal = _per_head(jax.nn.sigmoid), _per_head(lax.rsqrt), _per_head(operator.eq)
_hdot_h, _bdot_h = _per_head(_hdot), _per_head(_bdot)


def _params(n_axes):
    return pltpu.CompilerParams(dimension_semantics=("arbitrary",) * n_axes, vmem_limit_bytes=VMEM_LIMIT)


def _full_spec(a):
    nd = a.ndim
    return pl.BlockSpec(a.shape, lambda *_: (0,) * nd)


def _mm(a, b, *, name, ta=False, tb=False, extras=(), epilogue=None, out_dtypes=(F32,), tm=1024, tn=1024, tk=2048,
        out_slots=False, comm=()):
    m = a.shape[1] if ta else a.shape[0]
    k = a.shape[0] if ta else a.shape[1]
    n = b.shape[0] if tb else b.shape[1]
    assert k == (b.shape[1] if tb else b.shape[0]), (name, a.shape, b.shape)
    tm, tn, tk = min(tm, m), min(tn, n), min(tk, k)
    assert m % tm == 0 and n % tn == 0 and k % tk == 0, (name, m, n, k)
    nk = k // tk
    n_ex, n_out = len(extras), len(out_dtypes)
    dn = (((0 if ta else 1,), (1 if tb else 0,)), ((), ()))

    def body(a_ref, b_ref, *rest):
        ex_refs, out_refs = rest[:n_ex], rest[n_ex:n_ex + n_out]
        part = lax.dot_general(a_ref[...].astype(MM_DTYPE), b_ref[...].astype(MM_DTYPE), dn, preferred_element_type=F32)

        def finish(acc):
            outs = epilogue(acc, *[r[...] for r in ex_refs]) if epilogue is not None else (acc,)
            for o_ref, o in zip(out_refs, outs):
                o_ref[...] = o.astype(o_ref.dtype)

        if nk == 1:
            finish(part)
            return
        acc_ref = rest[-1]
        kk = pl.program_id(2)

        @pl.when(kk == 0)
        def _():
            acc_ref[...] = part

        @pl.when((kk > 0) & (kk < nk - 1))
        def _():
            acc_ref[...] += part

        @pl.when(kk == nk - 1)
        def _():
            finish(acc_ref[...] + part)

    a_spec = pl.BlockSpec((tk, tm), lambda i, j, q: (q, i)) if ta else pl.BlockSpec((tm, tk), lambda i, j, q: (i, q))
    b_spec = pl.BlockSpec((tn, tk), lambda i, j, q: (j, q)) if tb else pl.BlockSpec((tk, tn), lambda i, j, q: (q, j))
    ex_specs = []
    for e in extras:
        if e.shape[0] == 1 and m != 1:
            ex_specs.append(pl.BlockSpec((1, tn), lambda i, j, q: (0, j)))
        else:
            ex_specs.append(pl.BlockSpec((tm, tn), lambda i, j, q: (i, j)))
    if out_slots:
        out_spec, out_dims = pl.BlockSpec((None, tm, tn), lambda i, j, q: (j, i, 0)), (n // tn, m, tn)
    else:
        out_spec, out_dims = pl.BlockSpec((tm, tn), lambda i, j, q: (i, j)), (m, n)
    outs, exchanged = _call(
        body, (a, b, *extras), name=name, grid=(m // tm, n // tn, nk),
        in_specs=[a_spec, b_spec] + ex_specs,
        out_specs=[out_spec for _ in out_dtypes],
        out_shape=[jax.ShapeDtypeStruct(out_dims, dt) for dt in out_dtypes],
        scratch_shapes=[pltpu.VMEM((tm, tn), F32)] if nk > 1 else [], comm=comm)
    outs = outs[0] if n_out == 1 else tuple(outs)
    return (outs, exchanged) if comm else outs


def _rows_call(fn, rows, consts, out_dtypes, *, name, tr=256):
    s = rows[0].shape[0]
    tr = min(tr, s)
    nr, nc = len(rows), len(consts)
    widths = [o.shape[1] for o in jax.eval_shape(
        fn, *[jax.ShapeDtypeStruct((tr, r.shape[1]), F32) for r in rows],
        *[jax.ShapeDtypeStruct(c.shape, F32) for c in consts])]

    def body(*refs):
        rv = [r[...].astype(F32) for r in refs[:nr]]
        cv = [c[...] for c in refs[nr:nr + nc]]
        for o_ref, o in zip(refs[nr + nc:], fn(*rv, *cv)):
            o_ref[...] = o.astype(o_ref.dtype)

    outs = pl.pallas_call(
        body, name=name, grid=(s // tr,),
        in_specs=[pl.BlockSpec((tr, r.shape[1]), lambda i: (i, 0)) for r in rows] + [_full_spec(c) for c in consts],
        out_specs=[pl.BlockSpec((tr, w), lambda i: (i, 0)) for w in widths],
        out_shape=[jax.ShapeDtypeStruct((s, w), dt) for w, dt in zip(widths, out_dtypes)],
        compiler_params=_params(1),
    )(*rows, *consts)
    return outs[0] if len(outs) == 1 else tuple(outs)


def _rows_vjp(fn, rows, consts, cots, *, name, row_grads, adds=None, tr=256):
    adds = adds or {}
    s = rows[0].shape[0]
    tr = min(tr, s)
    nr, nc, nt = len(rows), len(consts), len(cots)
    rg = sorted(row_grads)
    ad = sorted(adds)

    def body(*refs):
        rv = [r[...].astype(F32) for r in refs[:nr]]
        cv = [c[...] for c in refs[nr:nr + nc]]
        ct = [c[...].astype(F32) for c in refs[nr + nc:nr + nc + nt]]
        av = {i: r[...].astype(F32) for i, r in zip(ad, refs[nr + nc + nt:nr + nc + nt + len(ad)])}
        out_refs = refs[nr + nc + nt + len(ad):]
        _, vjp = jax.vjp(fn, *rv, *cv)
        grads = vjp(tuple(ct))
        for o_ref, i in zip(out_refs[:len(rg)], rg):
            g = grads[i]
            if i in av:
                g = g + av[i]
            o_ref[...] = g.astype(o_ref.dtype)

        @pl.when(pl.program_id(0) == 0)
        def _():
            for o_ref in out_refs[len(rg):]:
                o_ref[...] = jnp.zeros_like(o_ref)

        for o_ref, g in zip(out_refs[len(rg):], grads[nr:]):
            o_ref[...] += g

    row_spec = lambda a: pl.BlockSpec((tr, a.shape[1]), lambda i: (i, 0))
    outs = pl.pallas_call(
        body, name=name, grid=(s // tr,),
        in_specs=[row_spec(r) for r in rows] + [_full_spec(c) for c in consts] + [row_spec(c) for c in cots]
        + [row_spec(adds[i]) for i in ad],
        out_specs=[row_spec(rows[i]) for i in rg] + [_full_spec(c) for c in consts],
        out_shape=[jax.ShapeDtypeStruct(rows[i].shape, row_grads[i]) for i in rg]
        + [jax.ShapeDtypeStruct(c.shape, F32) for c in consts],
        compiler_params=_params(1),
    )(*rows, *consts, *cots, *[adds[i] for i in ad])
    return list(outs[:len(rg)]), list(outs[len(rg):])


def _small_call(fn, ins, *, name):
    shapes = jax.eval_shape(fn, *[jax.ShapeDtypeStruct(a.shape, F32) for a in ins])

    def body(*refs):
        for o_ref, o in zip(refs[len(ins):], fn(*[r[...] for r in refs[:len(ins)]])):
            o_ref[...] = o

    return pl.pallas_call(
        body, name=name, in_specs=[_full_spec(a) for a in ins],
        out_specs=[pl.BlockSpec(o.shape, functools.partial(lambda nd, *_: (0,) * nd, len(o.shape))) for o in shapes],
        out_shape=[jax.ShapeDtypeStruct(o.shape, F32) for o in shapes], grid=(1,),
        compiler_params=_params(1),
    )(*ins)


def _small_vjp(fn, ins, cots, *, name):
    def body(*refs):
        vals = [r[...] for r in refs[:len(ins)]]
        ct = [r[...] for r in refs[len(ins):len(ins) + len(cots)]]
        _, vjp = jax.vjp(fn, *vals)
        for o_ref, g in zip(refs[len(ins) + len(cots):], vjp(tuple(ct))):
            o_ref[...] = g

    return pl.pallas_call(
        body, name=name, in_specs=[_full_spec(a) for a in ins] + [_full_spec(c) for c in cots],
        out_specs=[_full_spec(a) for a in ins],
        out_shape=[jax.ShapeDtypeStruct(a.shape, F32) for a in ins], grid=(1,),
        compiler_params=_params(1),
    )(*ins, *cots)


def _rms(x, g):
    return x * _rsqrt(_mean(x * x, axis=-1, keepdims=True) + NORM_EPS) * g


def _rms_stage(x, g):
    return (_rms(x, g),)


def _silu(x):
    return x * _sigmoid(x)


def _softplus(x):
    return jnp.maximum(x, 0.0) + jnp.log1p(jnp.exp(-jnp.abs(x)))


def _gelu(x):
    return jax.nn.gelu(x, approximate=True)


def _gelu_stage(y):
    return (_gelu(y),)


def _glu_stage(y, gl_raw, b):
    return (_gelu(y) * jax.nn.sigmoid(gl_raw + b),)


def _ple_stage(h, gpre, pp):
    return (h + jax.nn.sigmoid(gpre) * pp,)


def _relu2_grad_epilogue(acc, up):
    return (acc * (2.0 * jnp.maximum(up, 0.0)),)


def _lb0_stage(x0, x1, x2):
    mx = jnp.maximum(jnp.maximum(x0, x1), x2)
    e0, e1, e2 = jnp.exp(x0 - mx), jnp.exp(x1 - mx), jnp.exp(x2 - mx)
    return (e0 / (e0 + e1 + e2),)


def _s5_prep_stage(a_re, a_im, log_dt, b_re, b_im, expand):
    step = jnp.exp(log_dt)
    mag = jnp.exp(a_re * step)
    lr = mag * jnp.cos(a_im * step)
    li = mag * jnp.sin(a_im * step)
    den = a_re * a_re + a_im * a_im
    cr = ((lr - 1.0) * a_re + li * a_im) / den
    ci = (li * a_re - (lr - 1.0) * a_im) / den
    cr_e, ci_e = _dot(cr, expand), _dot(ci, expand)
    return lr, li, cr_e * b_re - ci_e * b_im, cr_e * b_im + ci_e * b_re


def _loss_call(h, g, target, *, name, tr=256):
    s, d = h.shape
    tr = min(tr, s)

    def loss_fn(hv, gv, tv):
        err = _rms(hv, gv) - tv
        return 0.5 * jnp.sum(jnp.mean(err * err, axis=-1))

    def body(h_ref, g_ref, t_ref, dh_ref, dg_ref, loss_ref):
        val, (dh, dg) = jax.value_and_grad(loss_fn, argnums=(0, 1))(h_ref[...], g_ref[...], t_ref[...])
        dh_ref[...] = dh

        @pl.when(pl.program_id(0) == 0)
        def _():
            dg_ref[...] = jnp.zeros_like(dg_ref)
            loss_ref[...] = jnp.zeros_like(loss_ref)

        dg_ref[...] += dg
        loss_ref[...] += jnp.full(loss_ref.shape, val, F32)

    row = pl.BlockSpec((tr, d), lambda i: (i, 0))
    return pl.pallas_call(
        body, name=name, grid=(s // tr,),
        in_specs=[row, _full_spec(g), row],
        out_specs=[row, _full_spec(g), pl.BlockSpec((8, 128), lambda i: (0, 0))],
        out_shape=[jax.ShapeDtypeStruct((s, d), F32), jax.ShapeDtypeStruct(g.shape, F32),
                   jax.ShapeDtypeStruct((8, 128), F32)],
        compiler_params=_params(1),
    )(h, g, target)


def _hgrn_chunk(q, fp, iv, gp, lb, gn, st_t):
    c = CHUNK
    row = lax.broadcasted_iota(jnp.int32, (c, c), 0)
    col = lax.broadcasted_iota(jnp.int32, (c, c), 1)
    causal = row >= col
    fg = lb + (1.0 - lb) * _sigmoid(fp)
    k = 1.0 - fg
    lf = _log(fg)
    cum = _hdot_h(causal.astype(F32), lf, NN)
    first_half = (lax.broadcasted_iota(jnp.int32, (c, 1), 0) < c // 2).astype(F32)
    ref = _sum(lf * first_half, axis=0, keepdims=True)
    cend = _sum(lf, axis=0, keepdims=True)
    scores = _where(causal, _hdot_h(q * _exp(cum - ref), k * _exp(ref - cum), NT), 0.0)
    out = _bdot_h(scores, iv, NN) + _bdot_h(q * _exp(cum), st_t, NT)
    st_new = st_t * _exp(cend) + _bdot_h(iv, k * _exp(cend - cum), TN)
    res = _rms(out, gn) * _silu(gp)
    return res, st_new


def _hgrn_heads(qs, fs, ivs, gs, lbs, gn, sts):
    res, st_new = _hgrn_chunk(_Heads(qs), _Heads(fs), _Heads(ivs), _Heads(gs), _Heads(lbs), gn, _Heads(sts))
    return res.vals, st_new.vals


def _lanes(j):
    return slice(j * HEAD, (j + 1) * HEAD)


def _hgrn_fwd(proj, lb, gn, *, heads, name, comm=()):
    s = proj.shape[0]
    n = s // CHUNK
    hpb = min(HEADS_PER_STEP, heads)
    assert heads % hpb == 0

    def body(q_ref, f_ref, i_ref, g_ref, lb_ref, gn_ref, o_ref, st_ref, state):
        @pl.when(pl.program_id(1) == 0)
        def _():
            state[...] = jnp.zeros_like(state)

        gnv = gn_ref[...]
        loaded = [(q_ref[:, _lanes(j)], f_ref[:, _lanes(j)], i_ref[:, _lanes(j)], g_ref[:, _lanes(j)],
                   lb_ref[:, _lanes(j)], state[j]) for j in range(hpb)]
        qs, fs, ivs, gs, lbs, sts = (list(t) for t in zip(*loaded))
        res, st_new = _hgrn_heads(qs, fs, ivs, gs, lbs, gnv, sts)
        for j in range(hpb):
            st_ref[j] = sts[j]
            o_ref[:, _lanes(j)] = res[j].astype(o_ref.dtype)
            state[j] = st_new[j]

    wide = hpb * HEAD
    blk = lambda off: pl.BlockSpec((CHUNK, wide), lambda h, c: (c, off // hpb + h))
    return _call(
        body, (proj, proj, proj, proj, lb, gn), name=name, grid=(heads // hpb, n),
        in_specs=[blk(0), blk(heads), blk(2 * heads), blk(3 * heads),
                  pl.BlockSpec((1, wide), lambda h, c: (0, h)), pl.BlockSpec((1, HEAD), lambda h, c: (0, 0))],
        out_specs=[pl.BlockSpec((CHUNK, wide), lambda h, c: (c, h)),
                   pl.BlockSpec((hpb, None, HEAD, HEAD), lambda h, c: (h, c, 0, 0))],
        out_shape=[jax.ShapeDtypeStruct((s, heads * HEAD), BF16), jax.ShapeDtypeStruct((heads, n, HEAD, HEAD), F32)],
        scratch_shapes=[pltpu.VMEM((hpb, HEAD, HEAD), F32)], comm=comm)


def _hgrn_bwd(proj, lb, gn, states, d_out, *, heads, name):
    s = proj.shape[0]
    n = s // CHUNK
    hpb = min(HEADS_PER_STEP, heads)

    def body(q_ref, f_ref, i_ref, g_ref, lb_ref, gn_ref, st_ref, do_ref,
             dq_ref, df_ref, di_ref, dg_ref, dlb_ref, dgn_ref, dstate):
        h, c = pl.program_id(0), pl.program_id(1)

        @pl.when(c == 0)
        def _():
            dstate[...] = jnp.zeros_like(dstate)
            dlb_ref[...] = jnp.zeros_like(dlb_ref)

        @pl.when((c == 0) & (h == 0))
        def _():
            dgn_ref[...] = jnp.zeros_like(dgn_ref)

        gnv = gn_ref[...]
        loaded = [(q_ref[:, _lanes(j)], f_ref[:, _lanes(j)], i_ref[:, _lanes(j)], g_ref[:, _lanes(j)],
                   lb_ref[:, _lanes(j)], st_ref[j], do_ref[:, _lanes(j)].astype(F32), dstate[j]) for j in range(hpb)]
        qs, fs, ivs, gs, lbs, sts, dos, dss = (list(t) for t in zip(*loaded))
        _, vjp = jax.vjp(_hgrn_heads, qs, fs, ivs, gs, lbs, gnv, sts)
        dqs, dfs, dis, dgs, dlbs, dgn_sum, dsts = vjp((dos, dss))
        for j in range(hpb):
            ln = _lanes(j)
            dq_ref[:, ln] = dqs[j].astype(dq_ref.dtype)
            df_ref[:, ln] = dfs[j].astype(df_ref.dtype)
            di_ref[:, ln] = dis[j].astype(di_ref.dtype)
            dg_ref[:, ln] = dgs[j].astype(dg_ref.dtype)
            dlb_ref[:, ln] += dlbs[j]
            dstate[j] = dsts[j]
        dgn_ref[...] += dgn_sum

    wide = hpb * HEAD
    rev = lambda off: pl.BlockSpec((CHUNK, wide), lambda h, c: (n - 1 - c, off // hpb + h))
    out_blk = pl.BlockSpec((CHUNK, wide), lambda h, c: (n - 1 - c, h))
    width = heads * HEAD
    return pl.pallas_call(
        body, name=name, grid=(heads // hpb, n),
        in_specs=[rev(0), rev(heads), rev(2 * heads), rev(3 * heads),
                  pl.BlockSpec((1, wide), lambda h, c: (0, h)), pl.BlockSpec((1, HEAD), lambda h, c: (0, 0)),
                  pl.BlockSpec((hpb, None, HEAD, HEAD), lambda h, c: (h, n - 1 - c, 0, 0)), out_blk],
        out_specs=[out_blk, out_blk, out_blk, out_blk,
                   pl.BlockSpec((1, wide), lambda h, c: (0, h)), pl.BlockSpec((1, HEAD), lambda h, c: (0, 0))],
        out_shape=[jax.ShapeDtypeStruct((s, width), BF16)] * 4
        + [jax.ShapeDtypeStruct((1, width), F32), jax.ShapeDtypeStruct((1, HEAD), F32)],
        scratch_shapes=[pltpu.VMEM((hpb, HEAD, HEAD), F32)],
        compiler_params=_params(2),
    )(proj, proj, proj, proj, lb, gn, states, d_out)


def _shift_rows(x, d, rowi):
    if d == 0:
        return x
    n = x.shape[0]
    rolled = pltpu.roll(x, d % n, 0)
    keep = rowi >= d if d > 0 else rowi < n + d
    return jnp.where(keep, rolled, 0.0)


def _conv_pre(x, w_ref, rowi):
    acc = None
    for j in range(CONV_WIDTH):
        term = w_ref[j:j + 1, :] * _shift_rows(x, CONV_WIDTH - 1 - j, rowi)
        acc = term if acc is None else acc + term
    return acc


def _conv_fwd(proj, w, *, col_off, name, cb=256):
    s = proj.shape[0]
    width = w.shape[1]
    cb = min(cb, width)

    def body(x_ref, w_ref, o_ref):
        rowi = lax.broadcasted_iota(jnp.int32, (s, cb), 0)
        o_ref[...] = _silu(_conv_pre(x_ref[...], w_ref, rowi))

    return pl.pallas_call(
        body, name=name, grid=(width // cb,),
        in_specs=[pl.BlockSpec((s, cb), lambda j: (0, col_off // cb + j)), pl.BlockSpec((CONV_WIDTH, cb), lambda j: (0, j))],
        out_specs=pl.BlockSpec((s, cb), lambda j: (0, j)),
        out_shape=jax.ShapeDtypeStruct((s, width), F32),
        compiler_params=_params(1),
    )(proj, w)


def _conv_bwd(proj, w, d_out, *, col_off, name, cb=256):
    s = proj.shape[0]
    width = w.shape[1]
    cb = min(cb, width)

    def body(x_ref, w_ref, do_ref, dx_ref, dw_ref):
        rowi = lax.broadcasted_iota(jnp.int32, (s, cb), 0)
        x = x_ref[...]
        pre = _conv_pre(x, w_ref, rowi)
        sg = jax.nn.sigmoid(pre)
        dpre = do_ref[...] * (sg + pre * sg * (1.0 - sg))
        dx = None
        for j in range(CONV_WIDTH):
            d = CONV_WIDTH - 1 - j
            term = w_ref[j:j + 1, :] * _shift_rows(dpre, -d, rowi)
            dx = term if dx is None else dx + term
            dw_ref[j:j + 1, :] = jnp.sum(dpre * _shift_rows(x, d, rowi), axis=0, keepdims=True)
        dx_ref[...] = dx.astype(dx_ref.dtype)

    return pl.pallas_call(
        body, name=name, grid=(width // cb,),
        in_specs=[pl.BlockSpec((s, cb), lambda j: (0, col_off // cb + j)), pl.BlockSpec((CONV_WIDTH, cb), lambda j: (0, j)),
                  pl.BlockSpec((s, cb), lambda j: (0, j))],
        out_specs=[pl.BlockSpec((s, cb), lambda j: (0, j)), pl.BlockSpec((CONV_WIDTH, cb), lambda j: (0, j))],
        out_shape=[jax.ShapeDtypeStruct((s, width), BF16), jax.ShapeDtypeStruct((CONV_WIDTH, width), F32)],
        compiler_params=_params(1),
    )(proj, w, d_out)


_lane_concat = _per_head(lambda a, b: jnp.concatenate([a, b], axis=1))
_lane_half = _per_head(lambda a, j: a[:, j * HEAD:(j + 1) * HEAD])


def _tri_inverse(lower):
    c = CHUNK
    row = lax.broadcasted_iota(jnp.int32, (c, c), 0)
    col = lax.broadcasted_iota(jnp.int32, (c, c), 1)
    inv = (row == col).astype(F32)
    lvl = 0
    while (1 << lvl) < c:
        same_pair = (row >> (lvl + 1)) == (col >> (lvl + 1))
        off_block = same_pair & (((row >> lvl) & 1) == 1) & (((col >> lvl) & 1) == 0)
        inv = inv - _hdot_h(_hdot_h(inv, _where(off_block, lower, 0.0), NN), inv, NN)
        lvl += 1
    return inv


@jax.custom_vjp
def _tri_solve(lowers, rhss):
    return _tri_solve_fwd(lowers, rhss)[0]


def _tri_solve_fwd(lowers, rhss):
    inv = _tri_inverse(_Heads(lowers))
    sol = _hdot_h(inv, _Heads(rhss), NN)
    return sol.vals, (inv.vals, sol.vals)


def _tri_solve_bwd(res, g):
    inv, sol = _Heads(res[0]), _Heads(res[1])
    d_rhs = _hdot_h(inv, _Heads(g), TN)
    return (-_hdot_h(d_rhs, sol, NT)).vals, d_rhs.vals


_tri_solve.defvjp(_tri_solve_fwd, _tri_solve_bwd)


def _solve(lower, rhs):
    if isinstance(lower, _Heads):
        return _Heads(_tri_solve(lower.vals, rhs.vals))
    return _tri_solve([lower], [rhs])[0]


def _delta_chunk(h, heads, qr, kr, vr, ab, zp, alog, dtb, gn, st):
    c = CHUNK
    row = lax.broadcasted_iota(jnp.int32, (c, c), 0)
    col = lax.broadcasted_iota(jnp.int32, (c, c), 1)
    causal = row >= col
    strict = row > col
    lane = lax.broadcasted_iota(jnp.int32, (c, HEAD), 1)
    mine = _equal(h, lane)
    la_full = -jnp.exp(alog) * _softplus(ab + dtb)
    cum_full = _hdot(causal.astype(F32), la_full)
    cum = _sum(_where(mine, cum_full, 0.0), axis=1, keepdims=True)
    cend = _sum(_sum(_where(mine, la_full, 0.0), axis=1, keepdims=True), axis=0, keepdims=True)
    beta = _sum(_where(_equal(heads + h, lane), jax.nn.sigmoid(ab), 0.0), axis=1, keepdims=True)
    cum_row = _hdot_h(_where(mine, 1.0, 0.0), cum_full, NT)
    decay = _where(causal, _exp(_where(causal, cum - cum_row, 0.0)), 0.0)
    qn = qr * _rsqrt(_sum(qr * qr, axis=-1, keepdims=True) + NORM_EPS) * (HEAD ** -0.5)
    kn = kr * _rsqrt(_sum(kr * kr, axis=-1, keepdims=True) + NORM_EPS)
    kb = kn * beta
    lower = _where(strict, _bdot_h(kb, kn, NT) * decay, 0.0)
    ecum = _exp(cum)
    sol = _solve(lower, _lane_concat(vr * beta, kb * ecum))
    u, w = _lane_half(sol, 0), _lane_half(sol, 1)
    intra = _bdot_h(qn, kn, NT) * decay
    v_new = u - _bdot_h(w, st, NN)
    out = _bdot_h(qn * ecum, st, NN) + _bdot_h(intra, v_new, NN)
    st_new = st * _exp(cend) + _bdot_h(kn * _exp(cend - cum), v_new, TN)
    res = _rms(out, gn) * _silu(zp)
    return res, st_new


def _delta_heads(hs, heads, qs, ks, vs, ab, zs, alog, dtb, gn, sts):
    res, st_new = _delta_chunk(_Heads(hs), heads, _Heads(qs), _Heads(ks), _Heads(vs), ab, _Heads(zs), alog, dtb, gn,
                               _Heads(sts))
    return res.vals, st_new.vals


def _delta_fwd(qkv, ab, proj, hp, gn, *, heads, z_off, name, comm=()):
    s = qkv.shape[0]
    n = s // CHUNK

    hpb = min(HEADS_PER_STEP, heads)
    assert heads % hpb == 0 and z_off % hpb == 0

    def body(q_ref, k_ref, v_ref, ab_ref, z_ref, hp_ref, gn_ref, o_ref, st_ref, state):
        hb = pl.program_id(1)

        @pl.when(pl.program_id(0) == 0)
        def _():
            for j in range(hpb):
                state[hb * hpb + j] = jnp.zeros((HEAD, HEAD), F32)

        shared = (ab_ref[...], hp_ref[0:1, :], hp_ref[1:2, :], gn_ref[...])
        loaded = [(q_ref[:, _lanes(j)], k_ref[:, _lanes(j)], v_ref[:, _lanes(j)], z_ref[:, _lanes(j)],
                   state[hb * hpb + j]) for j in range(hpb)]
        qs, ks, vs, zs, sts = (list(t) for t in zip(*loaded))
        res, st_new = _delta_heads([hb * hpb + j for j in range(hpb)], heads, qs, ks, vs, shared[0], zs, shared[1],
                                   shared[2], shared[3], sts)
        for j in range(hpb):
            st_ref[j] = sts[j]
            o_ref[:, _lanes(j)] = res[j].astype(o_ref.dtype)
            state[hb * hpb + j] = st_new[j]

    wide = hpb * HEAD
    blk = lambda off: pl.BlockSpec((CHUNK, wide), lambda c, h: (c, off // hpb + h))
    return _call(
        body, (qkv, qkv, qkv, ab, proj, hp, gn), name=name, grid=(n, heads // hpb),
        in_specs=[blk(0), blk(heads), blk(2 * heads), pl.BlockSpec((CHUNK, HEAD), lambda c, h: (c, 0)), blk(z_off),
                  pl.BlockSpec((8, HEAD), lambda c, h: (0, 0)), pl.BlockSpec((1, HEAD), lambda c, h: (0, 0))],
        out_specs=[pl.BlockSpec((CHUNK, wide), lambda c, h: (c, h)),
                   pl.BlockSpec((hpb, None, HEAD, HEAD), lambda c, h: (h, c, 0, 0))],
        out_shape=[jax.ShapeDtypeStruct((s, heads * HEAD), BF16), jax.ShapeDtypeStruct((heads, n, HEAD, HEAD), F32)],
        scratch_shapes=[pltpu.VMEM((heads, HEAD, HEAD), F32)], comm=comm)


def _delta_bwd(qkv, ab, proj, hp, gn, states, d_out, *, heads, z_off, name, comm=()):
    s = qkv.shape[0]
    n = s // CHUNK
    hpb = min(HEADS_PER_STEP, heads)

    def body(q_ref, k_ref, v_ref, ab_ref, z_ref, hp_ref, gn_ref, st_ref, do_ref,
             dq_ref, dk_ref, dv_ref, dab_ref, dz_ref, dhp_ref, dgn_ref, dstate):
        c, hb = pl.program_id(0), pl.program_id(1)

        @pl.when(c == 0)
        def _():
            for j in range(hpb):
                dstate[hb * hpb + j] = jnp.zeros((HEAD, HEAD), F32)

        @pl.when((c == 0) & (hb == 0))
        def _():
            dgn_ref[...] = jnp.zeros_like(dgn_ref)
            dhp_ref[...] = jnp.zeros_like(dhp_ref)

        @pl.when(hb == 0)
        def _():
            dab_ref[...] = jnp.zeros_like(dab_ref)

        shared = (ab_ref[...], hp_ref[0:1, :], hp_ref[1:2, :], gn_ref[...])
        loaded = [(q_ref[:, _lanes(j)], k_ref[:, _lanes(j)], v_ref[:, _lanes(j)], z_ref[:, _lanes(j)], st_ref[j],
                   do_ref[:, _lanes(j)].astype(F32), dstate[hb * hpb + j]) for j in range(hpb)]
        qs, ks, vs, zs, sts, dos, dss = (list(t) for t in zip(*loaded))
        fn = functools.partial(_delta_heads, [hb * hpb + j for j in range(hpb)], heads)
        _, vjp = jax.vjp(fn, qs, ks, vs, shared[0], zs, shared[1], shared[2], shared[3], sts)
        dqs, dks, dvs, dab, dzs, dal, ddt, dgn, dsts = vjp((dos, dss))
        for j in range(hpb):
            ln = _lanes(j)
            dq_ref[:, ln] = dqs[j]
            dk_ref[:, ln] = dks[j]
            dv_ref[:, ln] = dvs[j]
            dz_ref[:, ln] = dzs[j].astype(dz_ref.dtype)
            dstate[hb * hpb + j] = dsts[j]
        dab_ref[...] += dab
        dhp_ref[0:1, :] += dal
        dhp_ref[1:2, :] += ddt
        dgn_ref[...] += dgn

    wide = hpb * HEAD
    rev = lambda off: pl.BlockSpec((CHUNK, wide), lambda c, h: (n - 1 - c, off // hpb + h))
    width = heads * HEAD
    head_blk = pl.BlockSpec((CHUNK, wide), lambda c, h: (n - 1 - c, h))
    ab_blk = pl.BlockSpec((CHUNK, HEAD), lambda c, h: (n - 1 - c, 0))
    return _call(
        body, (qkv, qkv, qkv, ab, proj, hp, gn, states, d_out), name=name, grid=(n, heads // hpb),
        in_specs=[rev(0), rev(heads), rev(2 * heads), ab_blk, rev(z_off),
                  pl.BlockSpec((8, HEAD), lambda c, h: (0, 0)), pl.BlockSpec((1, HEAD), lambda c, h: (0, 0)),
                  pl.BlockSpec((hpb, None, HEAD, HEAD), lambda c, h: (h, n - 1 - c, 0, 0)), head_blk],
        out_specs=[head_blk, head_blk, head_blk, ab_blk, head_blk,
                   pl.BlockSpec((8, HEAD), lambda c, h: (0, 0)), pl.BlockSpec((1, HEAD), lambda c, h: (0, 0))],
        out_shape=[jax.ShapeDtypeStruct((s, width), F32)] * 3
        + [jax.ShapeDtypeStruct((s, HEAD), F32), jax.ShapeDtypeStruct((s, width), BF16),
           jax.ShapeDtypeStruct((8, HEAD), F32), jax.ShapeDtypeStruct((1, HEAD), F32)],
        scratch_shapes=[pltpu.VMEM((heads, HEAD, HEAD), F32)], comm=comm)


def _s5_scan(buf, lt_ref, cin_r, cin_i, tt, reverse):
    nblk = tt // 8
    hl = S5_HALF
    base = 8 if reverse else 0

    def body(j, carry):
        cr, ci = carry
        off = pl.multiple_of((nblk - 1 - j if reverse else j) * 8, 8)
        xr = buf[pl.ds(off, 8), 0:hl]
        xi = buf[pl.ds(off, 8), hl:2 * hl]
        for lv, d in enumerate((1, 2, 4)):
            ar, ai = lt_ref[base + 2 * lv], lt_ref[base + 2 * lv + 1]
            sr = pltpu.roll(xr, 8 - d if reverse else d, 0)
            si = pltpu.roll(xi, 8 - d if reverse else d, 0)
            xr, xi = xr + ar * sr - ai * si, xi + ar * si + ai * sr
        pr, pi = lt_ref[base + 6], lt_ref[base + 7]
        xr, xi = xr + pr * cr - pi * ci, xi + pr * ci + pi * cr
        buf[pl.ds(off, 8), 0:hl] = xr
        buf[pl.ds(off, 8), hl:2 * hl] = xi
        edge = 0 if reverse else 7
        return xr[edge:edge + 1, :], xi[edge:edge + 1, :]

    return lax.fori_loop(0, nblk, body, (cin_r, cin_i))


def _s5_fwd(u, wb, wc, lt, dskip, *, name, tt=512):
    s, d = u.shape
    nb = d // HEAD
    tt = min(tt, s)
    nt = s // tt
    hl = S5_HALF

    def body(u_ref, wb_ref, wc_ref, lt_ref, d_ref, y_ref, cin_ref, st_ref, buf, carry):
        @pl.when(pl.program_id(1) == 0)
        def _():
            carry[...] = jnp.zeros_like(carry)

        cin_ref[...] = carry[0:1, :]
        uv = u_ref[...]
        buf[...] = _bdot_raw(uv, wb_ref[...])
        cr, ci = _s5_scan(buf, lt_ref, carry[0:1, 0:hl], carry[0:1, hl:2 * hl], tt, False)
        carry[0:1, 0:hl] = cr
        carry[0:1, hl:2 * hl] = ci
        states = buf[...].astype(BF16)
        st_ref[...] = states
        y_ref[...] = _bdot_raw(states, wc_ref[...]) + d_ref[...] * uv

    return pl.pallas_call(
        body, name=name, grid=(nb, nt),
        in_specs=[pl.BlockSpec((tt, HEAD), lambda b, t: (t, b)),
                  pl.BlockSpec((None, HEAD, 2 * hl), lambda b, t: (b, 0, 0)),
                  pl.BlockSpec((None, 2 * hl, HEAD), lambda b, t: (b, 0, 0)),
                  pl.BlockSpec((None, 16, 8, hl), lambda b, t: (b, 0, 0, 0)),
                  pl.BlockSpec((1, HEAD), lambda b, t: (0, b))],
        out_specs=[pl.BlockSpec((tt, HEAD), lambda b, t: (t, b)),
                   pl.BlockSpec((None, None, 1, 2 * hl), lambda b, t: (b, t, 0, 0)),
                   pl.BlockSpec((tt, 2 * hl), lambda b, t: (t, b))],
        out_shape=[jax.ShapeDtypeStruct((s, d), F32), jax.ShapeDtypeStruct((nb, nt, 1, 2 * hl), F32),
                   jax.ShapeDtypeStruct((s, nb * 2 * hl), BF16)],
        scratch_shapes=[pltpu.VMEM((tt, 2 * hl), F32), pltpu.VMEM((8, 2 * hl), F32)],
        compiler_params=_params(2),
    )(u, wb, wc, lt, dskip)


def _s5_bwd(u, dy, wb, wc, lt, dskip, cins, states, *, name, tt=512, comm=()):
    s, d = u.shape
    nb = d // HEAD
    tt = min(tt, s)
    nt = s // tt
    hl = S5_HALF

    def body(u_ref, dy_ref, wb_ref, wc_ref, lt_ref, d_ref, cin_ref, st_ref,
             du_ref, dwb_ref, dwc_ref, dd_ref, dlam_ref, abuf, acarry):
        @pl.when(pl.program_id(1) == 0)
        def _():
            acarry[...] = jnp.zeros_like(acarry)
            dwb_ref[...] = jnp.zeros_like(dwb_ref)
            dwc_ref[...] = jnp.zeros_like(dwc_ref)
            dd_ref[...] = jnp.zeros_like(dd_ref)
            dlam_ref[...] = jnp.zeros_like(dlam_ref)

        uv, dyv = u_ref[...], dy_ref[...]
        abuf[...] = _bdot_raw(dyv, wc_ref[...], NT)
        ar, ai = _s5_scan(abuf, lt_ref, acarry[0:1, 0:hl], acarry[0:1, hl:2 * hl], tt, True)
        acarry[0:1, 0:hl] = ar
        acarry[0:1, hl:2 * hl] = ai
        du_ref[...] = _bdot_raw(abuf[...], wb_ref[...], NT) + d_ref[...] * dyv
        dwb_ref[...] += _bdot_raw(uv, abuf[...], TN)
        dwc_ref[...] += _bdot_raw(st_ref[...], dyv, TN)
        dd_ref[...] += jnp.sum(dyv * uv, axis=0, keepdims=True)
        first = lax.broadcasted_iota(jnp.int32, (tt, hl), 0) == 0
        spr = jnp.where(first, cin_ref[:, 0:hl], pltpu.roll(st_ref[:, 0:hl].astype(F32), 1, 0))
        spi = jnp.where(first, cin_ref[:, hl:2 * hl], pltpu.roll(st_ref[:, hl:2 * hl].astype(F32), 1, 0))
        avr, avi = abuf[:, 0:hl], abuf[:, hl:2 * hl]
        dlam_ref[:, 0:hl] += jnp.sum(avr * spr + avi * spi, axis=0, keepdims=True)
        dlam_ref[:, hl:2 * hl] += jnp.sum(avi * spr - avr * spi, axis=0, keepdims=True)

    rev = pl.BlockSpec((tt, HEAD), lambda b, t: (nt - 1 - t, b))
    return _call(
        body, (u, dy, wb, wc, lt, dskip, cins, states), name=name, grid=(nb, nt),
        in_specs=[rev, rev,
                  pl.BlockSpec((None, HEAD, 2 * hl), lambda b, t: (b, 0, 0)),
                  pl.BlockSpec((None, 2 * hl, HEAD), lambda b, t: (b, 0, 0)),
                  pl.BlockSpec((None, 16, 8, hl), lambda b, t: (b, 0, 0, 0)),
                  pl.BlockSpec((1, HEAD), lambda b, t: (0, b)),
                  pl.BlockSpec((None, None, 1, 2 * hl), lambda b, t: (b, nt - 1 - t, 0, 0)),
                  pl.BlockSpec((tt, 2 * hl), lambda b, t: (nt - 1 - t, b))],
        out_specs=[rev,
                   pl.BlockSpec((None, HEAD, 2 * hl), lambda b, t: (b, 0, 0)),
                   pl.BlockSpec((None, 2 * hl, HEAD), lambda b, t: (b, 0, 0)),
                   pl.BlockSpec((1, HEAD), lambda b, t: (0, b)),
                   pl.BlockSpec((None, 1, 2 * hl), lambda b, t: (b, 0, 0))],
        out_shape=[jax.ShapeDtypeStruct((s, d), F32), jax.ShapeDtypeStruct(wb.shape, F32),
                   jax.ShapeDtypeStruct(wc.shape, F32), jax.ShapeDtypeStruct((1, d), F32),
                   jax.ShapeDtypeStruct((nb, 1, 2 * hl), F32)],
        scratch_shapes=[pltpu.VMEM((tt, 2 * hl), F32), pltpu.VMEM((8, 2 * hl), F32)],
        comm=comm)


def _s5_pack(lr, li, br, bi, c_re, c_im):
    g = lr.shape[0]
    nb = g // S5_GB
    eye = jnp.eye(S5_GB, dtype=F32)
    bm = jnp.stack([br, bi]).reshape(2, nb, S5_GB, S5_STATE, S5_GROUP)
    wb = jnp.einsum("rbgpc,gh->bgcrhp", bm, eye).reshape(nb, HEAD, 2 * S5_HALF)
    cm = jnp.stack([c_re, -c_im]).reshape(2, nb, S5_GB, S5_GROUP, S5_STATE)
    wc = jnp.einsum("rbgcp,gh->brgphc", cm, eye).reshape(nb, 2 * S5_HALF, HEAD)
    pw = [(lr, li)]
    for _ in range(7):
        pr, pi = pw[-1]
        pw.append((pr * lr - pi * li, pr * li + pi * lr))
    blk = lambda a: a.reshape(nb, 1, S5_HALF)
    rows = jnp.arange(8).reshape(1, 8, 1)
    tables = []
    for conj, keep, order in ((1.0, lambda n: rows >= n, range(8)), (-1.0, lambda n: rows < 8 - n, range(7, -1, -1))):
        for n in (1, 2, 4):
            tables += [jnp.where(keep(n), blk(pw[n - 1][0]), 0.0), jnp.where(keep(n), conj * blk(pw[n - 1][1]), 0.0)]
        tables += [jnp.concatenate([blk(pw[n][0]) for n in order], axis=1),
                   jnp.concatenate([conj * blk(pw[n][1]) for n in order], axis=1)]
    return wb, wc, jnp.stack(tables, axis=1)


def _s5_unpack(dwb, dwc, dlam):
    nb = dwb.shape[0]
    g = nb * S5_GB
    eye = jnp.eye(S5_GB, dtype=F32)
    db = jnp.einsum("bgcrhp,gh->rbgpc", dwb.reshape(nb, S5_GB, S5_GROUP, 2, S5_GB, S5_STATE), eye)
    db = db.reshape(2, g, S5_STATE * S5_GROUP)
    dc = jnp.einsum("brgphc,gh->rbgcp", dwc.reshape(nb, 2, S5_GB, S5_STATE, S5_GB, S5_GROUP), eye)
    dc = dc.reshape(2, g, S5_GROUP, S5_STATE)
    dl = dlam.reshape(nb, 2, S5_GB, S5_STATE).transpose(1, 0, 2, 3).reshape(2, g, S5_STATE)
    return dl[0], dl[1], db[0], db[1], dc[0], -dc[1]


def _peer(r):
    mx, my, mc = lax.axis_index("x"), lax.axis_index("y"), lax.axis_index("c")
    px = 1 - mx if r & 4 else mx
    py = 1 - my if r & 2 else my
    pc = 1 - mc if r & 1 else mc
    return (px, py, pc), 4 * px + 2 * py + pc


_COMM_SCRATCH = [pltpu.SemaphoreType.DMA((N_DEV - 1,)), pltpu.SemaphoreType.DMA((N_DEV - 1,)), pltpu.SemaphoreType.DMA]


class _AllToAll:
    def __init__(self, x):
        self.x = x
        self.out_shape = jax.ShapeDtypeStruct(x.shape, x.dtype)

    def _copies(self, x_ref, out_ref, send_sems, recv_sems, local_sem):
        _, me = _peer(0)
        mine = pltpu.make_async_copy(x_ref.at[me], out_ref.at[me], local_sem)
        sends, recvs = [], []
        for r in range(1, N_DEV):
            pos, idx = _peer(r)
            sems = dict(send_sem=send_sems.at[r - 1], recv_sem=recv_sems.at[r - 1], device_id=pos, device_id_type=MESH)
            sends.append(pltpu.make_async_remote_copy(src_ref=x_ref.at[idx], dst_ref=out_ref.at[me], **sems))
            recvs.append(pltpu.make_async_remote_copy(src_ref=x_ref.at[idx], dst_ref=out_ref.at[idx], **sems))
        return mine, sends, recvs

    def start(self, *refs):
        mine, sends, _ = self._copies(*refs)
        mine.start()
        for cp in sends:
            cp.start()

    def finish(self, *refs):
        mine, sends, recvs = self._copies(*refs)
        for cp in recvs:
            cp.wait_recv()
        for cp in sends:
            cp.wait_send()
        mine.wait()


class _Gather:
    def __init__(self, x):
        self.x = x
        self.out_shape = jax.ShapeDtypeStruct((N_DEV,) + tuple(x.shape), x.dtype)

    def _copies(self, x_ref, out_ref, send_sems, recv_sems, local_sem):
        mx, my, mc = lax.axis_index("x"), lax.axis_index("y"), lax.axis_index("c")
        me, sibling = (mx, my, mc), (mx, my, 1 - mc)
        chips = [(1 - mx, my), (mx, 1 - my), (1 - mx, 1 - my)]

        def slot(px, py, pc):
            return out_ref.at[4 * px + 2 * py + pc]

        def copy(k, block, to, src=None):
            return pltpu.make_async_remote_copy(
                src_ref=slot(*block) if src is None else src, dst_ref=slot(*block),
                send_sem=send_sems.at[k], recv_sem=recv_sems.at[k], device_id=to, device_id_type=MESH)

        return dict(
            mine=pltpu.make_async_copy(x_ref, slot(*me), local_sem),
            first=[copy(0, me, sibling, src=x_ref)] + [copy(1 + j, me, (*chip, mc), src=x_ref) for j, chip in enumerate(chips)],
            passed=[copy(4 + j, (*chip, mc), sibling) for j, chip in enumerate(chips)],
            over_ici=[copy(1 + j, (*chip, mc), me) for j, chip in enumerate(chips)],
            from_sibling=[copy(0, sibling, me)] + [copy(4 + j, (*chip, 1 - mc), me) for j, chip in enumerate(chips)])

    def start(self, *refs):
        cps = self._copies(*refs)
        cps["mine"].start()
        for cp in cps["first"]:
            cp.start()

    def finish(self, *refs):
        cps = self._copies(*refs)
        for arrived, onward in zip(cps["over_ici"], cps["passed"]):
            arrived.wait_recv()
            onward.start()
        for cp in cps["from_sibling"]:
            cp.wait_recv()
        for cp in cps["first"] + cps["passed"]:
            cp.wait_send()
        cps["mine"].wait()


def _call(body, args, *, name, grid, in_specs, out_specs, out_shape, scratch_shapes=(), comm=()):
    n_in, n_out, n_scr, nc = len(in_specs), len(out_shape), len(scratch_shapes), len(comm)

    def wrapped(*refs):
        ins, c_in = refs[:n_in], refs[n_in:n_in + nc]
        outs, c_out = refs[n_in + nc:n_in + nc + n_out], refs[n_in + nc + n_out:n_in + 2 * nc + n_out]
        scr = refs[n_in + 2 * nc + n_out:n_in + 2 * nc + n_out + n_scr]
        sems = refs[n_in + 2 * nc + n_out + n_scr:]
        ids = [pl.program_id(a) for a in range(len(grid))]
        if nc:
            @pl.when(functools.reduce(operator.and_, [i == 0 for i in ids]))
            def _():
                for k, op in enumerate(comm):
                    op.start(c_in[k], c_out[k], *sems[3 * k:3 * k + 3])

        body(*ins, *outs, *scr)
        if nc:
            @pl.when(functools.reduce(operator.and_, [i == g - 1 for i, g in zip(ids, grid)]))
            def _():
                for k, op in enumerate(comm):
                    op.finish(c_in[k], c_out[k], *sems[3 * k:3 * k + 3])

    any_spec = pl.BlockSpec(memory_space=pl.ANY)
    res = pl.pallas_call(
        wrapped, name=name, grid=grid,
        in_specs=list(in_specs) + [any_spec] * nc, out_specs=list(out_specs) + [any_spec] * nc,
        out_shape=list(out_shape) + [op.out_shape for op in comm],
        scratch_shapes=list(scratch_shapes) + list(_COMM_SCRATCH) * nc,
        compiler_params=_params(len(grid)),
    )(*args, *[op.x for op in comm])
    return list(res[:n_out]), list(res[n_out:])


def _comm_call(op, *, name):
    return _call(lambda: None, (), name=name, grid=(1,), in_specs=[], out_specs=[], out_shape=[], comm=(op,))[1][0]


def _adamw(w, parts, m, v, *, name, tr=128):
    nl, r, c = w.shape
    assert len(parts) == nl
    npart = parts[0].shape[0]
    tr = min(tr, r)
    assert r % tr == 0, (name, r)

    def body(w_ref, m_ref, v_ref, *rest):
        p_refs, (g_ref, d_ref, mo_ref, vo_ref) = rest[:nl], rest[nl:]
        for l in range(nl):
            @pl.when(pl.program_id(0) == l)
            def _():
                g = p_refs[l][0].astype(F32)
                for k in range(1, npart):
                    g = g + p_refs[l][k].astype(F32)
                m2 = ADAM_B1 * m_ref[...] + (1.0 - ADAM_B1) * g
                v2 = ADAM_B2 * v_ref[...] + (1.0 - ADAM_B2) * (g * g)
                m_hat = m2 / (1.0 - ADAM_B1 ** ADAM_STEP)
                v_hat = v2 / (1.0 - ADAM_B2 ** ADAM_STEP)
                g_ref[...] = g
                d_ref[...] = -ADAM_LR * (m_hat / (jnp.sqrt(v_hat) + ADAM_EPS) + ADAM_WD * w_ref[...])
                mo_ref[...] = m2
                vo_ref[...] = v2

    blk = pl.BlockSpec((None, tr, c), lambda l, i: (l, i, 0))
    part_spec = lambda k: pl.BlockSpec((npart, tr, c), lambda l, i: (0, jnp.where(l == k, i, 0), 0))
    return pl.pallas_call(
        body, name=name, grid=(nl, r // tr),
        in_specs=[blk, blk, blk] + [part_spec(k) for k in range(nl)],
        out_specs=[blk] * 4, out_shape=[jax.ShapeDtypeStruct((nl, r, c), F32)] * 4,
        compiler_params=_params(2),
    )(w, m, v, *parts)


def _sum_parts(parts, *, name):
    npart = parts.shape[0]

    def body(p_ref, o_ref):
        g = p_ref[0]
        for k in range(1, npart):
            g = g + p_ref[k]
        o_ref[...] = g

    return pl.pallas_call(
        body, name=name, grid=(1,), in_specs=[_full_spec(parts)],
        out_specs=pl.BlockSpec(parts.shape[1:], lambda i: (0, 0)),
        out_shape=jax.ShapeDtypeStruct(parts.shape[1:], F32), compiler_params=_params(1),
    )(parts)


def _pack(arrs):
    flat = jnp.concatenate([a.reshape(-1).astype(F32) for a in arrs])
    pad = (-flat.shape[0]) % (HEAD * HEAD)
    return jnp.pad(flat, (0, pad)).reshape(-1, HEAD)


def _unpack(packed, shapes):
    flat = packed.reshape(-1)
    out, off = [], 0
    for shp in shapes:
        size = math.prod(shp)
        out.append(flat[off:off + size].reshape(shp))
        off += size
    return out


def _add_epilogue(acc, res):
    return (acc + res,)


def _relu2_epilogue(acc):
    r = jnp.maximum(acc, 0.0)
    return acc, r * r


def _ple_epilogue(acc, gpre, h):
    return h + jax.nn.sigmoid(gpre) * acc, acc


def kernel(x, p, norm_mix, norm_mlp, norm_ple, w_in_e, w_out_e, hgrn_lb, g_norm_a, conv_w, a_log, dt_bias, g_norm_b, s5_a_re, s5_a_im, s5_b_re, s5_b_im, s5_c_re, s5_c_im, s5_d, s5_log_dt, w_glu, b_glu, w_out_o, w_up, w_down, w_ple_gate, w_ple_proj, final_norm, loss_target, m_norm_mix, m_norm_mlp, m_norm_ple, m_w_in_e, m_w_out_e, m_hgrn_lb, m_g_norm_a, m_conv_w, m_a_log, m_dt_bias, m_g_norm_b, m_s5_a_re, m_s5_a_im, m_s5_b_re, m_s5_b_im, m_s5_c_re, m_s5_c_im, m_s5_d, m_s5_log_dt, m_w_glu, m_b_glu, m_w_out_o, m_w_up, m_w_down, m_w_ple_gate, m_w_ple_proj, m_final_norm, v_norm_mix, v_norm_mlp, v_norm_ple, v_w_in_e, v_w_out_e, v_hgrn_lb, v_g_norm_a, v_conv_w, v_a_log, v_dt_bias, v_g_norm_b, v_s5_a_re, v_s5_a_im, v_s5_b_re, v_s5_b_im, v_s5_c_re, v_s5_c_im, v_s5_d, v_s5_log_dt, v_w_glu, v_b_glu, v_w_out_o, v_w_up, v_w_down, v_w_ple_gate, v_w_ple_proj, v_final_norm):
    args = dict(locals())
    s, d = x.shape[1], x.shape[2]
    aw = d // 2
    ha = hb = aw // HEAD
    main = 4 * d
    z_col = 2 * d + 3 * aw
    ff = w_up.shape[2] * N_DEV
    ple = p.shape[-1]
    groups = d // S5_GROUP
    me = 4 * lax.axis_index("x") + 2 * lax.axis_index("y") + lax.axis_index("c")
    x2, target = x[0], loss_target[0]
    row = lambda a, i: a[i:i + 1]

    def gather_of(w):
        return _Gather(w.astype(BF16))

    w_in = jnp.transpose(_comm_call(gather_of(w_in_e[0]), name="ag_w_in"), (1, 0, 2)).reshape(d, -1)
    w_main = w_in[:, :main]
    w_tail = jnp.pad(w_in[:, main:], ((0, 0), (0, HEAD - 2 * hb)))

    lb_rows = [row(hgrn_lb, 0), row(hgrn_lb, 1), row(hgrn_lb, 2)]
    (lb0,) = _small_call(_lb0_stage, lb_rows, name="f_lb0")
    hp = jnp.zeros((8, HEAD), F32).at[0, :hb].set(a_log[0]).at[1, :hb].set(dt_bias[0])
    expand = jnp.asarray(np.kron(np.eye(S5_STATE, dtype=np.float32), np.ones((1, S5_GROUP), np.float32)))
    prep_in = [s5_a_re[0], s5_a_im[0], s5_log_dt[0].reshape(groups, 1),
               s5_b_re[0].reshape(groups, -1), s5_b_im[0].reshape(groups, -1), expand]
    lr, li, br, bi = _small_call(_s5_prep_stage, prep_in, name="f_s5_prep")
    wb, wc, lt = _s5_pack(lr, li, br, bi, s5_c_re[0], s5_c_im[0])
    fnorm = final_norm.reshape(1, d)

    def block_fwd(h, l, w_dn):
        hn = _rows_call(_rms_stage, [h], [row(norm_mlp, l)], [BF16], name=f"f_norm_mlp{l}")
        up_args = dict(epilogue=_relu2_epilogue, out_dtypes=(F32, BF16), name=f"f_up{l}")
        next_dn = None
        if w_dn is None:
            (up, act), (dn0,) = _mm(hn, w_upg[l], comm=(gather_of(w_down[0]),), **up_args)
            w_dn = dn0.reshape(ff, d)
            h2, (dn1,) = _mm(act, w_dn, extras=(h,), epilogue=_add_epilogue, name=f"f_down{l}",
                             comm=(gather_of(w_down[1]),))
            next_dn = dn1.reshape(ff, d)
        else:
            up, act = _mm(hn, w_upg[l], **up_args)
            h2 = _mm(act, w_dn, extras=(h,), epilogue=_add_epilogue, name=f"f_down{l}")
        hq = _rows_call(_rms_stage, [h2], [row(norm_ple, l)], [BF16], name=f"f_norm_ple{l}")
        gpre = _mm(hq, w_pgg[l], name=f"f_ple_gate{l}")
        h3, pp = _mm(p[l, 0], w_ppg[l], extras=(gpre, h2), epilogue=_ple_epilogue, out_dtypes=(F32, F32),
                     name=f"f_ple_proj{l}")
        return h3, dict(h=h, hn=hn, up=up, act=act, h2=h2, hq=hq, gpre=gpre, pp=pp, w_dn=w_dn), next_dn

    hn0 = _rows_call(_rms_stage, [x2], [row(norm_mix, 0)], [BF16], name="f_norm_mix0")
    shard_shapes = [conv_w[0].shape, s5_d.shape, b_glu.shape]
    proj, (oe8, pg8, small) = _mm(hn0, w_main, name="f_proj", comm=(
        gather_of(w_out_e[0]), gather_of(w_ple_gate), _Gather(_pack([conv_w[0], s5_d, b_glu]))))
    w_oe = oe8.reshape(d, d)
    w_top, w_bot = w_oe[:aw], w_oe[aw:]
    w_pgg = jnp.transpose(pg8, (1, 0, 2, 3)).reshape(2, d, d)
    conv_g, s5d_g, bglu_g = zip(*[_unpack(small[j], shard_shapes) for j in range(N_DEV)])
    conv_full = jnp.concatenate(conv_g, axis=1)
    s5d_full = jnp.concatenate(s5d_g, axis=1)
    bglu_full = jnp.concatenate(bglu_g, axis=1)
    ab = _mm(hn0, w_tail, name="f_ab")
    (oa, st_a), (gl8, oo8) = _hgrn_fwd(proj, lb0, g_norm_a, heads=ha, name="f_hgrn",
                                       comm=(gather_of(w_glu[0]), gather_of(w_out_o[0])))
    w_gl, w_oo = gl8.reshape(d, d), oo8.reshape(d, d)
    qkv = _conv_fwd(proj, conv_full, col_off=2 * d, name="f_conv")
    (ob, st_b), (up8,) = _delta_fwd(qkv, ab, proj, hp, g_norm_b, heads=hb, z_off=z_col // HEAD, name="f_delta",
                                    comm=(gather_of(w_up),))
    w_upg = jnp.transpose(up8, (1, 2, 0, 3)).reshape(2, d, ff)
    h1, (pp8,) = _mm(oa, w_top, extras=(x2,), epilogue=_add_epilogue, name="f_out_a", comm=(gather_of(w_ple_proj),))
    w_ppg = jnp.transpose(pp8, (1, 2, 0, 3)).reshape(2, ple, d)
    h1 = _mm(ob, w_bot, extras=(h1,), epilogue=_add_epilogue, name="f_out_b")
    h3, sv0, w_dn1 = block_fwd(h1, 0, None)

    u = _rows_call(_rms_stage, [h3], [row(norm_mix, 1)], [F32], name="f_norm_mix1")
    y, cins, s5_states = _s5_fwd(u, wb, wc, lt, s5d_full, name="f_s5")
    act_g = _rows_call(_gelu_stage, [y], [], [BF16], name="f_gelu")
    gl_raw = _mm(act_g, w_gl, name="f_glu")
    glu = _rows_call(_glu_stage, [y, gl_raw], [bglu_full], [BF16], name="f_glu_gate")
    h4 = _mm(glu, w_oo, extras=(h3,), epilogue=_add_epilogue, name="f_out_o")
    h6, sv1, _ = block_fwd(h4, 1, w_dn1)
    dh, d_fnorm, loss8 = _loss_call(h6, fnorm, target, name="loss")
    loss = lax.psum(loss8[0, 0], ("x", "y", "c"))

    dshard, ffs, cols = d // N_DEV, ff // N_DEV, w_in_e.shape[2]
    rows8 = lambda g: _AllToAll(g.reshape(N_DEV, -1, g.shape[-1]))
    cols8 = lambda g: _AllToAll(jnp.transpose(g.reshape(g.shape[0], N_DEV, -1), (1, 0, 2)))

    def block_bwd(dh3, l, sv, carried, carried_up=()):
        (dgpre, dpp), _ = _rows_vjp(_ple_stage, [sv["h2"], sv["gpre"], sv["pp"]], [], [dh3],
                                    row_grads={1: BF16, 2: BF16}, name=f"b_ple{l}")
        g_pp = _mm(p[l, 0], dpp, ta=True, out_dtypes=(BF16,), name=f"b_w_ple_proj{l}")
        g_pg = _mm(sv["hq"], dgpre, ta=True, out_dtypes=(BF16,), name=f"b_w_ple_gate{l}")
        dhq = _mm(dgpre, w_pgg[l], tb=True, name=f"b_ple_gate{l}")
        (dh2,), (g_nple,) = _rows_vjp(_rms_stage, [sv["h2"]], [row(norm_ple, l)], [dhq], row_grads={0: F32},
                                      adds={0: dh3}, name=f"b_norm_ple{l}")
        dup, (r_pg, r_pp) = _mm(dh2, sv["w_dn"], tb=True, extras=(sv["up"],), epilogue=_relu2_grad_epilogue,
                                out_dtypes=(BF16,), name=f"b_down{l}", comm=(rows8(g_pg), cols8(g_pp)))
        g_dn = _mm(sv["act"], dh2, ta=True, out_dtypes=(BF16,), name=f"b_w_down{l}", comm=carried)
        g_dn, r_carried = g_dn if carried else (g_dn, [])
        g_up = _mm(sv["hn"], dup, ta=True, out_dtypes=(BF16,), tn=ffs, out_slots=True, name=f"b_w_up{l}")
        dhn = _mm(dup, w_upg[l], tb=True, name=f"b_up{l}", comm=carried_up)
        dhn, r_carried_up = dhn if carried_up else (dhn, [])
        (dh0,), (g_nmlp,) = _rows_vjp(_rms_stage, [sv["h"]], [row(norm_mlp, l)], [dhn], row_grads={0: F32},
                                      adds={0: dh2}, name=f"b_norm_mlp{l}")
        return dh0, dict(w_ple_proj=r_pp, w_ple_gate=r_pg, norm_ple=g_nple, w_down=g_dn, w_up=g_up, norm_mlp=g_nmlp,
                         carried=r_carried, carried_up=r_carried_up)

    dh4, gb1 = block_bwd(dh, 1, sv1, ())
    dglu = _mm(dh4, w_oo, tb=True, name="b_out_o")
    g_oo = _mm(glu, dh4, ta=True, out_dtypes=(BF16,), name="b_w_out_o")
    (dy1, dgl), (g_bglu,) = _rows_vjp(_glu_stage, [y, gl_raw], [bglu_full], [dglu], row_grads={0: F32, 1: BF16},
                                      name="b_glu_gate")
    g_gl = _mm(act_g, dgl, ta=True, out_dtypes=(BF16,), name="b_w_glu")
    dact = _mm(dgl, w_gl, tb=True, name="b_glu")
    (dy,), _ = _rows_vjp(_gelu_stage, [y], [], [dact], row_grads={0: F32}, adds={0: dy1}, name="b_gelu")
    (du, dwb, dwc, g_s5d, dlam), (r_dn1, r_up1) = _s5_bwd(u, dy, wb, wc, lt, s5d_full, cins, s5_states, name="b_s5",
                                                         comm=(rows8(gb1["w_down"]), _AllToAll(gb1["w_up"])))
    (dh3,), (g_nmix1,) = _rows_vjp(_rms_stage, [h3], [row(norm_mix, 1)], [du], row_grads={0: F32}, adds={0: dh4},
                                   name="b_norm_mix1")
    dlr, dli, dbr, dbi, g_cre, g_cim = _s5_unpack(dwb, dwc, dlam)
    g_are, g_aim, g_ldt, g_bre, g_bim, _ = _small_vjp(_s5_prep_stage, prep_in, [dlr, dli, dbr, dbi], name="b_s5_prep")

    early_grads = dict(
        s5_a_re=g_are[None], s5_a_im=g_aim[None], s5_b_re=g_bre.reshape(s5_b_re.shape),
        s5_b_im=g_bim.reshape(s5_b_im.shape), s5_c_re=g_cre[None], s5_c_im=g_cim[None],
        s5_log_dt=g_ldt.reshape(1, groups), final_norm=d_fnorm.reshape(d), s5_d=g_s5d, b_glu=g_bglu)
    dh1, gb0 = block_bwd(dh3, 0, sv0, (rows8(g_oo), rows8(g_gl)), (_Gather(_pack(list(early_grads.values()))),))
    r_oo, r_gl = gb0["carried"]
    (early_parts,) = gb0["carried_up"]
    doa = _mm(dh1, w_top, tb=True, name="b_out_a")
    dob = _mm(dh1, w_bot, tb=True, name="b_out_b")
    g_oe = jnp.concatenate([_mm(oa, dh1, ta=True, out_dtypes=(BF16,), name="b_w_out_a"),
                            _mm(ob, dh1, ta=True, out_dtypes=(BF16,), name="b_w_out_b")], axis=0)
    dq, df, di, dg, dlb, g_gna = _hgrn_bwd(proj, lb0, g_norm_a, st_a, doa, heads=ha, name="b_hgrn")
    (dqb, dkb, dvb, dab, dz, dhp, g_gnb), (r_dn0, r_up0, r_oe) = _delta_bwd(
        qkv, ab, proj, hp, g_norm_b, st_b, dob, heads=hb, z_off=z_col // HEAD, name="b_delta",
        comm=(rows8(gb0["w_down"]), _AllToAll(gb0["w_up"]), rows8(g_oe)))
    dqkv, g_conv = _conv_bwd(proj, conv_full, jnp.concatenate([dqb, dkb, dvb], axis=1), col_off=2 * d, name="b_conv")
    dproj = jnp.concatenate([dq, df, di, dg, dqkv, dz], axis=1)
    g_main = _mm(hn0, dproj, ta=True, out_dtypes=(BF16,), name="b_w_proj")
    g_tail = _mm(hn0, dab, ta=True, out_dtypes=(BF16,), name="b_w_ab")
    dhn0, (r_in,) = _mm(dproj, w_main, tb=True, name="b_proj",
                        comm=(cols8(jnp.concatenate([g_main, g_tail[:, :2 * hb]], axis=1)),))
    dhn0 = _mm(dab, w_tail, tb=True, extras=(dhn0,), epilogue=_add_epilogue, name="b_ab")
    (dx,), (g_nmix0,) = _rows_vjp(_rms_stage, [x2], [row(norm_mix, 0)], [dhn0], row_grads={0: F32}, adds={0: dh1},
                                  name="b_norm_mix0")
    g_lb = jnp.concatenate(_small_vjp(_lb0_stage, lb_rows, [dlb], name="b_lb0"), axis=0)

    late_grads = dict(
        norm_mix=jnp.concatenate([g_nmix0, g_nmix1], axis=0),
        norm_mlp=jnp.concatenate([gb0["norm_mlp"], gb1["norm_mlp"]], axis=0),
        norm_ple=jnp.concatenate([gb0["norm_ple"], gb1["norm_ple"]], axis=0),
        hgrn_lb=g_lb, g_norm_a=g_gna, a_log=dhp[0:1, :hb], dt_bias=dhp[1:2, :hb], g_norm_b=g_gnb, conv_w=g_conv)
    rep_names = ["norm_mix", "norm_mlp", "norm_ple", "hgrn_lb", "g_norm_a", "a_log", "dt_bias", "g_norm_b", "s5_a_re",
                 "s5_a_im", "s5_b_re", "s5_b_im", "s5_c_re", "s5_c_im", "s5_log_dt", "final_norm"]
    late_parts = _comm_call(_Gather(_pack(list(late_grads.values()))), name="ag_small_grads")
    summed = {}
    for tag, grads, parts in (("early", early_grads, early_parts), ("late", late_grads, late_parts)):
        sums = _unpack(_sum_parts(parts, name=f"sum_small_grads_{tag}"), [g.shape for g in grads.values()])
        summed.update(zip(grads, sums))
    cw = conv_w.shape[2]
    dshard = d // N_DEV
    shard_g = dict(conv_w=lax.dynamic_slice(summed["conv_w"], (0, me * cw), (CONV_WIDTH, cw))[None],
                   s5_d=lax.dynamic_slice(summed["s5_d"], (0, me * dshard), (1, dshard)),
                   b_glu=lax.dynamic_slice(summed["b_glu"], (0, me * dshard), (1, dshard)))
    small_names = rep_names + ["conv_w", "s5_d", "b_glu"]
    g_small = [summed[k] if k in rep_names else shard_g[k] for k in small_names]
    shapes = [args[k].shape for k in small_names]
    sm_out = _adamw(_pack([args[k] for k in small_names])[None], [_pack(g_small)[None]],
                    _pack([args["m_" + k] for k in small_names])[None], _pack([args["v_" + k] for k in small_names])[None],
                    name="adamw_small")
    sm_out = [dict(zip(small_names, _unpack(o[0], shapes))) for o in sm_out]

    received = dict(w_in_e=[r_in], w_out_e=[r_oe], w_glu=[r_gl], w_out_o=[r_oo], w_up=[r_up0, r_up1],
                    w_down=[r_dn0, r_dn1], w_ple_gate=[gb0["w_ple_gate"], gb1["w_ple_gate"]],
                    w_ple_proj=[gb0["w_ple_proj"], gb1["w_ple_proj"]])
    big_out = {k: _adamw(args[k], layers, args["m_" + k], args["v_" + k], name="adamw_" + k)
               for k, layers in received.items()}

    names = ["norm_mix", "norm_mlp", "norm_ple", "w_in_e", "w_out_e", "hgrn_lb", "g_norm_a", "conv_w", "a_log", "dt_bias",
             "g_norm_b", "s5_a_re", "s5_a_im", "s5_b_re", "s5_b_im", "s5_c_re", "s5_c_im", "s5_d", "s5_log_dt", "w_glu",
             "b_glu", "w_out_o", "w_up", "w_down", "w_ple_gate", "w_ple_proj", "final_norm"]
    result = [loss, dx[None]]
    for j in range(4):
        result += [big_out[k][j] if k in big_out else sm_out[j][k] for k in names]
    return tuple(result)
```

```python
import functools
import math
import operator

import numpy as np
import jax
import jax.numpy as jnp
from jax import lax
from jax.experimental import pallas as pl
from jax.experimental.pallas import tpu as pltpu

F32 = jnp.float32
BF16 = jnp.bfloat16
MM_DTYPE = BF16
HI = lax.Precision.HIGHEST
MESH = pl.DeviceIdType.MESH

NORM_EPS = 1e-6
CHUNK = 64
HEAD = 128
CONV_WIDTH = 4
S5_GROUP = 16
S5_STATE = 64
S5_GB = 8
S5_HALF = S5_GB * S5_STATE
N_DEV = 8
HEADS_PER_STEP = 8
ADAM_LR, ADAM_B1, ADAM_B2, ADAM_EPS, ADAM_WD, ADAM_STEP = 0.001, 0.9, 0.999, 1e-08, 0.01, 10
VMEM_LIMIT = 56 * 1024 * 1024

NN = (((1,), (0,)), ((), ()))
NT = (((1,), (1,)), ((), ()))
TN = (((0,), (0,)), ((), ()))


def _dot(a, b, dn=NN):
    return lax.dot_general(a, b, dn, precision=HI, preferred_element_type=F32)


def _hdot(a, b, dn=NN):
    return lax.dot_general(a, b, dn, precision=lax.Precision.HIGH, preferred_element_type=F32)


def _bdot_raw(a, b, dn=NN):
    return lax.dot_general(a.astype(BF16), b.astype(BF16), dn, preferred_element_type=F32)


@functools.partial(jax.custom_vjp, nondiff_argnums=(2,))
def _bdot(a, b, dn):
    return _bdot_raw(a, b, dn)


def _bdot_fwd(a, b, dn):
    return _bdot_raw(a, b, dn), (a, b)


def _bdot_bwd(dn, res, g):
    a, b = res
    if dn == NN:
        return _bdot_raw(g, b, NT), _bdot_raw(a, g, TN)
    if dn == NT:
        return _bdot_raw(g, b, NN), _bdot_raw(g, a, TN)
    assert dn == TN
    return _bdot_raw(b, g, NT), _bdot_raw(a, g, NN)


_bdot.defvjp(_bdot_fwd, _bdot_bwd)


def _per_head(f):
    def g(*args, **kw):
        n = [len(a.vals) for a in args if isinstance(a, _Heads)]
        if not n:
            return f(*args, **kw)
        return _Heads([f(*[a.vals[j] if isinstance(a, _Heads) else a for a in args], **kw) for j in range(n[0])])
    return g


class _Heads:
    def __init__(self, vals):
        self.vals = list(vals)

    def __add__(self, o):
        return _per_head(operator.add)(self, o)

    def __radd__(self, o):
        return _per_head(operator.add)(o, self)

    def __sub__(self, o):
        return _per_head(operator.sub)(self, o)

    def __rsub__(self, o):
        return _per_head(operator.sub)(o, self)

    def __mul__(self, o):
        return _per_head(operator.mul)(self, o)

    def __rmul__(self, o):
        return _per_head(operator.mul)(o, self)

    def __neg__(self):
        return _per_head(operator.neg)(self)


_exp, _log, _where, _sum, _mean = (_per_head(f) for f in (jnp.exp, jnp.log, jnp.where, jnp.sum, jnp.mean))
_sigmoid, _rsqrt, _equal = _per_head(jax.nn.sigmoid), _per_head(lax.rsqrt), _per_head(operator.eq)
_hdot_h, _bdot_h = _per_head(_hdot), _per_head(_bdot)


def _params(n_axes):
    return pltpu.CompilerParams(dimension_semantics=("arbitrary",) * n_axes, vmem_limit_bytes=VMEM_LIMIT)


def _full_spec(a):
    nd = a.ndim
    return pl.BlockSpec(a.shape, lambda *_: (0,) * nd)


def _mm(a, b, *, name, ta=False, tb=False, extras=(), epilogue=None, out_dtypes=(F32,), tm=1024, tn=1024, tk=2048,
        out_slots=False, comm=()):
    m = a.shape[1] if ta else a.shape[0]
    k = a.shape[0] if ta else a.shape[1]
    n = b.shape[0] if tb else b.shape[1]
    assert k == (b.shape[1] if tb else b.shape[0]), (name, a.shape, b.shape)
    tm, tn, tk = min(tm, m), min(tn, n), min(tk, k)
    assert m % tm == 0 and n % tn == 0 and k % tk == 0, (name, m, n, k)
    nk = k // tk
    n_ex, n_out = len(extras), len(out_dtypes)
    dn = (((0 if ta else 1,), (1 if tb else 0,)), ((), ()))

    def body(a_ref, b_ref, *rest):
        ex_refs, out_refs = rest[:n_ex], rest[n_ex:n_ex + n_out]
        part = lax.dot_general(a_ref[...].astype(MM_DTYPE), b_ref[...].astype(MM_DTYPE), dn, preferred_element_type=F32)

        def finish(acc):
            outs = epilogue(acc, *[r[...] for r in ex_refs]) if epilogue is not None else (acc,)
            for o_ref, o in zip(out_refs, outs):
                o_ref[...] = o.astype(o_ref.dtype)

        if nk == 1:
            finish(part)
            return
        acc_ref = rest[-1]
        kk = pl.program_id(2)

        @pl.when(kk == 0)
        def _():
            acc_ref[...] = part

        @pl.when((kk > 0) & (kk < nk - 1))
        def _():
            acc_ref[...] += part

        @pl.when(kk == nk - 1)
        def _():
            finish(acc_ref[...] + part)

    a_spec = pl.BlockSpec((tk, tm), lambda i, j, q: (q, i)) if ta else pl.BlockSpec((tm, tk), lambda i, j, q: (i, q))
    b_spec = pl.BlockSpec((tn, tk), lambda i, j, q: (j, q)) if tb else pl.BlockSpec((tk, tn), lambda i, j, q: (q, j))
    ex_specs = []
    for e in extras:
        if e.shape[0] == 1 and m != 1:
            ex_specs.append(pl.BlockSpec((1, tn), lambda i, j, q: (0, j)))
        else:
            ex_specs.append(pl.BlockSpec((tm, tn), lambda i, j, q: (i, j)))
    if out_slots:
        out_spec, out_dims = pl.BlockSpec((None, tm, tn), lambda i, j, q: (j, i, 0)), (n // tn, m, tn)
    else:
        out_spec, out_dims = pl.BlockSpec((tm, tn), lambda i, j, q: (i, j)), (m, n)
    outs, exchanged = _call(
        body, (a, b, *extras), name=name, grid=(m // tm, n // tn, nk),
        in_specs=[a_spec, b_spec] + ex_specs,
        out_specs=[out_spec for _ in out_dtypes],
        out_shape=[jax.ShapeDtypeStruct(out_dims, dt) for dt in out_dtypes],
        scratch_shapes=[pltpu.VMEM((tm, tn), F32)] if nk > 1 else [], comm=comm)
    outs = outs[0] if n_out == 1 else tuple(outs)
    return (outs, exchanged) if comm else outs


def _rows_call(fn, rows, consts, out_dtypes, *, name, tr=256):
    s = rows[0].shape[0]
    tr = min(tr, s)
    nr, nc = len(rows), len(consts)
    widths = [o.shape[1] for o in jax.eval_shape(
        fn, *[jax.ShapeDtypeStruct((tr, r.shape[1]), F32) for r in rows],
        *[jax.ShapeDtypeStruct(c.shape, F32) for c in consts])]

    def body(*refs):
        rv = [r[...].astype(F32) for r in refs[:nr]]
        cv = [c[...] for c in refs[nr:nr + nc]]
        for o_ref, o in zip(refs[nr + nc:], fn(*rv, *cv)):
            o_ref[...] = o.astype(o_ref.dtype)

    outs = pl.pallas_call(
        body, name=name, grid=(s // tr,),
        in_specs=[pl.BlockSpec((tr, r.shape[1]), lambda i: (i, 0)) for r in rows] + [_full_spec(c) for c in consts],
        out_specs=[pl.BlockSpec((tr, w), lambda i: (i, 0)) for w in widths],
        out_shape=[jax.ShapeDtypeStruct((s, w), dt) for w, dt in zip(widths, out_dtypes)],
        compiler_params=_params(1),
    )(*rows, *consts)
    return outs[0] if len(outs) == 1 else tuple(outs)


def _rows_vjp(fn, rows, consts, cots, *, name, row_grads, adds=None, tr=256):
    adds = adds or {}
    s = rows[0].shape[0]
    tr = min(tr, s)
    nr, nc, nt = len(rows), len(consts), len(cots)
    rg = sorted(row_grads)
    ad = sorted(adds)

    def body(*refs):
        rv = [r[...].astype(F32) for r in refs[:nr]]
        cv = [c[...] for c in refs[nr:nr + nc]]
        ct = [c[...].astype(F32) for c in refs[nr + nc:nr + nc + nt]]
        av = {i: r[...].astype(F32) for i, r in zip(ad, refs[nr + nc + nt:nr + nc + nt + len(ad)])}
        out_refs = refs[nr + nc + nt + len(ad):]
        _, vjp = jax.vjp(fn, *rv, *cv)
        grads = vjp(tuple(ct))
        for o_ref, i in zip(out_refs[:len(rg)], rg):
            g = grads[i]
            if i in av:
                g = g + av[i]
            o_ref[...] = g.astype(o_ref.dtype)

        @pl.when(pl.program_id(0) == 0)
        def _():
            for o_ref in out_refs[len(rg):]:
                o_ref[...] = jnp.zeros_like(o_ref)

        for o_ref, g in zip(out_refs[len(rg):], grads[nr:]):
            o_ref[...] += g

    row_spec = lambda a: pl.BlockSpec((tr, a.shape[1]), lambda i: (i, 0))
    outs = pl.pallas_call(
        body, name=name, grid=(s // tr,),
        in_specs=[row_spec(r) for r in rows] + [_full_spec(c) for c in consts] + [row_spec(c) for c in cots]
        + [row_spec(adds[i]) for i in ad],
        out_specs=[row_spec(rows[i]) for i in rg] + [_full_spec(c) for c in consts],
        out_shape=[jax.ShapeDtypeStruct(rows[i].shape, row_grads[i]) for i in rg]
        + [jax.ShapeDtypeStruct(c.shape, F32) for c in consts],
        compiler_params=_params(1),
    )(*rows, *consts, *cots, *[adds[i] for i in ad])
    return list(outs[:len(rg)]), list(outs[len(rg):])


def _small_call(fn, ins, *, name):
    shapes = jax.eval_shape(fn, *[jax.ShapeDtypeStruct(a.shape, F32) for a in ins])

    def body(*refs):
        for o_ref, o in zip(refs[len(ins):], fn(*[r[...] for r in refs[:len(ins)]])):
            o_ref[...] = o

    return pl.pallas_call(
        body, name=name, in_specs=[_full_spec(a) for a in ins],
        out_specs=[pl.BlockSpec(o.shape, functools.partial(lambda nd, *_: (0,) * nd, len(o.shape))) for o in shapes],
        out_shape=[jax.ShapeDtypeStruct(o.shape, F32) for o in shapes], grid=(1,),
        compiler_params=_params(1),
    )(*ins)


def _small_vjp(fn, ins, cots, *, name):
    def body(*refs):
        vals = [r[...] for r in refs[:len(ins)]]
        ct = [r[...] for r in refs[len(ins):len(ins) + len(cots)]]
        _, vjp = jax.vjp(fn, *vals)
        for o_ref, g in zip(refs[len(ins) + len(cots):], vjp(tuple(ct))):
            o_ref[...] = g

    return pl.pallas_call(
        body, name=name, in_specs=[_full_spec(a) for a in ins] + [_full_spec(c) for c in cots],
        out_specs=[_full_spec(a) for a in ins],
        out_shape=[jax.ShapeDtypeStruct(a.shape, F32) for a in ins], grid=(1,),
        compiler_params=_params(1),
    )(*ins, *cots)


def _rms(x, g):
    return x * _rsqrt(_mean(x * x, axis=-1, keepdims=True) + NORM_EPS) * g


def _rms_stage(x, g):
    return (_rms(x, g),)


def _silu(x):
    return x * _sigmoid(x)


def _softplus(x):
    return jnp.maximum(x, 0.0) + jnp.log1p(jnp.exp(-jnp.abs(x)))


def _gelu(x):
    return jax.nn.gelu(x, approximate=True)


def _gelu_stage(y):
    return (_gelu(y),)


def _glu_stage(y, gl_raw, b):
    return (_gelu(y) * jax.nn.sigmoid(gl_raw + b),)


def _ple_stage(h, gpre, pp):
    return (h + jax.nn.sigmoid(gpre) * pp,)


def _relu2_grad_epilogue(acc, up):
    return (acc * (2.0 * jnp.maximum(up, 0.0)),)


def _lb0_stage(x0, x1, x2):
    mx = jnp.maximum(jnp.maximum(x0, x1), x2)
    e0, e1, e2 = jnp.exp(x0 - mx), jnp.exp(x1 - mx), jnp.exp(x2 - mx)
    return (e0 / (e0 + e1 + e2),)


def _s5_prep_stage(a_re, a_im, log_dt, b_re, b_im, expand):
    step = jnp.exp(log_dt)
    mag = jnp.exp(a_re * step)
    lr = mag * jnp.cos(a_im * step)
    li = mag * jnp.sin(a_im * step)
    den = a_re * a_re + a_im * a_im
    cr = ((lr - 1.0) * a_re + li * a_im) / den
    ci = (li * a_re - (lr - 1.0) * a_im) / den
    cr_e, ci_e = _dot(cr, expand), _dot(ci, expand)
    return lr, li, cr_e * b_re - ci_e * b_im, cr_e * b_im + ci_e * b_re


def _loss_call(h, g, target, *, name, tr=256):
    s, d = h.shape
    tr = min(tr, s)

    def loss_fn(hv, gv, tv):
        err = _rms(hv, gv) - tv
        return 0.5 * jnp.sum(jnp.mean(err * err, axis=-1))

    def body(h_ref, g_ref, t_ref, dh_ref, dg_ref, loss_ref):
        val, (dh, dg) = jax.value_and_grad(loss_fn, argnums=(0, 1))(h_ref[...], g_ref[...], t_ref[...])
        dh_ref[...] = dh

        @pl.when(pl.program_id(0) == 0)
        def _():
            dg_ref[...] = jnp.zeros_like(dg_ref)
            loss_ref[...] = jnp.zeros_like(loss_ref)

        dg_ref[...] += dg
        loss_ref[...] += jnp.full(loss_ref.shape, val, F32)

    row = pl.BlockSpec((tr, d), lambda i: (i, 0))
    return pl.pallas_call(
        body, name=name, grid=(s // tr,),
        in_specs=[row, _full_spec(g), row],
        out_specs=[row, _full_spec(g), pl.BlockSpec((8, 128), lambda i: (0, 0))],
        out_shape=[jax.ShapeDtypeStruct((s, d), F32), jax.ShapeDtypeStruct(g.shape, F32),
                   jax.ShapeDtypeStruct((8, 128), F32)],
        compiler_params=_params(1),
    )(h, g, target)


def _hgrn_chunk(q, fp, iv, gp, lb, gn, st_t):
    c = CHUNK
    row = lax.broadcasted_iota(jnp.int32, (c, c), 0)
    col = lax.broadcasted_iota(jnp.int32, (c, c), 1)
    causal = row >= col
    fg = lb + (1.0 - lb) * _sigmoid(fp)
    k = 1.0 - fg
    lf = _log(fg)
    cum = _hdot_h(causal.astype(F32), lf, NN)
    first_half = (lax.broadcasted_iota(jnp.int32, (c, 1), 0) < c // 2).astype(F32)
    ref = _sum(lf * first_half, axis=0, keepdims=True)
    cend = _sum(lf, axis=0, keepdims=True)
    scores = _where(causal, _hdot_h(q * _exp(cum - ref), k * _exp(ref - cum), NT), 0.0)
    out = _bdot_h(scores, iv, NN) + _bdot_h(q * _exp(cum), st_t, NT)
    st_new = st_t * _exp(cend) + _bdot_h(iv, k * _exp(cend - cum), TN)
    res = _rms(out, gn) * _silu(gp)
    return res, st_new


def _hgrn_heads(qs, fs, ivs, gs, lbs, gn, sts):
    res, st_new = _hgrn_chunk(_Heads(qs), _Heads(fs), _Heads(ivs), _Heads(gs), _Heads(lbs), gn, _Heads(sts))
    return res.vals, st_new.vals


def _lanes(j):
    return slice(j * HEAD, (j + 1) * HEAD)


def _hgrn_fwd(proj, lb, gn, *, heads, name, comm=()):
    s = proj.shape[0]
    n = s // CHUNK
    hpb = min(HEADS_PER_STEP, heads)
    assert heads % hpb == 0

    def body(q_ref, f_ref, i_ref, g_ref, lb_ref, gn_ref, o_ref, st_ref, state):
        @pl.when(pl.program_id(1) == 0)
        def _():
            state[...] = jnp.zeros_like(state)

        gnv = gn_ref[...]
        loaded = [(q_ref[:, _lanes(j)], f_ref[:, _lanes(j)], i_ref[:, _lanes(j)], g_ref[:, _lanes(j)],
                   lb_ref[:, _lanes(j)], state[j]) for j in range(hpb)]
        qs, fs, ivs, gs, lbs, sts = (list(t) for t in zip(*loaded))
        res, st_new = _hgrn_heads(qs, fs, ivs, gs, lbs, gnv, sts)
        for j in range(hpb):
            st_ref[j] = sts[j]
            o_ref[:, _lanes(j)] = res[j].astype(o_ref.dtype)
            state[j] = st_new[j]

    wide = hpb * HEAD
    blk = lambda off: pl.BlockSpec((CHUNK, wide), lambda h, c: (c, off // hpb + h))
    return _call(
        body, (proj, proj, proj, proj, lb, gn), name=name, grid=(heads // hpb, n),
        in_specs=[blk(0), blk(heads), blk(2 * heads), blk(3 * heads),
                  pl.BlockSpec((1, wide), lambda h, c: (0, h)), pl.BlockSpec((1, HEAD), lambda h, c: (0, 0))],
        out_specs=[pl.BlockSpec((CHUNK, wide), lambda h, c: (c, h)),
                   pl.BlockSpec((hpb, None, HEAD, HEAD), lambda h, c: (h, c, 0, 0))],
        out_shape=[jax.ShapeDtypeStruct((s, heads * HEAD), BF16), jax.ShapeDtypeStruct((heads, n, HEAD, HEAD), F32)],
        scratch_shapes=[pltpu.VMEM((hpb, HEAD, HEAD), F32)], comm=comm)


def _hgrn_bwd(proj, lb, gn, states, d_out, *, heads, name, comm=()):
    s = proj.shape[0]
    n = s // CHUNK
    hpb = min(HEADS_PER_STEP, heads)

    def body(q_ref, f_ref, i_ref, g_ref, lb_ref, gn_ref, st_ref, do_ref,
             dq_ref, df_ref, di_ref, dg_ref, dlb_ref, dgn_ref, dstate):
        h, c = pl.program_id(0), pl.program_id(1)

        @pl.when(c == 0)
        def _():
            dstate[...] = jnp.zeros_like(dstate)
            dlb_ref[...] = jnp.zeros_like(dlb_ref)

        @pl.when((c == 0) & (h == 0))
        def _():
            dgn_ref[...] = jnp.zeros_like(dgn_ref)

        gnv = gn_ref[...]
        loaded = [(q_ref[:, _lanes(j)], f_ref[:, _lanes(j)], i_ref[:, _lanes(j)], g_ref[:, _lanes(j)],
                   lb_ref[:, _lanes(j)], st_ref[j], do_ref[:, _lanes(j)].astype(F32), dstate[j]) for j in range(hpb)]
        qs, fs, ivs, gs, lbs, sts, dos, dss = (list(t) for t in zip(*loaded))
        _, vjp = jax.vjp(_hgrn_heads, qs, fs, ivs, gs, lbs, gnv, sts)
        dqs, dfs, dis, dgs, dlbs, dgn_sum, dsts = vjp((dos, dss))
        for j in range(hpb):
            ln = _lanes(j)
            dq_ref[:, ln] = dqs[j].astype(dq_ref.dtype)
            df_ref[:, ln] = dfs[j].astype(df_ref.dtype)
            di_ref[:, ln] = dis[j].astype(di_ref.dtype)
            dg_ref[:, ln] = dgs[j].astype(dg_ref.dtype)
            dlb_ref[:, ln] += dlbs[j]
            dstate[j] = dsts[j]
        dgn_ref[...] += dgn_sum

    wide = hpb * HEAD
    rev = lambda off: pl.BlockSpec((CHUNK, wide), lambda h, c: (n - 1 - c, off // hpb + h))
    out_blk = pl.BlockSpec((CHUNK, wide), lambda h, c: (n - 1 - c, h))
    width = heads * HEAD
    return _call(
        body, (proj, proj, proj, proj, lb, gn, states, d_out), name=name, grid=(heads // hpb, n),
        in_specs=[rev(0), rev(heads), rev(2 * heads), rev(3 * heads),
                  pl.BlockSpec((1, wide), lambda h, c: (0, h)), pl.BlockSpec((1, HEAD), lambda h, c: (0, 0)),
                  pl.BlockSpec((hpb, None, HEAD, HEAD), lambda h, c: (h, n - 1 - c, 0, 0)), out_blk],
        out_specs=[out_blk, out_blk, out_blk, out_blk,
                   pl.BlockSpec((1, wide), lambda h, c: (0, h)), pl.BlockSpec((1, HEAD), lambda h, c: (0, 0))],
        out_shape=[jax.ShapeDtypeStruct((s, width), BF16)] * 4
        + [jax.ShapeDtypeStruct((1, width), F32), jax.ShapeDtypeStruct((1, HEAD), F32)],
        scratch_shapes=[pltpu.VMEM((hpb, HEAD, HEAD), F32)], comm=comm)


def _shift_rows(x, d, rowi):
    if d == 0:
        return x
    n = x.shape[0]
    rolled = pltpu.roll(x, d % n, 0)
    keep = rowi >= d if d > 0 else rowi < n + d
    return jnp.where(keep, rolled, 0.0)


def _conv_pre(x, w_ref, rowi):
    acc = None
    for j in range(CONV_WIDTH):
        term = w_ref[j:j + 1, :] * _shift_rows(x, CONV_WIDTH - 1 - j, rowi)
        acc = term if acc is None else acc + term
    return acc


def _conv_fwd(proj, w, *, col_off, name, cb=256):
    s = proj.shape[0]
    width = w.shape[1]
    cb = min(cb, width)

    def body(x_ref, w_ref, o_ref):
        rowi = lax.broadcasted_iota(jnp.int32, (s, cb), 0)
        o_ref[...] = _silu(_conv_pre(x_ref[...], w_ref, rowi))

    return pl.pallas_call(
        body, name=name, grid=(width // cb,),
        in_specs=[pl.BlockSpec((s, cb), lambda j: (0, col_off // cb + j)), pl.BlockSpec((CONV_WIDTH, cb), lambda j: (0, j))],
        out_specs=pl.BlockSpec((s, cb), lambda j: (0, j)),
        out_shape=jax.ShapeDtypeStruct((s, width), F32),
        compiler_params=_params(1),
    )(proj, w)


def _conv_bwd(proj, w, d_out, *, col_off, name, cb=256, comm=()):
    s = proj.shape[0]
    width = w.shape[1]
    cb = min(cb, width)

    def body(x_ref, w_ref, do_ref, dx_ref, dw_ref):
        rowi = lax.broadcasted_iota(jnp.int32, (s, cb), 0)
        x = x_ref[...]
        pre = _conv_pre(x, w_ref, rowi)
        sg = jax.nn.sigmoid(pre)
        dpre = do_ref[...] * (sg + pre * sg * (1.0 - sg))
        dx = None
        for j in range(CONV_WIDTH):
            d = CONV_WIDTH - 1 - j
            term = w_ref[j:j + 1, :] * _shift_rows(dpre, -d, rowi)
            dx = term if dx is None else dx + term
            dw_ref[j:j + 1, :] = jnp.sum(dpre * _shift_rows(x, d, rowi), axis=0, keepdims=True)
        dx_ref[...] = dx.astype(dx_ref.dtype)

    return _call(
        body, (proj, w, d_out), name=name, grid=(width // cb,),
        in_specs=[pl.BlockSpec((s, cb), lambda j: (0, col_off // cb + j)), pl.BlockSpec((CONV_WIDTH, cb), lambda j: (0, j)),
                  pl.BlockSpec((s, cb), lambda j: (0, j))],
        out_specs=[pl.BlockSpec((s, cb), lambda j: (0, j)), pl.BlockSpec((CONV_WIDTH, cb), lambda j: (0, j))],
        out_shape=[jax.ShapeDtypeStruct((s, width), BF16), jax.ShapeDtypeStruct((CONV_WIDTH, width), F32)],
        comm=comm)


_lane_concat = _per_head(lambda a, b: jnp.concatenate([a, b], axis=1))
_lane_half = _per_head(lambda a, j: a[:, j * HEAD:(j + 1) * HEAD])


def _tri_inverse(lower):
    c = CHUNK
    row = lax.broadcasted_iota(jnp.int32, (c, c), 0)
    col = lax.broadcasted_iota(jnp.int32, (c, c), 1)
    inv = (row == col).astype(F32)
    lvl = 0
    while (1 << lvl) < c:
        same_pair = (row >> (lvl + 1)) == (col >> (lvl + 1))
        off_block = same_pair & (((row >> lvl) & 1) == 1) & (((col >> lvl) & 1) == 0)
        inv = inv - _hdot_h(_hdot_h(inv, _where(off_block, lower, 0.0), NN), inv, NN)
        lvl += 1
    return inv


@jax.custom_vjp
def _tri_solve(lowers, rhss):
    return _tri_solve_fwd(lowers, rhss)[0]


def _tri_solve_fwd(lowers, rhss):
    inv = _tri_inverse(_Heads(lowers))
    sol = _hdot_h(inv, _Heads(rhss), NN)
    return sol.vals, (inv.vals, sol.vals)


def _tri_solve_bwd(res, g):
    inv, sol = _Heads(res[0]), _Heads(res[1])
    d_rhs = _hdot_h(inv, _Heads(g), TN)
    return (-_hdot_h(d_rhs, sol, NT)).vals, d_rhs.vals


_tri_solve.defvjp(_tri_solve_fwd, _tri_solve_bwd)


def _solve(lower, rhs):
    if isinstance(lower, _Heads):
        return _Heads(_tri_solve(lower.vals, rhs.vals))
    return _tri_solve([lower], [rhs])[0]


def _delta_chunk(h, heads, qr, kr, vr, ab, zp, alog, dtb, gn, st):
    c = CHUNK
    row = lax.broadcasted_iota(jnp.int32, (c, c), 0)
    col = lax.broadcasted_iota(jnp.int32, (c, c), 1)
    causal = row >= col
    strict = row > col
    lane = lax.broadcasted_iota(jnp.int32, (c, HEAD), 1)
    mine = _equal(h, lane)
    la_full = -jnp.exp(alog) * _softplus(ab + dtb)
    cum_full = _hdot(causal.astype(F32), la_full)
    cum = _sum(_where(mine, cum_full, 0.0), axis=1, keepdims=True)
    cend = _sum(_sum(_where(mine, la_full, 0.0), axis=1, keepdims=True), axis=0, keepdims=True)
    beta = _sum(_where(_equal(heads + h, lane), jax.nn.sigmoid(ab), 0.0), axis=1, keepdims=True)
    cum_row = _hdot_h(_where(mine, 1.0, 0.0), cum_full, NT)
    decay = _where(causal, _exp(_where(causal, cum - cum_row, 0.0)), 0.0)
    qn = qr * _rsqrt(_sum(qr * qr, axis=-1, keepdims=True) + NORM_EPS) * (HEAD ** -0.5)
    kn = kr * _rsqrt(_sum(kr * kr, axis=-1, keepdims=True) + NORM_EPS)
    kb = kn * beta
    lower = _where(strict, _bdot_h(kb, kn, NT) * decay, 0.0)
    ecum = _exp(cum)
    sol = _solve(lower, _lane_concat(vr * beta, kb * ecum))
    u, w = _lane_half(sol, 0), _lane_half(sol, 1)
    intra = _bdot_h(qn, kn, NT) * decay
    v_new = u - _bdot_h(w, st, NN)
    out = _bdot_h(qn * ecum, st, NN) + _bdot_h(intra, v_new, NN)
    st_new = st * _exp(cend) + _bdot_h(kn * _exp(cend - cum), v_new, TN)
    res = _rms(out, gn) * _silu(zp)
    return res, st_new


def _delta_heads(hs, heads, qs, ks, vs, ab, zs, alog, dtb, gn, sts):
    res, st_new = _delta_chunk(_Heads(hs), heads, _Heads(qs), _Heads(ks), _Heads(vs), ab, _Heads(zs), alog, dtb, gn,
                               _Heads(sts))
    return res.vals, st_new.vals


def _delta_fwd(qkv, ab, proj, hp, gn, *, heads, z_off, name, comm=()):
    s = qkv.shape[0]
    n = s // CHUNK

    hpb = min(HEADS_PER_STEP, heads)
    assert heads % hpb == 0 and z_off % hpb == 0

    def body(q_ref, k_ref, v_ref, ab_ref, z_ref, hp_ref, gn_ref, o_ref, st_ref, state):
        hb = pl.program_id(1)

        @pl.when(pl.program_id(0) == 0)
        def _():
            for j in range(hpb):
                state[hb * hpb + j] = jnp.zeros((HEAD, HEAD), F32)

        shared = (ab_ref[...], hp_ref[0:1, :], hp_ref[1:2, :], gn_ref[...])
        loaded = [(q_ref[:, _lanes(j)], k_ref[:, _lanes(j)], v_ref[:, _lanes(j)], z_ref[:, _lanes(j)],
                   state[hb * hpb + j]) for j in range(hpb)]
        qs, ks, vs, zs, sts = (list(t) for t in zip(*loaded))
        res, st_new = _delta_heads([hb * hpb + j for j in range(hpb)], heads, qs, ks, vs, shared[0], zs, shared[1],
                                   shared[2], shared[3], sts)
        for j in range(hpb):
            st_ref[j] = sts[j]
            o_ref[:, _lanes(j)] = res[j].astype(o_ref.dtype)
            state[hb * hpb + j] = st_new[j]

    wide = hpb * HEAD
    blk = lambda off: pl.BlockSpec((CHUNK, wide), lambda c, h: (c, off // hpb + h))
    return _call(
        body, (qkv, qkv, qkv, ab, proj, hp, gn), name=name, grid=(n, heads // hpb),
        in_specs=[blk(0), blk(heads), blk(2 * heads), pl.BlockSpec((CHUNK, HEAD), lambda c, h: (c, 0)), blk(z_off),
                  pl.BlockSpec((8, HEAD), lambda c, h: (0, 0)), pl.BlockSpec((1, HEAD), lambda c, h: (0, 0))],
        out_specs=[pl.BlockSpec((CHUNK, wide), lambda c, h: (c, h)),
                   pl.BlockSpec((hpb, None, HEAD, HEAD), lambda c, h: (h, c, 0, 0))],
        out_shape=[jax.ShapeDtypeStruct((s, heads * HEAD), BF16), jax.ShapeDtypeStruct((heads, n, HEAD, HEAD), F32)],
        scratch_shapes=[pltpu.VMEM((heads, HEAD, HEAD), F32)], comm=comm)


def _delta_bwd(qkv, ab, proj, hp, gn, states, d_out, *, heads, z_off, name, comm=()):
    s = qkv.shape[0]
    n = s // CHUNK
    hpb = min(HEADS_PER_STEP, heads)

    def body(q_ref, k_ref, v_ref, ab_ref, z_ref, hp_ref, gn_ref, st_ref, do_ref,
             dq_ref, dk_ref, dv_ref, dab_ref, dz_ref, dhp_ref, dgn_ref, dstate):
        c, hb = pl.program_id(0), pl.program_id(1)

        @pl.when(c == 0)
        def _():
            for j in range(hpb):
                dstate[hb * hpb + j] = jnp.zeros((HEAD, HEAD), F32)

        @pl.when((c == 0) & (hb == 0))
        def _():
            dgn_ref[...] = jnp.zeros_like(dgn_ref)
            dhp_ref[...] = jnp.zeros_like(dhp_ref)

        @pl.when(hb == 0)
        def _():
            dab_ref[...] = jnp.zeros_like(dab_ref)

        shared = (ab_ref[...], hp_ref[0:1, :], hp_ref[1:2, :], gn_ref[...])
        loaded = [(q_ref[:, _lanes(j)], k_ref[:, _lanes(j)], v_ref[:, _lanes(j)], z_ref[:, _lanes(j)], st_ref[j],
                   do_ref[:, _lanes(j)].astype(F32), dstate[hb * hpb + j]) for j in range(hpb)]
        qs, ks, vs, zs, sts, dos, dss = (list(t) for t in zip(*loaded))
        fn = functools.partial(_delta_heads, [hb * hpb + j for j in range(hpb)], heads)
        _, vjp = jax.vjp(fn, qs, ks, vs, shared[0], zs, shared[1], shared[2], shared[3], sts)
        dqs, dks, dvs, dab, dzs, dal, ddt, dgn, dsts = vjp((dos, dss))
        for j in range(hpb):
            ln = _lanes(j)
            dq_ref[:, ln] = dqs[j]
            dk_ref[:, ln] = dks[j]
            dv_ref[:, ln] = dvs[j]
            dz_ref[:, ln] = dzs[j].astype(dz_ref.dtype)
            dstate[hb * hpb + j] = dsts[j]
        dab_ref[...] += dab
        dhp_ref[0:1, :] += dal
        dhp_ref[1:2, :] += ddt
        dgn_ref[...] += dgn

    wide = hpb * HEAD
    rev = lambda off: pl.BlockSpec((CHUNK, wide), lambda c, h: (n - 1 - c, off // hpb + h))
    width = heads * HEAD
    head_blk = pl.BlockSpec((CHUNK, wide), lambda c, h: (n - 1 - c, h))
    ab_blk = pl.BlockSpec((CHUNK, HEAD), lambda c, h: (n - 1 - c, 0))
    return _call(
        body, (qkv, qkv, qkv, ab, proj, hp, gn, states, d_out), name=name, grid=(n, heads // hpb),
        in_specs=[rev(0), rev(heads), rev(2 * heads), ab_blk, rev(z_off),
                  pl.BlockSpec((8, HEAD), lambda c, h: (0, 0)), pl.BlockSpec((1, HEAD), lambda c, h: (0, 0)),
                  pl.BlockSpec((hpb, None, HEAD, HEAD), lambda c, h: (h, n - 1 - c, 0, 0)), head_blk],
        out_specs=[head_blk, head_blk, head_blk, ab_blk, head_blk,
                   pl.BlockSpec((8, HEAD), lambda c, h: (0, 0)), pl.BlockSpec((1, HEAD), lambda c, h: (0, 0))],
        out_shape=[jax.ShapeDtypeStruct((s, width), F32)] * 3
        + [jax.ShapeDtypeStruct((s, HEAD), F32), jax.ShapeDtypeStruct((s, width), BF16),
           jax.ShapeDtypeStruct((8, HEAD), F32), jax.ShapeDtypeStruct((1, HEAD), F32)],
        scratch_shapes=[pltpu.VMEM((heads, HEAD, HEAD), F32)], comm=comm)


def _s5_scan(buf, lt_ref, cin_r, cin_i, tt, reverse):
    nblk = tt // 8
    hl = S5_HALF
    base = 8 if reverse else 0

    def body(j, carry):
        cr, ci = carry
        off = pl.multiple_of((nblk - 1 - j if reverse else j) * 8, 8)
        xr = buf[pl.ds(off, 8), 0:hl]
        xi = buf[pl.ds(off, 8), hl:2 * hl]
        for lv, d in enumerate((1, 2, 4)):
            ar, ai = lt_ref[base + 2 * lv], lt_ref[base + 2 * lv + 1]
            sr = pltpu.roll(xr, 8 - d if reverse else d, 0)
            si = pltpu.roll(xi, 8 - d if reverse else d, 0)
            xr, xi = xr + ar * sr - ai * si, xi + ar * si + ai * sr
        pr, pi = lt_ref[base + 6], lt_ref[base + 7]
        xr, xi = xr + pr * cr - pi * ci, xi + pr * ci + pi * cr
        buf[pl.ds(off, 8), 0:hl] = xr
        buf[pl.ds(off, 8), hl:2 * hl] = xi
        edge = 0 if reverse else 7
        return xr[edge:edge + 1, :], xi[edge:edge + 1, :]

    return lax.fori_loop(0, nblk, body, (cin_r, cin_i))


def _s5_fwd(u, wb, wc, lt, dskip, *, name, tt=512):
    s, d = u.shape
    nb = d // HEAD
    tt = min(tt, s)
    nt = s // tt
    hl = S5_HALF

    def body(u_ref, wb_ref, wc_ref, lt_ref, d_ref, y_ref, cin_ref, st_ref, buf, carry):
        @pl.when(pl.program_id(1) == 0)
        def _():
            carry[...] = jnp.zeros_like(carry)

        cin_ref[...] = carry[0:1, :]
        uv = u_ref[...]
        buf[...] = _bdot_raw(uv, wb_ref[...])
        cr, ci = _s5_scan(buf, lt_ref, carry[0:1, 0:hl], carry[0:1, hl:2 * hl], tt, False)
        carry[0:1, 0:hl] = cr
        carry[0:1, hl:2 * hl] = ci
        states = buf[...].astype(BF16)
        st_ref[...] = states
        y_ref[...] = _bdot_raw(states, wc_ref[...]) + d_ref[...] * uv

    return pl.pallas_call(
        body, name=name, grid=(nb, nt),
        in_specs=[pl.BlockSpec((tt, HEAD), lambda b, t: (t, b)),
                  pl.BlockSpec((None, HEAD, 2 * hl), lambda b, t: (b, 0, 0)),
                  pl.BlockSpec((None, 2 * hl, HEAD), lambda b, t: (b, 0, 0)),
                  pl.BlockSpec((None, 16, 8, hl), lambda b, t: (b, 0, 0, 0)),
                  pl.BlockSpec((1, HEAD), lambda b, t: (0, b))],
        out_specs=[pl.BlockSpec((tt, HEAD), lambda b, t: (t, b)),
                   pl.BlockSpec((None, None, 1, 2 * hl), lambda b, t: (b, t, 0, 0)),
                   pl.BlockSpec((tt, 2 * hl), lambda b, t: (t, b))],
        out_shape=[jax.ShapeDtypeStruct((s, d), F32), jax.ShapeDtypeStruct((nb, nt, 1, 2 * hl), F32),
                   jax.ShapeDtypeStruct((s, nb * 2 * hl), BF16)],
        scratch_shapes=[pltpu.VMEM((tt, 2 * hl), F32), pltpu.VMEM((8, 2 * hl), F32)],
        compiler_params=_params(2),
    )(u, wb, wc, lt, dskip)


def _s5_bwd(u, dy, wb, wc, lt, dskip, cins, states, *, name, tt=512, comm=()):
    s, d = u.shape
    nb = d // HEAD
    tt = min(tt, s)
    nt = s // tt
    hl = S5_HALF

    def body(u_ref, dy_ref, wb_ref, wc_ref, lt_ref, d_ref, cin_ref, st_ref,
             du_ref, dwb_ref, dwc_ref, dd_ref, dlam_ref, abuf, acarry):
        @pl.when(pl.program_id(1) == 0)
        def _():
            acarry[...] = jnp.zeros_like(acarry)
            dwb_ref[...] = jnp.zeros_like(dwb_ref)
            dwc_ref[...] = jnp.zeros_like(dwc_ref)
            dd_ref[...] = jnp.zeros_like(dd_ref)
            dlam_ref[...] = jnp.zeros_like(dlam_ref)

        uv, dyv = u_ref[...], dy_ref[...]
        abuf[...] = _bdot_raw(dyv, wc_ref[...], NT)
        ar, ai = _s5_scan(abuf, lt_ref, acarry[0:1, 0:hl], acarry[0:1, hl:2 * hl], tt, True)
        acarry[0:1, 0:hl] = ar
        acarry[0:1, hl:2 * hl] = ai
        du_ref[...] = _bdot_raw(abuf[...], wb_ref[...], NT) + d_ref[...] * dyv
        dwb_ref[...] += _bdot_raw(uv, abuf[...], TN)
        dwc_ref[...] += _bdot_raw(st_ref[...], dyv, TN)
        dd_ref[...] += jnp.sum(dyv * uv, axis=0, keepdims=True)
        first = lax.broadcasted_iota(jnp.int32, (tt, hl), 0) == 0
        spr = jnp.where(first, cin_ref[:, 0:hl], pltpu.roll(st_ref[:, 0:hl].astype(F32), 1, 0))
        spi = jnp.where(first, cin_ref[:, hl:2 * hl], pltpu.roll(st_ref[:, hl:2 * hl].astype(F32), 1, 0))
        avr, avi = abuf[:, 0:hl], abuf[:, hl:2 * hl]
        dlam_ref[:, 0:hl] += jnp.sum(avr * spr + avi * spi, axis=0, keepdims=True)
        dlam_ref[:, hl:2 * hl] += jnp.sum(avi * spr - avr * spi, axis=0, keepdims=True)

    rev = pl.BlockSpec((tt, HEAD), lambda b, t: (nt - 1 - t, b))
    return _call(
        body, (u, dy, wb, wc, lt, dskip, cins, states), name=name, grid=(nb, nt),
        in_specs=[rev, rev,
                  pl.BlockSpec((None, HEAD, 2 * hl), lambda b, t: (b, 0, 0)),
                  pl.BlockSpec((None, 2 * hl, HEAD), lambda b, t: (b, 0, 0)),
                  pl.BlockSpec((None, 16, 8, hl), lambda b, t: (b, 0, 0, 0)),
                  pl.BlockSpec((1, HEAD), lambda b, t: (0, b)),
                  pl.BlockSpec((None, None, 1, 2 * hl), lambda b, t: (b, nt - 1 - t, 0, 0)),
                  pl.BlockSpec((tt, 2 * hl), lambda b, t: (nt - 1 - t, b))],
        out_specs=[rev,
                   pl.BlockSpec((None, HEAD, 2 * hl), lambda b, t: (b, 0, 0)),
                   pl.BlockSpec((None, 2 * hl, HEAD), lambda b, t: (b, 0, 0)),
                   pl.BlockSpec((1, HEAD), lambda b, t: (0, b)),
                   pl.BlockSpec((None, 1, 2 * hl), lambda b, t: (b, 0, 0))],
        out_shape=[jax.ShapeDtypeStruct((s, d), F32), jax.ShapeDtypeStruct(wb.shape, F32),
                   jax.ShapeDtypeStruct(wc.shape, F32), jax.ShapeDtypeStruct((1, d), F32),
                   jax.ShapeDtypeStruct((nb, 1, 2 * hl), F32)],
        scratch_shapes=[pltpu.VMEM((tt, 2 * hl), F32), pltpu.VMEM((8, 2 * hl), F32)],
        comm=comm)


def _s5_pack(lr, li, br, bi, c_re, c_im):
    g = lr.shape[0]
    nb = g // S5_GB
    eye = jnp.eye(S5_GB, dtype=F32)
    bm = jnp.stack([br, bi]).reshape(2, nb, S5_GB, S5_STATE, S5_GROUP)
    wb = jnp.einsum("rbgpc,gh->bgcrhp", bm, eye).reshape(nb, HEAD, 2 * S5_HALF)
    cm = jnp.stack([c_re, -c_im]).reshape(2, nb, S5_GB, S5_GROUP, S5_STATE)
    wc = jnp.einsum("rbgcp,gh->brgphc", cm, eye).reshape(nb, 2 * S5_HALF, HEAD)
    pw = [(lr, li)]
    for _ in range(7):
        pr, pi = pw[-1]
        pw.append((pr * lr - pi * li, pr * li + pi * lr))
    blk = lambda a: a.reshape(nb, 1, S5_HALF)
    rows = jnp.arange(8).reshape(1, 8, 1)
    tables = []
    for conj, keep, order in ((1.0, lambda n: rows >= n, range(8)), (-1.0, lambda n: rows < 8 - n, range(7, -1, -1))):
        for n in (1, 2, 4):
            tables += [jnp.where(keep(n), blk(pw[n - 1][0]), 0.0), jnp.where(keep(n), conj * blk(pw[n - 1][1]), 0.0)]
        tables += [jnp.concatenate([blk(pw[n][0]) for n in order], axis=1),
                   jnp.concatenate([conj * blk(pw[n][1]) for n in order], axis=1)]
    return wb, wc, jnp.stack(tables, axis=1)


def _s5_unpack(dwb, dwc, dlam):
    nb = dwb.shape[0]
    g = nb * S5_GB
    eye = jnp.eye(S5_GB, dtype=F32)
    db = jnp.einsum("bgcrhp,gh->rbgpc", dwb.reshape(nb, S5_GB, S5_GROUP, 2, S5_GB, S5_STATE), eye)
    db = db.reshape(2, g, S5_STATE * S5_GROUP)
    dc = jnp.einsum("brgphc,gh->rbgcp", dwc.reshape(nb, 2, S5_GB, S5_STATE, S5_GB, S5_GROUP), eye)
    dc = dc.reshape(2, g, S5_GROUP, S5_STATE)
    dl = dlam.reshape(nb, 2, S5_GB, S5_STATE).transpose(1, 0, 2, 3).reshape(2, g, S5_STATE)
    return dl[0], dl[1], db[0], db[1], dc[0], -dc[1]


def _peer(r):
    mx, my, mc = lax.axis_index("x"), lax.axis_index("y"), lax.axis_index("c")
    px = 1 - mx if r & 4 else mx
    py = 1 - my if r & 2 else my
    pc = 1 - mc if r & 1 else mc
    return (px, py, pc), 4 * px + 2 * py + pc


_COMM_SCRATCH = [pltpu.SemaphoreType.DMA((N_DEV - 1,)), pltpu.SemaphoreType.DMA((N_DEV - 1,)), pltpu.SemaphoreType.DMA]


class _AllToAll:
    def __init__(self, x):
        self.x = x
        self.out_shape = jax.ShapeDtypeStruct(x.shape, x.dtype)

    def _copies(self, x_ref, out_ref, send_sems, recv_sems, local_sem):
        _, me = _peer(0)
        mine = pltpu.make_async_copy(x_ref.at[me], out_ref.at[me], local_sem)
        sends, recvs = [], []
        for r in range(1, N_DEV):
            pos, idx = _peer(r)
            sems = dict(send_sem=send_sems.at[r - 1], recv_sem=recv_sems.at[r - 1], device_id=pos, device_id_type=MESH)
            sends.append(pltpu.make_async_remote_copy(src_ref=x_ref.at[idx], dst_ref=out_ref.at[me], **sems))
            recvs.append(pltpu.make_async_remote_copy(src_ref=x_ref.at[idx], dst_ref=out_ref.at[idx], **sems))
        return mine, sends, recvs

    def start(self, *refs):
        mine, sends, _ = self._copies(*refs)
        mine.start()
        for cp in sends:
            cp.start()

    def finish(self, *refs):
        mine, sends, recvs = self._copies(*refs)
        for cp in recvs:
            cp.wait_recv()
        for cp in sends:
            cp.wait_send()
        mine.wait()


class _Gather:
    def __init__(self, x):
        self.x = x
        self.out_shape = jax.ShapeDtypeStruct((N_DEV,) + tuple(x.shape), x.dtype)

    def _copies(self, x_ref, out_ref, send_sems, recv_sems, local_sem):
        mx, my, mc = lax.axis_index("x"), lax.axis_index("y"), lax.axis_index("c")
        me, sibling = (mx, my, mc), (mx, my, 1 - mc)
        chips = [(1 - mx, my), (mx, 1 - my), (1 - mx, 1 - my)]

        def slot(px, py, pc):
            return out_ref.at[4 * px + 2 * py + pc]

        def copy(k, block, to, src=None):
            return pltpu.make_async_remote_copy(
                src_ref=slot(*block) if src is None else src, dst_ref=slot(*block),
                send_sem=send_sems.at[k], recv_sem=recv_sems.at[k], device_id=to, device_id_type=MESH)

        return dict(
            mine=pltpu.make_async_copy(x_ref, slot(*me), local_sem),
            first=[copy(0, me, sibling, src=x_ref)] + [copy(1 + j, me, (*chip, mc), src=x_ref) for j, chip in enumerate(chips)],
            passed=[copy(4 + j, (*chip, mc), sibling) for j, chip in enumerate(chips)],
            over_ici=[copy(1 + j, (*chip, mc), me) for j, chip in enumerate(chips)],
            from_sibling=[copy(0, sibling, me)] + [copy(4 + j, (*chip, 1 - mc), me) for j, chip in enumerate(chips)])

    def start(self, *refs):
        cps = self._copies(*refs)
        cps["mine"].start()
        for cp in cps["first"]:
            cp.start()

    def finish(self, *refs):
        cps = self._copies(*refs)
        for arrived, onward in zip(cps["over_ici"], cps["passed"]):
            arrived.wait_recv()
            onward.start()
        for cp in cps["from_sibling"]:
            cp.wait_recv()
        for cp in cps["first"] + cps["passed"]:
            cp.wait_send()
        cps["mine"].wait()


def _call(body, args, *, name, grid, in_specs, out_specs, out_shape, scratch_shapes=(), comm=()):
    n_in, n_out, n_scr, nc = len(in_specs), len(out_shape), len(scratch_shapes), len(comm)

    def wrapped(*refs):
        ins, c_in = refs[:n_in], refs[n_in:n_in + nc]
        outs, c_out = refs[n_in + nc:n_in + nc + n_out], refs[n_in + nc + n_out:n_in + 2 * nc + n_out]
        scr = refs[n_in + 2 * nc + n_out:n_in + 2 * nc + n_out + n_scr]
        sems = refs[n_in + 2 * nc + n_out + n_scr:]
        ids = [pl.program_id(a) for a in range(len(grid))]
        if nc:
            @pl.when(functools.reduce(operator.and_, [i == 0 for i in ids]))
            def _():
                for k, op in enumerate(comm):
                    op.start(c_in[k], c_out[k], *sems[3 * k:3 * k + 3])

        body(*ins, *outs, *scr)
        if nc:
            @pl.when(functools.reduce(operator.and_, [i == g - 1 for i, g in zip(ids, grid)]))
            def _():
                for k, op in enumerate(comm):
                    op.finish(c_in[k], c_out[k], *sems[3 * k:3 * k + 3])

    any_spec = pl.BlockSpec(memory_space=pl.ANY)
    res = pl.pallas_call(
        wrapped, name=name, grid=grid,
        in_specs=list(in_specs) + [any_spec] * nc, out_specs=list(out_specs) + [any_spec] * nc,
        out_shape=list(out_shape) + [op.out_shape for op in comm],
        scratch_shapes=list(scratch_shapes) + list(_COMM_SCRATCH) * nc,
        compiler_params=_params(len(grid)),
    )(*args, *[op.x for op in comm])
    return list(res[:n_out]), list(res[n_out:])


def _comm_call(op, *, name):
    return _call(lambda: None, (), name=name, grid=(1,), in_specs=[], out_specs=[], out_shape=[], comm=(op,))[1][0]


def _adamw(w, parts, m, v, *, name, tr=128):
    nl, r, c = w.shape
    assert len(parts) == nl
    npart = parts[0].shape[0]
    tr = min(tr, r)
    assert r % tr == 0, (name, r)

    def body(w_ref, m_ref, v_ref, *rest):
        p_refs, (g_ref, d_ref, mo_ref, vo_ref) = rest[:nl], rest[nl:]
        for l in range(nl):
            @pl.when(pl.program_id(0) == l)
            def _():
                g = p_refs[l][0].astype(F32)
                for k in range(1, npart):
                    g = g + p_refs[l][k].astype(F32)
                m2 = ADAM_B1 * m_ref[...] + (1.0 - ADAM_B1) * g
                v2 = ADAM_B2 * v_ref[...] + (1.0 - ADAM_B2) * (g * g)
                m_hat = m2 / (1.0 - ADAM_B1 ** ADAM_STEP)
                v_hat = v2 / (1.0 - ADAM_B2 ** ADAM_STEP)
                g_ref[...] = g
                d_ref[...] = -ADAM_LR * (m_hat / (jnp.sqrt(v_hat) + ADAM_EPS) + ADAM_WD * w_ref[...])
                mo_ref[...] = m2
                vo_ref[...] = v2

    blk = pl.BlockSpec((None, tr, c), lambda l, i: (l, i, 0))
    part_spec = lambda k: pl.BlockSpec((npart, tr, c), lambda l, i: (0, jnp.where(l == k, i, 0), 0))
    return pl.pallas_call(
        body, name=name, grid=(nl, r // tr),
        in_specs=[blk, blk, blk] + [part_spec(k) for k in range(nl)],
        out_specs=[blk] * 4, out_shape=[jax.ShapeDtypeStruct((nl, r, c), F32)] * 4,
        compiler_params=_params(2),
    )(w, m, v, *parts)


def _sum_parts(parts, *, name):
    npart = parts.shape[0]

    def body(p_ref, o_ref):
        g = p_ref[0]
        for k in range(1, npart):
            g = g + p_ref[k]
        o_ref[...] = g

    return pl.pallas_call(
        body, name=name, grid=(1,), in_specs=[_full_spec(parts)],
        out_specs=pl.BlockSpec(parts.shape[1:], lambda i: (0, 0)),
        out_shape=jax.ShapeDtypeStruct(parts.shape[1:], F32), compiler_params=_params(1),
    )(parts)


def _pack(arrs):
    flat = jnp.concatenate([a.reshape(-1).astype(F32) for a in arrs])
    pad = (-flat.shape[0]) % (HEAD * HEAD)
    return jnp.pad(flat, (0, pad)).reshape(-1, HEAD)


def _unpack(packed, shapes):
    flat = packed.reshape(-1)
    out, off = [], 0
    for shp in shapes:
        size = math.prod(shp)
        out.append(flat[off:off + size].reshape(shp))
        off += size
    return out


def _add_epilogue(acc, res):
    return (acc + res,)


def _relu2_epilogue(acc):
    r = jnp.maximum(acc, 0.0)
    return acc, r * r


def _ple_epilogue(acc, gpre, h):
    return h + jax.nn.sigmoid(gpre) * acc, acc


def kernel(x, p, norm_mix, norm_mlp, norm_ple, w_in_e, w_out_e, hgrn_lb, g_norm_a, conv_w, a_log, dt_bias, g_norm_b, s5_a_re, s5_a_im, s5_b_re, s5_b_im, s5_c_re, s5_c_im, s5_d, s5_log_dt, w_glu, b_glu, w_out_o, w_up, w_down, w_ple_gate, w_ple_proj, final_norm, loss_target, m_norm_mix, m_norm_mlp, m_norm_ple, m_w_in_e, m_w_out_e, m_hgrn_lb, m_g_norm_a, m_conv_w, m_a_log, m_dt_bias, m_g_norm_b, m_s5_a_re, m_s5_a_im, m_s5_b_re, m_s5_b_im, m_s5_c_re, m_s5_c_im, m_s5_d, m_s5_log_dt, m_w_glu, m_b_glu, m_w_out_o, m_w_up, m_w_down, m_w_ple_gate, m_w_ple_proj, m_final_norm, v_norm_mix, v_norm_mlp, v_norm_ple, v_w_in_e, v_w_out_e, v_hgrn_lb, v_g_norm_a, v_conv_w, v_a_log, v_dt_bias, v_g_norm_b, v_s5_a_re, v_s5_a_im, v_s5_b_re, v_s5_b_im, v_s5_c_re, v_s5_c_im, v_s5_d, v_s5_log_dt, v_w_glu, v_b_glu, v_w_out_o, v_w_up, v_w_down, v_w_ple_gate, v_w_ple_proj, v_final_norm):
    args = dict(locals())
    s, d = x.shape[1], x.shape[2]
    aw = d // 2
    ha = hb = aw // HEAD
    main = 4 * d
    z_col = 2 * d + 3 * aw
    ff = w_up.shape[2] * N_DEV
    ple = p.shape[-1]
    groups = d // S5_GROUP
    me = 4 * lax.axis_index("x") + 2 * lax.axis_index("y") + lax.axis_index("c")
    x2, target = x[0], loss_target[0]
    row = lambda a, i: a[i:i + 1]

    def gather_of(w):
        return _Gather(w.astype(BF16))

    w_in = jnp.transpose(_comm_call(gather_of(w_in_e[0]), name="ag_w_in"), (1, 0, 2)).reshape(d, -1)
    w_main = w_in[:, :main]
    w_tail = jnp.pad(w_in[:, main:], ((0, 0), (0, HEAD - 2 * hb)))

    lb_rows = [row(hgrn_lb, 0), row(hgrn_lb, 1), row(hgrn_lb, 2)]
    (lb0,) = _small_call(_lb0_stage, lb_rows, name="f_lb0")
    hp = jnp.zeros((8, HEAD), F32).at[0, :hb].set(a_log[0]).at[1, :hb].set(dt_bias[0])
    expand = jnp.asarray(np.kron(np.eye(S5_STATE, dtype=np.float32), np.ones((1, S5_GROUP), np.float32)))
    prep_in = [s5_a_re[0], s5_a_im[0], s5_log_dt[0].reshape(groups, 1),
               s5_b_re[0].reshape(groups, -1), s5_b_im[0].reshape(groups, -1), expand]
    lr, li, br, bi = _small_call(_s5_prep_stage, prep_in, name="f_s5_prep")
    wb, wc, lt = _s5_pack(lr, li, br, bi, s5_c_re[0], s5_c_im[0])
    fnorm = final_norm.reshape(1, d)

    def block_fwd(h, l, w_dn):
        hn = _rows_call(_rms_stage, [h], [row(norm_mlp, l)], [BF16], name=f"f_norm_mlp{l}")
        up_args = dict(epilogue=_relu2_epilogue, out_dtypes=(F32, BF16), name=f"f_up{l}")
        next_dn = None
        if w_dn is None:
            (up, act), (dn0,) = _mm(hn, w_upg[l], comm=(gather_of(w_down[0]),), **up_args)
            w_dn = dn0.reshape(ff, d)
            h2, (dn1,) = _mm(act, w_dn, extras=(h,), epilogue=_add_epilogue, name=f"f_down{l}",
                             comm=(gather_of(w_down[1]),))
            next_dn = dn1.reshape(ff, d)
        else:
            up, act = _mm(hn, w_upg[l], **up_args)
            h2 = _mm(act, w_dn, extras=(h,), epilogue=_add_epilogue, name=f"f_down{l}")
        hq = _rows_call(_rms_stage, [h2], [row(norm_ple, l)], [BF16], name=f"f_norm_ple{l}")
        gpre = _mm(hq, w_pgg[l], name=f"f_ple_gate{l}")
        h3, pp = _mm(p[l, 0], w_ppg[l], extras=(gpre, h2), epilogue=_ple_epilogue, out_dtypes=(F32, F32),
                     name=f"f_ple_proj{l}")
        return h3, dict(h=h, hn=hn, up=up, act=act, h2=h2, hq=hq, gpre=gpre, pp=pp, w_dn=w_dn), next_dn

    hn0 = _rows_call(_rms_stage, [x2], [row(norm_mix, 0)], [BF16], name="f_norm_mix0")
    shard_shapes = [conv_w[0].shape, s5_d.shape, b_glu.shape]
    proj, (oe8, pg8, small) = _mm(hn0, w_main, name="f_proj", comm=(
        gather_of(w_out_e[0]), gather_of(w_ple_gate), _Gather(_pack([conv_w[0], s5_d, b_glu]))))
    w_oe = oe8.reshape(d, d)
    w_top, w_bot = w_oe[:aw], w_oe[aw:]
    w_pgg = jnp.transpose(pg8, (1, 0, 2, 3)).reshape(2, d, d)
    conv_g, s5d_g, bglu_g = zip(*[_unpack(small[j], shard_shapes) for j in range(N_DEV)])
    conv_full = jnp.concatenate(conv_g, axis=1)
    s5d_full = jnp.concatenate(s5d_g, axis=1)
    bglu_full = jnp.concatenate(bglu_g, axis=1)
    ab = _mm(hn0, w_tail, name="f_ab")
    (oa, st_a), (gl8, oo8) = _hgrn_fwd(proj, lb0, g_norm_a, heads=ha, name="f_hgrn",
                                       comm=(gather_of(w_glu[0]), gather_of(w_out_o[0])))
    w_gl, w_oo = gl8.reshape(d, d), oo8.reshape(d, d)
    qkv = _conv_fwd(proj, conv_full, col_off=2 * d, name="f_conv")
    (ob, st_b), (up8,) = _delta_fwd(qkv, ab, proj, hp, g_norm_b, heads=hb, z_off=z_col // HEAD, name="f_delta",
                                    comm=(gather_of(w_up),))
    w_upg = jnp.transpose(up8, (1, 2, 0, 3)).reshape(2, d, ff)
    h1, (pp8,) = _mm(oa, w_top, extras=(x2,), epilogue=_add_epilogue, name="f_out_a", comm=(gather_of(w_ple_proj),))
    w_ppg = jnp.transpose(pp8, (1, 2, 0, 3)).reshape(2, ple, d)
    h1 = _mm(ob, w_bot, extras=(h1,), epilogue=_add_epilogue, name="f_out_b")
    h3, sv0, w_dn1 = block_fwd(h1, 0, None)

    u = _rows_call(_rms_stage, [h3], [row(norm_mix, 1)], [F32], name="f_norm_mix1")
    y, cins, s5_states = _s5_fwd(u, wb, wc, lt, s5d_full, name="f_s5")
    act_g = _rows_call(_gelu_stage, [y], [], [BF16], name="f_gelu")
    gl_raw = _mm(act_g, w_gl, name="f_glu")
    glu = _rows_call(_glu_stage, [y, gl_raw], [bglu_full], [BF16], name="f_glu_gate")
    h4 = _mm(glu, w_oo, extras=(h3,), epilogue=_add_epilogue, name="f_out_o")
    h6, sv1, _ = block_fwd(h4, 1, w_dn1)
    dh, d_fnorm, loss8 = _loss_call(h6, fnorm, target, name="loss")
    loss = lax.psum(loss8[0, 0], ("x", "y", "c"))

    dshard, ffs, cols = d // N_DEV, ff // N_DEV, w_in_e.shape[2]
    rows8 = lambda g: _AllToAll(g.reshape(N_DEV, -1, g.shape[-1]))
    cols8 = lambda g: _AllToAll(jnp.transpose(g.reshape(g.shape[0], N_DEV, -1), (1, 0, 2)))

    def halves(g8):
        r = g8.shape[1] // 2
        return _AllToAll(g8[:, :r]), _AllToAll(g8[:, r:])

    def joined(first, second):
        return jnp.concatenate([first, second], axis=1)

    def block_bwd(dh3, l, sv, carried=(), carried_wup=(), carried_up=()):
        (dgpre, dpp), _ = _rows_vjp(_ple_stage, [sv["h2"], sv["gpre"], sv["pp"]], [], [dh3],
                                    row_grads={1: BF16, 2: BF16}, name=f"b_ple{l}")
        g_pp = _mm(p[l, 0], dpp, ta=True, out_dtypes=(BF16,), name=f"b_w_ple_proj{l}")
        g_pg = _mm(sv["hq"], dgpre, ta=True, out_dtypes=(BF16,), name=f"b_w_ple_gate{l}")
        dhq = _mm(dgpre, w_pgg[l], tb=True, name=f"b_ple_gate{l}")
        (dh2,), (g_nple,) = _rows_vjp(_rms_stage, [sv["h2"]], [row(norm_ple, l)], [dhq], row_grads={0: F32},
                                      adds={0: dh3}, name=f"b_norm_ple{l}")
        dup, (r_pg, r_pp) = _mm(dh2, sv["w_dn"], tb=True, extras=(sv["up"],), epilogue=_relu2_grad_epilogue,
                                out_dtypes=(BF16,), name=f"b_down{l}", comm=(rows8(g_pg), cols8(g_pp)))
        g_dn = _mm(sv["act"], dh2, ta=True, out_dtypes=(BF16,), name=f"b_w_down{l}", comm=carried)
        g_dn, r_carried = g_dn if carried else (g_dn, [])
        g_up = _mm(sv["hn"], dup, ta=True, out_dtypes=(BF16,), tn=ffs, out_slots=True, name=f"b_w_up{l}",
                   comm=carried_wup)
        g_up, r_carried_wup = g_up if carried_wup else (g_up, [])
        dhn = _mm(dup, w_upg[l], tb=True, name=f"b_up{l}", comm=carried_up)
        dhn, r_carried_up = dhn if carried_up else (dhn, [])
        (dh0,), (g_nmlp,) = _rows_vjp(_rms_stage, [sv["h"]], [row(norm_mlp, l)], [dhn], row_grads={0: F32},
                                      adds={0: dh2}, name=f"b_norm_mlp{l}")
        return dh0, dict(w_ple_proj=r_pp, w_ple_gate=r_pg, norm_ple=g_nple, w_down=g_dn, w_up=g_up, norm_mlp=g_nmlp,
                         carried=r_carried, carried_wup=r_carried_wup, carried_up=r_carried_up)

    dh4, gb1 = block_bwd(dh, 1, sv1)
    up1_a, up1_b = halves(gb1["w_up"])
    dglu = _mm(dh4, w_oo, tb=True, name="b_out_o")
    g_oo = _mm(glu, dh4, ta=True, out_dtypes=(BF16,), name="b_w_out_o")
    (dy1, dgl), (g_bglu,) = _rows_vjp(_glu_stage, [y, gl_raw], [bglu_full], [dglu], row_grads={0: F32, 1: BF16},
                                      name="b_glu_gate")
    g_gl = _mm(act_g, dgl, ta=True, out_dtypes=(BF16,), name="b_w_glu")
    dact = _mm(dgl, w_gl, tb=True, name="b_glu")
    (dy,), _ = _rows_vjp(_gelu_stage, [y], [], [dact], row_grads={0: F32}, adds={0: dy1}, name="b_gelu")
    (du, dwb, dwc, g_s5d, dlam), (r_dn1, r_up1_a) = _s5_bwd(u, dy, wb, wc, lt, s5d_full, cins, s5_states, name="b_s5",
                                                           comm=(rows8(gb1["w_down"]), up1_a))
    (dh3,), (g_nmix1,) = _rows_vjp(_rms_stage, [h3], [row(norm_mix, 1)], [du], row_grads={0: F32}, adds={0: dh4},
                                   name="b_norm_mix1")
    dlr, dli, dbr, dbi, g_cre, g_cim = _s5_unpack(dwb, dwc, dlam)
    g_are, g_aim, g_ldt, g_bre, g_bim, _ = _small_vjp(_s5_prep_stage, prep_in, [dlr, dli, dbr, dbi], name="b_s5_prep")

    early_grads = dict(
        s5_a_re=g_are[None], s5_a_im=g_aim[None], s5_b_re=g_bre.reshape(s5_b_re.shape),
        s5_b_im=g_bim.reshape(s5_b_im.shape), s5_c_re=g_cre[None], s5_c_im=g_cim[None],
        s5_log_dt=g_ldt.reshape(1, groups), final_norm=d_fnorm.reshape(d), s5_d=g_s5d, b_glu=g_bglu)
    dh1, gb0 = block_bwd(dh3, 0, sv0, (rows8(g_oo), rows8(g_gl)), (up1_b,),
                         (_Gather(_pack(list(early_grads.values()))),))
    r_oo, r_gl = gb0["carried"]
    r_up1 = joined(r_up1_a, gb0["carried_wup"][0])
    (early_parts,) = gb0["carried_up"]
    dn0_a, dn0_b = halves(gb0["w_down"].reshape(N_DEV, ffs, d))
    up0_a, up0_b = halves(gb0["w_up"])
    doa = _mm(dh1, w_top, tb=True, name="b_out_a")
    dob = _mm(dh1, w_bot, tb=True, name="b_out_b")
    g_oe = jnp.concatenate([_mm(oa, dh1, ta=True, out_dtypes=(BF16,), name="b_w_out_a"),
                            _mm(ob, dh1, ta=True, out_dtypes=(BF16,), name="b_w_out_b")], axis=0)
    (dq, df, di, dg, dlb, g_gna), (r_dn0_a,) = _hgrn_bwd(proj, lb0, g_norm_a, st_a, doa, heads=ha, name="b_hgrn",
                                                        comm=(dn0_a,))
    (dqb, dkb, dvb, dab, dz, dhp, g_gnb), (r_dn0_b, r_up0_a) = _delta_bwd(
        qkv, ab, proj, hp, g_norm_b, st_b, dob, heads=hb, z_off=z_col // HEAD, name="b_delta", comm=(dn0_b, up0_a))
    (dqkv, g_conv), (r_oe,) = _conv_bwd(proj, conv_full, jnp.concatenate([dqb, dkb, dvb], axis=1), col_off=2 * d,
                                        name="b_conv", comm=(rows8(g_oe),))
    dproj = jnp.concatenate([dq, df, di, dg, dqkv, dz], axis=1)
    g_main, (r_up0_b,) = _mm(hn0, dproj, ta=True, out_dtypes=(BF16,), name="b_w_proj", comm=(up0_b,))
    r_dn0, r_up0 = joined(r_dn0_a, r_dn0_b), joined(r_up0_a, r_up0_b)
    g_tail = _mm(hn0, dab, ta=True, out_dtypes=(BF16,), name="b_w_ab")
    dhn0, (r_in,) = _mm(dproj, w_main, tb=True, name="b_proj",
                        comm=(cols8(jnp.concatenate([g_main, g_tail[:, :2 * hb]], axis=1)),))
    dhn0 = _mm(dab, w_tail, tb=True, extras=(dhn0,), epilogue=_add_epilogue, name="b_ab")
    (dx,), (g_nmix0,) = _rows_vjp(_rms_stage, [x2], [row(norm_mix, 0)], [dhn0], row_grads={0: F32}, adds={0: dh1},
                                  name="b_norm_mix0")
    g_lb = jnp.concatenate(_small_vjp(_lb0_stage, lb_rows, [dlb], name="b_lb0"), axis=0)

    late_grads = dict(
        norm_mix=jnp.concatenate([g_nmix0, g_nmix1], axis=0),
        norm_mlp=jnp.concatenate([gb0["norm_mlp"], gb1["norm_mlp"]], axis=0),
        norm_ple=jnp.concatenate([gb0["norm_ple"], gb1["norm_ple"]], axis=0),
        hgrn_lb=g_lb, g_norm_a=g_gna, a_log=dhp[0:1, :hb], dt_bias=dhp[1:2, :hb], g_norm_b=g_gnb, conv_w=g_conv)
    rep_names = ["norm_mix", "norm_mlp", "norm_ple", "hgrn_lb", "g_norm_a", "a_log", "dt_bias", "g_norm_b", "s5_a_re",
                 "s5_a_im", "s5_b_re", "s5_b_im", "s5_c_re", "s5_c_im", "s5_log_dt", "final_norm"]
    late_parts = _comm_call(_Gather(_pack(list(late_grads.values()))), name="ag_small_grads")
    summed = {}
    for tag, grads, parts in (("early", early_grads, early_parts), ("late", late_grads, late_parts)):
        sums = _unpack(_sum_parts(parts, name=f"sum_small_grads_{tag}"), [g.shape for g in grads.values()])
        summed.update(zip(grads, sums))
    cw = conv_w.shape[2]
    dshard = d // N_DEV
    shard_g = dict(conv_w=lax.dynamic_slice(summed["conv_w"], (0, me * cw), (CONV_WIDTH, cw))[None],
                   s5_d=lax.dynamic_slice(summed["s5_d"], (0, me * dshard), (1, dshard)),
                   b_glu=lax.dynamic_slice(summed["b_glu"], (0, me * dshard), (1, dshard)))
    small_names = rep_names + ["conv_w", "s5_d", "b_glu"]
    g_small = [summed[k] if k in rep_names else shard_g[k] for k in small_names]
    shapes = [args[k].shape for k in small_names]
    sm_out = _adamw(_pack([args[k] for k in small_names])[None], [_pack(g_small)[None]],
                    _pack([args["m_" + k] for k in small_names])[None], _pack([args["v_" + k] for k in small_names])[None],
                    name="adamw_small")
    sm_out = [dict(zip(small_names, _unpack(o[0], shapes))) for o in sm_out]

    received = dict(w_in_e=[r_in], w_out_e=[r_oe], w_glu=[r_gl], w_out_o=[r_oo], w_up=[r_up0, r_up1],
                    w_down=[r_dn0, r_dn1], w_ple_gate=[gb0["w_ple_gate"], gb1["w_ple_gate"]],
                    w_ple_proj=[gb0["w_ple_proj"], gb1["w_ple_proj"]])
    big_out = {k: _adamw(args[k], layers, args["m_" + k], args["v_" + k], name="adamw_" + k)
               for k, layers in received.items()}

    names = ["norm_mix", "norm_mlp", "norm_ple", "w_in_e", "w_out_e", "hgrn_lb", "g_norm_a", "conv_w", "a_log", "dt_bias",
             "g_norm_b", "s5_a_re", "s5_a_im", "s5_b_re", "s5_b_im", "s5_c_re", "s5_c_im", "s5_d", "s5_log_dt", "w_glu",
             "b_glu", "w_out_o", "w_up", "w_down", "w_ple_gate", "w_ple_proj", "final_norm"]
    result = [loss, dx[None]]
    for j in range(4):
        result += [big_out[k][j] if k in big_out else sm_out[j][k] for k in names]
    return tuple(result)
```

```python
import functools
import math
import operator

import numpy as np
import jax
import jax.numpy as jnp
from jax import lax
from jax.experimental import pallas as pl
from jax.experimental.pallas import tpu as pltpu

F32 = jnp.float32
BF16 = jnp.bfloat16
MM_DTYPE = BF16
HI = lax.Precision.HIGHEST
MESH = pl.DeviceIdType.MESH

NORM_EPS = 1e-6
CHUNK = 64
HEAD = 128
CONV_WIDTH = 4
S5_GROUP = 16
S5_STATE = 64
S5_GB = 8
S5_HALF = S5_GB * S5_STATE
N_DEV = 8
HEADS_PER_STEP = 8
ADAM_LR, ADAM_B1, ADAM_B2, ADAM_EPS, ADAM_WD, ADAM_STEP = 0.001, 0.9, 0.999, 1e-08, 0.01, 10
VMEM_LIMIT = 56 * 1024 * 1024

NN = (((1,), (0,)), ((), ()))
NT = (((1,), (1,)), ((), ()))
TN = (((0,), (0,)), ((), ()))


def _dot(a, b, dn=NN):
    return lax.dot_general(a, b, dn, precision=HI, preferred_element_type=F32)


def _hdot(a, b, dn=NN):
    return lax.dot_general(a, b, dn, precision=lax.Precision.HIGH, preferred_element_type=F32)


def _bdot_raw(a, b, dn=NN):
    return lax.dot_general(a.astype(BF16), b.astype(BF16), dn, preferred_element_type=F32)


@functools.partial(jax.custom_vjp, nondiff_argnums=(2,))
def _bdot(a, b, dn):
    return _bdot_raw(a, b, dn)


def _bdot_fwd(a, b, dn):
    return _bdot_raw(a, b, dn), (a, b)


def _bdot_bwd(dn, res, g):
    a, b = res
    if dn == NN:
        return _bdot_raw(g, b, NT), _bdot_raw(a, g, TN)
    if dn == NT:
        return _bdot_raw(g, b, NN), _bdot_raw(g, a, TN)
    assert dn == TN
    return _bdot_raw(b, g, NT), _bdot_raw(a, g, NN)


_bdot.defvjp(_bdot_fwd, _bdot_bwd)


def _per_head(f):
    def g(*args, **kw):
        n = [len(a.vals) for a in args if isinstance(a, _Heads)]
        if not n:
            return f(*args, **kw)
        return _Heads([f(*[a.vals[j] if isinstance(a, _Heads) else a for a in args], **kw) for j in range(n[0])])
    return g


class _Heads:
    def __init__(self, vals):
        self.vals = list(vals)

    def __add__(self, o):
        return _per_head(operator.add)(self, o)

    def __radd__(self, o):
        return _per_head(operator.add)(o, self)

    def __sub__(self, o):
        return _per_head(operator.sub)(self, o)

    def __rsub__(self, o):
        return _per_head(operator.sub)(o, self)

    def __mul__(self, o):
        return _per_head(operator.mul)(self, o)

    def __rmul__(self, o):
        return _per_head(operator.mul)(o, self)

    def __neg__(self):
        return _per_head(operator.neg)(self)


_exp, _log, _where, _sum, _mean = (_per_head(f) for f in (jnp.exp, jnp.log, jnp.where, jnp.sum, jnp.mean))
_sigmoid, _rsqrt, _equal = _per_head(jax.nn.sigmoid), _per_head(lax.rsqrt), _per_head(operator.eq)
_hdot_h, _bdot_h = _per_head(_hdot), _per_head(_bdot)


def _params(n_axes):
    return pltpu.CompilerParams(dimension_semantics=("arbitrary",) * n_axes, vmem_limit_bytes=VMEM_LIMIT)


def _full_spec(a):
    nd = a.ndim
    return pl.BlockSpec(a.shape, lambda *_: (0,) * nd)


def _mm(a, b, *, name, ta=False, tb=False, extras=(), epilogue=None, out_dtypes=(F32,), tm=1024, tn=1024, tk=2048,
        out_slots=False, comm=()):
    m = a.shape[1] if ta else a.shape[0]
    k = a.shape[0] if ta else a.shape[1]
    n = b.shape[0] if tb else b.shape[1]
    assert k == (b.shape[1] if tb else b.shape[0]), (name, a.shape, b.shape)
    tm, tn, tk = min(tm, m), min(tn, n), min(tk, k)
    assert m % tm == 0 and n % tn == 0 and k % tk == 0, (name, m, n, k)
    nk = k // tk
    n_ex, n_out = len(extras), len(out_dtypes)
    dn = (((0 if ta else 1,), (1 if tb else 0,)), ((), ()))

    def body(a_ref, b_ref, *rest):
        ex_refs, out_refs = rest[:n_ex], rest[n_ex:n_ex + n_out]
        part = lax.dot_general(a_ref[...].astype(MM_DTYPE), b_ref[...].astype(MM_DTYPE), dn, preferred_element_type=F32)

        def finish(acc):
            outs = epilogue(acc, *[r[...] for r in ex_refs]) if epilogue is not None else (acc,)
            for o_ref, o in zip(out_refs, outs):
                o_ref[...] = o.astype(o_ref.dtype)

        if nk == 1:
            finish(part)
            return
        acc_ref = rest[-1]
        kk = pl.program_id(2)

        @pl.when(kk == 0)
        def _():
            acc_ref[...] = part

        @pl.when((kk > 0) & (kk < nk - 1))
        def _():
            acc_ref[...] += part

        @pl.when(kk == nk - 1)
        def _():
            finish(acc_ref[...] + part)

    a_spec = pl.BlockSpec((tk, tm), lambda i, j, q: (q, i)) if ta else pl.BlockSpec((tm, tk), lambda i, j, q: (i, q))
    b_spec = pl.BlockSpec((tn, tk), lambda i, j, q: (j, q)) if tb else pl.BlockSpec((tk, tn), lambda i, j, q: (q, j))
    ex_specs = []
    for e in extras:
        if e.shape[0] == 1 and m != 1:
            ex_specs.append(pl.BlockSpec((1, tn), lambda i, j, q: (0, j)))
        else:
            ex_specs.append(pl.BlockSpec((tm, tn), lambda i, j, q: (i, j)))
    if out_slots:
        out_spec, out_dims = pl.BlockSpec((None, tm, tn), lambda i, j, q: (j, i, 0)), (n // tn, m, tn)
    else:
        out_spec, out_dims = pl.BlockSpec((tm, tn), lambda i, j, q: (i, j)), (m, n)
    outs, exchanged = _call(
        body, (a, b, *extras), name=name, grid=(m // tm, n // tn, nk),
        in_specs=[a_spec, b_spec] + ex_specs,
        out_specs=[out_spec for _ in out_dtypes],
        out_shape=[jax.ShapeDtypeStruct(out_dims, dt) for dt in out_dtypes],
        scratch_shapes=[pltpu.VMEM((tm, tn), F32)] if nk > 1 else [], comm=comm)
    outs = outs[0] if n_out == 1 else tuple(outs)
    return (outs, exchanged) if comm else outs


def _rows_call(fn, rows, consts, out_dtypes, *, name, tr=256):
    s = rows[0].shape[0]
    tr = min(tr, s)
    nr, nc = len(rows), len(consts)
    widths = [o.shape[1] for o in jax.eval_shape(
        fn, *[jax.ShapeDtypeStruct((tr, r.shape[1]), F32) for r in rows],
        *[jax.ShapeDtypeStruct(c.shape, F32) for c in consts])]

    def body(*refs):
        rv = [r[...].astype(F32) for r in refs[:nr]]
        cv = [c[...] for c in refs[nr:nr + nc]]
        for o_ref, o in zip(refs[nr + nc:], fn(*rv, *cv)):
            o_ref[...] = o.astype(o_ref.dtype)

    outs = pl.pallas_call(
        body, name=name, grid=(s // tr,),
        in_specs=[pl.BlockSpec((tr, r.shape[1]), lambda i: (i, 0)) for r in rows] + [_full_spec(c) for c in consts],
        out_specs=[pl.BlockSpec((tr, w), lambda i: (i, 0)) for w in widths],
        out_shape=[jax.ShapeDtypeStruct((s, w), dt) for w, dt in zip(widths, out_dtypes)],
        compiler_params=_params(1),
    )(*rows, *consts)
    return outs[0] if len(outs) == 1 else tuple(outs)


def _rows_vjp(fn, rows, consts, cots, *, name, row_grads, adds=None, tr=256):
    adds = adds or {}
    s = rows[0].shape[0]
    tr = min(tr, s)
    nr, nc, nt = len(rows), len(consts), len(cots)
    rg = sorted(row_grads)
    ad = sorted(adds)

    def body(*refs):
        rv = [r[...].astype(F32) for r in refs[:nr]]
        cv = [c[...] for c in refs[nr:nr + nc]]
        ct = [c[...].astype(F32) for c in refs[nr + nc:nr + nc + nt]]
        av = {i: r[...].astype(F32) for i, r in zip(ad, refs[nr + nc + nt:nr + nc + nt + len(ad)])}
        out_refs = refs[nr + nc + nt + len(ad):]
        _, vjp = jax.vjp(fn, *rv, *cv)
        grads = vjp(tuple(ct))
        for o_ref, i in zip(out_refs[:len(rg)], rg):
            g = grads[i]
            if i in av:
                g = g + av[i]
            o_ref[...] = g.astype(o_ref.dtype)

        @pl.when(pl.program_id(0) == 0)
        def _():
            for o_ref in out_refs[len(rg):]:
                o_ref[...] = jnp.zeros_like(o_ref)

        for o_ref, g in zip(out_refs[len(rg):], grads[nr:]):
            o_ref[...] += g

    row_spec = lambda a: pl.BlockSpec((tr, a.shape[1]), lambda i: (i, 0))
    outs = pl.pallas_call(
        body, name=name, grid=(s // tr,),
        in_specs=[row_spec(r) for r in rows] + [_full_spec(c) for c in consts] + [row_spec(c) for c in cots]
        + [row_spec(adds[i]) for i in ad],
        out_specs=[row_spec(rows[i]) for i in rg] + [_full_spec(c) for c in consts],
        out_shape=[jax.ShapeDtypeStruct(rows[i].shape, row_grads[i]) for i in rg]
        + [jax.ShapeDtypeStruct(c.shape, F32) for c in consts],
        compiler_params=_params(1),
    )(*rows, *consts, *cots, *[adds[i] for i in ad])
    return list(outs[:len(rg)]), list(outs[len(rg):])


def _small_call(fn, ins, *, name):
    shapes = jax.eval_shape(fn, *[jax.ShapeDtypeStruct(a.shape, F32) for a in ins])

    def body(*refs):
        for o_ref, o in zip(refs[len(ins):], fn(*[r[...] for r in refs[:len(ins)]])):
            o_ref[...] = o

    return pl.pallas_call(
        body, name=name, in_specs=[_full_spec(a) for a in ins],
        out_specs=[pl.BlockSpec(o.shape, functools.partial(lambda nd, *_: (0,) * nd, len(o.shape))) for o in shapes],
        out_shape=[jax.ShapeDtypeStruct(o.shape, F32) for o in shapes], grid=(1,),
        compiler_params=_params(1),
    )(*ins)


def _small_vjp(fn, ins, cots, *, name):
    def body(*refs):
        vals = [r[...] for r in refs[:len(ins)]]
        ct = [r[...] for r in refs[len(ins):len(ins) + len(cots)]]
        _, vjp = jax.vjp(fn, *vals)
        for o_ref, g in zip(refs[len(ins) + len(cots):], vjp(tuple(ct))):
            o_ref[...] = g

    return pl.pallas_call(
        body, name=name, in_specs=[_full_spec(a) for a in ins] + [_full_spec(c) for c in cots],
        out_specs=[_full_spec(a) for a in ins],
        out_shape=[jax.ShapeDtypeStruct(a.shape, F32) for a in ins], grid=(1,),
        compiler_params=_params(1),
    )(*ins, *cots)


def _rms(x, g):
    return x * _rsqrt(_mean(x * x, axis=-1, keepdims=True) + NORM_EPS) * g


def _rms_stage(x, g):
    return (_rms(x, g),)


def _silu(x):
    return x * _sigmoid(x)


def _softplus(x):
    return jnp.maximum(x, 0.0) + jnp.log1p(jnp.exp(-jnp.abs(x)))


def _gelu(x):
    return jax.nn.gelu(x, approximate=True)


def _gelu_stage(y):
    return (_gelu(y),)


def _glu_stage(y, gl_raw, b):
    return (_gelu(y) * jax.nn.sigmoid(gl_raw + b),)


def _ple_stage(h, gpre, pp):
    return (h + jax.nn.sigmoid(gpre) * pp,)


def _relu2_grad_epilogue(acc, up):
    return (acc * (2.0 * jnp.maximum(up, 0.0)),)


def _lb0_stage(x0, x1, x2):
    mx = jnp.maximum(jnp.maximum(x0, x1), x2)
    e0, e1, e2 = jnp.exp(x0 - mx), jnp.exp(x1 - mx), jnp.exp(x2 - mx)
    return (e0 / (e0 + e1 + e2),)


def _s5_prep_stage(a_re, a_im, log_dt, b_re, b_im, expand):
    step = jnp.exp(log_dt)
    mag = jnp.exp(a_re * step)
    lr = mag * jnp.cos(a_im * step)
    li = mag * jnp.sin(a_im * step)
    den = a_re * a_re + a_im * a_im
    cr = ((lr - 1.0) * a_re + li * a_im) / den
    ci = (li * a_re - (lr - 1.0) * a_im) / den
    cr_e, ci_e = _dot(cr, expand), _dot(ci, expand)
    return lr, li, cr_e * b_re - ci_e * b_im, cr_e * b_im + ci_e * b_re


def _loss_call(h, g, target, *, name, tr=256):
    s, d = h.shape
    tr = min(tr, s)

    def loss_fn(hv, gv, tv):
        err = _rms(hv, gv) - tv
        return 0.5 * jnp.sum(jnp.mean(err * err, axis=-1))

    def body(h_ref, g_ref, t_ref, dh_ref, dg_ref, loss_ref):
        val, (dh, dg) = jax.value_and_grad(loss_fn, argnums=(0, 1))(h_ref[...], g_ref[...], t_ref[...])
        dh_ref[...] = dh

        @pl.when(pl.program_id(0) == 0)
        def _():
            dg_ref[...] = jnp.zeros_like(dg_ref)
            loss_ref[...] = jnp.zeros_like(loss_ref)

        dg_ref[...] += dg
        loss_ref[...] += jnp.full(loss_ref.shape, val, F32)

    row = pl.BlockSpec((tr, d), lambda i: (i, 0))
    return pl.pallas_call(
        body, name=name, grid=(s // tr,),
        in_specs=[row, _full_spec(g), row],
        out_specs=[row, _full_spec(g), pl.BlockSpec((8, 128), lambda i: (0, 0))],
        out_shape=[jax.ShapeDtypeStruct((s, d), F32), jax.ShapeDtypeStruct(g.shape, F32),
                   jax.ShapeDtypeStruct((8, 128), F32)],
        compiler_params=_params(1),
    )(h, g, target)


def _hgrn_chunk(q, fp, iv, gp, lb, gn, st_t):
    c = CHUNK
    row = lax.broadcasted_iota(jnp.int32, (c, c), 0)
    col = lax.broadcasted_iota(jnp.int32, (c, c), 1)
    causal = row >= col
    fg = lb + (1.0 - lb) * _sigmoid(fp)
    k = 1.0 - fg
    lf = _log(fg)
    cum = _hdot_h(causal.astype(F32), lf, NN)
    first_half = (lax.broadcasted_iota(jnp.int32, (c, 1), 0) < c // 2).astype(F32)
    ref = _sum(lf * first_half, axis=0, keepdims=True)
    cend = _sum(lf, axis=0, keepdims=True)
    scores = _where(causal, _hdot_h(q * _exp(cum - ref), k * _exp(ref - cum), NT), 0.0)
    out = _bdot_h(scores, iv, NN) + _bdot_h(q * _exp(cum), st_t, NT)
    st_new = st_t * _exp(cend) + _bdot_h(iv, k * _exp(cend - cum), TN)
    res = _rms(out, gn) * _silu(gp)
    return res, st_new


def _hgrn_heads(qs, fs, ivs, gs, lbs, gn, sts):
    res, st_new = _hgrn_chunk(_Heads(qs), _Heads(fs), _Heads(ivs), _Heads(gs), _Heads(lbs), gn, _Heads(sts))
    return res.vals, st_new.vals


def _lanes(j):
    return slice(j * HEAD, (j + 1) * HEAD)


def _hgrn_fwd(proj, lb, gn, *, heads, name, comm=()):
    s = proj.shape[0]
    n = s // CHUNK
    hpb = min(HEADS_PER_STEP, heads)
    assert heads % hpb == 0

    def body(q_ref, f_ref, i_ref, g_ref, lb_ref, gn_ref, o_ref, st_ref, state):
        @pl.when(pl.program_id(1) == 0)
        def _():
            state[...] = jnp.zeros_like(state)

        gnv = gn_ref[...]
        loaded = [(q_ref[:, _lanes(j)], f_ref[:, _lanes(j)], i_ref[:, _lanes(j)], g_ref[:, _lanes(j)],
                   lb_ref[:, _lanes(j)], state[j]) for j in range(hpb)]
        qs, fs, ivs, gs, lbs, sts = (list(t) for t in zip(*loaded))
        res, st_new = _hgrn_heads(qs, fs, ivs, gs, lbs, gnv, sts)
        for j in range(hpb):
            st_ref[j] = sts[j]
            o_ref[:, _lanes(j)] = res[j].astype(o_ref.dtype)
            state[j] = st_new[j]

    wide = hpb * HEAD
    blk = lambda off: pl.BlockSpec((CHUNK, wide), lambda h, c: (c, off // hpb + h))
    return _call(
        body, (proj, proj, proj, proj, lb, gn), name=name, grid=(heads // hpb, n),
        in_specs=[blk(0), blk(heads), blk(2 * heads), blk(3 * heads),
                  pl.BlockSpec((1, wide), lambda h, c: (0, h)), pl.BlockSpec((1, HEAD), lambda h, c: (0, 0))],
        out_specs=[pl.BlockSpec((CHUNK, wide), lambda h, c: (c, h)),
                   pl.BlockSpec((hpb, None, HEAD, HEAD), lambda h, c: (h, c, 0, 0))],
        out_shape=[jax.ShapeDtypeStruct((s, heads * HEAD), BF16), jax.ShapeDtypeStruct((heads, n, HEAD, HEAD), F32)],
        scratch_shapes=[pltpu.VMEM((hpb, HEAD, HEAD), F32)], comm=comm)


def _hgrn_bwd(proj, lb, gn, states, d_out, *, heads, name, comm=()):
    s = proj.shape[0]
    n = s // CHUNK
    hpb = min(HEADS_PER_STEP, heads)

    def body(q_ref, f_ref, i_ref, g_ref, lb_ref, gn_ref, st_ref, do_ref,
             dq_ref, df_ref, di_ref, dg_ref, dlb_ref, dgn_ref, dstate):
        h, c = pl.program_id(0), pl.program_id(1)

        @pl.when(c == 0)
        def _():
            dstate[...] = jnp.zeros_like(dstate)
            dlb_ref[...] = jnp.zeros_like(dlb_ref)

        @pl.when((c == 0) & (h == 0))
        def _():
            dgn_ref[...] = jnp.zeros_like(dgn_ref)

        gnv = gn_ref[...]
        loaded = [(q_ref[:, _lanes(j)], f_ref[:, _lanes(j)], i_ref[:, _lanes(j)], g_ref[:, _lanes(j)],
                   lb_ref[:, _lanes(j)], st_ref[j], do_ref[:, _lanes(j)].astype(F32), dstate[j]) for j in range(hpb)]
        qs, fs, ivs, gs, lbs, sts, dos, dss = (list(t) for t in zip(*loaded))
        _, vjp = jax.vjp(_hgrn_heads, qs, fs, ivs, gs, lbs, gnv, sts)
        dqs, dfs, dis, dgs, dlbs, dgn_sum, dsts = vjp((dos, dss))
        for j in range(hpb):
            ln = _lanes(j)
            dq_ref[:, ln] = dqs[j].astype(dq_ref.dtype)
            df_ref[:, ln] = dfs[j].astype(df_ref.dtype)
            di_ref[:, ln] = dis[j].astype(di_ref.dtype)
            dg_ref[:, ln] = dgs[j].astype(dg_ref.dtype)
            dlb_ref[:, ln] += dlbs[j]
            dstate[j] = dsts[j]
        dgn_ref[...] += dgn_sum

    wide = hpb * HEAD
    rev = lambda off: pl.BlockSpec((CHUNK, wide), lambda h, c: (n - 1 - c, off // hpb + h))
    out_blk = pl.BlockSpec((CHUNK, wide), lambda h, c: (n - 1 - c, h))
    width = heads * HEAD
    return _call(
        body, (proj, proj, proj, proj, lb, gn, states, d_out), name=name, grid=(heads // hpb, n),
        in_specs=[rev(0), rev(heads), rev(2 * heads), rev(3 * heads),
                  pl.BlockSpec((1, wide), lambda h, c: (0, h)), pl.BlockSpec((1, HEAD), lambda h, c: (0, 0)),
                  pl.BlockSpec((hpb, None, HEAD, HEAD), lambda h, c: (h, n - 1 - c, 0, 0)), out_blk],
        out_specs=[out_blk, out_blk, out_blk, out_blk,
                   pl.BlockSpec((1, wide), lambda h, c: (0, h)), pl.BlockSpec((1, HEAD), lambda h, c: (0, 0))],
        out_shape=[jax.ShapeDtypeStruct((s, width), BF16)] * 4
        + [jax.ShapeDtypeStruct((1, width), F32), jax.ShapeDtypeStruct((1, HEAD), F32)],
        scratch_shapes=[pltpu.VMEM((hpb, HEAD, HEAD), F32)], comm=comm)


def _shift_rows(x, d, rowi):
    if d == 0:
        return x
    n = x.shape[0]
    rolled = pltpu.roll(x, d % n, 0)
    keep = rowi >= d if d > 0 else rowi < n + d
    return jnp.where(keep, rolled, 0.0)


def _conv_pre(x, w_ref, rowi):
    acc = None
    for j in range(CONV_WIDTH):
        term = w_ref[j:j + 1, :] * _shift_rows(x, CONV_WIDTH - 1 - j, rowi)
        acc = term if acc is None else acc + term
    return acc


def _conv_fwd(proj, w, *, col_off, name, cb=256):
    s = proj.shape[0]
    width = w.shape[1]
    cb = min(cb, width)

    def body(x_ref, w_ref, o_ref):
        rowi = lax.broadcasted_iota(jnp.int32, (s, cb), 0)
        o_ref[...] = _silu(_conv_pre(x_ref[...], w_ref, rowi))

    return pl.pallas_call(
        body, name=name, grid=(width // cb,),
        in_specs=[pl.BlockSpec((s, cb), lambda j: (0, col_off // cb + j)), pl.BlockSpec((CONV_WIDTH, cb), lambda j: (0, j))],
        out_specs=pl.BlockSpec((s, cb), lambda j: (0, j)),
        out_shape=jax.ShapeDtypeStruct((s, width), F32),
        compiler_params=_params(1),
    )(proj, w)


def _conv_bwd(proj, w, d_out, *, col_off, name, cb=256, comm=()):
    s = proj.shape[0]
    width = w.shape[1]
    cb = min(cb, width)

    def body(x_ref, w_ref, do_ref, dx_ref, dw_ref):
        rowi = lax.broadcasted_iota(jnp.int32, (s, cb), 0)
        x = x_ref[...]
        pre = _conv_pre(x, w_ref, rowi)
        sg = jax.nn.sigmoid(pre)
        dpre = do_ref[...] * (sg + pre * sg * (1.0 - sg))
        dx = None
        for j in range(CONV_WIDTH):
            d = CONV_WIDTH - 1 - j
            term = w_ref[j:j + 1, :] * _shift_rows(dpre, -d, rowi)
            dx = term if dx is None else dx + term
            dw_ref[j:j + 1, :] = jnp.sum(dpre * _shift_rows(x, d, rowi), axis=0, keepdims=True)
        dx_ref[...] = dx.astype(dx_ref.dtype)

    return _call(
        body, (proj, w, d_out), name=name, grid=(width // cb,),
        in_specs=[pl.BlockSpec((s, cb), lambda j: (0, col_off // cb + j)), pl.BlockSpec((CONV_WIDTH, cb), lambda j: (0, j)),
                  pl.BlockSpec((s, cb), lambda j: (0, j))],
        out_specs=[pl.BlockSpec((s, cb), lambda j: (0, j)), pl.BlockSpec((CONV_WIDTH, cb), lambda j: (0, j))],
        out_shape=[jax.ShapeDtypeStruct((s, width), BF16), jax.ShapeDtypeStruct((CONV_WIDTH, width), F32)],
        comm=comm)


_lane_concat = _per_head(lambda a, b: jnp.concatenate([a, b], axis=1))
_lane_half = _per_head(lambda a, j: a[:, j * HEAD:(j + 1) * HEAD])


def _tri_inverse(lower):
    c = CHUNK
    row = lax.broadcasted_iota(jnp.int32, (c, c), 0)
    col = lax.broadcasted_iota(jnp.int32, (c, c), 1)
    inv = (row == col).astype(F32)
    lvl = 0
    while (1 << lvl) < c:
        same_pair = (row >> (lvl + 1)) == (col >> (lvl + 1))
        off_block = same_pair & (((row >> lvl) & 1) == 1) & (((col >> lvl) & 1) == 0)
        inv = inv - _hdot_h(_hdot_h(inv, _where(off_block, lower, 0.0), NN), inv, NN)
        lvl += 1
    return inv


@jax.custom_vjp
def _tri_solve(lowers, rhss):
    return _tri_solve_fwd(lowers, rhss)[0]


def _tri_solve_fwd(lowers, rhss):
    inv = _tri_inverse(_Heads(lowers))
    sol = _hdot_h(inv, _Heads(rhss), NN)
    return sol.vals, (inv.vals, sol.vals)


def _tri_solve_bwd(res, g):
    inv, sol = _Heads(res[0]), _Heads(res[1])
    d_rhs = _hdot_h(inv, _Heads(g), TN)
    return (-_hdot_h(d_rhs, sol, NT)).vals, d_rhs.vals


_tri_solve.defvjp(_tri_solve_fwd, _tri_solve_bwd)


def _solve(lower, rhs):
    if isinstance(lower, _Heads):
        return _Heads(_tri_solve(lower.vals, rhs.vals))
    return _tri_solve([lower], [rhs])[0]


def _delta_chunk(h, heads, qr, kr, vr, ab, zp, alog, dtb, gn, st):
    c = CHUNK
    row = lax.broadcasted_iota(jnp.int32, (c, c), 0)
    col = lax.broadcasted_iota(jnp.int32, (c, c), 1)
    causal = row >= col
    strict = row > col
    lane = lax.broadcasted_iota(jnp.int32, (c, HEAD), 1)
    mine = _equal(h, lane)
    la_full = -jnp.exp(alog) * _softplus(ab + dtb)
    cum_full = _hdot(causal.astype(F32), la_full)
    cum = _sum(_where(mine, cum_full, 0.0), axis=1, keepdims=True)
    cend = _sum(_sum(_where(mine, la_full, 0.0), axis=1, keepdims=True), axis=0, keepdims=True)
    beta = _sum(_where(_equal(heads + h, lane), jax.nn.sigmoid(ab), 0.0), axis=1, keepdims=True)
    cum_row = _hdot_h(_where(mine, 1.0, 0.0), cum_full, NT)
    decay = _where(causal, _exp(_where(causal, cum - cum_row, 0.0)), 0.0)
    qn = qr * _rsqrt(_sum(qr * qr, axis=-1, keepdims=True) + NORM_EPS) * (HEAD ** -0.5)
    kn = kr * _rsqrt(_sum(kr * kr, axis=-1, keepdims=True) + NORM_EPS)
    kb = kn * beta
    lower = _where(strict, _bdot_h(kb, kn, NT) * decay, 0.0)
    ecum = _exp(cum)
    sol = _solve(lower, _lane_concat(vr * beta, kb * ecum))
    u, w = _lane_half(sol, 0), _lane_half(sol, 1)
    intra = _bdot_h(qn, kn, NT) * decay
    v_new = u - _bdot_h(w, st, NN)
    out = _bdot_h(qn * ecum, st, NN) + _bdot_h(intra, v_new, NN)
    st_new = st * _exp(cend) + _bdot_h(kn * _exp(cend - cum), v_new, TN)
    res = _rms(out, gn) * _silu(zp)
    return res, st_new


def _delta_heads(hs, heads, qs, ks, vs, ab, zs, alog, dtb, gn, sts):
    res, st_new = _delta_chunk(_Heads(hs), heads, _Heads(qs), _Heads(ks), _Heads(vs), ab, _Heads(zs), alog, dtb, gn,
                               _Heads(sts))
    return res.vals, st_new.vals


def _delta_fwd(qkv, ab, proj, hp, gn, *, heads, z_off, name, comm=()):
    s = qkv.shape[0]
    n = s // CHUNK

    hpb = min(HEADS_PER_STEP, heads)
    assert heads % hpb == 0 and z_off % hpb == 0

    def body(q_ref, k_ref, v_ref, ab_ref, z_ref, hp_ref, gn_ref, o_ref, st_ref, state):
        hb = pl.program_id(1)

        @pl.when(pl.program_id(0) == 0)
        def _():
            for j in range(hpb):
                state[hb * hpb + j] = jnp.zeros((HEAD, HEAD), F32)

        shared = (ab_ref[...], hp_ref[0:1, :], hp_ref[1:2, :], gn_ref[...])
        loaded = [(q_ref[:, _lanes(j)], k_ref[:, _lanes(j)], v_ref[:, _lanes(j)], z_ref[:, _lanes(j)],
                   state[hb * hpb + j]) for j in range(hpb)]
        qs, ks, vs, zs, sts = (list(t) for t in zip(*loaded))
        res, st_new = _delta_heads([hb * hpb + j for j in range(hpb)], heads, qs, ks, vs, shared[0], zs, shared[1],
                                   shared[2], shared[3], sts)
        for j in range(hpb):
            st_ref[j] = sts[j]
            o_ref[:, _lanes(j)] = res[j].astype(o_ref.dtype)
            state[hb * hpb + j] = st_new[j]

    wide = hpb * HEAD
    blk = lambda off: pl.BlockSpec((CHUNK, wide), lambda c, h: (c, off // hpb + h))
    return _call(
        body, (qkv, qkv, qkv, ab, proj, hp, gn), name=name, grid=(n, heads // hpb),
        in_specs=[blk(0), blk(heads), blk(2 * heads), pl.BlockSpec((CHUNK, HEAD), lambda c, h: (c, 0)), blk(z_off),
                  pl.BlockSpec((8, HEAD), lambda c, h: (0, 0)), pl.BlockSpec((1, HEAD), lambda c, h: (0, 0))],
        out_specs=[pl.BlockSpec((CHUNK, wide), lambda c, h: (c, h)),
                   pl.BlockSpec((hpb, None, HEAD, HEAD), lambda c, h: (h, c, 0, 0))],
        out_shape=[jax.ShapeDtypeStruct((s, heads * HEAD), BF16), jax.ShapeDtypeStruct((heads, n, HEAD, HEAD), F32)],
        scratch_shapes=[pltpu.VMEM((heads, HEAD, HEAD), F32)], comm=comm)


def _delta_bwd(qkv, ab, proj, hp, gn, states, d_out, *, heads, z_off, name, comm=()):
    s = qkv.shape[0]
    n = s // CHUNK
    hpb = min(HEADS_PER_STEP, heads)

    def body(q_ref, k_ref, v_ref, ab_ref, z_ref, hp_ref, gn_ref, st_ref, do_ref,
             dq_ref, dk_ref, dv_ref, dab_ref, dz_ref, dhp_ref, dgn_ref, dstate):
        c, hb = pl.program_id(0), pl.program_id(1)

        @pl.when(c == 0)
        def _():
            for j in range(hpb):
                dstate[hb * hpb + j] = jnp.zeros((HEAD, HEAD), F32)

        @pl.when((c == 0) & (hb == 0))
        def _():
            dgn_ref[...] = jnp.zeros_like(dgn_ref)
            dhp_ref[...] = jnp.zeros_like(dhp_ref)

        @pl.when(hb == 0)
        def _():
            dab_ref[...] = jnp.zeros_like(dab_ref)

        shared = (ab_ref[...], hp_ref[0:1, :], hp_ref[1:2, :], gn_ref[...])
        loaded = [(q_ref[:, _lanes(j)], k_ref[:, _lanes(j)], v_ref[:, _lanes(j)], z_ref[:, _lanes(j)], st_ref[j],
                   do_ref[:, _lanes(j)].astype(F32), dstate[hb * hpb + j]) for j in range(hpb)]
        qs, ks, vs, zs, sts, dos, dss = (list(t) for t in zip(*loaded))
        fn = functools.partial(_delta_heads, [hb * hpb + j for j in range(hpb)], heads)
        _, vjp = jax.vjp(fn, qs, ks, vs, shared[0], zs, shared[1], shared[2], shared[3], sts)
        dqs, dks, dvs, dab, dzs, dal, ddt, dgn, dsts = vjp((dos, dss))
        for j in range(hpb):
            ln = _lanes(j)
            dq_ref[:, ln] = dqs[j]
            dk_ref[:, ln] = dks[j]
            dv_ref[:, ln] = dvs[j]
            dz_ref[:, ln] = dzs[j].astype(dz_ref.dtype)
            dstate[hb * hpb + j] = dsts[j]
        dab_ref[...] += dab
        dhp_ref[0:1, :] += dal
        dhp_ref[1:2, :] += ddt
        dgn_ref[...] += dgn

    wide = hpb * HEAD
    rev = lambda off: pl.BlockSpec((CHUNK, wide), lambda c, h: (n - 1 - c, off // hpb + h))
    width = heads * HEAD
    head_blk = pl.BlockSpec((CHUNK, wide), lambda c, h: (n - 1 - c, h))
    ab_blk = pl.BlockSpec((CHUNK, HEAD), lambda c, h: (n - 1 - c, 0))
    return _call(
        body, (qkv, qkv, qkv, ab, proj, hp, gn, states, d_out), name=name, grid=(n, heads // hpb),
        in_specs=[rev(0), rev(heads), rev(2 * heads), ab_blk, rev(z_off),
                  pl.BlockSpec((8, HEAD), lambda c, h: (0, 0)), pl.BlockSpec((1, HEAD), lambda c, h: (0, 0)),
                  pl.BlockSpec((hpb, None, HEAD, HEAD), lambda c, h: (h, n - 1 - c, 0, 0)), head_blk],
        out_specs=[head_blk, head_blk, head_blk, ab_blk, head_blk,
                   pl.BlockSpec((8, HEAD), lambda c, h: (0, 0)), pl.BlockSpec((1, HEAD), lambda c, h: (0, 0))],
        out_shape=[jax.ShapeDtypeStruct((s, width), F32)] * 3
        + [jax.ShapeDtypeStruct((s, HEAD), F32), jax.ShapeDtypeStruct((s, width), BF16),
           jax.ShapeDtypeStruct((8, HEAD), F32), jax.ShapeDtypeStruct((1, HEAD), F32)],
        scratch_shapes=[pltpu.VMEM((heads, HEAD, HEAD), F32)], comm=comm)


def _s5_scan(buf, lt_ref, cin_r, cin_i, tt, reverse):
    nblk = tt // 8
    hl = S5_HALF
    base = 8 if reverse else 0

    def body(j, carry):
        cr, ci = carry
        off = pl.multiple_of((nblk - 1 - j if reverse else j) * 8, 8)
        xr = buf[pl.ds(off, 8), 0:hl]
        xi = buf[pl.ds(off, 8), hl:2 * hl]
        for lv, d in enumerate((1, 2, 4)):
            ar, ai = lt_ref[base + 2 * lv], lt_ref[base + 2 * lv + 1]
            sr = pltpu.roll(xr, 8 - d if reverse else d, 0)
            si = pltpu.roll(xi, 8 - d if reverse else d, 0)
            xr, xi = xr + ar * sr - ai * si, xi + ar * si + ai * sr
        pr, pi = lt_ref[base + 6], lt_ref[base + 7]
        xr, xi = xr + pr * cr - pi * ci, xi + pr * ci + pi * cr
        buf[pl.ds(off, 8), 0:hl] = xr
        buf[pl.ds(off, 8), hl:2 * hl] = xi
        edge = 0 if reverse else 7
        return xr[edge:edge + 1, :], xi[edge:edge + 1, :]

    return lax.fori_loop(0, nblk, body, (cin_r, cin_i))


def _s5_fwd(u, wb, wc, lt, dskip, *, name, tt=512, comm=()):
    s, d = u.shape
    nb = d // HEAD
    tt = min(tt, s)
    nt = s // tt
    hl = S5_HALF

    def body(u_ref, wb_ref, wc_ref, lt_ref, d_ref, y_ref, cin_ref, st_ref, buf, carry):
        @pl.when(pl.program_id(1) == 0)
        def _():
            carry[...] = jnp.zeros_like(carry)

        cin_ref[...] = carry[0:1, :]
        uv = u_ref[...]
        buf[...] = _bdot_raw(uv, wb_ref[...])
        cr, ci = _s5_scan(buf, lt_ref, carry[0:1, 0:hl], carry[0:1, hl:2 * hl], tt, False)
        carry[0:1, 0:hl] = cr
        carry[0:1, hl:2 * hl] = ci
        states = buf[...].astype(BF16)
        st_ref[...] = states
        y_ref[...] = _bdot_raw(states, wc_ref[...]) + d_ref[...] * uv

    return _call(
        body, (u, wb, wc, lt, dskip), name=name, grid=(nb, nt),
        in_specs=[pl.BlockSpec((tt, HEAD), lambda b, t: (t, b)),
                  pl.BlockSpec((None, HEAD, 2 * hl), lambda b, t: (b, 0, 0)),
                  pl.BlockSpec((None, 2 * hl, HEAD), lambda b, t: (b, 0, 0)),
                  pl.BlockSpec((None, 16, 8, hl), lambda b, t: (b, 0, 0, 0)),
                  pl.BlockSpec((1, HEAD), lambda b, t: (0, b))],
        out_specs=[pl.BlockSpec((tt, HEAD), lambda b, t: (t, b)),
                   pl.BlockSpec((None, None, 1, 2 * hl), lambda b, t: (b, t, 0, 0)),
                   pl.BlockSpec((tt, 2 * hl), lambda b, t: (t, b))],
        out_shape=[jax.ShapeDtypeStruct((s, d), F32), jax.ShapeDtypeStruct((nb, nt, 1, 2 * hl), F32),
                   jax.ShapeDtypeStruct((s, nb * 2 * hl), BF16)],
        scratch_shapes=[pltpu.VMEM((tt, 2 * hl), F32), pltpu.VMEM((8, 2 * hl), F32)], comm=comm)


def _s5_bwd(u, dy, wb, wc, lt, dskip, cins, states, *, name, tt=512, comm=()):
    s, d = u.shape
    nb = d // HEAD
    tt = min(tt, s)
    nt = s // tt
    hl = S5_HALF

    def body(u_ref, dy_ref, wb_ref, wc_ref, lt_ref, d_ref, cin_ref, st_ref,
             du_ref, dwb_ref, dwc_ref, dd_ref, dlam_ref, abuf, acarry):
        @pl.when(pl.program_id(1) == 0)
        def _():
            acarry[...] = jnp.zeros_like(acarry)
            dwb_ref[...] = jnp.zeros_like(dwb_ref)
            dwc_ref[...] = jnp.zeros_like(dwc_ref)
            dd_ref[...] = jnp.zeros_like(dd_ref)
            dlam_ref[...] = jnp.zeros_like(dlam_ref)

        uv, dyv = u_ref[...], dy_ref[...]
        abuf[...] = _bdot_raw(dyv, wc_ref[...], NT)
        ar, ai = _s5_scan(abuf, lt_ref, acarry[0:1, 0:hl], acarry[0:1, hl:2 * hl], tt, True)
        acarry[0:1, 0:hl] = ar
        acarry[0:1, hl:2 * hl] = ai
        du_ref[...] = _bdot_raw(abuf[...], wb_ref[...], NT) + d_ref[...] * dyv
        dwb_ref[...] += _bdot_raw(uv, abuf[...], TN)
        dwc_ref[...] += _bdot_raw(st_ref[...], dyv, TN)
        dd_ref[...] += jnp.sum(dyv * uv, axis=0, keepdims=True)
        first = lax.broadcasted_iota(jnp.int32, (tt, hl), 0) == 0
        spr = jnp.where(first, cin_ref[:, 0:hl], pltpu.roll(st_ref[:, 0:hl].astype(F32), 1, 0))
        spi = jnp.where(first, cin_ref[:, hl:2 * hl], pltpu.roll(st_ref[:, hl:2 * hl].astype(F32), 1, 0))
        avr, avi = abuf[:, 0:hl], abuf[:, hl:2 * hl]
        dlam_ref[:, 0:hl] += jnp.sum(avr * spr + avi * spi, axis=0, keepdims=True)
        dlam_ref[:, hl:2 * hl] += jnp.sum(avi * spr - avr * spi, axis=0, keepdims=True)

    rev = pl.BlockSpec((tt, HEAD), lambda b, t: (nt - 1 - t, b))
    return _call(
        body, (u, dy, wb, wc, lt, dskip, cins, states), name=name, grid=(nb, nt),
        in_specs=[rev, rev,
                  pl.BlockSpec((None, HEAD, 2 * hl), lambda b, t: (b, 0, 0)),
                  pl.BlockSpec((None, 2 * hl, HEAD), lambda b, t: (b, 0, 0)),
                  pl.BlockSpec((None, 16, 8, hl), lambda b, t: (b, 0, 0, 0)),
                  pl.BlockSpec((1, HEAD), lambda b, t: (0, b)),
                  pl.BlockSpec((None, None, 1, 2 * hl), lambda b, t: (b, nt - 1 - t, 0, 0)),
                  pl.BlockSpec((tt, 2 * hl), lambda b, t: (nt - 1 - t, b))],
        out_specs=[rev,
                   pl.BlockSpec((None, HEAD, 2 * hl), lambda b, t: (b, 0, 0)),
                   pl.BlockSpec((None, 2 * hl, HEAD), lambda b, t: (b, 0, 0)),
                   pl.BlockSpec((1, HEAD), lambda b, t: (0, b)),
                   pl.BlockSpec((None, 1, 2 * hl), lambda b, t: (b, 0, 0))],
        out_shape=[jax.ShapeDtypeStruct((s, d), F32), jax.ShapeDtypeStruct(wb.shape, F32),
                   jax.ShapeDtypeStruct(wc.shape, F32), jax.ShapeDtypeStruct((1, d), F32),
                   jax.ShapeDtypeStruct((nb, 1, 2 * hl), F32)],
        scratch_shapes=[pltpu.VMEM((tt, 2 * hl), F32), pltpu.VMEM((8, 2 * hl), F32)],
        comm=comm)


def _s5_pack(lr, li, br, bi, c_re, c_im):
    g = lr.shape[0]
    nb = g // S5_GB
    eye = jnp.eye(S5_GB, dtype=F32)
    bm = jnp.stack([br, bi]).reshape(2, nb, S5_GB, S5_STATE, S5_GROUP)
    wb = jnp.einsum("rbgpc,gh->bgcrhp", bm, eye).reshape(nb, HEAD, 2 * S5_HALF)
    cm = jnp.stack([c_re, -c_im]).reshape(2, nb, S5_GB, S5_GROUP, S5_STATE)
    wc = jnp.einsum("rbgcp,gh->brgphc", cm, eye).reshape(nb, 2 * S5_HALF, HEAD)
    pw = [(lr, li)]
    for _ in range(7):
        pr, pi = pw[-1]
        pw.append((pr * lr - pi * li, pr * li + pi * lr))
    blk = lambda a: a.reshape(nb, 1, S5_HALF)
    rows = jnp.arange(8).reshape(1, 8, 1)
    tables = []
    for conj, keep, order in ((1.0, lambda n: rows >= n, range(8)), (-1.0, lambda n: rows < 8 - n, range(7, -1, -1))):
        for n in (1, 2, 4):
            tables += [jnp.where(keep(n), blk(pw[n - 1][0]), 0.0), jnp.where(keep(n), conj * blk(pw[n - 1][1]), 0.0)]
        tables += [jnp.concatenate([blk(pw[n][0]) for n in order], axis=1),
                   jnp.concatenate([conj * blk(pw[n][1]) for n in order], axis=1)]
    return wb, wc, jnp.stack(tables, axis=1)


def _s5_unpack(dwb, dwc, dlam):
    nb = dwb.shape[0]
    g = nb * S5_GB
    eye = jnp.eye(S5_GB, dtype=F32)
    db = jnp.einsum("bgcrhp,gh->rbgpc", dwb.reshape(nb, S5_GB, S5_GROUP, 2, S5_GB, S5_STATE), eye)
    db = db.reshape(2, g, S5_STATE * S5_GROUP)
    dc = jnp.einsum("brgphc,gh->rbgcp", dwc.reshape(nb, 2, S5_GB, S5_STATE, S5_GB, S5_GROUP), eye)
    dc = dc.reshape(2, g, S5_GROUP, S5_STATE)
    dl = dlam.reshape(nb, 2, S5_GB, S5_STATE).transpose(1, 0, 2, 3).reshape(2, g, S5_STATE)
    return dl[0], dl[1], db[0], db[1], dc[0], -dc[1]


def _peer(r):
    mx, my, mc = lax.axis_index("x"), lax.axis_index("y"), lax.axis_index("c")
    px = 1 - mx if r & 4 else mx
    py = 1 - my if r & 2 else my
    pc = 1 - mc if r & 1 else mc
    return (px, py, pc), 4 * px + 2 * py + pc


_COMM_SCRATCH = [pltpu.SemaphoreType.DMA((N_DEV - 1,)), pltpu.SemaphoreType.DMA((N_DEV - 1,)), pltpu.SemaphoreType.DMA]


class _AllToAll:
    def __init__(self, x, rows=None):
        self.x = x
        self.rows = rows
        shape = x.shape if rows is None else (x.shape[0], rows[1]) + tuple(x.shape[2:])
        self.out_shape = jax.ShapeDtypeStruct(shape, x.dtype)

    def _copies(self, x_ref, out_ref, send_sems, recv_sems, local_sem):
        def block(j):
            return x_ref.at[j] if self.rows is None else x_ref.at[j, pl.ds(self.rows[0], self.rows[1])]

        _, me = _peer(0)
        mine = pltpu.make_async_copy(block(me), out_ref.at[me], local_sem)
        sends, recvs = [], []
        for r in range(1, N_DEV):
            pos, idx = _peer(r)
            sems = dict(send_sem=send_sems.at[r - 1], recv_sem=recv_sems.at[r - 1], device_id=pos, device_id_type=MESH)
            sends.append(pltpu.make_async_remote_copy(src_ref=block(idx), dst_ref=out_ref.at[me], **sems))
            recvs.append(pltpu.make_async_remote_copy(src_ref=block(idx), dst_ref=out_ref.at[idx], **sems))
        return mine, sends, recvs

    def start(self, *refs):
        mine, sends, _ = self._copies(*refs)
        mine.start()
        for cp in sends:
            cp.start()

    def finish(self, *refs):
        mine, sends, recvs = self._copies(*refs)
        for cp in recvs:
            cp.wait_recv()
        for cp in sends:
            cp.wait_send()
        mine.wait()


class _Gather:
    def __init__(self, x):
        self.x = x
        self.out_shape = jax.ShapeDtypeStruct((N_DEV,) + tuple(x.shape), x.dtype)

    def _copies(self, x_ref, out_ref, send_sems, recv_sems, local_sem):
        mx, my, mc = lax.axis_index("x"), lax.axis_index("y"), lax.axis_index("c")
        me, sibling = (mx, my, mc), (mx, my, 1 - mc)
        chips = [(1 - mx, my), (mx, 1 - my), (1 - mx, 1 - my)]

        def slot(px, py, pc):
            return out_ref.at[4 * px + 2 * py + pc]

        def copy(k, block, to, src=None):
            return pltpu.make_async_remote_copy(
                src_ref=slot(*block) if src is None else src, dst_ref=slot(*block),
                send_sem=send_sems.at[k], recv_sem=recv_sems.at[k], device_id=to, device_id_type=MESH)

        return dict(
            mine=pltpu.make_async_copy(x_ref, slot(*me), local_sem),
            first=[copy(0, me, sibling, src=x_ref)] + [copy(1 + j, me, (*chip, mc), src=x_ref) for j, chip in enumerate(chips)],
            passed=[copy(4 + j, (*chip, mc), sibling) for j, chip in enumerate(chips)],
            over_ici=[copy(1 + j, (*chip, mc), me) for j, chip in enumerate(chips)],
            from_sibling=[copy(0, sibling, me)] + [copy(4 + j, (*chip, 1 - mc), me) for j, chip in enumerate(chips)])

    def start(self, *refs):
        cps = self._copies(*refs)
        cps["mine"].start()
        for cp in cps["first"]:
            cp.start()

    def finish(self, *refs):
        cps = self._copies(*refs)
        for arrived, onward in zip(cps["over_ici"], cps["passed"]):
            arrived.wait_recv()
            onward.start()
        for cp in cps["from_sibling"]:
            cp.wait_recv()
        for cp in cps["first"] + cps["passed"]:
            cp.wait_send()
        cps["mine"].wait()


def _call(body, args, *, name, grid, in_specs, out_specs, out_shape, scratch_shapes=(), comm=()):
    n_in, n_out, n_scr, nc = len(in_specs), len(out_shape), len(scratch_shapes), len(comm)

    def wrapped(*refs):
        ins, c_in = refs[:n_in], refs[n_in:n_in + nc]
        outs, c_out = refs[n_in + nc:n_in + nc + n_out], refs[n_in + nc + n_out:n_in + 2 * nc + n_out]
        scr = refs[n_in + 2 * nc + n_out:n_in + 2 * nc + n_out + n_scr]
        sems = refs[n_in + 2 * nc + n_out + n_scr:]
        ids = [pl.program_id(a) for a in range(len(grid))]
        if nc:
            @pl.when(functools.reduce(operator.and_, [i == 0 for i in ids]))
            def _():
                for k, op in enumerate(comm):
                    op.start(c_in[k], c_out[k], *sems[3 * k:3 * k + 3])

        body(*ins, *outs, *scr)
        if nc:
            @pl.when(functools.reduce(operator.and_, [i == g - 1 for i, g in zip(ids, grid)]))
            def _():
                for k, op in enumerate(comm):
                    op.finish(c_in[k], c_out[k], *sems[3 * k:3 * k + 3])

    any_spec = pl.BlockSpec(memory_space=pl.ANY)
    res = pl.pallas_call(
        wrapped, name=name, grid=grid,
        in_specs=list(in_specs) + [any_spec] * nc, out_specs=list(out_specs) + [any_spec] * nc,
        out_shape=list(out_shape) + [op.out_shape for op in comm],
        scratch_shapes=list(scratch_shapes) + list(_COMM_SCRATCH) * nc,
        compiler_params=_params(len(grid)),
    )(*args, *[op.x for op in comm])
    return list(res[:n_out]), list(res[n_out:])


def _comm_call(op, *, name):
    return _call(lambda: None, (), name=name, grid=(1,), in_specs=[], out_specs=[], out_shape=[], comm=(op,))[1][0]


def _adamw(w, parts, m, v, *, name, tr=128):
    nl, r, c = w.shape
    assert len(parts) == nl
    parts = [list(p) if isinstance(p, (list, tuple)) else [p] for p in parts]
    npart = parts[0][0].shape[0]
    tr = min(tr, r, *[pc.shape[1] for p in parts for pc in p])
    assert r % tr == 0 and all(pc.shape[1] % tr == 0 for p in parts for pc in p), (name, r, tr)
    pieces = []
    for l, p in enumerate(parts):
        first = 0
        for pc in p:
            pieces.append((l, first, pc.shape[1] // tr, pc))
            first += pc.shape[1] // tr
        assert first == r // tr, (name, l)

    def body(w_ref, m_ref, v_ref, *rest):
        p_refs, (g_ref, d_ref, mo_ref, vo_ref) = rest[:len(pieces)], rest[len(pieces):]
        layer, tile = pl.program_id(0), pl.program_id(1)
        for p_ref, (l, first, count, _) in zip(p_refs, pieces):
            @pl.when((layer == l) & (tile >= first) & (tile < first + count))
            def _():
                g = p_ref[0].astype(F32)
                for k in range(1, npart):
                    g = g + p_ref[k].astype(F32)
                m2 = ADAM_B1 * m_ref[...] + (1.0 - ADAM_B1) * g
                v2 = ADAM_B2 * v_ref[...] + (1.0 - ADAM_B2) * (g * g)
                m_hat = m2 / (1.0 - ADAM_B1 ** ADAM_STEP)
                v_hat = v2 / (1.0 - ADAM_B2 ** ADAM_STEP)
                g_ref[...] = g
                d_ref[...] = -ADAM_LR * (m_hat / (jnp.sqrt(v_hat) + ADAM_EPS) + ADAM_WD * w_ref[...])
                mo_ref[...] = m2
                vo_ref[...] = v2

    blk = pl.BlockSpec((None, tr, c), lambda l, i: (l, i, 0))

    def part_spec(l, first, count):
        return pl.BlockSpec((npart, tr, c), lambda ll, i: (0, jnp.where(ll == l, jnp.clip(i - first, 0, count - 1), 0), 0))

    return pl.pallas_call(
        body, name=name, grid=(nl, r // tr),
        in_specs=[blk, blk, blk] + [part_spec(l, first, count) for l, first, count, _ in pieces],
        out_specs=[blk] * 4, out_shape=[jax.ShapeDtypeStruct((nl, r, c), F32)] * 4,
        compiler_params=_params(2),
    )(w, m, v, *[pc for _, _, _, pc in pieces])


def _sum_parts(parts, *, name):
    npart = parts.shape[0]

    def body(p_ref, o_ref):
        g = p_ref[0]
        for k in range(1, npart):
            g = g + p_ref[k]
        o_ref[...] = g

    return pl.pallas_call(
        body, name=name, grid=(1,), in_specs=[_full_spec(parts)],
        out_specs=pl.BlockSpec(parts.shape[1:], lambda i: (0, 0)),
        out_shape=jax.ShapeDtypeStruct(parts.shape[1:], F32), compiler_params=_params(1),
    )(parts)


def _pack(arrs):
    blocks = []
    for a in arrs:
        flat = a.reshape(-1).astype(F32)
        blocks.append(jnp.pad(flat, (0, (-flat.shape[0]) % (8 * HEAD))).reshape(-1, HEAD))
    out = jnp.concatenate(blocks, axis=0)
    return jnp.pad(out, ((0, (-out.shape[0]) % HEAD), (0, 0)))


def _unpack(packed, shapes):
    out, off = [], 0
    for shp in shapes:
        size = math.prod(shp)
        rows = -(-size // (8 * HEAD)) * 8
        out.append(packed[off:off + rows].reshape(-1)[:size].reshape(shp))
        off += rows
    return out


def _add_epilogue(acc, res):
    return (acc + res,)


def _relu2_epilogue(acc):
    r = jnp.maximum(acc, 0.0)
    return acc, r * r


def _ple_epilogue(acc, gpre, h):
    return h + jax.nn.sigmoid(gpre) * acc, acc


def kernel(x, p, norm_mix, norm_mlp, norm_ple, w_in_e, w_out_e, hgrn_lb, g_norm_a, conv_w, a_log, dt_bias, g_norm_b, s5_a_re, s5_a_im, s5_b_re, s5_b_im, s5_c_re, s5_c_im, s5_d, s5_log_dt, w_glu, b_glu, w_out_o, w_up, w_down, w_ple_gate, w_ple_proj, final_norm, loss_target, m_norm_mix, m_norm_mlp, m_norm_ple, m_w_in_e, m_w_out_e, m_hgrn_lb, m_g_norm_a, m_conv_w, m_a_log, m_dt_bias, m_g_norm_b, m_s5_a_re, m_s5_a_im, m_s5_b_re, m_s5_b_im, m_s5_c_re, m_s5_c_im, m_s5_d, m_s5_log_dt, m_w_glu, m_b_glu, m_w_out_o, m_w_up, m_w_down, m_w_ple_gate, m_w_ple_proj, m_final_norm, v_norm_mix, v_norm_mlp, v_norm_ple, v_w_in_e, v_w_out_e, v_hgrn_lb, v_g_norm_a, v_conv_w, v_a_log, v_dt_bias, v_g_norm_b, v_s5_a_re, v_s5_a_im, v_s5_b_re, v_s5_b_im, v_s5_c_re, v_s5_c_im, v_s5_d, v_s5_log_dt, v_w_glu, v_b_glu, v_w_out_o, v_w_up, v_w_down, v_w_ple_gate, v_w_ple_proj, v_final_norm):
    args = dict(locals())
    s, d = x.shape[1], x.shape[2]
    aw = d // 2
    ha = hb = aw // HEAD
    main = 4 * d
    z_col = 2 * d + 3 * aw
    ff = w_up.shape[2] * N_DEV
    ple = p.shape[-1]
    groups = d // S5_GROUP
    me = 4 * lax.axis_index("x") + 2 * lax.axis_index("y") + lax.axis_index("c")
    x2, target = x[0], loss_target[0]
    row = lambda a, i: a[i:i + 1]

    def gather_of(w):
        return _Gather(w.astype(BF16))

    w_in = jnp.transpose(_comm_call(gather_of(w_in_e[0]), name="ag_w_in"), (1, 0, 2)).reshape(d, -1)
    w_main = w_in[:, :main]
    w_tail = jnp.pad(w_in[:, main:], ((0, 0), (0, HEAD - 2 * hb)))

    lb_rows = [row(hgrn_lb, 0), row(hgrn_lb, 1), row(hgrn_lb, 2)]
    (lb0,) = _small_call(_lb0_stage, lb_rows, name="f_lb0")
    hp = jnp.zeros((8, HEAD), F32).at[0, :hb].set(a_log[0]).at[1, :hb].set(dt_bias[0])
    expand = jnp.asarray(np.kron(np.eye(S5_STATE, dtype=np.float32), np.ones((1, S5_GROUP), np.float32)))
    prep_in = [s5_a_re[0], s5_a_im[0], s5_log_dt[0].reshape(groups, 1),
               s5_b_re[0].reshape(groups, -1), s5_b_im[0].reshape(groups, -1), expand]
    lr, li, br, bi = _small_call(_s5_prep_stage, prep_in, name="f_s5_prep")
    wb, wc, lt = _s5_pack(lr, li, br, bi, s5_c_re[0], s5_c_im[0])
    fnorm = final_norm.reshape(1, d)

    w_upg = []

    def block_fwd(h, l, w_dn=None):
        hn = _rows_call(_rms_stage, [h], [row(norm_mlp, l)], [BF16], name=f"f_norm_mlp{l}")
        up_args = dict(epilogue=_relu2_epilogue, out_dtypes=(F32, BF16), name=f"f_up{l}")
        if w_dn is None:
            (up, act), (dn8,) = _mm(hn, w_upg[l], comm=(gather_of(w_down[l]),), **up_args)
            w_dn = dn8.reshape(ff, d)
        else:
            up, act = _mm(hn, w_upg[l], **up_args)
        h2 = _mm(act, w_dn, extras=(h,), epilogue=_add_epilogue, name=f"f_down{l}")
        hq = _rows_call(_rms_stage, [h2], [row(norm_ple, l)], [BF16], name=f"f_norm_ple{l}")
        gpre = _mm(hq, w_pgg[l], name=f"f_ple_gate{l}")
        h3, pp = _mm(p[l, 0], w_ppg[l], extras=(gpre, h2), epilogue=_ple_epilogue, out_dtypes=(F32, F32),
                     name=f"f_ple_proj{l}")
        return h3, dict(h=h, hn=hn, up=up, act=act, h2=h2, hq=hq, gpre=gpre, pp=pp, w_dn=w_dn)

    hn0 = _rows_call(_rms_stage, [x2], [row(norm_mix, 0)], [BF16], name="f_norm_mix0")
    shard_shapes = [conv_w[0].shape, s5_d.shape, b_glu.shape]
    proj, (oe8, pg8, small) = _mm(hn0, w_main, name="f_proj", comm=(
        gather_of(w_out_e[0]), gather_of(w_ple_gate), _Gather(_pack([conv_w[0], s5_d, b_glu]))))
    w_oe = oe8.reshape(d, d)
    w_top, w_bot = w_oe[:aw], w_oe[aw:]
    w_pgg = jnp.transpose(pg8, (1, 0, 2, 3)).reshape(2, d, d)
    conv_g, s5d_g, bglu_g = zip(*[_unpack(small[j], shard_shapes) for j in range(N_DEV)])
    conv_full = jnp.concatenate(conv_g, axis=1)
    s5d_full = jnp.concatenate(s5d_g, axis=1)
    bglu_full = jnp.concatenate(bglu_g, axis=1)
    ab = _mm(hn0, w_tail, name="f_ab")
    (oa, st_a), (gl8, oo8) = _hgrn_fwd(proj, lb0, g_norm_a, heads=ha, name="f_hgrn",
                                       comm=(gather_of(w_glu[0]), gather_of(w_out_o[0])))
    w_gl, w_oo = gl8.reshape(d, d), oo8.reshape(d, d)
    qkv = _conv_fwd(proj, conv_full, col_off=2 * d, name="f_conv")
    slots_to_cols = lambda g8: jnp.transpose(g8, (1, 0, 2)).reshape(g8.shape[1], -1)
    (ob, st_b), (up8,) = _delta_fwd(qkv, ab, proj, hp, g_norm_b, heads=hb, z_off=z_col // HEAD, name="f_delta",
                                    comm=(gather_of(w_up[0]),))
    w_upg.append(slots_to_cols(up8))
    h1, (pp8,) = _mm(oa, w_top, extras=(x2,), epilogue=_add_epilogue, name="f_out_a", comm=(gather_of(w_ple_proj),))
    w_ppg = jnp.transpose(pp8, (1, 2, 0, 3)).reshape(2, ple, d)
    h1 = _mm(ob, w_bot, extras=(h1,), epilogue=_add_epilogue, name="f_out_b")
    h3, sv0 = block_fwd(h1, 0)

    u = _rows_call(_rms_stage, [h3], [row(norm_mix, 1)], [F32], name="f_norm_mix1")
    (y, cins, s5_states), (up8, dn8) = _s5_fwd(u, wb, wc, lt, s5d_full, name="f_s5",
                                               comm=(gather_of(w_up[1]), gather_of(w_down[1])))
    w_upg.append(slots_to_cols(up8))
    act_g = _rows_call(_gelu_stage, [y], [], [BF16], name="f_gelu")
    gl_raw = _mm(act_g, w_gl, name="f_glu")
    glu = _rows_call(_glu_stage, [y, gl_raw], [bglu_full], [BF16], name="f_glu_gate")
    h4 = _mm(glu, w_oo, extras=(h3,), epilogue=_add_epilogue, name="f_out_o")
    h6, sv1 = block_fwd(h4, 1, dn8.reshape(ff, d))
    dh, d_fnorm, loss8 = _loss_call(h6, fnorm, target, name="loss")
    loss = lax.psum(loss8[0, 0], ("x", "y", "c"))

    dshard, ffs, cols = d // N_DEV, ff // N_DEV, w_in_e.shape[2]
    rows8 = lambda g: _AllToAll(g.reshape(N_DEV, -1, g.shape[-1]))
    cols8 = lambda g: _AllToAll(jnp.transpose(g.reshape(g.shape[0], N_DEV, -1), (1, 0, 2)))

    def halves(g8):
        r = g8.shape[1] // 2
        return _AllToAll(g8, rows=(0, r)), _AllToAll(g8, rows=(r, r))

    def block_bwd(dh3, l, sv, carried=(), carried_wup=(), carried_up=()):
        (dgpre, dpp), _ = _rows_vjp(_ple_stage, [sv["h2"], sv["gpre"], sv["pp"]], [], [dh3],
                                    row_grads={1: BF16, 2: BF16}, name=f"b_ple{l}")
        g_pp = _mm(p[l, 0], dpp, ta=True, out_dtypes=(BF16,), name=f"b_w_ple_proj{l}")
        g_pg = _mm(sv["hq"], dgpre, ta=True, out_dtypes=(BF16,), name=f"b_w_ple_gate{l}")
        dhq = _mm(dgpre, w_pgg[l], tb=True, name=f"b_ple_gate{l}")
        (dh2,), (g_nple,) = _rows_vjp(_rms_stage, [sv["h2"]], [row(norm_ple, l)], [dhq], row_grads={0: F32},
                                      adds={0: dh3}, name=f"b_norm_ple{l}")
        dup, (r_pg, r_pp) = _mm(dh2, sv["w_dn"], tb=True, extras=(sv["up"],), epilogue=_relu2_grad_epilogue,
                                out_dtypes=(BF16,), name=f"b_down{l}", comm=(rows8(g_pg), cols8(g_pp)))
        g_dn = _mm(sv["act"], dh2, ta=True, out_dtypes=(BF16,), name=f"b_w_down{l}", comm=carried)
        g_dn, r_carried = g_dn if carried else (g_dn, [])
        g_up = _mm(sv["hn"], dup, ta=True, out_dtypes=(BF16,), tn=ffs, out_slots=True, name=f"b_w_up{l}",
                   comm=carried_wup)
        g_up, r_carried_wup = g_up if carried_wup else (g_up, [])
        dhn = _mm(dup, w_upg[l], tb=True, name=f"b_up{l}", comm=carried_up)
        dhn, r_carried_up = dhn if carried_up else (dhn, [])
        (dh0,), (g_nmlp,) = _rows_vjp(_rms_stage, [sv["h"]], [row(norm_mlp, l)], [dhn], row_grads={0: F32},
                                      adds={0: dh2}, name=f"b_norm_mlp{l}")
        return dh0, dict(w_ple_proj=r_pp, w_ple_gate=r_pg, norm_ple=g_nple, w_down=g_dn, w_up=g_up, norm_mlp=g_nmlp,
                         carried=r_carried, carried_wup=r_carried_wup, carried_up=r_carried_up)

    dh4, gb1 = block_bwd(dh, 1, sv1)
    up1_a, up1_b = halves(gb1["w_up"])
    dglu = _mm(dh4, w_oo, tb=True, name="b_out_o")
    g_oo = _mm(glu, dh4, ta=True, out_dtypes=(BF16,), name="b_w_out_o")
    (dy1, dgl), (g_bglu,) = _rows_vjp(_glu_stage, [y, gl_raw], [bglu_full], [dglu], row_grads={0: F32, 1: BF16},
                                      name="b_glu_gate")
    g_gl = _mm(act_g, dgl, ta=True, out_dtypes=(BF16,), name="b_w_glu")
    dact = _mm(dgl, w_gl, tb=True, name="b_glu")
    (dy,), _ = _rows_vjp(_gelu_stage, [y], [], [dact], row_grads={0: F32}, adds={0: dy1}, name="b_gelu")
    (du, dwb, dwc, g_s5d, dlam), (r_dn1, r_up1_a) = _s5_bwd(u, dy, wb, wc, lt, s5d_full, cins, s5_states, name="b_s5",
                                                           comm=(rows8(gb1["w_down"]), up1_a))
    (dh3,), (g_nmix1,) = _rows_vjp(_rms_stage, [h3], [row(norm_mix, 1)], [du], row_grads={0: F32}, adds={0: dh4},
                                   name="b_norm_mix1")
    dlr, dli, dbr, dbi, g_cre, g_cim = _s5_unpack(dwb, dwc, dlam)
    g_are, g_aim, g_ldt, g_bre, g_bim, _ = _small_vjp(_s5_prep_stage, prep_in, [dlr, dli, dbr, dbi], name="b_s5_prep")

    early_grads = dict(
        s5_a_re=g_are[None], s5_a_im=g_aim[None], s5_b_re=g_bre.reshape(s5_b_re.shape),
        s5_b_im=g_bim.reshape(s5_b_im.shape), s5_c_re=g_cre[None], s5_c_im=g_cim[None],
        s5_log_dt=g_ldt.reshape(1, groups), final_norm=d_fnorm.reshape(d), s5_d=g_s5d, b_glu=g_bglu)
    dh1, gb0 = block_bwd(dh3, 0, sv0, (rows8(g_oo), rows8(g_gl)), (up1_b,),
                         (_Gather(_pack(list(early_grads.values()))),))
    r_oo, r_gl = gb0["carried"]
    r_up1 = [r_up1_a, gb0["carried_wup"][0]]
    (early_parts,) = gb0["carried_up"]
    dn0_a, dn0_b = halves(gb0["w_down"].reshape(N_DEV, ffs, d))
    up0_a, up0_b = halves(gb0["w_up"])
    doa = _mm(dh1, w_top, tb=True, name="b_out_a")
    dob = _mm(dh1, w_bot, tb=True, name="b_out_b")
    g_oe = jnp.concatenate([_mm(oa, dh1, ta=True, out_dtypes=(BF16,), name="b_w_out_a"),
                            _mm(ob, dh1, ta=True, out_dtypes=(BF16,), name="b_w_out_b")], axis=0)
    (dq, df, di, dg, dlb, g_gna), (r_dn0_a,) = _hgrn_bwd(proj, lb0, g_norm_a, st_a, doa, heads=ha, name="b_hgrn",
                                                        comm=(dn0_a,))
    (dqb, dkb, dvb, dab, dz, dhp, g_gnb), (r_dn0_b, r_up0_a) = _delta_bwd(
        qkv, ab, proj, hp, g_norm_b, st_b, dob, heads=hb, z_off=z_col // HEAD, name="b_delta", comm=(dn0_b, up0_a))
    (dqkv, g_conv), (r_oe,) = _conv_bwd(proj, conv_full, jnp.concatenate([dqb, dkb, dvb], axis=1), col_off=2 * d,
                                        name="b_conv", comm=(rows8(g_oe),))
    dproj = jnp.concatenate([dq, df, di, dg, dqkv, dz], axis=1)
    g_main, (r_up0_b,) = _mm(hn0, dproj, ta=True, out_dtypes=(BF16,), name="b_w_proj", comm=(up0_b,))
    r_dn0, r_up0 = [r_dn0_a, r_dn0_b], [r_up0_a, r_up0_b]
    g_tail = _mm(hn0, dab, ta=True, out_dtypes=(BF16,), name="b_w_ab")
    dhn0, (r_in,) = _mm(dproj, w_main, tb=True, name="b_proj",
                        comm=(cols8(jnp.concatenate([g_main, g_tail[:, :2 * hb]], axis=1)),))
    dhn0 = _mm(dab, w_tail, tb=True, extras=(dhn0,), epilogue=_add_epilogue, name="b_ab")
    (dx,), (g_nmix0,) = _rows_vjp(_rms_stage, [x2], [row(norm_mix, 0)], [dhn0], row_grads={0: F32}, adds={0: dh1},
                                  name="b_norm_mix0")
    g_lb = jnp.concatenate(_small_vjp(_lb0_stage, lb_rows, [dlb], name="b_lb0"), axis=0)

    late_grads = dict(
        norm_mix=jnp.concatenate([g_nmix0, g_nmix1], axis=0),
        norm_mlp=jnp.concatenate([gb0["norm_mlp"], gb1["norm_mlp"]], axis=0),
        norm_ple=jnp.concatenate([gb0["norm_ple"], gb1["norm_ple"]], axis=0),
        hgrn_lb=g_lb, g_norm_a=g_gna, a_log=dhp[0:1, :hb], dt_bias=dhp[1:2, :hb], g_norm_b=g_gnb, conv_w=g_conv)
    rep_names = ["norm_mix", "norm_mlp", "norm_ple", "hgrn_lb", "g_norm_a", "a_log", "dt_bias", "g_norm_b", "s5_a_re",
                 "s5_a_im", "s5_b_re", "s5_b_im", "s5_c_re", "s5_c_im", "s5_log_dt", "final_norm"]
    late_parts = _comm_call(_Gather(_pack(list(late_grads.values()))), name="ag_small_grads")
    summed = {}
    for tag, grads, parts in (("early", early_grads, early_parts), ("late", late_grads, late_parts)):
        sums = _unpack(_sum_parts(parts, name=f"sum_small_grads_{tag}"), [g.shape for g in grads.values()])
        summed.update(zip(grads, sums))
    cw = conv_w.shape[2]
    dshard = d // N_DEV
    shard_g = dict(conv_w=lax.dynamic_slice(summed["conv_w"], (0, me * cw), (CONV_WIDTH, cw))[None],
                   s5_d=lax.dynamic_slice(summed["s5_d"], (0, me * dshard), (1, dshard)),
                   b_glu=lax.dynamic_slice(summed["b_glu"], (0, me * dshard), (1, dshard)))
    small_names = rep_names + ["conv_w", "s5_d", "b_glu"]
    g_small = [summed[k] if k in rep_names else shard_g[k] for k in small_names]
    shapes = [args[k].shape for k in small_names]
    sm_out = _adamw(_pack([args[k] for k in small_names])[None], [_pack(g_small)[None]],
                    _pack([args["m_" + k] for k in small_names])[None], _pack([args["v_" + k] for k in small_names])[None],
                    name="adamw_small")
    sm_out = [dict(zip(small_names, _unpack(o[0], shapes))) for o in sm_out]

    received = dict(w_in_e=[r_in], w_out_e=[r_oe], w_glu=[r_gl], w_out_o=[r_oo], w_up=[r_up0, r_up1],
                    w_down=[r_dn0, r_dn1], w_ple_gate=[gb0["w_ple_gate"], gb1["w_ple_gate"]],
                    w_ple_proj=[gb0["w_ple_proj"], gb1["w_ple_proj"]])
    big_out = {k: _adamw(args[k], layers, args["m_" + k], args["v_" + k], name="adamw_" + k)
               for k, layers in received.items()}

    names = ["norm_mix", "norm_mlp", "norm_ple", "w_in_e", "w_out_e", "hgrn_lb", "g_norm_a", "conv_w", "a_log", "dt_bias",
             "g_norm_b", "s5_a_re", "s5_a_im", "s5_b_re", "s5_b_im", "s5_c_re", "s5_c_im", "s5_d", "s5_log_dt", "w_glu",
             "b_glu", "w_out_o", "w_up", "w_down", "w_ple_gate", "w_ple_proj", "final_norm"]
    result = [loss, dx[None]]
    for j in range(4):
        result += [big_out[k][j] if k in big_out else sm_out[j][k] for k in names]
    return tuple(result)
```

```python
import functools
import math
import operator

import numpy as np
import jax
import jax.numpy as jnp
from jax import lax
from jax.experimental import pallas as pl
from jax.experimental.pallas import tpu as pltpu

F32 = jnp.float32
BF16 = jnp.bfloat16
MM_DTYPE = BF16
HI = lax.Precision.HIGHEST
MESH = pl.DeviceIdType.MESH

NORM_EPS = 1e-6
CHUNK = 64
HEAD = 128
CONV_WIDTH = 4
S5_GROUP = 16
S5_STATE = 64
S5_GB = 8
S5_HALF = S5_GB * S5_STATE
N_DEV = 8
HEADS_PER_STEP = 8
HGRN_SUB = 16
ADAM_LR, ADAM_B1, ADAM_B2, ADAM_EPS, ADAM_WD, ADAM_STEP = 0.001, 0.9, 0.999, 1e-08, 0.01, 10
VMEM_LIMIT = 56 * 1024 * 1024

NN = (((1,), (0,)), ((), ()))
NT = (((1,), (1,)), ((), ()))
TN = (((0,), (0,)), ((), ()))


def _dot(a, b, dn=NN):
    return lax.dot_general(a, b, dn, precision=HI, preferred_element_type=F32)


def _hdot(a, b, dn=NN):
    return lax.dot_general(a, b, dn, precision=lax.Precision.HIGH, preferred_element_type=F32)


def _bdot_raw(a, b, dn=NN):
    return lax.dot_general(a.astype(BF16), b.astype(BF16), dn, preferred_element_type=F32)


@functools.partial(jax.custom_vjp, nondiff_argnums=(2,))
def _bdot(a, b, dn):
    return _bdot_raw(a, b, dn)


def _bdot_fwd(a, b, dn):
    return _bdot_raw(a, b, dn), (a, b)


def _bdot_bwd(dn, res, g):
    a, b = res
    if dn == NN:
        return _bdot_raw(g, b, NT), _bdot_raw(a, g, TN)
    if dn == NT:
        return _bdot_raw(g, b, NN), _bdot_raw(g, a, TN)
    assert dn == TN
    return _bdot_raw(b, g, NT), _bdot_raw(a, g, NN)


_bdot.defvjp(_bdot_fwd, _bdot_bwd)


def _per_head(f):
    def g(*args, **kw):
        n = [len(a.vals) for a in args if isinstance(a, _Heads)]
        if not n:
            return f(*args, **kw)
        return _Heads([f(*[a.vals[j] if isinstance(a, _Heads) else a for a in args], **kw) for j in range(n[0])])
    return g


class _Heads:
    def __init__(self, vals):
        self.vals = list(vals)

    def __add__(self, o):
        return _per_head(operator.add)(self, o)

    def __radd__(self, o):
        return _per_head(operator.add)(o, self)

    def __sub__(self, o):
        return _per_head(operator.sub)(self, o)

    def __rsub__(self, o):
        return _per_head(operator.sub)(o, self)

    def __mul__(self, o):
        return _per_head(operator.mul)(self, o)

    def __rmul__(self, o):
        return _per_head(operator.mul)(o, self)

    def __neg__(self):
        return _per_head(operator.neg)(self)


_exp, _log, _where, _sum, _mean = (_per_head(f) for f in (jnp.exp, jnp.log, jnp.where, jnp.sum, jnp.mean))
_sigmoid, _rsqrt, _equal = _per_head(jax.nn.sigmoid), _per_head(lax.rsqrt), _per_head(operator.eq)
_hdot_h, _bdot_h = _per_head(_hdot), _per_head(_bdot)
_rows = _per_head(lambda a, lo, n: a[lo:lo + n, :])
_row_concat = _per_head(lambda *xs: jnp.concatenate(xs, axis=0))


def _params(n_axes):
    return pltpu.CompilerParams(dimension_semantics=("arbitrary",) * n_axes, vmem_limit_bytes=VMEM_LIMIT)


def _full_spec(a):
    nd = a.ndim
    return pl.BlockSpec(a.shape, lambda *_: (0,) * nd)


def _mm(a, b, *, name, ta=False, tb=False, extras=(), epilogue=None, out_dtypes=(F32,), tm=1024, tn=1024, tk=2048,
        out_slots=False, comm=()):
    m = a.shape[1] if ta else a.shape[0]
    k = a.shape[0] if ta else a.shape[1]
    n = b.shape[0] if tb else b.shape[1]
    assert k == (b.shape[1] if tb else b.shape[0]), (name, a.shape, b.shape)
    tm, tn, tk = min(tm, m), min(tn, n), min(tk, k)
    assert m % tm == 0 and n % tn == 0 and k % tk == 0, (name, m, n, k)
    nk = k // tk
    n_ex, n_out = len(extras), len(out_dtypes)
    dn = (((0 if ta else 1,), (1 if tb else 0,)), ((), ()))

    def body(a_ref, b_ref, *rest):
        ex_refs, out_refs = rest[:n_ex], rest[n_ex:n_ex + n_out]
        part = lax.dot_general(a_ref[...].astype(MM_DTYPE), b_ref[...].astype(MM_DTYPE), dn, preferred_element_type=F32)

        def finish(acc):
            outs = epilogue(acc, *[r[...] for r in ex_refs]) if epilogue is not None else (acc,)
            for o_ref, o in zip(out_refs, outs):
                o_ref[...] = o.astype(o_ref.dtype)

        if nk == 1:
            finish(part)
            return
        acc_ref = rest[-1]
        kk = pl.program_id(2)

        @pl.when(kk == 0)
        def _():
            acc_ref[...] = part

        @pl.when((kk > 0) & (kk < nk - 1))
        def _():
            acc_ref[...] += part

        @pl.when(kk == nk - 1)
        def _():
            finish(acc_ref[...] + part)

    a_spec = pl.BlockSpec((tk, tm), lambda i, j, q: (q, i)) if ta else pl.BlockSpec((tm, tk), lambda i, j, q: (i, q))
    b_spec = pl.BlockSpec((tn, tk), lambda i, j, q: (j, q)) if tb else pl.BlockSpec((tk, tn), lambda i, j, q: (q, j))
    ex_specs = []
    for e in extras:
        if e.shape[0] == 1 and m != 1:
            ex_specs.append(pl.BlockSpec((1, tn), lambda i, j, q: (0, j)))
        else:
            ex_specs.append(pl.BlockSpec((tm, tn), lambda i, j, q: (i, j)))
    if out_slots:
        out_spec, out_dims = pl.BlockSpec((None, tm, tn), lambda i, j, q: (j, i, 0)), (n // tn, m, tn)
    else:
        out_spec, out_dims = pl.BlockSpec((tm, tn), lambda i, j, q: (i, j)), (m, n)
    outs, exchanged = _call(
        body, (a, b, *extras), name=name, grid=(m // tm, n // tn, nk),
        in_specs=[a_spec, b_spec] + ex_specs,
        out_specs=[out_spec for _ in out_dtypes],
        out_shape=[jax.ShapeDtypeStruct(out_dims, dt) for dt in out_dtypes],
        scratch_shapes=[pltpu.VMEM((tm, tn), F32)] if nk > 1 else [], comm=comm)
    outs = outs[0] if n_out == 1 else tuple(outs)
    return (outs, exchanged) if comm else outs


def _rows_call(fn, rows, consts, out_dtypes, *, name, tr=256):
    s = rows[0].shape[0]
    tr = min(tr, s)
    nr, nc = len(rows), len(consts)
    widths = [o.shape[1] for o in jax.eval_shape(
        fn, *[jax.ShapeDtypeStruct((tr, r.shape[1]), F32) for r in rows],
        *[jax.ShapeDtypeStruct(c.shape, F32) for c in consts])]

    def body(*refs):
        rv = [r[...].astype(F32) for r in refs[:nr]]
        cv = [c[...] for c in refs[nr:nr + nc]]
        for o_ref, o in zip(refs[nr + nc:], fn(*rv, *cv)):
            o_ref[...] = o.astype(o_ref.dtype)

    outs = pl.pallas_call(
        body, name=name, grid=(s // tr,),
        in_specs=[pl.BlockSpec((tr, r.shape[1]), lambda i: (i, 0)) for r in rows] + [_full_spec(c) for c in consts],
        out_specs=[pl.BlockSpec((tr, w), lambda i: (i, 0)) for w in widths],
        out_shape=[jax.ShapeDtypeStruct((s, w), dt) for w, dt in zip(widths, out_dtypes)],
        compiler_params=_params(1),
    )(*rows, *consts)
    return outs[0] if len(outs) == 1 else tuple(outs)


def _rows_vjp(fn, rows, consts, cots, *, name, row_grads, adds=None, tr=256):
    adds = adds or {}
    s = rows[0].shape[0]
    tr = min(tr, s)
    nr, nc, nt = len(rows), len(consts), len(cots)
    rg = sorted(row_grads)
    ad = sorted(adds)

    def body(*refs):
        rv = [r[...].astype(F32) for r in refs[:nr]]
        cv = [c[...] for c in refs[nr:nr + nc]]
        ct = [c[...].astype(F32) for c in refs[nr + nc:nr + nc + nt]]
        av = {i: r[...].astype(F32) for i, r in zip(ad, refs[nr + nc + nt:nr + nc + nt + len(ad)])}
        out_refs = refs[nr + nc + nt + len(ad):]
        _, vjp = jax.vjp(fn, *rv, *cv)
        grads = vjp(tuple(ct))
        for o_ref, i in zip(out_refs[:len(rg)], rg):
            g = grads[i]
            if i in av:
                g = g + av[i]
            o_ref[...] = g.astype(o_ref.dtype)

        @pl.when(pl.program_id(0) == 0)
        def _():
            for o_ref in out_refs[len(rg):]:
                o_ref[...] = jnp.zeros_like(o_ref)

        for o_ref, g in zip(out_refs[len(rg):], grads[nr:]):
            o_ref[...] += g

    row_spec = lambda a: pl.BlockSpec((tr, a.shape[1]), lambda i: (i, 0))
    outs = pl.pallas_call(
        body, name=name, grid=(s // tr,),
        in_specs=[row_spec(r) for r in rows] + [_full_spec(c) for c in consts] + [row_spec(c) for c in cots]
        + [row_spec(adds[i]) for i in ad],
        out_specs=[row_spec(rows[i]) for i in rg] + [_full_spec(c) for c in consts],
        out_shape=[jax.ShapeDtypeStruct(rows[i].shape, row_grads[i]) for i in rg]
        + [jax.ShapeDtypeStruct(c.shape, F32) for c in consts],
        compiler_params=_params(1),
    )(*rows, *consts, *cots, *[adds[i] for i in ad])
    return list(outs[:len(rg)]), list(outs[len(rg):])


def _small_call(fn, ins, *, name):
    shapes = jax.eval_shape(fn, *[jax.ShapeDtypeStruct(a.shape, F32) for a in ins])

    def body(*refs):
        for o_ref, o in zip(refs[len(ins):], fn(*[r[...] for r in refs[:len(ins)]])):
            o_ref[...] = o

    return pl.pallas_call(
        body, name=name, in_specs=[_full_spec(a) for a in ins],
        out_specs=[pl.BlockSpec(o.shape, functools.partial(lambda nd, *_: (0,) * nd, len(o.shape))) for o in shapes],
        out_shape=[jax.ShapeDtypeStruct(o.shape, F32) for o in shapes], grid=(1,),
        compiler_params=_params(1),
    )(*ins)


def _small_vjp(fn, ins, cots, *, name):
    def body(*refs):
        vals = [r[...] for r in refs[:len(ins)]]
        ct = [r[...] for r in refs[len(ins):len(ins) + len(cots)]]
        _, vjp = jax.vjp(fn, *vals)
        for o_ref, g in zip(refs[len(ins) + len(cots):], vjp(tuple(ct))):
            o_ref[...] = g

    return pl.pallas_call(
        body, name=name, in_specs=[_full_spec(a) for a in ins] + [_full_spec(c) for c in cots],
        out_specs=[_full_spec(a) for a in ins],
        out_shape=[jax.ShapeDtypeStruct(a.shape, F32) for a in ins], grid=(1,),
        compiler_params=_params(1),
    )(*ins, *cots)


def _rms(x, g):
    return x * _rsqrt(_mean(x * x, axis=-1, keepdims=True) + NORM_EPS) * g


def _rms_stage(x, g):
    return (_rms(x, g),)


def _silu(x):
    return x * _sigmoid(x)


def _softplus(x):
    return jnp.maximum(x, 0.0) + jnp.log1p(jnp.exp(-jnp.abs(x)))


def _gelu(x):
    return jax.nn.gelu(x, approximate=True)


def _gelu_stage(y):
    return (_gelu(y),)


def _glu_stage(y, gl_raw, b):
    return (_gelu(y) * jax.nn.sigmoid(gl_raw + b),)


def _ple_stage(h, gpre, pp):
    return (h + jax.nn.sigmoid(gpre) * pp,)


def _relu2_grad_epilogue(acc, up):
    return (acc * (2.0 * jnp.maximum(up, 0.0)),)


def _lb0_stage(x0, x1, x2):
    mx = jnp.maximum(jnp.maximum(x0, x1), x2)
    e0, e1, e2 = jnp.exp(x0 - mx), jnp.exp(x1 - mx), jnp.exp(x2 - mx)
    return (e0 / (e0 + e1 + e2),)


def _s5_prep_stage(a_re, a_im, log_dt, b_re, b_im, expand):
    step = jnp.exp(log_dt)
    mag = jnp.exp(a_re * step)
    lr = mag * jnp.cos(a_im * step)
    li = mag * jnp.sin(a_im * step)
    den = a_re * a_re + a_im * a_im
    cr = ((lr - 1.0) * a_re + li * a_im) / den
    ci = (li * a_re - (lr - 1.0) * a_im) / den
    cr_e, ci_e = _dot(cr, expand), _dot(ci, expand)
    return lr, li, cr_e * b_re - ci_e * b_im, cr_e * b_im + ci_e * b_re


def _loss_call(h, g, target, *, name, tr=256):
    s, d = h.shape
    tr = min(tr, s)

    def loss_fn(hv, gv, tv):
        err = _rms(hv, gv) - tv
        return 0.5 * jnp.sum(jnp.mean(err * err, axis=-1))

    def body(h_ref, g_ref, t_ref, dh_ref, dg_ref, loss_ref):
        val, (dh, dg) = jax.value_and_grad(loss_fn, argnums=(0, 1))(h_ref[...], g_ref[...], t_ref[...])
        dh_ref[...] = dh

        @pl.when(pl.program_id(0) == 0)
        def _():
            dg_ref[...] = jnp.zeros_like(dg_ref)
            loss_ref[...] = jnp.zeros_like(loss_ref)

        dg_ref[...] += dg
        loss_ref[...] += jnp.full(loss_ref.shape, val, F32)

    row = pl.BlockSpec((tr, d), lambda i: (i, 0))
    return pl.pallas_call(
        body, name=name, grid=(s // tr,),
        in_specs=[row, _full_spec(g), row],
        out_specs=[row, _full_spec(g), pl.BlockSpec((8, 128), lambda i: (0, 0))],
        out_shape=[jax.ShapeDtypeStruct((s, d), F32), jax.ShapeDtypeStruct(g.shape, F32),
                   jax.ShapeDtypeStruct((8, 128), F32)],
        compiler_params=_params(1),
    )(h, g, target)


def _hgrn_chunk(q, fp, iv, gp, lb, gn, st_t):
    c = CHUNK
    row = lax.broadcasted_iota(jnp.int32, (c, c), 0)
    col = lax.broadcasted_iota(jnp.int32, (c, c), 1)
    causal = row >= col
    fg = lb + (1.0 - lb) * _sigmoid(fp)
    k = 1.0 - fg
    lf = _log(fg)
    cum = _hdot_h(causal.astype(F32), lf, NN)
    cend = _sum(lf, axis=0, keepdims=True)
    shift = HGRN_SUB.bit_length() - 1
    ref = _hdot_h(((row >> shift) > (col >> shift)).astype(F32), lf, NN)
    q_dec = q * _exp(cum - ref)
    key_row = lax.broadcasted_iota(jnp.int32, (c, 1), 0)
    blocks = []
    for lo in range(0, c, HGRN_SUB):
        live = key_row < lo + HGRN_SUB
        k_dec = _where(live, k * _exp(_where(live, _rows(ref, lo, 1) - cum, 0.0)), 0.0)
        blocks.append(_hdot_h(_rows(q_dec, lo, HGRN_SUB), k_dec, NT))
    scores = _where(causal, _row_concat(*blocks), 0.0)
    out = _bdot_h(scores, iv, NN) + _bdot_h(q * _exp(cum), st_t, NT)
    st_new = st_t * _exp(cend) + _bdot_h(iv, k * _exp(cend - cum), TN)
    res = _rms(out, gn) * _silu(gp)
    return res, st_new


def _hgrn_heads(qs, fs, ivs, gs, lbs, gn, sts):
    res, st_new = _hgrn_chunk(_Heads(qs), _Heads(fs), _Heads(ivs), _Heads(gs), _Heads(lbs), gn, _Heads(sts))
    return res.vals, st_new.vals


def _lanes(j):
    return slice(j * HEAD, (j + 1) * HEAD)


def _hgrn_fwd(proj, lb, gn, *, heads, name, comm=()):
    s = proj.shape[0]
    n = s // CHUNK
    hpb = min(HEADS_PER_STEP, heads)
    assert heads % hpb == 0

    def body(q_ref, f_ref, i_ref, g_ref, lb_ref, gn_ref, o_ref, st_ref, state):
        @pl.when(pl.program_id(1) == 0)
        def _():
            state[...] = jnp.zeros_like(state)

        gnv = gn_ref[...]
        loaded = [(q_ref[:, _lanes(j)], f_ref[:, _lanes(j)], i_ref[:, _lanes(j)], g_ref[:, _lanes(j)],
                   lb_ref[:, _lanes(j)], state[j]) for j in range(hpb)]
        qs, fs, ivs, gs, lbs, sts = (list(t) for t in zip(*loaded))
        res, st_new = _hgrn_heads(qs, fs, ivs, gs, lbs, gnv, sts)
        for j in range(hpb):
            st_ref[j] = sts[j]
            o_ref[:, _lanes(j)] = res[j].astype(o_ref.dtype)
            state[j] = st_new[j]

    wide = hpb * HEAD
    blk = lambda off: pl.BlockSpec((CHUNK, wide), lambda h, c: (c, off // hpb + h))
    return _call(
        body, (proj, proj, proj, proj, lb, gn), name=name, grid=(heads // hpb, n),
        in_specs=[blk(0), blk(heads), blk(2 * heads), blk(3 * heads),
                  pl.BlockSpec((1, wide), lambda h, c: (0, h)), pl.BlockSpec((1, HEAD), lambda h, c: (0, 0))],
        out_specs=[pl.BlockSpec((CHUNK, wide), lambda h, c: (c, h)),
                   pl.BlockSpec((hpb, None, HEAD, HEAD), lambda h, c: (h, c, 0, 0))],
        out_shape=[jax.ShapeDtypeStruct((s, heads * HEAD), BF16), jax.ShapeDtypeStruct((heads, n, HEAD, HEAD), F32)],
        scratch_shapes=[pltpu.VMEM((hpb, HEAD, HEAD), F32)], comm=comm)


def _hgrn_bwd(proj, lb, gn, states, d_out, *, heads, name, comm=()):
    s = proj.shape[0]
    n = s // CHUNK
    hpb = min(HEADS_PER_STEP, heads)

    def body(q_ref, f_ref, i_ref, g_ref, lb_ref, gn_ref, st_ref, do_ref,
             dq_ref, df_ref, di_ref, dg_ref, dlb_ref, dgn_ref, dstate):
        h, c = pl.program_id(0), pl.program_id(1)

        @pl.when(c == 0)
        def _():
            dstate[...] = jnp.zeros_like(dstate)
            dlb_ref[...] = jnp.zeros_like(dlb_ref)

        @pl.when((c == 0) & (h == 0))
        def _():
            dgn_ref[...] = jnp.zeros_like(dgn_ref)

        gnv = gn_ref[...]
        loaded = [(q_ref[:, _lanes(j)], f_ref[:, _lanes(j)], i_ref[:, _lanes(j)], g_ref[:, _lanes(j)],
                   lb_ref[:, _lanes(j)], st_ref[j], do_ref[:, _lanes(j)].astype(F32), dstate[j]) for j in range(hpb)]
        qs, fs, ivs, gs, lbs, sts, dos, dss = (list(t) for t in zip(*loaded))
        _, vjp = jax.vjp(_hgrn_heads, qs, fs, ivs, gs, lbs, gnv, sts)
        dqs, dfs, dis, dgs, dlbs, dgn_sum, dsts = vjp((dos, dss))
        for j in range(hpb):
            ln = _lanes(j)
            dq_ref[:, ln] = dqs[j].astype(dq_ref.dtype)
            df_ref[:, ln] = dfs[j].astype(df_ref.dtype)
            di_ref[:, ln] = dis[j].astype(di_ref.dtype)
            dg_ref[:, ln] = dgs[j].astype(dg_ref.dtype)
            dlb_ref[:, ln] += dlbs[j]
            dstate[j] = dsts[j]
        dgn_ref[...] += dgn_sum

    wide = hpb * HEAD
    rev = lambda off: pl.BlockSpec((CHUNK, wide), lambda h, c: (n - 1 - c, off // hpb + h))
    out_blk = pl.BlockSpec((CHUNK, wide), lambda h, c: (n - 1 - c, h))
    width = heads * HEAD
    return _call(
        body, (proj, proj, proj, proj, lb, gn, states, d_out), name=name, grid=(heads // hpb, n),
        in_specs=[rev(0), rev(heads), rev(2 * heads), rev(3 * heads),
                  pl.BlockSpec((1, wide), lambda h, c: (0, h)), pl.BlockSpec((1, HEAD), lambda h, c: (0, 0)),
                  pl.BlockSpec((hpb, None, HEAD, HEAD), lambda h, c: (h, n - 1 - c, 0, 0)), out_blk],
        out_specs=[out_blk, out_blk, out_blk, out_blk,
                   pl.BlockSpec((1, wide), lambda h, c: (0, h)), pl.BlockSpec((1, HEAD), lambda h, c: (0, 0))],
        out_shape=[jax.ShapeDtypeStruct((s, width), BF16)] * 4
        + [jax.ShapeDtypeStruct((1, width), F32), jax.ShapeDtypeStruct((1, HEAD), F32)],
        scratch_shapes=[pltpu.VMEM((hpb, HEAD, HEAD), F32)], comm=comm)


def _shift_rows(x, d, rowi):
    if d == 0:
        return x
    n = x.shape[0]
    rolled = pltpu.roll(x, d % n, 0)
    keep = rowi >= d if d > 0 else rowi < n + d
    return jnp.where(keep, rolled, 0.0)


def _conv_pre(x, w_ref, rowi):
    acc = None
    for j in range(CONV_WIDTH):
        term = w_ref[j:j + 1, :] * _shift_rows(x, CONV_WIDTH - 1 - j, rowi)
        acc = term if acc is None else acc + term
    return acc


def _conv_fwd(proj, w, *, col_off, name, cb=256):
    s = proj.shape[0]
    width = w.shape[1]
    cb = min(cb, width)

    def body(x_ref, w_ref, o_ref):
        rowi = lax.broadcasted_iota(jnp.int32, (s, cb), 0)
        o_ref[...] = _silu(_conv_pre(x_ref[...], w_ref, rowi))

    return pl.pallas_call(
        body, name=name, grid=(width // cb,),
        in_specs=[pl.BlockSpec((s, cb), lambda j: (0, col_off // cb + j)), pl.BlockSpec((CONV_WIDTH, cb), lambda j: (0, j))],
        out_specs=pl.BlockSpec((s, cb), lambda j: (0, j)),
        out_shape=jax.ShapeDtypeStruct((s, width), F32),
        compiler_params=_params(1),
    )(proj, w)


def _conv_bwd(proj, w, d_out, *, col_off, name, cb=256, comm=()):
    s = proj.shape[0]
    width = w.shape[1]
    cb = min(cb, width)

    def body(x_ref, w_ref, do_ref, dx_ref, dw_ref):
        rowi = lax.broadcasted_iota(jnp.int32, (s, cb), 0)
        x = x_ref[...]
        pre = _conv_pre(x, w_ref, rowi)
        sg = jax.nn.sigmoid(pre)
        dpre = do_ref[...] * (sg + pre * sg * (1.0 - sg))
        dx = None
        for j in range(CONV_WIDTH):
            d = CONV_WIDTH - 1 - j
            term = w_ref[j:j + 1, :] * _shift_rows(dpre, -d, rowi)
            dx = term if dx is None else dx + term
            dw_ref[j:j + 1, :] = jnp.sum(dpre * _shift_rows(x, d, rowi), axis=0, keepdims=True)
        dx_ref[...] = dx.astype(dx_ref.dtype)

    return _call(
        body, (proj, w, d_out), name=name, grid=(width // cb,),
        in_specs=[pl.BlockSpec((s, cb), lambda j: (0, col_off // cb + j)), pl.BlockSpec((CONV_WIDTH, cb), lambda j: (0, j)),
                  pl.BlockSpec((s, cb), lambda j: (0, j))],
        out_specs=[pl.BlockSpec((s, cb), lambda j: (0, j)), pl.BlockSpec((CONV_WIDTH, cb), lambda j: (0, j))],
        out_shape=[jax.ShapeDtypeStruct((s, width), BF16), jax.ShapeDtypeStruct((CONV_WIDTH, width), F32)],
        comm=comm)


_lane_concat = _per_head(lambda a, b: jnp.concatenate([a, b], axis=1))
_lane_half = _per_head(lambda a, j: a[:, j * HEAD:(j + 1) * HEAD])


def _tri_inverse(lower):
    c = CHUNK
    row = lax.broadcasted_iota(jnp.int32, (c, c), 0)
    col = lax.broadcasted_iota(jnp.int32, (c, c), 1)
    inv = (row == col).astype(F32)
    lvl = 0
    while (1 << lvl) < c:
        same_pair = (row >> (lvl + 1)) == (col >> (lvl + 1))
        off_block = same_pair & (((row >> lvl) & 1) == 1) & (((col >> lvl) & 1) == 0)
        inv = inv - _hdot_h(_hdot_h(inv, _where(off_block, lower, 0.0), NN), inv, NN)
        lvl += 1
    return inv


@jax.custom_vjp
def _tri_solve(lowers, rhss):
    return _tri_solve_fwd(lowers, rhss)[0]


def _tri_solve_fwd(lowers, rhss):
    inv = _tri_inverse(_Heads(lowers))
    sol = _hdot_h(inv, _Heads(rhss), NN)
    return sol.vals, (inv.vals, sol.vals)


def _tri_solve_bwd(res, g):
    inv, sol = _Heads(res[0]), _Heads(res[1])
    d_rhs = _hdot_h(inv, _Heads(g), TN)
    return (-_hdot_h(d_rhs, sol, NT)).vals, d_rhs.vals


_tri_solve.defvjp(_tri_solve_fwd, _tri_solve_bwd)


def _solve(lower, rhs):
    if isinstance(lower, _Heads):
        return _Heads(_tri_solve(lower.vals, rhs.vals))
    return _tri_solve([lower], [rhs])[0]


def _delta_chunk(h, heads, qr, kr, vr, ab, zp, alog, dtb, gn, st):
    c = CHUNK
    row = lax.broadcasted_iota(jnp.int32, (c, c), 0)
    col = lax.broadcasted_iota(jnp.int32, (c, c), 1)
    causal = row >= col
    strict = row > col
    lane = lax.broadcasted_iota(jnp.int32, (c, HEAD), 1)
    mine = _equal(h, lane)
    la_full = -jnp.exp(alog) * _softplus(ab + dtb)
    cum_full = _hdot(causal.astype(F32), la_full)
    cum = _sum(_where(mine, cum_full, 0.0), axis=1, keepdims=True)
    cend = _sum(_sum(_where(mine, la_full, 0.0), axis=1, keepdims=True), axis=0, keepdims=True)
    beta = _sum(_where(_equal(heads + h, lane), jax.nn.sigmoid(ab), 0.0), axis=1, keepdims=True)
    cum_row = _hdot_h(_where(mine, 1.0, 0.0), cum_full, NT)
    decay = _where(causal, _exp(_where(causal, cum - cum_row, 0.0)), 0.0)
    qn = qr * _rsqrt(_sum(qr * qr, axis=-1, keepdims=True) + NORM_EPS) * (HEAD ** -0.5)
    kn = kr * _rsqrt(_sum(kr * kr, axis=-1, keepdims=True) + NORM_EPS)
    kb = kn * beta
    lower = _where(strict, _bdot_h(kb, kn, NT) * decay, 0.0)
    ecum = _exp(cum)
    sol = _solve(lower, _lane_concat(vr * beta, kb * ecum))
    u, w = _lane_half(sol, 0), _lane_half(sol, 1)
    intra = _bdot_h(qn, kn, NT) * decay
    v_new = u - _bdot_h(w, st, NN)
    out = _bdot_h(qn * ecum, st, NN) + _bdot_h(intra, v_new, NN)
    st_new = st * _exp(cend) + _bdot_h(kn * _exp(cend - cum), v_new, TN)
    res = _rms(out, gn) * _silu(zp)
    return res, st_new


def _delta_heads(hs, heads, qs, ks, vs, ab, zs, alog, dtb, gn, sts):
    res, st_new = _delta_chunk(_Heads(hs), heads, _Heads(qs), _Heads(ks), _Heads(vs), ab, _Heads(zs), alog, dtb, gn,
                               _Heads(sts))
    return res.vals, st_new.vals


def _delta_fwd(qkv, ab, proj, hp, gn, *, heads, z_off, name, comm=()):
    s = qkv.shape[0]
    n = s // CHUNK

    hpb = min(HEADS_PER_STEP, heads)
    assert heads % hpb == 0 and z_off % hpb == 0

    def body(q_ref, k_ref, v_ref, ab_ref, z_ref, hp_ref, gn_ref, o_ref, st_ref, state):
        hb = pl.program_id(1)

        @pl.when(pl.program_id(0) == 0)
        def _():
            for j in range(hpb):
                state[hb * hpb + j] = jnp.zeros((HEAD, HEAD), F32)

        shared = (ab_ref[...], hp_ref[0:1, :], hp_ref[1:2, :], gn_ref[...])
        loaded = [(q_ref[:, _lanes(j)], k_ref[:, _lanes(j)], v_ref[:, _lanes(j)], z_ref[:, _lanes(j)],
                   state[hb * hpb + j]) for j in range(hpb)]
        qs, ks, vs, zs, sts = (list(t) for t in zip(*loaded))
        res, st_new = _delta_heads([hb * hpb + j for j in range(hpb)], heads, qs, ks, vs, shared[0], zs, shared[1],
                                   shared[2], shared[3], sts)
        for j in range(hpb):
            st_ref[j] = sts[j]
            o_ref[:, _lanes(j)] = res[j].astype(o_ref.dtype)
            state[hb * hpb + j] = st_new[j]

    wide = hpb * HEAD
    blk = lambda off: pl.BlockSpec((CHUNK, wide), lambda c, h: (c, off // hpb + h))
    return _call(
        body, (qkv, qkv, qkv, ab, proj, hp, gn), name=name, grid=(n, heads // hpb),
        in_specs=[blk(0), blk(heads), blk(2 * heads), pl.BlockSpec((CHUNK, HEAD), lambda c, h: (c, 0)), blk(z_off),
                  pl.BlockSpec((8, HEAD), lambda c, h: (0, 0)), pl.BlockSpec((1, HEAD), lambda c, h: (0, 0))],
        out_specs=[pl.BlockSpec((CHUNK, wide), lambda c, h: (c, h)),
                   pl.BlockSpec((hpb, None, HEAD, HEAD), lambda c, h: (h, c, 0, 0))],
        out_shape=[jax.ShapeDtypeStruct((s, heads * HEAD), BF16), jax.ShapeDtypeStruct((heads, n, HEAD, HEAD), F32)],
        scratch_shapes=[pltpu.VMEM((heads, HEAD, HEAD), F32)], comm=comm)


def _delta_bwd(qkv, ab, proj, hp, gn, states, d_out, *, heads, z_off, name, comm=()):
    s = qkv.shape[0]
    n = s // CHUNK
    hpb = min(HEADS_PER_STEP, heads)

    def body(q_ref, k_ref, v_ref, ab_ref, z_ref, hp_ref, gn_ref, st_ref, do_ref,
             dq_ref, dk_ref, dv_ref, dab_ref, dz_ref, dhp_ref, dgn_ref, dstate):
        c, hb = pl.program_id(0), pl.program_id(1)

        @pl.when(c == 0)
        def _():
            for j in range(hpb):
                dstate[hb * hpb + j] = jnp.zeros((HEAD, HEAD), F32)

        @pl.when((c == 0) & (hb == 0))
        def _():
            dgn_ref[...] = jnp.zeros_like(dgn_ref)
            dhp_ref[...] = jnp.zeros_like(dhp_ref)

        @pl.when(hb == 0)
        def _():
            dab_ref[...] = jnp.zeros_like(dab_ref)

        shared = (ab_ref[...], hp_ref[0:1, :], hp_ref[1:2, :], gn_ref[...])
        loaded = [(q_ref[:, _lanes(j)], k_ref[:, _lanes(j)], v_ref[:, _lanes(j)], z_ref[:, _lanes(j)], st_ref[j],
                   do_ref[:, _lanes(j)].astype(F32), dstate[hb * hpb + j]) for j in range(hpb)]
        qs, ks, vs, zs, sts, dos, dss = (list(t) for t in zip(*loaded))
        fn = functools.partial(_delta_heads, [hb * hpb + j for j in range(hpb)], heads)
        _, vjp = jax.vjp(fn, qs, ks, vs, shared[0], zs, shared[1], shared[2], shared[3], sts)
        dqs, dks, dvs, dab, dzs, dal, ddt, dgn, dsts = vjp((dos, dss))
        for j in range(hpb):
            ln = _lanes(j)
            dq_ref[:, ln] = dqs[j]
            dk_ref[:, ln] = dks[j]
            dv_ref[:, ln] = dvs[j]
            dz_ref[:, ln] = dzs[j].astype(dz_ref.dtype)
            dstate[hb * hpb + j] = dsts[j]
        dab_ref[...] += dab
        dhp_ref[0:1, :] += dal
        dhp_ref[1:2, :] += ddt
        dgn_ref[...] += dgn

    wide = hpb * HEAD
    rev = lambda off: pl.BlockSpec((CHUNK, wide), lambda c, h: (n - 1 - c, off // hpb + h))
    width = heads * HEAD
    head_blk = pl.BlockSpec((CHUNK, wide), lambda c, h: (n - 1 - c, h))
    ab_blk = pl.BlockSpec((CHUNK, HEAD), lambda c, h: (n - 1 - c, 0))
    return _call(
        body, (qkv, qkv, qkv, ab, proj, hp, gn, states, d_out), name=name, grid=(n, heads // hpb),
        in_specs=[rev(0), rev(heads), rev(2 * heads), ab_blk, rev(z_off),
                  pl.BlockSpec((8, HEAD), lambda c, h: (0, 0)), pl.BlockSpec((1, HEAD), lambda c, h: (0, 0)),
                  pl.BlockSpec((hpb, None, HEAD, HEAD), lambda c, h: (h, n - 1 - c, 0, 0)), head_blk],
        out_specs=[head_blk, head_blk, head_blk, ab_blk, head_blk,
                   pl.BlockSpec((8, HEAD), lambda c, h: (0, 0)), pl.BlockSpec((1, HEAD), lambda c, h: (0, 0))],
        out_shape=[jax.ShapeDtypeStruct((s, width), F32)] * 3
        + [jax.ShapeDtypeStruct((s, HEAD), F32), jax.ShapeDtypeStruct((s, width), BF16),
           jax.ShapeDtypeStruct((8, HEAD), F32), jax.ShapeDtypeStruct((1, HEAD), F32)],
        scratch_shapes=[pltpu.VMEM((heads, HEAD, HEAD), F32)], comm=comm)


def _s5_scan(buf, lt_ref, cin_r, cin_i, tt, reverse):
    nblk = tt // 8
    hl = S5_HALF
    base = 8 if reverse else 0

    def body(j, carry):
        cr, ci = carry
        off = pl.multiple_of((nblk - 1 - j if reverse else j) * 8, 8)
        xr = buf[pl.ds(off, 8), 0:hl]
        xi = buf[pl.ds(off, 8), hl:2 * hl]
        for lv, d in enumerate((1, 2, 4)):
            ar, ai = lt_ref[base + 2 * lv], lt_ref[base + 2 * lv + 1]
            sr = pltpu.roll(xr, 8 - d if reverse else d, 0)
            si = pltpu.roll(xi, 8 - d if reverse else d, 0)
            xr, xi = xr + ar * sr - ai * si, xi + ar * si + ai * sr
        pr, pi = lt_ref[base + 6], lt_ref[base + 7]
        xr, xi = xr + pr * cr - pi * ci, xi + pr * ci + pi * cr
        buf[pl.ds(off, 8), 0:hl] = xr
        buf[pl.ds(off, 8), hl:2 * hl] = xi
        edge = 0 if reverse else 7
        return xr[edge:edge + 1, :], xi[edge:edge + 1, :]

    return lax.fori_loop(0, nblk, body, (cin_r, cin_i))


def _s5_fwd(u, wb, wc, lt, dskip, *, name, tt=512, comm=()):
    s, d = u.shape
    nb = d // HEAD
    tt = min(tt, s)
    nt = s // tt
    hl = S5_HALF

    def body(u_ref, wb_ref, wc_ref, lt_ref, d_ref, y_ref, cin_ref, st_ref, buf, carry):
        @pl.when(pl.program_id(1) == 0)
        def _():
            carry[...] = jnp.zeros_like(carry)

        cin_ref[...] = carry[0:1, :]
        uv = u_ref[...]
        buf[...] = _bdot_raw(uv, wb_ref[...])
        cr, ci = _s5_scan(buf, lt_ref, carry[0:1, 0:hl], carry[0:1, hl:2 * hl], tt, False)
        carry[0:1, 0:hl] = cr
        carry[0:1, hl:2 * hl] = ci
        states = buf[...].astype(BF16)
        st_ref[...] = states
        y_ref[...] = _bdot_raw(states, wc_ref[...]) + d_ref[...] * uv

    return _call(
        body, (u, wb, wc, lt, dskip), name=name, grid=(nb, nt),
        in_specs=[pl.BlockSpec((tt, HEAD), lambda b, t: (t, b)),
                  pl.BlockSpec((None, HEAD, 2 * hl), lambda b, t: (b, 0, 0)),
                  pl.BlockSpec((None, 2 * hl, HEAD), lambda b, t: (b, 0, 0)),
                  pl.BlockSpec((None, 16, 8, hl), lambda b, t: (b, 0, 0, 0)),
                  pl.BlockSpec((1, HEAD), lambda b, t: (0, b))],
        out_specs=[pl.BlockSpec((tt, HEAD), lambda b, t: (t, b)),
                   pl.BlockSpec((None, None, 1, 2 * hl), lambda b, t: (b, t, 0, 0)),
                   pl.BlockSpec((tt, 2 * hl), lambda b, t: (t, b))],
        out_shape=[jax.ShapeDtypeStruct((s, d), F32), jax.ShapeDtypeStruct((nb, nt, 1, 2 * hl), F32),
                   jax.ShapeDtypeStruct((s, nb * 2 * hl), BF16)],
        scratch_shapes=[pltpu.VMEM((tt, 2 * hl), F32), pltpu.VMEM((8, 2 * hl), F32)], comm=comm)


def _s5_bwd(u, dy, wb, wc, lt, dskip, cins, states, *, name, tt=512, comm=()):
    s, d = u.shape
    nb = d // HEAD
    tt = min(tt, s)
    nt = s // tt
    hl = S5_HALF

    def body(u_ref, dy_ref, wb_ref, wc_ref, lt_ref, d_ref, cin_ref, st_ref,
             du_ref, dwb_ref, dwc_ref, dd_ref, dlam_ref, abuf, acarry):
        @pl.when(pl.program_id(1) == 0)
        def _():
            acarry[...] = jnp.zeros_like(acarry)
            dwb_ref[...] = jnp.zeros_like(dwb_ref)
            dwc_ref[...] = jnp.zeros_like(dwc_ref)
            dd_ref[...] = jnp.zeros_like(dd_ref)
            dlam_ref[...] = jnp.zeros_like(dlam_ref)

        uv, dyv = u_ref[...], dy_ref[...]
        abuf[...] = _bdot_raw(dyv, wc_ref[...], NT)
        ar, ai = _s5_scan(abuf, lt_ref, acarry[0:1, 0:hl], acarry[0:1, hl:2 * hl], tt, True)
        acarry[0:1, 0:hl] = ar
        acarry[0:1, hl:2 * hl] = ai
        du_ref[...] = _bdot_raw(abuf[...], wb_ref[...], NT) + d_ref[...] * dyv
        dwb_ref[...] += _bdot_raw(uv, abuf[...], TN)
        dwc_ref[...] += _bdot_raw(st_ref[...], dyv, TN)
        dd_ref[...] += jnp.sum(dyv * uv, axis=0, keepdims=True)
        first = lax.broadcasted_iota(jnp.int32, (tt, hl), 0) == 0
        spr = jnp.where(first, cin_ref[:, 0:hl], pltpu.roll(st_ref[:, 0:hl].astype(F32), 1, 0))
        spi = jnp.where(first, cin_ref[:, hl:2 * hl], pltpu.roll(st_ref[:, hl:2 * hl].astype(F32), 1, 0))
        avr, avi = abuf[:, 0:hl], abuf[:, hl:2 * hl]
        dlam_ref[:, 0:hl] += jnp.sum(avr * spr + avi * spi, axis=0, keepdims=True)
        dlam_ref[:, hl:2 * hl] += jnp.sum(avi * spr - avr * spi, axis=0, keepdims=True)

    rev = pl.BlockSpec((tt, HEAD), lambda b, t: (nt - 1 - t, b))
    return _call(
        body, (u, dy, wb, wc, lt, dskip, cins, states), name=name, grid=(nb, nt),
        in_specs=[rev, rev,
                  pl.BlockSpec((None, HEAD, 2 * hl), lambda b, t: (b, 0, 0)),
                  pl.BlockSpec((None, 2 * hl, HEAD), lambda b, t: (b, 0, 0)),
                  pl.BlockSpec((None, 16, 8, hl), lambda b, t: (b, 0, 0, 0)),
                  pl.BlockSpec((1, HEAD), lambda b, t: (0, b)),
                  pl.BlockSpec((None, None, 1, 2 * hl), lambda b, t: (b, nt - 1 - t, 0, 0)),
                  pl.BlockSpec((tt, 2 * hl), lambda b, t: (nt - 1 - t, b))],
        out_specs=[rev,
                   pl.BlockSpec((None, HEAD, 2 * hl), lambda b, t: (b, 0, 0)),
                   pl.BlockSpec((None, 2 * hl, HEAD), lambda b, t: (b, 0, 0)),
                   pl.BlockSpec((1, HEAD), lambda b, t: (0, b)),
                   pl.BlockSpec((None, 1, 2 * hl), lambda b, t: (b, 0, 0))],
        out_shape=[jax.ShapeDtypeStruct((s, d), F32), jax.ShapeDtypeStruct(wb.shape, F32),
                   jax.ShapeDtypeStruct(wc.shape, F32), jax.ShapeDtypeStruct((1, d), F32),
                   jax.ShapeDtypeStruct((nb, 1, 2 * hl), F32)],
        scratch_shapes=[pltpu.VMEM((tt, 2 * hl), F32), pltpu.VMEM((8, 2 * hl), F32)],
        comm=comm)


def _s5_pack(lr, li, br, bi, c_re, c_im):
    g = lr.shape[0]
    nb = g // S5_GB
    eye = jnp.eye(S5_GB, dtype=F32)
    bm = jnp.stack([br, bi]).reshape(2, nb, S5_GB, S5_STATE, S5_GROUP)
    wb = jnp.einsum("rbgpc,gh->bgcrhp", bm, eye).reshape(nb, HEAD, 2 * S5_HALF)
    cm = jnp.stack([c_re, -c_im]).reshape(2, nb, S5_GB, S5_GROUP, S5_STATE)
    wc = jnp.einsum("rbgcp,gh->brgphc", cm, eye).reshape(nb, 2 * S5_HALF, HEAD)
    pw = [(lr, li)]
    for _ in range(7):
        pr, pi = pw[-1]
        pw.append((pr * lr - pi * li, pr * li + pi * lr))
    blk = lambda a: a.reshape(nb, 1, S5_HALF)
    rows = jnp.arange(8).reshape(1, 8, 1)
    tables = []
    for conj, keep, order in ((1.0, lambda n: rows >= n, range(8)), (-1.0, lambda n: rows < 8 - n, range(7, -1, -1))):
        for n in (1, 2, 4):
            tables += [jnp.where(keep(n), blk(pw[n - 1][0]), 0.0), jnp.where(keep(n), conj * blk(pw[n - 1][1]), 0.0)]
        tables += [jnp.concatenate([blk(pw[n][0]) for n in order], axis=1),
                   jnp.concatenate([conj * blk(pw[n][1]) for n in order], axis=1)]
    return wb, wc, jnp.stack(tables, axis=1)


def _s5_unpack(dwb, dwc, dlam):
    nb = dwb.shape[0]
    g = nb * S5_GB
    eye = jnp.eye(S5_GB, dtype=F32)
    db = jnp.einsum("bgcrhp,gh->rbgpc", dwb.reshape(nb, S5_GB, S5_GROUP, 2, S5_GB, S5_STATE), eye)
    db = db.reshape(2, g, S5_STATE * S5_GROUP)
    dc = jnp.einsum("brgphc,gh->rbgcp", dwc.reshape(nb, 2, S5_GB, S5_STATE, S5_GB, S5_GROUP), eye)
    dc = dc.reshape(2, g, S5_GROUP, S5_STATE)
    dl = dlam.reshape(nb, 2, S5_GB, S5_STATE).transpose(1, 0, 2, 3).reshape(2, g, S5_STATE)
    return dl[0], dl[1], db[0], db[1], dc[0], -dc[1]


def _peer(r):
    mx, my, mc = lax.axis_index("x"), lax.axis_index("y"), lax.axis_index("c")
    px = 1 - mx if r & 4 else mx
    py = 1 - my if r & 2 else my
    pc = 1 - mc if r & 1 else mc
    return (px, py, pc), 4 * px + 2 * py + pc


_COMM_SCRATCH = [pltpu.SemaphoreType.DMA((N_DEV - 1,)), pltpu.SemaphoreType.DMA((N_DEV - 1,)), pltpu.SemaphoreType.DMA]


class _AllToAll:
    def __init__(self, x, rows=None):
        self.x = x
        self.rows = rows
        shape = x.shape if rows is None else (x.shape[0], rows[1]) + tuple(x.shape[2:])
        self.out_shape = jax.ShapeDtypeStruct(shape, x.dtype)

    def _copies(self, x_ref, out_ref, send_sems, recv_sems, local_sem):
        def block(j):
            return x_ref.at[j] if self.rows is None else x_ref.at[j, pl.ds(self.rows[0], self.rows[1])]

        _, me = _peer(0)
        mine = pltpu.make_async_copy(block(me), out_ref.at[me], local_sem)
        sends, recvs = [], []
        for r in range(1, N_DEV):
            pos, idx = _peer(r)
            sems = dict(send_sem=send_sems.at[r - 1], recv_sem=recv_sems.at[r - 1], device_id=pos, device_id_type=MESH)
            sends.append(pltpu.make_async_remote_copy(src_ref=block(idx), dst_ref=out_ref.at[me], **sems))
            recvs.append(pltpu.make_async_remote_copy(src_ref=block(idx), dst_ref=out_ref.at[idx], **sems))
        return mine, sends, recvs

    def start(self, *refs):
        mine, sends, _ = self._copies(*refs)
        mine.start()
        for cp in sends:
            cp.start()

    def finish(self, *refs):
        mine, sends, recvs = self._copies(*refs)
        for cp in recvs:
            cp.wait_recv()
        for cp in sends:
            cp.wait_send()
        mine.wait()


class _Gather:
    def __init__(self, x):
        self.x = x
        self.out_shape = jax.ShapeDtypeStruct((N_DEV,) + tuple(x.shape), x.dtype)

    def _copies(self, x_ref, out_ref, send_sems, recv_sems, local_sem):
        mx, my, mc = lax.axis_index("x"), lax.axis_index("y"), lax.axis_index("c")
        me, sibling = (mx, my, mc), (mx, my, 1 - mc)
        chips = [(1 - mx, my), (mx, 1 - my), (1 - mx, 1 - my)]

        def slot(px, py, pc):
            return out_ref.at[4 * px + 2 * py + pc]

        def copy(k, block, to, src=None):
            return pltpu.make_async_remote_copy(
                src_ref=slot(*block) if src is None else src, dst_ref=slot(*block),
                send_sem=send_sems.at[k], recv_sem=recv_sems.at[k], device_id=to, device_id_type=MESH)

        return dict(
            mine=pltpu.make_async_copy(x_ref, slot(*me), local_sem),
            first=[copy(0, me, sibling, src=x_ref)] + [copy(1 + j, me, (*chip, mc), src=x_ref) for j, chip in enumerate(chips)],
            passed=[copy(4 + j, (*chip, mc), sibling) for j, chip in enumerate(chips)],
            over_ici=[copy(1 + j, (*chip, mc), me) for j, chip in enumerate(chips)],
            from_sibling=[copy(0, sibling, me)] + [copy(4 + j, (*chip, 1 - mc), me) for j, chip in enumerate(chips)])

    def start(self, *refs):
        cps = self._copies(*refs)
        cps["mine"].start()
        for cp in cps["first"]:
            cp.start()

    def finish(self, *refs):
        cps = self._copies(*refs)
        for arrived, onward in zip(cps["over_ici"], cps["passed"]):
            arrived.wait_recv()
            onward.start()
        for cp in cps["from_sibling"]:
            cp.wait_recv()
        for cp in cps["first"] + cps["passed"]:
            cp.wait_send()
        cps["mine"].wait()


def _call(body, args, *, name, grid, in_specs, out_specs, out_shape, scratch_shapes=(), comm=()):
    n_in, n_out, n_scr, nc = len(in_specs), len(out_shape), len(scratch_shapes), len(comm)

    def wrapped(*refs):
        ins, c_in = refs[:n_in], refs[n_in:n_in + nc]
        outs, c_out = refs[n_in + nc:n_in + nc + n_out], refs[n_in + nc + n_out:n_in + 2 * nc + n_out]
        scr = refs[n_in + 2 * nc + n_out:n_in + 2 * nc + n_out + n_scr]
        sems = refs[n_in + 2 * nc + n_out + n_scr:]
        ids = [pl.program_id(a) for a in range(len(grid))]
        if nc:
            @pl.when(functools.reduce(operator.and_, [i == 0 for i in ids]))
            def _():
                for k, op in enumerate(comm):
                    op.start(c_in[k], c_out[k], *sems[3 * k:3 * k + 3])

        body(*ins, *outs, *scr)
        if nc:
            @pl.when(functools.reduce(operator.and_, [i == g - 1 for i, g in zip(ids, grid)]))
            def _():
                for k, op in enumerate(comm):
                    op.finish(c_in[k], c_out[k], *sems[3 * k:3 * k + 3])

    any_spec = pl.BlockSpec(memory_space=pl.ANY)
    res = pl.pallas_call(
        wrapped, name=name, grid=grid,
        in_specs=list(in_specs) + [any_spec] * nc, out_specs=list(out_specs) + [any_spec] * nc,
        out_shape=list(out_shape) + [op.out_shape for op in comm],
        scratch_shapes=list(scratch_shapes) + list(_COMM_SCRATCH) * nc,
        compiler_params=_params(len(grid)),
    )(*args, *[op.x for op in comm])
    return list(res[:n_out]), list(res[n_out:])


def _comm_call(op, *, name):
    return _call(lambda: None, (), name=name, grid=(1,), in_specs=[], out_specs=[], out_shape=[], comm=(op,))[1][0]


def _adamw(w, parts, m, v, *, name, tr=128):
    nl, r, c = w.shape
    assert len(parts) == nl
    parts = [list(p) if isinstance(p, (list, tuple)) else [p] for p in parts]
    npart = parts[0][0].shape[0]
    tr = min(tr, r, *[pc.shape[1] for p in parts for pc in p])
    assert r % tr == 0 and all(pc.shape[1] % tr == 0 for p in parts for pc in p), (name, r, tr)
    pieces = []
    for l, p in enumerate(parts):
        first = 0
        for pc in p:
            pieces.append((l, first, pc.shape[1] // tr, pc))
            first += pc.shape[1] // tr
        assert first == r // tr, (name, l)

    def body(w_ref, m_ref, v_ref, *rest):
        p_refs, (g_ref, d_ref, mo_ref, vo_ref) = rest[:len(pieces)], rest[len(pieces):]
        layer, tile = pl.program_id(0), pl.program_id(1)
        for p_ref, (l, first, count, _) in zip(p_refs, pieces):
            @pl.when((layer == l) & (tile >= first) & (tile < first + count))
            def _():
                g = p_ref[0].astype(F32)
                for k in range(1, npart):
                    g = g + p_ref[k].astype(F32)
                m2 = ADAM_B1 * m_ref[...] + (1.0 - ADAM_B1) * g
                v2 = ADAM_B2 * v_ref[...] + (1.0 - ADAM_B2) * (g * g)
                m_hat = m2 / (1.0 - ADAM_B1 ** ADAM_STEP)
                v_hat = v2 / (1.0 - ADAM_B2 ** ADAM_STEP)
                g_ref[...] = g
                d_ref[...] = -ADAM_LR * (m_hat / (jnp.sqrt(v_hat) + ADAM_EPS) + ADAM_WD * w_ref[...])
                mo_ref[...] = m2
                vo_ref[...] = v2

    blk = pl.BlockSpec((None, tr, c), lambda l, i: (l, i, 0))

    def part_spec(l, first, count):
        return pl.BlockSpec((npart, tr, c), lambda ll, i: (0, jnp.where(ll == l, jnp.clip(i - first, 0, count - 1), 0), 0))

    return pl.pallas_call(
        body, name=name, grid=(nl, r // tr),
        in_specs=[blk, blk, blk] + [part_spec(l, first, count) for l, first, count, _ in pieces],
        out_specs=[blk] * 4, out_shape=[jax.ShapeDtypeStruct((nl, r, c), F32)] * 4,
        compiler_params=_params(2),
    )(w, m, v, *[pc for _, _, _, pc in pieces])


def _sum_parts(parts, *, name):
    npart = parts.shape[0]

    def body(p_ref, o_ref):
        g = p_ref[0]
        for k in range(1, npart):
            g = g + p_ref[k]
        o_ref[...] = g

    return pl.pallas_call(
        body, name=name, grid=(1,), in_specs=[_full_spec(parts)],
        out_specs=pl.BlockSpec(parts.shape[1:], lambda i: (0, 0)),
        out_shape=jax.ShapeDtypeStruct(parts.shape[1:], F32), compiler_params=_params(1),
    )(parts)


def _pack(arrs):
    blocks = []
    for a in arrs:
        flat = a.reshape(-1).astype(F32)
        blocks.append(jnp.pad(flat, (0, (-flat.shape[0]) % (8 * HEAD))).reshape(-1, HEAD))
    out = jnp.concatenate(blocks, axis=0)
    return jnp.pad(out, ((0, (-out.shape[0]) % HEAD), (0, 0)))


def _unpack(packed, shapes):
    out, off = [], 0
    for shp in shapes:
        size = math.prod(shp)
        rows = -(-size // (8 * HEAD)) * 8
        out.append(packed[off:off + rows].reshape(-1)[:size].reshape(shp))
        off += rows
    return out


def _add_epilogue(acc, res):
    return (acc + res,)


def _relu2_epilogue(acc):
    r = jnp.maximum(acc, 0.0)
    return acc, r * r


def _ple_epilogue(acc, gpre, h):
    return h + jax.nn.sigmoid(gpre) * acc, acc


def kernel(x, p, norm_mix, norm_mlp, norm_ple, w_in_e, w_out_e, hgrn_lb, g_norm_a, conv_w, a_log, dt_bias, g_norm_b, s5_a_re, s5_a_im, s5_b_re, s5_b_im, s5_c_re, s5_c_im, s5_d, s5_log_dt, w_glu, b_glu, w_out_o, w_up, w_down, w_ple_gate, w_ple_proj, final_norm, loss_target, m_norm_mix, m_norm_mlp, m_norm_ple, m_w_in_e, m_w_out_e, m_hgrn_lb, m_g_norm_a, m_conv_w, m_a_log, m_dt_bias, m_g_norm_b, m_s5_a_re, m_s5_a_im, m_s5_b_re, m_s5_b_im, m_s5_c_re, m_s5_c_im, m_s5_d, m_s5_log_dt, m_w_glu, m_b_glu, m_w_out_o, m_w_up, m_w_down, m_w_ple_gate, m_w_ple_proj, m_final_norm, v_norm_mix, v_norm_mlp, v_norm_ple, v_w_in_e, v_w_out_e, v_hgrn_lb, v_g_norm_a, v_conv_w, v_a_log, v_dt_bias, v_g_norm_b, v_s5_a_re, v_s5_a_im, v_s5_b_re, v_s5_b_im, v_s5_c_re, v_s5_c_im, v_s5_d, v_s5_log_dt, v_w_glu, v_b_glu, v_w_out_o, v_w_up, v_w_down, v_w_ple_gate, v_w_ple_proj, v_final_norm):
    args = dict(locals())
    s, d = x.shape[1], x.shape[2]
    aw = d // 2
    ha = hb = aw // HEAD
    main = 4 * d
    z_col = 2 * d + 3 * aw
    ff = w_up.shape[2] * N_DEV
    ple = p.shape[-1]
    groups = d // S5_GROUP
    me = 4 * lax.axis_index("x") + 2 * lax.axis_index("y") + lax.axis_index("c")
    x2, target = x[0], loss_target[0]
    row = lambda a, i: a[i:i + 1]

    def gather_of(w):
        return _Gather(w.astype(BF16))

    w_in = jnp.transpose(_comm_call(gather_of(w_in_e[0]), name="ag_w_in"), (1, 0, 2)).reshape(d, -1)
    w_main = w_in[:, :main]
    w_tail = jnp.pad(w_in[:, main:], ((0, 0), (0, HEAD - 2 * hb)))

    lb_rows = [row(hgrn_lb, 0), row(hgrn_lb, 1), row(hgrn_lb, 2)]
    (lb0,) = _small_call(_lb0_stage, lb_rows, name="f_lb0")
    hp = jnp.zeros((8, HEAD), F32).at[0, :hb].set(a_log[0]).at[1, :hb].set(dt_bias[0])
    expand = jnp.asarray(np.kron(np.eye(S5_STATE, dtype=np.float32), np.ones((1, S5_GROUP), np.float32)))
    prep_in = [s5_a_re[0], s5_a_im[0], s5_log_dt[0].reshape(groups, 1),
               s5_b_re[0].reshape(groups, -1), s5_b_im[0].reshape(groups, -1), expand]
    lr, li, br, bi = _small_call(_s5_prep_stage, prep_in, name="f_s5_prep")
    wb, wc, lt = _s5_pack(lr, li, br, bi, s5_c_re[0], s5_c_im[0])
    fnorm = final_norm.reshape(1, d)

    w_upg = []

    def block_fwd(h, l):
        hn = _rows_call(_rms_stage, [h], [row(norm_mlp, l)], [BF16], name=f"f_norm_mlp{l}")
        (up, act), (dn8,) = _mm(hn, w_upg[l], epilogue=_relu2_epilogue, out_dtypes=(F32, BF16), name=f"f_up{l}",
                                comm=(gather_of(w_down[l]),))
        w_dn = dn8.reshape(ff, d)
        h2 = _mm(act, w_dn, extras=(h,), epilogue=_add_epilogue, name=f"f_down{l}")
        hq = _rows_call(_rms_stage, [h2], [row(norm_ple, l)], [BF16], name=f"f_norm_ple{l}")
        gpre = _mm(hq, w_pgg[l], name=f"f_ple_gate{l}")
        h3, pp = _mm(p[l, 0], w_ppg[l], extras=(gpre, h2), epilogue=_ple_epilogue, out_dtypes=(F32, F32),
                     name=f"f_ple_proj{l}")
        return h3, dict(h=h, hn=hn, up=up, act=act, h2=h2, hq=hq, gpre=gpre, pp=pp, w_dn=w_dn)

    hn0 = _rows_call(_rms_stage, [x2], [row(norm_mix, 0)], [BF16], name="f_norm_mix0")
    shard_shapes = [conv_w[0].shape, s5_d.shape, b_glu.shape]
    proj, (oe8, pg8, small) = _mm(hn0, w_main, name="f_proj", comm=(
        gather_of(w_out_e[0]), gather_of(w_ple_gate), _Gather(_pack([conv_w[0], s5_d, b_glu]))))
    w_oe = oe8.reshape(d, d)
    w_top, w_bot = w_oe[:aw], w_oe[aw:]
    w_pgg = jnp.transpose(pg8, (1, 0, 2, 3)).reshape(2, d, d)
    conv_g, s5d_g, bglu_g = zip(*[_unpack(small[j], shard_shapes) for j in range(N_DEV)])
    conv_full = jnp.concatenate(conv_g, axis=1)
    s5d_full = jnp.concatenate(s5d_g, axis=1)
    bglu_full = jnp.concatenate(bglu_g, axis=1)
    ab = _mm(hn0, w_tail, name="f_ab")
    (oa, st_a), (gl8, oo8) = _hgrn_fwd(proj, lb0, g_norm_a, heads=ha, name="f_hgrn",
                                       comm=(gather_of(w_glu[0]), gather_of(w_out_o[0])))
    w_gl, w_oo = gl8.reshape(d, d), oo8.reshape(d, d)
    qkv = _conv_fwd(proj, conv_full, col_off=2 * d, name="f_conv")
    slots_to_cols = lambda g8: jnp.transpose(g8, (1, 0, 2)).reshape(g8.shape[1], -1)
    (ob, st_b), (up8,) = _delta_fwd(qkv, ab, proj, hp, g_norm_b, heads=hb, z_off=z_col // HEAD, name="f_delta",
                                    comm=(gather_of(w_up[0]),))
    w_upg.append(slots_to_cols(up8))
    h1, (pp8,) = _mm(oa, w_top, extras=(x2,), epilogue=_add_epilogue, name="f_out_a", comm=(gather_of(w_ple_proj),))
    w_ppg = jnp.transpose(pp8, (1, 2, 0, 3)).reshape(2, ple, d)
    h1 = _mm(ob, w_bot, extras=(h1,), epilogue=_add_epilogue, name="f_out_b")
    h3, sv0 = block_fwd(h1, 0)

    u = _rows_call(_rms_stage, [h3], [row(norm_mix, 1)], [F32], name="f_norm_mix1")
    (y, cins, s5_states), (up8,) = _s5_fwd(u, wb, wc, lt, s5d_full, name="f_s5", comm=(gather_of(w_up[1]),))
    w_upg.append(slots_to_cols(up8))
    act_g = _rows_call(_gelu_stage, [y], [], [BF16], name="f_gelu")
    gl_raw = _mm(act_g, w_gl, name="f_glu")
    glu = _rows_call(_glu_stage, [y, gl_raw], [bglu_full], [BF16], name="f_glu_gate")
    h4 = _mm(glu, w_oo, extras=(h3,), epilogue=_add_epilogue, name="f_out_o")
    h6, sv1 = block_fwd(h4, 1)
    dh, d_fnorm, loss8 = _loss_call(h6, fnorm, target, name="loss")
    loss = lax.psum(loss8[0, 0], ("x", "y", "c"))

    dshard, ffs, cols = d // N_DEV, ff // N_DEV, w_in_e.shape[2]
    rows8 = lambda g: _AllToAll(g.reshape(N_DEV, -1, g.shape[-1]))
    cols8 = lambda g: _AllToAll(jnp.transpose(g.reshape(g.shape[0], N_DEV, -1), (1, 0, 2)))

    def halves(g8):
        r = g8.shape[1] // 2
        return _AllToAll(g8, rows=(0, r)), _AllToAll(g8, rows=(r, r))

    def block_bwd(dh3, l, sv, carried=(), carried_wup=(), carried_up=()):
        (dgpre, dpp), _ = _rows_vjp(_ple_stage, [sv["h2"], sv["gpre"], sv["pp"]], [], [dh3],
                                    row_grads={1: BF16, 2: BF16}, name=f"b_ple{l}")
        g_pp = _mm(p[l, 0], dpp, ta=True, out_dtypes=(BF16,), name=f"b_w_ple_proj{l}")
        g_pg = _mm(sv["hq"], dgpre, ta=True, out_dtypes=(BF16,), name=f"b_w_ple_gate{l}")
        dhq = _mm(dgpre, w_pgg[l], tb=True, name=f"b_ple_gate{l}")
        (dh2,), (g_nple,) = _rows_vjp(_rms_stage, [sv["h2"]], [row(norm_ple, l)], [dhq], row_grads={0: F32},
                                      adds={0: dh3}, name=f"b_norm_ple{l}")
        dup, (r_pg, r_pp) = _mm(dh2, sv["w_dn"], tb=True, extras=(sv["up"],), epilogue=_relu2_grad_epilogue,
                                out_dtypes=(BF16,), name=f"b_down{l}", comm=(rows8(g_pg), cols8(g_pp)))
        g_dn = _mm(sv["act"], dh2, ta=True, out_dtypes=(BF16,), name=f"b_w_down{l}", comm=carried)
        g_dn, r_carried = g_dn if carried else (g_dn, [])
        g_up = _mm(sv["hn"], dup, ta=True, out_dtypes=(BF16,), tn=ffs, out_slots=True, name=f"b_w_up{l}",
                   comm=carried_wup)
        g_up, r_carried_wup = g_up if carried_wup else (g_up, [])
        dhn = _mm(dup, w_upg[l], tb=True, name=f"b_up{l}", comm=carried_up)
        dhn, r_carried_up = dhn if carried_up else (dhn, [])
        (dh0,), (g_nmlp,) = _rows_vjp(_rms_stage, [sv["h"]], [row(norm_mlp, l)], [dhn], row_grads={0: F32},
                                      adds={0: dh2}, name=f"b_norm_mlp{l}")
        return dh0, dict(w_ple_proj=r_pp, w_ple_gate=r_pg, norm_ple=g_nple, w_down=g_dn, w_up=g_up, norm_mlp=g_nmlp,
                         carried=r_carried, carried_wup=r_carried_wup, carried_up=r_carried_up)

    dh4, gb1 = block_bwd(dh, 1, sv1)
    up1_a, up1_b = halves(gb1["w_up"])
    dglu = _mm(dh4, w_oo, tb=True, name="b_out_o")
    g_oo = _mm(glu, dh4, ta=True, out_dtypes=(BF16,), name="b_w_out_o")
    (dy1, dgl), (g_bglu,) = _rows_vjp(_glu_stage, [y, gl_raw], [bglu_full], [dglu], row_grads={0: F32, 1: BF16},
                                      name="b_glu_gate")
    g_gl = _mm(act_g, dgl, ta=True, out_dtypes=(BF16,), name="b_w_glu")
    dact = _mm(dgl, w_gl, tb=True, name="b_glu")
    (dy,), _ = _rows_vjp(_gelu_stage, [y], [], [dact], row_grads={0: F32}, adds={0: dy1}, name="b_gelu")
    (du, dwb, dwc, g_s5d, dlam), (r_dn1, r_up1_a) = _s5_bwd(u, dy, wb, wc, lt, s5d_full, cins, s5_states, name="b_s5",
                                                           comm=(rows8(gb1["w_down"]), up1_a))
    (dh3,), (g_nmix1,) = _rows_vjp(_rms_stage, [h3], [row(norm_mix, 1)], [du], row_grads={0: F32}, adds={0: dh4},
                                   name="b_norm_mix1")
    dlr, dli, dbr, dbi, g_cre, g_cim = _s5_unpack(dwb, dwc, dlam)
    g_are, g_aim, g_ldt, g_bre, g_bim, _ = _small_vjp(_s5_prep_stage, prep_in, [dlr, dli, dbr, dbi], name="b_s5_prep")

    early_grads = dict(
        s5_a_re=g_are[None], s5_a_im=g_aim[None], s5_b_re=g_bre.reshape(s5_b_re.shape),
        s5_b_im=g_bim.reshape(s5_b_im.shape), s5_c_re=g_cre[None], s5_c_im=g_cim[None],
        s5_log_dt=g_ldt.reshape(1, groups), final_norm=d_fnorm.reshape(d), s5_d=g_s5d, b_glu=g_bglu)
    dh1, gb0 = block_bwd(dh3, 0, sv0, (rows8(g_oo), rows8(g_gl)), (up1_b,),
                         (_Gather(_pack(list(early_grads.values()))),))
    r_oo, r_gl = gb0["carried"]
    r_up1 = [r_up1_a, gb0["carried_wup"][0]]
    (early_parts,) = gb0["carried_up"]
    dn0_a, dn0_b = halves(gb0["w_down"].reshape(N_DEV, ffs, d))
    up0_a, up0_b = halves(gb0["w_up"])
    doa = _mm(dh1, w_top, tb=True, name="b_out_a")
    dob = _mm(dh1, w_bot, tb=True, name="b_out_b")
    g_oe = jnp.concatenate([_mm(oa, dh1, ta=True, out_dtypes=(BF16,), name="b_w_out_a"),
                            _mm(ob, dh1, ta=True, out_dtypes=(BF16,), name="b_w_out_b")], axis=0)
    (dq, df, di, dg, dlb, g_gna), (r_dn0_a,) = _hgrn_bwd(proj, lb0, g_norm_a, st_a, doa, heads=ha, name="b_hgrn",
                                                        comm=(dn0_a,))
    (dqb, dkb, dvb, dab, dz, dhp, g_gnb), (r_dn0_b, r_up0_a) = _delta_bwd(
        qkv, ab, proj, hp, g_norm_b, st_b, dob, heads=hb, z_off=z_col // HEAD, name="b_delta", comm=(dn0_b, up0_a))
    (dqkv, g_conv), (r_oe,) = _conv_bwd(proj, conv_full, jnp.concatenate([dqb, dkb, dvb], axis=1), col_off=2 * d,
                                        name="b_conv", comm=(rows8(g_oe),))
    dproj = jnp.concatenate([dq, df, di, dg, dqkv, dz], axis=1)
    g_main, (r_up0_b,) = _mm(hn0, dproj, ta=True, out_dtypes=(BF16,), name="b_w_proj", comm=(up0_b,))
    r_dn0, r_up0 = [r_dn0_a, r_dn0_b], [r_up0_a, r_up0_b]
    g_tail = _mm(hn0, dab, ta=True, out_dtypes=(BF16,), name="b_w_ab")
    dhn0, (r_in,) = _mm(dproj, w_main, tb=True, name="b_proj",
                        comm=(cols8(jnp.concatenate([g_main, g_tail[:, :2 * hb]], axis=1)),))
    dhn0 = _mm(dab, w_tail, tb=True, extras=(dhn0,), epilogue=_add_epilogue, name="b_ab")
    (dx,), (g_nmix0,) = _rows_vjp(_rms_stage, [x2], [row(norm_mix, 0)], [dhn0], row_grads={0: F32}, adds={0: dh1},
                                  name="b_norm_mix0")
    g_lb = jnp.concatenate(_small_vjp(_lb0_stage, lb_rows, [dlb], name="b_lb0"), axis=0)

    late_grads = dict(
        norm_mix=jnp.concatenate([g_nmix0, g_nmix1], axis=0),
        norm_mlp=jnp.concatenate([gb0["norm_mlp"], gb1["norm_mlp"]], axis=0),
        norm_ple=jnp.concatenate([gb0["norm_ple"], gb1["norm_ple"]], axis=0),
        hgrn_lb=g_lb, g_norm_a=g_gna, a_log=dhp[0:1, :hb], dt_bias=dhp[1:2, :hb], g_norm_b=g_gnb, conv_w=g_conv)
    rep_names = ["norm_mix", "norm_mlp", "norm_ple", "hgrn_lb", "g_norm_a", "a_log", "dt_bias", "g_norm_b", "s5_a_re",
                 "s5_a_im", "s5_b_re", "s5_b_im", "s5_c_re", "s5_c_im", "s5_log_dt", "final_norm"]
    late_parts = _comm_call(_Gather(_pack(list(late_grads.values()))), name="ag_small_grads")
    summed = {}
    for tag, grads, parts in (("early", early_grads, early_parts), ("late", late_grads, late_parts)):
        sums = _unpack(_sum_parts(parts, name=f"sum_small_grads_{tag}"), [g.shape for g in grads.values()])
        summed.update(zip(grads, sums))
    cw = conv_w.shape[2]
    dshard = d // N_DEV
    shard_g = dict(conv_w=lax.dynamic_slice(summed["conv_w"], (0, me * cw), (CONV_WIDTH, cw))[None],
                   s5_d=lax.dynamic_slice(summed["s5_d"], (0, me * dshard), (1, dshard)),
                   b_glu=lax.dynamic_slice(summed["b_glu"], (0, me * dshard), (1, dshard)))
    small_names = rep_names + ["conv_w", "s5_d", "b_glu"]
    g_small = [summed[k] if k in rep_names else shard_g[k] for k in small_names]
    shapes = [args[k].shape for k in small_names]
    sm_out = _adamw(_pack([args[k] for k in small_names])[None], [_pack(g_small)[None]],
                    _pack([args["m_" + k] for k in small_names])[None], _pack([args["v_" + k] for k in small_names])[None],
                    name="adamw_small")
    sm_out = [dict(zip(small_names, _unpack(o[0], shapes))) for o in sm_out]

    received = dict(w_in_e=[r_in], w_out_e=[r_oe], w_glu=[r_gl], w_out_o=[r_oo], w_up=[r_up0, r_up1],
                    w_down=[r_dn0, r_dn1], w_ple_gate=[gb0["w_ple_gate"], gb1["w_ple_gate"]],
                    w_ple_proj=[gb0["w_ple_proj"], gb1["w_ple_proj"]])
    big_out = {k: _adamw(args[k], layers, args["m_" + k], args["v_" + k], name="adamw_" + k)
               for k, layers in received.items()}

    names = ["norm_mix", "norm_mlp", "norm_ple", "w_in_e", "w_out_e", "hgrn_lb", "g_norm_a", "conv_w", "a_log", "dt_bias",
             "g_norm_b", "s5_a_re", "s5_a_im", "s5_b_re", "s5_b_im", "s5_c_re", "s5_c_im", "s5_d", "s5_log_dt", "w_glu",
             "b_glu", "w_out_o", "w_up", "w_down", "w_ple_gate", "w_ple_proj", "final_norm"]
    result = [loss, dx[None]]
    for j in range(4):
        result += [big_out[k][j] if k in big_out else sm_out[j][k] for k in names]
    return tuple(result)
```

```python
import functools
import math
import operator

import numpy as np
import jax
import jax.numpy as jnp
from jax import lax
from jax.experimental import pallas as pl
from jax.experimental.pallas import tpu as pltpu

F32 = jnp.float32
BF16 = jnp.bfloat16
MM_DTYPE = BF16
HI = lax.Precision.HIGHEST
MESH = pl.DeviceIdType.MESH

NORM_EPS = 1e-6
CHUNK = 64
HEAD = 128
CONV_WIDTH = 4
S5_GROUP = 16
S5_STATE = 64
S5_GB = 8
S5_HALF = S5_GB * S5_STATE
N_DEV = 8
HEADS_PER_STEP = 8
HGRN_SUB = 16
ADAM_LR, ADAM_B1, ADAM_B2, ADAM_EPS, ADAM_WD, ADAM_STEP = 0.001, 0.9, 0.999, 1e-08, 0.01, 10
VMEM_LIMIT = 56 * 1024 * 1024

NN = (((1,), (0,)), ((), ()))
NT = (((1,), (1,)), ((), ()))
TN = (((0,), (0,)), ((), ()))


def _dot(a, b, dn=NN):
    return lax.dot_general(a, b, dn, precision=HI, preferred_element_type=F32)


def _hdot(a, b, dn=NN):
    return lax.dot_general(a, b, dn, precision=lax.Precision.HIGH, preferred_element_type=F32)


def _bdot_raw(a, b, dn=NN):
    return lax.dot_general(a.astype(BF16), b.astype(BF16), dn, preferred_element_type=F32)


@functools.partial(jax.custom_vjp, nondiff_argnums=(2,))
def _bdot(a, b, dn):
    return _bdot_raw(a, b, dn)


def _bdot_fwd(a, b, dn):
    return _bdot_raw(a, b, dn), (a, b)


def _bdot_bwd(dn, res, g):
    a, b = res
    if dn == NN:
        return _bdot_raw(g, b, NT), _bdot_raw(a, g, TN)
    if dn == NT:
        return _bdot_raw(g, b, NN), _bdot_raw(g, a, TN)
    assert dn == TN
    return _bdot_raw(b, g, NT), _bdot_raw(a, g, NN)


_bdot.defvjp(_bdot_fwd, _bdot_bwd)


def _per_head(f):
    def g(*args, **kw):
        n = [len(a.vals) for a in args if isinstance(a, _Heads)]
        if not n:
            return f(*args, **kw)
        return _Heads([f(*[a.vals[j] if isinstance(a, _Heads) else a for a in args], **kw) for j in range(n[0])])
    return g


class _Heads:
    def __init__(self, vals):
        self.vals = list(vals)

    def __add__(self, o):
        return _per_head(operator.add)(self, o)

    def __radd__(self, o):
        return _per_head(operator.add)(o, self)

    def __sub__(self, o):
        return _per_head(operator.sub)(self, o)

    def __rsub__(self, o):
        return _per_head(operator.sub)(o, self)

    def __mul__(self, o):
        return _per_head(operator.mul)(self, o)

    def __rmul__(self, o):
        return _per_head(operator.mul)(o, self)

    def __neg__(self):
        return _per_head(operator.neg)(self)


_exp, _log, _where, _sum, _mean = (_per_head(f) for f in (jnp.exp, jnp.log, jnp.where, jnp.sum, jnp.mean))
_sigmoid, _rsqrt, _equal = _per_head(jax.nn.sigmoid), _per_head(lax.rsqrt), _per_head(operator.eq)
_hdot_h, _bdot_h = _per_head(_hdot), _per_head(_bdot)
_rows = _per_head(lambda a, lo, n: a[lo:lo + n, :])
_row_concat = _per_head(lambda *xs: jnp.concatenate(xs, axis=0))


def _params(n_axes):
    return pltpu.CompilerParams(dimension_semantics=("arbitrary",) * n_axes, vmem_limit_bytes=VMEM_LIMIT)


def _full_spec(a):
    nd = a.ndim
    return pl.BlockSpec(a.shape, lambda *_: (0,) * nd)


def _mm(a, b, *, name, ta=False, tb=False, extras=(), epilogue=None, out_dtypes=(F32,), tm=1024, tn=1024, tk=2048,
        out_slots=False, comm=()):
    m = a.shape[1] if ta else a.shape[0]
    k = a.shape[0] if ta else a.shape[1]
    n = b.shape[0] if tb else b.shape[1]
    assert k == (b.shape[1] if tb else b.shape[0]), (name, a.shape, b.shape)
    tm, tn, tk = min(tm, m), min(tn, n), min(tk, k)
    assert m % tm == 0 and n % tn == 0 and k % tk == 0, (name, m, n, k)
    nk = k // tk
    n_ex, n_out = len(extras), len(out_dtypes)
    dn = (((0 if ta else 1,), (1 if tb else 0,)), ((), ()))

    def body(a_ref, b_ref, *rest):
        ex_refs, out_refs = rest[:n_ex], rest[n_ex:n_ex + n_out]
        part = lax.dot_general(a_ref[...].astype(MM_DTYPE), b_ref[...].astype(MM_DTYPE), dn, preferred_element_type=F32)

        def finish(acc):
            outs = epilogue(acc, *[r[...] for r in ex_refs]) if epilogue is not None else (acc,)
            for o_ref, o in zip(out_refs, outs):
                o_ref[...] = o.astype(o_ref.dtype)

        if nk == 1:
            finish(part)
            return
        acc_ref = rest[-1]
        kk = pl.program_id(2)

        @pl.when(kk == 0)
        def _():
            acc_ref[...] = part

        @pl.when((kk > 0) & (kk < nk - 1))
        def _():
            acc_ref[...] += part

        @pl.when(kk == nk - 1)
        def _():
            finish(acc_ref[...] + part)

    a_spec = pl.BlockSpec((tk, tm), lambda i, j, q: (q, i)) if ta else pl.BlockSpec((tm, tk), lambda i, j, q: (i, q))
    b_spec = pl.BlockSpec((tn, tk), lambda i, j, q: (j, q)) if tb else pl.BlockSpec((tk, tn), lambda i, j, q: (q, j))
    ex_specs = []
    for e in extras:
        if e.shape[0] == 1 and m != 1:
            ex_specs.append(pl.BlockSpec((1, tn), lambda i, j, q: (0, j)))
        else:
            ex_specs.append(pl.BlockSpec((tm, tn), lambda i, j, q: (i, j)))
    if out_slots:
        out_spec, out_dims = pl.BlockSpec((None, tm, tn), lambda i, j, q: (j, i, 0)), (n // tn, m, tn)
    else:
        out_spec, out_dims = pl.BlockSpec((tm, tn), lambda i, j, q: (i, j)), (m, n)
    outs, exchanged = _call(
        body, (a, b, *extras), name=name, grid=(m // tm, n // tn, nk),
        in_specs=[a_spec, b_spec] + ex_specs,
        out_specs=[out_spec for _ in out_dtypes],
        out_shape=[jax.ShapeDtypeStruct(out_dims, dt) for dt in out_dtypes],
        scratch_shapes=[pltpu.VMEM((tm, tn), F32)] if nk > 1 else [], comm=comm)
    outs = outs[0] if n_out == 1 else tuple(outs)
    return (outs, exchanged) if comm else outs


def _rows_call(fn, rows, consts, out_dtypes, *, name, tr=256):
    s = rows[0].shape[0]
    tr = min(tr, s)
    nr, nc = len(rows), len(consts)
    widths = [o.shape[1] for o in jax.eval_shape(
        fn, *[jax.ShapeDtypeStruct((tr, r.shape[1]), F32) for r in rows],
        *[jax.ShapeDtypeStruct(c.shape, F32) for c in consts])]

    def body(*refs):
        rv = [r[...].astype(F32) for r in refs[:nr]]
        cv = [c[...] for c in refs[nr:nr + nc]]
        for o_ref, o in zip(refs[nr + nc:], fn(*rv, *cv)):
            o_ref[...] = o.astype(o_ref.dtype)

    outs = pl.pallas_call(
        body, name=name, grid=(s // tr,),
        in_specs=[pl.BlockSpec((tr, r.shape[1]), lambda i: (i, 0)) for r in rows] + [_full_spec(c) for c in consts],
        out_specs=[pl.BlockSpec((tr, w), lambda i: (i, 0)) for w in widths],
        out_shape=[jax.ShapeDtypeStruct((s, w), dt) for w, dt in zip(widths, out_dtypes)],
        compiler_params=_params(1),
    )(*rows, *consts)
    return outs[0] if len(outs) == 1 else tuple(outs)


def _rows_vjp(fn, rows, consts, cots, *, name, row_grads, adds=None, tr=256):
    adds = adds or {}
    s = rows[0].shape[0]
    tr = min(tr, s)
    nr, nc, nt = len(rows), len(consts), len(cots)
    rg = sorted(row_grads)
    ad = sorted(adds)

    def body(*refs):
        rv = [r[...].astype(F32) for r in refs[:nr]]
        cv = [c[...] for c in refs[nr:nr + nc]]
        ct = [c[...].astype(F32) for c in refs[nr + nc:nr + nc + nt]]
        av = {i: r[...].astype(F32) for i, r in zip(ad, refs[nr + nc + nt:nr + nc + nt + len(ad)])}
        out_refs = refs[nr + nc + nt + len(ad):]
        _, vjp = jax.vjp(fn, *rv, *cv)
        grads = vjp(tuple(ct))
        for o_ref, i in zip(out_refs[:len(rg)], rg):
            g = grads[i]
            if i in av:
                g = g + av[i]
            o_ref[...] = g.astype(o_ref.dtype)

        @pl.when(pl.program_id(0) == 0)
        def _():
            for o_ref in out_refs[len(rg):]:
                o_ref[...] = jnp.zeros_like(o_ref)

        for o_ref, g in zip(out_refs[len(rg):], grads[nr:]):
            o_ref[...] += g

    row_spec = lambda a: pl.BlockSpec((tr, a.shape[1]), lambda i: (i, 0))
    outs = pl.pallas_call(
        body, name=name, grid=(s // tr,),
        in_specs=[row_spec(r) for r in rows] + [_full_spec(c) for c in consts] + [row_spec(c) for c in cots]
        + [row_spec(adds[i]) for i in ad],
        out_specs=[row_spec(rows[i]) for i in rg] + [_full_spec(c) for c in consts],
        out_shape=[jax.ShapeDtypeStruct(rows[i].shape, row_grads[i]) for i in rg]
        + [jax.ShapeDtypeStruct(c.shape, F32) for c in consts],
        compiler_params=_params(1),
    )(*rows, *consts, *cots, *[adds[i] for i in ad])
    return list(outs[:len(rg)]), list(outs[len(rg):])


def _small_call(fn, ins, *, name):
    shapes = jax.eval_shape(fn, *[jax.ShapeDtypeStruct(a.shape, F32) for a in ins])

    def body(*refs):
        for o_ref, o in zip(refs[len(ins):], fn(*[r[...] for r in refs[:len(ins)]])):
            o_ref[...] = o

    return pl.pallas_call(
        body, name=name, in_specs=[_full_spec(a) for a in ins],
        out_specs=[pl.BlockSpec(o.shape, functools.partial(lambda nd, *_: (0,) * nd, len(o.shape))) for o in shapes],
        out_shape=[jax.ShapeDtypeStruct(o.shape, F32) for o in shapes], grid=(1,),
        compiler_params=_params(1),
    )(*ins)


def _small_vjp(fn, ins, cots, *, name):
    def body(*refs):
        vals = [r[...] for r in refs[:len(ins)]]
        ct = [r[...] for r in refs[len(ins):len(ins) + len(cots)]]
        _, vjp = jax.vjp(fn, *vals)
        for o_ref, g in zip(refs[len(ins) + len(cots):], vjp(tuple(ct))):
            o_ref[...] = g

    return pl.pallas_call(
        body, name=name, in_specs=[_full_spec(a) for a in ins] + [_full_spec(c) for c in cots],
        out_specs=[_full_spec(a) for a in ins],
        out_shape=[jax.ShapeDtypeStruct(a.shape, F32) for a in ins], grid=(1,),
        compiler_params=_params(1),
    )(*ins, *cots)


def _rms(x, g):
    return x * _rsqrt(_mean(x * x, axis=-1, keepdims=True) + NORM_EPS) * g


def _rms_stage(x, g):
    return (_rms(x, g),)


def _silu(x):
    return x * _sigmoid(x)


def _softplus(x):
    return jnp.maximum(x, 0.0) + jnp.log1p(jnp.exp(-jnp.abs(x)))


def _gelu(x):
    return jax.nn.gelu(x, approximate=True)


def _gelu_stage(y):
    return (_gelu(y),)


def _glu_stage(y, gl_raw, b):
    return (_gelu(y) * jax.nn.sigmoid(gl_raw + b),)


def _ple_stage(h, gpre, pp):
    return (h + jax.nn.sigmoid(gpre) * pp,)


def _relu2_grad_epilogue(acc, up):
    return (acc * (2.0 * jnp.maximum(up, 0.0)),)


def _lb0_stage(x0, x1, x2):
    mx = jnp.maximum(jnp.maximum(x0, x1), x2)
    e0, e1, e2 = jnp.exp(x0 - mx), jnp.exp(x1 - mx), jnp.exp(x2 - mx)
    return (e0 / (e0 + e1 + e2),)


def _s5_prep_stage(a_re, a_im, log_dt, b_re, b_im, expand):
    step = jnp.exp(log_dt)
    mag = jnp.exp(a_re * step)
    lr = mag * jnp.cos(a_im * step)
    li = mag * jnp.sin(a_im * step)
    den = a_re * a_re + a_im * a_im
    cr = ((lr - 1.0) * a_re + li * a_im) / den
    ci = (li * a_re - (lr - 1.0) * a_im) / den
    cr_e, ci_e = _dot(cr, expand), _dot(ci, expand)
    return lr, li, cr_e * b_re - ci_e * b_im, cr_e * b_im + ci_e * b_re


def _loss_call(h, g, target, *, name, tr=256):
    s, d = h.shape
    tr = min(tr, s)

    def loss_fn(hv, gv, tv):
        err = _rms(hv, gv) - tv
        return 0.5 * jnp.sum(jnp.mean(err * err, axis=-1))

    def body(h_ref, g_ref, t_ref, dh_ref, dg_ref, loss_ref):
        val, (dh, dg) = jax.value_and_grad(loss_fn, argnums=(0, 1))(h_ref[...], g_ref[...], t_ref[...])
        dh_ref[...] = dh

        @pl.when(pl.program_id(0) == 0)
        def _():
            dg_ref[...] = jnp.zeros_like(dg_ref)
            loss_ref[...] = jnp.zeros_like(loss_ref)

        dg_ref[...] += dg
        loss_ref[...] += jnp.full(loss_ref.shape, val, F32)

    row = pl.BlockSpec((tr, d), lambda i: (i, 0))
    return pl.pallas_call(
        body, name=name, grid=(s // tr,),
        in_specs=[row, _full_spec(g), row],
        out_specs=[row, _full_spec(g), pl.BlockSpec((8, 128), lambda i: (0, 0))],
        out_shape=[jax.ShapeDtypeStruct((s, d), F32), jax.ShapeDtypeStruct(g.shape, F32),
                   jax.ShapeDtypeStruct((8, 128), F32)],
        compiler_params=_params(1),
    )(h, g, target)


def _hgrn_chunk(q, fp, iv, gp, lb, gn, st_t):
    c = CHUNK
    row = lax.broadcasted_iota(jnp.int32, (c, c), 0)
    col = lax.broadcasted_iota(jnp.int32, (c, c), 1)
    causal = row >= col
    fg = lb + (1.0 - lb) * _sigmoid(fp)
    k = 1.0 - fg
    lf = _log(fg)
    cum = _hdot_h(causal.astype(F32), lf, NN)
    cend = _sum(lf, axis=0, keepdims=True)
    shift = HGRN_SUB.bit_length() - 1
    ref = _hdot_h(((row >> shift) > (col >> shift)).astype(F32), lf, NN)
    q_dec = q * _exp(cum - ref)
    key_row = lax.broadcasted_iota(jnp.int32, (c, 1), 0)
    blocks = []
    for lo in range(0, c, HGRN_SUB):
        live = key_row < lo + HGRN_SUB
        k_dec = _where(live, k * _exp(_where(live, _rows(ref, lo, 1) - cum, 0.0)), 0.0)
        blocks.append(_hdot_h(_rows(q_dec, lo, HGRN_SUB), k_dec, NT))
    scores = _where(causal, _row_concat(*blocks), 0.0)
    out = _bdot_h(scores, iv, NN) + _bdot_h(q * _exp(cum), st_t, NT)
    st_new = st_t * _exp(cend) + _bdot_h(iv, k * _exp(cend - cum), TN)
    res = _rms(out, gn) * _silu(gp)
    return res, st_new


def _hgrn_heads(qs, fs, ivs, gs, lbs, gn, sts):
    res, st_new = _hgrn_chunk(_Heads(qs), _Heads(fs), _Heads(ivs), _Heads(gs), _Heads(lbs), gn, _Heads(sts))
    return res.vals, st_new.vals


def _lanes(j):
    return slice(j * HEAD, (j + 1) * HEAD)


def _hgrn_fwd(proj, lb, gn, *, heads, name, comm=()):
    s = proj.shape[0]
    n = s // CHUNK
    hpb = min(HEADS_PER_STEP, heads)
    assert heads % hpb == 0

    def body(q_ref, f_ref, i_ref, g_ref, lb_ref, gn_ref, o_ref, st_ref, state):
        @pl.when(pl.program_id(1) == 0)
        def _():
            state[...] = jnp.zeros_like(state)

        gnv = gn_ref[...]
        loaded = [(q_ref[:, _lanes(j)], f_ref[:, _lanes(j)], i_ref[:, _lanes(j)], g_ref[:, _lanes(j)],
                   lb_ref[:, _lanes(j)], state[j]) for j in range(hpb)]
        qs, fs, ivs, gs, lbs, sts = (list(t) for t in zip(*loaded))
        res, st_new = _hgrn_heads(qs, fs, ivs, gs, lbs, gnv, sts)
        for j in range(hpb):
            st_ref[j] = sts[j]
            o_ref[:, _lanes(j)] = res[j].astype(o_ref.dtype)
            state[j] = st_new[j]

    wide = hpb * HEAD
    blk = lambda off: pl.BlockSpec((CHUNK, wide), lambda h, c: (c, off // hpb + h))
    return _call(
        body, (proj, proj, proj, proj, lb, gn), name=name, grid=(heads // hpb, n),
        in_specs=[blk(0), blk(heads), blk(2 * heads), blk(3 * heads),
                  pl.BlockSpec((1, wide), lambda h, c: (0, h)), pl.BlockSpec((1, HEAD), lambda h, c: (0, 0))],
        out_specs=[pl.BlockSpec((CHUNK, wide), lambda h, c: (c, h)),
                   pl.BlockSpec((hpb, None, HEAD, HEAD), lambda h, c: (h, c, 0, 0))],
        out_shape=[jax.ShapeDtypeStruct((s, heads * HEAD), BF16), jax.ShapeDtypeStruct((heads, n, HEAD, HEAD), F32)],
        scratch_shapes=[pltpu.VMEM((hpb, HEAD, HEAD), F32)], comm=comm)


def _hgrn_bwd(proj, lb, gn, states, d_out, *, heads, name, comm=()):
    s = proj.shape[0]
    n = s // CHUNK
    hpb = min(HEADS_PER_STEP, heads)

    def body(q_ref, f_ref, i_ref, g_ref, lb_ref, gn_ref, st_ref, do_ref,
             dq_ref, df_ref, di_ref, dg_ref, dlb_ref, dgn_ref, dstate):
        h, c = pl.program_id(0), pl.program_id(1)

        @pl.when(c == 0)
        def _():
            dstate[...] = jnp.zeros_like(dstate)
            dlb_ref[...] = jnp.zeros_like(dlb_ref)

        @pl.when((c == 0) & (h == 0))
        def _():
            dgn_ref[...] = jnp.zeros_like(dgn_ref)

        gnv = gn_ref[...]
        loaded = [(q_ref[:, _lanes(j)], f_ref[:, _lanes(j)], i_ref[:, _lanes(j)], g_ref[:, _lanes(j)],
                   lb_ref[:, _lanes(j)], st_ref[j], do_ref[:, _lanes(j)].astype(F32), dstate[j]) for j in range(hpb)]
        qs, fs, ivs, gs, lbs, sts, dos, dss = (list(t) for t in zip(*loaded))
        _, vjp = jax.vjp(_hgrn_heads, qs, fs, ivs, gs, lbs, gnv, sts)
        dqs, dfs, dis, dgs, dlbs, dgn_sum, dsts = vjp((dos, dss))
        for j in range(hpb):
            ln = _lanes(j)
            dq_ref[:, ln] = dqs[j].astype(dq_ref.dtype)
            df_ref[:, ln] = dfs[j].astype(df_ref.dtype)
            di_ref[:, ln] = dis[j].astype(di_ref.dtype)
            dg_ref[:, ln] = dgs[j].astype(dg_ref.dtype)
            dlb_ref[:, ln] += dlbs[j]
            dstate[j] = dsts[j]
        dgn_ref[...] += dgn_sum

    wide = hpb * HEAD
    rev = lambda off: pl.BlockSpec((CHUNK, wide), lambda h, c: (n - 1 - c, off // hpb + h))
    out_blk = pl.BlockSpec((CHUNK, wide), lambda h, c: (n - 1 - c, h))
    width = heads * HEAD
    return _call(
        body, (proj, proj, proj, proj, lb, gn, states, d_out), name=name, grid=(heads // hpb, n),
        in_specs=[rev(0), rev(heads), rev(2 * heads), rev(3 * heads),
                  pl.BlockSpec((1, wide), lambda h, c: (0, h)), pl.BlockSpec((1, HEAD), lambda h, c: (0, 0)),
                  pl.BlockSpec((hpb, None, HEAD, HEAD), lambda h, c: (h, n - 1 - c, 0, 0)), out_blk],
        out_specs=[out_blk, out_blk, out_blk, out_blk,
                   pl.BlockSpec((1, wide), lambda h, c: (0, h)), pl.BlockSpec((1, HEAD), lambda h, c: (0, 0))],
        out_shape=[jax.ShapeDtypeStruct((s, width), BF16)] * 4
        + [jax.ShapeDtypeStruct((1, width), F32), jax.ShapeDtypeStruct((1, HEAD), F32)],
        scratch_shapes=[pltpu.VMEM((hpb, HEAD, HEAD), F32)], comm=comm)


def _shift_rows(x, d, rowi):
    if d == 0:
        return x
    n = x.shape[0]
    rolled = pltpu.roll(x, d % n, 0)
    keep = rowi >= d if d > 0 else rowi < n + d
    return jnp.where(keep, rolled, 0.0)


def _conv_pre(x, w_ref, rowi):
    acc = None
    for j in range(CONV_WIDTH):
        term = w_ref[j:j + 1, :] * _shift_rows(x, CONV_WIDTH - 1 - j, rowi)
        acc = term if acc is None else acc + term
    return acc


def _conv_fwd(proj, w, *, col_off, name, cb=256):
    s = proj.shape[0]
    width = w.shape[1]
    cb = min(cb, width)

    def body(x_ref, w_ref, o_ref):
        rowi = lax.broadcasted_iota(jnp.int32, (s, cb), 0)
        o_ref[...] = _silu(_conv_pre(x_ref[...], w_ref, rowi))

    return pl.pallas_call(
        body, name=name, grid=(width // cb,),
        in_specs=[pl.BlockSpec((s, cb), lambda j: (0, col_off // cb + j)), pl.BlockSpec((CONV_WIDTH, cb), lambda j: (0, j))],
        out_specs=pl.BlockSpec((s, cb), lambda j: (0, j)),
        out_shape=jax.ShapeDtypeStruct((s, width), F32),
        compiler_params=_params(1),
    )(proj, w)


def _conv_bwd(proj, w, d_out, *, col_off, name, cb=256, comm=()):
    s = proj.shape[0]
    width = w.shape[1]
    cb = min(cb, width)

    def body(x_ref, w_ref, do_ref, dx_ref, dw_ref):
        rowi = lax.broadcasted_iota(jnp.int32, (s, cb), 0)
        x = x_ref[...]
        pre = _conv_pre(x, w_ref, rowi)
        sg = jax.nn.sigmoid(pre)
        dpre = do_ref[...] * (sg + pre * sg * (1.0 - sg))
        dx = None
        for j in range(CONV_WIDTH):
            d = CONV_WIDTH - 1 - j
            term = w_ref[j:j + 1, :] * _shift_rows(dpre, -d, rowi)
            dx = term if dx is None else dx + term
            dw_ref[j:j + 1, :] = jnp.sum(dpre * _shift_rows(x, d, rowi), axis=0, keepdims=True)
        dx_ref[...] = dx.astype(dx_ref.dtype)

    return _call(
        body, (proj, w, d_out), name=name, grid=(width // cb,),
        in_specs=[pl.BlockSpec((s, cb), lambda j: (0, col_off // cb + j)), pl.BlockSpec((CONV_WIDTH, cb), lambda j: (0, j)),
                  pl.BlockSpec((s, cb), lambda j: (0, j))],
        out_specs=[pl.BlockSpec((s, cb), lambda j: (0, j)), pl.BlockSpec((CONV_WIDTH, cb), lambda j: (0, j))],
        out_shape=[jax.ShapeDtypeStruct((s, width), BF16), jax.ShapeDtypeStruct((CONV_WIDTH, width), F32)],
        comm=comm)


_lane_concat = _per_head(lambda a, b: jnp.concatenate([a, b], axis=1))
_lane_half = _per_head(lambda a, j: a[:, j * HEAD:(j + 1) * HEAD])


def _tri_inverse(lower):
    c = CHUNK
    row = lax.broadcasted_iota(jnp.int32, (c, c), 0)
    col = lax.broadcasted_iota(jnp.int32, (c, c), 1)
    inv = (row == col).astype(F32)
    lvl = 0
    while (1 << lvl) < c:
        same_pair = (row >> (lvl + 1)) == (col >> (lvl + 1))
        off_block = same_pair & (((row >> lvl) & 1) == 1) & (((col >> lvl) & 1) == 0)
        inv = inv - _hdot_h(_hdot_h(inv, _where(off_block, lower, 0.0), NN), inv, NN)
        lvl += 1
    return inv


@jax.custom_vjp
def _tri_solve(lowers, rhss):
    return _tri_solve_fwd(lowers, rhss)[0]


def _tri_solve_fwd(lowers, rhss):
    inv = _tri_inverse(_Heads(lowers))
    sol = _hdot_h(inv, _Heads(rhss), NN)
    return sol.vals, (inv.vals, sol.vals)


def _tri_solve_bwd(res, g):
    inv, sol = _Heads(res[0]), _Heads(res[1])
    d_rhs = _hdot_h(inv, _Heads(g), TN)
    return (-_hdot_h(d_rhs, sol, NT)).vals, d_rhs.vals


_tri_solve.defvjp(_tri_solve_fwd, _tri_solve_bwd)


def _solve(lower, rhs):
    if isinstance(lower, _Heads):
        return _Heads(_tri_solve(lower.vals, rhs.vals))
    return _tri_solve([lower], [rhs])[0]


def _delta_chunk(h, heads, qr, kr, vr, ab, zp, alog, dtb, gn, st):
    c = CHUNK
    row = lax.broadcasted_iota(jnp.int32, (c, c), 0)
    col = lax.broadcasted_iota(jnp.int32, (c, c), 1)
    causal = row >= col
    strict = row > col
    lane = lax.broadcasted_iota(jnp.int32, (c, HEAD), 1)
    mine = _equal(h, lane)
    la_full = -jnp.exp(alog) * _softplus(ab + dtb)
    cum_full = _hdot(causal.astype(F32), la_full)
    cum = _sum(_where(mine, cum_full, 0.0), axis=1, keepdims=True)
    cend = _sum(_sum(_where(mine, la_full, 0.0), axis=1, keepdims=True), axis=0, keepdims=True)
    beta = _sum(_where(_equal(heads + h, lane), jax.nn.sigmoid(ab), 0.0), axis=1, keepdims=True)
    cum_row = _hdot_h(_where(mine, 1.0, 0.0), cum_full, NT)
    decay = _where(causal, _exp(_where(causal, cum - cum_row, 0.0)), 0.0)
    qn = qr * _rsqrt(_sum(qr * qr, axis=-1, keepdims=True) + NORM_EPS) * (HEAD ** -0.5)
    kn = kr * _rsqrt(_sum(kr * kr, axis=-1, keepdims=True) + NORM_EPS)
    kb = kn * beta
    lower = _where(strict, _bdot_h(kb, kn, NT) * decay, 0.0)
    ecum = _exp(cum)
    sol = _solve(lower, _lane_concat(vr * beta, kb * ecum))
    u, w = _lane_half(sol, 0), _lane_half(sol, 1)
    intra = _bdot_h(qn, kn, NT) * decay
    v_new = u - _bdot_h(w, st, NN)
    out = _bdot_h(qn * ecum, st, NN) + _bdot_h(intra, v_new, NN)
    st_new = st * _exp(cend) + _bdot_h(kn * _exp(cend - cum), v_new, TN)
    res = _rms(out, gn) * _silu(zp)
    return res, st_new


def _delta_heads(hs, heads, qs, ks, vs, ab, zs, alog, dtb, gn, sts):
    res, st_new = _delta_chunk(_Heads(hs), heads, _Heads(qs), _Heads(ks), _Heads(vs), ab, _Heads(zs), alog, dtb, gn,
                               _Heads(sts))
    return res.vals, st_new.vals


def _delta_fwd(qkv, ab, proj, hp, gn, *, heads, z_off, name, comm=()):
    s = qkv.shape[0]
    n = s // CHUNK

    hpb = min(HEADS_PER_STEP, heads)
    assert heads % hpb == 0 and z_off % hpb == 0

    def body(q_ref, k_ref, v_ref, ab_ref, z_ref, hp_ref, gn_ref, o_ref, st_ref, state):
        hb = pl.program_id(1)

        @pl.when(pl.program_id(0) == 0)
        def _():
            for j in range(hpb):
                state[hb * hpb + j] = jnp.zeros((HEAD, HEAD), F32)

        shared = (ab_ref[...], hp_ref[0:1, :], hp_ref[1:2, :], gn_ref[...])
        loaded = [(q_ref[:, _lanes(j)], k_ref[:, _lanes(j)], v_ref[:, _lanes(j)], z_ref[:, _lanes(j)],
                   state[hb * hpb + j]) for j in range(hpb)]
        qs, ks, vs, zs, sts = (list(t) for t in zip(*loaded))
        res, st_new = _delta_heads([hb * hpb + j for j in range(hpb)], heads, qs, ks, vs, shared[0], zs, shared[1],
                                   shared[2], shared[3], sts)
        for j in range(hpb):
            st_ref[j] = sts[j]
            o_ref[:, _lanes(j)] = res[j].astype(o_ref.dtype)
            state[hb * hpb + j] = st_new[j]

    wide = hpb * HEAD
    blk = lambda off: pl.BlockSpec((CHUNK, wide), lambda c, h: (c, off // hpb + h))
    return _call(
        body, (qkv, qkv, qkv, ab, proj, hp, gn), name=name, grid=(n, heads // hpb),
        in_specs=[blk(0), blk(heads), blk(2 * heads), pl.BlockSpec((CHUNK, HEAD), lambda c, h: (c, 0)), blk(z_off),
                  pl.BlockSpec((8, HEAD), lambda c, h: (0, 0)), pl.BlockSpec((1, HEAD), lambda c, h: (0, 0))],
        out_specs=[pl.BlockSpec((CHUNK, wide), lambda c, h: (c, h)),
                   pl.BlockSpec((hpb, None, HEAD, HEAD), lambda c, h: (h, c, 0, 0))],
        out_shape=[jax.ShapeDtypeStruct((s, heads * HEAD), BF16), jax.ShapeDtypeStruct((heads, n, HEAD, HEAD), F32)],
        scratch_shapes=[pltpu.VMEM((heads, HEAD, HEAD), F32)], comm=comm)


def _delta_bwd(qkv, ab, proj, hp, gn, states, d_out, *, heads, z_off, name, comm=()):
    s = qkv.shape[0]
    n = s // CHUNK
    hpb = min(HEADS_PER_STEP, heads)

    def body(q_ref, k_ref, v_ref, ab_ref, z_ref, hp_ref, gn_ref, st_ref, do_ref,
             dq_ref, dk_ref, dv_ref, dab_ref, dz_ref, dhp_ref, dgn_ref, dstate):
        c, hb = pl.program_id(0), pl.program_id(1)

        @pl.when(c == 0)
        def _():
            for j in range(hpb):
                dstate[hb * hpb + j] = jnp.zeros((HEAD, HEAD), F32)

        @pl.when((c == 0) & (hb == 0))
        def _():
            dgn_ref[...] = jnp.zeros_like(dgn_ref)
            dhp_ref[...] = jnp.zeros_like(dhp_ref)

        @pl.when(hb == 0)
        def _():
            dab_ref[...] = jnp.zeros_like(dab_ref)

        shared = (ab_ref[...], hp_ref[0:1, :], hp_ref[1:2, :], gn_ref[...])
        loaded = [(q_ref[:, _lanes(j)], k_ref[:, _lanes(j)], v_ref[:, _lanes(j)], z_ref[:, _lanes(j)], st_ref[j],
                   do_ref[:, _lanes(j)].astype(F32), dstate[hb * hpb + j]) for j in range(hpb)]
        qs, ks, vs, zs, sts, dos, dss = (list(t) for t in zip(*loaded))
        fn = functools.partial(_delta_heads, [hb * hpb + j for j in range(hpb)], heads)
        _, vjp = jax.vjp(fn, qs, ks, vs, shared[0], zs, shared[1], shared[2], shared[3], sts)
        dqs, dks, dvs, dab, dzs, dal, ddt, dgn, dsts = vjp((dos, dss))
        for j in range(hpb):
            ln = _lanes(j)
            dq_ref[:, ln] = dqs[j]
            dk_ref[:, ln] = dks[j]
            dv_ref[:, ln] = dvs[j]
            dz_ref[:, ln] = dzs[j].astype(dz_ref.dtype)
            dstate[hb * hpb + j] = dsts[j]
        dab_ref[...] += dab
        dhp_ref[0:1, :] += dal
        dhp_ref[1:2, :] += ddt
        dgn_ref[...] += dgn

    wide = hpb * HEAD
    rev = lambda off: pl.BlockSpec((CHUNK, wide), lambda c, h: (n - 1 - c, off // hpb + h))
    width = heads * HEAD
    head_blk = pl.BlockSpec((CHUNK, wide), lambda c, h: (n - 1 - c, h))
    ab_blk = pl.BlockSpec((CHUNK, HEAD), lambda c, h: (n - 1 - c, 0))
    return _call(
        body, (qkv, qkv, qkv, ab, proj, hp, gn, states, d_out), name=name, grid=(n, heads // hpb),
        in_specs=[rev(0), rev(heads), rev(2 * heads), ab_blk, rev(z_off),
                  pl.BlockSpec((8, HEAD), lambda c, h: (0, 0)), pl.BlockSpec((1, HEAD), lambda c, h: (0, 0)),
                  pl.BlockSpec((hpb, None, HEAD, HEAD), lambda c, h: (h, n - 1 - c, 0, 0)), head_blk],
        out_specs=[head_blk, head_blk, head_blk, ab_blk, head_blk,
                   pl.BlockSpec((8, HEAD), lambda c, h: (0, 0)), pl.BlockSpec((1, HEAD), lambda c, h: (0, 0))],
        out_shape=[jax.ShapeDtypeStruct((s, width), F32)] * 3
        + [jax.ShapeDtypeStruct((s, HEAD), F32), jax.ShapeDtypeStruct((s, width), BF16),
           jax.ShapeDtypeStruct((8, HEAD), F32), jax.ShapeDtypeStruct((1, HEAD), F32)],
        scratch_shapes=[pltpu.VMEM((heads, HEAD, HEAD), F32)], comm=comm)


def _s5_scan(buf, lt_ref, cin_r, cin_i, tt, reverse):
    nblk = tt // 8
    hl = S5_HALF
    base = 8 if reverse else 0

    def body(j, carry):
        cr, ci = carry
        off = pl.multiple_of((nblk - 1 - j if reverse else j) * 8, 8)
        xr = buf[pl.ds(off, 8), 0:hl]
        xi = buf[pl.ds(off, 8), hl:2 * hl]
        for lv, d in enumerate((1, 2, 4)):
            ar, ai = lt_ref[base + 2 * lv], lt_ref[base + 2 * lv + 1]
            sr = pltpu.roll(xr, 8 - d if reverse else d, 0)
            si = pltpu.roll(xi, 8 - d if reverse else d, 0)
            xr, xi = xr + ar * sr - ai * si, xi + ar * si + ai * sr
        pr, pi = lt_ref[base + 6], lt_ref[base + 7]
        xr, xi = xr + pr * cr - pi * ci, xi + pr * ci + pi * cr
        buf[pl.ds(off, 8), 0:hl] = xr
        buf[pl.ds(off, 8), hl:2 * hl] = xi
        edge = 0 if reverse else 7
        return xr[edge:edge + 1, :], xi[edge:edge + 1, :]

    return lax.fori_loop(0, nblk, body, (cin_r, cin_i))


def _s5_fwd(u, wb, wc, lt, dskip, *, name, tt=1024, comm=()):
    s, d = u.shape
    nb = d // HEAD
    tt = min(tt, s)
    nt = s // tt
    hl = S5_HALF

    def body(u_ref, wb_ref, wc_ref, lt_ref, d_ref, y_ref, cin_ref, st_ref, buf, carry):
        @pl.when(pl.program_id(1) == 0)
        def _():
            carry[...] = jnp.zeros_like(carry)

        cin_ref[...] = carry[0:1, :]
        uv = u_ref[...]
        buf[...] = _bdot_raw(uv, wb_ref[...])
        cr, ci = _s5_scan(buf, lt_ref, carry[0:1, 0:hl], carry[0:1, hl:2 * hl], tt, False)
        carry[0:1, 0:hl] = cr
        carry[0:1, hl:2 * hl] = ci
        states = buf[...].astype(BF16)
        st_ref[...] = states
        y_ref[...] = _bdot_raw(states, wc_ref[...]) + d_ref[...] * uv

    return _call(
        body, (u, wb, wc, lt, dskip), name=name, grid=(nb, nt),
        in_specs=[pl.BlockSpec((tt, HEAD), lambda b, t: (t, b)),
                  pl.BlockSpec((None, HEAD, 2 * hl), lambda b, t: (b, 0, 0)),
                  pl.BlockSpec((None, 2 * hl, HEAD), lambda b, t: (b, 0, 0)),
                  pl.BlockSpec((None, 16, 8, hl), lambda b, t: (b, 0, 0, 0)),
                  pl.BlockSpec((1, HEAD), lambda b, t: (0, b))],
        out_specs=[pl.BlockSpec((tt, HEAD), lambda b, t: (t, b)),
                   pl.BlockSpec((None, None, 1, 2 * hl), lambda b, t: (b, t, 0, 0)),
                   pl.BlockSpec((tt, 2 * hl), lambda b, t: (t, b))],
        out_shape=[jax.ShapeDtypeStruct((s, d), F32), jax.ShapeDtypeStruct((nb, nt, 1, 2 * hl), F32),
                   jax.ShapeDtypeStruct((s, nb * 2 * hl), BF16)],
        scratch_shapes=[pltpu.VMEM((tt, 2 * hl), F32), pltpu.VMEM((8, 2 * hl), F32)], comm=comm)


def _s5_bwd(u, dy, wb, wc, lt, dskip, cins, states, *, name, tt=1024, comm=()):
    s, d = u.shape
    nb = d // HEAD
    tt = min(tt, s)
    nt = s // tt
    hl = S5_HALF

    def body(u_ref, dy_ref, wb_ref, wc_ref, lt_ref, d_ref, cin_ref, st_ref,
             du_ref, dwb_ref, dwc_ref, dd_ref, dlam_ref, abuf, acarry):
        @pl.when(pl.program_id(1) == 0)
        def _():
            acarry[...] = jnp.zeros_like(acarry)
            dwb_ref[...] = jnp.zeros_like(dwb_ref)
            dwc_ref[...] = jnp.zeros_like(dwc_ref)
            dd_ref[...] = jnp.zeros_like(dd_ref)
            dlam_ref[...] = jnp.zeros_like(dlam_ref)

        uv, dyv = u_ref[...], dy_ref[...]
        dy16 = dyv.astype(BF16)
        abuf[...] = _bdot_raw(dy16, wc_ref[...], NT)
        ar, ai = _s5_scan(abuf, lt_ref, acarry[0:1, 0:hl], acarry[0:1, hl:2 * hl], tt, True)
        acarry[0:1, 0:hl] = ar
        acarry[0:1, hl:2 * hl] = ai
        adj16 = abuf[...].astype(BF16)
        du_ref[...] = _bdot_raw(adj16, wb_ref[...], NT) + d_ref[...] * dyv
        dwb_ref[...] += _bdot_raw(uv, adj16, TN)
        dwc_ref[...] += _bdot_raw(st_ref[...], dy16, TN)
        dd_ref[...] += jnp.sum(dyv * uv, axis=0, keepdims=True)
        first = lax.broadcasted_iota(jnp.int32, (tt, hl), 0) == 0
        spr = jnp.where(first, cin_ref[:, 0:hl], pltpu.roll(st_ref[:, 0:hl].astype(F32), 1, 0))
        spi = jnp.where(first, cin_ref[:, hl:2 * hl], pltpu.roll(st_ref[:, hl:2 * hl].astype(F32), 1, 0))
        avr, avi = abuf[:, 0:hl], abuf[:, hl:2 * hl]
        dlam_ref[:, 0:hl] += jnp.sum(avr * spr + avi * spi, axis=0, keepdims=True)
        dlam_ref[:, hl:2 * hl] += jnp.sum(avi * spr - avr * spi, axis=0, keepdims=True)

    rev = pl.BlockSpec((tt, HEAD), lambda b, t: (nt - 1 - t, b))
    return _call(
        body, (u, dy, wb, wc, lt, dskip, cins, states), name=name, grid=(nb, nt),
        in_specs=[rev, rev,
                  pl.BlockSpec((None, HEAD, 2 * hl), lambda b, t: (b, 0, 0)),
                  pl.BlockSpec((None, 2 * hl, HEAD), lambda b, t: (b, 0, 0)),
                  pl.BlockSpec((None, 16, 8, hl), lambda b, t: (b, 0, 0, 0)),
                  pl.BlockSpec((1, HEAD), lambda b, t: (0, b)),
                  pl.BlockSpec((None, None, 1, 2 * hl), lambda b, t: (b, nt - 1 - t, 0, 0)),
                  pl.BlockSpec((tt, 2 * hl), lambda b, t: (nt - 1 - t, b))],
        out_specs=[rev,
                   pl.BlockSpec((None, HEAD, 2 * hl), lambda b, t: (b, 0, 0)),
                   pl.BlockSpec((None, 2 * hl, HEAD), lambda b, t: (b, 0, 0)),
                   pl.BlockSpec((1, HEAD), lambda b, t: (0, b)),
                   pl.BlockSpec((None, 1, 2 * hl), lambda b, t: (b, 0, 0))],
        out_shape=[jax.ShapeDtypeStruct((s, d), F32), jax.ShapeDtypeStruct(wb.shape, F32),
                   jax.ShapeDtypeStruct(wc.shape, F32), jax.ShapeDtypeStruct((1, d), F32),
                   jax.ShapeDtypeStruct((nb, 1, 2 * hl), F32)],
        scratch_shapes=[pltpu.VMEM((tt, 2 * hl), F32), pltpu.VMEM((8, 2 * hl), F32)],
        comm=comm)


def _s5_pack(lr, li, br, bi, c_re, c_im):
    g = lr.shape[0]
    nb = g // S5_GB
    eye = jnp.eye(S5_GB, dtype=F32)
    bm = jnp.stack([br, bi]).reshape(2, nb, S5_GB, S5_STATE, S5_GROUP)
    wb = jnp.einsum("rbgpc,gh->bgcrhp", bm, eye).reshape(nb, HEAD, 2 * S5_HALF)
    cm = jnp.stack([c_re, -c_im]).reshape(2, nb, S5_GB, S5_GROUP, S5_STATE)
    wc = jnp.einsum("rbgcp,gh->brgphc", cm, eye).reshape(nb, 2 * S5_HALF, HEAD)
    pw = [(lr, li)]
    for _ in range(7):
        pr, pi = pw[-1]
        pw.append((pr * lr - pi * li, pr * li + pi * lr))
    blk = lambda a: a.reshape(nb, 1, S5_HALF)
    rows = jnp.arange(8).reshape(1, 8, 1)
    tables = []
    for conj, keep, order in ((1.0, lambda n: rows >= n, range(8)), (-1.0, lambda n: rows < 8 - n, range(7, -1, -1))):
        for n in (1, 2, 4):
            tables += [jnp.where(keep(n), blk(pw[n - 1][0]), 0.0), jnp.where(keep(n), conj * blk(pw[n - 1][1]), 0.0)]
        tables += [jnp.concatenate([blk(pw[n][0]) for n in order], axis=1),
                   jnp.concatenate([conj * blk(pw[n][1]) for n in order], axis=1)]
    return wb, wc, jnp.stack(tables, axis=1)


def _s5_unpack(dwb, dwc, dlam):
    nb = dwb.shape[0]
    g = nb * S5_GB
    eye = jnp.eye(S5_GB, dtype=F32)
    db = jnp.einsum("bgcrhp,gh->rbgpc", dwb.reshape(nb, S5_GB, S5_GROUP, 2, S5_GB, S5_STATE), eye)
    db = db.reshape(2, g, S5_STATE * S5_GROUP)
    dc = jnp.einsum("brgphc,gh->rbgcp", dwc.reshape(nb, 2, S5_GB, S5_STATE, S5_GB, S5_GROUP), eye)
    dc = dc.reshape(2, g, S5_GROUP, S5_STATE)
    dl = dlam.reshape(nb, 2, S5_GB, S5_STATE).transpose(1, 0, 2, 3).reshape(2, g, S5_STATE)
    return dl[0], dl[1], db[0], db[1], dc[0], -dc[1]


def _peer(r):
    mx, my, mc = lax.axis_index("x"), lax.axis_index("y"), lax.axis_index("c")
    px = 1 - mx if r & 4 else mx
    py = 1 - my if r & 2 else my
    pc = 1 - mc if r & 1 else mc
    return (px, py, pc), 4 * px + 2 * py + pc


_COMM_SCRATCH = [pltpu.SemaphoreType.DMA((N_DEV - 1,)), pltpu.SemaphoreType.DMA((N_DEV - 1,)), pltpu.SemaphoreType.DMA]


class _AllToAll:
    def __init__(self, x, rows=None):
        self.x = x
        self.rows = rows
        shape = x.shape if rows is None else (x.shape[0], rows[1]) + tuple(x.shape[2:])
        self.out_shape = jax.ShapeDtypeStruct(shape, x.dtype)

    def _copies(self, x_ref, out_ref, send_sems, recv_sems, local_sem):
        def block(j):
            return x_ref.at[j] if self.rows is None else x_ref.at[j, pl.ds(self.rows[0], self.rows[1])]

        _, me = _peer(0)
        mine = pltpu.make_async_copy(block(me), out_ref.at[me], local_sem)
        sends, recvs = [], []
        for r in range(1, N_DEV):
            pos, idx = _peer(r)
            sems = dict(send_sem=send_sems.at[r - 1], recv_sem=recv_sems.at[r - 1], device_id=pos, device_id_type=MESH)
            sends.append(pltpu.make_async_remote_copy(src_ref=block(idx), dst_ref=out_ref.at[me], **sems))
            recvs.append(pltpu.make_async_remote_copy(src_ref=block(idx), dst_ref=out_ref.at[idx], **sems))
        return mine, sends, recvs

    def start(self, *refs):
        mine, sends, _ = self._copies(*refs)
        mine.start()
        for cp in sends:
            cp.start()

    def finish(self, *refs):
        mine, sends, recvs = self._copies(*refs)
        for cp in recvs:
            cp.wait_recv()
        for cp in sends:
            cp.wait_send()
        mine.wait()


class _Gather:
    def __init__(self, x):
        self.x = x
        self.out_shape = jax.ShapeDtypeStruct((N_DEV,) + tuple(x.shape), x.dtype)

    def _copies(self, x_ref, out_ref, send_sems, recv_sems, local_sem):
        mx, my, mc = lax.axis_index("x"), lax.axis_index("y"), lax.axis_index("c")
        me, sibling = (mx, my, mc), (mx, my, 1 - mc)
        chips = [(1 - mx, my), (mx, 1 - my), (1 - mx, 1 - my)]

        def slot(px, py, pc):
            return out_ref.at[4 * px + 2 * py + pc]

        def copy(k, block, to, src=None):
            return pltpu.make_async_remote_copy(
                src_ref=slot(*block) if src is None else src, dst_ref=slot(*block),
                send_sem=send_sems.at[k], recv_sem=recv_sems.at[k], device_id=to, device_id_type=MESH)

        return dict(
            mine=pltpu.make_async_copy(x_ref, slot(*me), local_sem),
            first=[copy(0, me, sibling, src=x_ref)] + [copy(1 + j, me, (*chip, mc), src=x_ref) for j, chip in enumerate(chips)],
            passed=[copy(4 + j, (*chip, mc), sibling) for j, chip in enumerate(chips)],
            over_ici=[copy(1 + j, (*chip, mc), me) for j, chip in enumerate(chips)],
            from_sibling=[copy(0, sibling, me)] + [copy(4 + j, (*chip, 1 - mc), me) for j, chip in enumerate(chips)])

    def start(self, *refs):
        cps = self._copies(*refs)
        cps["mine"].start()
        for cp in cps["first"]:
            cp.start()

    def finish(self, *refs):
        cps = self._copies(*refs)
        for arrived, onward in zip(cps["over_ici"], cps["passed"]):
            arrived.wait_recv()
            onward.start()
        for cp in cps["from_sibling"]:
            cp.wait_recv()
        for cp in cps["first"] + cps["passed"]:
            cp.wait_send()
        cps["mine"].wait()


def _call(body, args, *, name, grid, in_specs, out_specs, out_shape, scratch_shapes=(), comm=()):
    n_in, n_out, n_scr, nc = len(in_specs), len(out_shape), len(scratch_shapes), len(comm)

    def wrapped(*refs):
        ins, c_in = refs[:n_in], refs[n_in:n_in + nc]
        outs, c_out = refs[n_in + nc:n_in + nc + n_out], refs[n_in + nc + n_out:n_in + 2 * nc + n_out]
        scr = refs[n_in + 2 * nc + n_out:n_in + 2 * nc + n_out + n_scr]
        sems = refs[n_in + 2 * nc + n_out + n_scr:]
        ids = [pl.program_id(a) for a in range(len(grid))]
        if nc:
            @pl.when(functools.reduce(operator.and_, [i == 0 for i in ids]))
            def _():
                for k, op in enumerate(comm):
                    op.start(c_in[k], c_out[k], *sems[3 * k:3 * k + 3])

        body(*ins, *outs, *scr)
        if nc:
            @pl.when(functools.reduce(operator.and_, [i == g - 1 for i, g in zip(ids, grid)]))
            def _():
                for k, op in enumerate(comm):
                    op.finish(c_in[k], c_out[k], *sems[3 * k:3 * k + 3])

    any_spec = pl.BlockSpec(memory_space=pl.ANY)
    res = pl.pallas_call(
        wrapped, name=name, grid=grid,
        in_specs=list(in_specs) + [any_spec] * nc, out_specs=list(out_specs) + [any_spec] * nc,
        out_shape=list(out_shape) + [op.out_shape for op in comm],
        scratch_shapes=list(scratch_shapes) + list(_COMM_SCRATCH) * nc,
        compiler_params=_params(len(grid)),
    )(*args, *[op.x for op in comm])
    return list(res[:n_out]), list(res[n_out:])


def _comm_call(op, *, name):
    return _call(lambda: None, (), name=name, grid=(1,), in_specs=[], out_specs=[], out_shape=[], comm=(op,))[1][0]


def _adamw(w, parts, m, v, *, name, tr=128, comm=()):
    nl, r, c = w.shape
    assert len(parts) == nl
    parts = [list(p) if isinstance(p, (list, tuple)) else [p] for p in parts]
    npart = parts[0][0].shape[0]
    tr = min(tr, r, *[pc.shape[1] for p in parts for pc in p])
    assert r % tr == 0 and all(pc.shape[1] % tr == 0 for p in parts for pc in p), (name, r, tr)
    pieces = []
    for l, p in enumerate(parts):
        first = 0
        for pc in p:
            pieces.append((l, first, pc.shape[1] // tr, pc))
            first += pc.shape[1] // tr
        assert first == r // tr, (name, l)

    def body(w_ref, m_ref, v_ref, *rest):
        p_refs, (g_ref, d_ref, mo_ref, vo_ref) = rest[:len(pieces)], rest[len(pieces):]
        layer, tile = pl.program_id(0), pl.program_id(1)
        for p_ref, (l, first, count, _) in zip(p_refs, pieces):
            @pl.when((layer == l) & (tile >= first) & (tile < first + count))
            def _():
                g = p_ref[0].astype(F32)
                for k in range(1, npart):
                    g = g + p_ref[k].astype(F32)
                m2 = ADAM_B1 * m_ref[...] + (1.0 - ADAM_B1) * g
                v2 = ADAM_B2 * v_ref[...] + (1.0 - ADAM_B2) * (g * g)
                m_hat = m2 / (1.0 - ADAM_B1 ** ADAM_STEP)
                v_hat = v2 / (1.0 - ADAM_B2 ** ADAM_STEP)
                g_ref[...] = g
                d_ref[...] = -ADAM_LR * (m_hat / (jnp.sqrt(v_hat) + ADAM_EPS) + ADAM_WD * w_ref[...])
                mo_ref[...] = m2
                vo_ref[...] = v2

    blk = pl.BlockSpec((None, tr, c), lambda l, i: (l, i, 0))

    def part_spec(l, first, count):
        return pl.BlockSpec((npart, tr, c), lambda ll, i: (0, jnp.where(ll == l, jnp.clip(i - first, 0, count - 1), 0), 0))

    outs, exchanged = _call(
        body, (w, m, v, *[pc for _, _, _, pc in pieces]), name=name, grid=(nl, r // tr),
        in_specs=[blk, blk, blk] + [part_spec(l, first, count) for l, first, count, _ in pieces],
        out_specs=[blk] * 4, out_shape=[jax.ShapeDtypeStruct((nl, r, c), F32)] * 4, comm=comm)
    return (outs, exchanged) if comm else outs


def _sum_parts(parts, *, name):
    npart = parts.shape[0]

    def body(p_ref, o_ref):
        g = p_ref[0]
        for k in range(1, npart):
            g = g + p_ref[k]
        o_ref[...] = g

    return pl.pallas_call(
        body, name=name, grid=(1,), in_specs=[_full_spec(parts)],
        out_specs=pl.BlockSpec(parts.shape[1:], lambda i: (0, 0)),
        out_shape=jax.ShapeDtypeStruct(parts.shape[1:], F32), compiler_params=_params(1),
    )(parts)


def _pack(arrs):
    blocks = []
    for a in arrs:
        flat = a.reshape(-1).astype(F32)
        blocks.append(jnp.pad(flat, (0, (-flat.shape[0]) % (8 * HEAD))).reshape(-1, HEAD))
    out = jnp.concatenate(blocks, axis=0)
    return jnp.pad(out, ((0, (-out.shape[0]) % HEAD), (0, 0)))


def _unpack(packed, shapes):
    out, off = [], 0
    for shp in shapes:
        size = math.prod(shp)
        rows = -(-size // (8 * HEAD)) * 8
        out.append(packed[off:off + rows].reshape(-1)[:size].reshape(shp))
        off += rows
    return out


def _add_epilogue(acc, res):
    return (acc + res,)


def _relu2_epilogue(acc):
    r = jnp.maximum(acc, 0.0)
    return acc, r * r


def _ple_epilogue(acc, gpre, h):
    return h + jax.nn.sigmoid(gpre) * acc, acc


def kernel(x, p, norm_mix, norm_mlp, norm_ple, w_in_e, w_out_e, hgrn_lb, g_norm_a, conv_w, a_log, dt_bias, g_norm_b, s5_a_re, s5_a_im, s5_b_re, s5_b_im, s5_c_re, s5_c_im, s5_d, s5_log_dt, w_glu, b_glu, w_out_o, w_up, w_down, w_ple_gate, w_ple_proj, final_norm, loss_target, m_norm_mix, m_norm_mlp, m_norm_ple, m_w_in_e, m_w_out_e, m_hgrn_lb, m_g_norm_a, m_conv_w, m_a_log, m_dt_bias, m_g_norm_b, m_s5_a_re, m_s5_a_im, m_s5_b_re, m_s5_b_im, m_s5_c_re, m_s5_c_im, m_s5_d, m_s5_log_dt, m_w_glu, m_b_glu, m_w_out_o, m_w_up, m_w_down, m_w_ple_gate, m_w_ple_proj, m_final_norm, v_norm_mix, v_norm_mlp, v_norm_ple, v_w_in_e, v_w_out_e, v_hgrn_lb, v_g_norm_a, v_conv_w, v_a_log, v_dt_bias, v_g_norm_b, v_s5_a_re, v_s5_a_im, v_s5_b_re, v_s5_b_im, v_s5_c_re, v_s5_c_im, v_s5_d, v_s5_log_dt, v_w_glu, v_b_glu, v_w_out_o, v_w_up, v_w_down, v_w_ple_gate, v_w_ple_proj, v_final_norm):
    args = dict(locals())
    s, d = x.shape[1], x.shape[2]
    aw = d // 2
    ha = hb = aw // HEAD
    main = 4 * d
    z_col = 2 * d + 3 * aw
    ff = w_up.shape[2] * N_DEV
    ple = p.shape[-1]
    groups = d // S5_GROUP
    me = 4 * lax.axis_index("x") + 2 * lax.axis_index("y") + lax.axis_index("c")
    x2, target = x[0], loss_target[0]
    row = lambda a, i: a[i:i + 1]

    def gather_of(w):
        return _Gather(w.astype(BF16))

    w_in = jnp.transpose(_comm_call(gather_of(w_in_e[0]), name="ag_w_in"), (1, 0, 2)).reshape(d, -1)
    w_main = w_in[:, :main]
    w_tail = jnp.pad(w_in[:, main:], ((0, 0), (0, HEAD - 2 * hb)))

    lb_rows = [row(hgrn_lb, 0), row(hgrn_lb, 1), row(hgrn_lb, 2)]
    (lb0,) = _small_call(_lb0_stage, lb_rows, name="f_lb0")
    hp = jnp.zeros((8, HEAD), F32).at[0, :hb].set(a_log[0]).at[1, :hb].set(dt_bias[0])
    expand = jnp.asarray(np.kron(np.eye(S5_STATE, dtype=np.float32), np.ones((1, S5_GROUP), np.float32)))
    prep_in = [s5_a_re[0], s5_a_im[0], s5_log_dt[0].reshape(groups, 1),
               s5_b_re[0].reshape(groups, -1), s5_b_im[0].reshape(groups, -1), expand]
    lr, li, br, bi = _small_call(_s5_prep_stage, prep_in, name="f_s5_prep")
    wb, wc, lt = _s5_pack(lr, li, br, bi, s5_c_re[0], s5_c_im[0])
    wb, wc = wb.astype(BF16), wc.astype(BF16)
    fnorm = final_norm.reshape(1, d)

    w_upg = []

    def block_fwd(h, l):
        hn = _rows_call(_rms_stage, [h], [row(norm_mlp, l)], [BF16], name=f"f_norm_mlp{l}")
        (up, act), (dn8,) = _mm(hn, w_upg[l], epilogue=_relu2_epilogue, out_dtypes=(F32, BF16), name=f"f_up{l}",
                                comm=(gather_of(w_down[l]),))
        w_dn = dn8.reshape(ff, d)
        h2 = _mm(act, w_dn, extras=(h,), epilogue=_add_epilogue, name=f"f_down{l}")
        hq = _rows_call(_rms_stage, [h2], [row(norm_ple, l)], [BF16], name=f"f_norm_ple{l}")
        gpre = _mm(hq, w_pgg[l], name=f"f_ple_gate{l}")
        h3, pp = _mm(p[l, 0], w_ppg[l], extras=(gpre, h2), epilogue=_ple_epilogue, out_dtypes=(F32, F32),
                     name=f"f_ple_proj{l}")
        return h3, dict(h=h, hn=hn, up=up, act=act, h2=h2, hq=hq, gpre=gpre, pp=pp, w_dn=w_dn)

    hn0 = _rows_call(_rms_stage, [x2], [row(norm_mix, 0)], [BF16], name="f_norm_mix0")
    shard_shapes = [conv_w[0].shape, s5_d.shape, b_glu.shape]
    proj, (oe8, pg8, small) = _mm(hn0, w_main, name="f_proj", comm=(
        gather_of(w_out_e[0]), gather_of(w_ple_gate), _Gather(_pack([conv_w[0], s5_d, b_glu]))))
    w_oe = oe8.reshape(d, d)
    w_top, w_bot = w_oe[:aw], w_oe[aw:]
    w_pgg = jnp.transpose(pg8, (1, 0, 2, 3)).reshape(2, d, d)
    conv_g, s5d_g, bglu_g = zip(*[_unpack(small[j], shard_shapes) for j in range(N_DEV)])
    conv_full = jnp.concatenate(conv_g, axis=1)
    s5d_full = jnp.concatenate(s5d_g, axis=1)
    bglu_full = jnp.concatenate(bglu_g, axis=1)
    ab = _mm(hn0, w_tail, name="f_ab")
    (oa, st_a), (gl8, oo8) = _hgrn_fwd(proj, lb0, g_norm_a, heads=ha, name="f_hgrn",
                                       comm=(gather_of(w_glu[0]), gather_of(w_out_o[0])))
    w_gl, w_oo = gl8.reshape(d, d), oo8.reshape(d, d)
    qkv = _conv_fwd(proj, conv_full, col_off=2 * d, name="f_conv")
    slots_to_cols = lambda g8: jnp.transpose(g8, (1, 0, 2)).reshape(g8.shape[1], -1)
    (ob, st_b), (up8,) = _delta_fwd(qkv, ab, proj, hp, g_norm_b, heads=hb, z_off=z_col // HEAD, name="f_delta",
                                    comm=(gather_of(w_up[0]),))
    w_upg.append(slots_to_cols(up8))
    h1, (pp8,) = _mm(oa, w_top, extras=(x2,), epilogue=_add_epilogue, name="f_out_a", comm=(gather_of(w_ple_proj),))
    w_ppg = jnp.transpose(pp8, (1, 2, 0, 3)).reshape(2, ple, d)
    h1 = _mm(ob, w_bot, extras=(h1,), epilogue=_add_epilogue, name="f_out_b")
    h3, sv0 = block_fwd(h1, 0)

    u = _rows_call(_rms_stage, [h3], [row(norm_mix, 1)], [F32], name="f_norm_mix1")
    (y, cins, s5_states), (up8,) = _s5_fwd(u, wb, wc, lt, s5d_full, name="f_s5", comm=(gather_of(w_up[1]),))
    w_upg.append(slots_to_cols(up8))
    act_g = _rows_call(_gelu_stage, [y], [], [BF16], name="f_gelu")
    gl_raw = _mm(act_g, w_gl, name="f_glu")
    glu = _rows_call(_glu_stage, [y, gl_raw], [bglu_full], [BF16], name="f_glu_gate")
    h4 = _mm(glu, w_oo, extras=(h3,), epilogue=_add_epilogue, name="f_out_o")
    h6, sv1 = block_fwd(h4, 1)
    dh, d_fnorm, loss8 = _loss_call(h6, fnorm, target, name="loss")
    loss = lax.psum(loss8[0, 0], ("x", "y", "c"))

    dshard, ffs, cols = d // N_DEV, ff // N_DEV, w_in_e.shape[2]
    rows8 = lambda g: _AllToAll(g.reshape(N_DEV, -1, g.shape[-1]))
    col_slots = lambda g: jnp.transpose(g.reshape(g.shape[0], N_DEV, -1), (1, 0, 2))
    cols8 = lambda g: _AllToAll(col_slots(g))

    def halves(g8):
        r = g8.shape[1] // 2
        return _AllToAll(g8, rows=(0, r)), _AllToAll(g8, rows=(r, r))

    def block_bwd(dh3, l, sv, carried=(), carried_wup=(), carried_up=()):
        (dgpre, dpp), _ = _rows_vjp(_ple_stage, [sv["h2"], sv["gpre"], sv["pp"]], [], [dh3],
                                    row_grads={1: BF16, 2: BF16}, name=f"b_ple{l}")
        g_pp = _mm(p[l, 0], dpp, ta=True, out_dtypes=(BF16,), name=f"b_w_ple_proj{l}")
        g_pg = _mm(sv["hq"], dgpre, ta=True, out_dtypes=(BF16,), name=f"b_w_ple_gate{l}")
        dhq = _mm(dgpre, w_pgg[l], tb=True, name=f"b_ple_gate{l}")
        (dh2,), (g_nple,) = _rows_vjp(_rms_stage, [sv["h2"]], [row(norm_ple, l)], [dhq], row_grads={0: F32},
                                      adds={0: dh3}, name=f"b_norm_ple{l}")
        dup, (r_pg, r_pp) = _mm(dh2, sv["w_dn"], tb=True, extras=(sv["up"],), epilogue=_relu2_grad_epilogue,
                                out_dtypes=(BF16,), name=f"b_down{l}", comm=(rows8(g_pg), cols8(g_pp)))
        g_dn = _mm(sv["act"], dh2, ta=True, out_dtypes=(BF16,), name=f"b_w_down{l}", comm=carried)
        g_dn, r_carried = g_dn if carried else (g_dn, [])
        g_up = _mm(sv["hn"], dup, ta=True, out_dtypes=(BF16,), tn=ffs, out_slots=True, name=f"b_w_up{l}",
                   comm=carried_wup)
        g_up, r_carried_wup = g_up if carried_wup else (g_up, [])
        dhn = _mm(dup, w_upg[l], tb=True, name=f"b_up{l}", comm=carried_up)
        dhn, r_carried_up = dhn if carried_up else (dhn, [])
        (dh0,), (g_nmlp,) = _rows_vjp(_rms_stage, [sv["h"]], [row(norm_mlp, l)], [dhn], row_grads={0: F32},
                                      adds={0: dh2}, name=f"b_norm_mlp{l}")
        return dh0, dict(w_ple_proj=r_pp, w_ple_gate=r_pg, norm_ple=g_nple, w_down=g_dn, w_up=g_up, norm_mlp=g_nmlp,
                         carried=r_carried, carried_wup=r_carried_wup, carried_up=r_carried_up)

    dh4, gb1 = block_bwd(dh, 1, sv1)
    up1_a, up1_b = halves(gb1["w_up"])
    dglu = _mm(dh4, w_oo, tb=True, name="b_out_o")
    g_oo = _mm(glu, dh4, ta=True, out_dtypes=(BF16,), name="b_w_out_o")
    (dy1, dgl), (g_bglu,) = _rows_vjp(_glu_stage, [y, gl_raw], [bglu_full], [dglu], row_grads={0: F32, 1: BF16},
                                      name="b_glu_gate")
    g_gl = _mm(act_g, dgl, ta=True, out_dtypes=(BF16,), name="b_w_glu")
    dact = _mm(dgl, w_gl, tb=True, name="b_glu")
    (dy,), _ = _rows_vjp(_gelu_stage, [y], [], [dact], row_grads={0: F32}, adds={0: dy1}, name="b_gelu")
    (du, dwb, dwc, g_s5d, dlam), (r_dn1, r_up1_a) = _s5_bwd(u, dy, wb, wc, lt, s5d_full, cins, s5_states, name="b_s5",
                                                           comm=(rows8(gb1["w_down"]), up1_a))
    (dh3,), (g_nmix1,) = _rows_vjp(_rms_stage, [h3], [row(norm_mix, 1)], [du], row_grads={0: F32}, adds={0: dh4},
                                   name="b_norm_mix1")
    dlr, dli, dbr, dbi, g_cre, g_cim = _s5_unpack(dwb, dwc, dlam)
    g_are, g_aim, g_ldt, g_bre, g_bim, _ = _small_vjp(_s5_prep_stage, prep_in, [dlr, dli, dbr, dbi], name="b_s5_prep")

    early_grads = dict(
        s5_a_re=g_are[None], s5_a_im=g_aim[None], s5_b_re=g_bre.reshape(s5_b_re.shape),
        s5_b_im=g_bim.reshape(s5_b_im.shape), s5_c_re=g_cre[None], s5_c_im=g_cim[None],
        s5_log_dt=g_ldt.reshape(1, groups), final_norm=d_fnorm.reshape(d), s5_d=g_s5d, b_glu=g_bglu)
    dh1, gb0 = block_bwd(dh3, 0, sv0, (rows8(g_oo), rows8(g_gl)), (up1_b,),
                         (_Gather(_pack(list(early_grads.values()))),))
    r_oo, r_gl = gb0["carried"]
    r_up1 = [r_up1_a, gb0["carried_wup"][0]]
    (early_parts,) = gb0["carried_up"]
    dn0_a, dn0_b = halves(gb0["w_down"].reshape(N_DEV, ffs, d))
    up0_a, up0_b = halves(gb0["w_up"])
    doa = _mm(dh1, w_top, tb=True, name="b_out_a")
    dob = _mm(dh1, w_bot, tb=True, name="b_out_b")
    g_oe = jnp.concatenate([_mm(oa, dh1, ta=True, out_dtypes=(BF16,), name="b_w_out_a"),
                            _mm(ob, dh1, ta=True, out_dtypes=(BF16,), name="b_w_out_b")], axis=0)
    (dq, df, di, dg, dlb, g_gna), (r_dn0_a,) = _hgrn_bwd(proj, lb0, g_norm_a, st_a, doa, heads=ha, name="b_hgrn",
                                                        comm=(dn0_a,))
    (dqb, dkb, dvb, dab, dz, dhp, g_gnb), (r_dn0_b, r_up0_a) = _delta_bwd(
        qkv, ab, proj, hp, g_norm_b, st_b, dob, heads=hb, z_off=z_col // HEAD, name="b_delta", comm=(dn0_b, up0_a))
    (dqkv, g_conv), (r_oe,) = _conv_bwd(proj, conv_full, jnp.concatenate([dqb, dkb, dvb], axis=1), col_off=2 * d,
                                        name="b_conv", comm=(rows8(g_oe),))
    dproj = jnp.concatenate([dq, df, di, dg, dqkv, dz], axis=1)
    g_main, (r_up0_b,) = _mm(hn0, dproj, ta=True, out_dtypes=(BF16,), name="b_w_proj", comm=(up0_b,))
    r_dn0, r_up0 = [r_dn0_a, r_dn0_b], [r_up0_a, r_up0_b]
    g_tail = _mm(hn0, dab, ta=True, out_dtypes=(BF16,), name="b_w_ab")
    in_a, in_b = halves(col_slots(jnp.concatenate([g_main, g_tail[:, :2 * hb]], axis=1)))
    dhn0, (r_in_a,) = _mm(dproj, w_main, tb=True, name="b_proj", comm=(in_a,))
    dhn0 = _mm(dab, w_tail, tb=True, extras=(dhn0,), epilogue=_add_epilogue, name="b_ab")
    (dx,), (g_nmix0,) = _rows_vjp(_rms_stage, [x2], [row(norm_mix, 0)], [dhn0], row_grads={0: F32}, adds={0: dh1},
                                  name="b_norm_mix0")
    g_lb = jnp.concatenate(_small_vjp(_lb0_stage, lb_rows, [dlb], name="b_lb0"), axis=0)

    late_grads = dict(
        norm_mix=jnp.concatenate([g_nmix0, g_nmix1], axis=0),
        norm_mlp=jnp.concatenate([gb0["norm_mlp"], gb1["norm_mlp"]], axis=0),
        norm_ple=jnp.concatenate([gb0["norm_ple"], gb1["norm_ple"]], axis=0),
        hgrn_lb=g_lb, g_norm_a=g_gna, a_log=dhp[0:1, :hb], dt_bias=dhp[1:2, :hb], g_norm_b=g_gnb, conv_w=g_conv)
    rep_names = ["norm_mix", "norm_mlp", "norm_ple", "hgrn_lb", "g_norm_a", "a_log", "dt_bias", "g_norm_b", "s5_a_re",
                 "s5_a_im", "s5_b_re", "s5_b_im", "s5_c_re", "s5_c_im", "s5_log_dt", "final_norm"]
    late_parts = _comm_call(_Gather(_pack(list(late_grads.values()))), name="ag_small_grads")
    summed = {}
    for tag, grads, parts in (("early", early_grads, early_parts), ("late", late_grads, late_parts)):
        sums = _unpack(_sum_parts(parts, name=f"sum_small_grads_{tag}"), [g.shape for g in grads.values()])
        summed.update(zip(grads, sums))
    cw = conv_w.shape[2]
    dshard = d // N_DEV
    shard_g = dict(conv_w=lax.dynamic_slice(summed["conv_w"], (0, me * cw), (CONV_WIDTH, cw))[None],
                   s5_d=lax.dynamic_slice(summed["s5_d"], (0, me * dshard), (1, dshard)),
                   b_glu=lax.dynamic_slice(summed["b_glu"], (0, me * dshard), (1, dshard)))
    small_names = rep_names + ["conv_w", "s5_d", "b_glu"]
    g_small = [summed[k] if k in rep_names else shard_g[k] for k in small_names]
    shapes = [args[k].shape for k in small_names]
    sm_out = _adamw(_pack([args[k] for k in small_names])[None], [_pack(g_small)[None]],
                    _pack([args["m_" + k] for k in small_names])[None], _pack([args["v_" + k] for k in small_names])[None],
                    name="adamw_small")
    sm_out = [dict(zip(small_names, _unpack(o[0], shapes))) for o in sm_out]

    up_out, (r_in_b,) = _adamw(w_up, [r_up0, r_up1], m_w_up, v_w_up, name="adamw_w_up", comm=(in_b,))
    received = dict(w_in_e=[[r_in_a, r_in_b]], w_out_e=[r_oe], w_glu=[r_gl], w_out_o=[r_oo],
                    w_down=[r_dn0, r_dn1], w_ple_gate=[gb0["w_ple_gate"], gb1["w_ple_gate"]],
                    w_ple_proj=[gb0["w_ple_proj"], gb1["w_ple_proj"]])
    big_out = {k: _adamw(args[k], layers, args["m_" + k], args["v_" + k], name="adamw_" + k)
               for k, layers in received.items()}
    big_out["w_up"] = up_out

    names = ["norm_mix", "norm_mlp", "norm_ple", "w_in_e", "w_out_e", "hgrn_lb", "g_norm_a", "conv_w", "a_log", "dt_bias",
             "g_norm_b", "s5_a_re", "s5_a_im", "s5_b_re", "s5_b_im", "s5_c_re", "s5_c_im", "s5_d", "s5_log_dt", "w_glu",
             "b_glu", "w_out_o", "w_up", "w_down", "w_ple_gate", "w_ple_proj", "final_norm"]
    result = [loss, dx[None]]
    for j in range(4):
        result += [big_out[k][j] if k in big_out else sm_out[j][k] for k in names]
    return tuple(result)
```

```python
import functools
import math
import operator

import numpy as np
import jax
import jax.numpy as jnp
from jax import lax
from jax.experimental import pallas as pl
from jax.experimental.pallas import tpu as pltpu

F32 = jnp.float32
BF16 = jnp.bfloat16
MM_DTYPE = BF16
HI = lax.Precision.HIGHEST
MESH = pl.DeviceIdType.MESH

NORM_EPS = 1e-6
CHUNK = 64
HEAD = 128
CONV_WIDTH = 4
S5_GROUP = 16
S5_STATE = 64
S5_GB = 8
S5_HALF = S5_GB * S5_STATE
N_DEV = 8
HEADS_PER_STEP = 8
HGRN_SUB = 16
ADAM_LR, ADAM_B1, ADAM_B2, ADAM_EPS, ADAM_WD, ADAM_STEP = 0.001, 0.9, 0.999, 1e-08, 0.01, 10
VMEM_LIMIT = 56 * 1024 * 1024

NN = (((1,), (0,)), ((), ()))
NT = (((1,), (1,)), ((), ()))
TN = (((0,), (0,)), ((), ()))


def _dot(a, b, dn=NN):
    return lax.dot_general(a, b, dn, precision=HI, preferred_element_type=F32)


def _hdot(a, b, dn=NN):
    return lax.dot_general(a, b, dn, precision=lax.Precision.HIGH, preferred_element_type=F32)


def _bdot_raw(a, b, dn=NN):
    return lax.dot_general(a.astype(BF16), b.astype(BF16), dn, preferred_element_type=F32)


@functools.partial(jax.custom_vjp, nondiff_argnums=(2,))
def _bdot(a, b, dn):
    return _bdot_raw(a, b, dn)


def _bdot_fwd(a, b, dn):
    return _bdot_raw(a, b, dn), (a, b)


def _bdot_bwd(dn, res, g):
    a, b = res
    if dn == NN:
        return _bdot_raw(g, b, NT), _bdot_raw(a, g, TN)
    if dn == NT:
        return _bdot_raw(g, b, NN), _bdot_raw(g, a, TN)
    assert dn == TN
    return _bdot_raw(b, g, NT), _bdot_raw(a, g, NN)


_bdot.defvjp(_bdot_fwd, _bdot_bwd)


def _per_head(f):
    def g(*args, **kw):
        n = [len(a.vals) for a in args if isinstance(a, _Heads)]
        if not n:
            return f(*args, **kw)
        return _Heads([f(*[a.vals[j] if isinstance(a, _Heads) else a for a in args], **kw) for j in range(n[0])])
    return g


class _Heads:
    def __init__(self, vals):
        self.vals = list(vals)

    def __add__(self, o):
        return _per_head(operator.add)(self, o)

    def __radd__(self, o):
        return _per_head(operator.add)(o, self)

    def __sub__(self, o):
        return _per_head(operator.sub)(self, o)

    def __rsub__(self, o):
        return _per_head(operator.sub)(o, self)

    def __mul__(self, o):
        return _per_head(operator.mul)(self, o)

    def __rmul__(self, o):
        return _per_head(operator.mul)(o, self)

    def __neg__(self):
        return _per_head(operator.neg)(self)


_exp, _log, _where, _sum, _mean = (_per_head(f) for f in (jnp.exp, jnp.log, jnp.where, jnp.sum, jnp.mean))
_sigmoid, _rsqrt, _equal = _per_head(jax.nn.sigmoid), _per_head(lax.rsqrt), _per_head(operator.eq)
_hdot_h, _bdot_h = _per_head(_hdot), _per_head(_bdot)
_rows = _per_head(lambda a, lo, n: a[lo:lo + n, :])
_row_concat = _per_head(lambda *xs: jnp.concatenate(xs, axis=0))


def _params(n_axes):
    return pltpu.CompilerParams(dimension_semantics=("arbitrary",) * n_axes, vmem_limit_bytes=VMEM_LIMIT)


def _full_spec(a):
    nd = a.ndim
    return pl.BlockSpec(a.shape, lambda *_: (0,) * nd)


def _mm(a, b, *, name, ta=False, tb=False, extras=(), epilogue=None, out_dtypes=(F32,), tm=1024, tn=1024, tk=2048,
        out_slots=False, comm=()):
    m = a.shape[1] if ta else a.shape[0]
    k = a.shape[0] if ta else a.shape[1]
    n = b.shape[0] if tb else b.shape[1]
    assert k == (b.shape[1] if tb else b.shape[0]), (name, a.shape, b.shape)
    tm, tn, tk = min(tm, m), min(tn, n), min(tk, k)
    assert m % tm == 0 and n % tn == 0 and k % tk == 0, (name, m, n, k)
    nk = k // tk
    n_ex, n_out = len(extras), len(out_dtypes)
    dn = (((0 if ta else 1,), (1 if tb else 0,)), ((), ()))

    def body(a_ref, b_ref, *rest):
        ex_refs, out_refs = rest[:n_ex], rest[n_ex:n_ex + n_out]
        part = lax.dot_general(a_ref[...].astype(MM_DTYPE), b_ref[...].astype(MM_DTYPE), dn, preferred_element_type=F32)

        def finish(acc):
            outs = epilogue(acc, *[r[...] for r in ex_refs]) if epilogue is not None else (acc,)
            for o_ref, o in zip(out_refs, outs):
                o_ref[...] = o.astype(o_ref.dtype)

        if nk == 1:
            finish(part)
            return
        acc_ref = rest[-1]
        kk = pl.program_id(2)

        @pl.when(kk == 0)
        def _():
            acc_ref[...] = part

        @pl.when((kk > 0) & (kk < nk - 1))
        def _():
            acc_ref[...] += part

        @pl.when(kk == nk - 1)
        def _():
            finish(acc_ref[...] + part)

    a_spec = pl.BlockSpec((tk, tm), lambda i, j, q: (q, i)) if ta else pl.BlockSpec((tm, tk), lambda i, j, q: (i, q))
    b_spec = pl.BlockSpec((tn, tk), lambda i, j, q: (j, q)) if tb else pl.BlockSpec((tk, tn), lambda i, j, q: (q, j))
    ex_specs = []
    for e in extras:
        if e.shape[0] == 1 and m != 1:
            ex_specs.append(pl.BlockSpec((1, tn), lambda i, j, q: (0, j)))
        else:
            ex_specs.append(pl.BlockSpec((tm, tn), lambda i, j, q: (i, j)))
    if out_slots:
        out_spec, out_dims = pl.BlockSpec((None, tm, tn), lambda i, j, q: (j, i, 0)), (n // tn, m, tn)
    else:
        out_spec, out_dims = pl.BlockSpec((tm, tn), lambda i, j, q: (i, j)), (m, n)
    outs, exchanged = _call(
        body, (a, b, *extras), name=name, grid=(m // tm, n // tn, nk),
        in_specs=[a_spec, b_spec] + ex_specs,
        out_specs=[out_spec for _ in out_dtypes],
        out_shape=[jax.ShapeDtypeStruct(out_dims, dt) for dt in out_dtypes],
        scratch_shapes=[pltpu.VMEM((tm, tn), F32)] if nk > 1 else [], comm=comm)
    outs = outs[0] if n_out == 1 else tuple(outs)
    return (outs, exchanged) if comm else outs


def _rows_call(fn, rows, consts, out_dtypes, *, name, tr=256):
    s = rows[0].shape[0]
    tr = min(tr, s)
    nr, nc = len(rows), len(consts)
    widths = [o.shape[1] for o in jax.eval_shape(
        fn, *[jax.ShapeDtypeStruct((tr, r.shape[1]), F32) for r in rows],
        *[jax.ShapeDtypeStruct(c.shape, F32) for c in consts])]

    def body(*refs):
        rv = [r[...].astype(F32) for r in refs[:nr]]
        cv = [c[...] for c in refs[nr:nr + nc]]
        for o_ref, o in zip(refs[nr + nc:], fn(*rv, *cv)):
            o_ref[...] = o.astype(o_ref.dtype)

    outs = pl.pallas_call(
        body, name=name, grid=(s // tr,),
        in_specs=[pl.BlockSpec((tr, r.shape[1]), lambda i: (i, 0)) for r in rows] + [_full_spec(c) for c in consts],
        out_specs=[pl.BlockSpec((tr, w), lambda i: (i, 0)) for w in widths],
        out_shape=[jax.ShapeDtypeStruct((s, w), dt) for w, dt in zip(widths, out_dtypes)],
        compiler_params=_params(1),
    )(*rows, *consts)
    return outs[0] if len(outs) == 1 else tuple(outs)


def _rows_vjp(fn, rows, consts, cots, *, name, row_grads, adds=None, tr=256):
    adds = adds or {}
    s = rows[0].shape[0]
    tr = min(tr, s)
    nr, nc, nt = len(rows), len(consts), len(cots)
    rg = sorted(row_grads)
    ad = sorted(adds)

    def body(*refs):
        rv = [r[...].astype(F32) for r in refs[:nr]]
        cv = [c[...] for c in refs[nr:nr + nc]]
        ct = [c[...].astype(F32) for c in refs[nr + nc:nr + nc + nt]]
        av = {i: r[...].astype(F32) for i, r in zip(ad, refs[nr + nc + nt:nr + nc + nt + len(ad)])}
        out_refs = refs[nr + nc + nt + len(ad):]
        _, vjp = jax.vjp(fn, *rv, *cv)
        grads = vjp(tuple(ct))
        for o_ref, i in zip(out_refs[:len(rg)], rg):
            g = grads[i]
            if i in av:
                g = g + av[i]
            o_ref[...] = g.astype(o_ref.dtype)

        @pl.when(pl.program_id(0) == 0)
        def _():
            for o_ref in out_refs[len(rg):]:
                o_ref[...] = jnp.zeros_like(o_ref)

        for o_ref, g in zip(out_refs[len(rg):], grads[nr:]):
            o_ref[...] += g

    row_spec = lambda a: pl.BlockSpec((tr, a.shape[1]), lambda i: (i, 0))
    outs = pl.pallas_call(
        body, name=name, grid=(s // tr,),
        in_specs=[row_spec(r) for r in rows] + [_full_spec(c) for c in consts] + [row_spec(c) for c in cots]
        + [row_spec(adds[i]) for i in ad],
        out_specs=[row_spec(rows[i]) for i in rg] + [_full_spec(c) for c in consts],
        out_shape=[jax.ShapeDtypeStruct(rows[i].shape, row_grads[i]) for i in rg]
        + [jax.ShapeDtypeStruct(c.shape, F32) for c in consts],
        compiler_params=_params(1),
    )(*rows, *consts, *cots, *[adds[i] for i in ad])
    return list(outs[:len(rg)]), list(outs[len(rg):])


def _small_call(fn, ins, *, name):
    shapes = jax.eval_shape(fn, *[jax.ShapeDtypeStruct(a.shape, F32) for a in ins])

    def body(*refs):
        for o_ref, o in zip(refs[len(ins):], fn(*[r[...] for r in refs[:len(ins)]])):
            o_ref[...] = o

    return pl.pallas_call(
        body, name=name, in_specs=[_full_spec(a) for a in ins],
        out_specs=[pl.BlockSpec(o.shape, functools.partial(lambda nd, *_: (0,) * nd, len(o.shape))) for o in shapes],
        out_shape=[jax.ShapeDtypeStruct(o.shape, F32) for o in shapes], grid=(1,),
        compiler_params=_params(1),
    )(*ins)


def _small_vjp(fn, ins, cots, *, name):
    def body(*refs):
        vals = [r[...] for r in refs[:len(ins)]]
        ct = [r[...] for r in refs[len(ins):len(ins) + len(cots)]]
        _, vjp = jax.vjp(fn, *vals)
        for o_ref, g in zip(refs[len(ins) + len(cots):], vjp(tuple(ct))):
            o_ref[...] = g

    return pl.pallas_call(
        body, name=name, in_specs=[_full_spec(a) for a in ins] + [_full_spec(c) for c in cots],
        out_specs=[_full_spec(a) for a in ins],
        out_shape=[jax.ShapeDtypeStruct(a.shape, F32) for a in ins], grid=(1,),
        compiler_params=_params(1),
    )(*ins, *cots)


def _rms(x, g):
    return x * _rsqrt(_mean(x * x, axis=-1, keepdims=True) + NORM_EPS) * g


def _rms_stage(x, g):
    return (_rms(x, g),)


def _silu(x):
    return x * _sigmoid(x)


def _softplus(x):
    return jnp.maximum(x, 0.0) + jnp.log1p(jnp.exp(-jnp.abs(x)))


def _gelu(x):
    return jax.nn.gelu(x, approximate=True)


def _gelu_stage(y):
    return (_gelu(y),)


def _glu_stage(y, gl_raw, b):
    return (_gelu(y) * jax.nn.sigmoid(gl_raw + b),)


def _ple_stage(h, gpre, pp):
    return (h + jax.nn.sigmoid(gpre) * pp,)


def _relu2_grad_epilogue(acc, up):
    return (acc * (2.0 * jnp.maximum(up, 0.0)),)


def _lb0_stage(x0, x1, x2):
    mx = jnp.maximum(jnp.maximum(x0, x1), x2)
    e0, e1, e2 = jnp.exp(x0 - mx), jnp.exp(x1 - mx), jnp.exp(x2 - mx)
    return (e0 / (e0 + e1 + e2),)


def _s5_prep_stage(a_re, a_im, log_dt, b_re, b_im, expand):
    step = jnp.exp(log_dt)
    mag = jnp.exp(a_re * step)
    lr = mag * jnp.cos(a_im * step)
    li = mag * jnp.sin(a_im * step)
    den = a_re * a_re + a_im * a_im
    cr = ((lr - 1.0) * a_re + li * a_im) / den
    ci = (li * a_re - (lr - 1.0) * a_im) / den
    cr_e, ci_e = _dot(cr, expand), _dot(ci, expand)
    return lr, li, cr_e * b_re - ci_e * b_im, cr_e * b_im + ci_e * b_re


def _loss_call(h, g, target, *, name, tr=256):
    s, d = h.shape
    tr = min(tr, s)

    def loss_fn(hv, gv, tv):
        err = _rms(hv, gv) - tv
        return 0.5 * jnp.sum(jnp.mean(err * err, axis=-1))

    def body(h_ref, g_ref, t_ref, dh_ref, dg_ref, loss_ref):
        val, (dh, dg) = jax.value_and_grad(loss_fn, argnums=(0, 1))(h_ref[...], g_ref[...], t_ref[...])
        dh_ref[...] = dh

        @pl.when(pl.program_id(0) == 0)
        def _():
            dg_ref[...] = jnp.zeros_like(dg_ref)
            loss_ref[...] = jnp.zeros_like(loss_ref)

        dg_ref[...] += dg
        loss_ref[...] += jnp.full(loss_ref.shape, val, F32)

    row = pl.BlockSpec((tr, d), lambda i: (i, 0))
    return pl.pallas_call(
        body, name=name, grid=(s // tr,),
        in_specs=[row, _full_spec(g), row],
        out_specs=[row, _full_spec(g), pl.BlockSpec((8, 128), lambda i: (0, 0))],
        out_shape=[jax.ShapeDtypeStruct((s, d), F32), jax.ShapeDtypeStruct(g.shape, F32),
                   jax.ShapeDtypeStruct((8, 128), F32)],
        compiler_params=_params(1),
    )(h, g, target)


def _hgrn_chunk(q, fp, iv, gp, lb, gn, st_t):
    c = CHUNK
    row = lax.broadcasted_iota(jnp.int32, (c, c), 0)
    col = lax.broadcasted_iota(jnp.int32, (c, c), 1)
    causal = row >= col
    fg = lb + (1.0 - lb) * _sigmoid(fp)
    k = 1.0 - fg
    lf = _log(fg)
    cum = _hdot_h(causal.astype(F32), lf, NN)
    cend = _sum(lf, axis=0, keepdims=True)
    shift = HGRN_SUB.bit_length() - 1
    ref = _hdot_h(((row >> shift) > (col >> shift)).astype(F32), lf, NN)
    q_dec = q * _exp(cum - ref)
    key_row = lax.broadcasted_iota(jnp.int32, (c, 1), 0)
    blocks = []
    for lo in range(0, c, HGRN_SUB):
        live = key_row < lo + HGRN_SUB
        k_dec = _where(live, k * _exp(_where(live, _rows(ref, lo, 1) - cum, 0.0)), 0.0)
        blocks.append(_hdot_h(_rows(q_dec, lo, HGRN_SUB), k_dec, NT))
    scores = _where(causal, _row_concat(*blocks), 0.0)
    out = _bdot_h(scores, iv, NN) + _bdot_h(q * _exp(cum), st_t, NT)
    st_new = st_t * _exp(cend) + _bdot_h(iv, k * _exp(cend - cum), TN)
    res = _rms(out, gn) * _silu(gp)
    return res, st_new


def _hgrn_heads(qs, fs, ivs, gs, lbs, gn, sts):
    res, st_new = _hgrn_chunk(_Heads(qs), _Heads(fs), _Heads(ivs), _Heads(gs), _Heads(lbs), gn, _Heads(sts))
    return res.vals, st_new.vals


def _lanes(j):
    return slice(j * HEAD, (j + 1) * HEAD)


def _hgrn_fwd(proj, lb, gn, *, heads, name, comm=()):
    s = proj.shape[0]
    n = s // CHUNK
    hpb = min(HEADS_PER_STEP, heads)
    assert heads % hpb == 0

    def body(q_ref, f_ref, i_ref, g_ref, lb_ref, gn_ref, o_ref, st_ref, state):
        @pl.when(pl.program_id(1) == 0)
        def _():
            state[...] = jnp.zeros_like(state)

        gnv = gn_ref[...]
        loaded = [(q_ref[:, _lanes(j)], f_ref[:, _lanes(j)], i_ref[:, _lanes(j)], g_ref[:, _lanes(j)],
                   lb_ref[:, _lanes(j)], state[j]) for j in range(hpb)]
        qs, fs, ivs, gs, lbs, sts = (list(t) for t in zip(*loaded))
        res, st_new = _hgrn_heads(qs, fs, ivs, gs, lbs, gnv, sts)
        for j in range(hpb):
            st_ref[j] = sts[j]
            o_ref[:, _lanes(j)] = res[j].astype(o_ref.dtype)
            state[j] = st_new[j]

    wide = hpb * HEAD
    blk = lambda off: pl.BlockSpec((CHUNK, wide), lambda h, c: (c, off // hpb + h))
    return _call(
        body, (proj, proj, proj, proj, lb, gn), name=name, grid=(heads // hpb, n),
        in_specs=[blk(0), blk(heads), blk(2 * heads), blk(3 * heads),
                  pl.BlockSpec((1, wide), lambda h, c: (0, h)), pl.BlockSpec((1, HEAD), lambda h, c: (0, 0))],
        out_specs=[pl.BlockSpec((CHUNK, wide), lambda h, c: (c, h)),
                   pl.BlockSpec((hpb, None, HEAD, HEAD), lambda h, c: (h, c, 0, 0))],
        out_shape=[jax.ShapeDtypeStruct((s, heads * HEAD), BF16), jax.ShapeDtypeStruct((heads, n, HEAD, HEAD), F32)],
        scratch_shapes=[pltpu.VMEM((hpb, HEAD, HEAD), F32)], comm=comm)


def _hgrn_bwd(proj, lb, gn, states, d_out, *, heads, name, comm=()):
    s = proj.shape[0]
    n = s // CHUNK
    hpb = min(HEADS_PER_STEP, heads)

    def body(q_ref, f_ref, i_ref, g_ref, lb_ref, gn_ref, st_ref, do_ref,
             dq_ref, df_ref, di_ref, dg_ref, dlb_ref, dgn_ref, dstate):
        h, c = pl.program_id(0), pl.program_id(1)

        @pl.when(c == 0)
        def _():
            dstate[...] = jnp.zeros_like(dstate)
            dlb_ref[...] = jnp.zeros_like(dlb_ref)

        @pl.when((c == 0) & (h == 0))
        def _():
            dgn_ref[...] = jnp.zeros_like(dgn_ref)

        gnv = gn_ref[...]
        loaded = [(q_ref[:, _lanes(j)], f_ref[:, _lanes(j)], i_ref[:, _lanes(j)], g_ref[:, _lanes(j)],
                   lb_ref[:, _lanes(j)], st_ref[j], do_ref[:, _lanes(j)].astype(F32), dstate[j]) for j in range(hpb)]
        qs, fs, ivs, gs, lbs, sts, dos, dss = (list(t) for t in zip(*loaded))
        _, vjp = jax.vjp(_hgrn_heads, qs, fs, ivs, gs, lbs, gnv, sts)
        dqs, dfs, dis, dgs, dlbs, dgn_sum, dsts = vjp((dos, dss))
        for j in range(hpb):
            ln = _lanes(j)
            dq_ref[:, ln] = dqs[j].astype(dq_ref.dtype)
            df_ref[:, ln] = dfs[j].astype(df_ref.dtype)
            di_ref[:, ln] = dis[j].astype(di_ref.dtype)
            dg_ref[:, ln] = dgs[j].astype(dg_ref.dtype)
            dlb_ref[:, ln] += dlbs[j]
            dstate[j] = dsts[j]
        dgn_ref[...] += dgn_sum

    wide = hpb * HEAD
    rev = lambda off: pl.BlockSpec((CHUNK, wide), lambda h, c: (n - 1 - c, off // hpb + h))
    out_blk = pl.BlockSpec((CHUNK, wide), lambda h, c: (n - 1 - c, h))
    width = heads * HEAD
    return _call(
        body, (proj, proj, proj, proj, lb, gn, states, d_out), name=name, grid=(heads // hpb, n),
        in_specs=[rev(0), rev(heads), rev(2 * heads), rev(3 * heads),
                  pl.BlockSpec((1, wide), lambda h, c: (0, h)), pl.BlockSpec((1, HEAD), lambda h, c: (0, 0)),
                  pl.BlockSpec((hpb, None, HEAD, HEAD), lambda h, c: (h, n - 1 - c, 0, 0)), out_blk],
        out_specs=[out_blk, out_blk, out_blk, out_blk,
                   pl.BlockSpec((1, wide), lambda h, c: (0, h)), pl.BlockSpec((1, HEAD), lambda h, c: (0, 0))],
        out_shape=[jax.ShapeDtypeStruct((s, width), BF16)] * 4
        + [jax.ShapeDtypeStruct((1, width), F32), jax.ShapeDtypeStruct((1, HEAD), F32)],
        scratch_shapes=[pltpu.VMEM((hpb, HEAD, HEAD), F32)], comm=comm)


def _shift_rows(x, d, rowi):
    if d == 0:
        return x
    n = x.shape[0]
    rolled = pltpu.roll(x, d % n, 0)
    keep = rowi >= d if d > 0 else rowi < n + d
    return jnp.where(keep, rolled, 0.0)


def _conv_pre(x, w_ref, rowi):
    acc = None
    for j in range(CONV_WIDTH):
        term = w_ref[j:j + 1, :] * _shift_rows(x, CONV_WIDTH - 1 - j, rowi)
        acc = term if acc is None else acc + term
    return acc


def _conv_fwd(proj, w, *, col_off, name, cb=256):
    s = proj.shape[0]
    width = w.shape[1]
    cb = min(cb, width)

    def body(x_ref, w_ref, o_ref):
        rowi = lax.broadcasted_iota(jnp.int32, (s, cb), 0)
        o_ref[...] = _silu(_conv_pre(x_ref[...], w_ref, rowi))

    return pl.pallas_call(
        body, name=name, grid=(width // cb,),
        in_specs=[pl.BlockSpec((s, cb), lambda j: (0, col_off // cb + j)), pl.BlockSpec((CONV_WIDTH, cb), lambda j: (0, j))],
        out_specs=pl.BlockSpec((s, cb), lambda j: (0, j)),
        out_shape=jax.ShapeDtypeStruct((s, width), F32),
        compiler_params=_params(1),
    )(proj, w)


def _conv_bwd(proj, w, d_out, *, col_off, name, cb=256, comm=()):
    s = proj.shape[0]
    width = w.shape[1]
    cb = min(cb, width)

    def body(x_ref, w_ref, do_ref, dx_ref, dw_ref):
        rowi = lax.broadcasted_iota(jnp.int32, (s, cb), 0)
        x = x_ref[...]
        pre = _conv_pre(x, w_ref, rowi)
        sg = jax.nn.sigmoid(pre)
        dpre = do_ref[...] * (sg + pre * sg * (1.0 - sg))
        dx = None
        for j in range(CONV_WIDTH):
            d = CONV_WIDTH - 1 - j
            term = w_ref[j:j + 1, :] * _shift_rows(dpre, -d, rowi)
            dx = term if dx is None else dx + term
            dw_ref[j:j + 1, :] = jnp.sum(dpre * _shift_rows(x, d, rowi), axis=0, keepdims=True)
        dx_ref[...] = dx.astype(dx_ref.dtype)

    return _call(
        body, (proj, w, d_out), name=name, grid=(width // cb,),
        in_specs=[pl.BlockSpec((s, cb), lambda j: (0, col_off // cb + j)), pl.BlockSpec((CONV_WIDTH, cb), lambda j: (0, j)),
                  pl.BlockSpec((s, cb), lambda j: (0, j))],
        out_specs=[pl.BlockSpec((s, cb), lambda j: (0, j)), pl.BlockSpec((CONV_WIDTH, cb), lambda j: (0, j))],
        out_shape=[jax.ShapeDtypeStruct((s, width), BF16), jax.ShapeDtypeStruct((CONV_WIDTH, width), F32)],
        comm=comm)


_lane_concat = _per_head(lambda a, b: jnp.concatenate([a, b], axis=1))
_lane_half = _per_head(lambda a, j: a[:, j * HEAD:(j + 1) * HEAD])


def _tri_inverse(lower):
    c = CHUNK
    row = lax.broadcasted_iota(jnp.int32, (c, c), 0)
    col = lax.broadcasted_iota(jnp.int32, (c, c), 1)
    inv = (row == col).astype(F32)
    lvl = 0
    while (1 << lvl) < c:
        same_pair = (row >> (lvl + 1)) == (col >> (lvl + 1))
        off_block = same_pair & (((row >> lvl) & 1) == 1) & (((col >> lvl) & 1) == 0)
        inv = inv - _hdot_h(_hdot_h(inv, _where(off_block, lower, 0.0), NN), inv, NN)
        lvl += 1
    return inv


@jax.custom_vjp
def _tri_solve(lowers, rhss):
    return _tri_solve_fwd(lowers, rhss)[0]


def _tri_solve_fwd(lowers, rhss):
    inv = _tri_inverse(_Heads(lowers))
    sol = _hdot_h(inv, _Heads(rhss), NN)
    return sol.vals, (inv.vals, sol.vals)


def _tri_solve_bwd(res, g):
    inv, sol = _Heads(res[0]), _Heads(res[1])
    d_rhs = _hdot_h(inv, _Heads(g), TN)
    return (-_hdot_h(d_rhs, sol, NT)).vals, d_rhs.vals


_tri_solve.defvjp(_tri_solve_fwd, _tri_solve_bwd)


def _solve(lower, rhs):
    if isinstance(lower, _Heads):
        return _Heads(_tri_solve(lower.vals, rhs.vals))
    return _tri_solve([lower], [rhs])[0]


def _delta_chunk(h, heads, qr, kr, vr, ab, zp, alog, dtb, gn, st):
    c = CHUNK
    row = lax.broadcasted_iota(jnp.int32, (c, c), 0)
    col = lax.broadcasted_iota(jnp.int32, (c, c), 1)
    causal = row >= col
    strict = row > col
    lane = lax.broadcasted_iota(jnp.int32, (c, HEAD), 1)
    mine = _equal(h, lane)
    la_full = -jnp.exp(alog) * _softplus(ab + dtb)
    cum_full = _hdot(causal.astype(F32), la_full)
    cum = _sum(_where(mine, cum_full, 0.0), axis=1, keepdims=True)
    cend = _sum(_sum(_where(mine, la_full, 0.0), axis=1, keepdims=True), axis=0, keepdims=True)
    beta = _sum(_where(_equal(heads + h, lane), jax.nn.sigmoid(ab), 0.0), axis=1, keepdims=True)
    cum_row = _hdot_h(_where(mine, 1.0, 0.0), cum_full, NT)
    decay = _where(causal, _exp(_where(causal, cum - cum_row, 0.0)), 0.0)
    qn = qr * _rsqrt(_sum(qr * qr, axis=-1, keepdims=True) + NORM_EPS) * (HEAD ** -0.5)
    kn = kr * _rsqrt(_sum(kr * kr, axis=-1, keepdims=True) + NORM_EPS)
    kb = kn * beta
    lower = _where(strict, _bdot_h(kb, kn, NT) * decay, 0.0)
    ecum = _exp(cum)
    sol = _solve(lower, _lane_concat(vr * beta, kb * ecum))
    u, w = _lane_half(sol, 0), _lane_half(sol, 1)
    intra = _bdot_h(qn, kn, NT) * decay
    v_new = u - _bdot_h(w, st, NN)
    out = _bdot_h(qn * ecum, st, NN) + _bdot_h(intra, v_new, NN)
    st_new = st * _exp(cend) + _bdot_h(kn * _exp(cend - cum), v_new, TN)
    res = _rms(out, gn) * _silu(zp)
    return res, st_new


def _delta_heads(hs, heads, qs, ks, vs, ab, zs, alog, dtb, gn, sts):
    res, st_new = _delta_chunk(_Heads(hs), heads, _Heads(qs), _Heads(ks), _Heads(vs), ab, _Heads(zs), alog, dtb, gn,
                               _Heads(sts))
    return res.vals, st_new.vals


def _delta_fwd(qkv, ab, proj, hp, gn, *, heads, z_off, name, comm=()):
    s = qkv.shape[0]
    n = s // CHUNK

    hpb = min(HEADS_PER_STEP, heads)
    assert heads % hpb == 0 and z_off % hpb == 0

    def body(q_ref, k_ref, v_ref, ab_ref, z_ref, hp_ref, gn_ref, o_ref, st_ref, state):
        hb = pl.program_id(1)

        @pl.when(pl.program_id(0) == 0)
        def _():
            for j in range(hpb):
                state[hb * hpb + j] = jnp.zeros((HEAD, HEAD), F32)

        shared = (ab_ref[...], hp_ref[0:1, :], hp_ref[1:2, :], gn_ref[...])
        loaded = [(q_ref[:, _lanes(j)], k_ref[:, _lanes(j)], v_ref[:, _lanes(j)], z_ref[:, _lanes(j)],
                   state[hb * hpb + j]) for j in range(hpb)]
        qs, ks, vs, zs, sts = (list(t) for t in zip(*loaded))
        res, st_new = _delta_heads([hb * hpb + j for j in range(hpb)], heads, qs, ks, vs, shared[0], zs, shared[1],
                                   shared[2], shared[3], sts)
        for j in range(hpb):
            st_ref[j] = sts[j]
            o_ref[:, _lanes(j)] = res[j].astype(o_ref.dtype)
            state[hb * hpb + j] = st_new[j]

    wide = hpb * HEAD
    blk = lambda off: pl.BlockSpec((CHUNK, wide), lambda c, h: (c, off // hpb + h))
    return _call(
        body, (qkv, qkv, qkv, ab, proj, hp, gn), name=name, grid=(n, heads // hpb),
        in_specs=[blk(0), blk(heads), blk(2 * heads), pl.BlockSpec((CHUNK, HEAD), lambda c, h: (c, 0)), blk(z_off),
                  pl.BlockSpec((8, HEAD), lambda c, h: (0, 0)), pl.BlockSpec((1, HEAD), lambda c, h: (0, 0))],
        out_specs=[pl.BlockSpec((CHUNK, wide), lambda c, h: (c, h)),
                   pl.BlockSpec((hpb, None, HEAD, HEAD), lambda c, h: (h, c, 0, 0))],
        out_shape=[jax.ShapeDtypeStruct((s, heads * HEAD), BF16), jax.ShapeDtypeStruct((heads, n, HEAD, HEAD), F32)],
        scratch_shapes=[pltpu.VMEM((heads, HEAD, HEAD), F32)], comm=comm)


def _delta_bwd(qkv, ab, proj, hp, gn, states, d_out, *, heads, z_off, name, comm=()):
    s = qkv.shape[0]
    n = s // CHUNK
    hpb = min(HEADS_PER_STEP, heads)

    def body(q_ref, k_ref, v_ref, ab_ref, z_ref, hp_ref, gn_ref, st_ref, do_ref,
             dq_ref, dk_ref, dv_ref, dab_ref, dz_ref, dhp_ref, dgn_ref, dstate):
        c, hb = pl.program_id(0), pl.program_id(1)

        @pl.when(c == 0)
        def _():
            for j in range(hpb):
                dstate[hb * hpb + j] = jnp.zeros((HEAD, HEAD), F32)

        @pl.when((c == 0) & (hb == 0))
        def _():
            dgn_ref[...] = jnp.zeros_like(dgn_ref)
            dhp_ref[...] = jnp.zeros_like(dhp_ref)

        @pl.when(hb == 0)
        def _():
            dab_ref[...] = jnp.zeros_like(dab_ref)

        shared = (ab_ref[...], hp_ref[0:1, :], hp_ref[1:2, :], gn_ref[...])
        loaded = [(q_ref[:, _lanes(j)], k_ref[:, _lanes(j)], v_ref[:, _lanes(j)], z_ref[:, _lanes(j)], st_ref[j],
                   do_ref[:, _lanes(j)].astype(F32), dstate[hb * hpb + j]) for j in range(hpb)]
        qs, ks, vs, zs, sts, dos, dss = (list(t) for t in zip(*loaded))
        fn = functools.partial(_delta_heads, [hb * hpb + j for j in range(hpb)], heads)
        _, vjp = jax.vjp(fn, qs, ks, vs, shared[0], zs, shared[1], shared[2], shared[3], sts)
        dqs, dks, dvs, dab, dzs, dal, ddt, dgn, dsts = vjp((dos, dss))
        for j in range(hpb):
            ln = _lanes(j)
            dq_ref[:, ln] = dqs[j]
            dk_ref[:, ln] = dks[j]
            dv_ref[:, ln] = dvs[j]
            dz_ref[:, ln] = dzs[j].astype(dz_ref.dtype)
            dstate[hb * hpb + j] = dsts[j]
        dab_ref[...] += dab
        dhp_ref[0:1, :] += dal
        dhp_ref[1:2, :] += ddt
        dgn_ref[...] += dgn

    wide = hpb * HEAD
    rev = lambda off: pl.BlockSpec((CHUNK, wide), lambda c, h: (n - 1 - c, off // hpb + h))
    width = heads * HEAD
    head_blk = pl.BlockSpec((CHUNK, wide), lambda c, h: (n - 1 - c, h))
    ab_blk = pl.BlockSpec((CHUNK, HEAD), lambda c, h: (n - 1 - c, 0))
    return _call(
        body, (qkv, qkv, qkv, ab, proj, hp, gn, states, d_out), name=name, grid=(n, heads // hpb),
        in_specs=[rev(0), rev(heads), rev(2 * heads), ab_blk, rev(z_off),
                  pl.BlockSpec((8, HEAD), lambda c, h: (0, 0)), pl.BlockSpec((1, HEAD), lambda c, h: (0, 0)),
                  pl.BlockSpec((hpb, None, HEAD, HEAD), lambda c, h: (h, n - 1 - c, 0, 0)), head_blk],
        out_specs=[head_blk, head_blk, head_blk, ab_blk, head_blk,
                   pl.BlockSpec((8, HEAD), lambda c, h: (0, 0)), pl.BlockSpec((1, HEAD), lambda c, h: (0, 0))],
        out_shape=[jax.ShapeDtypeStruct((s, width), F32)] * 3
        + [jax.ShapeDtypeStruct((s, HEAD), F32), jax.ShapeDtypeStruct((s, width), BF16),
           jax.ShapeDtypeStruct((8, HEAD), F32), jax.ShapeDtypeStruct((1, HEAD), F32)],
        scratch_shapes=[pltpu.VMEM((heads, HEAD, HEAD), F32)], comm=comm)


def _s5_scan(buf, lt_ref, cin_r, cin_i, tt, reverse):
    nblk = tt // 8
    hl = S5_HALF
    base = 8 if reverse else 0

    def body(j, carry):
        cr, ci = carry
        off = pl.multiple_of((nblk - 1 - j if reverse else j) * 8, 8)
        xr = buf[pl.ds(off, 8), 0:hl]
        xi = buf[pl.ds(off, 8), hl:2 * hl]
        for lv, d in enumerate((1, 2, 4)):
            ar, ai = lt_ref[base + 2 * lv], lt_ref[base + 2 * lv + 1]
            sr = pltpu.roll(xr, 8 - d if reverse else d, 0)
            si = pltpu.roll(xi, 8 - d if reverse else d, 0)
            xr, xi = xr + ar * sr - ai * si, xi + ar * si + ai * sr
        pr, pi = lt_ref[base + 6], lt_ref[base + 7]
        xr, xi = xr + pr * cr - pi * ci, xi + pr * ci + pi * cr
        buf[pl.ds(off, 8), 0:hl] = xr
        buf[pl.ds(off, 8), hl:2 * hl] = xi
        edge = 0 if reverse else 7
        return xr[edge:edge + 1, :], xi[edge:edge + 1, :]

    return lax.fori_loop(0, nblk, body, (cin_r, cin_i))


def _s5_fwd(u, wb, wc, lt, dskip, *, name, tt=1024, comm=()):
    s, d = u.shape
    nb = d // HEAD
    tt = min(tt, s)
    nt = s // tt
    hl = S5_HALF

    def body(u_ref, wb_ref, wc_ref, lt_ref, d_ref, y_ref, cin_ref, st_ref, buf, carry):
        @pl.when(pl.program_id(1) == 0)
        def _():
            carry[...] = jnp.zeros_like(carry)

        cin_ref[...] = carry[0:1, :]
        uv = u_ref[...]
        buf[...] = _bdot_raw(uv, wb_ref[...])
        cr, ci = _s5_scan(buf, lt_ref, carry[0:1, 0:hl], carry[0:1, hl:2 * hl], tt, False)
        carry[0:1, 0:hl] = cr
        carry[0:1, hl:2 * hl] = ci
        states = buf[...].astype(BF16)
        st_ref[...] = states
        y_ref[...] = _bdot_raw(states, wc_ref[...]) + d_ref[...] * uv

    return _call(
        body, (u, wb, wc, lt, dskip), name=name, grid=(nb, nt),
        in_specs=[pl.BlockSpec((tt, HEAD), lambda b, t: (t, b)),
                  pl.BlockSpec((None, HEAD, 2 * hl), lambda b, t: (b, 0, 0)),
                  pl.BlockSpec((None, 2 * hl, HEAD), lambda b, t: (b, 0, 0)),
                  pl.BlockSpec((None, 16, 8, hl), lambda b, t: (b, 0, 0, 0)),
                  pl.BlockSpec((1, HEAD), lambda b, t: (0, b))],
        out_specs=[pl.BlockSpec((tt, HEAD), lambda b, t: (t, b)),
                   pl.BlockSpec((None, None, 1, 2 * hl), lambda b, t: (b, t, 0, 0)),
                   pl.BlockSpec((tt, 2 * hl), lambda b, t: (t, b))],
        out_shape=[jax.ShapeDtypeStruct((s, d), F32), jax.ShapeDtypeStruct((nb, nt, 1, 2 * hl), F32),
                   jax.ShapeDtypeStruct((s, nb * 2 * hl), BF16)],
        scratch_shapes=[pltpu.VMEM((tt, 2 * hl), F32), pltpu.VMEM((8, 2 * hl), F32)], comm=comm)


def _s5_bwd(u, dy, wb, wc, lt, dskip, cins, states, *, name, tt=1024, comm=()):
    s, d = u.shape
    nb = d // HEAD
    tt = min(tt, s)
    nt = s // tt
    hl = S5_HALF

    def body(u_ref, dy_ref, wb_ref, wc_ref, lt_ref, d_ref, cin_ref, st_ref,
             du_ref, dwb_ref, dwc_ref, dd_ref, dlam_ref, abuf, acarry):
        @pl.when(pl.program_id(1) == 0)
        def _():
            acarry[...] = jnp.zeros_like(acarry)
            dwb_ref[...] = jnp.zeros_like(dwb_ref)
            dwc_ref[...] = jnp.zeros_like(dwc_ref)
            dd_ref[...] = jnp.zeros_like(dd_ref)
            dlam_ref[...] = jnp.zeros_like(dlam_ref)

        uv, dyv = u_ref[...], dy_ref[...]
        dy16 = dyv.astype(BF16)
        abuf[...] = _bdot_raw(dy16, wc_ref[...], NT)
        ar, ai = _s5_scan(abuf, lt_ref, acarry[0:1, 0:hl], acarry[0:1, hl:2 * hl], tt, True)
        acarry[0:1, 0:hl] = ar
        acarry[0:1, hl:2 * hl] = ai
        adj16 = abuf[...].astype(BF16)
        du_ref[...] = _bdot_raw(adj16, wb_ref[...], NT) + d_ref[...] * dyv
        dwb_ref[...] += _bdot_raw(uv, adj16, TN)
        dwc_ref[...] += _bdot_raw(st_ref[...], dy16, TN)
        dd_ref[...] += jnp.sum(dyv * uv, axis=0, keepdims=True)
        first = lax.broadcasted_iota(jnp.int32, (tt, hl), 0) == 0
        spr = jnp.where(first, cin_ref[:, 0:hl], pltpu.roll(st_ref[:, 0:hl].astype(F32), 1, 0))
        spi = jnp.where(first, cin_ref[:, hl:2 * hl], pltpu.roll(st_ref[:, hl:2 * hl].astype(F32), 1, 0))
        avr, avi = abuf[:, 0:hl], abuf[:, hl:2 * hl]
        dlam_ref[:, 0:hl] += jnp.sum(avr * spr + avi * spi, axis=0, keepdims=True)
        dlam_ref[:, hl:2 * hl] += jnp.sum(avi * spr - avr * spi, axis=0, keepdims=True)

    rev = pl.BlockSpec((tt, HEAD), lambda b, t: (nt - 1 - t, b))
    return _call(
        body, (u, dy, wb, wc, lt, dskip, cins, states), name=name, grid=(nb, nt),
        in_specs=[rev, rev,
                  pl.BlockSpec((None, HEAD, 2 * hl), lambda b, t: (b, 0, 0)),
                  pl.BlockSpec((None, 2 * hl, HEAD), lambda b, t: (b, 0, 0)),
                  pl.BlockSpec((None, 16, 8, hl), lambda b, t: (b, 0, 0, 0)),
                  pl.BlockSpec((1, HEAD), lambda b, t: (0, b)),
                  pl.BlockSpec((None, None, 1, 2 * hl), lambda b, t: (b, nt - 1 - t, 0, 0)),
                  pl.BlockSpec((tt, 2 * hl), lambda b, t: (nt - 1 - t, b))],
        out_specs=[rev,
                   pl.BlockSpec((None, HEAD, 2 * hl), lambda b, t: (b, 0, 0)),
                   pl.BlockSpec((None, 2 * hl, HEAD), lambda b, t: (b, 0, 0)),
                   pl.BlockSpec((1, HEAD), lambda b, t: (0, b)),
                   pl.BlockSpec((None, 1, 2 * hl), lambda b, t: (b, 0, 0))],
        out_shape=[jax.ShapeDtypeStruct((s, d), F32), jax.ShapeDtypeStruct(wb.shape, F32),
                   jax.ShapeDtypeStruct(wc.shape, F32), jax.ShapeDtypeStruct((1, d), F32),
                   jax.ShapeDtypeStruct((nb, 1, 2 * hl), F32)],
        scratch_shapes=[pltpu.VMEM((tt, 2 * hl), F32), pltpu.VMEM((8, 2 * hl), F32)],
        comm=comm)


def _s5_pack(lr, li, br, bi, c_re, c_im):
    g = lr.shape[0]
    nb = g // S5_GB
    eye = jnp.eye(S5_GB, dtype=F32)
    bm = jnp.stack([br, bi]).reshape(2, nb, S5_GB, S5_STATE, S5_GROUP)
    wb = jnp.einsum("rbgpc,gh->bgcrhp", bm, eye).reshape(nb, HEAD, 2 * S5_HALF)
    cm = jnp.stack([c_re, -c_im]).reshape(2, nb, S5_GB, S5_GROUP, S5_STATE)
    wc = jnp.einsum("rbgcp,gh->brgphc", cm, eye).reshape(nb, 2 * S5_HALF, HEAD)
    pw = [(lr, li)]
    for _ in range(7):
        pr, pi = pw[-1]
        pw.append((pr * lr - pi * li, pr * li + pi * lr))
    blk = lambda a: a.reshape(nb, 1, S5_HALF)
    rows = jnp.arange(8).reshape(1, 8, 1)
    tables = []
    for conj, keep, order in ((1.0, lambda n: rows >= n, range(8)), (-1.0, lambda n: rows < 8 - n, range(7, -1, -1))):
        for n in (1, 2, 4):
            tables += [jnp.where(keep(n), blk(pw[n - 1][0]), 0.0), jnp.where(keep(n), conj * blk(pw[n - 1][1]), 0.0)]
        tables += [jnp.concatenate([blk(pw[n][0]) for n in order], axis=1),
                   jnp.concatenate([conj * blk(pw[n][1]) for n in order], axis=1)]
    return wb, wc, jnp.stack(tables, axis=1)


def _s5_unpack(dwb, dwc, dlam):
    nb = dwb.shape[0]
    g = nb * S5_GB
    eye = jnp.eye(S5_GB, dtype=F32)
    db = jnp.einsum("bgcrhp,gh->rbgpc", dwb.reshape(nb, S5_GB, S5_GROUP, 2, S5_GB, S5_STATE), eye)
    db = db.reshape(2, g, S5_STATE * S5_GROUP)
    dc = jnp.einsum("brgphc,gh->rbgcp", dwc.reshape(nb, 2, S5_GB, S5_STATE, S5_GB, S5_GROUP), eye)
    dc = dc.reshape(2, g, S5_GROUP, S5_STATE)
    dl = dlam.reshape(nb, 2, S5_GB, S5_STATE).transpose(1, 0, 2, 3).reshape(2, g, S5_STATE)
    return dl[0], dl[1], db[0], db[1], dc[0], -dc[1]


def _peer(r):
    mx, my, mc = lax.axis_index("x"), lax.axis_index("y"), lax.axis_index("c")
    px = 1 - mx if r & 4 else mx
    py = 1 - my if r & 2 else my
    pc = 1 - mc if r & 1 else mc
    return (px, py, pc), 4 * px + 2 * py + pc


_COMM_SCRATCH = [pltpu.SemaphoreType.DMA((N_DEV - 1,)), pltpu.SemaphoreType.DMA((N_DEV - 1,)), pltpu.SemaphoreType.DMA]


class _AllToAll:
    def __init__(self, x, rows=None):
        self.x = x
        self.rows = rows
        shape = x.shape if rows is None else (x.shape[0], rows[1]) + tuple(x.shape[2:])
        self.out_shape = jax.ShapeDtypeStruct(shape, x.dtype)

    def _copies(self, x_ref, out_ref, send_sems, recv_sems, local_sem):
        def block(j):
            return x_ref.at[j] if self.rows is None else x_ref.at[j, pl.ds(self.rows[0], self.rows[1])]

        _, me = _peer(0)
        mine = pltpu.make_async_copy(block(me), out_ref.at[me], local_sem)
        sends, recvs = [], []
        for r in range(1, N_DEV):
            pos, idx = _peer(r)
            sems = dict(send_sem=send_sems.at[r - 1], recv_sem=recv_sems.at[r - 1], device_id=pos, device_id_type=MESH)
            sends.append(pltpu.make_async_remote_copy(src_ref=block(idx), dst_ref=out_ref.at[me], **sems))
            recvs.append(pltpu.make_async_remote_copy(src_ref=block(idx), dst_ref=out_ref.at[idx], **sems))
        return mine, sends, recvs

    def start(self, *refs):
        mine, sends, _ = self._copies(*refs)
        mine.start()
        for cp in sends:
            cp.start()

    def finish(self, *refs):
        mine, sends, recvs = self._copies(*refs)
        for cp in recvs:
            cp.wait_recv()
        for cp in sends:
            cp.wait_send()
        mine.wait()


class _Gather:
    def __init__(self, x):
        self.x = x
        self.out_shape = jax.ShapeDtypeStruct((N_DEV,) + tuple(x.shape), x.dtype)

    def _copies(self, x_ref, out_ref, send_sems, recv_sems, local_sem):
        mx, my, mc = lax.axis_index("x"), lax.axis_index("y"), lax.axis_index("c")
        me, sibling = (mx, my, mc), (mx, my, 1 - mc)
        chips = [(1 - mx, my), (mx, 1 - my), (1 - mx, 1 - my)]

        def slot(px, py, pc):
            return out_ref.at[4 * px + 2 * py + pc]

        def copy(k, block, to, src=None):
            return pltpu.make_async_remote_copy(
                src_ref=slot(*block) if src is None else src, dst_ref=slot(*block),
                send_sem=send_sems.at[k], recv_sem=recv_sems.at[k], device_id=to, device_id_type=MESH)

        return dict(
            mine=pltpu.make_async_copy(x_ref, slot(*me), local_sem),
            first=[copy(0, me, sibling, src=x_ref)] + [copy(1 + j, me, (*chip, mc), src=x_ref) for j, chip in enumerate(chips)],
            passed=[copy(4 + j, (*chip, mc), sibling) for j, chip in enumerate(chips)],
            over_ici=[copy(1 + j, (*chip, mc), me) for j, chip in enumerate(chips)],
            from_sibling=[copy(0, sibling, me)] + [copy(4 + j, (*chip, 1 - mc), me) for j, chip in enumerate(chips)])

    def start(self, *refs):
        cps = self._copies(*refs)
        cps["mine"].start()
        for cp in cps["first"]:
            cp.start()

    def finish(self, *refs):
        cps = self._copies(*refs)
        for arrived, onward in zip(cps["over_ici"], cps["passed"]):
            arrived.wait_recv()
            onward.start()
        for cp in cps["from_sibling"]:
            cp.wait_recv()
        for cp in cps["first"] + cps["passed"]:
            cp.wait_send()
        cps["mine"].wait()


def _call(body, args, *, name, grid, in_specs, out_specs, out_shape, scratch_shapes=(), comm=()):
    n_in, n_out, n_scr, nc = len(in_specs), len(out_shape), len(scratch_shapes), len(comm)

    def wrapped(*refs):
        ins, c_in = refs[:n_in], refs[n_in:n_in + nc]
        outs, c_out = refs[n_in + nc:n_in + nc + n_out], refs[n_in + nc + n_out:n_in + 2 * nc + n_out]
        scr = refs[n_in + 2 * nc + n_out:n_in + 2 * nc + n_out + n_scr]
        sems = refs[n_in + 2 * nc + n_out + n_scr:]
        ids = [pl.program_id(a) for a in range(len(grid))]
        if nc:
            @pl.when(functools.reduce(operator.and_, [i == 0 for i in ids]))
            def _():
                for k, op in enumerate(comm):
                    op.start(c_in[k], c_out[k], *sems[3 * k:3 * k + 3])

        body(*ins, *outs, *scr)
        if nc:
            @pl.when(functools.reduce(operator.and_, [i == g - 1 for i, g in zip(ids, grid)]))
            def _():
                for k, op in enumerate(comm):
                    op.finish(c_in[k], c_out[k], *sems[3 * k:3 * k + 3])

    any_spec = pl.BlockSpec(memory_space=pl.ANY)
    res = pl.pallas_call(
        wrapped, name=name, grid=grid,
        in_specs=list(in_specs) + [any_spec] * nc, out_specs=list(out_specs) + [any_spec] * nc,
        out_shape=list(out_shape) + [op.out_shape for op in comm],
        scratch_shapes=list(scratch_shapes) + list(_COMM_SCRATCH) * nc,
        compiler_params=_params(len(grid)),
    )(*args, *[op.x for op in comm])
    return list(res[:n_out]), list(res[n_out:])


def _comm_call(op, *, name):
    return _call(lambda: None, (), name=name, grid=(1,), in_specs=[], out_specs=[], out_shape=[], comm=(op,))[1][0]


def _adamw(w, parts, m, v, *, name, tr=128, comm=()):
    nl, r, c = w.shape
    assert len(parts) == nl
    parts = [list(p) if isinstance(p, (list, tuple)) else [p] for p in parts]
    npart = parts[0][0].shape[0]
    tr = min(tr, r, *[pc.shape[1] for p in parts for pc in p])
    assert r % tr == 0 and all(pc.shape[1] % tr == 0 for p in parts for pc in p), (name, r, tr)
    pieces = []
    for l, p in enumerate(parts):
        first = 0
        for pc in p:
            pieces.append((l, first, pc.shape[1] // tr, pc))
            first += pc.shape[1] // tr
        assert first == r // tr, (name, l)

    def body(w_ref, m_ref, v_ref, *rest):
        p_refs, (g_ref, d_ref, mo_ref, vo_ref) = rest[:len(pieces)], rest[len(pieces):]
        layer, tile = pl.program_id(0), pl.program_id(1)
        for p_ref, (l, first, count, _) in zip(p_refs, pieces):
            @pl.when((layer == l) & (tile >= first) & (tile < first + count))
            def _():
                g = p_ref[0].astype(F32)
                for k in range(1, npart):
                    g = g + p_ref[k].astype(F32)
                m2 = ADAM_B1 * m_ref[...] + (1.0 - ADAM_B1) * g
                v2 = ADAM_B2 * v_ref[...] + (1.0 - ADAM_B2) * (g * g)
                m_hat = m2 / (1.0 - ADAM_B1 ** ADAM_STEP)
                v_hat = v2 / (1.0 - ADAM_B2 ** ADAM_STEP)
                g_ref[...] = g
                d_ref[...] = -ADAM_LR * (m_hat / (jnp.sqrt(v_hat) + ADAM_EPS) + ADAM_WD * w_ref[...])
                mo_ref[...] = m2
                vo_ref[...] = v2

    blk = pl.BlockSpec((None, tr, c), lambda l, i: (l, i, 0))

    def part_spec(l, first, count):
        return pl.BlockSpec((npart, tr, c), lambda ll, i: (0, jnp.where(ll == l, jnp.clip(i - first, 0, count - 1), 0), 0))

    outs, exchanged = _call(
        body, (w, m, v, *[pc for _, _, _, pc in pieces]), name=name, grid=(nl, r // tr),
        in_specs=[blk, blk, blk] + [part_spec(l, first, count) for l, first, count, _ in pieces],
        out_specs=[blk] * 4, out_shape=[jax.ShapeDtypeStruct((nl, r, c), F32)] * 4, comm=comm)
    return (outs, exchanged) if comm else outs


def _sum_parts(parts, *, name):
    npart = parts.shape[0]

    def body(p_ref, o_ref):
        g = p_ref[0]
        for k in range(1, npart):
            g = g + p_ref[k]
        o_ref[...] = g

    return pl.pallas_call(
        body, name=name, grid=(1,), in_specs=[_full_spec(parts)],
        out_specs=pl.BlockSpec(parts.shape[1:], lambda i: (0, 0)),
        out_shape=jax.ShapeDtypeStruct(parts.shape[1:], F32), compiler_params=_params(1),
    )(parts)


def _pack(arrs):
    blocks = []
    for a in arrs:
        flat = a.reshape(-1).astype(F32)
        blocks.append(jnp.pad(flat, (0, (-flat.shape[0]) % (8 * HEAD))).reshape(-1, HEAD))
    out = jnp.concatenate(blocks, axis=0)
    return jnp.pad(out, ((0, (-out.shape[0]) % HEAD), (0, 0)))


def _unpack(packed, shapes):
    out, off = [], 0
    for shp in shapes:
        size = math.prod(shp)
        rows = -(-size // (8 * HEAD)) * 8
        out.append(packed[off:off + rows].reshape(-1)[:size].reshape(shp))
        off += rows
    return out


def _add_epilogue(acc, res):
    return (acc + res,)


def _relu2_epilogue(acc):
    r = jnp.maximum(acc, 0.0)
    return acc, r * r


def _ple_epilogue(acc, gpre, h):
    return h + jax.nn.sigmoid(gpre) * acc, acc


def kernel(x, p, norm_mix, norm_mlp, norm_ple, w_in_e, w_out_e, hgrn_lb, g_norm_a, conv_w, a_log, dt_bias, g_norm_b, s5_a_re, s5_a_im, s5_b_re, s5_b_im, s5_c_re, s5_c_im, s5_d, s5_log_dt, w_glu, b_glu, w_out_o, w_up, w_down, w_ple_gate, w_ple_proj, final_norm, loss_target, m_norm_mix, m_norm_mlp, m_norm_ple, m_w_in_e, m_w_out_e, m_hgrn_lb, m_g_norm_a, m_conv_w, m_a_log, m_dt_bias, m_g_norm_b, m_s5_a_re, m_s5_a_im, m_s5_b_re, m_s5_b_im, m_s5_c_re, m_s5_c_im, m_s5_d, m_s5_log_dt, m_w_glu, m_b_glu, m_w_out_o, m_w_up, m_w_down, m_w_ple_gate, m_w_ple_proj, m_final_norm, v_norm_mix, v_norm_mlp, v_norm_ple, v_w_in_e, v_w_out_e, v_hgrn_lb, v_g_norm_a, v_conv_w, v_a_log, v_dt_bias, v_g_norm_b, v_s5_a_re, v_s5_a_im, v_s5_b_re, v_s5_b_im, v_s5_c_re, v_s5_c_im, v_s5_d, v_s5_log_dt, v_w_glu, v_b_glu, v_w_out_o, v_w_up, v_w_down, v_w_ple_gate, v_w_ple_proj, v_final_norm):
    args = dict(locals())
    s, d = x.shape[1], x.shape[2]
    aw = d // 2
    ha = hb = aw // HEAD
    main = 4 * d
    z_col = 2 * d + 3 * aw
    ff = w_up.shape[2] * N_DEV
    ple = p.shape[-1]
    groups = d // S5_GROUP
    me = 4 * lax.axis_index("x") + 2 * lax.axis_index("y") + lax.axis_index("c")
    x2, target = x[0], loss_target[0]
    row = lambda a, i: a[i:i + 1]

    def gather_of(w):
        return _Gather(w.astype(BF16))

    w_in = jnp.transpose(_comm_call(gather_of(w_in_e[0]), name="ag_w_in"), (1, 0, 2)).reshape(d, -1)
    w_main = w_in[:, :main]
    w_tail = jnp.pad(w_in[:, main:], ((0, 0), (0, HEAD - 2 * hb)))

    lb_rows = [row(hgrn_lb, 0), row(hgrn_lb, 1), row(hgrn_lb, 2)]
    (lb0,) = _small_call(_lb0_stage, lb_rows, name="f_lb0")
    hp = jnp.zeros((8, HEAD), F32).at[0, :hb].set(a_log[0]).at[1, :hb].set(dt_bias[0])
    expand = jnp.asarray(np.kron(np.eye(S5_STATE, dtype=np.float32), np.ones((1, S5_GROUP), np.float32)))
    prep_in = [s5_a_re[0], s5_a_im[0], s5_log_dt[0].reshape(groups, 1),
               s5_b_re[0].reshape(groups, -1), s5_b_im[0].reshape(groups, -1), expand]
    lr, li, br, bi = _small_call(_s5_prep_stage, prep_in, name="f_s5_prep")
    wb, wc, lt = _s5_pack(lr, li, br, bi, s5_c_re[0], s5_c_im[0])
    wb, wc = wb.astype(BF16), wc.astype(BF16)
    fnorm = final_norm.reshape(1, d)

    w_upg = []

    def block_fwd(h, l):
        hn = _rows_call(_rms_stage, [h], [row(norm_mlp, l)], [BF16], name=f"f_norm_mlp{l}")
        (up, act), (dn8,) = _mm(hn, w_upg[l], epilogue=_relu2_epilogue, out_dtypes=(F32, BF16), name=f"f_up{l}",
                                comm=(gather_of(w_down[l]),))
        w_dn = dn8.reshape(ff, d)
        h2 = _mm(act, w_dn, extras=(h,), epilogue=_add_epilogue, name=f"f_down{l}")
        hq = _rows_call(_rms_stage, [h2], [row(norm_ple, l)], [BF16], name=f"f_norm_ple{l}")
        gpre = _mm(hq, w_pgg[l], name=f"f_ple_gate{l}")
        h3, pp = _mm(p[l, 0], w_ppg[l], extras=(gpre, h2), epilogue=_ple_epilogue, out_dtypes=(F32, F32),
                     name=f"f_ple_proj{l}")
        return h3, dict(h=h, hn=hn, up=up, act=act, h2=h2, hq=hq, gpre=gpre, pp=pp, w_dn=w_dn)

    hn0 = _rows_call(_rms_stage, [x2], [row(norm_mix, 0)], [BF16], name="f_norm_mix0")
    shard_shapes = [conv_w[0].shape, s5_d.shape, b_glu.shape]
    proj, (oe8, pg8, small) = _mm(hn0, w_main, name="f_proj", comm=(
        gather_of(w_out_e[0]), gather_of(w_ple_gate), _Gather(_pack([conv_w[0], s5_d, b_glu]))))
    w_oe = oe8.reshape(d, d)
    w_top, w_bot = w_oe[:aw], w_oe[aw:]
    w_pgg = jnp.transpose(pg8, (1, 0, 2, 3)).reshape(2, d, d)
    conv_g, s5d_g, bglu_g = zip(*[_unpack(small[j], shard_shapes) for j in range(N_DEV)])
    conv_full = jnp.concatenate(conv_g, axis=1)
    s5d_full = jnp.concatenate(s5d_g, axis=1)
    bglu_full = jnp.concatenate(bglu_g, axis=1)
    ab = _mm(hn0, w_tail, name="f_ab")
    (oa, st_a), (gl8, oo8) = _hgrn_fwd(proj, lb0, g_norm_a, heads=ha, name="f_hgrn",
                                       comm=(gather_of(w_glu[0]), gather_of(w_out_o[0])))
    w_gl, w_oo = gl8.reshape(d, d), oo8.reshape(d, d)
    qkv = _conv_fwd(proj, conv_full, col_off=2 * d, name="f_conv")
    slots_to_cols = lambda g8: jnp.transpose(g8, (1, 0, 2)).reshape(g8.shape[1], -1)
    (ob, st_b), (up8,) = _delta_fwd(qkv, ab, proj, hp, g_norm_b, heads=hb, z_off=z_col // HEAD, name="f_delta",
                                    comm=(gather_of(w_up[0]),))
    w_upg.append(slots_to_cols(up8))
    h1, (pp8,) = _mm(oa, w_top, extras=(x2,), epilogue=_add_epilogue, name="f_out_a", comm=(gather_of(w_ple_proj),))
    w_ppg = jnp.transpose(pp8, (1, 2, 0, 3)).reshape(2, ple, d)
    h1 = _mm(ob, w_bot, extras=(h1,), epilogue=_add_epilogue, name="f_out_b")
    h3, sv0 = block_fwd(h1, 0)

    u = _rows_call(_rms_stage, [h3], [row(norm_mix, 1)], [F32], name="f_norm_mix1")
    (y, cins, s5_states), (up8,) = _s5_fwd(u, wb, wc, lt, s5d_full, name="f_s5", comm=(gather_of(w_up[1]),))
    w_upg.append(slots_to_cols(up8))
    act_g = _rows_call(_gelu_stage, [y], [], [BF16], name="f_gelu")
    gl_raw = _mm(act_g, w_gl, name="f_glu")
    glu = _rows_call(_glu_stage, [y, gl_raw], [bglu_full], [BF16], name="f_glu_gate")
    h4 = _mm(glu, w_oo, extras=(h3,), epilogue=_add_epilogue, name="f_out_o")
    h6, sv1 = block_fwd(h4, 1)
    dh, d_fnorm, loss8 = _loss_call(h6, fnorm, target, name="loss")
    loss = lax.psum(loss8[0, 0], ("x", "y", "c"))

    dshard, ffs, cols = d // N_DEV, ff // N_DEV, w_in_e.shape[2]
    rows8 = lambda g: _AllToAll(g.reshape(N_DEV, -1, g.shape[-1]))
    col_slots = lambda g: jnp.transpose(g.reshape(g.shape[0], N_DEV, -1), (1, 0, 2))
    cols8 = lambda g: _AllToAll(col_slots(g))

    def halves(g8):
        r = g8.shape[1] // 2
        return _AllToAll(g8, rows=(0, r)), _AllToAll(g8, rows=(r, r))

    def block_bwd(dh3, l, sv, carried=(), carried_wup=(), carried_up=()):
        (dgpre, dpp), _ = _rows_vjp(_ple_stage, [sv["h2"], sv["gpre"], sv["pp"]], [], [dh3],
                                    row_grads={1: BF16, 2: BF16}, name=f"b_ple{l}")
        g_pp = _mm(p[l, 0], dpp, ta=True, out_dtypes=(BF16,), name=f"b_w_ple_proj{l}")
        g_pg = _mm(sv["hq"], dgpre, ta=True, out_dtypes=(BF16,), name=f"b_w_ple_gate{l}")
        dhq = _mm(dgpre, w_pgg[l], tb=True, name=f"b_ple_gate{l}")
        (dh2,), (g_nple,) = _rows_vjp(_rms_stage, [sv["h2"]], [row(norm_ple, l)], [dhq], row_grads={0: F32},
                                      adds={0: dh3}, name=f"b_norm_ple{l}")
        dup, (r_pg, r_pp) = _mm(dh2, sv["w_dn"], tb=True, extras=(sv["up"],), epilogue=_relu2_grad_epilogue,
                                out_dtypes=(BF16,), name=f"b_down{l}", comm=(rows8(g_pg), cols8(g_pp)))
        g_dn = _mm(sv["act"], dh2, ta=True, out_dtypes=(BF16,), name=f"b_w_down{l}", comm=carried)
        g_dn, r_carried = g_dn if carried else (g_dn, [])
        g_up = _mm(sv["hn"], dup, ta=True, out_dtypes=(BF16,), tn=ffs, out_slots=True, name=f"b_w_up{l}",
                   comm=carried_wup)
        g_up, r_carried_wup = g_up if carried_wup else (g_up, [])
        dhn = _mm(dup, w_upg[l], tb=True, name=f"b_up{l}", comm=carried_up)
        dhn, r_carried_up = dhn if carried_up else (dhn, [])
        (dh0,), (g_nmlp,) = _rows_vjp(_rms_stage, [sv["h"]], [row(norm_mlp, l)], [dhn], row_grads={0: F32},
                                      adds={0: dh2}, name=f"b_norm_mlp{l}")
        return dh0, dict(w_ple_proj=r_pp, w_ple_gate=r_pg, norm_ple=g_nple, w_down=g_dn, w_up=g_up, norm_mlp=g_nmlp,
                         carried=r_carried, carried_wup=r_carried_wup, carried_up=r_carried_up)

    dh4, gb1 = block_bwd(dh, 1, sv1)
    up1_a, up1_b = halves(gb1["w_up"])
    dglu = _mm(dh4, w_oo, tb=True, name="b_out_o")
    g_oo = _mm(glu, dh4, ta=True, out_dtypes=(BF16,), name="b_w_out_o")
    (dy1, dgl), (g_bglu,) = _rows_vjp(_glu_stage, [y, gl_raw], [bglu_full], [dglu], row_grads={0: F32, 1: BF16},
                                      name="b_glu_gate")
    g_gl = _mm(act_g, dgl, ta=True, out_dtypes=(BF16,), name="b_w_glu")
    dact = _mm(dgl, w_gl, tb=True, name="b_glu")
    (dy,), _ = _rows_vjp(_gelu_stage, [y], [], [dact], row_grads={0: F32}, adds={0: dy1}, name="b_gelu")
    (du, dwb, dwc, g_s5d, dlam), (r_dn1, r_up1_a) = _s5_bwd(u, dy, wb, wc, lt, s5d_full, cins, s5_states, name="b_s5",
                                                           comm=(rows8(gb1["w_down"]), up1_a))
    (dh3,), (g_nmix1,) = _rows_vjp(_rms_stage, [h3], [row(norm_mix, 1)], [du], row_grads={0: F32}, adds={0: dh4},
                                   name="b_norm_mix1")
    dlr, dli, dbr, dbi, g_cre, g_cim = _s5_unpack(dwb, dwc, dlam)
    g_are, g_aim, g_ldt, g_bre, g_bim, _ = _small_vjp(_s5_prep_stage, prep_in, [dlr, dli, dbr, dbi], name="b_s5_prep")

    early_grads = dict(
        s5_a_re=g_are[None], s5_a_im=g_aim[None], s5_b_re=g_bre.reshape(s5_b_re.shape),
        s5_b_im=g_bim.reshape(s5_b_im.shape), s5_c_re=g_cre[None], s5_c_im=g_cim[None],
        s5_log_dt=g_ldt.reshape(1, groups), final_norm=d_fnorm.reshape(d), s5_d=g_s5d, b_glu=g_bglu)
    dh1, gb0 = block_bwd(dh3, 0, sv0, (rows8(g_oo), rows8(g_gl)), (up1_b,),
                         (_Gather(_pack(list(early_grads.values()))),))
    r_oo, r_gl = gb0["carried"]
    r_up1 = [r_up1_a, gb0["carried_wup"][0]]
    (early_parts,) = gb0["carried_up"]
    dn0_a, dn0_b = halves(gb0["w_down"].reshape(N_DEV, ffs, d))
    up0_a, up0_b = halves(gb0["w_up"])
    doa = _mm(dh1, w_top, tb=True, name="b_out_a")
    dob = _mm(dh1, w_bot, tb=True, name="b_out_b")
    g_oe = jnp.concatenate([_mm(oa, dh1, ta=True, out_dtypes=(BF16,), name="b_w_out_a"),
                            _mm(ob, dh1, ta=True, out_dtypes=(BF16,), name="b_w_out_b")], axis=0)
    (dq, df, di, dg, dlb, g_gna), (r_dn0_a,) = _hgrn_bwd(proj, lb0, g_norm_a, st_a, doa, heads=ha, name="b_hgrn",
                                                        comm=(dn0_a,))
    (dqb, dkb, dvb, dab, dz, dhp, g_gnb), (r_dn0_b, r_up0_a) = _delta_bwd(
        qkv, ab, proj, hp, g_norm_b, st_b, dob, heads=hb, z_off=z_col // HEAD, name="b_delta", comm=(dn0_b, up0_a))
    (dqkv, g_conv), (r_oe,) = _conv_bwd(proj, conv_full, jnp.concatenate([dqb, dkb, dvb], axis=1), col_off=2 * d,
                                        name="b_conv", comm=(rows8(g_oe),))
    dproj = jnp.concatenate([dq, df, di, dg, dqkv, dz], axis=1)
    g_main, (r_up0_b,) = _mm(hn0, dproj, ta=True, out_dtypes=(BF16,), name="b_w_proj", comm=(up0_b,))
    r_dn0, r_up0 = [r_dn0_a, r_dn0_b], [r_up0_a, r_up0_b]
    g_tail = _mm(hn0, dab, ta=True, out_dtypes=(BF16,), name="b_w_ab")
    in8 = col_slots(jnp.concatenate([g_main, g_tail[:, :2 * hb]], axis=1))
    in_a = _AllToAll(in8, rows=(0, d // 2))
    in_b, in_c = (_AllToAll(in8, rows=(d // 2 + j * (d // 4), d // 4)) for j in range(2))
    dhn0, (r_in_a,) = _mm(dproj, w_main, tb=True, name="b_proj", comm=(in_a,))
    dhn0 = _mm(dab, w_tail, tb=True, extras=(dhn0,), epilogue=_add_epilogue, name="b_ab")
    (dx,), (g_nmix0,) = _rows_vjp(_rms_stage, [x2], [row(norm_mix, 0)], [dhn0], row_grads={0: F32}, adds={0: dh1},
                                  name="b_norm_mix0")
    g_lb = jnp.concatenate(_small_vjp(_lb0_stage, lb_rows, [dlb], name="b_lb0"), axis=0)

    late_grads = dict(
        norm_mix=jnp.concatenate([g_nmix0, g_nmix1], axis=0),
        norm_mlp=jnp.concatenate([gb0["norm_mlp"], gb1["norm_mlp"]], axis=0),
        norm_ple=jnp.concatenate([gb0["norm_ple"], gb1["norm_ple"]], axis=0),
        hgrn_lb=g_lb, g_norm_a=g_gna, a_log=dhp[0:1, :hb], dt_bias=dhp[1:2, :hb], g_norm_b=g_gnb, conv_w=g_conv)
    rep_names = ["norm_mix", "norm_mlp", "norm_ple", "hgrn_lb", "g_norm_a", "a_log", "dt_bias", "g_norm_b", "s5_a_re",
                 "s5_a_im", "s5_b_re", "s5_b_im", "s5_c_re", "s5_c_im", "s5_log_dt", "final_norm"]
    late_parts = _comm_call(_Gather(_pack(list(late_grads.values()))), name="ag_small_grads")
    summed = {}
    for tag, grads, parts in (("early", early_grads, early_parts), ("late", late_grads, late_parts)):
        sums = _unpack(_sum_parts(parts, name=f"sum_small_grads_{tag}"), [g.shape for g in grads.values()])
        summed.update(zip(grads, sums))
    cw = conv_w.shape[2]
    dshard = d // N_DEV
    shard_g = dict(conv_w=lax.dynamic_slice(summed["conv_w"], (0, me * cw), (CONV_WIDTH, cw))[None],
                   s5_d=lax.dynamic_slice(summed["s5_d"], (0, me * dshard), (1, dshard)),
                   b_glu=lax.dynamic_slice(summed["b_glu"], (0, me * dshard), (1, dshard)))
    small_names = rep_names + ["conv_w", "s5_d", "b_glu"]
    g_small = [summed[k] if k in rep_names else shard_g[k] for k in small_names]
    shapes = [args[k].shape for k in small_names]
    sm_out = _adamw(_pack([args[k] for k in small_names])[None], [_pack(g_small)[None]],
                    _pack([args["m_" + k] for k in small_names])[None], _pack([args["v_" + k] for k in small_names])[None],
                    name="adamw_small")
    sm_out = [dict(zip(small_names, _unpack(o[0], shapes))) for o in sm_out]

    up_out, (r_in_b,) = _adamw(w_up, [r_up0, r_up1], m_w_up, v_w_up, name="adamw_w_up", comm=(in_b,))
    down_out, (r_in_c,) = _adamw(w_down, [r_dn0, r_dn1], m_w_down, v_w_down, name="adamw_w_down", comm=(in_c,))
    received = dict(w_in_e=[[r_in_a, r_in_b, r_in_c]], w_out_e=[r_oe], w_glu=[r_gl], w_out_o=[r_oo],
                    w_ple_gate=[gb0["w_ple_gate"], gb1["w_ple_gate"]], w_ple_proj=[gb0["w_ple_proj"], gb1["w_ple_proj"]])
    big_out = {k: _adamw(args[k], layers, args["m_" + k], args["v_" + k], name="adamw_" + k)
               for k, layers in received.items()}
    big_out.update(w_up=up_out, w_down=down_out)

    names = ["norm_mix", "norm_mlp", "norm_ple", "w_in_e", "w_out_e", "hgrn_lb", "g_norm_a", "conv_w", "a_log", "dt_bias",
             "g_norm_b", "s5_a_re", "s5_a_im", "s5_b_re", "s5_b_im", "s5_c_re", "s5_c_im", "s5_d", "s5_log_dt", "w_glu",
             "b_glu", "w_out_o", "w_up", "w_down", "w_ple_gate", "w_ple_proj", "final_norm"]
    result = [loss, dx[None]]
    for j in range(4):
        result += [big_out[k][j] if k in big_out else sm_out[j][k] for k in names]
    return tuple(result)
```

```python
import functools
import math
import operator

import numpy as np
import jax
import jax.numpy as jnp
from jax import lax
from jax.experimental import pallas as pl
from jax.experimental.pallas import tpu as pltpu

F32 = jnp.float32
BF16 = jnp.bfloat16
MM_DTYPE = BF16
HI = lax.Precision.HIGHEST
MESH = pl.DeviceIdType.MESH

NORM_EPS = 1e-6
CHUNK = 64
HEAD = 128
CONV_WIDTH = 4
S5_GROUP = 16
S5_STATE = 64
S5_GB = 8
S5_HALF = S5_GB * S5_STATE
N_DEV = 8
HEADS_PER_STEP = 8
HGRN_SUB = 16
ADAM_LR, ADAM_B1, ADAM_B2, ADAM_EPS, ADAM_WD, ADAM_STEP = 0.001, 0.9, 0.999, 1e-08, 0.01, 10
VMEM_LIMIT = 56 * 1024 * 1024

NN = (((1,), (0,)), ((), ()))
NT = (((1,), (1,)), ((), ()))
TN = (((0,), (0,)), ((), ()))


def _dot(a, b, dn=NN):
    return lax.dot_general(a, b, dn, precision=HI, preferred_element_type=F32)


def _hdot(a, b, dn=NN):
    return lax.dot_general(a, b, dn, precision=lax.Precision.HIGH, preferred_element_type=F32)


def _bdot_raw(a, b, dn=NN):
    return lax.dot_general(a.astype(BF16), b.astype(BF16), dn, preferred_element_type=F32)


@functools.partial(jax.custom_vjp, nondiff_argnums=(2,))
def _bdot(a, b, dn):
    return _bdot_raw(a, b, dn)


def _bdot_fwd(a, b, dn):
    return _bdot_raw(a, b, dn), (a, b)


def _bdot_bwd(dn, res, g):
    a, b = res
    if dn == NN:
        return _bdot_raw(g, b, NT), _bdot_raw(a, g, TN)
    if dn == NT:
        return _bdot_raw(g, b, NN), _bdot_raw(g, a, TN)
    assert dn == TN
    return _bdot_raw(b, g, NT), _bdot_raw(a, g, NN)


_bdot.defvjp(_bdot_fwd, _bdot_bwd)


def _per_head(f):
    def g(*args, **kw):
        n = [len(a.vals) for a in args if isinstance(a, _Heads)]
        if not n:
            return f(*args, **kw)
        return _Heads([f(*[a.vals[j] if isinstance(a, _Heads) else a for a in args], **kw) for j in range(n[0])])
    return g


class _Heads:
    def __init__(self, vals):
        self.vals = list(vals)

    def __add__(self, o):
        return _per_head(operator.add)(self, o)

    def __radd__(self, o):
        return _per_head(operator.add)(o, self)

    def __sub__(self, o):
        return _per_head(operator.sub)(self, o)

    def __rsub__(self, o):
        return _per_head(operator.sub)(o, self)

    def __mul__(self, o):
        return _per_head(operator.mul)(self, o)

    def __rmul__(self, o):
        return _per_head(operator.mul)(o, self)

    def __neg__(self):
        return _per_head(operator.neg)(self)


_exp, _log, _where, _sum, _mean = (_per_head(f) for f in (jnp.exp, jnp.log, jnp.where, jnp.sum, jnp.mean))
_sigmoid, _rsqrt, _equal = _per_head(jax.nn.sigmoid), _per_head(lax.rsqrt), _per_head(operator.eq)
_hdot_h, _bdot_h = _per_head(_hdot), _per_head(_bdot)
_rows = _per_head(lambda a, lo, n: a[lo:lo + n, :])
_row_concat = _per_head(lambda *xs: jnp.concatenate(xs, axis=0))


def _params(n_axes):
    return pltpu.CompilerParams(dimension_semantics=("arbitrary",) * n_axes, vmem_limit_bytes=VMEM_LIMIT)


def _full_spec(a):
    nd = a.ndim
    return pl.BlockSpec(a.shape, lambda *_: (0,) * nd)


def _mm(a, b, *, name, ta=False, tb=False, extras=(), epilogue=None, out_dtypes=(F32,), tm=1024, tn=1024, tk=2048,
        out_slots=False, comm=()):
    m = a.shape[1] if ta else a.shape[0]
    k = a.shape[0] if ta else a.shape[1]
    n = b.shape[0] if tb else b.shape[1]
    assert k == (b.shape[1] if tb else b.shape[0]), (name, a.shape, b.shape)
    tm, tn, tk = min(tm, m), min(tn, n), min(tk, k)
    assert m % tm == 0 and n % tn == 0 and k % tk == 0, (name, m, n, k)
    nk = k // tk
    n_ex, n_out = len(extras), len(out_dtypes)
    dn = (((0 if ta else 1,), (1 if tb else 0,)), ((), ()))

    def body(a_ref, b_ref, *rest):
        ex_refs, out_refs = rest[:n_ex], rest[n_ex:n_ex + n_out]
        part = lax.dot_general(a_ref[...].astype(MM_DTYPE), b_ref[...].astype(MM_DTYPE), dn, preferred_element_type=F32)

        def finish(acc):
            outs = epilogue(acc, *[r[...] for r in ex_refs]) if epilogue is not None else (acc,)
            for o_ref, o in zip(out_refs, outs):
                o_ref[...] = o.astype(o_ref.dtype)

        if nk == 1:
            finish(part)
            return
        acc_ref = rest[-1]
        kk = pl.program_id(2)

        @pl.when(kk == 0)
        def _():
            acc_ref[...] = part

        @pl.when((kk > 0) & (kk < nk - 1))
        def _():
            acc_ref[...] += part

        @pl.when(kk == nk - 1)
        def _():
            finish(acc_ref[...] + part)

    a_spec = pl.BlockSpec((tk, tm), lambda i, j, q: (q, i)) if ta else pl.BlockSpec((tm, tk), lambda i, j, q: (i, q))
    b_spec = pl.BlockSpec((tn, tk), lambda i, j, q: (j, q)) if tb else pl.BlockSpec((tk, tn), lambda i, j, q: (q, j))
    ex_specs = []
    for e in extras:
        if e.shape[0] == 1 and m != 1:
            ex_specs.append(pl.BlockSpec((1, tn), lambda i, j, q: (0, j)))
        else:
            ex_specs.append(pl.BlockSpec((tm, tn), lambda i, j, q: (i, j)))
    if out_slots:
        out_spec, out_dims = pl.BlockSpec((None, tm, tn), lambda i, j, q: (j, i, 0)), (n // tn, m, tn)
    else:
        out_spec, out_dims = pl.BlockSpec((tm, tn), lambda i, j, q: (i, j)), (m, n)
    outs, exchanged = _call(
        body, (a, b, *extras), name=name, grid=(m // tm, n // tn, nk),
        in_specs=[a_spec, b_spec] + ex_specs,
        out_specs=[out_spec for _ in out_dtypes],
        out_shape=[jax.ShapeDtypeStruct(out_dims, dt) for dt in out_dtypes],
        scratch_shapes=[pltpu.VMEM((tm, tn), F32)] if nk > 1 else [], comm=comm)
    outs = outs[0] if n_out == 1 else tuple(outs)
    return (outs, exchanged) if comm else outs


def _rows_call(fn, rows, consts, out_dtypes, *, name, tr=256):
    s = rows[0].shape[0]
    tr = min(tr, s)
    nr, nc = len(rows), len(consts)
    widths = [o.shape[1] for o in jax.eval_shape(
        fn, *[jax.ShapeDtypeStruct((tr, r.shape[1]), F32) for r in rows],
        *[jax.ShapeDtypeStruct(c.shape, F32) for c in consts])]

    def body(*refs):
        rv = [r[...].astype(F32) for r in refs[:nr]]
        cv = [c[...] for c in refs[nr:nr + nc]]
        for o_ref, o in zip(refs[nr + nc:], fn(*rv, *cv)):
            o_ref[...] = o.astype(o_ref.dtype)

    outs = pl.pallas_call(
        body, name=name, grid=(s // tr,),
        in_specs=[pl.BlockSpec((tr, r.shape[1]), lambda i: (i, 0)) for r in rows] + [_full_spec(c) for c in consts],
        out_specs=[pl.BlockSpec((tr, w), lambda i: (i, 0)) for w in widths],
        out_shape=[jax.ShapeDtypeStruct((s, w), dt) for w, dt in zip(widths, out_dtypes)],
        compiler_params=_params(1),
    )(*rows, *consts)
    return outs[0] if len(outs) == 1 else tuple(outs)


def _rows_vjp(fn, rows, consts, cots, *, name, row_grads, adds=None, tr=256):
    adds = adds or {}
    s = rows[0].shape[0]
    tr = min(tr, s)
    nr, nc, nt = len(rows), len(consts), len(cots)
    rg = [(i, dt) for i in sorted(row_grads)
          for dt in (row_grads[i] if isinstance(row_grads[i], tuple) else (row_grads[i],))]
    ad = sorted(adds)

    def body(*refs):
        rv = [r[...].astype(F32) for r in refs[:nr]]
        cv = [c[...] for c in refs[nr:nr + nc]]
        ct = [c[...].astype(F32) for c in refs[nr + nc:nr + nc + nt]]
        av = {i: r[...].astype(F32) for i, r in zip(ad, refs[nr + nc + nt:nr + nc + nt + len(ad)])}
        out_refs = refs[nr + nc + nt + len(ad):]
        _, vjp = jax.vjp(fn, *rv, *cv)
        grads = vjp(tuple(ct))
        for o_ref, (i, _) in zip(out_refs[:len(rg)], rg):
            g = grads[i]
            if i in av:
                g = g + av[i]
            o_ref[...] = g.astype(o_ref.dtype)

        @pl.when(pl.program_id(0) == 0)
        def _():
            for o_ref in out_refs[len(rg):]:
                o_ref[...] = jnp.zeros_like(o_ref)

        for o_ref, g in zip(out_refs[len(rg):], grads[nr:]):
            o_ref[...] += g

    row_spec = lambda a: pl.BlockSpec((tr, a.shape[1]), lambda i: (i, 0))
    outs = pl.pallas_call(
        body, name=name, grid=(s // tr,),
        in_specs=[row_spec(r) for r in rows] + [_full_spec(c) for c in consts] + [row_spec(c) for c in cots]
        + [row_spec(adds[i]) for i in ad],
        out_specs=[row_spec(rows[i]) for i, _ in rg] + [_full_spec(c) for c in consts],
        out_shape=[jax.ShapeDtypeStruct(rows[i].shape, dt) for i, dt in rg]
        + [jax.ShapeDtypeStruct(c.shape, F32) for c in consts],
        compiler_params=_params(1),
    )(*rows, *consts, *cots, *[adds[i] for i in ad])
    return list(outs[:len(rg)]), list(outs[len(rg):])


def _small_call(fn, ins, *, name):
    shapes = jax.eval_shape(fn, *[jax.ShapeDtypeStruct(a.shape, F32) for a in ins])

    def body(*refs):
        for o_ref, o in zip(refs[len(ins):], fn(*[r[...] for r in refs[:len(ins)]])):
            o_ref[...] = o

    return pl.pallas_call(
        body, name=name, in_specs=[_full_spec(a) for a in ins],
        out_specs=[pl.BlockSpec(o.shape, functools.partial(lambda nd, *_: (0,) * nd, len(o.shape))) for o in shapes],
        out_shape=[jax.ShapeDtypeStruct(o.shape, F32) for o in shapes], grid=(1,),
        compiler_params=_params(1),
    )(*ins)


def _small_vjp(fn, ins, cots, *, name):
    def body(*refs):
        vals = [r[...] for r in refs[:len(ins)]]
        ct = [r[...] for r in refs[len(ins):len(ins) + len(cots)]]
        _, vjp = jax.vjp(fn, *vals)
        for o_ref, g in zip(refs[len(ins) + len(cots):], vjp(tuple(ct))):
            o_ref[...] = g

    return pl.pallas_call(
        body, name=name, in_specs=[_full_spec(a) for a in ins] + [_full_spec(c) for c in cots],
        out_specs=[_full_spec(a) for a in ins],
        out_shape=[jax.ShapeDtypeStruct(a.shape, F32) for a in ins], grid=(1,),
        compiler_params=_params(1),
    )(*ins, *cots)


def _rms(x, g):
    return x * _rsqrt(_mean(x * x, axis=-1, keepdims=True) + NORM_EPS) * g


def _rms_stage(x, g):
    return (_rms(x, g),)


def _silu(x):
    return x * _sigmoid(x)


def _softplus(x):
    return jnp.maximum(x, 0.0) + jnp.log1p(jnp.exp(-jnp.abs(x)))


def _gelu(x):
    return jax.nn.gelu(x, approximate=True)


def _gelu_stage(y):
    return (_gelu(y),)


def _glu_stage(y, gl_raw, b):
    return (_gelu(y) * jax.nn.sigmoid(gl_raw + b),)


def _ple_stage(h, gpre, pp):
    return (h + jax.nn.sigmoid(gpre) * pp,)


def _relu2_grad_epilogue(acc, up):
    return (acc * (2.0 * jnp.maximum(up, 0.0)),)


def _lb0_stage(x0, x1, x2):
    mx = jnp.maximum(jnp.maximum(x0, x1), x2)
    e0, e1, e2 = jnp.exp(x0 - mx), jnp.exp(x1 - mx), jnp.exp(x2 - mx)
    return (e0 / (e0 + e1 + e2),)


def _s5_prep_stage(a_re, a_im, log_dt, b_re, b_im, expand):
    step = jnp.exp(log_dt)
    mag = jnp.exp(a_re * step)
    lr = mag * jnp.cos(a_im * step)
    li = mag * jnp.sin(a_im * step)
    den = a_re * a_re + a_im * a_im
    cr = ((lr - 1.0) * a_re + li * a_im) / den
    ci = (li * a_re - (lr - 1.0) * a_im) / den
    cr_e, ci_e = _dot(cr, expand), _dot(ci, expand)
    return lr, li, cr_e * b_re - ci_e * b_im, cr_e * b_im + ci_e * b_re


def _loss_call(h, g, target, *, name, tr=256):
    s, d = h.shape
    tr = min(tr, s)

    def loss_fn(hv, gv, tv):
        err = _rms(hv, gv) - tv
        return 0.5 * jnp.sum(jnp.mean(err * err, axis=-1))

    def body(h_ref, g_ref, t_ref, dh_ref, dg_ref, loss_ref):
        val, (dh, dg) = jax.value_and_grad(loss_fn, argnums=(0, 1))(h_ref[...], g_ref[...], t_ref[...])
        dh_ref[...] = dh

        @pl.when(pl.program_id(0) == 0)
        def _():
            dg_ref[...] = jnp.zeros_like(dg_ref)
            loss_ref[...] = jnp.zeros_like(loss_ref)

        dg_ref[...] += dg
        loss_ref[...] += jnp.full(loss_ref.shape, val, F32)

    row = pl.BlockSpec((tr, d), lambda i: (i, 0))
    return pl.pallas_call(
        body, name=name, grid=(s // tr,),
        in_specs=[row, _full_spec(g), row],
        out_specs=[row, _full_spec(g), pl.BlockSpec((8, 128), lambda i: (0, 0))],
        out_shape=[jax.ShapeDtypeStruct((s, d), F32), jax.ShapeDtypeStruct(g.shape, F32),
                   jax.ShapeDtypeStruct((8, 128), F32)],
        compiler_params=_params(1),
    )(h, g, target)


def _hgrn_chunk(q, fp, iv, gp, lb, gn, st_t):
    c = CHUNK
    row = lax.broadcasted_iota(jnp.int32, (c, c), 0)
    col = lax.broadcasted_iota(jnp.int32, (c, c), 1)
    causal = row >= col
    fg = lb + (1.0 - lb) * _sigmoid(fp)
    k = 1.0 - fg
    lf = _log(fg)
    cum = _hdot_h(causal.astype(F32), lf, NN)
    cend = _sum(lf, axis=0, keepdims=True)
    shift = HGRN_SUB.bit_length() - 1
    ref = _hdot_h(((row >> shift) > (col >> shift)).astype(F32), lf, NN)
    q_dec = q * _exp(cum - ref)
    key_row = lax.broadcasted_iota(jnp.int32, (c, 1), 0)
    blocks = []
    for lo in range(0, c, HGRN_SUB):
        live = key_row < lo + HGRN_SUB
        k_dec = _where(live, k * _exp(_where(live, _rows(ref, lo, 1) - cum, 0.0)), 0.0)
        blocks.append(_hdot_h(_rows(q_dec, lo, HGRN_SUB), k_dec, NT))
    scores = _where(causal, _row_concat(*blocks), 0.0)
    out = _bdot_h(scores, iv, NN) + _bdot_h(q * _exp(cum), st_t, NT)
    st_new = st_t * _exp(cend) + _bdot_h(iv, k * _exp(cend - cum), TN)
    res = _rms(out, gn) * _silu(gp)
    return res, st_new


def _hgrn_heads(qs, fs, ivs, gs, lbs, gn, sts):
    res, st_new = _hgrn_chunk(_Heads(qs), _Heads(fs), _Heads(ivs), _Heads(gs), _Heads(lbs), gn, _Heads(sts))
    return res.vals, st_new.vals


def _lanes(j):
    return slice(j * HEAD, (j + 1) * HEAD)


def _hgrn_fwd(proj, lb, gn, *, heads, name, comm=()):
    s = proj.shape[0]
    n = s // CHUNK
    hpb = min(HEADS_PER_STEP, heads)
    assert heads % hpb == 0

    def body(q_ref, f_ref, i_ref, g_ref, lb_ref, gn_ref, o_ref, st_ref, state):
        @pl.when(pl.program_id(1) == 0)
        def _():
            state[...] = jnp.zeros_like(state)

        gnv = gn_ref[...]
        loaded = [(q_ref[:, _lanes(j)], f_ref[:, _lanes(j)], i_ref[:, _lanes(j)], g_ref[:, _lanes(j)],
                   lb_ref[:, _lanes(j)], state[j]) for j in range(hpb)]
        qs, fs, ivs, gs, lbs, sts = (list(t) for t in zip(*loaded))
        res, st_new = _hgrn_heads(qs, fs, ivs, gs, lbs, gnv, sts)
        for j in range(hpb):
            st_ref[j] = sts[j]
            o_ref[:, _lanes(j)] = res[j].astype(o_ref.dtype)
            state[j] = st_new[j]

    wide = hpb * HEAD
    blk = lambda off: pl.BlockSpec((CHUNK, wide), lambda h, c: (c, off // hpb + h))
    return _call(
        body, (proj, proj, proj, proj, lb, gn), name=name, grid=(heads // hpb, n),
        in_specs=[blk(0), blk(heads), blk(2 * heads), blk(3 * heads),
                  pl.BlockSpec((1, wide), lambda h, c: (0, h)), pl.BlockSpec((1, HEAD), lambda h, c: (0, 0))],
        out_specs=[pl.BlockSpec((CHUNK, wide), lambda h, c: (c, h)),
                   pl.BlockSpec((hpb, None, HEAD, HEAD), lambda h, c: (h, c, 0, 0))],
        out_shape=[jax.ShapeDtypeStruct((s, heads * HEAD), BF16), jax.ShapeDtypeStruct((heads, n, HEAD, HEAD), F32)],
        scratch_shapes=[pltpu.VMEM((hpb, HEAD, HEAD), F32)], comm=comm)


def _hgrn_bwd(proj, lb, gn, states, d_out, *, heads, name, comm=()):
    s = proj.shape[0]
    n = s // CHUNK
    hpb = min(HEADS_PER_STEP, heads)

    def body(q_ref, f_ref, i_ref, g_ref, lb_ref, gn_ref, st_ref, do_ref,
             dq_ref, df_ref, di_ref, dg_ref, dlb_ref, dgn_ref, dstate):
        h, c = pl.program_id(0), pl.program_id(1)

        @pl.when(c == 0)
        def _():
            dstate[...] = jnp.zeros_like(dstate)
            dlb_ref[...] = jnp.zeros_like(dlb_ref)

        @pl.when((c == 0) & (h == 0))
        def _():
            dgn_ref[...] = jnp.zeros_like(dgn_ref)

        gnv = gn_ref[...]
        loaded = [(q_ref[:, _lanes(j)], f_ref[:, _lanes(j)], i_ref[:, _lanes(j)], g_ref[:, _lanes(j)],
                   lb_ref[:, _lanes(j)], st_ref[j], do_ref[:, _lanes(j)].astype(F32), dstate[j]) for j in range(hpb)]
        qs, fs, ivs, gs, lbs, sts, dos, dss = (list(t) for t in zip(*loaded))
        _, vjp = jax.vjp(_hgrn_heads, qs, fs, ivs, gs, lbs, gnv, sts)
        dqs, dfs, dis, dgs, dlbs, dgn_sum, dsts = vjp((dos, dss))
        for j in range(hpb):
            ln = _lanes(j)
            dq_ref[:, ln] = dqs[j].astype(dq_ref.dtype)
            df_ref[:, ln] = dfs[j].astype(df_ref.dtype)
            di_ref[:, ln] = dis[j].astype(di_ref.dtype)
            dg_ref[:, ln] = dgs[j].astype(dg_ref.dtype)
            dlb_ref[:, ln] += dlbs[j]
            dstate[j] = dsts[j]
        dgn_ref[...] += dgn_sum

    wide = hpb * HEAD
    rev = lambda off: pl.BlockSpec((CHUNK, wide), lambda h, c: (n - 1 - c, off // hpb + h))
    out_blk = pl.BlockSpec((CHUNK, wide), lambda h, c: (n - 1 - c, h))
    width = heads * HEAD
    return _call(
        body, (proj, proj, proj, proj, lb, gn, states, d_out), name=name, grid=(heads // hpb, n),
        in_specs=[rev(0), rev(heads), rev(2 * heads), rev(3 * heads),
                  pl.BlockSpec((1, wide), lambda h, c: (0, h)), pl.BlockSpec((1, HEAD), lambda h, c: (0, 0)),
                  pl.BlockSpec((hpb, None, HEAD, HEAD), lambda h, c: (h, n - 1 - c, 0, 0)), out_blk],
        out_specs=[out_blk, out_blk, out_blk, out_blk,
                   pl.BlockSpec((1, wide), lambda h, c: (0, h)), pl.BlockSpec((1, HEAD), lambda h, c: (0, 0))],
        out_shape=[jax.ShapeDtypeStruct((s, width), BF16)] * 4
        + [jax.ShapeDtypeStruct((1, width), F32), jax.ShapeDtypeStruct((1, HEAD), F32)],
        scratch_shapes=[pltpu.VMEM((hpb, HEAD, HEAD), F32)], comm=comm)


def _shift_rows(x, d, rowi):
    if d == 0:
        return x
    n = x.shape[0]
    rolled = pltpu.roll(x, d % n, 0)
    keep = rowi >= d if d > 0 else rowi < n + d
    return jnp.where(keep, rolled, 0.0)


def _conv_pre(x, w_ref, rowi):
    acc = None
    for j in range(CONV_WIDTH):
        term = w_ref[j:j + 1, :] * _shift_rows(x, CONV_WIDTH - 1 - j, rowi)
        acc = term if acc is None else acc + term
    return acc


def _conv_fwd(proj, w, *, col_off, name, cb=256):
    s = proj.shape[0]
    width = w.shape[1]
    cb = min(cb, width)

    def body(x_ref, w_ref, o_ref):
        rowi = lax.broadcasted_iota(jnp.int32, (s, cb), 0)
        o_ref[...] = _silu(_conv_pre(x_ref[...], w_ref, rowi))

    return pl.pallas_call(
        body, name=name, grid=(width // cb,),
        in_specs=[pl.BlockSpec((s, cb), lambda j: (0, col_off // cb + j)), pl.BlockSpec((CONV_WIDTH, cb), lambda j: (0, j))],
        out_specs=pl.BlockSpec((s, cb), lambda j: (0, j)),
        out_shape=jax.ShapeDtypeStruct((s, width), F32),
        compiler_params=_params(1),
    )(proj, w)


def _conv_bwd(proj, w, d_out, *, col_off, name, cb=256, comm=()):
    s = proj.shape[0]
    width = w.shape[1]
    cb = min(cb, width)

    def body(x_ref, w_ref, do_ref, dx_ref, dw_ref):
        rowi = lax.broadcasted_iota(jnp.int32, (s, cb), 0)
        x = x_ref[...]
        pre = _conv_pre(x, w_ref, rowi)
        sg = jax.nn.sigmoid(pre)
        dpre = do_ref[...] * (sg + pre * sg * (1.0 - sg))
        dx = None
        for j in range(CONV_WIDTH):
            d = CONV_WIDTH - 1 - j
            term = w_ref[j:j + 1, :] * _shift_rows(dpre, -d, rowi)
            dx = term if dx is None else dx + term
            dw_ref[j:j + 1, :] = jnp.sum(dpre * _shift_rows(x, d, rowi), axis=0, keepdims=True)
        dx_ref[...] = dx.astype(dx_ref.dtype)

    return _call(
        body, (proj, w, d_out), name=name, grid=(width // cb,),
        in_specs=[pl.BlockSpec((s, cb), lambda j: (0, col_off // cb + j)), pl.BlockSpec((CONV_WIDTH, cb), lambda j: (0, j)),
                  pl.BlockSpec((s, cb), lambda j: (0, j))],
        out_specs=[pl.BlockSpec((s, cb), lambda j: (0, j)), pl.BlockSpec((CONV_WIDTH, cb), lambda j: (0, j))],
        out_shape=[jax.ShapeDtypeStruct((s, width), BF16), jax.ShapeDtypeStruct((CONV_WIDTH, width), F32)],
        comm=comm)


_lane_concat = _per_head(lambda a, b: jnp.concatenate([a, b], axis=1))
_lane_half = _per_head(lambda a, j: a[:, j * HEAD:(j + 1) * HEAD])


def _tri_inverse(lower):
    c = CHUNK
    row = lax.broadcasted_iota(jnp.int32, (c, c), 0)
    col = lax.broadcasted_iota(jnp.int32, (c, c), 1)
    inv = (row == col).astype(F32)
    lvl = 0
    while (1 << lvl) < c:
        same_pair = (row >> (lvl + 1)) == (col >> (lvl + 1))
        off_block = same_pair & (((row >> lvl) & 1) == 1) & (((col >> lvl) & 1) == 0)
        inv = inv - _hdot_h(_hdot_h(inv, _where(off_block, lower, 0.0), NN), inv, NN)
        lvl += 1
    return inv


@jax.custom_vjp
def _tri_solve(lowers, rhss):
    return _tri_solve_fwd(lowers, rhss)[0]


def _tri_solve_fwd(lowers, rhss):
    inv = _tri_inverse(_Heads(lowers))
    sol = _hdot_h(inv, _Heads(rhss), NN)
    return sol.vals, (inv.vals, sol.vals)


def _tri_solve_bwd(res, g):
    inv, sol = _Heads(res[0]), _Heads(res[1])
    d_rhs = _hdot_h(inv, _Heads(g), TN)
    return (-_hdot_h(d_rhs, sol, NT)).vals, d_rhs.vals


_tri_solve.defvjp(_tri_solve_fwd, _tri_solve_bwd)


def _solve(lower, rhs):
    if isinstance(lower, _Heads):
        return _Heads(_tri_solve(lower.vals, rhs.vals))
    return _tri_solve([lower], [rhs])[0]


def _delta_chunk(h, heads, qr, kr, vr, ab, zp, alog, dtb, gn, st):
    c = CHUNK
    row = lax.broadcasted_iota(jnp.int32, (c, c), 0)
    col = lax.broadcasted_iota(jnp.int32, (c, c), 1)
    causal = row >= col
    strict = row > col
    lane = lax.broadcasted_iota(jnp.int32, (c, HEAD), 1)
    mine = _equal(h, lane)
    la_full = -jnp.exp(alog) * _softplus(ab + dtb)
    cum_full = _hdot(causal.astype(F32), la_full)
    cum = _sum(_where(mine, cum_full, 0.0), axis=1, keepdims=True)
    cend = _sum(_sum(_where(mine, la_full, 0.0), axis=1, keepdims=True), axis=0, keepdims=True)
    beta = _sum(_where(_equal(heads + h, lane), jax.nn.sigmoid(ab), 0.0), axis=1, keepdims=True)
    cum_row = _hdot_h(_where(mine, 1.0, 0.0), cum_full, NT)
    decay = _where(causal, _exp(_where(causal, cum - cum_row, 0.0)), 0.0)
    qn = qr * _rsqrt(_sum(qr * qr, axis=-1, keepdims=True) + NORM_EPS) * (HEAD ** -0.5)
    kn = kr * _rsqrt(_sum(kr * kr, axis=-1, keepdims=True) + NORM_EPS)
    kb = kn * beta
    lower = _where(strict, _bdot_h(kb, kn, NT) * decay, 0.0)
    ecum = _exp(cum)
    sol = _solve(lower, _lane_concat(vr * beta, kb * ecum))
    u, w = _lane_half(sol, 0), _lane_half(sol, 1)
    intra = _bdot_h(qn, kn, NT) * decay
    v_new = u - _bdot_h(w, st, NN)
    out = _bdot_h(qn * ecum, st, NN) + _bdot_h(intra, v_new, NN)
    st_new = st * _exp(cend) + _bdot_h(kn * _exp(cend - cum), v_new, TN)
    res = _rms(out, gn) * _silu(zp)
    return res, st_new


def _delta_heads(hs, heads, qs, ks, vs, ab, zs, alog, dtb, gn, sts):
    res, st_new = _delta_chunk(_Heads(hs), heads, _Heads(qs), _Heads(ks), _Heads(vs), ab, _Heads(zs), alog, dtb, gn,
                               _Heads(sts))
    return res.vals, st_new.vals


def _delta_fwd(qkv, ab, proj, hp, gn, *, heads, z_off, name, comm=()):
    s = qkv.shape[0]
    n = s // CHUNK

    hpb = min(HEADS_PER_STEP, heads)
    assert heads % hpb == 0 and z_off % hpb == 0

    def body(q_ref, k_ref, v_ref, ab_ref, z_ref, hp_ref, gn_ref, o_ref, st_ref, state):
        hb = pl.program_id(1)

        @pl.when(pl.program_id(0) == 0)
        def _():
            for j in range(hpb):
                state[hb * hpb + j] = jnp.zeros((HEAD, HEAD), F32)

        shared = (ab_ref[...], hp_ref[0:1, :], hp_ref[1:2, :], gn_ref[...])
        loaded = [(q_ref[:, _lanes(j)], k_ref[:, _lanes(j)], v_ref[:, _lanes(j)], z_ref[:, _lanes(j)],
                   state[hb * hpb + j]) for j in range(hpb)]
        qs, ks, vs, zs, sts = (list(t) for t in zip(*loaded))
        res, st_new = _delta_heads([hb * hpb + j for j in range(hpb)], heads, qs, ks, vs, shared[0], zs, shared[1],
                                   shared[2], shared[3], sts)
        for j in range(hpb):
            st_ref[j] = sts[j]
            o_ref[:, _lanes(j)] = res[j].astype(o_ref.dtype)
            state[hb * hpb + j] = st_new[j]

    wide = hpb * HEAD
    blk = lambda off: pl.BlockSpec((CHUNK, wide), lambda c, h: (c, off // hpb + h))
    return _call(
        body, (qkv, qkv, qkv, ab, proj, hp, gn), name=name, grid=(n, heads // hpb),
        in_specs=[blk(0), blk(heads), blk(2 * heads), pl.BlockSpec((CHUNK, HEAD), lambda c, h: (c, 0)), blk(z_off),
                  pl.BlockSpec((8, HEAD), lambda c, h: (0, 0)), pl.BlockSpec((1, HEAD), lambda c, h: (0, 0))],
        out_specs=[pl.BlockSpec((CHUNK, wide), lambda c, h: (c, h)),
                   pl.BlockSpec((hpb, None, HEAD, HEAD), lambda c, h: (h, c, 0, 0))],
        out_shape=[jax.ShapeDtypeStruct((s, heads * HEAD), BF16), jax.ShapeDtypeStruct((heads, n, HEAD, HEAD), F32)],
        scratch_shapes=[pltpu.VMEM((heads, HEAD, HEAD), F32)], comm=comm)


def _delta_bwd(qkv, ab, proj, hp, gn, states, d_out, *, heads, z_off, name, comm=()):
    s = qkv.shape[0]
    n = s // CHUNK
    hpb = min(HEADS_PER_STEP, heads)

    def body(q_ref, k_ref, v_ref, ab_ref, z_ref, hp_ref, gn_ref, st_ref, do_ref,
             dq_ref, dk_ref, dv_ref, dab_ref, dz_ref, dhp_ref, dgn_ref, dstate):
        c, hb = pl.program_id(0), pl.program_id(1)

        @pl.when(c == 0)
        def _():
            for j in range(hpb):
                dstate[hb * hpb + j] = jnp.zeros((HEAD, HEAD), F32)

        @pl.when((c == 0) & (hb == 0))
        def _():
            dgn_ref[...] = jnp.zeros_like(dgn_ref)
            dhp_ref[...] = jnp.zeros_like(dhp_ref)

        @pl.when(hb == 0)
        def _():
            dab_ref[...] = jnp.zeros_like(dab_ref)

        shared = (ab_ref[...], hp_ref[0:1, :], hp_ref[1:2, :], gn_ref[...])
        loaded = [(q_ref[:, _lanes(j)], k_ref[:, _lanes(j)], v_ref[:, _lanes(j)], z_ref[:, _lanes(j)], st_ref[j],
                   do_ref[:, _lanes(j)].astype(F32), dstate[hb * hpb + j]) for j in range(hpb)]
        qs, ks, vs, zs, sts, dos, dss = (list(t) for t in zip(*loaded))
        fn = functools.partial(_delta_heads, [hb * hpb + j for j in range(hpb)], heads)
        _, vjp = jax.vjp(fn, qs, ks, vs, shared[0], zs, shared[1], shared[2], shared[3], sts)
        dqs, dks, dvs, dab, dzs, dal, ddt, dgn, dsts = vjp((dos, dss))
        for j in range(hpb):
            ln = _lanes(j)
            dq_ref[:, ln] = dqs[j]
            dk_ref[:, ln] = dks[j]
            dv_ref[:, ln] = dvs[j]
            dz_ref[:, ln] = dzs[j].astype(dz_ref.dtype)
            dstate[hb * hpb + j] = dsts[j]
        dab_ref[...] += dab
        dhp_ref[0:1, :] += dal
        dhp_ref[1:2, :] += ddt
        dgn_ref[...] += dgn

    wide = hpb * HEAD
    rev = lambda off: pl.BlockSpec((CHUNK, wide), lambda c, h: (n - 1 - c, off // hpb + h))
    width = heads * HEAD
    head_blk = pl.BlockSpec((CHUNK, wide), lambda c, h: (n - 1 - c, h))
    ab_blk = pl.BlockSpec((CHUNK, HEAD), lambda c, h: (n - 1 - c, 0))
    return _call(
        body, (qkv, qkv, qkv, ab, proj, hp, gn, states, d_out), name=name, grid=(n, heads // hpb),
        in_specs=[rev(0), rev(heads), rev(2 * heads), ab_blk, rev(z_off),
                  pl.BlockSpec((8, HEAD), lambda c, h: (0, 0)), pl.BlockSpec((1, HEAD), lambda c, h: (0, 0)),
                  pl.BlockSpec((hpb, None, HEAD, HEAD), lambda c, h: (h, n - 1 - c, 0, 0)), head_blk],
        out_specs=[head_blk, head_blk, head_blk, ab_blk, head_blk,
                   pl.BlockSpec((8, HEAD), lambda c, h: (0, 0)), pl.BlockSpec((1, HEAD), lambda c, h: (0, 0))],
        out_shape=[jax.ShapeDtypeStruct((s, width), F32)] * 3
        + [jax.ShapeDtypeStruct((s, HEAD), F32), jax.ShapeDtypeStruct((s, width), BF16),
           jax.ShapeDtypeStruct((8, HEAD), F32), jax.ShapeDtypeStruct((1, HEAD), F32)],
        scratch_shapes=[pltpu.VMEM((heads, HEAD, HEAD), F32)], comm=comm)


def _s5_scan(buf, lt_ref, cin_r, cin_i, tt, reverse):
    nblk = tt // 8
    hl = S5_HALF
    base = 8 if reverse else 0

    def body(j, carry):
        cr, ci = carry
        off = pl.multiple_of((nblk - 1 - j if reverse else j) * 8, 8)
        xr = buf[pl.ds(off, 8), 0:hl]
        xi = buf[pl.ds(off, 8), hl:2 * hl]
        for lv, d in enumerate((1, 2, 4)):
            ar, ai = lt_ref[base + 2 * lv], lt_ref[base + 2 * lv + 1]
            sr = pltpu.roll(xr, 8 - d if reverse else d, 0)
            si = pltpu.roll(xi, 8 - d if reverse else d, 0)
            xr, xi = xr + ar * sr - ai * si, xi + ar * si + ai * sr
        pr, pi = lt_ref[base + 6], lt_ref[base + 7]
        xr, xi = xr + pr * cr - pi * ci, xi + pr * ci + pi * cr
        buf[pl.ds(off, 8), 0:hl] = xr
        buf[pl.ds(off, 8), hl:2 * hl] = xi
        edge = 0 if reverse else 7
        return xr[edge:edge + 1, :], xi[edge:edge + 1, :]

    return lax.fori_loop(0, nblk, body, (cin_r, cin_i))


def _s5_fwd(u, wb, wc, lt, dskip, *, name, tt=1024, comm=()):
    s, d = u.shape
    nb = d // HEAD
    tt = min(tt, s)
    nt = s // tt
    hl = S5_HALF

    def body(u_ref, wb_ref, wc_ref, lt_ref, d_ref, y_ref, cin_ref, st_ref, buf, carry):
        @pl.when(pl.program_id(1) == 0)
        def _():
            carry[...] = jnp.zeros_like(carry)

        cin_ref[...] = carry[0:1, :]
        uv = u_ref[...]
        buf[...] = _bdot_raw(uv, wb_ref[...])
        cr, ci = _s5_scan(buf, lt_ref, carry[0:1, 0:hl], carry[0:1, hl:2 * hl], tt, False)
        carry[0:1, 0:hl] = cr
        carry[0:1, hl:2 * hl] = ci
        states = buf[...].astype(BF16)
        st_ref[...] = states
        y_ref[...] = _bdot_raw(states, wc_ref[...]) + d_ref[...] * uv

    return _call(
        body, (u, wb, wc, lt, dskip), name=name, grid=(nb, nt),
        in_specs=[pl.BlockSpec((tt, HEAD), lambda b, t: (t, b)),
                  pl.BlockSpec((None, HEAD, 2 * hl), lambda b, t: (b, 0, 0)),
                  pl.BlockSpec((None, 2 * hl, HEAD), lambda b, t: (b, 0, 0)),
                  pl.BlockSpec((None, 16, 8, hl), lambda b, t: (b, 0, 0, 0)),
                  pl.BlockSpec((1, HEAD), lambda b, t: (0, b))],
        out_specs=[pl.BlockSpec((tt, HEAD), lambda b, t: (t, b)),
                   pl.BlockSpec((None, None, 1, 2 * hl), lambda b, t: (b, t, 0, 0)),
                   pl.BlockSpec((tt, 2 * hl), lambda b, t: (t, b))],
        out_shape=[jax.ShapeDtypeStruct((s, d), F32), jax.ShapeDtypeStruct((nb, nt, 1, 2 * hl), F32),
                   jax.ShapeDtypeStruct((s, nb * 2 * hl), BF16)],
        scratch_shapes=[pltpu.VMEM((tt, 2 * hl), F32), pltpu.VMEM((8, 2 * hl), F32)], comm=comm)


def _s5_bwd(u, dy, wb, wc, lt, dskip, cins, states, *, name, tt=1024, comm=()):
    s, d = u.shape
    nb = d // HEAD
    tt = min(tt, s)
    nt = s // tt
    hl = S5_HALF

    def body(u_ref, dy_ref, wb_ref, wc_ref, lt_ref, d_ref, cin_ref, st_ref,
             du_ref, dwb_ref, dwc_ref, dd_ref, dlam_ref, abuf, acarry):
        @pl.when(pl.program_id(1) == 0)
        def _():
            acarry[...] = jnp.zeros_like(acarry)
            dwb_ref[...] = jnp.zeros_like(dwb_ref)
            dwc_ref[...] = jnp.zeros_like(dwc_ref)
            dd_ref[...] = jnp.zeros_like(dd_ref)
            dlam_ref[...] = jnp.zeros_like(dlam_ref)

        uv, dyv = u_ref[...], dy_ref[...]
        dy16 = dyv.astype(BF16)
        abuf[...] = _bdot_raw(dy16, wc_ref[...], NT)
        ar, ai = _s5_scan(abuf, lt_ref, acarry[0:1, 0:hl], acarry[0:1, hl:2 * hl], tt, True)
        acarry[0:1, 0:hl] = ar
        acarry[0:1, hl:2 * hl] = ai
        adj16 = abuf[...].astype(BF16)
        du_ref[...] = _bdot_raw(adj16, wb_ref[...], NT) + d_ref[...] * dyv
        dwb_ref[...] += _bdot_raw(uv, adj16, TN)
        dwc_ref[...] += _bdot_raw(st_ref[...], dy16, TN)
        dd_ref[...] += jnp.sum(dyv * uv, axis=0, keepdims=True)
        first = lax.broadcasted_iota(jnp.int32, (tt, hl), 0) == 0
        spr = jnp.where(first, cin_ref[:, 0:hl], pltpu.roll(st_ref[:, 0:hl].astype(F32), 1, 0))
        spi = jnp.where(first, cin_ref[:, hl:2 * hl], pltpu.roll(st_ref[:, hl:2 * hl].astype(F32), 1, 0))
        avr, avi = abuf[:, 0:hl], abuf[:, hl:2 * hl]
        dlam_ref[:, 0:hl] += jnp.sum(avr * spr + avi * spi, axis=0, keepdims=True)
        dlam_ref[:, hl:2 * hl] += jnp.sum(avi * spr - avr * spi, axis=0, keepdims=True)

    rev = pl.BlockSpec((tt, HEAD), lambda b, t: (nt - 1 - t, b))
    return _call(
        body, (u, dy, wb, wc, lt, dskip, cins, states), name=name, grid=(nb, nt),
        in_specs=[rev, rev,
                  pl.BlockSpec((None, HEAD, 2 * hl), lambda b, t: (b, 0, 0)),
                  pl.BlockSpec((None, 2 * hl, HEAD), lambda b, t: (b, 0, 0)),
                  pl.BlockSpec((None, 16, 8, hl), lambda b, t: (b, 0, 0, 0)),
                  pl.BlockSpec((1, HEAD), lambda b, t: (0, b)),
                  pl.BlockSpec((None, None, 1, 2 * hl), lambda b, t: (b, nt - 1 - t, 0, 0)),
                  pl.BlockSpec((tt, 2 * hl), lambda b, t: (nt - 1 - t, b))],
        out_specs=[rev,
                   pl.BlockSpec((None, HEAD, 2 * hl), lambda b, t: (b, 0, 0)),
                   pl.BlockSpec((None, 2 * hl, HEAD), lambda b, t: (b, 0, 0)),
                   pl.BlockSpec((1, HEAD), lambda b, t: (0, b)),
                   pl.BlockSpec((None, 1, 2 * hl), lambda b, t: (b, 0, 0))],
        out_shape=[jax.ShapeDtypeStruct((s, d), F32), jax.ShapeDtypeStruct(wb.shape, F32),
                   jax.ShapeDtypeStruct(wc.shape, F32), jax.ShapeDtypeStruct((1, d), F32),
                   jax.ShapeDtypeStruct((nb, 1, 2 * hl), F32)],
        scratch_shapes=[pltpu.VMEM((tt, 2 * hl), F32), pltpu.VMEM((8, 2 * hl), F32)],
        comm=comm)


def _s5_pack(lr, li, br, bi, c_re, c_im):
    g = lr.shape[0]
    nb = g // S5_GB
    eye = jnp.eye(S5_GB, dtype=F32)
    bm = jnp.stack([br, bi]).reshape(2, nb, S5_GB, S5_STATE, S5_GROUP)
    wb = jnp.einsum("rbgpc,gh->bgcrhp", bm, eye).reshape(nb, HEAD, 2 * S5_HALF)
    cm = jnp.stack([c_re, -c_im]).reshape(2, nb, S5_GB, S5_GROUP, S5_STATE)
    wc = jnp.einsum("rbgcp,gh->brgphc", cm, eye).reshape(nb, 2 * S5_HALF, HEAD)
    pw = [(lr, li)]
    for _ in range(7):
        pr, pi = pw[-1]
        pw.append((pr * lr - pi * li, pr * li + pi * lr))
    blk = lambda a: a.reshape(nb, 1, S5_HALF)
    rows = jnp.arange(8).reshape(1, 8, 1)
    tables = []
    for conj, keep, order in ((1.0, lambda n: rows >= n, range(8)), (-1.0, lambda n: rows < 8 - n, range(7, -1, -1))):
        for n in (1, 2, 4):
            tables += [jnp.where(keep(n), blk(pw[n - 1][0]), 0.0), jnp.where(keep(n), conj * blk(pw[n - 1][1]), 0.0)]
        tables += [jnp.concatenate([blk(pw[n][0]) for n in order], axis=1),
                   jnp.concatenate([conj * blk(pw[n][1]) for n in order], axis=1)]
    return wb, wc, jnp.stack(tables, axis=1)


def _s5_unpack(dwb, dwc, dlam):
    nb = dwb.shape[0]
    g = nb * S5_GB
    eye = jnp.eye(S5_GB, dtype=F32)
    db = jnp.einsum("bgcrhp,gh->rbgpc", dwb.reshape(nb, S5_GB, S5_GROUP, 2, S5_GB, S5_STATE), eye)
    db = db.reshape(2, g, S5_STATE * S5_GROUP)
    dc = jnp.einsum("brgphc,gh->rbgcp", dwc.reshape(nb, 2, S5_GB, S5_STATE, S5_GB, S5_GROUP), eye)
    dc = dc.reshape(2, g, S5_GROUP, S5_STATE)
    dl = dlam.reshape(nb, 2, S5_GB, S5_STATE).transpose(1, 0, 2, 3).reshape(2, g, S5_STATE)
    return dl[0], dl[1], db[0], db[1], dc[0], -dc[1]


def _peer(r):
    mx, my, mc = lax.axis_index("x"), lax.axis_index("y"), lax.axis_index("c")
    px = 1 - mx if r & 4 else mx
    py = 1 - my if r & 2 else my
    pc = 1 - mc if r & 1 else mc
    return (px, py, pc), 4 * px + 2 * py + pc


_COMM_SCRATCH = [pltpu.SemaphoreType.DMA((N_DEV - 1,)), pltpu.SemaphoreType.DMA((N_DEV - 1,)), pltpu.SemaphoreType.DMA]


class _AllToAll:
    def __init__(self, x, rows=None):
        self.x = x
        self.rows = rows
        shape = x.shape if rows is None else (x.shape[0], rows[1]) + tuple(x.shape[2:])
        self.out_shape = jax.ShapeDtypeStruct(shape, x.dtype)

    def _copies(self, x_ref, out_ref, send_sems, recv_sems, local_sem):
        def block(j):
            return x_ref.at[j] if self.rows is None else x_ref.at[j, pl.ds(self.rows[0], self.rows[1])]

        _, me = _peer(0)
        mine = pltpu.make_async_copy(block(me), out_ref.at[me], local_sem)
        sends, recvs = [], []
        for r in range(1, N_DEV):
            pos, idx = _peer(r)
            sems = dict(send_sem=send_sems.at[r - 1], recv_sem=recv_sems.at[r - 1], device_id=pos, device_id_type=MESH)
            sends.append(pltpu.make_async_remote_copy(src_ref=block(idx), dst_ref=out_ref.at[me], **sems))
            recvs.append(pltpu.make_async_remote_copy(src_ref=block(idx), dst_ref=out_ref.at[idx], **sems))
        return mine, sends, recvs

    def start(self, *refs):
        mine, sends, _ = self._copies(*refs)
        mine.start()
        for cp in sends:
            cp.start()

    def finish(self, *refs):
        mine, sends, recvs = self._copies(*refs)
        for cp in recvs:
            cp.wait_recv()
        for cp in sends:
            cp.wait_send()
        mine.wait()


class _Gather:
    def __init__(self, x):
        self.x = x
        self.out_shape = jax.ShapeDtypeStruct((N_DEV,) + tuple(x.shape), x.dtype)

    def _copies(self, x_ref, out_ref, send_sems, recv_sems, local_sem):
        mx, my, mc = lax.axis_index("x"), lax.axis_index("y"), lax.axis_index("c")
        me, sibling = (mx, my, mc), (mx, my, 1 - mc)
        chips = [(1 - mx, my), (mx, 1 - my), (1 - mx, 1 - my)]

        def slot(px, py, pc):
            return out_ref.at[4 * px + 2 * py + pc]

        def copy(k, block, to, src=None):
            return pltpu.make_async_remote_copy(
                src_ref=slot(*block) if src is None else src, dst_ref=slot(*block),
                send_sem=send_sems.at[k], recv_sem=recv_sems.at[k], device_id=to, device_id_type=MESH)

        return dict(
            mine=pltpu.make_async_copy(x_ref, slot(*me), local_sem),
            first=[copy(0, me, sibling, src=x_ref)] + [copy(1 + j, me, (*chip, mc), src=x_ref) for j, chip in enumerate(chips)],
            passed=[copy(4 + j, (*chip, mc), sibling) for j, chip in enumerate(chips)],
            over_ici=[copy(1 + j, (*chip, mc), me) for j, chip in enumerate(chips)],
            from_sibling=[copy(0, sibling, me)] + [copy(4 + j, (*chip, 1 - mc), me) for j, chip in enumerate(chips)])

    def start(self, *refs):
        cps = self._copies(*refs)
        cps["mine"].start()
        for cp in cps["first"]:
            cp.start()

    def finish(self, *refs):
        cps = self._copies(*refs)
        for arrived, onward in zip(cps["over_ici"], cps["passed"]):
            arrived.wait_recv()
            onward.start()
        for cp in cps["from_sibling"]:
            cp.wait_recv()
        for cp in cps["first"] + cps["passed"]:
            cp.wait_send()
        cps["mine"].wait()


def _call(body, args, *, name, grid, in_specs, out_specs, out_shape, scratch_shapes=(), comm=()):
    n_in, n_out, n_scr, nc = len(in_specs), len(out_shape), len(scratch_shapes), len(comm)

    def wrapped(*refs):
        ins, c_in = refs[:n_in], refs[n_in:n_in + nc]
        outs, c_out = refs[n_in + nc:n_in + nc + n_out], refs[n_in + nc + n_out:n_in + 2 * nc + n_out]
        scr = refs[n_in + 2 * nc + n_out:n_in + 2 * nc + n_out + n_scr]
        sems = refs[n_in + 2 * nc + n_out + n_scr:]
        ids = [pl.program_id(a) for a in range(len(grid))]
        if nc:
            @pl.when(functools.reduce(operator.and_, [i == 0 for i in ids]))
            def _():
                for k, op in enumerate(comm):
                    op.start(c_in[k], c_out[k], *sems[3 * k:3 * k + 3])

        body(*ins, *outs, *scr)
        if nc:
            @pl.when(functools.reduce(operator.and_, [i == g - 1 for i, g in zip(ids, grid)]))
            def _():
                for k, op in enumerate(comm):
                    op.finish(c_in[k], c_out[k], *sems[3 * k:3 * k + 3])

    any_spec = pl.BlockSpec(memory_space=pl.ANY)
    res = pl.pallas_call(
        wrapped, name=name, grid=grid,
        in_specs=list(in_specs) + [any_spec] * nc, out_specs=list(out_specs) + [any_spec] * nc,
        out_shape=list(out_shape) + [op.out_shape for op in comm],
        scratch_shapes=list(scratch_shapes) + list(_COMM_SCRATCH) * nc,
        compiler_params=_params(len(grid)),
    )(*args, *[op.x for op in comm])
    return list(res[:n_out]), list(res[n_out:])


def _comm_call(op, *, name):
    return _call(lambda: None, (), name=name, grid=(1,), in_specs=[], out_specs=[], out_shape=[], comm=(op,))[1][0]


def _adamw(w, parts, m, v, *, name, tr=128, comm=()):
    nl, r, c = w.shape
    assert len(parts) == nl
    parts = [list(p) if isinstance(p, (list, tuple)) else [p] for p in parts]
    npart = parts[0][0].shape[0]
    tr = min(tr, r, *[pc.shape[1] for p in parts for pc in p])
    assert r % tr == 0 and all(pc.shape[1] % tr == 0 for p in parts for pc in p), (name, r, tr)
    pieces = []
    for l, p in enumerate(parts):
        first = 0
        for pc in p:
            pieces.append((l, first, pc.shape[1] // tr, pc))
            first += pc.shape[1] // tr
        assert first == r // tr, (name, l)

    def body(w_ref, m_ref, v_ref, *rest):
        p_refs, (g_ref, d_ref, mo_ref, vo_ref) = rest[:len(pieces)], rest[len(pieces):]
        layer, tile = pl.program_id(0), pl.program_id(1)
        for p_ref, (l, first, count, _) in zip(p_refs, pieces):
            @pl.when((layer == l) & (tile >= first) & (tile < first + count))
            def _():
                g = p_ref[0].astype(F32)
                for k in range(1, npart):
                    g = g + p_ref[k].astype(F32)
                m2 = ADAM_B1 * m_ref[...] + (1.0 - ADAM_B1) * g
                v2 = ADAM_B2 * v_ref[...] + (1.0 - ADAM_B2) * (g * g)
                m_hat = m2 / (1.0 - ADAM_B1 ** ADAM_STEP)
                v_hat = v2 / (1.0 - ADAM_B2 ** ADAM_STEP)
                g_ref[...] = g
                d_ref[...] = -ADAM_LR * (m_hat / (jnp.sqrt(v_hat) + ADAM_EPS) + ADAM_WD * w_ref[...])
                mo_ref[...] = m2
                vo_ref[...] = v2

    blk = pl.BlockSpec((None, tr, c), lambda l, i: (l, i, 0))

    def part_spec(l, first, count):
        return pl.BlockSpec((npart, tr, c), lambda ll, i: (0, jnp.where(ll == l, jnp.clip(i - first, 0, count - 1), 0), 0))

    outs, exchanged = _call(
        body, (w, m, v, *[pc for _, _, _, pc in pieces]), name=name, grid=(nl, r // tr),
        in_specs=[blk, blk, blk] + [part_spec(l, first, count) for l, first, count, _ in pieces],
        out_specs=[blk] * 4, out_shape=[jax.ShapeDtypeStruct((nl, r, c), F32)] * 4, comm=comm)
    return (outs, exchanged) if comm else outs


def _sum_parts(parts, *, name):
    npart = parts.shape[0]

    def body(p_ref, o_ref):
        g = p_ref[0]
        for k in range(1, npart):
            g = g + p_ref[k]
        o_ref[...] = g

    return pl.pallas_call(
        body, name=name, grid=(1,), in_specs=[_full_spec(parts)],
        out_specs=pl.BlockSpec(parts.shape[1:], lambda i: (0, 0)),
        out_shape=jax.ShapeDtypeStruct(parts.shape[1:], F32), compiler_params=_params(1),
    )(parts)


def _pack(arrs):
    blocks = []
    for a in arrs:
        flat = a.reshape(-1).astype(F32)
        blocks.append(jnp.pad(flat, (0, (-flat.shape[0]) % (8 * HEAD))).reshape(-1, HEAD))
    out = jnp.concatenate(blocks, axis=0)
    return jnp.pad(out, ((0, (-out.shape[0]) % HEAD), (0, 0)))


def _unpack(packed, shapes):
    out, off = [], 0
    for shp in shapes:
        size = math.prod(shp)
        rows = -(-size // (8 * HEAD)) * 8
        out.append(packed[off:off + rows].reshape(-1)[:size].reshape(shp))
        off += rows
    return out


def _add_epilogue(acc, res):
    return (acc + res,)


def _relu2_epilogue(acc):
    r = jnp.maximum(acc, 0.0)
    return acc, r * r


def _ple_epilogue(acc, gpre, h):
    return h + jax.nn.sigmoid(gpre) * acc, acc


def kernel(x, p, norm_mix, norm_mlp, norm_ple, w_in_e, w_out_e, hgrn_lb, g_norm_a, conv_w, a_log, dt_bias, g_norm_b, s5_a_re, s5_a_im, s5_b_re, s5_b_im, s5_c_re, s5_c_im, s5_d, s5_log_dt, w_glu, b_glu, w_out_o, w_up, w_down, w_ple_gate, w_ple_proj, final_norm, loss_target, m_norm_mix, m_norm_mlp, m_norm_ple, m_w_in_e, m_w_out_e, m_hgrn_lb, m_g_norm_a, m_conv_w, m_a_log, m_dt_bias, m_g_norm_b, m_s5_a_re, m_s5_a_im, m_s5_b_re, m_s5_b_im, m_s5_c_re, m_s5_c_im, m_s5_d, m_s5_log_dt, m_w_glu, m_b_glu, m_w_out_o, m_w_up, m_w_down, m_w_ple_gate, m_w_ple_proj, m_final_norm, v_norm_mix, v_norm_mlp, v_norm_ple, v_w_in_e, v_w_out_e, v_hgrn_lb, v_g_norm_a, v_conv_w, v_a_log, v_dt_bias, v_g_norm_b, v_s5_a_re, v_s5_a_im, v_s5_b_re, v_s5_b_im, v_s5_c_re, v_s5_c_im, v_s5_d, v_s5_log_dt, v_w_glu, v_b_glu, v_w_out_o, v_w_up, v_w_down, v_w_ple_gate, v_w_ple_proj, v_final_norm):
    args = dict(locals())
    s, d = x.shape[1], x.shape[2]
    aw = d // 2
    ha = hb = aw // HEAD
    main = 4 * d
    z_col = 2 * d + 3 * aw
    ff = w_up.shape[2] * N_DEV
    ple = p.shape[-1]
    groups = d // S5_GROUP
    me = 4 * lax.axis_index("x") + 2 * lax.axis_index("y") + lax.axis_index("c")
    x2, target = x[0], loss_target[0]
    row = lambda a, i: a[i:i + 1]

    def gather_of(w):
        return _Gather(w.astype(BF16))

    w_in = jnp.transpose(_comm_call(gather_of(w_in_e[0]), name="ag_w_in"), (1, 0, 2)).reshape(d, -1)
    w_main = w_in[:, :main]
    w_tail = jnp.pad(w_in[:, main:], ((0, 0), (0, HEAD - 2 * hb)))

    lb_rows = [row(hgrn_lb, 0), row(hgrn_lb, 1), row(hgrn_lb, 2)]
    (lb0,) = _small_call(_lb0_stage, lb_rows, name="f_lb0")
    hp = jnp.zeros((8, HEAD), F32).at[0, :hb].set(a_log[0]).at[1, :hb].set(dt_bias[0])
    expand = jnp.asarray(np.kron(np.eye(S5_STATE, dtype=np.float32), np.ones((1, S5_GROUP), np.float32)))
    prep_in = [s5_a_re[0], s5_a_im[0], s5_log_dt[0].reshape(groups, 1),
               s5_b_re[0].reshape(groups, -1), s5_b_im[0].reshape(groups, -1), expand]
    lr, li, br, bi = _small_call(_s5_prep_stage, prep_in, name="f_s5_prep")
    wb, wc, lt = _s5_pack(lr, li, br, bi, s5_c_re[0], s5_c_im[0])
    wb, wc = wb.astype(BF16), wc.astype(BF16)
    fnorm = final_norm.reshape(1, d)

    w_upg = []

    def block_fwd(h, l):
        hn = _rows_call(_rms_stage, [h], [row(norm_mlp, l)], [BF16], name=f"f_norm_mlp{l}")
        (up, act), (dn8,) = _mm(hn, w_upg[l], epilogue=_relu2_epilogue, out_dtypes=(F32, BF16), name=f"f_up{l}",
                                comm=(gather_of(w_down[l]),))
        w_dn = dn8.reshape(ff, d)
        h2 = _mm(act, w_dn, extras=(h,), epilogue=_add_epilogue, name=f"f_down{l}")
        hq = _rows_call(_rms_stage, [h2], [row(norm_ple, l)], [BF16], name=f"f_norm_ple{l}")
        gpre = _mm(hq, w_pgg[l], name=f"f_ple_gate{l}")
        h3, pp = _mm(p[l, 0], w_ppg[l], extras=(gpre, h2), epilogue=_ple_epilogue, out_dtypes=(F32, F32),
                     name=f"f_ple_proj{l}")
        return h3, dict(h=h, hn=hn, up=up, act=act, h2=h2, hq=hq, gpre=gpre, pp=pp, w_dn=w_dn)

    hn0 = _rows_call(_rms_stage, [x2], [row(norm_mix, 0)], [BF16], name="f_norm_mix0")
    shard_shapes = [conv_w[0].shape, s5_d.shape, b_glu.shape]
    proj, (oe8, pg8, small) = _mm(hn0, w_main, name="f_proj", comm=(
        gather_of(w_out_e[0]), gather_of(w_ple_gate), _Gather(_pack([conv_w[0], s5_d, b_glu]))))
    w_oe = oe8.reshape(d, d)
    w_top, w_bot = w_oe[:aw], w_oe[aw:]
    w_pgg = jnp.transpose(pg8, (1, 0, 2, 3)).reshape(2, d, d)
    conv_g, s5d_g, bglu_g = zip(*[_unpack(small[j], shard_shapes) for j in range(N_DEV)])
    conv_full = jnp.concatenate(conv_g, axis=1)
    s5d_full = jnp.concatenate(s5d_g, axis=1)
    bglu_full = jnp.concatenate(bglu_g, axis=1)
    ab = _mm(hn0, w_tail, name="f_ab")
    (oa, st_a), (gl8, oo8) = _hgrn_fwd(proj, lb0, g_norm_a, heads=ha, name="f_hgrn",
                                       comm=(gather_of(w_glu[0]), gather_of(w_out_o[0])))
    w_gl, w_oo = gl8.reshape(d, d), oo8.reshape(d, d)
    qkv = _conv_fwd(proj, conv_full, col_off=2 * d, name="f_conv")
    slots_to_cols = lambda g8: jnp.transpose(g8, (1, 0, 2)).reshape(g8.shape[1], -1)
    (ob, st_b), (up8,) = _delta_fwd(qkv, ab, proj, hp, g_norm_b, heads=hb, z_off=z_col // HEAD, name="f_delta",
                                    comm=(gather_of(w_up[0]),))
    w_upg.append(slots_to_cols(up8))
    h1, (pp8,) = _mm(oa, w_top, extras=(x2,), epilogue=_add_epilogue, name="f_out_a", comm=(gather_of(w_ple_proj),))
    w_ppg = jnp.transpose(pp8, (1, 2, 0, 3)).reshape(2, ple, d)
    h1 = _mm(ob, w_bot, extras=(h1,), epilogue=_add_epilogue, name="f_out_b")
    h3, sv0 = block_fwd(h1, 0)

    u = _rows_call(_rms_stage, [h3], [row(norm_mix, 1)], [F32], name="f_norm_mix1")
    (y, cins, s5_states), (up8,) = _s5_fwd(u, wb, wc, lt, s5d_full, name="f_s5", comm=(gather_of(w_up[1]),))
    w_upg.append(slots_to_cols(up8))
    act_g = _rows_call(_gelu_stage, [y], [], [BF16], name="f_gelu")
    gl_raw = _mm(act_g, w_gl, name="f_glu")
    glu = _rows_call(_glu_stage, [y, gl_raw], [bglu_full], [BF16], name="f_glu_gate")
    h4 = _mm(glu, w_oo, extras=(h3,), epilogue=_add_epilogue, name="f_out_o")
    h6, sv1 = block_fwd(h4, 1)
    dh, d_fnorm, loss8 = _loss_call(h6, fnorm, target, name="loss")
    loss = lax.psum(loss8[0, 0], ("x", "y", "c"))

    dshard, ffs, cols = d // N_DEV, ff // N_DEV, w_in_e.shape[2]
    rows8 = lambda g: _AllToAll(g.reshape(N_DEV, -1, g.shape[-1]))
    col_slots = lambda g: jnp.transpose(g.reshape(g.shape[0], N_DEV, -1), (1, 0, 2))
    cols8 = lambda g: _AllToAll(col_slots(g))

    def halves(g8):
        r = g8.shape[1] // 2
        return _AllToAll(g8, rows=(0, r)), _AllToAll(g8, rows=(r, r))

    def block_bwd(dh3, l, sv, carried=(), carried_wup=(), carried_up=()):
        (dgpre, dpp), _ = _rows_vjp(_ple_stage, [sv["h2"], sv["gpre"], sv["pp"]], [], [dh3],
                                    row_grads={1: BF16, 2: BF16}, name=f"b_ple{l}")
        g_pp = _mm(p[l, 0], dpp, ta=True, out_dtypes=(BF16,), name=f"b_w_ple_proj{l}")
        g_pg = _mm(sv["hq"], dgpre, ta=True, out_dtypes=(BF16,), name=f"b_w_ple_gate{l}")
        dhq = _mm(dgpre, w_pgg[l], tb=True, name=f"b_ple_gate{l}")
        (dh2, dh2_mm), (g_nple,) = _rows_vjp(_rms_stage, [sv["h2"]], [row(norm_ple, l)], [dhq],
                                             row_grads={0: (F32, MM_DTYPE)}, adds={0: dh3}, name=f"b_norm_ple{l}")
        dup, (r_pg, r_pp) = _mm(dh2_mm, sv["w_dn"], tb=True, extras=(sv["up"],), epilogue=_relu2_grad_epilogue,
                                out_dtypes=(BF16,), name=f"b_down{l}", comm=(rows8(g_pg), cols8(g_pp)))
        g_dn = _mm(sv["act"], dh2_mm, ta=True, out_dtypes=(BF16,), name=f"b_w_down{l}", comm=carried)
        g_dn, r_carried = g_dn if carried else (g_dn, [])
        g_up = _mm(sv["hn"], dup, ta=True, out_dtypes=(BF16,), tn=ffs, out_slots=True, name=f"b_w_up{l}",
                   comm=carried_wup)
        g_up, r_carried_wup = g_up if carried_wup else (g_up, [])
        dhn = _mm(dup, w_upg[l], tb=True, name=f"b_up{l}", comm=carried_up)
        dhn, r_carried_up = dhn if carried_up else (dhn, [])
        (dh0,), (g_nmlp,) = _rows_vjp(_rms_stage, [sv["h"]], [row(norm_mlp, l)], [dhn], row_grads={0: F32},
                                      adds={0: dh2}, name=f"b_norm_mlp{l}")
        return dh0, dict(w_ple_proj=r_pp, w_ple_gate=r_pg, norm_ple=g_nple, w_down=g_dn, w_up=g_up, norm_mlp=g_nmlp,
                         carried=r_carried, carried_wup=r_carried_wup, carried_up=r_carried_up)

    dh4, gb1 = block_bwd(dh, 1, sv1)
    up1_a, up1_b = halves(gb1["w_up"])
    dglu = _mm(dh4, w_oo, tb=True, name="b_out_o")
    g_oo = _mm(glu, dh4, ta=True, out_dtypes=(BF16,), name="b_w_out_o")
    (dy1, dgl), (g_bglu,) = _rows_vjp(_glu_stage, [y, gl_raw], [bglu_full], [dglu], row_grads={0: F32, 1: BF16},
                                      name="b_glu_gate")
    g_gl = _mm(act_g, dgl, ta=True, out_dtypes=(BF16,), name="b_w_glu")
    dact = _mm(dgl, w_gl, tb=True, name="b_glu")
    (dy,), _ = _rows_vjp(_gelu_stage, [y], [], [dact], row_grads={0: F32}, adds={0: dy1}, name="b_gelu")
    (du, dwb, dwc, g_s5d, dlam), (r_dn1, r_up1_a) = _s5_bwd(u, dy, wb, wc, lt, s5d_full, cins, s5_states, name="b_s5",
                                                           comm=(rows8(gb1["w_down"]), up1_a))
    (dh3,), (g_nmix1,) = _rows_vjp(_rms_stage, [h3], [row(norm_mix, 1)], [du], row_grads={0: F32}, adds={0: dh4},
                                   name="b_norm_mix1")
    dlr, dli, dbr, dbi, g_cre, g_cim = _s5_unpack(dwb, dwc, dlam)
    g_are, g_aim, g_ldt, g_bre, g_bim, _ = _small_vjp(_s5_prep_stage, prep_in, [dlr, dli, dbr, dbi], name="b_s5_prep")

    early_grads = dict(
        s5_a_re=g_are[None], s5_a_im=g_aim[None], s5_b_re=g_bre.reshape(s5_b_re.shape),
        s5_b_im=g_bim.reshape(s5_b_im.shape), s5_c_re=g_cre[None], s5_c_im=g_cim[None],
        s5_log_dt=g_ldt.reshape(1, groups), final_norm=d_fnorm.reshape(d), s5_d=g_s5d, b_glu=g_bglu)
    dh1, gb0 = block_bwd(dh3, 0, sv0, (rows8(g_oo), rows8(g_gl)), (up1_b,),
                         (_Gather(_pack(list(early_grads.values()))),))
    r_oo, r_gl = gb0["carried"]
    r_up1 = [r_up1_a, gb0["carried_wup"][0]]
    (early_parts,) = gb0["carried_up"]
    dn0_a, dn0_b = halves(gb0["w_down"].reshape(N_DEV, ffs, d))
    up0_a, up0_b = halves(gb0["w_up"])
    doa = _mm(dh1, w_top, tb=True, name="b_out_a")
    dob = _mm(dh1, w_bot, tb=True, name="b_out_b")
    g_oe = jnp.concatenate([_mm(oa, dh1, ta=True, out_dtypes=(BF16,), name="b_w_out_a"),
                            _mm(ob, dh1, ta=True, out_dtypes=(BF16,), name="b_w_out_b")], axis=0)
    (dq, df, di, dg, dlb, g_gna), (r_dn0_a,) = _hgrn_bwd(proj, lb0, g_norm_a, st_a, doa, heads=ha, name="b_hgrn",
                                                        comm=(dn0_a,))
    (dqb, dkb, dvb, dab, dz, dhp, g_gnb), (r_dn0_b, r_up0_a) = _delta_bwd(
        qkv, ab, proj, hp, g_norm_b, st_b, dob, heads=hb, z_off=z_col // HEAD, name="b_delta", comm=(dn0_b, up0_a))
    (dqkv, g_conv), (r_oe,) = _conv_bwd(proj, conv_full, jnp.concatenate([dqb, dkb, dvb], axis=1), col_off=2 * d,
                                        name="b_conv", comm=(rows8(g_oe),))
    dproj = jnp.concatenate([dq, df, di, dg, dqkv, dz], axis=1)
    g_main, (r_up0_b,) = _mm(hn0, dproj, ta=True, out_dtypes=(BF16,), name="b_w_proj", comm=(up0_b,))
    r_dn0, r_up0 = [r_dn0_a, r_dn0_b], [r_up0_a, r_up0_b]
    g_tail = _mm(hn0, dab, ta=True, out_dtypes=(BF16,), name="b_w_ab")
    in_a, in_b = halves(col_slots(jnp.concatenate([g_main, g_tail[:, :2 * hb]], axis=1)))
    dhn0, (r_in_a,) = _mm(dproj, w_main, tb=True, name="b_proj", comm=(in_a,))
    dhn0 = _mm(dab, w_tail, tb=True, extras=(dhn0,), epilogue=_add_epilogue, name="b_ab")
    (dx,), (g_nmix0,) = _rows_vjp(_rms_stage, [x2], [row(norm_mix, 0)], [dhn0], row_grads={0: F32}, adds={0: dh1},
                                  name="b_norm_mix0")
    g_lb = jnp.concatenate(_small_vjp(_lb0_stage, lb_rows, [dlb], name="b_lb0"), axis=0)

    late_grads = dict(
        norm_mix=jnp.concatenate([g_nmix0, g_nmix1], axis=0),
        norm_mlp=jnp.concatenate([gb0["norm_mlp"], gb1["norm_mlp"]], axis=0),
        norm_ple=jnp.concatenate([gb0["norm_ple"], gb1["norm_ple"]], axis=0),
        hgrn_lb=g_lb, g_norm_a=g_gna, a_log=dhp[0:1, :hb], dt_bias=dhp[1:2, :hb], g_norm_b=g_gnb, conv_w=g_conv)
    rep_names = ["norm_mix", "norm_mlp", "norm_ple", "hgrn_lb", "g_norm_a", "a_log", "dt_bias", "g_norm_b", "s5_a_re",
                 "s5_a_im", "s5_b_re", "s5_b_im", "s5_c_re", "s5_c_im", "s5_log_dt", "final_norm"]
    late_parts = _comm_call(_Gather(_pack(list(late_grads.values()))), name="ag_small_grads")
    summed = {}
    for tag, grads, parts in (("early", early_grads, early_parts), ("late", late_grads, late_parts)):
        sums = _unpack(_sum_parts(parts, name=f"sum_small_grads_{tag}"), [g.shape for g in grads.values()])
        summed.update(zip(grads, sums))
    cw = conv_w.shape[2]
    dshard = d // N_DEV
    shard_g = dict(conv_w=lax.dynamic_slice(summed["conv_w"], (0, me * cw), (CONV_WIDTH, cw))[None],
                   s5_d=lax.dynamic_slice(summed["s5_d"], (0, me * dshard), (1, dshard)),
                   b_glu=lax.dynamic_slice(summed["b_glu"], (0, me * dshard), (1, dshard)))
    small_names = rep_names + ["conv_w", "s5_d", "b_glu"]
    g_small = [summed[k] if k in rep_names else shard_g[k] for k in small_names]
    shapes = [args[k].shape for k in small_names]
    sm_out = _adamw(_pack([args[k] for k in small_names])[None], [_pack(g_small)[None]],
                    _pack([args["m_" + k] for k in small_names])[None], _pack([args["v_" + k] for k in small_names])[None],
                    name="adamw_small")
    sm_out = [dict(zip(small_names, _unpack(o[0], shapes))) for o in sm_out]

    up_out, (r_in_b,) = _adamw(w_up, [r_up0, r_up1], m_w_up, v_w_up, name="adamw_w_up", comm=(in_b,))
    received = dict(w_in_e=[[r_in_a, r_in_b]], w_out_e=[r_oe], w_glu=[r_gl], w_out_o=[r_oo],
                    w_down=[r_dn0, r_dn1], w_ple_gate=[gb0["w_ple_gate"], gb1["w_ple_gate"]],
                    w_ple_proj=[gb0["w_ple_proj"], gb1["w_ple_proj"]])
    big_out = {k: _adamw(args[k], layers, args["m_" + k], args["v_" + k], name="adamw_" + k)
               for k, layers in received.items()}
    big_out["w_up"] = up_out

    names = ["norm_mix", "norm_mlp", "norm_ple", "w_in_e", "w_out_e", "hgrn_lb", "g_norm_a", "conv_w", "a_log", "dt_bias",
             "g_norm_b", "s5_a_re", "s5_a_im", "s5_b_re", "s5_b_im", "s5_c_re", "s5_c_im", "s5_d", "s5_log_dt", "w_glu",
             "b_glu", "w_out_o", "w_up", "w_down", "w_ple_gate", "w_ple_proj", "final_norm"]
    result = [loss, dx[None]]
    for j in range(4):
        result += [big_out[k][j] if k in big_out else sm_out[j][k] for k in names]
    return tuple(result)
```

```python
import functools
import math
import operator

import numpy as np
import jax
import jax.numpy as jnp
from jax import lax
from jax.experimental import pallas as pl
from jax.experimental.pallas import tpu as pltpu

F32 = jnp.float32
BF16 = jnp.bfloat16
MM_DTYPE = BF16
HI = lax.Precision.HIGHEST
MESH = pl.DeviceIdType.MESH

NORM_EPS = 1e-6
CHUNK = 64
HEAD = 128
CONV_WIDTH = 4
S5_GROUP = 16
S5_STATE = 64
S5_GB = 8
S5_HALF = S5_GB * S5_STATE
N_DEV = 8
HEADS_PER_STEP = 8
HGRN_SUB = 16
ADAM_LR, ADAM_B1, ADAM_B2, ADAM_EPS, ADAM_WD, ADAM_STEP = 0.001, 0.9, 0.999, 1e-08, 0.01, 10
VMEM_LIMIT = 56 * 1024 * 1024

NN = (((1,), (0,)), ((), ()))
NT = (((1,), (1,)), ((), ()))
TN = (((0,), (0,)), ((), ()))


def _dot(a, b, dn=NN):
    return lax.dot_general(a, b, dn, precision=HI, preferred_element_type=F32)


def _hdot(a, b, dn=NN):
    return lax.dot_general(a, b, dn, precision=lax.Precision.HIGH, preferred_element_type=F32)


def _bdot_raw(a, b, dn=NN):
    return lax.dot_general(a.astype(BF16), b.astype(BF16), dn, preferred_element_type=F32)


@functools.partial(jax.custom_vjp, nondiff_argnums=(2,))
def _bdot(a, b, dn):
    return _bdot_raw(a, b, dn)


def _bdot_fwd(a, b, dn):
    return _bdot_raw(a, b, dn), (a, b)


def _bdot_bwd(dn, res, g):
    a, b = res
    if dn == NN:
        return _bdot_raw(g, b, NT), _bdot_raw(a, g, TN)
    if dn == NT:
        return _bdot_raw(g, b, NN), _bdot_raw(g, a, TN)
    assert dn == TN
    return _bdot_raw(b, g, NT), _bdot_raw(a, g, NN)


_bdot.defvjp(_bdot_fwd, _bdot_bwd)


def _per_head(f):
    def g(*args, **kw):
        n = [len(a.vals) for a in args if isinstance(a, _Heads)]
        if not n:
            return f(*args, **kw)
        return _Heads([f(*[a.vals[j] if isinstance(a, _Heads) else a for a in args], **kw) for j in range(n[0])])
    return g


class _Heads:
    def __init__(self, vals):
        self.vals = list(vals)

    def __add__(self, o):
        return _per_head(operator.add)(self, o)

    def __radd__(self, o):
        return _per_head(operator.add)(o, self)

    def __sub__(self, o):
        return _per_head(operator.sub)(self, o)

    def __rsub__(self, o):
        return _per_head(operator.sub)(o, self)

    def __mul__(self, o):
        return _per_head(operator.mul)(self, o)

    def __rmul__(self, o):
        return _per_head(operator.mul)(o, self)

    def __neg__(self):
        return _per_head(operator.neg)(self)


_exp, _log, _where, _sum, _mean = (_per_head(f) for f in (jnp.exp, jnp.log, jnp.where, jnp.sum, jnp.mean))
_sigmoid, _rsqrt, _equal = _per_head(jax.nn.sigmoid), _per_head(lax.rsqrt), _per_head(operator.eq)
_hdot_h, _bdot_h = _per_head(_hdot), _per_head(_bdot)
_rows = _per_head(lambda a, lo, n: a[lo:lo + n, :])
_row_concat = _per_head(lambda *xs: jnp.concatenate(xs, axis=0))


def _params(n_axes):
    return pltpu.CompilerParams(dimension_semantics=("arbitrary",) * n_axes, vmem_limit_bytes=VMEM_LIMIT)


def _full_spec(a):
    nd = a.ndim
    return pl.BlockSpec(a.shape, lambda *_: (0,) * nd)


def _mm(a, b, *, name, ta=False, tb=False, extras=(), epilogue=None, out_dtypes=(F32,), tm=1024, tn=1024, tk=2048,
        out_slots=False, comm=()):
    m = a.shape[1] if ta else a.shape[0]
    k = a.shape[0] if ta else a.shape[1]
    n = b.shape[0] if tb else b.shape[1]
    assert k == (b.shape[1] if tb else b.shape[0]), (name, a.shape, b.shape)
    tm, tn, tk = min(tm, m), min(tn, n), min(tk, k)
    assert m % tm == 0 and n % tn == 0 and k % tk == 0, (name, m, n, k)
    nk = k // tk
    n_ex, n_out = len(extras), len(out_dtypes)
    dn = (((0 if ta else 1,), (1 if tb else 0,)), ((), ()))

    def body(a_ref, b_ref, *rest):
        ex_refs, out_refs = rest[:n_ex], rest[n_ex:n_ex + n_out]
        part = lax.dot_general(a_ref[...].astype(MM_DTYPE), b_ref[...].astype(MM_DTYPE), dn, preferred_element_type=F32)

        def finish(acc):
            outs = epilogue(acc, *[r[...] for r in ex_refs]) if epilogue is not None else (acc,)
            for o_ref, o in zip(out_refs, outs):
                o_ref[...] = o.astype(o_ref.dtype)

        if nk == 1:
            finish(part)
            return
        acc_ref = rest[-1]
        kk = pl.program_id(2)

        @pl.when(kk == 0)
        def _():
            acc_ref[...] = part

        @pl.when((kk > 0) & (kk < nk - 1))
        def _():
            acc_ref[...] += part

        @pl.when(kk == nk - 1)
        def _():
            finish(acc_ref[...] + part)

    a_spec = pl.BlockSpec((tk, tm), lambda i, j, q: (q, i)) if ta else pl.BlockSpec((tm, tk), lambda i, j, q: (i, q))
    b_spec = pl.BlockSpec((tn, tk), lambda i, j, q: (j, q)) if tb else pl.BlockSpec((tk, tn), lambda i, j, q: (q, j))
    ex_specs = []
    for e in extras:
        if e.shape[0] == 1 and m != 1:
            ex_specs.append(pl.BlockSpec((1, tn), lambda i, j, q: (0, j)))
        else:
            ex_specs.append(pl.BlockSpec((tm, tn), lambda i, j, q: (i, j)))
    if out_slots:
        out_spec, out_dims = pl.BlockSpec((None, tm, tn), lambda i, j, q: (j, i, 0)), (n // tn, m, tn)
    else:
        out_spec, out_dims = pl.BlockSpec((tm, tn), lambda i, j, q: (i, j)), (m, n)
    outs, exchanged = _call(
        body, (a, b, *extras), name=name, grid=(m // tm, n // tn, nk),
        in_specs=[a_spec, b_spec] + ex_specs,
        out_specs=[out_spec for _ in out_dtypes],
        out_shape=[jax.ShapeDtypeStruct(out_dims, dt) for dt in out_dtypes],
        scratch_shapes=[pltpu.VMEM((tm, tn), F32)] if nk > 1 else [], comm=comm)
    outs = outs[0] if n_out == 1 else tuple(outs)
    return (outs, exchanged) if comm else outs


def _rows_call(fn, rows, consts, out_dtypes, *, name, tr=256):
    s = rows[0].shape[0]
    tr = min(tr, s)
    nr, nc = len(rows), len(consts)
    widths = [o.shape[1] for o in jax.eval_shape(
        fn, *[jax.ShapeDtypeStruct((tr, r.shape[1]), F32) for r in rows],
        *[jax.ShapeDtypeStruct(c.shape, F32) for c in consts])]

    def body(*refs):
        rv = [r[...].astype(F32) for r in refs[:nr]]
        cv = [c[...] for c in refs[nr:nr + nc]]
        for o_ref, o in zip(refs[nr + nc:], fn(*rv, *cv)):
            o_ref[...] = o.astype(o_ref.dtype)

    outs = pl.pallas_call(
        body, name=name, grid=(s // tr,),
        in_specs=[pl.BlockSpec((tr, r.shape[1]), lambda i: (i, 0)) for r in rows] + [_full_spec(c) for c in consts],
        out_specs=[pl.BlockSpec((tr, w), lambda i: (i, 0)) for w in widths],
        out_shape=[jax.ShapeDtypeStruct((s, w), dt) for w, dt in zip(widths, out_dtypes)],
        compiler_params=_params(1),
    )(*rows, *consts)
    return outs[0] if len(outs) == 1 else tuple(outs)


def _rows_vjp(fn, rows, consts, cots, *, name, row_grads, adds=None, tr=256):
    adds = adds or {}
    s = rows[0].shape[0]
    tr = min(tr, s)
    nr, nc, nt = len(rows), len(consts), len(cots)
    rg = [(i, dt) for i in sorted(row_grads)
          for dt in (row_grads[i] if isinstance(row_grads[i], tuple) else (row_grads[i],))]
    ad = sorted(adds)

    def body(*refs):
        rv = [r[...].astype(F32) for r in refs[:nr]]
        cv = [c[...] for c in refs[nr:nr + nc]]
        ct = [c[...].astype(F32) for c in refs[nr + nc:nr + nc + nt]]
        av = {i: r[...].astype(F32) for i, r in zip(ad, refs[nr + nc + nt:nr + nc + nt + len(ad)])}
        out_refs = refs[nr + nc + nt + len(ad):]
        _, vjp = jax.vjp(fn, *rv, *cv)
        grads = vjp(tuple(ct))
        for o_ref, (i, _) in zip(out_refs[:len(rg)], rg):
            g = grads[i]
            if i in av:
                g = g + av[i]
            o_ref[...] = g.astype(o_ref.dtype)

        @pl.when(pl.program_id(0) == 0)
        def _():
            for o_ref in out_refs[len(rg):]:
                o_ref[...] = jnp.zeros_like(o_ref)

        for o_ref, g in zip(out_refs[len(rg):], grads[nr:]):
            o_ref[...] += g

    row_spec = lambda a: pl.BlockSpec((tr, a.shape[1]), lambda i: (i, 0))
    outs = pl.pallas_call(
        body, name=name, grid=(s // tr,),
        in_specs=[row_spec(r) for r in rows] + [_full_spec(c) for c in consts] + [row_spec(c) for c in cots]
        + [row_spec(adds[i]) for i in ad],
        out_specs=[row_spec(rows[i]) for i, _ in rg] + [_full_spec(c) for c in consts],
        out_shape=[jax.ShapeDtypeStruct(rows[i].shape, dt) for i, dt in rg]
        + [jax.ShapeDtypeStruct(c.shape, F32) for c in consts],
        compiler_params=_params(1),
    )(*rows, *consts, *cots, *[adds[i] for i in ad])
    return list(outs[:len(rg)]), list(outs[len(rg):])


def _small_call(fn, ins, *, name):
    shapes = jax.eval_shape(fn, *[jax.ShapeDtypeStruct(a.shape, F32) for a in ins])

    def body(*refs):
        for o_ref, o in zip(refs[len(ins):], fn(*[r[...] for r in refs[:len(ins)]])):
            o_ref[...] = o

    return pl.pallas_call(
        body, name=name, in_specs=[_full_spec(a) for a in ins],
        out_specs=[pl.BlockSpec(o.shape, functools.partial(lambda nd, *_: (0,) * nd, len(o.shape))) for o in shapes],
        out_shape=[jax.ShapeDtypeStruct(o.shape, F32) for o in shapes], grid=(1,),
        compiler_params=_params(1),
    )(*ins)


def _small_vjp(fn, ins, cots, *, name):
    def body(*refs):
        vals = [r[...] for r in refs[:len(ins)]]
        ct = [r[...] for r in refs[len(ins):len(ins) + len(cots)]]
        _, vjp = jax.vjp(fn, *vals)
        for o_ref, g in zip(refs[len(ins) + len(cots):], vjp(tuple(ct))):
            o_ref[...] = g

    return pl.pallas_call(
        body, name=name, in_specs=[_full_spec(a) for a in ins] + [_full_spec(c) for c in cots],
        out_specs=[_full_spec(a) for a in ins],
        out_shape=[jax.ShapeDtypeStruct(a.shape, F32) for a in ins], grid=(1,),
        compiler_params=_params(1),
    )(*ins, *cots)


def _rms(x, g):
    return x * _rsqrt(_mean(x * x, axis=-1, keepdims=True) + NORM_EPS) * g


def _rms_stage(x, g):
    return (_rms(x, g),)


def _silu(x):
    return x * _sigmoid(x)


def _softplus(x):
    return jnp.maximum(x, 0.0) + jnp.log1p(jnp.exp(-jnp.abs(x)))


def _gelu(x):
    return jax.nn.gelu(x, approximate=True)


def _gelu_stage(y):
    return (_gelu(y),)


def _glu_stage(y, gl_raw, b):
    return (_gelu(y) * jax.nn.sigmoid(gl_raw + b),)


def _ple_stage(h, gpre, pp):
    return (h + jax.nn.sigmoid(gpre) * pp,)


def _relu2_grad_epilogue(acc, up):
    return (acc * (2.0 * jnp.maximum(up, 0.0)),)


def _lb0_stage(x0, x1, x2):
    mx = jnp.maximum(jnp.maximum(x0, x1), x2)
    e0, e1, e2 = jnp.exp(x0 - mx), jnp.exp(x1 - mx), jnp.exp(x2 - mx)
    return (e0 / (e0 + e1 + e2),)


def _s5_prep_stage(a_re, a_im, log_dt, b_re, b_im, expand):
    step = jnp.exp(log_dt)
    mag = jnp.exp(a_re * step)
    lr = mag * jnp.cos(a_im * step)
    li = mag * jnp.sin(a_im * step)
    den = a_re * a_re + a_im * a_im
    cr = ((lr - 1.0) * a_re + li * a_im) / den
    ci = (li * a_re - (lr - 1.0) * a_im) / den
    cr_e, ci_e = _dot(cr, expand), _dot(ci, expand)
    return lr, li, cr_e * b_re - ci_e * b_im, cr_e * b_im + ci_e * b_re


def _loss_call(h, g, target, *, name, tr=256):
    s, d = h.shape
    tr = min(tr, s)

    def loss_fn(hv, gv, tv):
        err = _rms(hv, gv) - tv
        return 0.5 * jnp.sum(jnp.mean(err * err, axis=-1))

    def body(h_ref, g_ref, t_ref, dh_ref, dg_ref, loss_ref):
        val, (dh, dg) = jax.value_and_grad(loss_fn, argnums=(0, 1))(h_ref[...], g_ref[...], t_ref[...])
        dh_ref[...] = dh

        @pl.when(pl.program_id(0) == 0)
        def _():
            dg_ref[...] = jnp.zeros_like(dg_ref)
            loss_ref[...] = jnp.zeros_like(loss_ref)

        dg_ref[...] += dg
        loss_ref[...] += jnp.full(loss_ref.shape, val, F32)

    row = pl.BlockSpec((tr, d), lambda i: (i, 0))
    return pl.pallas_call(
        body, name=name, grid=(s // tr,),
        in_specs=[row, _full_spec(g), row],
        out_specs=[row, _full_spec(g), pl.BlockSpec((8, 128), lambda i: (0, 0))],
        out_shape=[jax.ShapeDtypeStruct((s, d), F32), jax.ShapeDtypeStruct(g.shape, F32),
                   jax.ShapeDtypeStruct((8, 128), F32)],
        compiler_params=_params(1),
    )(h, g, target)


def _hgrn_chunk(q, fp, iv, gp, lb, gn, st_t):
    c = CHUNK
    row = lax.broadcasted_iota(jnp.int32, (c, c), 0)
    col = lax.broadcasted_iota(jnp.int32, (c, c), 1)
    causal = row >= col
    fg = lb + (1.0 - lb) * _sigmoid(fp)
    k = 1.0 - fg
    lf = _log(fg)
    cum = _hdot_h(causal.astype(F32), lf, NN)
    cend = _sum(lf, axis=0, keepdims=True)
    shift = HGRN_SUB.bit_length() - 1
    ref = _hdot_h(((row >> shift) > (col >> shift)).astype(F32), lf, NN)
    q_dec = q * _exp(cum - ref)
    key_row = lax.broadcasted_iota(jnp.int32, (c, 1), 0)
    blocks = []
    for lo in range(0, c, HGRN_SUB):
        live = key_row < lo + HGRN_SUB
        k_dec = _where(live, k * _exp(_where(live, _rows(ref, lo, 1) - cum, 0.0)), 0.0)
        blocks.append(_hdot_h(_rows(q_dec, lo, HGRN_SUB), k_dec, NT))
    scores = _where(causal, _row_concat(*blocks), 0.0)
    out = _bdot_h(scores, iv, NN) + _bdot_h(q * _exp(cum), st_t, NT)
    st_new = st_t * _exp(cend) + _bdot_h(iv, k * _exp(cend - cum), TN)
    res = _rms(out, gn) * _silu(gp)
    return res, st_new


def _hgrn_heads(qs, fs, ivs, gs, lbs, gn, sts):
    res, st_new = _hgrn_chunk(_Heads(qs), _Heads(fs), _Heads(ivs), _Heads(gs), _Heads(lbs), gn, _Heads(sts))
    return res.vals, st_new.vals


def _lanes(j):
    return slice(j * HEAD, (j + 1) * HEAD)


def _hgrn_fwd(proj, lb, gn, *, heads, name, comm=()):
    s = proj.shape[0]
    n = s // CHUNK
    hpb = min(HEADS_PER_STEP, heads)
    assert heads % hpb == 0

    def body(q_ref, f_ref, i_ref, g_ref, lb_ref, gn_ref, o_ref, st_ref, state):
        @pl.when(pl.program_id(1) == 0)
        def _():
            state[...] = jnp.zeros_like(state)

        gnv = gn_ref[...]
        loaded = [(q_ref[:, _lanes(j)], f_ref[:, _lanes(j)], i_ref[:, _lanes(j)], g_ref[:, _lanes(j)],
                   lb_ref[:, _lanes(j)], state[j]) for j in range(hpb)]
        qs, fs, ivs, gs, lbs, sts = (list(t) for t in zip(*loaded))
        res, st_new = _hgrn_heads(qs, fs, ivs, gs, lbs, gnv, sts)
        for j in range(hpb):
            st_ref[j] = sts[j]
            o_ref[:, _lanes(j)] = res[j].astype(o_ref.dtype)
            state[j] = st_new[j]

    wide = hpb * HEAD
    blk = lambda off: pl.BlockSpec((CHUNK, wide), lambda h, c: (c, off // hpb + h))
    return _call(
        body, (proj, proj, proj, proj, lb, gn), name=name, grid=(heads // hpb, n),
        in_specs=[blk(0), blk(heads), blk(2 * heads), blk(3 * heads),
                  pl.BlockSpec((1, wide), lambda h, c: (0, h)), pl.BlockSpec((1, HEAD), lambda h, c: (0, 0))],
        out_specs=[pl.BlockSpec((CHUNK, wide), lambda h, c: (c, h)),
                   pl.BlockSpec((hpb, None, HEAD, HEAD), lambda h, c: (h, c, 0, 0))],
        out_shape=[jax.ShapeDtypeStruct((s, heads * HEAD), BF16), jax.ShapeDtypeStruct((heads, n, HEAD, HEAD), F32)],
        scratch_shapes=[pltpu.VMEM((hpb, HEAD, HEAD), F32)], comm=comm)


def _hgrn_bwd(proj, lb, gn, states, d_out, *, heads, name, comm=()):
    s = proj.shape[0]
    n = s // CHUNK
    hpb = min(HEADS_PER_STEP, heads)

    def body(q_ref, f_ref, i_ref, g_ref, lb_ref, gn_ref, st_ref, do_ref,
             dq_ref, df_ref, di_ref, dg_ref, dlb_ref, dgn_ref, dstate):
        h, c = pl.program_id(0), pl.program_id(1)

        @pl.when(c == 0)
        def _():
            dstate[...] = jnp.zeros_like(dstate)
            dlb_ref[...] = jnp.zeros_like(dlb_ref)

        @pl.when((c == 0) & (h == 0))
        def _():
            dgn_ref[...] = jnp.zeros_like(dgn_ref)

        gnv = gn_ref[...]
        loaded = [(q_ref[:, _lanes(j)], f_ref[:, _lanes(j)], i_ref[:, _lanes(j)], g_ref[:, _lanes(j)],
                   lb_ref[:, _lanes(j)], st_ref[j], do_ref[:, _lanes(j)].astype(F32), dstate[j]) for j in range(hpb)]
        qs, fs, ivs, gs, lbs, sts, dos, dss = (list(t) for t in zip(*loaded))
        _, vjp = jax.vjp(_hgrn_heads, qs, fs, ivs, gs, lbs, gnv, sts)
        dqs, dfs, dis, dgs, dlbs, dgn_sum, dsts = vjp((dos, dss))
        for j in range(hpb):
            ln = _lanes(j)
            dq_ref[:, ln] = dqs[j].astype(dq_ref.dtype)
            df_ref[:, ln] = dfs[j].astype(df_ref.dtype)
            di_ref[:, ln] = dis[j].astype(di_ref.dtype)
            dg_ref[:, ln] = dgs[j].astype(dg_ref.dtype)
            dlb_ref[:, ln] += dlbs[j]
            dstate[j] = dsts[j]
        dgn_ref[...] += dgn_sum

    wide = hpb * HEAD
    rev = lambda off: pl.BlockSpec((CHUNK, wide), lambda h, c: (n - 1 - c, off // hpb + h))
    out_blk = pl.BlockSpec((CHUNK, wide), lambda h, c: (n - 1 - c, h))
    width = heads * HEAD
    return _call(
        body, (proj, proj, proj, proj, lb, gn, states, d_out), name=name, grid=(heads // hpb, n),
        in_specs=[rev(0), rev(heads), rev(2 * heads), rev(3 * heads),
                  pl.BlockSpec((1, wide), lambda h, c: (0, h)), pl.BlockSpec((1, HEAD), lambda h, c: (0, 0)),
                  pl.BlockSpec((hpb, None, HEAD, HEAD), lambda h, c: (h, n - 1 - c, 0, 0)), out_blk],
        out_specs=[out_blk, out_blk, out_blk, out_blk,
                   pl.BlockSpec((1, wide), lambda h, c: (0, h)), pl.BlockSpec((1, HEAD), lambda h, c: (0, 0))],
        out_shape=[jax.ShapeDtypeStruct((s, width), BF16)] * 4
        + [jax.ShapeDtypeStruct((1, width), F32), jax.ShapeDtypeStruct((1, HEAD), F32)],
        scratch_shapes=[pltpu.VMEM((hpb, HEAD, HEAD), F32)], comm=comm)


def _shift_rows(x, d, rowi):
    if d == 0:
        return x
    n = x.shape[0]
    rolled = pltpu.roll(x, d % n, 0)
    keep = rowi >= d if d > 0 else rowi < n + d
    return jnp.where(keep, rolled, 0.0)


def _conv_pre(x, w_ref, rowi):
    acc = None
    for j in range(CONV_WIDTH):
        term = w_ref[j:j + 1, :] * _shift_rows(x, CONV_WIDTH - 1 - j, rowi)
        acc = term if acc is None else acc + term
    return acc


def _conv_fwd(proj, w, *, col_off, name, cb=256):
    s = proj.shape[0]
    width = w.shape[1]
    cb = min(cb, width)

    def body(x_ref, w_ref, o_ref):
        rowi = lax.broadcasted_iota(jnp.int32, (s, cb), 0)
        o_ref[...] = _silu(_conv_pre(x_ref[...], w_ref, rowi))

    return pl.pallas_call(
        body, name=name, grid=(width // cb,),
        in_specs=[pl.BlockSpec((s, cb), lambda j: (0, col_off // cb + j)), pl.BlockSpec((CONV_WIDTH, cb), lambda j: (0, j))],
        out_specs=pl.BlockSpec((s, cb), lambda j: (0, j)),
        out_shape=jax.ShapeDtypeStruct((s, width), F32),
        compiler_params=_params(1),
    )(proj, w)


def _conv_bwd(proj, w, d_out, *, col_off, name, cb=256, comm=()):
    s = proj.shape[0]
    width = w.shape[1]
    cb = min(cb, width)

    def body(x_ref, w_ref, do_ref, dx_ref, dw_ref):
        rowi = lax.broadcasted_iota(jnp.int32, (s, cb), 0)
        x = x_ref[...]
        pre = _conv_pre(x, w_ref, rowi)
        sg = jax.nn.sigmoid(pre)
        dpre = do_ref[...] * (sg + pre * sg * (1.0 - sg))
        dx = None
        for j in range(CONV_WIDTH):
            d = CONV_WIDTH - 1 - j
            term = w_ref[j:j + 1, :] * _shift_rows(dpre, -d, rowi)
            dx = term if dx is None else dx + term
            dw_ref[j:j + 1, :] = jnp.sum(dpre * _shift_rows(x, d, rowi), axis=0, keepdims=True)
        dx_ref[...] = dx.astype(dx_ref.dtype)

    return _call(
        body, (proj, w, d_out), name=name, grid=(width // cb,),
        in_specs=[pl.BlockSpec((s, cb), lambda j: (0, col_off // cb + j)), pl.BlockSpec((CONV_WIDTH, cb), lambda j: (0, j)),
                  pl.BlockSpec((s, cb), lambda j: (0, j))],
        out_specs=[pl.BlockSpec((s, cb), lambda j: (0, j)), pl.BlockSpec((CONV_WIDTH, cb), lambda j: (0, j))],
        out_shape=[jax.ShapeDtypeStruct((s, width), BF16), jax.ShapeDtypeStruct((CONV_WIDTH, width), F32)],
        comm=comm)


_lane_concat = _per_head(lambda a, b: jnp.concatenate([a, b], axis=1))
_lane_half = _per_head(lambda a, j: a[:, j * HEAD:(j + 1) * HEAD])


def _tri_inverse(lower):
    c = CHUNK
    row = lax.broadcasted_iota(jnp.int32, (c, c), 0)
    col = lax.broadcasted_iota(jnp.int32, (c, c), 1)
    inv = (row == col).astype(F32)
    lvl = 0
    while (1 << lvl) < c:
        same_pair = (row >> (lvl + 1)) == (col >> (lvl + 1))
        off_block = same_pair & (((row >> lvl) & 1) == 1) & (((col >> lvl) & 1) == 0)
        inv = inv - _hdot_h(_hdot_h(inv, _where(off_block, lower, 0.0), NN), inv, NN)
        lvl += 1
    return inv


@jax.custom_vjp
def _tri_solve(lowers, rhss):
    return _tri_solve_fwd(lowers, rhss)[0]


def _tri_solve_fwd(lowers, rhss):
    inv = _tri_inverse(_Heads(lowers))
    sol = _hdot_h(inv, _Heads(rhss), NN)
    return sol.vals, (inv.vals, sol.vals)


def _tri_solve_bwd(res, g):
    inv, sol = _Heads(res[0]), _Heads(res[1])
    d_rhs = _hdot_h(inv, _Heads(g), TN)
    return (-_hdot_h(d_rhs, sol, NT)).vals, d_rhs.vals


_tri_solve.defvjp(_tri_solve_fwd, _tri_solve_bwd)


def _solve(lower, rhs):
    if isinstance(lower, _Heads):
        return _Heads(_tri_solve(lower.vals, rhs.vals))
    return _tri_solve([lower], [rhs])[0]


def _delta_chunk(h, heads, qr, kr, vr, ab, zp, alog, dtb, gn, st):
    c = CHUNK
    row = lax.broadcasted_iota(jnp.int32, (c, c), 0)
    col = lax.broadcasted_iota(jnp.int32, (c, c), 1)
    causal = row >= col
    strict = row > col
    lane = lax.broadcasted_iota(jnp.int32, (c, HEAD), 1)
    mine = _equal(h, lane)
    la_full = -jnp.exp(alog) * _softplus(ab + dtb)
    cum_full = _hdot(causal.astype(F32), la_full)
    cum = _sum(_where(mine, cum_full, 0.0), axis=1, keepdims=True)
    cend = _sum(_sum(_where(mine, la_full, 0.0), axis=1, keepdims=True), axis=0, keepdims=True)
    beta = _sum(_where(_equal(heads + h, lane), jax.nn.sigmoid(ab), 0.0), axis=1, keepdims=True)
    cum_row = _hdot_h(_where(mine, 1.0, 0.0), cum_full, NT)
    decay = _where(causal, _exp(_where(causal, cum - cum_row, 0.0)), 0.0)
    qn = qr * _rsqrt(_sum(qr * qr, axis=-1, keepdims=True) + NORM_EPS) * (HEAD ** -0.5)
    kn = kr * _rsqrt(_sum(kr * kr, axis=-1, keepdims=True) + NORM_EPS)
    kb = kn * beta
    lower = _where(strict, _bdot_h(kb, kn, NT) * decay, 0.0)
    ecum = _exp(cum)
    sol = _solve(lower, _lane_concat(vr * beta, kb * ecum))
    u, w = _lane_half(sol, 0), _lane_half(sol, 1)
    intra = _bdot_h(qn, kn, NT) * decay
    v_new = u - _bdot_h(w, st, NN)
    out = _bdot_h(qn * ecum, st, NN) + _bdot_h(intra, v_new, NN)
    st_new = st * _exp(cend) + _bdot_h(kn * _exp(cend - cum), v_new, TN)
    res = _rms(out, gn) * _silu(zp)
    return res, st_new


def _delta_heads(hs, heads, qs, ks, vs, ab, zs, alog, dtb, gn, sts):
    res, st_new = _delta_chunk(_Heads(hs), heads, _Heads(qs), _Heads(ks), _Heads(vs), ab, _Heads(zs), alog, dtb, gn,
                               _Heads(sts))
    return res.vals, st_new.vals


def _delta_fwd(qkv, ab, proj, hp, gn, *, heads, z_off, name, comm=()):
    s = qkv.shape[0]
    n = s // CHUNK

    hpb = min(HEADS_PER_STEP, heads)
    assert heads % hpb == 0 and z_off % hpb == 0

    def body(q_ref, k_ref, v_ref, ab_ref, z_ref, hp_ref, gn_ref, o_ref, st_ref, state):
        hb = pl.program_id(1)

        @pl.when(pl.program_id(0) == 0)
        def _():
            for j in range(hpb):
                state[hb * hpb + j] = jnp.zeros((HEAD, HEAD), F32)

        shared = (ab_ref[...], hp_ref[0:1, :], hp_ref[1:2, :], gn_ref[...])
        loaded = [(q_ref[:, _lanes(j)], k_ref[:, _lanes(j)], v_ref[:, _lanes(j)], z_ref[:, _lanes(j)],
                   state[hb * hpb + j]) for j in range(hpb)]
        qs, ks, vs, zs, sts = (list(t) for t in zip(*loaded))
        res, st_new = _delta_heads([hb * hpb + j for j in range(hpb)], heads, qs, ks, vs, shared[0], zs, shared[1],
                                   shared[2], shared[3], sts)
        for j in range(hpb):
            st_ref[j] = sts[j]
            o_ref[:, _lanes(j)] = res[j].astype(o_ref.dtype)
            state[hb * hpb + j] = st_new[j]

    wide = hpb * HEAD
    blk = lambda off: pl.BlockSpec((CHUNK, wide), lambda c, h: (c, off // hpb + h))
    return _call(
        body, (qkv, qkv, qkv, ab, proj, hp, gn), name=name, grid=(n, heads // hpb),
        in_specs=[blk(0), blk(heads), blk(2 * heads), pl.BlockSpec((CHUNK, HEAD), lambda c, h: (c, 0)), blk(z_off),
                  pl.BlockSpec((8, HEAD), lambda c, h: (0, 0)), pl.BlockSpec((1, HEAD), lambda c, h: (0, 0))],
        out_specs=[pl.BlockSpec((CHUNK, wide), lambda c, h: (c, h)),
                   pl.BlockSpec((hpb, None, HEAD, HEAD), lambda c, h: (h, c, 0, 0))],
        out_shape=[jax.ShapeDtypeStruct((s, heads * HEAD), BF16), jax.ShapeDtypeStruct((heads, n, HEAD, HEAD), F32)],
        scratch_shapes=[pltpu.VMEM((heads, HEAD, HEAD), F32)], comm=comm)


def _delta_bwd(qkv, ab, proj, hp, gn, states, d_out, *, heads, z_off, name, comm=()):
    s = qkv.shape[0]
    n = s // CHUNK
    hpb = min(HEADS_PER_STEP, heads)

    def body(q_ref, k_ref, v_ref, ab_ref, z_ref, hp_ref, gn_ref, st_ref, do_ref,
             dq_ref, dk_ref, dv_ref, dab_ref, dz_ref, dhp_ref, dgn_ref, dstate):
        c, hb = pl.program_id(0), pl.program_id(1)

        @pl.when(c == 0)
        def _():
            for j in range(hpb):
                dstate[hb * hpb + j] = jnp.zeros((HEAD, HEAD), F32)

        @pl.when((c == 0) & (hb == 0))
        def _():
            dgn_ref[...] = jnp.zeros_like(dgn_ref)
            dhp_ref[...] = jnp.zeros_like(dhp_ref)

        @pl.when(hb == 0)
        def _():
            dab_ref[...] = jnp.zeros_like(dab_ref)

        shared = (ab_ref[...], hp_ref[0:1, :], hp_ref[1:2, :], gn_ref[...])
        loaded = [(q_ref[:, _lanes(j)], k_ref[:, _lanes(j)], v_ref[:, _lanes(j)], z_ref[:, _lanes(j)], st_ref[j],
                   do_ref[:, _lanes(j)].astype(F32), dstate[hb * hpb + j]) for j in range(hpb)]
        qs, ks, vs, zs, sts, dos, dss = (list(t) for t in zip(*loaded))
        fn = functools.partial(_delta_heads, [hb * hpb + j for j in range(hpb)], heads)
        _, vjp = jax.vjp(fn, qs, ks, vs, shared[0], zs, shared[1], shared[2], shared[3], sts)
        dqs, dks, dvs, dab, dzs, dal, ddt, dgn, dsts = vjp((dos, dss))
        for j in range(hpb):
            ln = _lanes(j)
            dq_ref[:, ln] = dqs[j]
            dk_ref[:, ln] = dks[j]
            dv_ref[:, ln] = dvs[j]
            dz_ref[:, ln] = dzs[j].astype(dz_ref.dtype)
            dstate[hb * hpb + j] = dsts[j]
        dab_ref[...] += dab
        dhp_ref[0:1, :] += dal
        dhp_ref[1:2, :] += ddt
        dgn_ref[...] += dgn

    wide = hpb * HEAD
    rev = lambda off: pl.BlockSpec((CHUNK, wide), lambda c, h: (n - 1 - c, off // hpb + h))
    width = heads * HEAD
    head_blk = pl.BlockSpec((CHUNK, wide), lambda c, h: (n - 1 - c, h))
    ab_blk = pl.BlockSpec((CHUNK, HEAD), lambda c, h: (n - 1 - c, 0))
    return _call(
        body, (qkv, qkv, qkv, ab, proj, hp, gn, states, d_out), name=name, grid=(n, heads // hpb),
        in_specs=[rev(0), rev(heads), rev(2 * heads), ab_blk, rev(z_off),
                  pl.BlockSpec((8, HEAD), lambda c, h: (0, 0)), pl.BlockSpec((1, HEAD), lambda c, h: (0, 0)),
                  pl.BlockSpec((hpb, None, HEAD, HEAD), lambda c, h: (h, n - 1 - c, 0, 0)), head_blk],
        out_specs=[head_blk, head_blk, head_blk, ab_blk, head_blk,
                   pl.BlockSpec((8, HEAD), lambda c, h: (0, 0)), pl.BlockSpec((1, HEAD), lambda c, h: (0, 0))],
        out_shape=[jax.ShapeDtypeStruct((s, width), F32)] * 3
        + [jax.ShapeDtypeStruct((s, HEAD), F32), jax.ShapeDtypeStruct((s, width), BF16),
           jax.ShapeDtypeStruct((8, HEAD), F32), jax.ShapeDtypeStruct((1, HEAD), F32)],
        scratch_shapes=[pltpu.VMEM((heads, HEAD, HEAD), F32)], comm=comm)


def _s5_scan(buf, lt_ref, cin_r, cin_i, tt, reverse):
    nblk = tt // 8
    hl = S5_HALF
    base = 8 if reverse else 0

    def body(j, carry):
        cr, ci = carry
        off = pl.multiple_of((nblk - 1 - j if reverse else j) * 8, 8)
        xr = buf[pl.ds(off, 8), 0:hl]
        xi = buf[pl.ds(off, 8), hl:2 * hl]
        for lv, d in enumerate((1, 2, 4)):
            ar, ai = lt_ref[base + 2 * lv], lt_ref[base + 2 * lv + 1]
            sr = pltpu.roll(xr, 8 - d if reverse else d, 0)
            si = pltpu.roll(xi, 8 - d if reverse else d, 0)
            xr, xi = xr + ar * sr - ai * si, xi + ar * si + ai * sr
        pr, pi = lt_ref[base + 6], lt_ref[base + 7]
        xr, xi = xr + pr * cr - pi * ci, xi + pr * ci + pi * cr
        buf[pl.ds(off, 8), 0:hl] = xr
        buf[pl.ds(off, 8), hl:2 * hl] = xi
        edge = 0 if reverse else 7
        return xr[edge:edge + 1, :], xi[edge:edge + 1, :]

    return lax.fori_loop(0, nblk, body, (cin_r, cin_i))


def _s5_fwd(u, wb, wc, lt, dskip, *, name, tt=1024, comm=()):
    s, d = u.shape
    nb = d // HEAD
    tt = min(tt, s)
    nt = s // tt
    hl = S5_HALF

    def body(u_ref, wb_ref, wc_ref, lt_ref, d_ref, y_ref, cin_ref, st_ref, act_ref, buf, carry):
        @pl.when(pl.program_id(1) == 0)
        def _():
            carry[...] = jnp.zeros_like(carry)

        cin_ref[...] = carry[0:1, :]
        uv = u_ref[...]
        buf[...] = _bdot_raw(uv, wb_ref[...])
        cr, ci = _s5_scan(buf, lt_ref, carry[0:1, 0:hl], carry[0:1, hl:2 * hl], tt, False)
        carry[0:1, 0:hl] = cr
        carry[0:1, hl:2 * hl] = ci
        states = buf[...].astype(BF16)
        st_ref[...] = states
        yv = _bdot_raw(states, wc_ref[...]) + d_ref[...] * uv
        y_ref[...] = yv
        act_ref[...] = _gelu(yv).astype(act_ref.dtype)

    return _call(
        body, (u, wb, wc, lt, dskip), name=name, grid=(nb, nt),
        in_specs=[pl.BlockSpec((tt, HEAD), lambda b, t: (t, b)),
                  pl.BlockSpec((None, HEAD, 2 * hl), lambda b, t: (b, 0, 0)),
                  pl.BlockSpec((None, 2 * hl, HEAD), lambda b, t: (b, 0, 0)),
                  pl.BlockSpec((None, 16, 8, hl), lambda b, t: (b, 0, 0, 0)),
                  pl.BlockSpec((1, HEAD), lambda b, t: (0, b))],
        out_specs=[pl.BlockSpec((tt, HEAD), lambda b, t: (t, b)),
                   pl.BlockSpec((None, None, 1, 2 * hl), lambda b, t: (b, t, 0, 0)),
                   pl.BlockSpec((tt, 2 * hl), lambda b, t: (t, b)),
                   pl.BlockSpec((tt, HEAD), lambda b, t: (t, b))],
        out_shape=[jax.ShapeDtypeStruct((s, d), F32), jax.ShapeDtypeStruct((nb, nt, 1, 2 * hl), F32),
                   jax.ShapeDtypeStruct((s, nb * 2 * hl), BF16), jax.ShapeDtypeStruct((s, d), BF16)],
        scratch_shapes=[pltpu.VMEM((tt, 2 * hl), F32), pltpu.VMEM((8, 2 * hl), F32)], comm=comm)


def _s5_bwd(u, dy, wb, wc, lt, dskip, cins, states, *, name, tt=1024, comm=()):
    s, d = u.shape
    nb = d // HEAD
    tt = min(tt, s)
    nt = s // tt
    hl = S5_HALF

    def body(u_ref, dy_ref, wb_ref, wc_ref, lt_ref, d_ref, cin_ref, st_ref,
             du_ref, dwb_ref, dwc_ref, dd_ref, dlam_ref, abuf, acarry):
        @pl.when(pl.program_id(1) == 0)
        def _():
            acarry[...] = jnp.zeros_like(acarry)
            dwb_ref[...] = jnp.zeros_like(dwb_ref)
            dwc_ref[...] = jnp.zeros_like(dwc_ref)
            dd_ref[...] = jnp.zeros_like(dd_ref)
            dlam_ref[...] = jnp.zeros_like(dlam_ref)

        uv, dyv = u_ref[...], dy_ref[...]
        dy16 = dyv.astype(BF16)
        abuf[...] = _bdot_raw(dy16, wc_ref[...], NT)
        ar, ai = _s5_scan(abuf, lt_ref, acarry[0:1, 0:hl], acarry[0:1, hl:2 * hl], tt, True)
        acarry[0:1, 0:hl] = ar
        acarry[0:1, hl:2 * hl] = ai
        adj16 = abuf[...].astype(BF16)
        du_ref[...] = _bdot_raw(adj16, wb_ref[...], NT) + d_ref[...] * dyv
        dwb_ref[...] += _bdot_raw(uv, adj16, TN)
        dwc_ref[...] += _bdot_raw(st_ref[...], dy16, TN)
        dd_ref[...] += jnp.sum(dyv * uv, axis=0, keepdims=True)
        first = lax.broadcasted_iota(jnp.int32, (tt, hl), 0) == 0
        spr = jnp.where(first, cin_ref[:, 0:hl], pltpu.roll(st_ref[:, 0:hl].astype(F32), 1, 0))
        spi = jnp.where(first, cin_ref[:, hl:2 * hl], pltpu.roll(st_ref[:, hl:2 * hl].astype(F32), 1, 0))
        avr, avi = abuf[:, 0:hl], abuf[:, hl:2 * hl]
        dlam_ref[:, 0:hl] += jnp.sum(avr * spr + avi * spi, axis=0, keepdims=True)
        dlam_ref[:, hl:2 * hl] += jnp.sum(avi * spr - avr * spi, axis=0, keepdims=True)

    rev = pl.BlockSpec((tt, HEAD), lambda b, t: (nt - 1 - t, b))
    return _call(
        body, (u, dy, wb, wc, lt, dskip, cins, states), name=name, grid=(nb, nt),
        in_specs=[rev, rev,
                  pl.BlockSpec((None, HEAD, 2 * hl), lambda b, t: (b, 0, 0)),
                  pl.BlockSpec((None, 2 * hl, HEAD), lambda b, t: (b, 0, 0)),
                  pl.BlockSpec((None, 16, 8, hl), lambda b, t: (b, 0, 0, 0)),
                  pl.BlockSpec((1, HEAD), lambda b, t: (0, b)),
                  pl.BlockSpec((None, None, 1, 2 * hl), lambda b, t: (b, nt - 1 - t, 0, 0)),
                  pl.BlockSpec((tt, 2 * hl), lambda b, t: (nt - 1 - t, b))],
        out_specs=[rev,
                   pl.BlockSpec((None, HEAD, 2 * hl), lambda b, t: (b, 0, 0)),
                   pl.BlockSpec((None, 2 * hl, HEAD), lambda b, t: (b, 0, 0)),
                   pl.BlockSpec((1, HEAD), lambda b, t: (0, b)),
                   pl.BlockSpec((None, 1, 2 * hl), lambda b, t: (b, 0, 0))],
        out_shape=[jax.ShapeDtypeStruct((s, d), F32), jax.ShapeDtypeStruct(wb.shape, F32),
                   jax.ShapeDtypeStruct(wc.shape, F32), jax.ShapeDtypeStruct((1, d), F32),
                   jax.ShapeDtypeStruct((nb, 1, 2 * hl), F32)],
        scratch_shapes=[pltpu.VMEM((tt, 2 * hl), F32), pltpu.VMEM((8, 2 * hl), F32)],
        comm=comm)


def _s5_pack(lr, li, br, bi, c_re, c_im):
    g = lr.shape[0]
    nb = g // S5_GB
    eye = jnp.eye(S5_GB, dtype=F32)
    bm = jnp.stack([br, bi]).reshape(2, nb, S5_GB, S5_STATE, S5_GROUP)
    wb = jnp.einsum("rbgpc,gh->bgcrhp", bm, eye).reshape(nb, HEAD, 2 * S5_HALF)
    cm = jnp.stack([c_re, -c_im]).reshape(2, nb, S5_GB, S5_GROUP, S5_STATE)
    wc = jnp.einsum("rbgcp,gh->brgphc", cm, eye).reshape(nb, 2 * S5_HALF, HEAD)
    pw = [(lr, li)]
    for _ in range(7):
        pr, pi = pw[-1]
        pw.append((pr * lr - pi * li, pr * li + pi * lr))
    blk = lambda a: a.reshape(nb, 1, S5_HALF)
    rows = jnp.arange(8).reshape(1, 8, 1)
    tables = []
    for conj, keep, order in ((1.0, lambda n: rows >= n, range(8)), (-1.0, lambda n: rows < 8 - n, range(7, -1, -1))):
        for n in (1, 2, 4):
            tables += [jnp.where(keep(n), blk(pw[n - 1][0]), 0.0), jnp.where(keep(n), conj * blk(pw[n - 1][1]), 0.0)]
        tables += [jnp.concatenate([blk(pw[n][0]) for n in order], axis=1),
                   jnp.concatenate([conj * blk(pw[n][1]) for n in order], axis=1)]
    return wb, wc, jnp.stack(tables, axis=1)


def _s5_unpack(dwb, dwc, dlam):
    nb = dwb.shape[0]
    g = nb * S5_GB
    eye = jnp.eye(S5_GB, dtype=F32)
    db = jnp.einsum("bgcrhp,gh->rbgpc", dwb.reshape(nb, S5_GB, S5_GROUP, 2, S5_GB, S5_STATE), eye)
    db = db.reshape(2, g, S5_STATE * S5_GROUP)
    dc = jnp.einsum("brgphc,gh->rbgcp", dwc.reshape(nb, 2, S5_GB, S5_STATE, S5_GB, S5_GROUP), eye)
    dc = dc.reshape(2, g, S5_GROUP, S5_STATE)
    dl = dlam.reshape(nb, 2, S5_GB, S5_STATE).transpose(1, 0, 2, 3).reshape(2, g, S5_STATE)
    return dl[0], dl[1], db[0], db[1], dc[0], -dc[1]


def _peer(r):
    mx, my, mc = lax.axis_index("x"), lax.axis_index("y"), lax.axis_index("c")
    px = 1 - mx if r & 4 else mx
    py = 1 - my if r & 2 else my
    pc = 1 - mc if r & 1 else mc
    return (px, py, pc), 4 * px + 2 * py + pc


_COMM_SCRATCH = [pltpu.SemaphoreType.DMA((N_DEV - 1,)), pltpu.SemaphoreType.DMA((N_DEV - 1,)), pltpu.SemaphoreType.DMA]


class _AllToAll:
    def __init__(self, x, rows=None):
        self.x = x
        self.rows = rows
        shape = x.shape if rows is None else (x.shape[0], rows[1]) + tuple(x.shape[2:])
        self.out_shape = jax.ShapeDtypeStruct(shape, x.dtype)

    def _copies(self, x_ref, out_ref, send_sems, recv_sems, local_sem):
        def block(j):
            return x_ref.at[j] if self.rows is None else x_ref.at[j, pl.ds(self.rows[0], self.rows[1])]

        _, me = _peer(0)
        mine = pltpu.make_async_copy(block(me), out_ref.at[me], local_sem)
        sends, recvs = [], []
        for r in range(1, N_DEV):
            pos, idx = _peer(r)
            sems = dict(send_sem=send_sems.at[r - 1], recv_sem=recv_sems.at[r - 1], device_id=pos, device_id_type=MESH)
            sends.append(pltpu.make_async_remote_copy(src_ref=block(idx), dst_ref=out_ref.at[me], **sems))
            recvs.append(pltpu.make_async_remote_copy(src_ref=block(idx), dst_ref=out_ref.at[idx], **sems))
        return mine, sends, recvs

    def start(self, *refs):
        mine, sends, _ = self._copies(*refs)
        mine.start()
        for cp in sends:
            cp.start()

    def finish(self, *refs):
        mine, sends, recvs = self._copies(*refs)
        for cp in recvs:
            cp.wait_recv()
        for cp in sends:
            cp.wait_send()
        mine.wait()


class _Gather:
    def __init__(self, x):
        self.x = x
        self.out_shape = jax.ShapeDtypeStruct((N_DEV,) + tuple(x.shape), x.dtype)

    def _copies(self, x_ref, out_ref, send_sems, recv_sems, local_sem):
        mx, my, mc = lax.axis_index("x"), lax.axis_index("y"), lax.axis_index("c")
        me, sibling = (mx, my, mc), (mx, my, 1 - mc)
        chips = [(1 - mx, my), (mx, 1 - my), (1 - mx, 1 - my)]

        def slot(px, py, pc):
            return out_ref.at[4 * px + 2 * py + pc]

        def copy(k, block, to, src=None):
            return pltpu.make_async_remote_copy(
                src_ref=slot(*block) if src is None else src, dst_ref=slot(*block),
                send_sem=send_sems.at[k], recv_sem=recv_sems.at[k], device_id=to, device_id_type=MESH)

        return dict(
            mine=pltpu.make_async_copy(x_ref, slot(*me), local_sem),
            first=[copy(0, me, sibling, src=x_ref)] + [copy(1 + j, me, (*chip, mc), src=x_ref) for j, chip in enumerate(chips)],
            passed=[copy(4 + j, (*chip, mc), sibling) for j, chip in enumerate(chips)],
            over_ici=[copy(1 + j, (*chip, mc), me) for j, chip in enumerate(chips)],
            from_sibling=[copy(0, sibling, me)] + [copy(4 + j, (*chip, 1 - mc), me) for j, chip in enumerate(chips)])

    def start(self, *refs):
        cps = self._copies(*refs)
        cps["mine"].start()
        for cp in cps["first"]:
            cp.start()

    def finish(self, *refs):
        cps = self._copies(*refs)
        for arrived, onward in zip(cps["over_ici"], cps["passed"]):
            arrived.wait_recv()
            onward.start()
        for cp in cps["from_sibling"]:
            cp.wait_recv()
        for cp in cps["first"] + cps["passed"]:
            cp.wait_send()
        cps["mine"].wait()


def _call(body, args, *, name, grid, in_specs, out_specs, out_shape, scratch_shapes=(), comm=()):
    n_in, n_out, n_scr, nc = len(in_specs), len(out_shape), len(scratch_shapes), len(comm)

    def wrapped(*refs):
        ins, c_in = refs[:n_in], refs[n_in:n_in + nc]
        outs, c_out = refs[n_in + nc:n_in + nc + n_out], refs[n_in + nc + n_out:n_in + 2 * nc + n_out]
        scr = refs[n_in + 2 * nc + n_out:n_in + 2 * nc + n_out + n_scr]
        sems = refs[n_in + 2 * nc + n_out + n_scr:]
        ids = [pl.program_id(a) for a in range(len(grid))]
        if nc:
            @pl.when(functools.reduce(operator.and_, [i == 0 for i in ids]))
            def _():
                for k, op in enumerate(comm):
                    op.start(c_in[k], c_out[k], *sems[3 * k:3 * k + 3])

        body(*ins, *outs, *scr)
        if nc:
            @pl.when(functools.reduce(operator.and_, [i == g - 1 for i, g in zip(ids, grid)]))
            def _():
                for k, op in enumerate(comm):
                    op.finish(c_in[k], c_out[k], *sems[3 * k:3 * k + 3])

    any_spec = pl.BlockSpec(memory_space=pl.ANY)
    res = pl.pallas_call(
        wrapped, name=name, grid=grid,
        in_specs=list(in_specs) + [any_spec] * nc, out_specs=list(out_specs) + [any_spec] * nc,
        out_shape=list(out_shape) + [op.out_shape for op in comm],
        scratch_shapes=list(scratch_shapes) + list(_COMM_SCRATCH) * nc,
        compiler_params=_params(len(grid)),
    )(*args, *[op.x for op in comm])
    return list(res[:n_out]), list(res[n_out:])


def _comm_call(op, *, name):
    return _call(lambda: None, (), name=name, grid=(1,), in_specs=[], out_specs=[], out_shape=[], comm=(op,))[1][0]


def _adamw(w, parts, m, v, *, name, tr=128, comm=()):
    nl, r, c = w.shape
    assert len(parts) == nl
    parts = [list(p) if isinstance(p, (list, tuple)) else [p] for p in parts]
    npart = parts[0][0].shape[0]
    tr = min(tr, r, *[pc.shape[1] for p in parts for pc in p])
    assert r % tr == 0 and all(pc.shape[1] % tr == 0 for p in parts for pc in p), (name, r, tr)
    pieces = []
    for l, p in enumerate(parts):
        first = 0
        for pc in p:
            pieces.append((l, first, pc.shape[1] // tr, pc))
            first += pc.shape[1] // tr
        assert first == r // tr, (name, l)

    def body(w_ref, m_ref, v_ref, *rest):
        p_refs, (g_ref, d_ref, mo_ref, vo_ref) = rest[:len(pieces)], rest[len(pieces):]
        layer, tile = pl.program_id(0), pl.program_id(1)
        for p_ref, (l, first, count, _) in zip(p_refs, pieces):
            @pl.when((layer == l) & (tile >= first) & (tile < first + count))
            def _():
                g = p_ref[0].astype(F32)
                for k in range(1, npart):
                    g = g + p_ref[k].astype(F32)
                m2 = ADAM_B1 * m_ref[...] + (1.0 - ADAM_B1) * g
                v2 = ADAM_B2 * v_ref[...] + (1.0 - ADAM_B2) * (g * g)
                m_hat = m2 / (1.0 - ADAM_B1 ** ADAM_STEP)
                v_hat = v2 / (1.0 - ADAM_B2 ** ADAM_STEP)
                g_ref[...] = g
                d_ref[...] = -ADAM_LR * (m_hat / (jnp.sqrt(v_hat) + ADAM_EPS) + ADAM_WD * w_ref[...])
                mo_ref[...] = m2
                vo_ref[...] = v2

    blk = pl.BlockSpec((None, tr, c), lambda l, i: (l, i, 0))

    def part_spec(l, first, count):
        return pl.BlockSpec((npart, tr, c), lambda ll, i: (0, jnp.where(ll == l, jnp.clip(i - first, 0, count - 1), 0), 0))

    outs, exchanged = _call(
        body, (w, m, v, *[pc for _, _, _, pc in pieces]), name=name, grid=(nl, r // tr),
        in_specs=[blk, blk, blk] + [part_spec(l, first, count) for l, first, count, _ in pieces],
        out_specs=[blk] * 4, out_shape=[jax.ShapeDtypeStruct((nl, r, c), F32)] * 4, comm=comm)
    return (outs, exchanged) if comm else outs


def _sum_parts(parts, *, name):
    npart = parts.shape[0]

    def body(p_ref, o_ref):
        g = p_ref[0]
        for k in range(1, npart):
            g = g + p_ref[k]
        o_ref[...] = g

    return pl.pallas_call(
        body, name=name, grid=(1,), in_specs=[_full_spec(parts)],
        out_specs=pl.BlockSpec(parts.shape[1:], lambda i: (0, 0)),
        out_shape=jax.ShapeDtypeStruct(parts.shape[1:], F32), compiler_params=_params(1),
    )(parts)


def _pack(arrs):
    blocks = []
    for a in arrs:
        flat = a.reshape(-1).astype(F32)
        blocks.append(jnp.pad(flat, (0, (-flat.shape[0]) % (8 * HEAD))).reshape(-1, HEAD))
    out = jnp.concatenate(blocks, axis=0)
    return jnp.pad(out, ((0, (-out.shape[0]) % HEAD), (0, 0)))


def _unpack(packed, shapes):
    out, off = [], 0
    for shp in shapes:
        size = math.prod(shp)
        rows = -(-size // (8 * HEAD)) * 8
        out.append(packed[off:off + rows].reshape(-1)[:size].reshape(shp))
        off += rows
    return out


def _add_epilogue(acc, res):
    return (acc + res,)


def _relu2_epilogue(acc):
    r = jnp.maximum(acc, 0.0)
    return acc, r * r


def _ple_epilogue(acc, gpre, h):
    return h + jax.nn.sigmoid(gpre) * acc, acc


def kernel(x, p, norm_mix, norm_mlp, norm_ple, w_in_e, w_out_e, hgrn_lb, g_norm_a, conv_w, a_log, dt_bias, g_norm_b, s5_a_re, s5_a_im, s5_b_re, s5_b_im, s5_c_re, s5_c_im, s5_d, s5_log_dt, w_glu, b_glu, w_out_o, w_up, w_down, w_ple_gate, w_ple_proj, final_norm, loss_target, m_norm_mix, m_norm_mlp, m_norm_ple, m_w_in_e, m_w_out_e, m_hgrn_lb, m_g_norm_a, m_conv_w, m_a_log, m_dt_bias, m_g_norm_b, m_s5_a_re, m_s5_a_im, m_s5_b_re, m_s5_b_im, m_s5_c_re, m_s5_c_im, m_s5_d, m_s5_log_dt, m_w_glu, m_b_glu, m_w_out_o, m_w_up, m_w_down, m_w_ple_gate, m_w_ple_proj, m_final_norm, v_norm_mix, v_norm_mlp, v_norm_ple, v_w_in_e, v_w_out_e, v_hgrn_lb, v_g_norm_a, v_conv_w, v_a_log, v_dt_bias, v_g_norm_b, v_s5_a_re, v_s5_a_im, v_s5_b_re, v_s5_b_im, v_s5_c_re, v_s5_c_im, v_s5_d, v_s5_log_dt, v_w_glu, v_b_glu, v_w_out_o, v_w_up, v_w_down, v_w_ple_gate, v_w_ple_proj, v_final_norm):
    args = dict(locals())
    s, d = x.shape[1], x.shape[2]
    aw = d // 2
    ha = hb = aw // HEAD
    main = 4 * d
    z_col = 2 * d + 3 * aw
    ff = w_up.shape[2] * N_DEV
    ple = p.shape[-1]
    groups = d // S5_GROUP
    me = 4 * lax.axis_index("x") + 2 * lax.axis_index("y") + lax.axis_index("c")
    x2, target = x[0], loss_target[0]
    row = lambda a, i: a[i:i + 1]

    def gather_of(w):
        return _Gather(w.astype(BF16))

    w_in = jnp.transpose(_comm_call(gather_of(w_in_e[0]), name="ag_w_in"), (1, 0, 2)).reshape(d, -1)
    w_main = w_in[:, :main]
    w_tail = jnp.pad(w_in[:, main:], ((0, 0), (0, HEAD - 2 * hb)))

    lb_rows = [row(hgrn_lb, 0), row(hgrn_lb, 1), row(hgrn_lb, 2)]
    (lb0,) = _small_call(_lb0_stage, lb_rows, name="f_lb0")
    hp = jnp.zeros((8, HEAD), F32).at[0, :hb].set(a_log[0]).at[1, :hb].set(dt_bias[0])
    expand = jnp.asarray(np.kron(np.eye(S5_STATE, dtype=np.float32), np.ones((1, S5_GROUP), np.float32)))
    prep_in = [s5_a_re[0], s5_a_im[0], s5_log_dt[0].reshape(groups, 1),
               s5_b_re[0].reshape(groups, -1), s5_b_im[0].reshape(groups, -1), expand]
    lr, li, br, bi = _small_call(_s5_prep_stage, prep_in, name="f_s5_prep")
    wb, wc, lt = _s5_pack(lr, li, br, bi, s5_c_re[0], s5_c_im[0])
    wb, wc = wb.astype(BF16), wc.astype(BF16)
    fnorm = final_norm.reshape(1, d)

    w_upg = []

    def block_fwd(h, l):
        hn = _rows_call(_rms_stage, [h], [row(norm_mlp, l)], [BF16], name=f"f_norm_mlp{l}")
        (up, act), (dn8,) = _mm(hn, w_upg[l], epilogue=_relu2_epilogue, out_dtypes=(F32, BF16), name=f"f_up{l}",
                                comm=(gather_of(w_down[l]),))
        w_dn = dn8.reshape(ff, d)
        h2 = _mm(act, w_dn, extras=(h,), epilogue=_add_epilogue, name=f"f_down{l}")
        hq = _rows_call(_rms_stage, [h2], [row(norm_ple, l)], [BF16], name=f"f_norm_ple{l}")
        gpre = _mm(hq, w_pgg[l], name=f"f_ple_gate{l}")
        h3, pp = _mm(p[l, 0], w_ppg[l], extras=(gpre, h2), epilogue=_ple_epilogue, out_dtypes=(F32, F32),
                     name=f"f_ple_proj{l}")
        return h3, dict(h=h, hn=hn, up=up, act=act, h2=h2, hq=hq, gpre=gpre, pp=pp, w_dn=w_dn)

    hn0 = _rows_call(_rms_stage, [x2], [row(norm_mix, 0)], [BF16], name="f_norm_mix0")
    shard_shapes = [conv_w[0].shape, s5_d.shape, b_glu.shape]
    proj, (oe8, pg8, small) = _mm(hn0, w_main, name="f_proj", comm=(
        gather_of(w_out_e[0]), gather_of(w_ple_gate), _Gather(_pack([conv_w[0], s5_d, b_glu]))))
    w_oe = oe8.reshape(d, d)
    w_top, w_bot = w_oe[:aw], w_oe[aw:]
    w_pgg = jnp.transpose(pg8, (1, 0, 2, 3)).reshape(2, d, d)
    conv_g, s5d_g, bglu_g = zip(*[_unpack(small[j], shard_shapes) for j in range(N_DEV)])
    conv_full = jnp.concatenate(conv_g, axis=1)
    s5d_full = jnp.concatenate(s5d_g, axis=1)
    bglu_full = jnp.concatenate(bglu_g, axis=1)
    ab = _mm(hn0, w_tail, name="f_ab")
    (oa, st_a), (gl8, oo8) = _hgrn_fwd(proj, lb0, g_norm_a, heads=ha, name="f_hgrn",
                                       comm=(gather_of(w_glu[0]), gather_of(w_out_o[0])))
    w_gl, w_oo = gl8.reshape(d, d), oo8.reshape(d, d)
    qkv = _conv_fwd(proj, conv_full, col_off=2 * d, name="f_conv")
    slots_to_cols = lambda g8: jnp.transpose(g8, (1, 0, 2)).reshape(g8.shape[1], -1)
    (ob, st_b), (up8,) = _delta_fwd(qkv, ab, proj, hp, g_norm_b, heads=hb, z_off=z_col // HEAD, name="f_delta",
                                    comm=(gather_of(w_up[0]),))
    w_upg.append(slots_to_cols(up8))
    h1, (pp8,) = _mm(oa, w_top, extras=(x2,), epilogue=_add_epilogue, name="f_out_a", comm=(gather_of(w_ple_proj),))
    w_ppg = jnp.transpose(pp8, (1, 2, 0, 3)).reshape(2, ple, d)
    h1 = _mm(ob, w_bot, extras=(h1,), epilogue=_add_epilogue, name="f_out_b")
    h3, sv0 = block_fwd(h1, 0)

    u = _rows_call(_rms_stage, [h3], [row(norm_mix, 1)], [F32], name="f_norm_mix1")
    (y, cins, s5_states, act_g), (up8,) = _s5_fwd(u, wb, wc, lt, s5d_full, name="f_s5", comm=(gather_of(w_up[1]),))
    w_upg.append(slots_to_cols(up8))
    gl_raw = _mm(act_g, w_gl, name="f_glu")
    glu = _rows_call(_glu_stage, [y, gl_raw], [bglu_full], [BF16], name="f_glu_gate")
    h4 = _mm(glu, w_oo, extras=(h3,), epilogue=_add_epilogue, name="f_out_o")
    h6, sv1 = block_fwd(h4, 1)
    dh, d_fnorm, loss8 = _loss_call(h6, fnorm, target, name="loss")
    loss = lax.psum(loss8[0, 0], ("x", "y", "c"))

    dshard, ffs, cols = d // N_DEV, ff // N_DEV, w_in_e.shape[2]
    rows8 = lambda g: _AllToAll(g.reshape(N_DEV, -1, g.shape[-1]))
    col_slots = lambda g: jnp.transpose(g.reshape(g.shape[0], N_DEV, -1), (1, 0, 2))
    cols8 = lambda g: _AllToAll(col_slots(g))

    def halves(g8):
        r = g8.shape[1] // 2
        return _AllToAll(g8, rows=(0, r)), _AllToAll(g8, rows=(r, r))

    def block_bwd(dh3, l, sv, carried=(), carried_wup=(), carried_up=()):
        (dgpre, dpp), _ = _rows_vjp(_ple_stage, [sv["h2"], sv["gpre"], sv["pp"]], [], [dh3],
                                    row_grads={1: BF16, 2: BF16}, name=f"b_ple{l}")
        g_pp = _mm(p[l, 0], dpp, ta=True, out_dtypes=(BF16,), name=f"b_w_ple_proj{l}")
        g_pg = _mm(sv["hq"], dgpre, ta=True, out_dtypes=(BF16,), name=f"b_w_ple_gate{l}")
        dhq = _mm(dgpre, w_pgg[l], tb=True, name=f"b_ple_gate{l}")
        (dh2, dh2_mm), (g_nple,) = _rows_vjp(_rms_stage, [sv["h2"]], [row(norm_ple, l)], [dhq],
                                             row_grads={0: (F32, MM_DTYPE)}, adds={0: dh3}, name=f"b_norm_ple{l}")
        dup, (r_pg, r_pp) = _mm(dh2_mm, sv["w_dn"], tb=True, extras=(sv["up"],), epilogue=_relu2_grad_epilogue,
                                out_dtypes=(BF16,), name=f"b_down{l}", comm=(rows8(g_pg), cols8(g_pp)))
        g_dn = _mm(sv["act"], dh2_mm, ta=True, out_dtypes=(BF16,), name=f"b_w_down{l}", comm=carried)
        g_dn, r_carried = g_dn if carried else (g_dn, [])
        g_up = _mm(sv["hn"], dup, ta=True, out_dtypes=(BF16,), tn=ffs, out_slots=True, name=f"b_w_up{l}",
                   comm=carried_wup)
        g_up, r_carried_wup = g_up if carried_wup else (g_up, [])
        dhn = _mm(dup, w_upg[l], tb=True, name=f"b_up{l}", comm=carried_up)
        dhn, r_carried_up = dhn if carried_up else (dhn, [])
        (dh0, dh0_mm), (g_nmlp,) = _rows_vjp(_rms_stage, [sv["h"]], [row(norm_mlp, l)], [dhn],
                                             row_grads={0: (F32, MM_DTYPE)}, adds={0: dh2}, name=f"b_norm_mlp{l}")
        return dh0, dict(w_ple_proj=r_pp, w_ple_gate=r_pg, norm_ple=g_nple, w_down=g_dn, w_up=g_up, norm_mlp=g_nmlp,
                         carried=r_carried, carried_wup=r_carried_wup, carried_up=r_carried_up, dh_mm=dh0_mm)

    dh4, gb1 = block_bwd(dh, 1, sv1)
    up1_a, up1_b = halves(gb1["w_up"])
    dglu = _mm(gb1["dh_mm"], w_oo, tb=True, name="b_out_o")
    g_oo = _mm(glu, gb1["dh_mm"], ta=True, out_dtypes=(BF16,), name="b_w_out_o")
    (dy1, dgl), (g_bglu,) = _rows_vjp(_glu_stage, [y, gl_raw], [bglu_full], [dglu], row_grads={0: F32, 1: BF16},
                                      name="b_glu_gate")
    g_gl = _mm(act_g, dgl, ta=True, out_dtypes=(BF16,), name="b_w_glu")
    dact = _mm(dgl, w_gl, tb=True, name="b_glu")
    (dy,), _ = _rows_vjp(_gelu_stage, [y], [], [dact], row_grads={0: F32}, adds={0: dy1}, name="b_gelu")
    (du, dwb, dwc, g_s5d, dlam), (r_dn1, r_up1_a) = _s5_bwd(u, dy, wb, wc, lt, s5d_full, cins, s5_states, name="b_s5",
                                                           comm=(rows8(gb1["w_down"]), up1_a))
    (dh3,), (g_nmix1,) = _rows_vjp(_rms_stage, [h3], [row(norm_mix, 1)], [du], row_grads={0: F32}, adds={0: dh4},
                                   name="b_norm_mix1")
    dlr, dli, dbr, dbi, g_cre, g_cim = _s5_unpack(dwb, dwc, dlam)
    g_are, g_aim, g_ldt, g_bre, g_bim, _ = _small_vjp(_s5_prep_stage, prep_in, [dlr, dli, dbr, dbi], name="b_s5_prep")

    early_grads = dict(
        s5_a_re=g_are[None], s5_a_im=g_aim[None], s5_b_re=g_bre.reshape(s5_b_re.shape),
        s5_b_im=g_bim.reshape(s5_b_im.shape), s5_c_re=g_cre[None], s5_c_im=g_cim[None],
        s5_log_dt=g_ldt.reshape(1, groups), final_norm=d_fnorm.reshape(d), s5_d=g_s5d, b_glu=g_bglu)
    dh1, gb0 = block_bwd(dh3, 0, sv0, (rows8(g_oo), rows8(g_gl)), (up1_b,),
                         (_Gather(_pack(list(early_grads.values()))),))
    r_oo, r_gl = gb0["carried"]
    r_up1 = [r_up1_a, gb0["carried_wup"][0]]
    (early_parts,) = gb0["carried_up"]
    dn0_a, dn0_b = halves(gb0["w_down"].reshape(N_DEV, ffs, d))
    up0_a, up0_b = halves(gb0["w_up"])
    dh1_mm = gb0["dh_mm"]
    doa = _mm(dh1_mm, w_top, tb=True, name="b_out_a")
    dob = _mm(dh1_mm, w_bot, tb=True, name="b_out_b")
    g_oe = jnp.concatenate([_mm(oa, dh1_mm, ta=True, out_dtypes=(BF16,), name="b_w_out_a"),
                            _mm(ob, dh1_mm, ta=True, out_dtypes=(BF16,), name="b_w_out_b")], axis=0)
    (dq, df, di, dg, dlb, g_gna), (r_dn0_a,) = _hgrn_bwd(proj, lb0, g_norm_a, st_a, doa, heads=ha, name="b_hgrn",
                                                        comm=(dn0_a,))
    (dqb, dkb, dvb, dab, dz, dhp, g_gnb), (r_dn0_b, r_up0_a) = _delta_bwd(
        qkv, ab, proj, hp, g_norm_b, st_b, dob, heads=hb, z_off=z_col // HEAD, name="b_delta", comm=(dn0_b, up0_a))
    (dqkv, g_conv), (r_oe,) = _conv_bwd(proj, conv_full, jnp.concatenate([dqb, dkb, dvb], axis=1), col_off=2 * d,
                                        name="b_conv", comm=(rows8(g_oe),))
    dproj = jnp.concatenate([dq, df, di, dg, dqkv, dz], axis=1)
    g_main, (r_up0_b,) = _mm(hn0, dproj, ta=True, out_dtypes=(BF16,), name="b_w_proj", comm=(up0_b,))
    r_dn0, r_up0 = [r_dn0_a, r_dn0_b], [r_up0_a, r_up0_b]
    g_tail = _mm(hn0, dab, ta=True, out_dtypes=(BF16,), name="b_w_ab")
    in_a, in_b = halves(col_slots(jnp.concatenate([g_main, g_tail[:, :2 * hb]], axis=1)))
    dhn0, (r_in_a,) = _mm(dproj, w_main, tb=True, name="b_proj", comm=(in_a,))
    dhn0 = _mm(dab, w_tail, tb=True, extras=(dhn0,), epilogue=_add_epilogue, name="b_ab")
    (dx,), (g_nmix0,) = _rows_vjp(_rms_stage, [x2], [row(norm_mix, 0)], [dhn0], row_grads={0: F32}, adds={0: dh1},
                                  name="b_norm_mix0")
    g_lb = jnp.concatenate(_small_vjp(_lb0_stage, lb_rows, [dlb], name="b_lb0"), axis=0)

    late_grads = dict(
        norm_mix=jnp.concatenate([g_nmix0, g_nmix1], axis=0),
        norm_mlp=jnp.concatenate([gb0["norm_mlp"], gb1["norm_mlp"]], axis=0),
        norm_ple=jnp.concatenate([gb0["norm_ple"], gb1["norm_ple"]], axis=0),
        hgrn_lb=g_lb, g_norm_a=g_gna, a_log=dhp[0:1, :hb], dt_bias=dhp[1:2, :hb], g_norm_b=g_gnb, conv_w=g_conv)
    rep_names = ["norm_mix", "norm_mlp", "norm_ple", "hgrn_lb", "g_norm_a", "a_log", "dt_bias", "g_norm_b", "s5_a_re",
                 "s5_a_im", "s5_b_re", "s5_b_im", "s5_c_re", "s5_c_im", "s5_log_dt", "final_norm"]
    late_parts = _comm_call(_Gather(_pack(list(late_grads.values()))), name="ag_small_grads")
    summed = {}
    for tag, grads, parts in (("early", early_grads, early_parts), ("late", late_grads, late_parts)):
        sums = _unpack(_sum_parts(parts, name=f"sum_small_grads_{tag}"), [g.shape for g in grads.values()])
        summed.update(zip(grads, sums))
    cw = conv_w.shape[2]
    dshard = d // N_DEV
    shard_g = dict(conv_w=lax.dynamic_slice(summed["conv_w"], (0, me * cw), (CONV_WIDTH, cw))[None],
                   s5_d=lax.dynamic_slice(summed["s5_d"], (0, me * dshard), (1, dshard)),
                   b_glu=lax.dynamic_slice(summed["b_glu"], (0, me * dshard), (1, dshard)))
    small_names = rep_names + ["conv_w", "s5_d", "b_glu"]
    g_small = [summed[k] if k in rep_names else shard_g[k] for k in small_names]
    shapes = [args[k].shape for k in small_names]
    sm_out = _adamw(_pack([args[k] for k in small_names])[None], [_pack(g_small)[None]],
                    _pack([args["m_" + k] for k in small_names])[None], _pack([args["v_" + k] for k in small_names])[None],
                    name="adamw_small")
    sm_out = [dict(zip(small_names, _unpack(o[0], shapes))) for o in sm_out]

    up_out, (r_in_b,) = _adamw(w_up, [r_up0, r_up1], m_w_up, v_w_up, name="adamw_w_up", comm=(in_b,))
    received = dict(w_in_e=[[r_in_a, r_in_b]], w_out_e=[r_oe], w_glu=[r_gl], w_out_o=[r_oo],
                    w_down=[r_dn0, r_dn1], w_ple_gate=[gb0["w_ple_gate"], gb1["w_ple_gate"]],
                    w_ple_proj=[gb0["w_ple_proj"], gb1["w_ple_proj"]])
    big_out = {k: _adamw(args[k], layers, args["m_" + k], args["v_" + k], name="adamw_" + k)
               for k, layers in received.items()}
    big_out["w_up"] = up_out

    names = ["norm_mix", "norm_mlp", "norm_ple", "w_in_e", "w_out_e", "hgrn_lb", "g_norm_a", "conv_w", "a_log", "dt_bias",
             "g_norm_b", "s5_a_re", "s5_a_im", "s5_b_re", "s5_b_im", "s5_c_re", "s5_c_im", "s5_d", "s5_log_dt", "w_glu",
             "b_glu", "w_out_o", "w_up", "w_down", "w_ple_gate", "w_ple_proj", "final_norm"]
    result = [loss, dx[None]]
    for j in range(4):
        result += [big_out[k][j] if k in big_out else sm_out[j][k] for k in names]
    return tuple(result)
```

```python
import functools
import math
import operator

import numpy as np
import jax
import jax.numpy as jnp
from jax import lax
from jax.experimental import pallas as pl
from jax.experimental.pallas import tpu as pltpu

F32 = jnp.float32
BF16 = jnp.bfloat16
MM_DTYPE = BF16
HI = lax.Precision.HIGHEST
MESH = pl.DeviceIdType.MESH

NORM_EPS = 1e-6
CHUNK = 64
HEAD = 128
CONV_WIDTH = 4
S5_GROUP = 16
S5_STATE = 64
S5_GB = 8
S5_HALF = S5_GB * S5_STATE
N_DEV = 8
HEADS_PER_STEP = 8
HGRN_SUB = 16
ADAM_LR, ADAM_B1, ADAM_B2, ADAM_EPS, ADAM_WD, ADAM_STEP = 0.001, 0.9, 0.999, 1e-08, 0.01, 10
VMEM_LIMIT = 56 * 1024 * 1024

NN = (((1,), (0,)), ((), ()))
NT = (((1,), (1,)), ((), ()))
TN = (((0,), (0,)), ((), ()))


def _dot(a, b, dn=NN):
    return lax.dot_general(a, b, dn, precision=HI, preferred_element_type=F32)


def _hdot(a, b, dn=NN):
    return lax.dot_general(a, b, dn, precision=lax.Precision.HIGH, preferred_element_type=F32)


def _bdot_raw(a, b, dn=NN):
    return lax.dot_general(a.astype(BF16), b.astype(BF16), dn, preferred_element_type=F32)


@functools.partial(jax.custom_vjp, nondiff_argnums=(2,))
def _bdot(a, b, dn):
    return _bdot_raw(a, b, dn)


def _bdot_fwd(a, b, dn):
    return _bdot_raw(a, b, dn), (a, b)


def _bdot_bwd(dn, res, g):
    a, b = res
    if dn == NN:
        return _bdot_raw(g, b, NT), _bdot_raw(a, g, TN)
    if dn == NT:
        return _bdot_raw(g, b, NN), _bdot_raw(g, a, TN)
    assert dn == TN
    return _bdot_raw(b, g, NT), _bdot_raw(a, g, NN)


_bdot.defvjp(_bdot_fwd, _bdot_bwd)


def _per_head(f):
    def g(*args, **kw):
        n = [len(a.vals) for a in args if isinstance(a, _Heads)]
        if not n:
            return f(*args, **kw)
        return _Heads([f(*[a.vals[j] if isinstance(a, _Heads) else a for a in args], **kw) for j in range(n[0])])
    return g


class _Heads:
    def __init__(self, vals):
        self.vals = list(vals)

    def __add__(self, o):
        return _per_head(operator.add)(self, o)

    def __radd__(self, o):
        return _per_head(operator.add)(o, self)

    def __sub__(self, o):
        return _per_head(operator.sub)(self, o)

    def __rsub__(self, o):
        return _per_head(operator.sub)(o, self)

    def __mul__(self, o):
        return _per_head(operator.mul)(self, o)

    def __rmul__(self, o):
        return _per_head(operator.mul)(o, self)

    def __neg__(self):
        return _per_head(operator.neg)(self)


_exp, _log, _where, _sum, _mean = (_per_head(f) for f in (jnp.exp, jnp.log, jnp.where, jnp.sum, jnp.mean))
_sigmoid, _rsqrt, _equal = _per_head(jax.nn.sigmoid), _per_head(lax.rsqrt), _per_head(operator.eq)
_hdot_h, _bdot_h = _per_head(_hdot), _per_head(_bdot)
_rows = _per_head(lambda a, lo, n: a[lo:lo + n, :])
_row_concat = _per_head(lambda *xs: jnp.concatenate(xs, axis=0))


def _params(n_axes):
    return pltpu.CompilerParams(dimension_semantics=("arbitrary",) * n_axes, vmem_limit_bytes=VMEM_LIMIT)


def _full_spec(a):
    nd = a.ndim
    return pl.BlockSpec(a.shape, lambda *_: (0,) * nd)


def _mm(a, b, *, name, ta=False, tb=False, extras=(), epilogue=None, out_dtypes=(F32,), tm=1024, tn=1024, tk=2048,
        out_slots=False, comm=()):
    m = a.shape[1] if ta else a.shape[0]
    k = a.shape[0] if ta else a.shape[1]
    n = b.shape[0] if tb else b.shape[1]
    assert k == (b.shape[1] if tb else b.shape[0]), (name, a.shape, b.shape)
    tm, tn, tk = min(tm, m), min(tn, n), min(tk, k)
    assert m % tm == 0 and n % tn == 0 and k % tk == 0, (name, m, n, k)
    nk = k // tk
    n_ex, n_out = len(extras), len(out_dtypes)
    dn = (((0 if ta else 1,), (1 if tb else 0,)), ((), ()))

    def body(a_ref, b_ref, *rest):
        ex_refs, out_refs = rest[:n_ex], rest[n_ex:n_ex + n_out]
        part = lax.dot_general(a_ref[...].astype(MM_DTYPE), b_ref[...].astype(MM_DTYPE), dn, preferred_element_type=F32)

        def finish(acc):
            outs = epilogue(acc, *[r[...] for r in ex_refs]) if epilogue is not None else (acc,)
            for o_ref, o in zip(out_refs, outs):
                o_ref[...] = o.astype(o_ref.dtype)

        if nk == 1:
            finish(part)
            return
        acc_ref = rest[-1]
        kk = pl.program_id(2)

        @pl.when(kk == 0)
        def _():
            acc_ref[...] = part

        @pl.when((kk > 0) & (kk < nk - 1))
        def _():
            acc_ref[...] += part

        @pl.when(kk == nk - 1)
        def _():
            finish(acc_ref[...] + part)

    a_spec = pl.BlockSpec((tk, tm), lambda i, j, q: (q, i)) if ta else pl.BlockSpec((tm, tk), lambda i, j, q: (i, q))
    b_spec = pl.BlockSpec((tn, tk), lambda i, j, q: (j, q)) if tb else pl.BlockSpec((tk, tn), lambda i, j, q: (q, j))
    ex_specs = []
    for e in extras:
        if e.shape[0] == 1 and m != 1:
            ex_specs.append(pl.BlockSpec((1, tn), lambda i, j, q: (0, j)))
        else:
            ex_specs.append(pl.BlockSpec((tm, tn), lambda i, j, q: (i, j)))
    if out_slots:
        out_spec, out_dims = pl.BlockSpec((None, tm, tn), lambda i, j, q: (j, i, 0)), (n // tn, m, tn)
    else:
        out_spec, out_dims = pl.BlockSpec((tm, tn), lambda i, j, q: (i, j)), (m, n)
    outs, exchanged = _call(
        body, (a, b, *extras), name=name, grid=(m // tm, n // tn, nk),
        in_specs=[a_spec, b_spec] + ex_specs,
        out_specs=[out_spec for _ in out_dtypes],
        out_shape=[jax.ShapeDtypeStruct(out_dims, dt) for dt in out_dtypes],
        scratch_shapes=[pltpu.VMEM((tm, tn), F32)] if nk > 1 else [], comm=comm)
    outs = outs[0] if n_out == 1 else tuple(outs)
    return (outs, exchanged) if comm else outs


def _rows_call(fn, rows, consts, out_dtypes, *, name, tr=256):
    s = rows[0].shape[0]
    tr = min(tr, s)
    nr, nc = len(rows), len(consts)
    widths = [o.shape[1] for o in jax.eval_shape(
        fn, *[jax.ShapeDtypeStruct((tr, r.shape[1]), F32) for r in rows],
        *[jax.ShapeDtypeStruct(c.shape, F32) for c in consts])]

    def body(*refs):
        rv = [r[...].astype(F32) for r in refs[:nr]]
        cv = [c[...] for c in refs[nr:nr + nc]]
        for o_ref, o in zip(refs[nr + nc:], fn(*rv, *cv)):
            o_ref[...] = o.astype(o_ref.dtype)

    outs = pl.pallas_call(
        body, name=name, grid=(s // tr,),
        in_specs=[pl.BlockSpec((tr, r.shape[1]), lambda i: (i, 0)) for r in rows] + [_full_spec(c) for c in consts],
        out_specs=[pl.BlockSpec((tr, w), lambda i: (i, 0)) for w in widths],
        out_shape=[jax.ShapeDtypeStruct((s, w), dt) for w, dt in zip(widths, out_dtypes)],
        compiler_params=_params(1),
    )(*rows, *consts)
    return outs[0] if len(outs) == 1 else tuple(outs)


def _rows_vjp(fn, rows, consts, cots, *, name, row_grads, adds=None, tr=256):
    adds = adds or {}
    s = rows[0].shape[0]
    tr = min(tr, s)
    nr, nc, nt = len(rows), len(consts), len(cots)
    rg = [(i, dt) for i in sorted(row_grads)
          for dt in (row_grads[i] if isinstance(row_grads[i], tuple) else (row_grads[i],))]
    ad = sorted(adds)

    def body(*refs):
        rv = [r[...].astype(F32) for r in refs[:nr]]
        cv = [c[...] for c in refs[nr:nr + nc]]
        ct = [c[...].astype(F32) for c in refs[nr + nc:nr + nc + nt]]
        av = {i: r[...].astype(F32) for i, r in zip(ad, refs[nr + nc + nt:nr + nc + nt + len(ad)])}
        out_refs = refs[nr + nc + nt + len(ad):]
        _, vjp = jax.vjp(fn, *rv, *cv)
        grads = vjp(tuple(ct))
        for o_ref, (i, _) in zip(out_refs[:len(rg)], rg):
            g = grads[i]
            if i in av:
                g = g + av[i]
            o_ref[...] = g.astype(o_ref.dtype)

        @pl.when(pl.program_id(0) == 0)
        def _():
            for o_ref in out_refs[len(rg):]:
                o_ref[...] = jnp.zeros_like(o_ref)

        for o_ref, g in zip(out_refs[len(rg):], grads[nr:]):
            o_ref[...] += g

    row_spec = lambda a: pl.BlockSpec((tr, a.shape[1]), lambda i: (i, 0))
    outs = pl.pallas_call(
        body, name=name, grid=(s // tr,),
        in_specs=[row_spec(r) for r in rows] + [_full_spec(c) for c in consts] + [row_spec(c) for c in cots]
        + [row_spec(adds[i]) for i in ad],
        out_specs=[row_spec(rows[i]) for i, _ in rg] + [_full_spec(c) for c in consts],
        out_shape=[jax.ShapeDtypeStruct(rows[i].shape, dt) for i, dt in rg]
        + [jax.ShapeDtypeStruct(c.shape, F32) for c in consts],
        compiler_params=_params(1),
    )(*rows, *consts, *cots, *[adds[i] for i in ad])
    return list(outs[:len(rg)]), list(outs[len(rg):])


def _small_call(fn, ins, *, name):
    shapes = jax.eval_shape(fn, *[jax.ShapeDtypeStruct(a.shape, F32) for a in ins])

    def body(*refs):
        for o_ref, o in zip(refs[len(ins):], fn(*[r[...] for r in refs[:len(ins)]])):
            o_ref[...] = o

    return pl.pallas_call(
        body, name=name, in_specs=[_full_spec(a) for a in ins],
        out_specs=[pl.BlockSpec(o.shape, functools.partial(lambda nd, *_: (0,) * nd, len(o.shape))) for o in shapes],
        out_shape=[jax.ShapeDtypeStruct(o.shape, F32) for o in shapes], grid=(1,),
        compiler_params=_params(1),
    )(*ins)


def _small_vjp(fn, ins, cots, *, name):
    def body(*refs):
        vals = [r[...] for r in refs[:len(ins)]]
        ct = [r[...] for r in refs[len(ins):len(ins) + len(cots)]]
        _, vjp = jax.vjp(fn, *vals)
        for o_ref, g in zip(refs[len(ins) + len(cots):], vjp(tuple(ct))):
            o_ref[...] = g

    return pl.pallas_call(
        body, name=name, in_specs=[_full_spec(a) for a in ins] + [_full_spec(c) for c in cots],
        out_specs=[_full_spec(a) for a in ins],
        out_shape=[jax.ShapeDtypeStruct(a.shape, F32) for a in ins], grid=(1,),
        compiler_params=_params(1),
    )(*ins, *cots)


def _rms(x, g):
    return x * _rsqrt(_mean(x * x, axis=-1, keepdims=True) + NORM_EPS) * g


def _rms_stage(x, g):
    return (_rms(x, g),)


def _silu(x):
    return x * _sigmoid(x)


def _softplus(x):
    return jnp.maximum(x, 0.0) + jnp.log1p(jnp.exp(-jnp.abs(x)))


def _gelu(x):
    return jax.nn.gelu(x, approximate=True)


def _gelu_stage(y):
    return (_gelu(y),)


def _glu_stage(y, gl_raw, b):
    return (_gelu(y) * jax.nn.sigmoid(gl_raw + b),)


def _ple_stage(h, gpre, pp):
    return (h + jax.nn.sigmoid(gpre) * pp,)


def _relu2_grad_epilogue(acc, up):
    return (acc * (2.0 * jnp.maximum(up, 0.0)),)


def _lb0_stage(x0, x1, x2):
    mx = jnp.maximum(jnp.maximum(x0, x1), x2)
    e0, e1, e2 = jnp.exp(x0 - mx), jnp.exp(x1 - mx), jnp.exp(x2 - mx)
    return (e0 / (e0 + e1 + e2),)


def _s5_prep_stage(a_re, a_im, log_dt, b_re, b_im, expand):
    step = jnp.exp(log_dt)
    mag = jnp.exp(a_re * step)
    lr = mag * jnp.cos(a_im * step)
    li = mag * jnp.sin(a_im * step)
    den = a_re * a_re + a_im * a_im
    cr = ((lr - 1.0) * a_re + li * a_im) / den
    ci = (li * a_re - (lr - 1.0) * a_im) / den
    cr_e, ci_e = _dot(cr, expand), _dot(ci, expand)
    return lr, li, cr_e * b_re - ci_e * b_im, cr_e * b_im + ci_e * b_re


def _loss_call(h, g, target, *, name, tr=256):
    s, d = h.shape
    tr = min(tr, s)

    def loss_fn(hv, gv, tv):
        err = _rms(hv, gv) - tv
        return 0.5 * jnp.sum(jnp.mean(err * err, axis=-1))

    def body(h_ref, g_ref, t_ref, dh_ref, dg_ref, loss_ref):
        val, (dh, dg) = jax.value_and_grad(loss_fn, argnums=(0, 1))(h_ref[...], g_ref[...], t_ref[...])
        dh_ref[...] = dh

        @pl.when(pl.program_id(0) == 0)
        def _():
            dg_ref[...] = jnp.zeros_like(dg_ref)
            loss_ref[...] = jnp.zeros_like(loss_ref)

        dg_ref[...] += dg
        loss_ref[...] += jnp.full(loss_ref.shape, val, F32)

    row = pl.BlockSpec((tr, d), lambda i: (i, 0))
    return pl.pallas_call(
        body, name=name, grid=(s // tr,),
        in_specs=[row, _full_spec(g), row],
        out_specs=[row, _full_spec(g), pl.BlockSpec((8, 128), lambda i: (0, 0))],
        out_shape=[jax.ShapeDtypeStruct((s, d), F32), jax.ShapeDtypeStruct(g.shape, F32),
                   jax.ShapeDtypeStruct((8, 128), F32)],
        compiler_params=_params(1),
    )(h, g, target)


def _hgrn_chunk(q, fp, iv, gp, lb, gn, st_t):
    c = CHUNK
    row = lax.broadcasted_iota(jnp.int32, (c, c), 0)
    col = lax.broadcasted_iota(jnp.int32, (c, c), 1)
    causal = row >= col
    fg = lb + (1.0 - lb) * _sigmoid(fp)
    k = 1.0 - fg
    lf = _log(fg)
    cum = _hdot_h(causal.astype(F32), lf, NN)
    cend = _sum(lf, axis=0, keepdims=True)
    shift = HGRN_SUB.bit_length() - 1
    ref = _hdot_h(((row >> shift) > (col >> shift)).astype(F32), lf, NN)
    q_dec = q * _exp(cum - ref)
    key_row = lax.broadcasted_iota(jnp.int32, (c, 1), 0)
    blocks = []
    for lo in range(0, c, HGRN_SUB):
        live = key_row < lo + HGRN_SUB
        k_dec = _where(live, k * _exp(_where(live, _rows(ref, lo, 1) - cum, 0.0)), 0.0)
        blocks.append(_hdot_h(_rows(q_dec, lo, HGRN_SUB), k_dec, NT))
    scores = _where(causal, _row_concat(*blocks), 0.0)
    out = _bdot_h(scores, iv, NN) + _bdot_h(q * _exp(cum), st_t, NT)
    st_new = st_t * _exp(cend) + _bdot_h(iv, k * _exp(cend - cum), TN)
    res = _rms(out, gn) * _silu(gp)
    return res, st_new


def _hgrn_heads(qs, fs, ivs, gs, lbs, gn, sts):
    res, st_new = _hgrn_chunk(_Heads(qs), _Heads(fs), _Heads(ivs), _Heads(gs), _Heads(lbs), gn, _Heads(sts))
    return res.vals, st_new.vals


def _lanes(j):
    return slice(j * HEAD, (j + 1) * HEAD)


def _hgrn_fwd(proj, lb, gn, *, heads, name, comm=()):
    s = proj.shape[0]
    n = s // CHUNK
    hpb = min(HEADS_PER_STEP, heads)
    assert heads % hpb == 0

    def body(q_ref, f_ref, i_ref, g_ref, lb_ref, gn_ref, o_ref, st_ref, state):
        @pl.when(pl.program_id(1) == 0)
        def _():
            state[...] = jnp.zeros_like(state)

        gnv = gn_ref[...]
        loaded = [(q_ref[:, _lanes(j)], f_ref[:, _lanes(j)], i_ref[:, _lanes(j)], g_ref[:, _lanes(j)],
                   lb_ref[:, _lanes(j)], state[j]) for j in range(hpb)]
        qs, fs, ivs, gs, lbs, sts = (list(t) for t in zip(*loaded))
        res, st_new = _hgrn_heads(qs, fs, ivs, gs, lbs, gnv, sts)
        for j in range(hpb):
            st_ref[j] = sts[j]
            o_ref[:, _lanes(j)] = res[j].astype(o_ref.dtype)
            state[j] = st_new[j]

    wide = hpb * HEAD
    blk = lambda off: pl.BlockSpec((CHUNK, wide), lambda h, c: (c, off // hpb + h))
    return _call(
        body, (proj, proj, proj, proj, lb, gn), name=name, grid=(heads // hpb, n),
        in_specs=[blk(0), blk(heads), blk(2 * heads), blk(3 * heads),
                  pl.BlockSpec((1, wide), lambda h, c: (0, h)), pl.BlockSpec((1, HEAD), lambda h, c: (0, 0))],
        out_specs=[pl.BlockSpec((CHUNK, wide), lambda h, c: (c, h)),
                   pl.BlockSpec((hpb, None, HEAD, HEAD), lambda h, c: (h, c, 0, 0))],
        out_shape=[jax.ShapeDtypeStruct((s, heads * HEAD), BF16), jax.ShapeDtypeStruct((heads, n, HEAD, HEAD), F32)],
        scratch_shapes=[pltpu.VMEM((hpb, HEAD, HEAD), F32)], comm=comm)


def _hgrn_bwd(proj, lb, gn, states, d_out, *, heads, name, comm=()):
    s = proj.shape[0]
    n = s // CHUNK
    hpb = min(HEADS_PER_STEP, heads)

    def body(q_ref, f_ref, i_ref, g_ref, lb_ref, gn_ref, st_ref, do_ref,
             dq_ref, df_ref, di_ref, dg_ref, dlb_ref, dgn_ref, dstate):
        h, c = pl.program_id(0), pl.program_id(1)

        @pl.when(c == 0)
        def _():
            dstate[...] = jnp.zeros_like(dstate)
            dlb_ref[...] = jnp.zeros_like(dlb_ref)

        @pl.when((c == 0) & (h == 0))
        def _():
            dgn_ref[...] = jnp.zeros_like(dgn_ref)

        gnv = gn_ref[...]
        loaded = [(q_ref[:, _lanes(j)], f_ref[:, _lanes(j)], i_ref[:, _lanes(j)], g_ref[:, _lanes(j)],
                   lb_ref[:, _lanes(j)], st_ref[j], do_ref[:, _lanes(j)].astype(F32), dstate[j]) for j in range(hpb)]
        qs, fs, ivs, gs, lbs, sts, dos, dss = (list(t) for t in zip(*loaded))
        _, vjp = jax.vjp(_hgrn_heads, qs, fs, ivs, gs, lbs, gnv, sts)
        dqs, dfs, dis, dgs, dlbs, dgn_sum, dsts = vjp((dos, dss))
        for j in range(hpb):
            ln = _lanes(j)
            dq_ref[:, ln] = dqs[j].astype(dq_ref.dtype)
            df_ref[:, ln] = dfs[j].astype(df_ref.dtype)
            di_ref[:, ln] = dis[j].astype(di_ref.dtype)
            dg_ref[:, ln] = dgs[j].astype(dg_ref.dtype)
            dlb_ref[:, ln] += dlbs[j]
            dstate[j] = dsts[j]
        dgn_ref[...] += dgn_sum

    wide = hpb * HEAD
    rev = lambda off: pl.BlockSpec((CHUNK, wide), lambda h, c: (n - 1 - c, off // hpb + h))
    out_blk = pl.BlockSpec((CHUNK, wide), lambda h, c: (n - 1 - c, h))
    width = heads * HEAD
    return _call(
        body, (proj, proj, proj, proj, lb, gn, states, d_out), name=name, grid=(heads // hpb, n),
        in_specs=[rev(0), rev(heads), rev(2 * heads), rev(3 * heads),
                  pl.BlockSpec((1, wide), lambda h, c: (0, h)), pl.BlockSpec((1, HEAD), lambda h, c: (0, 0)),
                  pl.BlockSpec((hpb, None, HEAD, HEAD), lambda h, c: (h, n - 1 - c, 0, 0)), out_blk],
        out_specs=[out_blk, out_blk, out_blk, out_blk,
                   pl.BlockSpec((1, wide), lambda h, c: (0, h)), pl.BlockSpec((1, HEAD), lambda h, c: (0, 0))],
        out_shape=[jax.ShapeDtypeStruct((s, width), BF16)] * 4
        + [jax.ShapeDtypeStruct((1, width), F32), jax.ShapeDtypeStruct((1, HEAD), F32)],
        scratch_shapes=[pltpu.VMEM((hpb, HEAD, HEAD), F32)], comm=comm)


def _shift_rows(x, d, rowi):
    if d == 0:
        return x
    n = x.shape[0]
    rolled = pltpu.roll(x, d % n, 0)
    keep = rowi >= d if d > 0 else rowi < n + d
    return jnp.where(keep, rolled, 0.0)


def _conv_pre(x, w_ref, rowi):
    acc = None
    for j in range(CONV_WIDTH):
        term = w_ref[j:j + 1, :] * _shift_rows(x, CONV_WIDTH - 1 - j, rowi)
        acc = term if acc is None else acc + term
    return acc


def _conv_fwd(proj, w, *, col_off, name, cb=256):
    s = proj.shape[0]
    width = w.shape[1]
    cb = min(cb, width)

    def body(x_ref, w_ref, o_ref):
        rowi = lax.broadcasted_iota(jnp.int32, (s, cb), 0)
        o_ref[...] = _silu(_conv_pre(x_ref[...], w_ref, rowi))

    return pl.pallas_call(
        body, name=name, grid=(width // cb,),
        in_specs=[pl.BlockSpec((s, cb), lambda j: (0, col_off // cb + j)), pl.BlockSpec((CONV_WIDTH, cb), lambda j: (0, j))],
        out_specs=pl.BlockSpec((s, cb), lambda j: (0, j)),
        out_shape=jax.ShapeDtypeStruct((s, width), F32),
        compiler_params=_params(1),
    )(proj, w)


def _conv_bwd(proj, w, d_out, *, col_off, name, cb=256, comm=()):
    s = proj.shape[0]
    width = w.shape[1]
    cb = min(cb, width)

    def body(x_ref, w_ref, do_ref, dx_ref, dw_ref):
        rowi = lax.broadcasted_iota(jnp.int32, (s, cb), 0)
        x = x_ref[...]
        pre = _conv_pre(x, w_ref, rowi)
        sg = jax.nn.sigmoid(pre)
        dpre = do_ref[...] * (sg + pre * sg * (1.0 - sg))
        dx = None
        for j in range(CONV_WIDTH):
            d = CONV_WIDTH - 1 - j
            term = w_ref[j:j + 1, :] * _shift_rows(dpre, -d, rowi)
            dx = term if dx is None else dx + term
            dw_ref[j:j + 1, :] = jnp.sum(dpre * _shift_rows(x, d, rowi), axis=0, keepdims=True)
        dx_ref[...] = dx.astype(dx_ref.dtype)

    return _call(
        body, (proj, w, d_out), name=name, grid=(width // cb,),
        in_specs=[pl.BlockSpec((s, cb), lambda j: (0, col_off // cb + j)), pl.BlockSpec((CONV_WIDTH, cb), lambda j: (0, j)),
                  pl.BlockSpec((s, cb), lambda j: (0, j))],
        out_specs=[pl.BlockSpec((s, cb), lambda j: (0, j)), pl.BlockSpec((CONV_WIDTH, cb), lambda j: (0, j))],
        out_shape=[jax.ShapeDtypeStruct((s, width), BF16), jax.ShapeDtypeStruct((CONV_WIDTH, width), F32)],
        comm=comm)


_lane_concat = _per_head(lambda a, b: jnp.concatenate([a, b], axis=1))
_lane_half = _per_head(lambda a, j: a[:, j * HEAD:(j + 1) * HEAD])


def _tri_inverse(lower):
    c = CHUNK
    row = lax.broadcasted_iota(jnp.int32, (c, c), 0)
    col = lax.broadcasted_iota(jnp.int32, (c, c), 1)
    inv = (row == col).astype(F32)
    lvl = 0
    while (1 << lvl) < c:
        same_pair = (row >> (lvl + 1)) == (col >> (lvl + 1))
        off_block = same_pair & (((row >> lvl) & 1) == 1) & (((col >> lvl) & 1) == 0)
        inv = inv - _hdot_h(_hdot_h(inv, _where(off_block, lower, 0.0), NN), inv, NN)
        lvl += 1
    return inv


@jax.custom_vjp
def _tri_solve(lowers, rhss):
    return _tri_solve_fwd(lowers, rhss)[0]


def _tri_solve_fwd(lowers, rhss):
    inv = _tri_inverse(_Heads(lowers))
    sol = _hdot_h(inv, _Heads(rhss), NN)
    return sol.vals, (inv.vals, sol.vals)


def _tri_solve_bwd(res, g):
    inv, sol = _Heads(res[0]), _Heads(res[1])
    d_rhs = _hdot_h(inv, _Heads(g), TN)
    return (-_hdot_h(d_rhs, sol, NT)).vals, d_rhs.vals


_tri_solve.defvjp(_tri_solve_fwd, _tri_solve_bwd)


def _solve(lower, rhs):
    if isinstance(lower, _Heads):
        return _Heads(_tri_solve(lower.vals, rhs.vals))
    return _tri_solve([lower], [rhs])[0]


def _delta_chunk(h, heads, qr, kr, vr, ab, zp, alog, dtb, gn, st):
    c = CHUNK
    row = lax.broadcasted_iota(jnp.int32, (c, c), 0)
    col = lax.broadcasted_iota(jnp.int32, (c, c), 1)
    causal = row >= col
    strict = row > col
    lane = lax.broadcasted_iota(jnp.int32, (c, HEAD), 1)
    mine = _equal(h, lane)
    la_full = -jnp.exp(alog) * _softplus(ab + dtb)
    cum_full = _hdot(causal.astype(F32), la_full)
    cum = _sum(_where(mine, cum_full, 0.0), axis=1, keepdims=True)
    cend = _sum(_sum(_where(mine, la_full, 0.0), axis=1, keepdims=True), axis=0, keepdims=True)
    beta = _sum(_where(_equal(heads + h, lane), jax.nn.sigmoid(ab), 0.0), axis=1, keepdims=True)
    cum_row = _hdot_h(_where(mine, 1.0, 0.0), cum_full, NT)
    decay = _where(causal, _exp(_where(causal, cum - cum_row, 0.0)), 0.0)
    qn = qr * _rsqrt(_sum(qr * qr, axis=-1, keepdims=True) + NORM_EPS) * (HEAD ** -0.5)
    kn = kr * _rsqrt(_sum(kr * kr, axis=-1, keepdims=True) + NORM_EPS)
    kb = kn * beta
    lower = _where(strict, _bdot_h(kb, kn, NT) * decay, 0.0)
    ecum = _exp(cum)
    sol = _solve(lower, _lane_concat(vr * beta, kb * ecum))
    u, w = _lane_half(sol, 0), _lane_half(sol, 1)
    intra = _bdot_h(qn, kn, NT) * decay
    v_new = u - _bdot_h(w, st, NN)
    out = _bdot_h(qn * ecum, st, NN) + _bdot_h(intra, v_new, NN)
    st_new = st * _exp(cend) + _bdot_h(kn * _exp(cend - cum), v_new, TN)
    res = _rms(out, gn) * _silu(zp)
    return res, st_new


def _delta_heads(hs, heads, qs, ks, vs, ab, zs, alog, dtb, gn, sts):
    res, st_new = _delta_chunk(_Heads(hs), heads, _Heads(qs), _Heads(ks), _Heads(vs), ab, _Heads(zs), alog, dtb, gn,
                               _Heads(sts))
    return res.vals, st_new.vals


def _delta_fwd(qkv, ab, proj, hp, gn, *, heads, z_off, name, comm=()):
    s = qkv.shape[0]
    n = s // CHUNK

    hpb = min(HEADS_PER_STEP, heads)
    assert heads % hpb == 0 and z_off % hpb == 0

    def body(q_ref, k_ref, v_ref, ab_ref, z_ref, hp_ref, gn_ref, o_ref, st_ref, state):
        hb = pl.program_id(1)

        @pl.when(pl.program_id(0) == 0)
        def _():
            for j in range(hpb):
                state[hb * hpb + j] = jnp.zeros((HEAD, HEAD), F32)

        shared = (ab_ref[...], hp_ref[0:1, :], hp_ref[1:2, :], gn_ref[...])
        loaded = [(q_ref[:, _lanes(j)], k_ref[:, _lanes(j)], v_ref[:, _lanes(j)], z_ref[:, _lanes(j)],
                   state[hb * hpb + j]) for j in range(hpb)]
        qs, ks, vs, zs, sts = (list(t) for t in zip(*loaded))
        res, st_new = _delta_heads([hb * hpb + j for j in range(hpb)], heads, qs, ks, vs, shared[0], zs, shared[1],
                                   shared[2], shared[3], sts)
        for j in range(hpb):
            st_ref[j] = sts[j]
            o_ref[:, _lanes(j)] = res[j].astype(o_ref.dtype)
            state[hb * hpb + j] = st_new[j]

    wide = hpb * HEAD
    blk = lambda off: pl.BlockSpec((CHUNK, wide), lambda c, h: (c, off // hpb + h))
    return _call(
        body, (qkv, qkv, qkv, ab, proj, hp, gn), name=name, grid=(n, heads // hpb),
        in_specs=[blk(0), blk(heads), blk(2 * heads), pl.BlockSpec((CHUNK, HEAD), lambda c, h: (c, 0)), blk(z_off),
                  pl.BlockSpec((8, HEAD), lambda c, h: (0, 0)), pl.BlockSpec((1, HEAD), lambda c, h: (0, 0))],
        out_specs=[pl.BlockSpec((CHUNK, wide), lambda c, h: (c, h)),
                   pl.BlockSpec((hpb, None, HEAD, HEAD), lambda c, h: (h, c, 0, 0))],
        out_shape=[jax.ShapeDtypeStruct((s, heads * HEAD), BF16), jax.ShapeDtypeStruct((heads, n, HEAD, HEAD), F32)],
        scratch_shapes=[pltpu.VMEM((heads, HEAD, HEAD), F32)], comm=comm)


def _delta_bwd(qkv, ab, proj, hp, gn, states, d_out, *, heads, z_off, name, comm=()):
    s = qkv.shape[0]
    n = s // CHUNK
    hpb = min(HEADS_PER_STEP, heads)

    def body(q_ref, k_ref, v_ref, ab_ref, z_ref, hp_ref, gn_ref, st_ref, do_ref,
             dq_ref, dk_ref, dv_ref, dab_ref, dz_ref, dhp_ref, dgn_ref, dstate):
        c, hb = pl.program_id(0), pl.program_id(1)

        @pl.when(c == 0)
        def _():
            for j in range(hpb):
                dstate[hb * hpb + j] = jnp.zeros((HEAD, HEAD), F32)

        @pl.when((c == 0) & (hb == 0))
        def _():
            dgn_ref[...] = jnp.zeros_like(dgn_ref)
            dhp_ref[...] = jnp.zeros_like(dhp_ref)

        @pl.when(hb == 0)
        def _():
            dab_ref[...] = jnp.zeros_like(dab_ref)

        shared = (ab_ref[...], hp_ref[0:1, :], hp_ref[1:2, :], gn_ref[...])
        loaded = [(q_ref[:, _lanes(j)], k_ref[:, _lanes(j)], v_ref[:, _lanes(j)], z_ref[:, _lanes(j)], st_ref[j],
                   do_ref[:, _lanes(j)].astype(F32), dstate[hb * hpb + j]) for j in range(hpb)]
        qs, ks, vs, zs, sts, dos, dss = (list(t) for t in zip(*loaded))
        fn = functools.partial(_delta_heads, [hb * hpb + j for j in range(hpb)], heads)
        _, vjp = jax.vjp(fn, qs, ks, vs, shared[0], zs, shared[1], shared[2], shared[3], sts)
        dqs, dks, dvs, dab, dzs, dal, ddt, dgn, dsts = vjp((dos, dss))
        for j in range(hpb):
            ln = _lanes(j)
            dq_ref[:, ln] = dqs[j]
            dk_ref[:, ln] = dks[j]
            dv_ref[:, ln] = dvs[j]
            dz_ref[:, ln] = dzs[j].astype(dz_ref.dtype)
            dstate[hb * hpb + j] = dsts[j]
        dab_ref[...] += dab
        dhp_ref[0:1, :] += dal
        dhp_ref[1:2, :] += ddt
        dgn_ref[...] += dgn

    wide = hpb * HEAD
    rev = lambda off: pl.BlockSpec((CHUNK, wide), lambda c, h: (n - 1 - c, off // hpb + h))
    width = heads * HEAD
    head_blk = pl.BlockSpec((CHUNK, wide), lambda c, h: (n - 1 - c, h))
    ab_blk = pl.BlockSpec((CHUNK, HEAD), lambda c, h: (n - 1 - c, 0))
    return _call(
        body, (qkv, qkv, qkv, ab, proj, hp, gn, states, d_out), name=name, grid=(n, heads // hpb),
        in_specs=[rev(0), rev(heads), rev(2 * heads), ab_blk, rev(z_off),
                  pl.BlockSpec((8, HEAD), lambda c, h: (0, 0)), pl.BlockSpec((1, HEAD), lambda c, h: (0, 0)),
                  pl.BlockSpec((hpb, None, HEAD, HEAD), lambda c, h: (h, n - 1 - c, 0, 0)), head_blk],
        out_specs=[head_blk, head_blk, head_blk, ab_blk, head_blk,
                   pl.BlockSpec((8, HEAD), lambda c, h: (0, 0)), pl.BlockSpec((1, HEAD), lambda c, h: (0, 0))],
        out_shape=[jax.ShapeDtypeStruct((s, width), F32)] * 3
        + [jax.ShapeDtypeStruct((s, HEAD), F32), jax.ShapeDtypeStruct((s, width), BF16),
           jax.ShapeDtypeStruct((8, HEAD), F32), jax.ShapeDtypeStruct((1, HEAD), F32)],
        scratch_shapes=[pltpu.VMEM((heads, HEAD, HEAD), F32)], comm=comm)


def _s5_scan(buf, lt_ref, cin_r, cin_i, tt, reverse):
    nblk = tt // 8
    hl = S5_HALF
    base = 8 if reverse else 0

    def body(j, carry):
        cr, ci = carry
        off = pl.multiple_of((nblk - 1 - j if reverse else j) * 8, 8)
        xr = buf[pl.ds(off, 8), 0:hl]
        xi = buf[pl.ds(off, 8), hl:2 * hl]
        for lv, d in enumerate((1, 2, 4)):
            ar, ai = lt_ref[base + 2 * lv], lt_ref[base + 2 * lv + 1]
            sr = pltpu.roll(xr, 8 - d if reverse else d, 0)
            si = pltpu.roll(xi, 8 - d if reverse else d, 0)
            xr, xi = xr + ar * sr - ai * si, xi + ar * si + ai * sr
        pr, pi = lt_ref[base + 6], lt_ref[base + 7]
        xr, xi = xr + pr * cr - pi * ci, xi + pr * ci + pi * cr
        buf[pl.ds(off, 8), 0:hl] = xr
        buf[pl.ds(off, 8), hl:2 * hl] = xi
        edge = 0 if reverse else 7
        return xr[edge:edge + 1, :], xi[edge:edge + 1, :]

    return lax.fori_loop(0, nblk, body, (cin_r, cin_i))


def _s5_fwd(u, wb, wc, lt, dskip, *, name, tt=1024, comm=()):
    s, d = u.shape
    nb = d // HEAD
    tt = min(tt, s)
    nt = s // tt
    hl = S5_HALF

    def body(u_ref, wb_ref, wc_ref, lt_ref, d_ref, y_ref, cin_ref, st_ref, act_ref, buf, carry):
        @pl.when(pl.program_id(1) == 0)
        def _():
            carry[...] = jnp.zeros_like(carry)

        cin_ref[...] = carry[0:1, :]
        uv = u_ref[...]
        buf[...] = _bdot_raw(uv, wb_ref[...])
        cr, ci = _s5_scan(buf, lt_ref, carry[0:1, 0:hl], carry[0:1, hl:2 * hl], tt, False)
        carry[0:1, 0:hl] = cr
        carry[0:1, hl:2 * hl] = ci
        states = buf[...].astype(BF16)
        st_ref[...] = states
        yv = _bdot_raw(states, wc_ref[...]) + d_ref[...] * uv
        y_ref[...] = yv
        act_ref[...] = _gelu(yv).astype(act_ref.dtype)

    return _call(
        body, (u, wb, wc, lt, dskip), name=name, grid=(nb, nt),
        in_specs=[pl.BlockSpec((tt, HEAD), lambda b, t: (t, b)),
                  pl.BlockSpec((None, HEAD, 2 * hl), lambda b, t: (b, 0, 0)),
                  pl.BlockSpec((None, 2 * hl, HEAD), lambda b, t: (b, 0, 0)),
                  pl.BlockSpec((None, 16, 8, hl), lambda b, t: (b, 0, 0, 0)),
                  pl.BlockSpec((1, HEAD), lambda b, t: (0, b))],
        out_specs=[pl.BlockSpec((tt, HEAD), lambda b, t: (t, b)),
                   pl.BlockSpec((None, None, 1, 2 * hl), lambda b, t: (b, t, 0, 0)),
                   pl.BlockSpec((tt, 2 * hl), lambda b, t: (t, b)),
                   pl.BlockSpec((tt, HEAD), lambda b, t: (t, b))],
        out_shape=[jax.ShapeDtypeStruct((s, d), F32), jax.ShapeDtypeStruct((nb, nt, 1, 2 * hl), F32),
                   jax.ShapeDtypeStruct((s, nb * 2 * hl), BF16), jax.ShapeDtypeStruct((s, d), BF16)],
        scratch_shapes=[pltpu.VMEM((tt, 2 * hl), F32), pltpu.VMEM((8, 2 * hl), F32)], comm=comm)


def _s5_bwd(u, dy, wb, wc, lt, dskip, cins, states, *, name, tt=1024, comm=()):
    s, d = u.shape
    nb = d // HEAD
    tt = min(tt, s)
    nt = s // tt
    hl = S5_HALF

    def body(u_ref, dy_ref, wb_ref, wc_ref, lt_ref, d_ref, cin_ref, st_ref,
             du_ref, dwb_ref, dwc_ref, dd_ref, dlam_ref, abuf, acarry):
        @pl.when(pl.program_id(1) == 0)
        def _():
            acarry[...] = jnp.zeros_like(acarry)
            dwb_ref[...] = jnp.zeros_like(dwb_ref)
            dwc_ref[...] = jnp.zeros_like(dwc_ref)
            dd_ref[...] = jnp.zeros_like(dd_ref)
            dlam_ref[...] = jnp.zeros_like(dlam_ref)

        uv, dyv = u_ref[...], dy_ref[...]
        dy16 = dyv.astype(BF16)
        abuf[...] = _bdot_raw(dy16, wc_ref[...], NT)
        ar, ai = _s5_scan(abuf, lt_ref, acarry[0:1, 0:hl], acarry[0:1, hl:2 * hl], tt, True)
        acarry[0:1, 0:hl] = ar
        acarry[0:1, hl:2 * hl] = ai
        adj16 = abuf[...].astype(BF16)
        du_ref[...] = _bdot_raw(adj16, wb_ref[...], NT) + d_ref[...] * dyv
        dwb_ref[...] += _bdot_raw(uv, adj16, TN)
        dwc_ref[...] += _bdot_raw(st_ref[...], dy16, TN)
        dd_ref[...] += jnp.sum(dyv * uv, axis=0, keepdims=True)
        first = lax.broadcasted_iota(jnp.int32, (tt, hl), 0) == 0
        spr = jnp.where(first, cin_ref[:, 0:hl], pltpu.roll(st_ref[:, 0:hl].astype(F32), 1, 0))
        spi = jnp.where(first, cin_ref[:, hl:2 * hl], pltpu.roll(st_ref[:, hl:2 * hl].astype(F32), 1, 0))
        avr, avi = abuf[:, 0:hl], abuf[:, hl:2 * hl]
        dlam_ref[:, 0:hl] += jnp.sum(avr * spr + avi * spi, axis=0, keepdims=True)
        dlam_ref[:, hl:2 * hl] += jnp.sum(avi * spr - avr * spi, axis=0, keepdims=True)

    rev = pl.BlockSpec((tt, HEAD), lambda b, t: (nt - 1 - t, b))
    return _call(
        body, (u, dy, wb, wc, lt, dskip, cins, states), name=name, grid=(nb, nt),
        in_specs=[rev, rev,
                  pl.BlockSpec((None, HEAD, 2 * hl), lambda b, t: (b, 0, 0)),
                  pl.BlockSpec((None, 2 * hl, HEAD), lambda b, t: (b, 0, 0)),
                  pl.BlockSpec((None, 16, 8, hl), lambda b, t: (b, 0, 0, 0)),
                  pl.BlockSpec((1, HEAD), lambda b, t: (0, b)),
                  pl.BlockSpec((None, None, 1, 2 * hl), lambda b, t: (b, nt - 1 - t, 0, 0)),
                  pl.BlockSpec((tt, 2 * hl), lambda b, t: (nt - 1 - t, b))],
        out_specs=[rev,
                   pl.BlockSpec((None, HEAD, 2 * hl), lambda b, t: (b, 0, 0)),
                   pl.BlockSpec((None, 2 * hl, HEAD), lambda b, t: (b, 0, 0)),
                   pl.BlockSpec((1, HEAD), lambda b, t: (0, b)),
                   pl.BlockSpec((None, 1, 2 * hl), lambda b, t: (b, 0, 0))],
        out_shape=[jax.ShapeDtypeStruct((s, d), F32), jax.ShapeDtypeStruct(wb.shape, F32),
                   jax.ShapeDtypeStruct(wc.shape, F32), jax.ShapeDtypeStruct((1, d), F32),
                   jax.ShapeDtypeStruct((nb, 1, 2 * hl), F32)],
        scratch_shapes=[pltpu.VMEM((tt, 2 * hl), F32), pltpu.VMEM((8, 2 * hl), F32)],
        comm=comm)


def _s5_pack(lr, li, br, bi, c_re, c_im):
    g = lr.shape[0]
    nb = g // S5_GB
    eye = jnp.eye(S5_GB, dtype=F32)
    bm = jnp.stack([br, bi]).reshape(2, nb, S5_GB, S5_STATE, S5_GROUP)
    wb = jnp.einsum("rbgpc,gh->bgcrhp", bm, eye).reshape(nb, HEAD, 2 * S5_HALF)
    cm = jnp.stack([c_re, -c_im]).reshape(2, nb, S5_GB, S5_GROUP, S5_STATE)
    wc = jnp.einsum("rbgcp,gh->brgphc", cm, eye).reshape(nb, 2 * S5_HALF, HEAD)
    pw = [(lr, li)]
    for _ in range(7):
        pr, pi = pw[-1]
        pw.append((pr * lr - pi * li, pr * li + pi * lr))
    blk = lambda a: a.reshape(nb, 1, S5_HALF)
    rows = jnp.arange(8).reshape(1, 8, 1)
    tables = []
    for conj, keep, order in ((1.0, lambda n: rows >= n, range(8)), (-1.0, lambda n: rows < 8 - n, range(7, -1, -1))):
        for n in (1, 2, 4):
            tables += [jnp.where(keep(n), blk(pw[n - 1][0]), 0.0), jnp.where(keep(n), conj * blk(pw[n - 1][1]), 0.0)]
        tables += [jnp.concatenate([blk(pw[n][0]) for n in order], axis=1),
                   jnp.concatenate([conj * blk(pw[n][1]) for n in order], axis=1)]
    return wb, wc, jnp.stack(tables, axis=1)


def _s5_unpack(dwb, dwc, dlam):
    nb = dwb.shape[0]
    g = nb * S5_GB
    eye = jnp.eye(S5_GB, dtype=F32)
    db = jnp.einsum("bgcrhp,gh->rbgpc", dwb.reshape(nb, S5_GB, S5_GROUP, 2, S5_GB, S5_STATE), eye)
    db = db.reshape(2, g, S5_STATE * S5_GROUP)
    dc = jnp.einsum("brgphc,gh->rbgcp", dwc.reshape(nb, 2, S5_GB, S5_STATE, S5_GB, S5_GROUP), eye)
    dc = dc.reshape(2, g, S5_GROUP, S5_STATE)
    dl = dlam.reshape(nb, 2, S5_GB, S5_STATE).transpose(1, 0, 2, 3).reshape(2, g, S5_STATE)
    return dl[0], dl[1], db[0], db[1], dc[0], -dc[1]


def _peer(r):
    mx, my, mc = lax.axis_index("x"), lax.axis_index("y"), lax.axis_index("c")
    px = 1 - mx if r & 4 else mx
    py = 1 - my if r & 2 else my
    pc = 1 - mc if r & 1 else mc
    return (px, py, pc), 4 * px + 2 * py + pc


_COMM_SCRATCH = [pltpu.SemaphoreType.DMA((N_DEV - 1,)), pltpu.SemaphoreType.DMA((N_DEV - 1,)), pltpu.SemaphoreType.DMA]


class _AllToAll:
    def __init__(self, x, rows=None):
        self.x = x
        self.rows = rows
        shape = x.shape if rows is None else (x.shape[0], rows[1]) + tuple(x.shape[2:])
        self.out_shape = jax.ShapeDtypeStruct(shape, x.dtype)

    def _copies(self, x_ref, out_ref, send_sems, recv_sems, local_sem):
        def block(j):
            return x_ref.at[j] if self.rows is None else x_ref.at[j, pl.ds(self.rows[0], self.rows[1])]

        _, me = _peer(0)
        mine = pltpu.make_async_copy(block(me), out_ref.at[me], local_sem)
        sends, recvs = [], []
        for r in range(1, N_DEV):
            pos, idx = _peer(r)
            sems = dict(send_sem=send_sems.at[r - 1], recv_sem=recv_sems.at[r - 1], device_id=pos, device_id_type=MESH)
            sends.append(pltpu.make_async_remote_copy(src_ref=block(idx), dst_ref=out_ref.at[me], **sems))
            recvs.append(pltpu.make_async_remote_copy(src_ref=block(idx), dst_ref=out_ref.at[idx], **sems))
        return mine, sends, recvs

    def start(self, *refs):
        mine, sends, _ = self._copies(*refs)
        mine.start()
        for cp in sends:
            cp.start()

    def finish(self, *refs):
        mine, sends, recvs = self._copies(*refs)
        for cp in recvs:
            cp.wait_recv()
        for cp in sends:
            cp.wait_send()
        mine.wait()


class _Gather:
    def __init__(self, x):
        self.x = x
        self.out_shape = jax.ShapeDtypeStruct((N_DEV,) + tuple(x.shape), x.dtype)

    def _copies(self, x_ref, out_ref, send_sems, recv_sems, local_sem):
        mx, my, mc = lax.axis_index("x"), lax.axis_index("y"), lax.axis_index("c")
        me, sibling = (mx, my, mc), (mx, my, 1 - mc)
        chips = [(1 - mx, my), (mx, 1 - my), (1 - mx, 1 - my)]

        def slot(px, py, pc):
            return out_ref.at[4 * px + 2 * py + pc]

        def copy(k, block, to, src=None):
            return pltpu.make_async_remote_copy(
                src_ref=slot(*block) if src is None else src, dst_ref=slot(*block),
                send_sem=send_sems.at[k], recv_sem=recv_sems.at[k], device_id=to, device_id_type=MESH)

        return dict(
            mine=pltpu.make_async_copy(x_ref, slot(*me), local_sem),
            first=[copy(0, me, sibling, src=x_ref)] + [copy(1 + j, me, (*chip, mc), src=x_ref) for j, chip in enumerate(chips)],
            passed=[copy(4 + j, (*chip, mc), sibling) for j, chip in enumerate(chips)],
            over_ici=[copy(1 + j, (*chip, mc), me) for j, chip in enumerate(chips)],
            from_sibling=[copy(0, sibling, me)] + [copy(4 + j, (*chip, 1 - mc), me) for j, chip in enumerate(chips)])

    def start(self, *refs):
        cps = self._copies(*refs)
        cps["mine"].start()
        for cp in cps["first"]:
            cp.start()

    def finish(self, *refs):
        cps = self._copies(*refs)
        for arrived, onward in zip(cps["over_ici"], cps["passed"]):
            arrived.wait_recv()
            onward.start()
        for cp in cps["from_sibling"]:
            cp.wait_recv()
        for cp in cps["first"] + cps["passed"]:
            cp.wait_send()
        cps["mine"].wait()


def _call(body, args, *, name, grid, in_specs, out_specs, out_shape, scratch_shapes=(), comm=()):
    n_in, n_out, n_scr, nc = len(in_specs), len(out_shape), len(scratch_shapes), len(comm)

    def wrapped(*refs):
        ins, c_in = refs[:n_in], refs[n_in:n_in + nc]
        outs, c_out = refs[n_in + nc:n_in + nc + n_out], refs[n_in + nc + n_out:n_in + 2 * nc + n_out]
        scr = refs[n_in + 2 * nc + n_out:n_in + 2 * nc + n_out + n_scr]
        sems = refs[n_in + 2 * nc + n_out + n_scr:]
        ids = [pl.program_id(a) for a in range(len(grid))]
        if nc:
            @pl.when(functools.reduce(operator.and_, [i == 0 for i in ids]))
            def _():
                for k, op in enumerate(comm):
                    op.start(c_in[k], c_out[k], *sems[3 * k:3 * k + 3])

        body(*ins, *outs, *scr)
        if nc:
            @pl.when(functools.reduce(operator.and_, [i == g - 1 for i, g in zip(ids, grid)]))
            def _():
                for k, op in enumerate(comm):
                    op.finish(c_in[k], c_out[k], *sems[3 * k:3 * k + 3])

    any_spec = pl.BlockSpec(memory_space=pl.ANY)
    res = pl.pallas_call(
        wrapped, name=name, grid=grid,
        in_specs=list(in_specs) + [any_spec] * nc, out_specs=list(out_specs) + [any_spec] * nc,
        out_shape=list(out_shape) + [op.out_shape for op in comm],
        scratch_shapes=list(scratch_shapes) + list(_COMM_SCRATCH) * nc,
        compiler_params=_params(len(grid)),
    )(*args, *[op.x for op in comm])
    return list(res[:n_out]), list(res[n_out:])


def _comm_call(op, *, name):
    return _call(lambda: None, (), name=name, grid=(1,), in_specs=[], out_specs=[], out_shape=[], comm=(op,))[1][0]


def _adamw(w, parts, m, v, *, name, tr=128, comm=()):
    nl, r, c = w.shape
    assert len(parts) == nl
    parts = [list(p) if isinstance(p, (list, tuple)) else [p] for p in parts]
    npart = parts[0][0].shape[0]
    tr = min(tr, r, *[pc.shape[1] for p in parts for pc in p])
    assert r % tr == 0 and all(pc.shape[1] % tr == 0 for p in parts for pc in p), (name, r, tr)
    pieces = []
    for l, p in enumerate(parts):
        first = 0
        for pc in p:
            pieces.append((l, first, pc.shape[1] // tr, pc))
            first += pc.shape[1] // tr
        assert first == r // tr, (name, l)

    def body(w_ref, m_ref, v_ref, *rest):
        p_refs, (g_ref, d_ref, mo_ref, vo_ref) = rest[:len(pieces)], rest[len(pieces):]
        layer, tile = pl.program_id(0), pl.program_id(1)
        for p_ref, (l, first, count, _) in zip(p_refs, pieces):
            @pl.when((layer == l) & (tile >= first) & (tile < first + count))
            def _():
                g = p_ref[0].astype(F32)
                for k in range(1, npart):
                    g = g + p_ref[k].astype(F32)
                m2 = ADAM_B1 * m_ref[...] + (1.0 - ADAM_B1) * g
                v2 = ADAM_B2 * v_ref[...] + (1.0 - ADAM_B2) * (g * g)
                m_hat = m2 / (1.0 - ADAM_B1 ** ADAM_STEP)
                v_hat = v2 / (1.0 - ADAM_B2 ** ADAM_STEP)
                g_ref[...] = g
                d_ref[...] = -ADAM_LR * (m_hat / (jnp.sqrt(v_hat) + ADAM_EPS) + ADAM_WD * w_ref[...])
                mo_ref[...] = m2
                vo_ref[...] = v2

    blk = pl.BlockSpec((None, tr, c), lambda l, i: (l, i, 0))

    def part_spec(l, first, count):
        return pl.BlockSpec((npart, tr, c), lambda ll, i: (0, jnp.where(ll == l, jnp.clip(i - first, 0, count - 1), 0), 0))

    outs, exchanged = _call(
        body, (w, m, v, *[pc for _, _, _, pc in pieces]), name=name, grid=(nl, r // tr),
        in_specs=[blk, blk, blk] + [part_spec(l, first, count) for l, first, count, _ in pieces],
        out_specs=[blk] * 4, out_shape=[jax.ShapeDtypeStruct((nl, r, c), F32)] * 4, comm=comm)
    return (outs, exchanged) if comm else outs


def _sum_parts(parts, *, name):
    npart = parts.shape[0]

    def body(p_ref, o_ref):
        g = p_ref[0]
        for k in range(1, npart):
            g = g + p_ref[k]
        o_ref[...] = g

    return pl.pallas_call(
        body, name=name, grid=(1,), in_specs=[_full_spec(parts)],
        out_specs=pl.BlockSpec(parts.shape[1:], lambda i: (0, 0)),
        out_shape=jax.ShapeDtypeStruct(parts.shape[1:], F32), compiler_params=_params(1),
    )(parts)


def _pack(arrs):
    blocks = []
    for a in arrs:
        flat = a.reshape(-1).astype(F32)
        blocks.append(jnp.pad(flat, (0, (-flat.shape[0]) % (8 * HEAD))).reshape(-1, HEAD))
    out = jnp.concatenate(blocks, axis=0)
    return jnp.pad(out, ((0, (-out.shape[0]) % HEAD), (0, 0)))


def _unpack(packed, shapes):
    out, off = [], 0
    for shp in shapes:
        size = math.prod(shp)
        rows = -(-size // (8 * HEAD)) * 8
        out.append(packed[off:off + rows].reshape(-1)[:size].reshape(shp))
        off += rows
    return out


def _add_epilogue(acc, res):
    return (acc + res,)


def _relu2_epilogue(acc):
    r = jnp.maximum(acc, 0.0)
    return acc, r * r


def _ple_epilogue(acc, gpre, h):
    return h + jax.nn.sigmoid(gpre) * acc, acc


def kernel(x, p, norm_mix, norm_mlp, norm_ple, w_in_e, w_out_e, hgrn_lb, g_norm_a, conv_w, a_log, dt_bias, g_norm_b, s5_a_re, s5_a_im, s5_b_re, s5_b_im, s5_c_re, s5_c_im, s5_d, s5_log_dt, w_glu, b_glu, w_out_o, w_up, w_down, w_ple_gate, w_ple_proj, final_norm, loss_target, m_norm_mix, m_norm_mlp, m_norm_ple, m_w_in_e, m_w_out_e, m_hgrn_lb, m_g_norm_a, m_conv_w, m_a_log, m_dt_bias, m_g_norm_b, m_s5_a_re, m_s5_a_im, m_s5_b_re, m_s5_b_im, m_s5_c_re, m_s5_c_im, m_s5_d, m_s5_log_dt, m_w_glu, m_b_glu, m_w_out_o, m_w_up, m_w_down, m_w_ple_gate, m_w_ple_proj, m_final_norm, v_norm_mix, v_norm_mlp, v_norm_ple, v_w_in_e, v_w_out_e, v_hgrn_lb, v_g_norm_a, v_conv_w, v_a_log, v_dt_bias, v_g_norm_b, v_s5_a_re, v_s5_a_im, v_s5_b_re, v_s5_b_im, v_s5_c_re, v_s5_c_im, v_s5_d, v_s5_log_dt, v_w_glu, v_b_glu, v_w_out_o, v_w_up, v_w_down, v_w_ple_gate, v_w_ple_proj, v_final_norm):
    args = dict(locals())
    s, d = x.shape[1], x.shape[2]
    aw = d // 2
    ha = hb = aw // HEAD
    main = 4 * d
    z_col = 2 * d + 3 * aw
    ff = w_up.shape[2] * N_DEV
    ple = p.shape[-1]
    groups = d // S5_GROUP
    me = 4 * lax.axis_index("x") + 2 * lax.axis_index("y") + lax.axis_index("c")
    x2, target = x[0], loss_target[0]
    row = lambda a, i: a[i:i + 1]

    def gather_of(w):
        return _Gather(w.astype(BF16))

    w_in = jnp.transpose(_comm_call(gather_of(w_in_e[0]), name="ag_w_in"), (1, 0, 2)).reshape(d, -1)
    w_main = w_in[:, :main]
    w_tail = jnp.pad(w_in[:, main:], ((0, 0), (0, HEAD - 2 * hb)))

    lb_rows = [row(hgrn_lb, 0), row(hgrn_lb, 1), row(hgrn_lb, 2)]
    (lb0,) = _small_call(_lb0_stage, lb_rows, name="f_lb0")
    hp = jnp.zeros((8, HEAD), F32).at[0, :hb].set(a_log[0]).at[1, :hb].set(dt_bias[0])
    expand = jnp.asarray(np.kron(np.eye(S5_STATE, dtype=np.float32), np.ones((1, S5_GROUP), np.float32)))
    prep_in = [s5_a_re[0], s5_a_im[0], s5_log_dt[0].reshape(groups, 1),
               s5_b_re[0].reshape(groups, -1), s5_b_im[0].reshape(groups, -1), expand]
    lr, li, br, bi = _small_call(_s5_prep_stage, prep_in, name="f_s5_prep")
    wb, wc, lt = _s5_pack(lr, li, br, bi, s5_c_re[0], s5_c_im[0])
    wb, wc = wb.astype(BF16), wc.astype(BF16)
    fnorm = final_norm.reshape(1, d)

    w_upg = []

    def block_fwd(h, l):
        hn = _rows_call(_rms_stage, [h], [row(norm_mlp, l)], [BF16], name=f"f_norm_mlp{l}")
        (up, act), (dn8,) = _mm(hn, w_upg[l], epilogue=_relu2_epilogue, out_dtypes=(F32, BF16), name=f"f_up{l}",
                                comm=(gather_of(w_down[l]),))
        w_dn = dn8.reshape(ff, d)
        h2 = _mm(act, w_dn, extras=(h,), epilogue=_add_epilogue, name=f"f_down{l}")
        hq = _rows_call(_rms_stage, [h2], [row(norm_ple, l)], [BF16], name=f"f_norm_ple{l}")
        gpre = _mm(hq, w_pgg[l], name=f"f_ple_gate{l}")
        h3, pp = _mm(p[l, 0], w_ppg[l], extras=(gpre, h2), epilogue=_ple_epilogue, out_dtypes=(F32, F32),
                     name=f"f_ple_proj{l}")
        return h3, dict(h=h, hn=hn, up=up, act=act, h2=h2, hq=hq, gpre=gpre, pp=pp, w_dn=w_dn)

    hn0 = _rows_call(_rms_stage, [x2], [row(norm_mix, 0)], [BF16], name="f_norm_mix0")
    shard_shapes = [conv_w[0].shape, s5_d.shape, b_glu.shape]
    proj, (oe8, pg8, small) = _mm(hn0, w_main, name="f_proj", comm=(
        gather_of(w_out_e[0]), gather_of(w_ple_gate), _Gather(_pack([conv_w[0], s5_d, b_glu]))))
    w_oe = oe8.reshape(d, d)
    w_top, w_bot = w_oe[:aw], w_oe[aw:]
    w_pgg = jnp.transpose(pg8, (1, 0, 2, 3)).reshape(2, d, d)
    conv_g, s5d_g, bglu_g = zip(*[_unpack(small[j], shard_shapes) for j in range(N_DEV)])
    conv_full = jnp.concatenate(conv_g, axis=1)
    s5d_full = jnp.concatenate(s5d_g, axis=1)
    bglu_full = jnp.concatenate(bglu_g, axis=1)
    ab = _mm(hn0, w_tail, name="f_ab")
    (oa, st_a), (gl8, oo8) = _hgrn_fwd(proj, lb0, g_norm_a, heads=ha, name="f_hgrn",
                                       comm=(gather_of(w_glu[0]), gather_of(w_out_o[0])))
    w_gl, w_oo = gl8.reshape(d, d), oo8.reshape(d, d)
    qkv = _conv_fwd(proj, conv_full, col_off=2 * d, name="f_conv")
    slots_to_cols = lambda g8: jnp.transpose(g8, (1, 0, 2)).reshape(g8.shape[1], -1)
    (ob, st_b), (up8,) = _delta_fwd(qkv, ab, proj, hp, g_norm_b, heads=hb, z_off=z_col // HEAD, name="f_delta",
                                    comm=(gather_of(w_up[0]),))
    w_upg.append(slots_to_cols(up8))
    merged = jnp.concatenate([oa, ob], axis=1)
    h1, (pp8,) = _mm(merged, w_oe, extras=(x2,), epilogue=_add_epilogue, name="f_out_e", comm=(gather_of(w_ple_proj),))
    w_ppg = jnp.transpose(pp8, (1, 2, 0, 3)).reshape(2, ple, d)
    h3, sv0 = block_fwd(h1, 0)

    u = _rows_call(_rms_stage, [h3], [row(norm_mix, 1)], [F32], name="f_norm_mix1")
    (y, cins, s5_states, act_g), (up8,) = _s5_fwd(u, wb, wc, lt, s5d_full, name="f_s5", comm=(gather_of(w_up[1]),))
    w_upg.append(slots_to_cols(up8))
    gl_raw = _mm(act_g, w_gl, name="f_glu")
    glu = _rows_call(_glu_stage, [y, gl_raw], [bglu_full], [BF16], name="f_glu_gate")
    h4 = _mm(glu, w_oo, extras=(h3,), epilogue=_add_epilogue, name="f_out_o")
    h6, sv1 = block_fwd(h4, 1)
    dh, d_fnorm, loss8 = _loss_call(h6, fnorm, target, name="loss")
    loss = lax.psum(loss8[0, 0], ("x", "y", "c"))

    dshard, ffs, cols = d // N_DEV, ff // N_DEV, w_in_e.shape[2]
    rows8 = lambda g: _AllToAll(g.reshape(N_DEV, -1, g.shape[-1]))
    col_slots = lambda g: jnp.transpose(g.reshape(g.shape[0], N_DEV, -1), (1, 0, 2))
    cols8 = lambda g: _AllToAll(col_slots(g))

    def halves(g8):
        r = g8.shape[1] // 2
        return _AllToAll(g8, rows=(0, r)), _AllToAll(g8, rows=(r, r))

    def block_bwd(dh3, l, sv, carried=(), carried_wup=(), carried_up=()):
        (dgpre, dpp), _ = _rows_vjp(_ple_stage, [sv["h2"], sv["gpre"], sv["pp"]], [], [dh3],
                                    row_grads={1: BF16, 2: BF16}, name=f"b_ple{l}")
        g_pp = _mm(p[l, 0], dpp, ta=True, out_dtypes=(BF16,), name=f"b_w_ple_proj{l}")
        g_pg = _mm(sv["hq"], dgpre, ta=True, out_dtypes=(BF16,), name=f"b_w_ple_gate{l}")
        dhq = _mm(dgpre, w_pgg[l], tb=True, name=f"b_ple_gate{l}")
        (dh2, dh2_mm), (g_nple,) = _rows_vjp(_rms_stage, [sv["h2"]], [row(norm_ple, l)], [dhq],
                                             row_grads={0: (F32, MM_DTYPE)}, adds={0: dh3}, name=f"b_norm_ple{l}")
        dup, (r_pg, r_pp) = _mm(dh2_mm, sv["w_dn"], tb=True, extras=(sv["up"],), epilogue=_relu2_grad_epilogue,
                                out_dtypes=(BF16,), name=f"b_down{l}", comm=(rows8(g_pg), cols8(g_pp)))
        g_dn = _mm(sv["act"], dh2_mm, ta=True, out_dtypes=(BF16,), name=f"b_w_down{l}", comm=carried)
        g_dn, r_carried = g_dn if carried else (g_dn, [])
        g_up = _mm(sv["hn"], dup, ta=True, out_dtypes=(BF16,), tn=ffs, out_slots=True, name=f"b_w_up{l}",
                   comm=carried_wup)
        g_up, r_carried_wup = g_up if carried_wup else (g_up, [])
        dhn = _mm(dup, w_upg[l], tb=True, name=f"b_up{l}", comm=carried_up)
        dhn, r_carried_up = dhn if carried_up else (dhn, [])
        (dh0, dh0_mm), (g_nmlp,) = _rows_vjp(_rms_stage, [sv["h"]], [row(norm_mlp, l)], [dhn],
                                             row_grads={0: (F32, MM_DTYPE)}, adds={0: dh2}, name=f"b_norm_mlp{l}")
        return dh0, dict(w_ple_proj=r_pp, w_ple_gate=r_pg, norm_ple=g_nple, w_down=g_dn, w_up=g_up, norm_mlp=g_nmlp,
                         carried=r_carried, carried_wup=r_carried_wup, carried_up=r_carried_up, dh_mm=dh0_mm)

    dh4, gb1 = block_bwd(dh, 1, sv1)
    up1_a, up1_b = halves(gb1["w_up"])
    dglu = _mm(gb1["dh_mm"], w_oo, tb=True, name="b_out_o")
    g_oo = _mm(glu, gb1["dh_mm"], ta=True, out_dtypes=(BF16,), name="b_w_out_o")
    (dy1, dgl), (g_bglu,) = _rows_vjp(_glu_stage, [y, gl_raw], [bglu_full], [dglu], row_grads={0: F32, 1: BF16},
                                      name="b_glu_gate")
    g_gl = _mm(act_g, dgl, ta=True, out_dtypes=(BF16,), name="b_w_glu")
    dact = _mm(dgl, w_gl, tb=True, name="b_glu")
    (dy,), _ = _rows_vjp(_gelu_stage, [y], [], [dact], row_grads={0: F32}, adds={0: dy1}, name="b_gelu")
    (du, dwb, dwc, g_s5d, dlam), (r_dn1, r_up1_a) = _s5_bwd(u, dy, wb, wc, lt, s5d_full, cins, s5_states, name="b_s5",
                                                           comm=(rows8(gb1["w_down"]), up1_a))
    (dh3,), (g_nmix1,) = _rows_vjp(_rms_stage, [h3], [row(norm_mix, 1)], [du], row_grads={0: F32}, adds={0: dh4},
                                   name="b_norm_mix1")
    dlr, dli, dbr, dbi, g_cre, g_cim = _s5_unpack(dwb, dwc, dlam)
    g_are, g_aim, g_ldt, g_bre, g_bim, _ = _small_vjp(_s5_prep_stage, prep_in, [dlr, dli, dbr, dbi], name="b_s5_prep")

    early_grads = dict(
        s5_a_re=g_are[None], s5_a_im=g_aim[None], s5_b_re=g_bre.reshape(s5_b_re.shape),
        s5_b_im=g_bim.reshape(s5_b_im.shape), s5_c_re=g_cre[None], s5_c_im=g_cim[None],
        s5_log_dt=g_ldt.reshape(1, groups), final_norm=d_fnorm.reshape(d), s5_d=g_s5d, b_glu=g_bglu)
    dh1, gb0 = block_bwd(dh3, 0, sv0, (rows8(g_oo), rows8(g_gl)), (up1_b,),
                         (_Gather(_pack(list(early_grads.values()))),))
    r_oo, r_gl = gb0["carried"]
    r_up1 = [r_up1_a, gb0["carried_wup"][0]]
    (early_parts,) = gb0["carried_up"]
    dn0_a, dn0_b = halves(gb0["w_down"].reshape(N_DEV, ffs, d))
    up0_a, up0_b = halves(gb0["w_up"])
    dh1_mm = gb0["dh_mm"]
    doa = _mm(dh1_mm, w_top, tb=True, name="b_out_a")
    dob = _mm(dh1_mm, w_bot, tb=True, name="b_out_b")
    g_oe = _mm(merged, dh1_mm, ta=True, out_dtypes=(BF16,), name="b_w_out_e")
    (dq, df, di, dg, dlb, g_gna), (r_dn0_a,) = _hgrn_bwd(proj, lb0, g_norm_a, st_a, doa, heads=ha, name="b_hgrn",
                                                        comm=(dn0_a,))
    (dqb, dkb, dvb, dab, dz, dhp, g_gnb), (r_dn0_b, r_up0_a) = _delta_bwd(
        qkv, ab, proj, hp, g_norm_b, st_b, dob, heads=hb, z_off=z_col // HEAD, name="b_delta", comm=(dn0_b, up0_a))
    (dqkv, g_conv), (r_oe,) = _conv_bwd(proj, conv_full, jnp.concatenate([dqb, dkb, dvb], axis=1), col_off=2 * d,
                                        name="b_conv", comm=(rows8(g_oe),))
    dproj = jnp.concatenate([dq, df, di, dg, dqkv, dz], axis=1)
    g_main, (r_up0_b,) = _mm(hn0, dproj, ta=True, out_dtypes=(BF16,), name="b_w_proj", comm=(up0_b,))
    r_dn0, r_up0 = [r_dn0_a, r_dn0_b], [r_up0_a, r_up0_b]
    g_tail = _mm(hn0, dab, ta=True, out_dtypes=(BF16,), name="b_w_ab")
    in_a, in_b = halves(col_slots(jnp.concatenate([g_main, g_tail[:, :2 * hb]], axis=1)))
    dhn0, (r_in_a,) = _mm(dproj, w_main, tb=True, name="b_proj", comm=(in_a,))
    dhn0 = _mm(dab, w_tail, tb=True, extras=(dhn0,), epilogue=_add_epilogue, name="b_ab")
    (dx,), (g_nmix0,) = _rows_vjp(_rms_stage, [x2], [row(norm_mix, 0)], [dhn0], row_grads={0: F32}, adds={0: dh1},
                                  name="b_norm_mix0")
    g_lb = jnp.concatenate(_small_vjp(_lb0_stage, lb_rows, [dlb], name="b_lb0"), axis=0)

    late_grads = dict(
        norm_mix=jnp.concatenate([g_nmix0, g_nmix1], axis=0),
        norm_mlp=jnp.concatenate([gb0["norm_mlp"], gb1["norm_mlp"]], axis=0),
        norm_ple=jnp.concatenate([gb0["norm_ple"], gb1["norm_ple"]], axis=0),
        hgrn_lb=g_lb, g_norm_a=g_gna, a_log=dhp[0:1, :hb], dt_bias=dhp[1:2, :hb], g_norm_b=g_gnb, conv_w=g_conv)
    rep_names = ["norm_mix", "norm_mlp", "norm_ple", "hgrn_lb", "g_norm_a", "a_log", "dt_bias", "g_norm_b", "s5_a_re",
                 "s5_a_im", "s5_b_re", "s5_b_im", "s5_c_re", "s5_c_im", "s5_log_dt", "final_norm"]
    late_parts = _comm_call(_Gather(_pack(list(late_grads.values()))), name="ag_small_grads")
    summed = {}
    for tag, grads, parts in (("early", early_grads, early_parts), ("late", late_grads, late_parts)):
        sums = _unpack(_sum_parts(parts, name=f"sum_small_grads_{tag}"), [g.shape for g in grads.values()])
        summed.update(zip(grads, sums))
    cw = conv_w.shape[2]
    dshard = d // N_DEV
    shard_g = dict(conv_w=lax.dynamic_slice(summed["conv_w"], (0, me * cw), (CONV_WIDTH, cw))[None],
                   s5_d=lax.dynamic_slice(summed["s5_d"], (0, me * dshard), (1, dshard)),
                   b_glu=lax.dynamic_slice(summed["b_glu"], (0, me * dshard), (1, dshard)))
    small_names = rep_names + ["conv_w", "s5_d", "b_glu"]
    g_small = [summed[k] if k in rep_names else shard_g[k] for k in small_names]
    shapes = [args[k].shape for k in small_names]
    sm_out = _adamw(_pack([args[k] for k in small_names])[None], [_pack(g_small)[None]],
                    _pack([args["m_" + k] for k in small_names])[None], _pack([args["v_" + k] for k in small_names])[None],
                    name="adamw_small")
    sm_out = [dict(zip(small_names, _unpack(o[0], shapes))) for o in sm_out]

    up_out, (r_in_b,) = _adamw(w_up, [r_up0, r_up1], m_w_up, v_w_up, name="adamw_w_up", comm=(in_b,))
    received = dict(w_in_e=[[r_in_a, r_in_b]], w_out_e=[r_oe], w_glu=[r_gl], w_out_o=[r_oo],
                    w_down=[r_dn0, r_dn1], w_ple_gate=[gb0["w_ple_gate"], gb1["w_ple_gate"]],
                    w_ple_proj=[gb0["w_ple_proj"], gb1["w_ple_proj"]])
    big_out = {k: _adamw(args[k], layers, args["m_" + k], args["v_" + k], name="adamw_" + k)
               for k, layers in received.items()}
    big_out["w_up"] = up_out

    names = ["norm_mix", "norm_mlp", "norm_ple", "w_in_e", "w_out_e", "hgrn_lb", "g_norm_a", "conv_w", "a_log", "dt_bias",
             "g_norm_b", "s5_a_re", "s5_a_im", "s5_b_re", "s5_b_im", "s5_c_re", "s5_c_im", "s5_d", "s5_log_dt", "w_glu",
             "b_glu", "w_out_o", "w_up", "w_down", "w_ple_gate", "w_ple_proj", "final_norm"]
    result = [loss, dx[None]]
    for j in range(4):
        result += [big_out[k][j] if k in big_out else sm_out[j][k] for k in names]
    return tuple(result)
```

```python
import functools
import math
import operator

import numpy as np
import jax
import jax.numpy as jnp
from jax import lax
from jax.experimental import pallas as pl
from jax.experimental.pallas import tpu as pltpu

F32 = jnp.float32
BF16 = jnp.bfloat16
MM_DTYPE = BF16
HI = lax.Precision.HIGHEST
MESH = pl.DeviceIdType.MESH

NORM_EPS = 1e-6
CHUNK = 64
HEAD = 128
CONV_WIDTH = 4
S5_GROUP = 16
S5_STATE = 64
S5_GB = 8
S5_HALF = S5_GB * S5_STATE
N_DEV = 8
HEADS_PER_STEP = 8
HGRN_SUB = 16
ADAM_LR, ADAM_B1, ADAM_B2, ADAM_EPS, ADAM_WD, ADAM_STEP = 0.001, 0.9, 0.999, 1e-08, 0.01, 10
VMEM_LIMIT = 56 * 1024 * 1024

NN = (((1,), (0,)), ((), ()))
NT = (((1,), (1,)), ((), ()))
TN = (((0,), (0,)), ((), ()))


def _dot(a, b, dn=NN):
    return lax.dot_general(a, b, dn, precision=HI, preferred_element_type=F32)


def _hdot(a, b, dn=NN):
    return lax.dot_general(a, b, dn, precision=lax.Precision.HIGH, preferred_element_type=F32)


def _bdot_raw(a, b, dn=NN):
    return lax.dot_general(a.astype(BF16), b.astype(BF16), dn, preferred_element_type=F32)


@functools.partial(jax.custom_vjp, nondiff_argnums=(2,))
def _bdot(a, b, dn):
    return _bdot_raw(a, b, dn)


def _bdot_fwd(a, b, dn):
    return _bdot_raw(a, b, dn), (a, b)


def _bdot_bwd(dn, res, g):
    a, b = res
    if dn == NN:
        return _bdot_raw(g, b, NT), _bdot_raw(a, g, TN)
    if dn == NT:
        return _bdot_raw(g, b, NN), _bdot_raw(g, a, TN)
    assert dn == TN
    return _bdot_raw(b, g, NT), _bdot_raw(a, g, NN)


_bdot.defvjp(_bdot_fwd, _bdot_bwd)


def _per_head(f):
    def g(*args, **kw):
        n = [len(a.vals) for a in args if isinstance(a, _Heads)]
        if not n:
            return f(*args, **kw)
        return _Heads([f(*[a.vals[j] if isinstance(a, _Heads) else a for a in args], **kw) for j in range(n[0])])
    return g


class _Heads:
    def __init__(self, vals):
        self.vals = list(vals)

    def __add__(self, o):
        return _per_head(operator.add)(self, o)

    def __radd__(self, o):
        return _per_head(operator.add)(o, self)

    def __sub__(self, o):
        return _per_head(operator.sub)(self, o)

    def __rsub__(self, o):
        return _per_head(operator.sub)(o, self)

    def __mul__(self, o):
        return _per_head(operator.mul)(self, o)

    def __rmul__(self, o):
        return _per_head(operator.mul)(o, self)

    def __neg__(self):
        return _per_head(operator.neg)(self)


_exp, _log, _where, _sum, _mean = (_per_head(f) for f in (jnp.exp, jnp.log, jnp.where, jnp.sum, jnp.mean))
_sigmoid, _rsqrt, _equal = _per_head(jax.nn.sigmoid), _per_head(lax.rsqrt), _per_head(operator.eq)
_hdot_h, _bdot_h = _per_head(_hdot), _per_head(_bdot)
_rows = _per_head(lambda a, lo, n: a[lo:lo + n, :])
_row_concat = _per_head(lambda *xs: jnp.concatenate(xs, axis=0))


def _params(n_axes):
    return pltpu.CompilerParams(dimension_semantics=("arbitrary",) * n_axes, vmem_limit_bytes=VMEM_LIMIT)


def _full_spec(a):
    nd = a.ndim
    return pl.BlockSpec(a.shape, lambda *_: (0,) * nd)


def _mm(a, b, *, name, ta=False, tb=False, extras=(), epilogue=None, out_dtypes=(F32,), tm=1024, tn=1024, tk=2048,
        out_slots=False, comm=()):
    m = a.shape[1] if ta else a.shape[0]
    k = a.shape[0] if ta else a.shape[1]
    n = b.shape[0] if tb else b.shape[1]
    assert k == (b.shape[1] if tb else b.shape[0]), (name, a.shape, b.shape)
    tm, tn, tk = min(tm, m), min(tn, n), min(tk, k)
    assert m % tm == 0 and n % tn == 0 and k % tk == 0, (name, m, n, k)
    nk = k // tk
    n_ex, n_out = len(extras), len(out_dtypes)
    dn = (((0 if ta else 1,), (1 if tb else 0,)), ((), ()))

    def body(a_ref, b_ref, *rest):
        ex_refs, out_refs = rest[:n_ex], rest[n_ex:n_ex + n_out]
        part = lax.dot_general(a_ref[...].astype(MM_DTYPE), b_ref[...].astype(MM_DTYPE), dn, preferred_element_type=F32)

        def finish(acc):
            outs = epilogue(acc, *[r[...] for r in ex_refs]) if epilogue is not None else (acc,)
            for o_ref, o in zip(out_refs, outs):
                o_ref[...] = o.astype(o_ref.dtype)

        if nk == 1:
            finish(part)
            return
        acc_ref = rest[-1]
        kk = pl.program_id(2)

        @pl.when(kk == 0)
        def _():
            acc_ref[...] = part

        @pl.when((kk > 0) & (kk < nk - 1))
        def _():
            acc_ref[...] += part

        @pl.when(kk == nk - 1)
        def _():
            finish(acc_ref[...] + part)

    a_spec = pl.BlockSpec((tk, tm), lambda i, j, q: (q, i)) if ta else pl.BlockSpec((tm, tk), lambda i, j, q: (i, q))
    b_spec = pl.BlockSpec((tn, tk), lambda i, j, q: (j, q)) if tb else pl.BlockSpec((tk, tn), lambda i, j, q: (q, j))
    ex_specs = []
    for e in extras:
        if e.shape[0] == 1 and m != 1:
            ex_specs.append(pl.BlockSpec((1, tn), lambda i, j, q: (0, j)))
        else:
            ex_specs.append(pl.BlockSpec((tm, tn), lambda i, j, q: (i, j)))
    if out_slots:
        out_spec, out_dims = pl.BlockSpec((None, tm, tn), lambda i, j, q: (j, i, 0)), (n // tn, m, tn)
    else:
        out_spec, out_dims = pl.BlockSpec((tm, tn), lambda i, j, q: (i, j)), (m, n)
    outs, exchanged = _call(
        body, (a, b, *extras), name=name, grid=(m // tm, n // tn, nk),
        in_specs=[a_spec, b_spec] + ex_specs,
        out_specs=[out_spec for _ in out_dtypes],
        out_shape=[jax.ShapeDtypeStruct(out_dims, dt) for dt in out_dtypes],
        scratch_shapes=[pltpu.VMEM((tm, tn), F32)] if nk > 1 else [], comm=comm)
    outs = outs[0] if n_out == 1 else tuple(outs)
    return (outs, exchanged) if comm else outs


def _rows_call(fn, rows, consts, out_dtypes, *, name, tr=256):
    s = rows[0].shape[0]
    tr = min(tr, s)
    nr, nc = len(rows), len(consts)
    widths = [o.shape[1] for o in jax.eval_shape(
        fn, *[jax.ShapeDtypeStruct((tr, r.shape[1]), F32) for r in rows],
        *[jax.ShapeDtypeStruct(c.shape, F32) for c in consts])]

    def body(*refs):
        rv = [r[...].astype(F32) for r in refs[:nr]]
        cv = [c[...] for c in refs[nr:nr + nc]]
        for o_ref, o in zip(refs[nr + nc:], fn(*rv, *cv)):
            o_ref[...] = o.astype(o_ref.dtype)

    outs = pl.pallas_call(
        body, name=name, grid=(s // tr,),
        in_specs=[pl.BlockSpec((tr, r.shape[1]), lambda i: (i, 0)) for r in rows] + [_full_spec(c) for c in consts],
        out_specs=[pl.BlockSpec((tr, w), lambda i: (i, 0)) for w in widths],
        out_shape=[jax.ShapeDtypeStruct((s, w), dt) for w, dt in zip(widths, out_dtypes)],
        compiler_params=_params(1),
    )(*rows, *consts)
    return outs[0] if len(outs) == 1 else tuple(outs)


def _rows_vjp(fn, rows, consts, cots, *, name, row_grads, adds=None, tr=256):
    adds = adds or {}
    s = rows[0].shape[0]
    tr = min(tr, s)
    nr, nc, nt = len(rows), len(consts), len(cots)
    rg = [(i, dt) for i in sorted(row_grads)
          for dt in (row_grads[i] if isinstance(row_grads[i], tuple) else (row_grads[i],))]
    ad = sorted(adds)

    def body(*refs):
        rv = [r[...].astype(F32) for r in refs[:nr]]
        cv = [c[...] for c in refs[nr:nr + nc]]
        ct = [c[...].astype(F32) for c in refs[nr + nc:nr + nc + nt]]
        av = {i: r[...].astype(F32) for i, r in zip(ad, refs[nr + nc + nt:nr + nc + nt + len(ad)])}
        out_refs = refs[nr + nc + nt + len(ad):]
        _, vjp = jax.vjp(fn, *rv, *cv)
        grads = vjp(tuple(ct))
        for o_ref, (i, _) in zip(out_refs[:len(rg)], rg):
            g = grads[i]
            if i in av:
                g = g + av[i]
            o_ref[...] = g.astype(o_ref.dtype)

        @pl.when(pl.program_id(0) == 0)
        def _():
            for o_ref in out_refs[len(rg):]:
                o_ref[...] = jnp.zeros_like(o_ref)

        for o_ref, g in zip(out_refs[len(rg):], grads[nr:]):
            o_ref[...] += g

    row_spec = lambda a: pl.BlockSpec((tr, a.shape[1]), lambda i: (i, 0))
    outs = pl.pallas_call(
        body, name=name, grid=(s // tr,),
        in_specs=[row_spec(r) for r in rows] + [_full_spec(c) for c in consts] + [row_spec(c) for c in cots]
        + [row_spec(adds[i]) for i in ad],
        out_specs=[row_spec(rows[i]) for i, _ in rg] + [_full_spec(c) for c in consts],
        out_shape=[jax.ShapeDtypeStruct(rows[i].shape, dt) for i, dt in rg]
        + [jax.ShapeDtypeStruct(c.shape, F32) for c in consts],
        compiler_params=_params(1),
    )(*rows, *consts, *cots, *[adds[i] for i in ad])
    return list(outs[:len(rg)]), list(outs[len(rg):])


def _small_call(fn, ins, *, name):
    shapes = jax.eval_shape(fn, *[jax.ShapeDtypeStruct(a.shape, F32) for a in ins])

    def body(*refs):
        for o_ref, o in zip(refs[len(ins):], fn(*[r[...] for r in refs[:len(ins)]])):
            o_ref[...] = o

    return pl.pallas_call(
        body, name=name, in_specs=[_full_spec(a) for a in ins],
        out_specs=[pl.BlockSpec(o.shape, functools.partial(lambda nd, *_: (0,) * nd, len(o.shape))) for o in shapes],
        out_shape=[jax.ShapeDtypeStruct(o.shape, F32) for o in shapes], grid=(1,),
        compiler_params=_params(1),
    )(*ins)


def _small_vjp(fn, ins, cots, *, name):
    def body(*refs):
        vals = [r[...] for r in refs[:len(ins)]]
        ct = [r[...] for r in refs[len(ins):len(ins) + len(cots)]]
        _, vjp = jax.vjp(fn, *vals)
        for o_ref, g in zip(refs[len(ins) + len(cots):], vjp(tuple(ct))):
            o_ref[...] = g

    return pl.pallas_call(
        body, name=name, in_specs=[_full_spec(a) for a in ins] + [_full_spec(c) for c in cots],
        out_specs=[_full_spec(a) for a in ins],
        out_shape=[jax.ShapeDtypeStruct(a.shape, F32) for a in ins], grid=(1,),
        compiler_params=_params(1),
    )(*ins, *cots)


def _rms(x, g):
    return x * _rsqrt(_mean(x * x, axis=-1, keepdims=True) + NORM_EPS) * g


def _rms_stage(x, g):
    return (_rms(x, g),)


def _silu(x):
    return x * _sigmoid(x)


def _softplus(x):
    return jnp.maximum(x, 0.0) + jnp.log1p(jnp.exp(-jnp.abs(x)))


def _gelu(x):
    return jax.nn.gelu(x, approximate=True)


def _gelu_stage(y):
    return (_gelu(y),)


def _glu_stage(y, gl_raw, b):
    return (_gelu(y) * jax.nn.sigmoid(gl_raw + b),)


def _ple_stage(h, gpre, pp):
    return (h + jax.nn.sigmoid(gpre) * pp,)


def _relu2_grad_epilogue(acc, up):
    return (acc * (2.0 * jnp.maximum(up, 0.0)),)


def _lb0_stage(x0, x1, x2):
    mx = jnp.maximum(jnp.maximum(x0, x1), x2)
    e0, e1, e2 = jnp.exp(x0 - mx), jnp.exp(x1 - mx), jnp.exp(x2 - mx)
    return (e0 / (e0 + e1 + e2),)


def _s5_prep_stage(a_re, a_im, log_dt, b_re, b_im, expand):
    step = jnp.exp(log_dt)
    mag = jnp.exp(a_re * step)
    lr = mag * jnp.cos(a_im * step)
    li = mag * jnp.sin(a_im * step)
    den = a_re * a_re + a_im * a_im
    cr = ((lr - 1.0) * a_re + li * a_im) / den
    ci = (li * a_re - (lr - 1.0) * a_im) / den
    cr_e, ci_e = _dot(cr, expand), _dot(ci, expand)
    return lr, li, cr_e * b_re - ci_e * b_im, cr_e * b_im + ci_e * b_re


def _loss_call(h, g, target, *, name, tr=256):
    s, d = h.shape
    tr = min(tr, s)

    def loss_fn(hv, gv, tv):
        err = _rms(hv, gv) - tv
        return 0.5 * jnp.sum(jnp.mean(err * err, axis=-1))

    def body(h_ref, g_ref, t_ref, dh_ref, dg_ref, loss_ref):
        val, (dh, dg) = jax.value_and_grad(loss_fn, argnums=(0, 1))(h_ref[...], g_ref[...], t_ref[...])
        dh_ref[...] = dh

        @pl.when(pl.program_id(0) == 0)
        def _():
            dg_ref[...] = jnp.zeros_like(dg_ref)
            loss_ref[...] = jnp.zeros_like(loss_ref)

        dg_ref[...] += dg
        loss_ref[...] += jnp.full(loss_ref.shape, val, F32)

    row = pl.BlockSpec((tr, d), lambda i: (i, 0))
    return pl.pallas_call(
        body, name=name, grid=(s // tr,),
        in_specs=[row, _full_spec(g), row],
        out_specs=[row, _full_spec(g), pl.BlockSpec((8, 128), lambda i: (0, 0))],
        out_shape=[jax.ShapeDtypeStruct((s, d), F32), jax.ShapeDtypeStruct(g.shape, F32),
                   jax.ShapeDtypeStruct((8, 128), F32)],
        compiler_params=_params(1),
    )(h, g, target)


def _hgrn_chunk(q, fp, iv, gp, lb, gn, st_t):
    c = CHUNK
    row = lax.broadcasted_iota(jnp.int32, (c, c), 0)
    col = lax.broadcasted_iota(jnp.int32, (c, c), 1)
    causal = row >= col
    fg = lb + (1.0 - lb) * _sigmoid(fp)
    k = 1.0 - fg
    lf = _log(fg)
    cum = _hdot_h(causal.astype(F32), lf, NN)
    cend = _sum(lf, axis=0, keepdims=True)
    shift = HGRN_SUB.bit_length() - 1
    ref = _hdot_h(((row >> shift) > (col >> shift)).astype(F32), lf, NN)
    q_dec = q * _exp(cum - ref)
    key_row = lax.broadcasted_iota(jnp.int32, (c, 1), 0)
    blocks = []
    for lo in range(0, c, HGRN_SUB):
        live = key_row < lo + HGRN_SUB
        k_dec = _where(live, k * _exp(_where(live, _rows(ref, lo, 1) - cum, 0.0)), 0.0)
        blocks.append(_hdot_h(_rows(q_dec, lo, HGRN_SUB), k_dec, NT))
    scores = _where(causal, _row_concat(*blocks), 0.0)
    out = _bdot_h(scores, iv, NN) + _bdot_h(q * _exp(cum), st_t, NT)
    st_new = st_t * _exp(cend) + _bdot_h(iv, k * _exp(cend - cum), TN)
    res = _rms(out, gn) * _silu(gp)
    return res, st_new


def _hgrn_heads(qs, fs, ivs, gs, lbs, gn, sts):
    res, st_new = _hgrn_chunk(_Heads(qs), _Heads(fs), _Heads(ivs), _Heads(gs), _Heads(lbs), gn, _Heads(sts))
    return res.vals, st_new.vals


def _lanes(j):
    return slice(j * HEAD, (j + 1) * HEAD)


def _hgrn_fwd(proj, lb, gn, *, heads, name, comm=()):
    s = proj.shape[0]
    n = s // CHUNK
    hpb = min(HEADS_PER_STEP, heads)
    assert heads % hpb == 0

    def body(q_ref, f_ref, i_ref, g_ref, lb_ref, gn_ref, o_ref, st_ref, state):
        @pl.when(pl.program_id(1) == 0)
        def _():
            state[...] = jnp.zeros_like(state)

        gnv = gn_ref[...]
        loaded = [(q_ref[:, _lanes(j)], f_ref[:, _lanes(j)], i_ref[:, _lanes(j)], g_ref[:, _lanes(j)],
                   lb_ref[:, _lanes(j)], state[j]) for j in range(hpb)]
        qs, fs, ivs, gs, lbs, sts = (list(t) for t in zip(*loaded))
        res, st_new = _hgrn_heads(qs, fs, ivs, gs, lbs, gnv, sts)
        for j in range(hpb):
            st_ref[j] = sts[j]
            o_ref[:, _lanes(j)] = res[j].astype(o_ref.dtype)
            state[j] = st_new[j]

    wide = hpb * HEAD
    blk = lambda off: pl.BlockSpec((CHUNK, wide), lambda h, c: (c, off // hpb + h))
    return _call(
        body, (proj, proj, proj, proj, lb, gn), name=name, grid=(heads // hpb, n),
        in_specs=[blk(0), blk(heads), blk(2 * heads), blk(3 * heads),
                  pl.BlockSpec((1, wide), lambda h, c: (0, h)), pl.BlockSpec((1, HEAD), lambda h, c: (0, 0))],
        out_specs=[pl.BlockSpec((CHUNK, wide), lambda h, c: (c, h)),
                   pl.BlockSpec((hpb, None, HEAD, HEAD), lambda h, c: (h, c, 0, 0))],
        out_shape=[jax.ShapeDtypeStruct((s, heads * HEAD), BF16), jax.ShapeDtypeStruct((heads, n, HEAD, HEAD), F32)],
        scratch_shapes=[pltpu.VMEM((hpb, HEAD, HEAD), F32)], comm=comm)


def _hgrn_bwd(proj, lb, gn, states, d_out, *, heads, name, comm=()):
    s = proj.shape[0]
    n = s // CHUNK
    hpb = min(HEADS_PER_STEP, heads)

    def body(q_ref, f_ref, i_ref, g_ref, lb_ref, gn_ref, st_ref, do_ref,
             dq_ref, df_ref, di_ref, dg_ref, dlb_ref, dgn_ref, dstate):
        h, c = pl.program_id(0), pl.program_id(1)

        @pl.when(c == 0)
        def _():
            dstate[...] = jnp.zeros_like(dstate)
            dlb_ref[...] = jnp.zeros_like(dlb_ref)

        @pl.when((c == 0) & (h == 0))
        def _():
            dgn_ref[...] = jnp.zeros_like(dgn_ref)

        gnv = gn_ref[...]
        loaded = [(q_ref[:, _lanes(j)], f_ref[:, _lanes(j)], i_ref[:, _lanes(j)], g_ref[:, _lanes(j)],
                   lb_ref[:, _lanes(j)], st_ref[j], do_ref[:, _lanes(j)].astype(F32), dstate[j]) for j in range(hpb)]
        qs, fs, ivs, gs, lbs, sts, dos, dss = (list(t) for t in zip(*loaded))
        _, vjp = jax.vjp(_hgrn_heads, qs, fs, ivs, gs, lbs, gnv, sts)
        dqs, dfs, dis, dgs, dlbs, dgn_sum, dsts = vjp((dos, dss))
        for j in range(hpb):
            ln = _lanes(j)
            dq_ref[:, ln] = dqs[j].astype(dq_ref.dtype)
            df_ref[:, ln] = dfs[j].astype(df_ref.dtype)
            di_ref[:, ln] = dis[j].astype(di_ref.dtype)
            dg_ref[:, ln] = dgs[j].astype(dg_ref.dtype)
            dlb_ref[:, ln] += dlbs[j]
            dstate[j] = dsts[j]
        dgn_ref[...] += dgn_sum

    wide = hpb * HEAD
    rev = lambda off: pl.BlockSpec((CHUNK, wide), lambda h, c: (n - 1 - c, off // hpb + h))
    out_blk = pl.BlockSpec((CHUNK, wide), lambda h, c: (n - 1 - c, h))
    width = heads * HEAD
    return _call(
        body, (proj, proj, proj, proj, lb, gn, states, d_out), name=name, grid=(heads // hpb, n),
        in_specs=[rev(0), rev(heads), rev(2 * heads), rev(3 * heads),
                  pl.BlockSpec((1, wide), lambda h, c: (0, h)), pl.BlockSpec((1, HEAD), lambda h, c: (0, 0)),
                  pl.BlockSpec((hpb, None, HEAD, HEAD), lambda h, c: (h, n - 1 - c, 0, 0)), out_blk],
        out_specs=[out_blk, out_blk, out_blk, out_blk,
                   pl.BlockSpec((1, wide), lambda h, c: (0, h)), pl.BlockSpec((1, HEAD), lambda h, c: (0, 0))],
        out_shape=[jax.ShapeDtypeStruct((s, width), BF16)] * 4
        + [jax.ShapeDtypeStruct((1, width), F32), jax.ShapeDtypeStruct((1, HEAD), F32)],
        scratch_shapes=[pltpu.VMEM((hpb, HEAD, HEAD), F32)], comm=comm)


def _shift_rows(x, d, rowi):
    if d == 0:
        return x
    n = x.shape[0]
    rolled = pltpu.roll(x, d % n, 0)
    keep = rowi >= d if d > 0 else rowi < n + d
    return jnp.where(keep, rolled, 0.0)


def _conv_pre(x, w_ref, rowi):
    acc = None
    for j in range(CONV_WIDTH):
        term = w_ref[j:j + 1, :] * _shift_rows(x, CONV_WIDTH - 1 - j, rowi)
        acc = term if acc is None else acc + term
    return acc


def _conv_fwd(proj, w, *, col_off, name, cb=256):
    s = proj.shape[0]
    width = w.shape[1]
    cb = min(cb, width)

    def body(x_ref, w_ref, o_ref):
        rowi = lax.broadcasted_iota(jnp.int32, (s, cb), 0)
        o_ref[...] = _silu(_conv_pre(x_ref[...], w_ref, rowi))

    return pl.pallas_call(
        body, name=name, grid=(width // cb,),
        in_specs=[pl.BlockSpec((s, cb), lambda j: (0, col_off // cb + j)), pl.BlockSpec((CONV_WIDTH, cb), lambda j: (0, j))],
        out_specs=pl.BlockSpec((s, cb), lambda j: (0, j)),
        out_shape=jax.ShapeDtypeStruct((s, width), F32),
        compiler_params=_params(1),
    )(proj, w)


def _conv_bwd(proj, w, d_out, *, col_off, name, cb=256, comm=()):
    s = proj.shape[0]
    width = w.shape[1]
    cb = min(cb, width)

    def body(x_ref, w_ref, do_ref, dx_ref, dw_ref):
        rowi = lax.broadcasted_iota(jnp.int32, (s, cb), 0)
        x = x_ref[...]
        pre = _conv_pre(x, w_ref, rowi)
        sg = jax.nn.sigmoid(pre)
        dpre = do_ref[...] * (sg + pre * sg * (1.0 - sg))
        dx = None
        for j in range(CONV_WIDTH):
            d = CONV_WIDTH - 1 - j
            term = w_ref[j:j + 1, :] * _shift_rows(dpre, -d, rowi)
            dx = term if dx is None else dx + term
            dw_ref[j:j + 1, :] = jnp.sum(dpre * _shift_rows(x, d, rowi), axis=0, keepdims=True)
        dx_ref[...] = dx.astype(dx_ref.dtype)

    return _call(
        body, (proj, w, d_out), name=name, grid=(width // cb,),
        in_specs=[pl.BlockSpec((s, cb), lambda j: (0, col_off // cb + j)), pl.BlockSpec((CONV_WIDTH, cb), lambda j: (0, j)),
                  pl.BlockSpec((s, cb), lambda j: (0, j))],
        out_specs=[pl.BlockSpec((s, cb), lambda j: (0, j)), pl.BlockSpec((CONV_WIDTH, cb), lambda j: (0, j))],
        out_shape=[jax.ShapeDtypeStruct((s, width), BF16), jax.ShapeDtypeStruct((CONV_WIDTH, width), F32)],
        comm=comm)


_lane_concat = _per_head(lambda a, b: jnp.concatenate([a, b], axis=1))
_lane_half = _per_head(lambda a, j: a[:, j * HEAD:(j + 1) * HEAD])


def _tri_inverse(lower):
    c = CHUNK
    row = lax.broadcasted_iota(jnp.int32, (c, c), 0)
    col = lax.broadcasted_iota(jnp.int32, (c, c), 1)
    inv = (row == col).astype(F32)
    lvl = 0
    while (1 << lvl) < c:
        same_pair = (row >> (lvl + 1)) == (col >> (lvl + 1))
        off_block = same_pair & (((row >> lvl) & 1) == 1) & (((col >> lvl) & 1) == 0)
        inv = inv - _hdot_h(_hdot_h(inv, _where(off_block, lower, 0.0), NN), inv, NN)
        lvl += 1
    return inv


@jax.custom_vjp
def _tri_solve(lowers, rhss):
    return _tri_solve_fwd(lowers, rhss)[0]


def _tri_solve_fwd(lowers, rhss):
    inv = _tri_inverse(_Heads(lowers))
    sol = _hdot_h(inv, _Heads(rhss), NN)
    return sol.vals, (inv.vals, sol.vals)


def _tri_solve_bwd(res, g):
    inv, sol = _Heads(res[0]), _Heads(res[1])
    d_rhs = _hdot_h(inv, _Heads(g), TN)
    return (-_hdot_h(d_rhs, sol, NT)).vals, d_rhs.vals


_tri_solve.defvjp(_tri_solve_fwd, _tri_solve_bwd)


def _solve(lower, rhs):
    if isinstance(lower, _Heads):
        return _Heads(_tri_solve(lower.vals, rhs.vals))
    return _tri_solve([lower], [rhs])[0]


def _delta_chunk(h, heads, qr, kr, vr, ab, zp, alog, dtb, gn, st):
    c = CHUNK
    row = lax.broadcasted_iota(jnp.int32, (c, c), 0)
    col = lax.broadcasted_iota(jnp.int32, (c, c), 1)
    causal = row >= col
    strict = row > col
    lane = lax.broadcasted_iota(jnp.int32, (c, HEAD), 1)
    mine = _equal(h, lane)
    la_full = -jnp.exp(alog) * _softplus(ab + dtb)
    cum_full = _hdot(causal.astype(F32), la_full)
    cum = _sum(_where(mine, cum_full, 0.0), axis=1, keepdims=True)
    cend = _sum(_sum(_where(mine, la_full, 0.0), axis=1, keepdims=True), axis=0, keepdims=True)
    beta = _sum(_where(_equal(heads + h, lane), jax.nn.sigmoid(ab), 0.0), axis=1, keepdims=True)
    cum_row = _hdot_h(_where(mine, 1.0, 0.0), cum_full, NT)
    decay = _where(causal, _exp(_where(causal, cum - cum_row, 0.0)), 0.0)
    qn = qr * _rsqrt(_sum(qr * qr, axis=-1, keepdims=True) + NORM_EPS) * (HEAD ** -0.5)
    kn = kr * _rsqrt(_sum(kr * kr, axis=-1, keepdims=True) + NORM_EPS)
    kb = kn * beta
    lower = _where(strict, _bdot_h(kb, kn, NT) * decay, 0.0)
    ecum = _exp(cum)
    sol = _solve(lower, _lane_concat(vr * beta, kb * ecum))
    u, w = _lane_half(sol, 0), _lane_half(sol, 1)
    intra = _bdot_h(qn, kn, NT) * decay
    v_new = u - _bdot_h(w, st, NN)
    out = _bdot_h(qn * ecum, st, NN) + _bdot_h(intra, v_new, NN)
    st_new = st * _exp(cend) + _bdot_h(kn * _exp(cend - cum), v_new, TN)
    res = _rms(out, gn) * _silu(zp)
    return res, st_new


def _delta_heads(hs, heads, qs, ks, vs, ab, zs, alog, dtb, gn, sts):
    res, st_new = _delta_chunk(_Heads(hs), heads, _Heads(qs), _Heads(ks), _Heads(vs), ab, _Heads(zs), alog, dtb, gn,
                               _Heads(sts))
    return res.vals, st_new.vals


def _delta_fwd(qkv, ab, proj, hp, gn, *, heads, z_off, name, comm=()):
    s = qkv.shape[0]
    n = s // CHUNK

    hpb = min(HEADS_PER_STEP, heads)
    assert heads % hpb == 0 and z_off % hpb == 0

    def body(q_ref, k_ref, v_ref, ab_ref, z_ref, hp_ref, gn_ref, o_ref, st_ref, state):
        hb = pl.program_id(1)

        @pl.when(pl.program_id(0) == 0)
        def _():
            for j in range(hpb):
                state[hb * hpb + j] = jnp.zeros((HEAD, HEAD), F32)

        shared = (ab_ref[...], hp_ref[0:1, :], hp_ref[1:2, :], gn_ref[...])
        loaded = [(q_ref[:, _lanes(j)], k_ref[:, _lanes(j)], v_ref[:, _lanes(j)], z_ref[:, _lanes(j)],
                   state[hb * hpb + j]) for j in range(hpb)]
        qs, ks, vs, zs, sts = (list(t) for t in zip(*loaded))
        res, st_new = _delta_heads([hb * hpb + j for j in range(hpb)], heads, qs, ks, vs, shared[0], zs, shared[1],
                                   shared[2], shared[3], sts)
        for j in range(hpb):
            st_ref[j] = sts[j]
            o_ref[:, _lanes(j)] = res[j].astype(o_ref.dtype)
            state[hb * hpb + j] = st_new[j]

    wide = hpb * HEAD
    blk = lambda off: pl.BlockSpec((CHUNK, wide), lambda c, h: (c, off // hpb + h))
    return _call(
        body, (qkv, qkv, qkv, ab, proj, hp, gn), name=name, grid=(n, heads // hpb),
        in_specs=[blk(0), blk(heads), blk(2 * heads), pl.BlockSpec((CHUNK, HEAD), lambda c, h: (c, 0)), blk(z_off),
                  pl.BlockSpec((8, HEAD), lambda c, h: (0, 0)), pl.BlockSpec((1, HEAD), lambda c, h: (0, 0))],
        out_specs=[pl.BlockSpec((CHUNK, wide), lambda c, h: (c, h)),
                   pl.BlockSpec((hpb, None, HEAD, HEAD), lambda c, h: (h, c, 0, 0))],
        out_shape=[jax.ShapeDtypeStruct((s, heads * HEAD), BF16), jax.ShapeDtypeStruct((heads, n, HEAD, HEAD), F32)],
        scratch_shapes=[pltpu.VMEM((heads, HEAD, HEAD), F32)], comm=comm)


def _delta_bwd(qkv, ab, proj, hp, gn, states, d_out, *, heads, z_off, name, do_off=0, comm=()):
    s = qkv.shape[0]
    n = s // CHUNK
    hpb = min(HEADS_PER_STEP, heads)

    def body(q_ref, k_ref, v_ref, ab_ref, z_ref, hp_ref, gn_ref, st_ref, do_ref,
             dq_ref, dk_ref, dv_ref, dab_ref, dz_ref, dhp_ref, dgn_ref, dstate):
        c, hb = pl.program_id(0), pl.program_id(1)

        @pl.when(c == 0)
        def _():
            for j in range(hpb):
                dstate[hb * hpb + j] = jnp.zeros((HEAD, HEAD), F32)

        @pl.when((c == 0) & (hb == 0))
        def _():
            dgn_ref[...] = jnp.zeros_like(dgn_ref)
            dhp_ref[...] = jnp.zeros_like(dhp_ref)

        @pl.when(hb == 0)
        def _():
            dab_ref[...] = jnp.zeros_like(dab_ref)

        shared = (ab_ref[...], hp_ref[0:1, :], hp_ref[1:2, :], gn_ref[...])
        loaded = [(q_ref[:, _lanes(j)], k_ref[:, _lanes(j)], v_ref[:, _lanes(j)], z_ref[:, _lanes(j)], st_ref[j],
                   do_ref[:, _lanes(j)].astype(F32), dstate[hb * hpb + j]) for j in range(hpb)]
        qs, ks, vs, zs, sts, dos, dss = (list(t) for t in zip(*loaded))
        fn = functools.partial(_delta_heads, [hb * hpb + j for j in range(hpb)], heads)
        _, vjp = jax.vjp(fn, qs, ks, vs, shared[0], zs, shared[1], shared[2], shared[3], sts)
        dqs, dks, dvs, dab, dzs, dal, ddt, dgn, dsts = vjp((dos, dss))
        for j in range(hpb):
            ln = _lanes(j)
            dq_ref[:, ln] = dqs[j]
            dk_ref[:, ln] = dks[j]
            dv_ref[:, ln] = dvs[j]
            dz_ref[:, ln] = dzs[j].astype(dz_ref.dtype)
            dstate[hb * hpb + j] = dsts[j]
        dab_ref[...] += dab
        dhp_ref[0:1, :] += dal
        dhp_ref[1:2, :] += ddt
        dgn_ref[...] += dgn

    wide = hpb * HEAD
    rev = lambda off: pl.BlockSpec((CHUNK, wide), lambda c, h: (n - 1 - c, off // hpb + h))
    width = heads * HEAD
    head_blk = pl.BlockSpec((CHUNK, wide), lambda c, h: (n - 1 - c, h))
    ab_blk = pl.BlockSpec((CHUNK, HEAD), lambda c, h: (n - 1 - c, 0))
    return _call(
        body, (qkv, qkv, qkv, ab, proj, hp, gn, states, d_out), name=name, grid=(n, heads // hpb),
        in_specs=[rev(0), rev(heads), rev(2 * heads), ab_blk, rev(z_off),
                  pl.BlockSpec((8, HEAD), lambda c, h: (0, 0)), pl.BlockSpec((1, HEAD), lambda c, h: (0, 0)),
                  pl.BlockSpec((hpb, None, HEAD, HEAD), lambda c, h: (h, n - 1 - c, 0, 0)), rev(do_off)],
        out_specs=[head_blk, head_blk, head_blk, ab_blk, head_blk,
                   pl.BlockSpec((8, HEAD), lambda c, h: (0, 0)), pl.BlockSpec((1, HEAD), lambda c, h: (0, 0))],
        out_shape=[jax.ShapeDtypeStruct((s, width), F32)] * 3
        + [jax.ShapeDtypeStruct((s, HEAD), F32), jax.ShapeDtypeStruct((s, width), BF16),
           jax.ShapeDtypeStruct((8, HEAD), F32), jax.ShapeDtypeStruct((1, HEAD), F32)],
        scratch_shapes=[pltpu.VMEM((heads, HEAD, HEAD), F32)], comm=comm)


def _s5_scan(buf, lt_ref, cin_r, cin_i, tt, reverse):
    nblk = tt // 8
    hl = S5_HALF
    base = 8 if reverse else 0

    def body(j, carry):
        cr, ci = carry
        off = pl.multiple_of((nblk - 1 - j if reverse else j) * 8, 8)
        xr = buf[pl.ds(off, 8), 0:hl]
        xi = buf[pl.ds(off, 8), hl:2 * hl]
        for lv, d in enumerate((1, 2, 4)):
            ar, ai = lt_ref[base + 2 * lv], lt_ref[base + 2 * lv + 1]
            sr = pltpu.roll(xr, 8 - d if reverse else d, 0)
            si = pltpu.roll(xi, 8 - d if reverse else d, 0)
            xr, xi = xr + ar * sr - ai * si, xi + ar * si + ai * sr
        pr, pi = lt_ref[base + 6], lt_ref[base + 7]
        xr, xi = xr + pr * cr - pi * ci, xi + pr * ci + pi * cr
        buf[pl.ds(off, 8), 0:hl] = xr
        buf[pl.ds(off, 8), hl:2 * hl] = xi
        edge = 0 if reverse else 7
        return xr[edge:edge + 1, :], xi[edge:edge + 1, :]

    return lax.fori_loop(0, nblk, body, (cin_r, cin_i))


def _s5_fwd(u, wb, wc, lt, dskip, *, name, tt=1024, comm=()):
    s, d = u.shape
    nb = d // HEAD
    tt = min(tt, s)
    nt = s // tt
    hl = S5_HALF

    def body(u_ref, wb_ref, wc_ref, lt_ref, d_ref, y_ref, cin_ref, st_ref, act_ref, buf, carry):
        @pl.when(pl.program_id(1) == 0)
        def _():
            carry[...] = jnp.zeros_like(carry)

        cin_ref[...] = carry[0:1, :]
        uv = u_ref[...]
        buf[...] = _bdot_raw(uv, wb_ref[...])
        cr, ci = _s5_scan(buf, lt_ref, carry[0:1, 0:hl], carry[0:1, hl:2 * hl], tt, False)
        carry[0:1, 0:hl] = cr
        carry[0:1, hl:2 * hl] = ci
        states = buf[...].astype(BF16)
        st_ref[...] = states
        yv = _bdot_raw(states, wc_ref[...]) + d_ref[...] * uv
        y_ref[...] = yv
        act_ref[...] = _gelu(yv).astype(act_ref.dtype)

    return _call(
        body, (u, wb, wc, lt, dskip), name=name, grid=(nb, nt),
        in_specs=[pl.BlockSpec((tt, HEAD), lambda b, t: (t, b)),
                  pl.BlockSpec((None, HEAD, 2 * hl), lambda b, t: (b, 0, 0)),
                  pl.BlockSpec((None, 2 * hl, HEAD), lambda b, t: (b, 0, 0)),
                  pl.BlockSpec((None, 16, 8, hl), lambda b, t: (b, 0, 0, 0)),
                  pl.BlockSpec((1, HEAD), lambda b, t: (0, b))],
        out_specs=[pl.BlockSpec((tt, HEAD), lambda b, t: (t, b)),
                   pl.BlockSpec((None, None, 1, 2 * hl), lambda b, t: (b, t, 0, 0)),
                   pl.BlockSpec((tt, 2 * hl), lambda b, t: (t, b)),
                   pl.BlockSpec((tt, HEAD), lambda b, t: (t, b))],
        out_shape=[jax.ShapeDtypeStruct((s, d), F32), jax.ShapeDtypeStruct((nb, nt, 1, 2 * hl), F32),
                   jax.ShapeDtypeStruct((s, nb * 2 * hl), BF16), jax.ShapeDtypeStruct((s, d), BF16)],
        scratch_shapes=[pltpu.VMEM((tt, 2 * hl), F32), pltpu.VMEM((8, 2 * hl), F32)], comm=comm)


def _s5_bwd(u, dy, wb, wc, lt, dskip, cins, states, *, name, tt=1024, comm=()):
    s, d = u.shape
    nb = d // HEAD
    tt = min(tt, s)
    nt = s // tt
    hl = S5_HALF

    def body(u_ref, dy_ref, wb_ref, wc_ref, lt_ref, d_ref, cin_ref, st_ref,
             du_ref, dwb_ref, dwc_ref, dd_ref, dlam_ref, abuf, acarry):
        @pl.when(pl.program_id(1) == 0)
        def _():
            acarry[...] = jnp.zeros_like(acarry)
            dwb_ref[...] = jnp.zeros_like(dwb_ref)
            dwc_ref[...] = jnp.zeros_like(dwc_ref)
            dd_ref[...] = jnp.zeros_like(dd_ref)
            dlam_ref[...] = jnp.zeros_like(dlam_ref)

        uv, dyv = u_ref[...], dy_ref[...]
        dy16 = dyv.astype(BF16)
        abuf[...] = _bdot_raw(dy16, wc_ref[...], NT)
        ar, ai = _s5_scan(abuf, lt_ref, acarry[0:1, 0:hl], acarry[0:1, hl:2 * hl], tt, True)
        acarry[0:1, 0:hl] = ar
        acarry[0:1, hl:2 * hl] = ai
        adj16 = abuf[...].astype(BF16)
        du_ref[...] = _bdot_raw(adj16, wb_ref[...], NT) + d_ref[...] * dyv
        dwb_ref[...] += _bdot_raw(uv, adj16, TN)
        dwc_ref[...] += _bdot_raw(st_ref[...], dy16, TN)
        dd_ref[...] += jnp.sum(dyv * uv, axis=0, keepdims=True)
        first = lax.broadcasted_iota(jnp.int32, (tt, hl), 0) == 0
        spr = jnp.where(first, cin_ref[:, 0:hl], pltpu.roll(st_ref[:, 0:hl].astype(F32), 1, 0))
        spi = jnp.where(first, cin_ref[:, hl:2 * hl], pltpu.roll(st_ref[:, hl:2 * hl].astype(F32), 1, 0))
        avr, avi = abuf[:, 0:hl], abuf[:, hl:2 * hl]
        dlam_ref[:, 0:hl] += jnp.sum(avr * spr + avi * spi, axis=0, keepdims=True)
        dlam_ref[:, hl:2 * hl] += jnp.sum(avi * spr - avr * spi, axis=0, keepdims=True)

    rev = pl.BlockSpec((tt, HEAD), lambda b, t: (nt - 1 - t, b))
    return _call(
        body, (u, dy, wb, wc, lt, dskip, cins, states), name=name, grid=(nb, nt),
        in_specs=[rev, rev,
                  pl.BlockSpec((None, HEAD, 2 * hl), lambda b, t: (b, 0, 0)),
                  pl.BlockSpec((None, 2 * hl, HEAD), lambda b, t: (b, 0, 0)),
                  pl.BlockSpec((None, 16, 8, hl), lambda b, t: (b, 0, 0, 0)),
                  pl.BlockSpec((1, HEAD), lambda b, t: (0, b)),
                  pl.BlockSpec((None, None, 1, 2 * hl), lambda b, t: (b, nt - 1 - t, 0, 0)),
                  pl.BlockSpec((tt, 2 * hl), lambda b, t: (nt - 1 - t, b))],
        out_specs=[rev,
                   pl.BlockSpec((None, HEAD, 2 * hl), lambda b, t: (b, 0, 0)),
                   pl.BlockSpec((None, 2 * hl, HEAD), lambda b, t: (b, 0, 0)),
                   pl.BlockSpec((1, HEAD), lambda b, t: (0, b)),
                   pl.BlockSpec((None, 1, 2 * hl), lambda b, t: (b, 0, 0))],
        out_shape=[jax.ShapeDtypeStruct((s, d), F32), jax.ShapeDtypeStruct(wb.shape, F32),
                   jax.ShapeDtypeStruct(wc.shape, F32), jax.ShapeDtypeStruct((1, d), F32),
                   jax.ShapeDtypeStruct((nb, 1, 2 * hl), F32)],
        scratch_shapes=[pltpu.VMEM((tt, 2 * hl), F32), pltpu.VMEM((8, 2 * hl), F32)],
        comm=comm)


def _s5_pack(lr, li, br, bi, c_re, c_im):
    g = lr.shape[0]
    nb = g // S5_GB
    eye = jnp.eye(S5_GB, dtype=F32)
    bm = jnp.stack([br, bi]).reshape(2, nb, S5_GB, S5_STATE, S5_GROUP)
    wb = jnp.einsum("rbgpc,gh->bgcrhp", bm, eye).reshape(nb, HEAD, 2 * S5_HALF)
    cm = jnp.stack([c_re, -c_im]).reshape(2, nb, S5_GB, S5_GROUP, S5_STATE)
    wc = jnp.einsum("rbgcp,gh->brgphc", cm, eye).reshape(nb, 2 * S5_HALF, HEAD)
    pw = [(lr, li)]
    for _ in range(7):
        pr, pi = pw[-1]
        pw.append((pr * lr - pi * li, pr * li + pi * lr))
    blk = lambda a: a.reshape(nb, 1, S5_HALF)
    rows = jnp.arange(8).reshape(1, 8, 1)
    tables = []
    for conj, keep, order in ((1.0, lambda n: rows >= n, range(8)), (-1.0, lambda n: rows < 8 - n, range(7, -1, -1))):
        for n in (1, 2, 4):
            tables += [jnp.where(keep(n), blk(pw[n - 1][0]), 0.0), jnp.where(keep(n), conj * blk(pw[n - 1][1]), 0.0)]
        tables += [jnp.concatenate([blk(pw[n][0]) for n in order], axis=1),
                   jnp.concatenate([conj * blk(pw[n][1]) for n in order], axis=1)]
    return wb, wc, jnp.stack(tables, axis=1)


def _s5_unpack(dwb, dwc, dlam):
    nb = dwb.shape[0]
    g = nb * S5_GB
    eye = jnp.eye(S5_GB, dtype=F32)
    db = jnp.einsum("bgcrhp,gh->rbgpc", dwb.reshape(nb, S5_GB, S5_GROUP, 2, S5_GB, S5_STATE), eye)
    db = db.reshape(2, g, S5_STATE * S5_GROUP)
    dc = jnp.einsum("brgphc,gh->rbgcp", dwc.reshape(nb, 2, S5_GB, S5_STATE, S5_GB, S5_GROUP), eye)
    dc = dc.reshape(2, g, S5_GROUP, S5_STATE)
    dl = dlam.reshape(nb, 2, S5_GB, S5_STATE).transpose(1, 0, 2, 3).reshape(2, g, S5_STATE)
    return dl[0], dl[1], db[0], db[1], dc[0], -dc[1]


def _peer(r):
    mx, my, mc = lax.axis_index("x"), lax.axis_index("y"), lax.axis_index("c")
    px = 1 - mx if r & 4 else mx
    py = 1 - my if r & 2 else my
    pc = 1 - mc if r & 1 else mc
    return (px, py, pc), 4 * px + 2 * py + pc


_COMM_SCRATCH = [pltpu.SemaphoreType.DMA((N_DEV - 1,)), pltpu.SemaphoreType.DMA((N_DEV - 1,)), pltpu.SemaphoreType.DMA]


class _AllToAll:
    def __init__(self, x, rows=None):
        self.x = x
        self.rows = rows
        shape = x.shape if rows is None else (x.shape[0], rows[1]) + tuple(x.shape[2:])
        self.out_shape = jax.ShapeDtypeStruct(shape, x.dtype)

    def _copies(self, x_ref, out_ref, send_sems, recv_sems, local_sem):
        def block(j):
            return x_ref.at[j] if self.rows is None else x_ref.at[j, pl.ds(self.rows[0], self.rows[1])]

        _, me = _peer(0)
        mine = pltpu.make_async_copy(block(me), out_ref.at[me], local_sem)
        sends, recvs = [], []
        for r in range(1, N_DEV):
            pos, idx = _peer(r)
            sems = dict(send_sem=send_sems.at[r - 1], recv_sem=recv_sems.at[r - 1], device_id=pos, device_id_type=MESH)
            sends.append(pltpu.make_async_remote_copy(src_ref=block(idx), dst_ref=out_ref.at[me], **sems))
            recvs.append(pltpu.make_async_remote_copy(src_ref=block(idx), dst_ref=out_ref.at[idx], **sems))
        return mine, sends, recvs

    def start(self, *refs):
        mine, sends, _ = self._copies(*refs)
        mine.start()
        for cp in sends:
            cp.start()

    def finish(self, *refs):
        mine, sends, recvs = self._copies(*refs)
        for cp in recvs:
            cp.wait_recv()
        for cp in sends:
            cp.wait_send()
        mine.wait()


class _Gather:
    def __init__(self, x):
        self.x = x
        self.out_shape = jax.ShapeDtypeStruct((N_DEV,) + tuple(x.shape), x.dtype)

    def _copies(self, x_ref, out_ref, send_sems, recv_sems, local_sem):
        mx, my, mc = lax.axis_index("x"), lax.axis_index("y"), lax.axis_index("c")
        me, sibling = (mx, my, mc), (mx, my, 1 - mc)
        chips = [(1 - mx, my), (mx, 1 - my), (1 - mx, 1 - my)]

        def slot(px, py, pc):
            return out_ref.at[4 * px + 2 * py + pc]

        def copy(k, block, to, src=None):
            return pltpu.make_async_remote_copy(
                src_ref=slot(*block) if src is None else src, dst_ref=slot(*block),
                send_sem=send_sems.at[k], recv_sem=recv_sems.at[k], device_id=to, device_id_type=MESH)

        return dict(
            mine=pltpu.make_async_copy(x_ref, slot(*me), local_sem),
            first=[copy(0, me, sibling, src=x_ref)] + [copy(1 + j, me, (*chip, mc), src=x_ref) for j, chip in enumerate(chips)],
            passed=[copy(4 + j, (*chip, mc), sibling) for j, chip in enumerate(chips)],
            over_ici=[copy(1 + j, (*chip, mc), me) for j, chip in enumerate(chips)],
            from_sibling=[copy(0, sibling, me)] + [copy(4 + j, (*chip, 1 - mc), me) for j, chip in enumerate(chips)])

    def start(self, *refs):
        cps = self._copies(*refs)
        cps["mine"].start()
        for cp in cps["first"]:
            cp.start()

    def finish(self, *refs):
        cps = self._copies(*refs)
        for arrived, onward in zip(cps["over_ici"], cps["passed"]):
            arrived.wait_recv()
            onward.start()
        for cp in cps["from_sibling"]:
            cp.wait_recv()
        for cp in cps["first"] + cps["passed"]:
            cp.wait_send()
        cps["mine"].wait()


def _call(body, args, *, name, grid, in_specs, out_specs, out_shape, scratch_shapes=(), comm=()):
    n_in, n_out, n_scr, nc = len(in_specs), len(out_shape), len(scratch_shapes), len(comm)

    def wrapped(*refs):
        ins, c_in = refs[:n_in], refs[n_in:n_in + nc]
        outs, c_out = refs[n_in + nc:n_in + nc + n_out], refs[n_in + nc + n_out:n_in + 2 * nc + n_out]
        scr = refs[n_in + 2 * nc + n_out:n_in + 2 * nc + n_out + n_scr]
        sems = refs[n_in + 2 * nc + n_out + n_scr:]
        ids = [pl.program_id(a) for a in range(len(grid))]
        if nc:
            @pl.when(functools.reduce(operator.and_, [i == 0 for i in ids]))
            def _():
                for k, op in enumerate(comm):
                    op.start(c_in[k], c_out[k], *sems[3 * k:3 * k + 3])

        body(*ins, *outs, *scr)
        if nc:
            @pl.when(functools.reduce(operator.and_, [i == g - 1 for i, g in zip(ids, grid)]))
            def _():
                for k, op in enumerate(comm):
                    op.finish(c_in[k], c_out[k], *sems[3 * k:3 * k + 3])

    any_spec = pl.BlockSpec(memory_space=pl.ANY)
    res = pl.pallas_call(
        wrapped, name=name, grid=grid,
        in_specs=list(in_specs) + [any_spec] * nc, out_specs=list(out_specs) + [any_spec] * nc,
        out_shape=list(out_shape) + [op.out_shape for op in comm],
        scratch_shapes=list(scratch_shapes) + list(_COMM_SCRATCH) * nc,
        compiler_params=_params(len(grid)),
    )(*args, *[op.x for op in comm])
    return list(res[:n_out]), list(res[n_out:])


def _comm_call(op, *, name):
    return _call(lambda: None, (), name=name, grid=(1,), in_specs=[], out_specs=[], out_shape=[], comm=(op,))[1][0]


def _adamw(w, parts, m, v, *, name, tr=128, comm=()):
    nl, r, c = w.shape
    assert len(parts) == nl
    parts = [list(p) if isinstance(p, (list, tuple)) else [p] for p in parts]
    npart = parts[0][0].shape[0]
    tr = min(tr, r, *[pc.shape[1] for p in parts for pc in p])
    assert r % tr == 0 and all(pc.shape[1] % tr == 0 for p in parts for pc in p), (name, r, tr)
    pieces = []
    for l, p in enumerate(parts):
        first = 0
        for pc in p:
            pieces.append((l, first, pc.shape[1] // tr, pc))
            first += pc.shape[1] // tr
        assert first == r // tr, (name, l)

    def body(w_ref, m_ref, v_ref, *rest):
        p_refs, (g_ref, d_ref, mo_ref, vo_ref) = rest[:len(pieces)], rest[len(pieces):]
        layer, tile = pl.program_id(0), pl.program_id(1)
        for p_ref, (l, first, count, _) in zip(p_refs, pieces):
            @pl.when((layer == l) & (tile >= first) & (tile < first + count))
            def _():
                g = p_ref[0].astype(F32)
                for k in range(1, npart):
                    g = g + p_ref[k].astype(F32)
                m2 = ADAM_B1 * m_ref[...] + (1.0 - ADAM_B1) * g
                v2 = ADAM_B2 * v_ref[...] + (1.0 - ADAM_B2) * (g * g)
                m_hat = m2 / (1.0 - ADAM_B1 ** ADAM_STEP)
                v_hat = v2 / (1.0 - ADAM_B2 ** ADAM_STEP)
                g_ref[...] = g
                d_ref[...] = -ADAM_LR * (m_hat / (jnp.sqrt(v_hat) + ADAM_EPS) + ADAM_WD * w_ref[...])
                mo_ref[...] = m2
                vo_ref[...] = v2

    blk = pl.BlockSpec((None, tr, c), lambda l, i: (l, i, 0))

    def part_spec(l, first, count):
        return pl.BlockSpec((npart, tr, c), lambda ll, i: (0, jnp.where(ll == l, jnp.clip(i - first, 0, count - 1), 0), 0))

    outs, exchanged = _call(
        body, (w, m, v, *[pc for _, _, _, pc in pieces]), name=name, grid=(nl, r // tr),
        in_specs=[blk, blk, blk] + [part_spec(l, first, count) for l, first, count, _ in pieces],
        out_specs=[blk] * 4, out_shape=[jax.ShapeDtypeStruct((nl, r, c), F32)] * 4, comm=comm)
    return (outs, exchanged) if comm else outs


def _sum_parts(parts, *, name):
    npart = parts.shape[0]

    def body(p_ref, o_ref):
        g = p_ref[0]
        for k in range(1, npart):
            g = g + p_ref[k]
        o_ref[...] = g

    return pl.pallas_call(
        body, name=name, grid=(1,), in_specs=[_full_spec(parts)],
        out_specs=pl.BlockSpec(parts.shape[1:], lambda i: (0, 0)),
        out_shape=jax.ShapeDtypeStruct(parts.shape[1:], F32), compiler_params=_params(1),
    )(parts)


def _pack(arrs):
    blocks = []
    for a in arrs:
        flat = a.reshape(-1).astype(F32)
        blocks.append(jnp.pad(flat, (0, (-flat.shape[0]) % (8 * HEAD))).reshape(-1, HEAD))
    out = jnp.concatenate(blocks, axis=0)
    return jnp.pad(out, ((0, (-out.shape[0]) % HEAD), (0, 0)))


def _unpack(packed, shapes):
    out, off = [], 0
    for shp in shapes:
        size = math.prod(shp)
        rows = -(-size // (8 * HEAD)) * 8
        out.append(packed[off:off + rows].reshape(-1)[:size].reshape(shp))
        off += rows
    return out


def _add_epilogue(acc, res):
    return (acc + res,)


def _relu2_epilogue(acc):
    r = jnp.maximum(acc, 0.0)
    return acc, r * r


def _ple_epilogue(acc, gpre, h):
    return h + jax.nn.sigmoid(gpre) * acc, acc


def kernel(x, p, norm_mix, norm_mlp, norm_ple, w_in_e, w_out_e, hgrn_lb, g_norm_a, conv_w, a_log, dt_bias, g_norm_b, s5_a_re, s5_a_im, s5_b_re, s5_b_im, s5_c_re, s5_c_im, s5_d, s5_log_dt, w_glu, b_glu, w_out_o, w_up, w_down, w_ple_gate, w_ple_proj, final_norm, loss_target, m_norm_mix, m_norm_mlp, m_norm_ple, m_w_in_e, m_w_out_e, m_hgrn_lb, m_g_norm_a, m_conv_w, m_a_log, m_dt_bias, m_g_norm_b, m_s5_a_re, m_s5_a_im, m_s5_b_re, m_s5_b_im, m_s5_c_re, m_s5_c_im, m_s5_d, m_s5_log_dt, m_w_glu, m_b_glu, m_w_out_o, m_w_up, m_w_down, m_w_ple_gate, m_w_ple_proj, m_final_norm, v_norm_mix, v_norm_mlp, v_norm_ple, v_w_in_e, v_w_out_e, v_hgrn_lb, v_g_norm_a, v_conv_w, v_a_log, v_dt_bias, v_g_norm_b, v_s5_a_re, v_s5_a_im, v_s5_b_re, v_s5_b_im, v_s5_c_re, v_s5_c_im, v_s5_d, v_s5_log_dt, v_w_glu, v_b_glu, v_w_out_o, v_w_up, v_w_down, v_w_ple_gate, v_w_ple_proj, v_final_norm):
    args = dict(locals())
    s, d = x.shape[1], x.shape[2]
    aw = d // 2
    ha = hb = aw // HEAD
    main = 4 * d
    z_col = 2 * d + 3 * aw
    ff = w_up.shape[2] * N_DEV
    ple = p.shape[-1]
    groups = d // S5_GROUP
    me = 4 * lax.axis_index("x") + 2 * lax.axis_index("y") + lax.axis_index("c")
    x2, target = x[0], loss_target[0]
    row = lambda a, i: a[i:i + 1]

    def gather_of(w):
        return _Gather(w.astype(BF16))

    w_in = jnp.transpose(_comm_call(gather_of(w_in_e[0]), name="ag_w_in"), (1, 0, 2)).reshape(d, -1)
    w_main = w_in[:, :main]
    w_tail = jnp.pad(w_in[:, main:], ((0, 0), (0, HEAD - 2 * hb)))

    lb_rows = [row(hgrn_lb, 0), row(hgrn_lb, 1), row(hgrn_lb, 2)]
    (lb0,) = _small_call(_lb0_stage, lb_rows, name="f_lb0")
    hp = jnp.zeros((8, HEAD), F32).at[0, :hb].set(a_log[0]).at[1, :hb].set(dt_bias[0])
    expand = jnp.asarray(np.kron(np.eye(S5_STATE, dtype=np.float32), np.ones((1, S5_GROUP), np.float32)))
    prep_in = [s5_a_re[0], s5_a_im[0], s5_log_dt[0].reshape(groups, 1),
               s5_b_re[0].reshape(groups, -1), s5_b_im[0].reshape(groups, -1), expand]
    lr, li, br, bi = _small_call(_s5_prep_stage, prep_in, name="f_s5_prep")
    wb, wc, lt = _s5_pack(lr, li, br, bi, s5_c_re[0], s5_c_im[0])
    wb, wc = wb.astype(BF16), wc.astype(BF16)
    fnorm = final_norm.reshape(1, d)

    w_upg = []

    def block_fwd(h, l):
        hn = _rows_call(_rms_stage, [h], [row(norm_mlp, l)], [BF16], name=f"f_norm_mlp{l}")
        (up, act), (dn8,) = _mm(hn, w_upg[l], epilogue=_relu2_epilogue, out_dtypes=(F32, BF16), name=f"f_up{l}",
                                comm=(gather_of(w_down[l]),))
        w_dn = dn8.reshape(ff, d)
        h2 = _mm(act, w_dn, extras=(h,), epilogue=_add_epilogue, name=f"f_down{l}")
        hq = _rows_call(_rms_stage, [h2], [row(norm_ple, l)], [BF16], name=f"f_norm_ple{l}")
        gpre = _mm(hq, w_pgg[l], name=f"f_ple_gate{l}")
        h3, pp = _mm(p[l, 0], w_ppg[l], extras=(gpre, h2), epilogue=_ple_epilogue, out_dtypes=(F32, F32),
                     name=f"f_ple_proj{l}")
        return h3, dict(h=h, hn=hn, up=up, act=act, h2=h2, hq=hq, gpre=gpre, pp=pp, w_dn=w_dn)

    hn0 = _rows_call(_rms_stage, [x2], [row(norm_mix, 0)], [BF16], name="f_norm_mix0")
    shard_shapes = [conv_w[0].shape, s5_d.shape, b_glu.shape]
    proj, (oe8, pg8, small) = _mm(hn0, w_main, name="f_proj", comm=(
        gather_of(w_out_e[0]), gather_of(w_ple_gate), _Gather(_pack([conv_w[0], s5_d, b_glu]))))
    w_oe = oe8.reshape(d, d)
    w_pgg = jnp.transpose(pg8, (1, 0, 2, 3)).reshape(2, d, d)
    conv_g, s5d_g, bglu_g = zip(*[_unpack(small[j], shard_shapes) for j in range(N_DEV)])
    conv_full = jnp.concatenate(conv_g, axis=1)
    s5d_full = jnp.concatenate(s5d_g, axis=1)
    bglu_full = jnp.concatenate(bglu_g, axis=1)
    ab = _mm(hn0, w_tail, name="f_ab")
    (oa, st_a), (gl8, oo8) = _hgrn_fwd(proj, lb0, g_norm_a, heads=ha, name="f_hgrn",
                                       comm=(gather_of(w_glu[0]), gather_of(w_out_o[0])))
    w_gl, w_oo = gl8.reshape(d, d), oo8.reshape(d, d)
    qkv = _conv_fwd(proj, conv_full, col_off=2 * d, name="f_conv")
    slots_to_cols = lambda g8: jnp.transpose(g8, (1, 0, 2)).reshape(g8.shape[1], -1)
    (ob, st_b), (up8,) = _delta_fwd(qkv, ab, proj, hp, g_norm_b, heads=hb, z_off=z_col // HEAD, name="f_delta",
                                    comm=(gather_of(w_up[0]),))
    w_upg.append(slots_to_cols(up8))
    merged = jnp.concatenate([oa, ob], axis=1)
    h1, (pp8,) = _mm(merged, w_oe, extras=(x2,), epilogue=_add_epilogue, name="f_out_e", comm=(gather_of(w_ple_proj),))
    w_ppg = jnp.transpose(pp8, (1, 2, 0, 3)).reshape(2, ple, d)
    h3, sv0 = block_fwd(h1, 0)

    u = _rows_call(_rms_stage, [h3], [row(norm_mix, 1)], [F32], name="f_norm_mix1")
    (y, cins, s5_states, act_g), (up8,) = _s5_fwd(u, wb, wc, lt, s5d_full, name="f_s5", comm=(gather_of(w_up[1]),))
    w_upg.append(slots_to_cols(up8))
    gl_raw = _mm(act_g, w_gl, name="f_glu")
    glu = _rows_call(_glu_stage, [y, gl_raw], [bglu_full], [BF16], name="f_glu_gate")
    h4 = _mm(glu, w_oo, extras=(h3,), epilogue=_add_epilogue, name="f_out_o")
    h6, sv1 = block_fwd(h4, 1)
    dh, d_fnorm, loss8 = _loss_call(h6, fnorm, target, name="loss")
    loss = lax.psum(loss8[0, 0], ("x", "y", "c"))

    dshard, ffs, cols = d // N_DEV, ff // N_DEV, w_in_e.shape[2]
    rows8 = lambda g: _AllToAll(g.reshape(N_DEV, -1, g.shape[-1]))
    col_slots = lambda g: jnp.transpose(g.reshape(g.shape[0], N_DEV, -1), (1, 0, 2))
    cols8 = lambda g: _AllToAll(col_slots(g))

    def halves(g8):
        r = g8.shape[1] // 2
        return _AllToAll(g8, rows=(0, r)), _AllToAll(g8, rows=(r, r))

    def block_bwd(dh3, l, sv, carried=(), carried_wup=(), carried_up=()):
        (dgpre, dpp), _ = _rows_vjp(_ple_stage, [sv["h2"], sv["gpre"], sv["pp"]], [], [dh3],
                                    row_grads={1: BF16, 2: BF16}, name=f"b_ple{l}")
        g_pp = _mm(p[l, 0], dpp, ta=True, out_dtypes=(BF16,), name=f"b_w_ple_proj{l}")
        g_pg = _mm(sv["hq"], dgpre, ta=True, out_dtypes=(BF16,), name=f"b_w_ple_gate{l}")
        dhq = _mm(dgpre, w_pgg[l], tb=True, name=f"b_ple_gate{l}")
        (dh2, dh2_mm), (g_nple,) = _rows_vjp(_rms_stage, [sv["h2"]], [row(norm_ple, l)], [dhq],
                                             row_grads={0: (F32, MM_DTYPE)}, adds={0: dh3}, name=f"b_norm_ple{l}")
        dup, (r_pg, r_pp) = _mm(dh2_mm, sv["w_dn"], tb=True, extras=(sv["up"],), epilogue=_relu2_grad_epilogue,
                                out_dtypes=(BF16,), name=f"b_down{l}", comm=(rows8(g_pg), cols8(g_pp)))
        g_dn = _mm(sv["act"], dh2_mm, ta=True, out_dtypes=(BF16,), name=f"b_w_down{l}", comm=carried)
        g_dn, r_carried = g_dn if carried else (g_dn, [])
        g_up = _mm(sv["hn"], dup, ta=True, out_dtypes=(BF16,), tn=ffs, out_slots=True, name=f"b_w_up{l}",
                   comm=carried_wup)
        g_up, r_carried_wup = g_up if carried_wup else (g_up, [])
        dhn = _mm(dup, w_upg[l], tb=True, name=f"b_up{l}", comm=carried_up)
        dhn, r_carried_up = dhn if carried_up else (dhn, [])
        (dh0, dh0_mm), (g_nmlp,) = _rows_vjp(_rms_stage, [sv["h"]], [row(norm_mlp, l)], [dhn],
                                             row_grads={0: (F32, MM_DTYPE)}, adds={0: dh2}, name=f"b_norm_mlp{l}")
        return dh0, dict(w_ple_proj=r_pp, w_ple_gate=r_pg, norm_ple=g_nple, w_down=g_dn, w_up=g_up, norm_mlp=g_nmlp,
                         carried=r_carried, carried_wup=r_carried_wup, carried_up=r_carried_up, dh_mm=dh0_mm)

    dh4, gb1 = block_bwd(dh, 1, sv1)
    up1_a, up1_b = halves(gb1["w_up"])
    dglu = _mm(gb1["dh_mm"], w_oo, tb=True, name="b_out_o")
    g_oo = _mm(glu, gb1["dh_mm"], ta=True, out_dtypes=(BF16,), name="b_w_out_o")
    (dy1, dgl), (g_bglu,) = _rows_vjp(_glu_stage, [y, gl_raw], [bglu_full], [dglu], row_grads={0: F32, 1: BF16},
                                      name="b_glu_gate")
    g_gl = _mm(act_g, dgl, ta=True, out_dtypes=(BF16,), name="b_w_glu")
    dact = _mm(dgl, w_gl, tb=True, name="b_glu")
    (dy,), _ = _rows_vjp(_gelu_stage, [y], [], [dact], row_grads={0: F32}, adds={0: dy1}, name="b_gelu")
    (du, dwb, dwc, g_s5d, dlam), (r_dn1, r_up1_a) = _s5_bwd(u, dy, wb, wc, lt, s5d_full, cins, s5_states, name="b_s5",
                                                           comm=(rows8(gb1["w_down"]), up1_a))
    (dh3,), (g_nmix1,) = _rows_vjp(_rms_stage, [h3], [row(norm_mix, 1)], [du], row_grads={0: F32}, adds={0: dh4},
                                   name="b_norm_mix1")
    dlr, dli, dbr, dbi, g_cre, g_cim = _s5_unpack(dwb, dwc, dlam)
    g_are, g_aim, g_ldt, g_bre, g_bim, _ = _small_vjp(_s5_prep_stage, prep_in, [dlr, dli, dbr, dbi], name="b_s5_prep")

    early_grads = dict(
        s5_a_re=g_are[None], s5_a_im=g_aim[None], s5_b_re=g_bre.reshape(s5_b_re.shape),
        s5_b_im=g_bim.reshape(s5_b_im.shape), s5_c_re=g_cre[None], s5_c_im=g_cim[None],
        s5_log_dt=g_ldt.reshape(1, groups), final_norm=d_fnorm.reshape(d), s5_d=g_s5d, b_glu=g_bglu)
    dh1, gb0 = block_bwd(dh3, 0, sv0, (rows8(g_oo), rows8(g_gl)), (up1_b,),
                         (_Gather(_pack(list(early_grads.values()))),))
    r_oo, r_gl = gb0["carried"]
    r_up1 = [r_up1_a, gb0["carried_wup"][0]]
    (early_parts,) = gb0["carried_up"]
    dn0_a, dn0_b = halves(gb0["w_down"].reshape(N_DEV, ffs, d))
    up0_a, up0_b = halves(gb0["w_up"])
    dh1_mm = gb0["dh_mm"]
    doa = dob = _mm(dh1_mm, w_oe, tb=True, name="b_out_e")
    g_oe = _mm(merged, dh1_mm, ta=True, out_dtypes=(BF16,), name="b_w_out_e")
    (dq, df, di, dg, dlb, g_gna), (r_dn0_a,) = _hgrn_bwd(proj, lb0, g_norm_a, st_a, doa, heads=ha, name="b_hgrn",
                                                        comm=(dn0_a,))
    (dqb, dkb, dvb, dab, dz, dhp, g_gnb), (r_dn0_b, r_up0_a) = _delta_bwd(
        qkv, ab, proj, hp, g_norm_b, st_b, dob, heads=hb, z_off=z_col // HEAD, name="b_delta", do_off=ha,
        comm=(dn0_b, up0_a))
    (dqkv, g_conv), (r_oe,) = _conv_bwd(proj, conv_full, jnp.concatenate([dqb, dkb, dvb], axis=1), col_off=2 * d,
                                        name="b_conv", comm=(rows8(g_oe),))
    dproj = jnp.concatenate([dq, df, di, dg, dqkv, dz], axis=1)
    g_main, (r_up0_b,) = _mm(hn0, dproj, ta=True, out_dtypes=(BF16,), name="b_w_proj", comm=(up0_b,))
    r_dn0, r_up0 = [r_dn0_a, r_dn0_b], [r_up0_a, r_up0_b]
    g_tail = _mm(hn0, dab, ta=True, out_dtypes=(BF16,), name="b_w_ab")
    in_a, in_b = halves(col_slots(jnp.concatenate([g_main, g_tail[:, :2 * hb]], axis=1)))
    dhn0, (r_in_a,) = _mm(dproj, w_main, tb=True, name="b_proj", comm=(in_a,))
    dhn0 = _mm(dab, w_tail, tb=True, extras=(dhn0,), epilogue=_add_epilogue, name="b_ab")
    (dx,), (g_nmix0,) = _rows_vjp(_rms_stage, [x2], [row(norm_mix, 0)], [dhn0], row_grads={0: F32}, adds={0: dh1},
                                  name="b_norm_mix0")
    g_lb = jnp.concatenate(_small_vjp(_lb0_stage, lb_rows, [dlb], name="b_lb0"), axis=0)

    late_grads = dict(
        norm_mix=jnp.concatenate([g_nmix0, g_nmix1], axis=0),
        norm_mlp=jnp.concatenate([gb0["norm_mlp"], gb1["norm_mlp"]], axis=0),
        norm_ple=jnp.concatenate([gb0["norm_ple"], gb1["norm_ple"]], axis=0),
        hgrn_lb=g_lb, g_norm_a=g_gna, a_log=dhp[0:1, :hb], dt_bias=dhp[1:2, :hb], g_norm_b=g_gnb, conv_w=g_conv)
    rep_names = ["norm_mix", "norm_mlp", "norm_ple", "hgrn_lb", "g_norm_a", "a_log", "dt_bias", "g_norm_b", "s5_a_re",
                 "s5_a_im", "s5_b_re", "s5_b_im", "s5_c_re", "s5_c_im", "s5_log_dt", "final_norm"]
    late_parts = _comm_call(_Gather(_pack(list(late_grads.values()))), name="ag_small_grads")
    summed = {}
    for tag, grads, parts in (("early", early_grads, early_parts), ("late", late_grads, late_parts)):
        sums = _unpack(_sum_parts(parts, name=f"sum_small_grads_{tag}"), [g.shape for g in grads.values()])
        summed.update(zip(grads, sums))
    cw = conv_w.shape[2]
    dshard = d // N_DEV
    shard_g = dict(conv_w=lax.dynamic_slice(summed["conv_w"], (0, me * cw), (CONV_WIDTH, cw))[None],
                   s5_d=lax.dynamic_slice(summed["s5_d"], (0, me * dshard), (1, dshard)),
                   b_glu=lax.dynamic_slice(summed["b_glu"], (0, me * dshard), (1, dshard)))
    small_names = rep_names + ["conv_w", "s5_d", "b_glu"]
    g_small = [summed[k] if k in rep_names else shard_g[k] for k in small_names]
    shapes = [args[k].shape for k in small_names]
    sm_out = _adamw(_pack([args[k] for k in small_names])[None], [_pack(g_small)[None]],
                    _pack([args["m_" + k] for k in small_names])[None], _pack([args["v_" + k] for k in small_names])[None],
                    name="adamw_small")
    sm_out = [dict(zip(small_names, _unpack(o[0], shapes))) for o in sm_out]

    up_out, (r_in_b,) = _adamw(w_up, [r_up0, r_up1], m_w_up, v_w_up, name="adamw_w_up", comm=(in_b,))
    received = dict(w_in_e=[[r_in_a, r_in_b]], w_out_e=[r_oe], w_glu=[r_gl], w_out_o=[r_oo],
                    w_down=[r_dn0, r_dn1], w_ple_gate=[gb0["w_ple_gate"], gb1["w_ple_gate"]],
                    w_ple_proj=[gb0["w_ple_proj"], gb1["w_ple_proj"]])
    big_out = {k: _adamw(args[k], layers, args["m_" + k], args["v_" + k], name="adamw_" + k)
               for k, layers in received.items()}
    big_out["w_up"] = up_out

    names = ["norm_mix", "norm_mlp", "norm_ple", "w_in_e", "w_out_e", "hgrn_lb", "g_norm_a", "conv_w", "a_log", "dt_bias",
             "g_norm_b", "s5_a_re", "s5_a_im", "s5_b_re", "s5_b_im", "s5_c_re", "s5_c_im", "s5_d", "s5_log_dt", "w_glu",
             "b_glu", "w_out_o", "w_up", "w_down", "w_ple_gate", "w_ple_proj", "final_norm"]
    result = [loss, dx[None]]
    for j in range(4):
        result += [big_out[k][j] if k in big_out else sm_out[j][k] for k in names]
    return tuple(result)
```
